```python
import math
import jax, jax.numpy as jnp
from jax import lax
import numpy as np

D_MODEL = 1024
BATCH = 16
SEQ = 2048
DEPTH = 2

SSD_HEAD_DIM = 64
SSD_HEADS = D_MODEL // SSD_HEAD_DIM
D_SSD = SSD_HEADS * SSD_HEAD_DIM
SSD_GROUPS = 2
SSD_HEADS_PER_GROUP = SSD_HEADS // SSD_GROUPS
D_STATE = 128
SSD_CONV = 4
SSD_CHUNK = 128
CONV_DIM = D_SSD + 2 * SSD_GROUPS * D_STATE
GM_HEAD_DIM = 128
GM_HEADS = D_MODEL // GM_HEAD_DIM
D_GM = GM_HEADS * GM_HEAD_DIM
GM_CHUNK = 128
D_MIX = D_SSD + D_GM
N_IN = D_SSD + CONV_DIM + SSD_HEADS + 2 * D_GM
D_FF = ((8 * D_MODEL // 3 + 255) // 256) * 256
FF_CONV = 3
N_MOD = 6
EPS = 1e-6

kernel_name = "hybrid_ssd_gmlp_convffn_adaln"


def rmsnorm(x, g):
    xf = x.astype(jnp.float32)
    y = xf * lax.rsqrt(jnp.mean(xf * xf, axis=-1, keepdims=True) + EPS)
    return (y * g.astype(jnp.float32)).astype(x.dtype)


def causal_dwconv(x, w, b):
    k, ch = w.shape
    y = lax.conv_general_dilated(x, w[:, None, :].astype(x.dtype), window_strides=(1,), padding=[(k - 1, 0)], dimension_numbers=("NWC", "WIO", "NWC"), feature_group_count=ch)
    return y + b.astype(x.dtype)


def ssd_chunked(xh, dt, a, bm, cm):
    bsz, s = xh.shape[:2]
    nc = s // SSD_CHUNK
    chunk = lambda t: t.reshape((bsz, nc, SSD_CHUNK) + t.shape[2:])
    xc = chunk(xh * dt[..., None])
    bc, cc = chunk(bm), chunk(cm)
    a_cs = jnp.cumsum(jnp.moveaxis(chunk(dt * a), 2, -1), axis=-1)
    causal = jnp.tril(jnp.ones((SSD_CHUNK, SSD_CHUNK), dtype=bool))
    seg = jnp.where(causal, a_cs[..., :, None] - a_cs[..., None, :], -jnp.inf)
    cb = jnp.einsum("bclgn,bcsgn->bcgls", cc, bc)
    att = cb[:, :, :, None] * jnp.exp(seg)
    y_diag = jnp.einsum("bcgrls,bcsgrp->bclgrp", att, xc)
    decay_st = jnp.moveaxis(jnp.exp(a_cs[..., -1:] - a_cs), -1, 2)
    states = jnp.einsum("bclgn,bclgrp->bcgrpn", bc, xc * decay_st[..., None])
    chunk_decay = jnp.exp(a_cs[..., -1])

    def step(h, inp):
        st, dec = inp
        return h * dec[..., None, None] + st, h

    h0 = jnp.zeros_like(states[:, 0])
    _, prev = lax.scan(step, h0, (jnp.moveaxis(states, 1, 0), jnp.moveaxis(chunk_decay, 1, 0)))
    prev = jnp.moveaxis(prev, 0, 1)
    decay_out = jnp.moveaxis(jnp.exp(a_cs), -1, 2)
    y_off = jnp.einsum("bclgn,bcgrpn->bclgrp", cc, prev) * decay_out[..., None]
    return (y_diag + y_off).reshape(xh.shape)


def hybrid_mixer(h, w_in, conv_w, conv_b, dt_bias, a_log, d_skip, ssd_norm_g, v_norm_g, ws, bs, gm_out_g, w_out):
    bsz, s, _ = h.shape
    f32 = jnp.float32
    proj = h @ w_in
    z, xbc, dt_raw, gm = jnp.split(proj, [D_SSD, D_SSD + CONV_DIM, D_SSD + CONV_DIM + SSD_HEADS], axis=-1)
    xbc = jax.nn.silu(causal_dwconv(xbc, conv_w, conv_b))
    xs, bm, cm = jnp.split(xbc, [D_SSD, D_SSD + SSD_GROUPS * D_STATE], axis=-1)
    xs = xs.astype(f32).reshape(bsz, s, SSD_GROUPS, SSD_HEADS_PER_GROUP, SSD_HEAD_DIM)
    bm = bm.astype(f32).reshape(bsz, s, SSD_GROUPS, D_STATE)
    cm = cm.astype(f32).reshape(bsz, s, SSD_GROUPS, D_STATE)
    dt = jax.nn.softplus(dt_raw.astype(f32) + dt_bias.astype(f32)).reshape(bsz, s, SSD_GROUPS, SSD_HEADS_PER_GROUP)
    a = -jnp.exp(a_log.astype(f32)).reshape(SSD_GROUPS, SSD_HEADS_PER_GROUP)
    y = ssd_chunked(xs, dt, a, bm, cm) + d_skip.astype(f32).reshape(SSD_GROUPS, SSD_HEADS_PER_GROUP)[..., None] * xs
    gw = SSD_HEADS_PER_GROUP * SSD_HEAD_DIM
    y = y.reshape(bsz, s, SSD_GROUPS, gw) * jax.nn.silu(z.astype(f32)).reshape(bsz, s, SSD_GROUPS, gw)
    y = rmsnorm(y, ssd_norm_g.reshape(SSD_GROUPS, gw)).reshape(bsz, s, D_SSD).astype(h.dtype)
    u, v = jnp.split(jax.nn.gelu(gm, approximate=False), 2, axis=-1)
    v = rmsnorm(v, v_norm_g).reshape(bsz, s // GM_CHUNK, GM_CHUNK, GM_HEADS, GM_HEAD_DIM)
    ws_c = jnp.where(jnp.tril(jnp.ones((GM_CHUNK, GM_CHUNK), dtype=bool)), ws, jnp.zeros_like(ws))
    sv = jnp.einsum("hts,bcshd->bcthd", ws_c, v) + bs.T[:, :, None]
    g_out = rmsnorm(u * sv.reshape(bsz, s, D_GM), gm_out_g)
    return jnp.concatenate([y, g_out], axis=-1) @ w_out


def conv_gated_ffn(h, w_up, conv_w, conv_b, w_down):
    gate, val = jnp.split(h @ w_up, 2, axis=-1)
    gate = causal_dwconv(gate, conv_w, conv_b)
    return (jax.nn.silu(gate) * val) @ w_down


def _fwd_setup_inputs(seed: int = 0) -> dict:
    key = jax.random.key(seed)
    ks = jax.random.split(key, 24)
    f32 = jnp.float32
    L = DEPTH
    nrm = lambda k, shape, sc: jax.random.normal(k, shape, f32) * sc
    gain = lambda k, shape: 1.0 + 0.1 * jax.random.normal(k, shape, f32)
    dt0 = jnp.exp(jax.random.uniform(ks[9], (L, SSD_HEADS), f32, math.log(1e-3), math.log(1e-1)))
    return {
        "x": nrm(ks[0], (BATCH, SEQ, D_MODEL), 1.0),
        "c": nrm(ks[1], (BATCH, D_MODEL), 1.0),
        "ada_w": nrm(ks[2], (L, D_MODEL, N_MOD * D_MODEL), 0.5 * D_MODEL ** -0.5),
        "ada_b": nrm(ks[3], (L, N_MOD * D_MODEL), 0.01),
        "norm1_g": gain(ks[4], (L, D_MODEL)),
        "norm2_g": gain(ks[5], (L, D_MODEL)),
        "w_in": nrm(ks[6], (L, D_MODEL, N_IN), D_MODEL ** -0.5),
        "ssd_conv_w": nrm(ks[7], (L, SSD_CONV, CONV_DIM), SSD_CONV ** -0.5),
        "ssd_conv_b": nrm(ks[8], (L, CONV_DIM), 0.01),
        "ssd_dt_bias": dt0 + jnp.log(-jnp.expm1(-dt0)),
        "ssd_a_log": jnp.log(jax.random.uniform(ks[10], (L, SSD_HEADS), f32, 1.0, 16.0)),
        "ssd_d": gain(ks[11], (L, SSD_HEADS)),
        "ssd_norm_g": gain(ks[12], (L, D_SSD)),
        "gm_vnorm_g": gain(ks[13], (L, D_GM)),
        "gm_ws": nrm(ks[14], (L, GM_HEADS, GM_CHUNK, GM_CHUNK), GM_CHUNK ** -0.5),
        "gm_bs": gain(ks[15], (L, GM_HEADS, GM_CHUNK)),
        "gm_out_g": gain(ks[16], (L, D_GM)),
        "w_out": nrm(ks[17], (L, D_MIX, D_MODEL), D_MIX ** -0.5),
        "ff_up": nrm(ks[18], (L, D_MODEL, 2 * D_FF), D_MODEL ** -0.5),
        "ff_conv_w": nrm(ks[19], (L, FF_CONV, D_FF), FF_CONV ** -0.5),
        "ff_conv_b": nrm(ks[20], (L, D_FF), 0.01),
        "ff_down": nrm(ks[21], (L, D_FF, D_MODEL), D_FF ** -0.5),
        "final_g": gain(ks[22], (D_MODEL,)),
    }


def _fwd_reference(x, c, ada_w, ada_b, norm1_g, norm2_g, w_in, ssd_conv_w, ssd_conv_b, ssd_dt_bias, ssd_a_log, ssd_d, ssd_norm_g, gm_vnorm_g, gm_ws, gm_bs, gm_out_g, w_out, ff_up, ff_conv_w, ff_conv_b, ff_down, final_g):
    c_act = jax.nn.silu(c)
    for l in range(DEPTH):
        mod = (c_act @ ada_w[l] + ada_b[l])[:, None, :]
        sh1, sc1, g1, sh2, sc2, g2 = jnp.split(mod, N_MOD, axis=-1)
        h = rmsnorm(x, norm1_g[l]) * (1 + sc1) + sh1
        x = x + g1 * hybrid_mixer(h, w_in[l], ssd_conv_w[l], ssd_conv_b[l], ssd_dt_bias[l], ssd_a_log[l], ssd_d[l], ssd_norm_g[l], gm_vnorm_g[l], gm_ws[l], gm_bs[l], gm_out_g[l], w_out[l])
        h = rmsnorm(x, norm2_g[l]) * (1 + sc2) + sh2
        x = x + g2 * conv_gated_ffn(h, ff_up[l], ff_conv_w[l], ff_conv_b[l], ff_down[l])
    return rmsnorm(x, final_g)


import jax as _jax
import jax.numpy as _jnp

TWIN_FORMAT = 'train_step'
FWD_PARAMS = ['x', 'c', 'ada_w', 'ada_b', 'norm1_g', 'norm2_g', 'w_in', 'ssd_conv_w', 'ssd_conv_b', 'ssd_dt_bias', 'ssd_a_log', 'ssd_d', 'ssd_norm_g', 'gm_vnorm_g', 'gm_ws', 'gm_bs', 'gm_out_g', 'w_out', 'ff_up', 'ff_conv_w', 'ff_conv_b', 'ff_down', 'final_g']
TWIN_WEIGHTS = ['ada_w', 'ada_b', 'norm1_g', 'norm2_g', 'w_in', 'ssd_conv_w', 'ssd_conv_b', 'ssd_dt_bias', 'ssd_a_log', 'ssd_d', 'ssd_norm_g', 'gm_vnorm_g', 'gm_ws', 'gm_bs', 'gm_out_g', 'w_out', 'ff_up', 'ff_conv_w', 'ff_conv_b', 'ff_down', 'final_g']
TWIN_DIFF_INPUT = 'x'
TWIN_INPUTS = ['x', 'c', 'ada_w', 'ada_b', 'norm1_g', 'norm2_g', 'w_in', 'ssd_conv_w', 'ssd_conv_b', 'ssd_dt_bias', 'ssd_a_log', 'ssd_d', 'ssd_norm_g', 'gm_vnorm_g', 'gm_ws', 'gm_bs', 'gm_out_g', 'w_out', 'ff_up', 'ff_conv_w', 'ff_conv_b', 'ff_down', 'final_g', 'loss_target', 'm_ada_w', 'm_ada_b', 'm_norm1_g', 'm_norm2_g', 'm_w_in', 'm_ssd_conv_w', 'm_ssd_conv_b', 'm_ssd_dt_bias', 'm_ssd_a_log', 'm_ssd_d', 'm_ssd_norm_g', 'm_gm_vnorm_g', 'm_gm_ws', 'm_gm_bs', 'm_gm_out_g', 'm_w_out', 'm_ff_up', 'm_ff_conv_w', 'm_ff_conv_b', 'm_ff_down', 'm_final_g', 'v_ada_w', 'v_ada_b', 'v_norm1_g', 'v_norm2_g', 'v_w_in', 'v_ssd_conv_w', 'v_ssd_conv_b', 'v_ssd_dt_bias', 'v_ssd_a_log', 'v_ssd_d', 'v_ssd_norm_g', 'v_gm_vnorm_g', 'v_gm_ws', 'v_gm_bs', 'v_gm_out_g', 'v_w_out', 'v_ff_up', 'v_ff_conv_w', 'v_ff_conv_b', 'v_ff_down', 'v_final_g']
TWIN_OUTPUTS = ['loss', 'grad_x', 'grad_ada_w', 'grad_ada_b', 'grad_norm1_g', 'grad_norm2_g', 'grad_w_in', 'grad_ssd_conv_w', 'grad_ssd_conv_b', 'grad_ssd_dt_bias', 'grad_ssd_a_log', 'grad_ssd_d', 'grad_ssd_norm_g', 'grad_gm_vnorm_g', 'grad_gm_ws', 'grad_gm_bs', 'grad_gm_out_g', 'grad_w_out', 'grad_ff_up', 'grad_ff_conv_w', 'grad_ff_conv_b', 'grad_ff_down', 'grad_final_g', 'delta_ada_w', 'delta_ada_b', 'delta_norm1_g', 'delta_norm2_g', 'delta_w_in', 'delta_ssd_conv_w', 'delta_ssd_conv_b', 'delta_ssd_dt_bias', 'delta_ssd_a_log', 'delta_ssd_d', 'delta_ssd_norm_g', 'delta_gm_vnorm_g', 'delta_gm_ws', 'delta_gm_bs', 'delta_gm_out_g', 'delta_w_out', 'delta_ff_up', 'delta_ff_conv_w', 'delta_ff_conv_b', 'delta_ff_down', 'delta_final_g', 'new_m_ada_w', 'new_m_ada_b', 'new_m_norm1_g', 'new_m_norm2_g', 'new_m_w_in', 'new_m_ssd_conv_w', 'new_m_ssd_conv_b', 'new_m_ssd_dt_bias', 'new_m_ssd_a_log', 'new_m_ssd_d', 'new_m_ssd_norm_g', 'new_m_gm_vnorm_g', 'new_m_gm_ws', 'new_m_gm_bs', 'new_m_gm_out_g', 'new_m_w_out', 'new_m_ff_up', 'new_m_ff_conv_w', 'new_m_ff_conv_b', 'new_m_ff_down', 'new_m_final_g', 'new_v_ada_w', 'new_v_ada_b', 'new_v_norm1_g', 'new_v_norm2_g', 'new_v_w_in', 'new_v_ssd_conv_w', 'new_v_ssd_conv_b', 'new_v_ssd_dt_bias', 'new_v_ssd_a_log', 'new_v_ssd_d', 'new_v_ssd_norm_g', 'new_v_gm_vnorm_g', 'new_v_gm_ws', 'new_v_gm_bs', 'new_v_gm_out_g', 'new_v_w_out', 'new_v_ff_up', 'new_v_ff_conv_w', 'new_v_ff_conv_b', 'new_v_ff_down', 'new_v_final_g']
TWIN_LEAF_KINDS = {'loss': 'loss', 'grad_x': 'grad_x', 'grad_ada_w': 'grad_w', 'grad_ada_b': 'grad_w', 'grad_norm1_g': 'grad_w', 'grad_norm2_g': 'grad_w', 'grad_w_in': 'grad_w', 'grad_ssd_conv_w': 'grad_w', 'grad_ssd_conv_b': 'grad_w', 'grad_ssd_dt_bias': 'grad_w', 'grad_ssd_a_log': 'grad_w', 'grad_ssd_d': 'grad_w', 'grad_ssd_norm_g': 'grad_w', 'grad_gm_vnorm_g': 'grad_w', 'grad_gm_ws': 'grad_w', 'grad_gm_bs': 'grad_w', 'grad_gm_out_g': 'grad_w', 'grad_w_out': 'grad_w', 'grad_ff_up': 'grad_w', 'grad_ff_conv_w': 'grad_w', 'grad_ff_conv_b': 'grad_w', 'grad_ff_down': 'grad_w', 'grad_final_g': 'grad_w', 'delta_ada_w': 'delta_w', 'delta_ada_b': 'delta_w', 'delta_norm1_g': 'delta_w', 'delta_norm2_g': 'delta_w', 'delta_w_in': 'delta_w', 'delta_ssd_conv_w': 'delta_w', 'delta_ssd_conv_b': 'delta_w', 'delta_ssd_dt_bias': 'delta_w', 'delta_ssd_a_log': 'delta_w', 'delta_ssd_d': 'delta_w', 'delta_ssd_norm_g': 'delta_w', 'delta_gm_vnorm_g': 'delta_w', 'delta_gm_ws': 'delta_w', 'delta_gm_bs': 'delta_w', 'delta_gm_out_g': 'delta_w', 'delta_w_out': 'delta_w', 'delta_ff_up': 'delta_w', 'delta_ff_conv_w': 'delta_w', 'delta_ff_conv_b': 'delta_w', 'delta_ff_down': 'delta_w', 'delta_final_g': 'delta_w', 'new_m_ada_w': 'new_m', 'new_m_ada_b': 'new_m', 'new_m_norm1_g': 'new_m', 'new_m_norm2_g': 'new_m', 'new_m_w_in': 'new_m', 'new_m_ssd_conv_w': 'new_m', 'new_m_ssd_conv_b': 'new_m', 'new_m_ssd_dt_bias': 'new_m', 'new_m_ssd_a_log': 'new_m', 'new_m_ssd_d': 'new_m', 'new_m_ssd_norm_g': 'new_m', 'new_m_gm_vnorm_g': 'new_m', 'new_m_gm_ws': 'new_m', 'new_m_gm_bs': 'new_m', 'new_m_gm_out_g': 'new_m', 'new_m_w_out': 'new_m', 'new_m_ff_up': 'new_m', 'new_m_ff_conv_w': 'new_m', 'new_m_ff_conv_b': 'new_m', 'new_m_ff_down': 'new_m', 'new_m_final_g': 'new_m', 'new_v_ada_w': 'new_v', 'new_v_ada_b': 'new_v', 'new_v_norm1_g': 'new_v', 'new_v_norm2_g': 'new_v', 'new_v_w_in': 'new_v', 'new_v_ssd_conv_w': 'new_v', 'new_v_ssd_conv_b': 'new_v', 'new_v_ssd_dt_bias': 'new_v', 'new_v_ssd_a_log': 'new_v', 'new_v_ssd_d': 'new_v', 'new_v_ssd_norm_g': 'new_v', 'new_v_gm_vnorm_g': 'new_v', 'new_v_gm_ws': 'new_v', 'new_v_gm_bs': 'new_v', 'new_v_gm_out_g': 'new_v', 'new_v_w_out': 'new_v', 'new_v_ff_up': 'new_v', 'new_v_ff_conv_w': 'new_v', 'new_v_ff_conv_b': 'new_v', 'new_v_ff_down': 'new_v', 'new_v_final_g': 'new_v'}


def _forward(args):
    return _fwd_reference(*[args[k] for k in FWD_PARAMS])


def _output_shape():
    out = _jax.eval_shape(lambda: _forward(_fwd_setup_inputs(0)))
    return out.shape, out.dtype

N_MICROBATCH = 1
ADAM_LR = 0.001
ADAM_B1 = 0.9
ADAM_B2 = 0.999
ADAM_EPS = 1e-08
ADAM_WD = 0.01
ADAM_STEP = 10
PER_EXAMPLE_BATCH_AXIS = {'x': 0, 'c': 0, 'loss_target': 0}
SHARED_INPUTS = []
_WEIGHT_DTYPES = {'ada_w': _jnp.float32, 'ada_b': _jnp.float32, 'norm1_g': _jnp.float32, 'norm2_g': _jnp.float32, 'w_in': _jnp.float32, 'ssd_conv_w': _jnp.float32, 'ssd_conv_b': _jnp.float32, 'ssd_dt_bias': _jnp.float32, 'ssd_a_log': _jnp.float32, 'ssd_d': _jnp.float32, 'ssd_norm_g': _jnp.float32, 'gm_vnorm_g': _jnp.float32, 'gm_ws': _jnp.float32, 'gm_bs': _jnp.float32, 'gm_out_g': _jnp.float32, 'w_out': _jnp.float32, 'ff_up': _jnp.float32, 'ff_conv_w': _jnp.float32, 'ff_conv_b': _jnp.float32, 'ff_down': _jnp.float32, 'final_g': _jnp.float32}
MOMENT_SCALE = {'ada_w': 1.736470e-01, 'ada_b': 3.374976e-01, 'norm1_g': 7.128166e-02, 'norm2_g': 5.348369e-02, 'w_in': 3.609665e-02, 'ssd_conv_w': 3.623535e-02, 'ssd_conv_b': 5.619306e-02, 'ssd_dt_bias': 1.482710e-01, 'ssd_a_log': 1.584078e-01, 'ssd_d': 1.348463e-01, 'ssd_norm_g': 4.463898e-02, 'gm_vnorm_g': 2.245909e-02, 'gm_ws': 2.181186e-02, 'gm_bs': 3.088831e-02, 'gm_out_g': 5.340174e-02, 'w_out': 7.019945e-02, 'ff_up': 2.402546e-02, 'ff_conv_w': 2.470573e-02, 'ff_conv_b': 2.294735e-02, 'ff_down': 3.989059e-02, 'final_g': 3.230051e+01}


def _to_microbatches(a, axis):
    t = _jnp.moveaxis(a, axis, 0)
    t = t.reshape((N_MICROBATCH, t.shape[0] // N_MICROBATCH) + t.shape[1:])
    return _jnp.moveaxis(t, 1, axis + 1)


def setup_inputs(seed: int = 0) -> dict:
    inp = _fwd_setup_inputs(seed)
    key = _jax.random.fold_in(_jax.random.key(seed), 7919)
    shape, _ = _output_shape()
    out = dict(inp)
    out["loss_target"] = _jax.random.normal(_jax.random.fold_in(key, 0), shape, _jnp.float32)
    for i, name in enumerate(TWIN_WEIGHTS):
        w = inp[name].astype(_jnp.float32)
        if MOMENT_SCALE is None:
            s = _jnp.sqrt(_jnp.mean(_jnp.square(w)) + 1e-30)
        else:
            s = MOMENT_SCALE[name]
        km, kv = _jax.random.split(_jax.random.fold_in(key, i + 1))
        out[name] = w
        out["m_" + name] = s * _jax.random.normal(km, w.shape, _jnp.float32)
        out["v_" + name] = (s * s) * _jax.random.uniform(kv, w.shape, _jnp.float32, 0.5, 1.5)
    if N_MICROBATCH > 1:
        for name, axis in PER_EXAMPLE_BATCH_AXIS.items():
            out[name] = _to_microbatches(out[name], axis)
    return {'x': out['x'], 'c': out['c'], 'ada_w': out['ada_w'], 'ada_b': out['ada_b'], 'norm1_g': out['norm1_g'], 'norm2_g': out['norm2_g'], 'w_in': out['w_in'], 'ssd_conv_w': out['ssd_conv_w'], 'ssd_conv_b': out['ssd_conv_b'], 'ssd_dt_bias': out['ssd_dt_bias'], 'ssd_a_log': out['ssd_a_log'], 'ssd_d': out['ssd_d'], 'ssd_norm_g': out['ssd_norm_g'], 'gm_vnorm_g': out['gm_vnorm_g'], 'gm_ws': out['gm_ws'], 'gm_bs': out['gm_bs'], 'gm_out_g': out['gm_out_g'], 'w_out': out['w_out'], 'ff_up': out['ff_up'], 'ff_conv_w': out['ff_conv_w'], 'ff_conv_b': out['ff_conv_b'], 'ff_down': out['ff_down'], 'final_g': out['final_g'], 'loss_target': out['loss_target'], 'm_ada_w': out['m_ada_w'], 'm_ada_b': out['m_ada_b'], 'm_norm1_g': out['m_norm1_g'], 'm_norm2_g': out['m_norm2_g'], 'm_w_in': out['m_w_in'], 'm_ssd_conv_w': out['m_ssd_conv_w'], 'm_ssd_conv_b': out['m_ssd_conv_b'], 'm_ssd_dt_bias': out['m_ssd_dt_bias'], 'm_ssd_a_log': out['m_ssd_a_log'], 'm_ssd_d': out['m_ssd_d'], 'm_ssd_norm_g': out['m_ssd_norm_g'], 'm_gm_vnorm_g': out['m_gm_vnorm_g'], 'm_gm_ws': out['m_gm_ws'], 'm_gm_bs': out['m_gm_bs'], 'm_gm_out_g': out['m_gm_out_g'], 'm_w_out': out['m_w_out'], 'm_ff_up': out['m_ff_up'], 'm_ff_conv_w': out['m_ff_conv_w'], 'm_ff_conv_b': out['m_ff_conv_b'], 'm_ff_down': out['m_ff_down'], 'm_final_g': out['m_final_g'], 'v_ada_w': out['v_ada_w'], 'v_ada_b': out['v_ada_b'], 'v_norm1_g': out['v_norm1_g'], 'v_norm2_g': out['v_norm2_g'], 'v_w_in': out['v_w_in'], 'v_ssd_conv_w': out['v_ssd_conv_w'], 'v_ssd_conv_b': out['v_ssd_conv_b'], 'v_ssd_dt_bias': out['v_ssd_dt_bias'], 'v_ssd_a_log': out['v_ssd_a_log'], 'v_ssd_d': out['v_ssd_d'], 'v_ssd_norm_g': out['v_ssd_norm_g'], 'v_gm_vnorm_g': out['v_gm_vnorm_g'], 'v_gm_ws': out['v_gm_ws'], 'v_gm_bs': out['v_gm_bs'], 'v_gm_out_g': out['v_gm_out_g'], 'v_w_out': out['v_w_out'], 'v_ff_up': out['v_ff_up'], 'v_ff_conv_w': out['v_ff_conv_w'], 'v_ff_conv_b': out['v_ff_conv_b'], 'v_ff_down': out['v_ff_down'], 'v_final_g': out['v_final_g']}


def _loss(weights, diff, rest, loss_target):
    with _jax.named_scope("forward"):
        args = {**rest, TWIN_DIFF_INPUT: diff, **{k: w.astype(_WEIGHT_DTYPES[k]) for k, w in weights.items()}}
        y = _forward(args)
    with _jax.named_scope("loss_head"):
        err = _jnp.square(y.astype(_jnp.float32) - loss_target)
        return 0.5 * _jnp.sum(_jnp.mean(err, axis=-1)) if err.ndim else 0.5 * err


def _adamw(w, g, m, v):
    m = ADAM_B1 * m + (1.0 - ADAM_B1) * g
    v = ADAM_B2 * v + (1.0 - ADAM_B2) * _jnp.square(g)
    m_hat = m / (1.0 - ADAM_B1 ** ADAM_STEP)
    v_hat = v / (1.0 - ADAM_B2 ** ADAM_STEP)
    delta = -ADAM_LR * (m_hat / (_jnp.sqrt(v_hat) + ADAM_EPS) + ADAM_WD * w)
    return delta, m, v


def reference(x, c, ada_w, ada_b, norm1_g, norm2_g, w_in, ssd_conv_w, ssd_conv_b, ssd_dt_bias, ssd_a_log, ssd_d, ssd_norm_g, gm_vnorm_g, gm_ws, gm_bs, gm_out_g, w_out, ff_up, ff_conv_w, ff_conv_b, ff_down, final_g, loss_target, m_ada_w, m_ada_b, m_norm1_g, m_norm2_g, m_w_in, m_ssd_conv_w, m_ssd_conv_b, m_ssd_dt_bias, m_ssd_a_log, m_ssd_d, m_ssd_norm_g, m_gm_vnorm_g, m_gm_ws, m_gm_bs, m_gm_out_g, m_w_out, m_ff_up, m_ff_conv_w, m_ff_conv_b, m_ff_down, m_final_g, v_ada_w, v_ada_b, v_norm1_g, v_norm2_g, v_w_in, v_ssd_conv_w, v_ssd_conv_b, v_ssd_dt_bias, v_ssd_a_log, v_ssd_d, v_ssd_norm_g, v_gm_vnorm_g, v_gm_ws, v_gm_bs, v_gm_out_g, v_w_out, v_ff_up, v_ff_conv_w, v_ff_conv_b, v_ff_down, v_final_g):
    given = dict(x=x, c=c, ada_w=ada_w, ada_b=ada_b, norm1_g=norm1_g, norm2_g=norm2_g, w_in=w_in, ssd_conv_w=ssd_conv_w, ssd_conv_b=ssd_conv_b, ssd_dt_bias=ssd_dt_bias, ssd_a_log=ssd_a_log, ssd_d=ssd_d, ssd_norm_g=ssd_norm_g, gm_vnorm_g=gm_vnorm_g, gm_ws=gm_ws, gm_bs=gm_bs, gm_out_g=gm_out_g, w_out=w_out, ff_up=ff_up, ff_conv_w=ff_conv_w, ff_conv_b=ff_conv_b, ff_down=ff_down, final_g=final_g, loss_target=loss_target, m_ada_w=m_ada_w, m_ada_b=m_ada_b, m_norm1_g=m_norm1_g, m_norm2_g=m_norm2_g, m_w_in=m_w_in, m_ssd_conv_w=m_ssd_conv_w, m_ssd_conv_b=m_ssd_conv_b, m_ssd_dt_bias=m_ssd_dt_bias, m_ssd_a_log=m_ssd_a_log, m_ssd_d=m_ssd_d, m_ssd_norm_g=m_ssd_norm_g, m_gm_vnorm_g=m_gm_vnorm_g, m_gm_ws=m_gm_ws, m_gm_bs=m_gm_bs, m_gm_out_g=m_gm_out_g, m_w_out=m_w_out, m_ff_up=m_ff_up, m_ff_conv_w=m_ff_conv_w, m_ff_conv_b=m_ff_conv_b, m_ff_down=m_ff_down, m_final_g=m_final_g, v_ada_w=v_ada_w, v_ada_b=v_ada_b, v_norm1_g=v_norm1_g, v_norm2_g=v_norm2_g, v_w_in=v_w_in, v_ssd_conv_w=v_ssd_conv_w, v_ssd_conv_b=v_ssd_conv_b, v_ssd_dt_bias=v_ssd_dt_bias, v_ssd_a_log=v_ssd_a_log, v_ssd_d=v_ssd_d, v_ssd_norm_g=v_ssd_norm_g, v_gm_vnorm_g=v_gm_vnorm_g, v_gm_ws=v_gm_ws, v_gm_bs=v_gm_bs, v_gm_out_g=v_gm_out_g, v_w_out=v_w_out, v_ff_up=v_ff_up, v_ff_conv_w=v_ff_conv_w, v_ff_conv_b=v_ff_conv_b, v_ff_down=v_ff_down, v_final_g=v_final_g)
    weights = {n: given[n] for n in TWIN_WEIGHTS}
    shared = {n: given[n] for n in SHARED_INPUTS}
    per_example = {n: given[n] for n in ['x', 'c']}
    grad_fn = _jax.value_and_grad(_loss, argnums=(0, 1))

    def one_microbatch(ex, loss_target):
        ex = dict(ex)
        diff = ex.pop(TWIN_DIFF_INPUT)
        return grad_fn(weights, diff, {**shared, **ex}, loss_target)

    if N_MICROBATCH == 1:
        loss, (grad_w, grad_x) = one_microbatch(per_example, given["loss_target"])
    else:
        def body(carry, xs):
            loss_sum, grad_sum = carry
            l_k, (gw_k, gx_k) = one_microbatch(xs[0], xs[1])
            with _jax.named_scope("update"):
                return (loss_sum + l_k, _jax.tree.map(_jnp.add, grad_sum, gw_k)), gx_k

        init = (_jnp.zeros((), _jnp.float32), _jax.tree.map(_jnp.zeros_like, weights))
        (loss, grad_w), grad_x = _jax.lax.scan(body, init, (per_example, given["loss_target"]))
    with _jax.named_scope("update"):
        delta_w, new_m, new_v = {}, {}, {}
        for n in TWIN_WEIGHTS:
            delta_w[n], new_m[n], new_v[n] = _adamw(weights[n], grad_w[n], given["m_" + n], given["v_" + n])
    return (loss, grad_x, *[grad_w[n] for n in TWIN_WEIGHTS], *[delta_w[n] for n in TWIN_WEIGHTS],
            *[new_m[n] for n in TWIN_WEIGHTS], *[new_v[n] for n in TWIN_WEIGHTS])
```

```python
import functools

import jax
import jax.numpy as jnp
from jax import lax
from jax.experimental import pallas as pl
from jax.experimental.pallas import tpu as pltpu

F32 = jnp.float32
BF16 = jnp.bfloat16
EPS = 1e-6
CHUNK = 128
HEAD_DIM = 64
STATE = 128
GM_HEAD = 128
SSD_GROUPS = 2
SSD_CONV = 4
FF_CONV = 3
N_MOD = 6
LANES = 128
SUBLANES = 8
PACK_COLS = 1024
V7X_VMEM_LIMIT = 48 * 1024 * 1024
MM_VMEM_BUDGET = 30 * 1024 * 1024
ADAM_LR, ADAM_B1, ADAM_B2, ADAM_EPS, ADAM_WD, ADAM_STEP = 0.001, 0.9, 0.999, 1e-08, 0.01, 10
MESH = pl.DeviceIdType.MESH
AXES = ("x", "y", "c")


def _round_up(n, m):
    return (n + m - 1) // m * m


def _divisors(n, cap, mult=LANES):
    out = [t for t in range(mult, min(n, cap) + 1, mult) if n % t == 0]
    return out or [n]


def _pcall(body, name, out_shape, grid=(), in_specs=None, out_specs=None, scratch=(), sem=None, prefetch=0):
    params = pltpu.CompilerParams(dimension_semantics=sem, vmem_limit_bytes=V7X_VMEM_LIMIT)
    if prefetch:
        spec = pltpu.PrefetchScalarGridSpec(num_scalar_prefetch=prefetch, grid=grid, in_specs=in_specs, out_specs=out_specs, scratch_shapes=list(scratch))
        return pl.pallas_call(body, name=name, out_shape=out_shape, grid_spec=spec, compiler_params=params)
    return pl.pallas_call(body, name=name, out_shape=out_shape, grid=grid, in_specs=in_specs, out_specs=out_specs, scratch_shapes=list(scratch), compiler_params=params)


def _sds(shape, dtype):
    return jax.ShapeDtypeStruct(tuple(shape), dtype)


def _mm_tiles(m, n, k, a_bytes, b_bytes, o_bytes):
    best, best_key = None, None
    for tm in _divisors(m, 1024):
        for tn in _divisors(n, 2560):
            for tk in _divisors(k, 2560):
                vmem = 2 * (tm * tk * a_bytes + tk * tn * b_bytes + tm * tn * o_bytes) + tm * tn * 4
                if vmem > MM_VMEM_BUDGET or tm * tn * tk > 512 * 1024 * 2560:
                    continue
                key = (tm * tn * tk, tm * tn, tm)
                if best_key is None or key > best_key:
                    best, best_key = (tm, tn, tk), key
    return best


def _matmul(name, a, b, mode, out_dtype):
    if mode == "nn":
        (m, k), n = a.shape, b.shape[1]
    elif mode == "nt":
        (m, k), n = a.shape, b.shape[0]
    else:
        (k, m), n = a.shape, b.shape[1]
    tm, tn, tk = _mm_tiles(m, n, k, a.dtype.itemsize, b.dtype.itemsize, jnp.dtype(out_dtype).itemsize)
    nk = k // tk
    if mode == "nn":
        a_spec = pl.BlockSpec((tm, tk), lambda i, j, kk: (i, kk))
        b_spec = pl.BlockSpec((tk, tn), lambda i, j, kk: (kk, j))
        dims = ((1,), (0,))
    elif mode == "nt":
        a_spec = pl.BlockSpec((tm, tk), lambda i, j, kk: (i, kk))
        b_spec = pl.BlockSpec((tn, tk), lambda i, j, kk: (j, kk))
        dims = ((1,), (1,))
    else:
        a_spec = pl.BlockSpec((tk, tm), lambda i, j, kk: (kk, i))
        b_spec = pl.BlockSpec((tk, tn), lambda i, j, kk: (kk, j))
        dims = ((0,), (0,))

    def body(a_ref, b_ref, o_ref, acc_ref):
        kk = pl.program_id(2)
        p = lax.dot_general(a_ref[...].astype(BF16), b_ref[...].astype(BF16), (dims, ((), ())), preferred_element_type=F32)
        if nk == 1:
            o_ref[...] = p.astype(o_ref.dtype)
        else:
            @pl.when(kk == 0)
            def _():
                acc_ref[...] = p

            @pl.when(kk > 0)
            def _():
                acc_ref[...] += p

            @pl.when(kk == nk - 1)
            def _():
                o_ref[...] = acc_ref[...].astype(o_ref.dtype)

    return _pcall(
        body, name, _sds((m, n), out_dtype), grid=(m // tm, n // tn, nk), in_specs=[a_spec, b_spec],
        out_specs=pl.BlockSpec((tm, tn), lambda i, j, kk: (i, j)), scratch=[pltpu.VMEM((tm, tn), F32)],
        sem=("parallel", "parallel", "arbitrary"),
    )(a, b)


def _make_mm(name, out_dtype):
    @jax.custom_vjp
    def mm(a, w):
        return _matmul(name + "_fwd", a.reshape(-1, a.shape[-1]), w, "nn", out_dtype).reshape(a.shape[:-1] + (w.shape[1],))

    def fwd(a, w):
        return mm(a, w), (a, w)

    def bwd(res, dy):
        a, w = res
        a2, dy2 = a.reshape(-1, a.shape[-1]), dy.reshape(-1, dy.shape[-1])
        da = _matmul(name + "_dx", dy2, w, "nt", a.dtype).reshape(a.shape)
        dw = _matmul(name + "_dw", a2, dy2, "tn", w.dtype)
        return da, dw

    mm.defvjp(fwd, bwd)
    return mm


def _dg(a, b, dims):
    return lax.dot_general(a.astype(BF16), b.astype(BF16), (dims, ((), ())), preferred_element_type=F32)


@jax.custom_vjp
def _dot_nn(a, b):
    return _dg(a, b, ((1,), (0,)))


_dot_nn.defvjp(lambda a, b: (_dot_nn(a, b), (a, b)), lambda r, d: (_dg(d, r[1], ((1,), (1,))), _dg(r[0], d, ((0,), (0,)))))


@jax.custom_vjp
def _dot_nt(a, b):
    return _dg(a, b, ((1,), (1,)))


_dot_nt.defvjp(lambda a, b: (_dot_nt(a, b), (a, b)), lambda r, d: (_dg(d, r[1], ((1,), (0,))), _dg(d, r[0], ((0,), (0,)))))


@jax.custom_vjp
def _dot_tn(a, b):
    return _dg(a, b, ((0,), (0,)))


_dot_tn.defvjp(lambda a, b: (_dot_tn(a, b), (a, b)), lambda r, d: (_dg(r[1], d, ((1,), (1,))), _dg(r[0], d, ((1,), (0,)))))


def _exact_dot(a, c, dims):
    hi = a.astype(BF16)
    r1 = a - hi.astype(F32)
    mid = r1.astype(BF16)
    lo = (r1 - mid.astype(F32)).astype(BF16)
    cb = c.astype(BF16)
    f = lambda t: lax.dot_general(t, cb, (dims, ((), ())), preferred_element_type=F32)
    return f(hi) + f(mid) + f(lo)


@jax.custom_vjp
def _sel_right(a, c):
    return _exact_dot(a, c, ((1,), (0,)))


_sel_right.defvjp(lambda a, c: (_sel_right(a, c), c), lambda c, d: (_exact_dot(d, c, ((1,), (1,))), jnp.zeros_like(c)))


def _exact_dot_left(c, a, dims):
    hi = a.astype(BF16)
    r1 = a - hi.astype(F32)
    mid = r1.astype(BF16)
    lo = (r1 - mid.astype(F32)).astype(BF16)
    cb = c.astype(BF16)
    f = lambda t: lax.dot_general(cb, t, (dims, ((), ())), preferred_element_type=F32)
    return f(hi) + f(mid) + f(lo)


@jax.custom_vjp
def _sel_left(c, a):
    return _exact_dot_left(c, a, ((1,), (0,)))


_sel_left.defvjp(lambda c, a: (_sel_left(c, a), c), lambda c, d: (jnp.zeros_like(c), _exact_dot_left(c, d, ((0,), (0,)))))


def _sigmoid(x):
    return 1.0 / (1.0 + jnp.exp(-x))


def _rms(x, g):
    return x * lax.rsqrt(jnp.mean(x * x, axis=-1, keepdims=True) + EPS) * g


def _gelu(x):
    return 0.5 * x * (1.0 + lax.erf(x * (2.0 ** -0.5)))


def _causal(n):
    return lax.broadcasted_iota(jnp.int32, (n, n), 0) >= lax.broadcasted_iota(jnp.int32, (n, n), 1)


def _row_in_specs(rows, bparams, gparams, tm):
    specs = [pl.BlockSpec((1, tm, w), lambda b, i, cb=cb: (b, i, cb)) for (_, w, cb, _) in rows]
    specs += [pl.BlockSpec((1, 1, p.shape[-1]), lambda b, i: (b, 0, 0)) for p in bparams]
    specs += [pl.BlockSpec(p.shape, lambda b, i, n=p.ndim: (0,) * n) for p in gparams]
    return specs


def _row_vals(refs, n_rows, n_b, n_g):
    vals = [r[0].astype(F32) for r in refs[:n_rows]]
    vals += [r[0].astype(F32) for r in refs[n_rows:n_rows + n_b]]
    vals += [r[...].astype(F32) for r in refs[n_rows + n_b:n_rows + n_b + n_g]]
    return vals


def _row_fwd(name, tile, rows, bparams, gparams, outs, n_sum, tm):
    bl, s = rows[0][0].shape[:2]
    n_in = len(rows) + len(bparams) + len(gparams)

    def body(*refs):
        first = (pl.program_id(0) == 0) & (pl.program_id(1) == 0)
        res = tile(*_row_vals(refs, len(rows), len(bparams), len(gparams)))
        o_refs = refs[n_in:]
        for k in range(len(outs)):
            o_refs[k][0] = res[k].astype(o_refs[k].dtype)
        for k in range(n_sum):
            o_ref, val = o_refs[len(outs) + k], res[len(outs) + k]

            @pl.when(first)
            def _(o_ref=o_ref, val=val):
                o_ref[...] = val

            @pl.when(jnp.logical_not(first))
            def _(o_ref=o_ref, val=val):
                o_ref[...] += val

    out_shape = [_sds((bl, s, w), dt) for (w, dt) in outs] + [_sds((1, 1), F32)] * n_sum
    out_specs = [pl.BlockSpec((1, tm, w), lambda b, i: (b, i, 0)) for (w, _) in outs] + [pl.BlockSpec((1, 1), lambda b, i: (0, 0))] * n_sum
    return _pcall(
        body, name, out_shape, grid=(bl, s // tm), in_specs=_row_in_specs(rows, bparams, gparams, tm), out_specs=out_specs,
        sem=("arbitrary", "arbitrary"),
    )(*[r[0] for r in rows], *bparams, *gparams)


def _row_bwd(name, tile, rows, bparams, gparams, outs, n_sum, tm, cts):
    bl, s = rows[0][0].shape[:2]
    n_r, n_b, n_g = len(rows), len(bparams), len(gparams)
    n_in = n_r + n_b + n_g
    n_ct = len(outs) + n_sum
    grad_rows = [k for k in range(n_r) if rows[k][3]]

    def body(*refs):
        b, i = pl.program_id(0), pl.program_id(1)
        vals = _row_vals(refs, n_r, n_b, n_g)
        ct_refs = refs[n_in:n_in + n_ct]
        ct = [r[0].astype(F32) for r in ct_refs[:len(outs)]] + [r[...] for r in ct_refs[len(outs):]]
        _, vjp = jax.vjp(tile, *vals)
        grads = vjp(tuple(ct))
        o_refs = refs[n_in + n_ct:]
        for j, k in enumerate(grad_rows):
            o_refs[j][0] = grads[k].astype(o_refs[j].dtype)
        for k in range(n_b):
            o_ref, val = o_refs[len(grad_rows) + k], grads[n_r + k]

            @pl.when(i == 0)
            def _(o_ref=o_ref, val=val):
                o_ref[0] = val

            @pl.when(i > 0)
            def _(o_ref=o_ref, val=val):
                o_ref[0] += val

        first = (b == 0) & (i == 0)
        for k in range(n_g):
            o_ref, val = o_refs[len(grad_rows) + n_b + k], grads[n_r + n_b + k]

            @pl.when(first)
            def _(o_ref=o_ref, val=val):
                o_ref[...] = val

            @pl.when(jnp.logical_not(first))
            def _(o_ref=o_ref, val=val):
                o_ref[...] += val

    in_specs = _row_in_specs(rows, bparams, gparams, tm)
    in_specs += [pl.BlockSpec((1, tm, w), lambda b, i: (b, i, 0)) for (w, _) in outs] + [pl.BlockSpec((1, 1), lambda b, i: (0, 0))] * n_sum
    out_shape = [_sds((bl, s, rows[k][1]), rows[k][0].dtype) for k in grad_rows]
    out_shape += [_sds(p.shape, F32) for p in bparams] + [_sds(p.shape, F32) for p in gparams]
    out_specs = [pl.BlockSpec((1, tm, rows[k][1]), lambda b, i: (b, i, 0)) for k in grad_rows]
    out_specs += [pl.BlockSpec((1, 1, p.shape[-1]), lambda b, i: (b, 0, 0)) for p in bparams]
    out_specs += [pl.BlockSpec(p.shape, lambda b, i, n=p.ndim: (0,) * n) for p in gparams]
    res = _pcall(
        body, name, out_shape, grid=(bl, s // tm), in_specs=in_specs, out_specs=out_specs, sem=("arbitrary", "arbitrary"),
    )(*[r[0] for r in rows], *bparams, *gparams, *cts)
    return res[:len(grad_rows)], res[len(grad_rows):len(grad_rows) + n_b], res[len(grad_rows) + n_b:]


def _make_row_op(name, tile, row_cfg, outs, n_sum, tm):
    def pack_rows(arrs):
        return [(a, w, cb, g) for a, (w, cb, g) in zip(arrs, row_cfg)]

    @jax.custom_vjp
    def op(arrs, bparams, gparams):
        return tuple(_row_fwd(name + "_fwd", tile, pack_rows(arrs), list(bparams), list(gparams), outs, n_sum, tm))

    def fwd(arrs, bparams, gparams):
        return op(arrs, bparams, gparams), (arrs, bparams, gparams)

    def bwd(res, cts):
        arrs, bparams, gparams = res
        d_rows, d_b, d_g = _row_bwd(name + "_bwd", tile, pack_rows(arrs), list(bparams), list(gparams), outs, n_sum, tm, list(cts))
        d_rows = list(d_rows)
        full = []
        for a, (w, cb, g) in zip(arrs, row_cfg):
            if g:
                d = d_rows.pop(0)
                assert d.shape == a.shape, "row inputs that want a gradient are passed at their own width"
                full.append(d)
            else:
                full.append(jnp.zeros_like(a))
        return tuple(full), tuple(d_b), tuple(d_g)

    op.defvjp(fwd, bwd)
    return op


def _nm_tile(x, sc, sh, g):
    return (_rms(x, g) * (1.0 + sc) + sh,)


def _rnm_tile(x, o, gate, sc, sh, g):
    xn = x + gate * o
    return xn, _rms(xn, g) * (1.0 + sc) + sh


def _final_tile(x, o, tgt, gate, g):
    e = _rms(x + gate * o, g) - tgt
    return (0.5 * jnp.sum(jnp.mean(e * e, axis=-1, keepdims=True), axis=0, keepdims=True),)


def _gm_tile(u_in, v_in, vg, ws, bs_t, og):
    d = u_in.shape[1]
    u, vn = _gelu(u_in), _rms(_gelu(v_in), vg)
    causal = _causal(CHUNK)
    lane = lax.broadcasted_iota(jnp.int32, (1, LANES), 1)
    parts = []
    for h in range(d // GM_HEAD):
        bias = jnp.sum(bs_t * (lane == h).astype(F32), axis=1, keepdims=True)
        parts.append(_dot_nn(jnp.where(causal, ws[h], 0.0), vn[:, h * GM_HEAD:(h + 1) * GM_HEAD]) + bias)
    return (_rms(u * jnp.concatenate(parts, axis=1), og),)


def _shift_down(x, j):
    if j == 0:
        return x
    row = lax.broadcasted_iota(jnp.int32, x.shape, 0)
    return jnp.where(row >= j, pltpu.roll(x, j, 0), 0.0)


def _shift_up(x, j):
    if j == 0:
        return x
    s = x.shape[0]
    row = lax.broadcasted_iota(jnp.int32, x.shape, 0)
    return jnp.where(row < s - j, pltpu.roll(x, s - j, 0), 0.0)


def _conv_pre(x, w_ref, b_ref, taps):
    acc = x * w_ref[taps - 1:taps, :] + b_ref[...]
    for j in range(1, taps):
        acc = acc + _shift_down(x, j) * w_ref[taps - 1 - j:taps - j, :]
    return acc


def _conv_tc(*widths):
    for tc in (512, 256, 128):
        if all(w % tc == 0 for w in widths):
            return tc
    raise ValueError("conv widths must be multiples of the lane count")


def _conv_act_fwd(name, src, col0, chans, w8, b, taps, val_col0, out_dtype):
    bl, s, _ = src.shape
    gated = val_col0 is not None
    tc = _conv_tc(chans, col0, val_col0 or 0)

    def body(*refs):
        if gated:
            x_ref, v_ref, w_ref, b_ref, o_ref = refs
        else:
            x_ref, w_ref, b_ref, o_ref = refs
        pre = _conv_pre(x_ref[0], w_ref, b_ref, taps)
        y = pre * _sigmoid(pre)
        if gated:
            y = y * v_ref[0]
        o_ref[0] = y.astype(o_ref.dtype)

    in_specs = [pl.BlockSpec((1, s, tc), lambda j, bb: (bb, 0, col0 // tc + j))]
    args = [src]
    if gated:
        in_specs.append(pl.BlockSpec((1, s, tc), lambda j, bb: (bb, 0, val_col0 // tc + j)))
        args.append(src)
    in_specs += [pl.BlockSpec((SUBLANES, tc), lambda j, bb: (0, j)), pl.BlockSpec((1, tc), lambda j, bb: (0, j))]
    return _pcall(
        body, name, _sds((bl, s, chans), out_dtype), grid=(chans // tc, bl), in_specs=in_specs,
        out_specs=pl.BlockSpec((1, s, tc), lambda j, bb: (bb, 0, j)), sem=("parallel", "arbitrary"),
    )(*args, w8, b)


def _conv_act_bwd(name, src, col0, chans, w8, b, taps, val_col0, dy):
    bl, s, _ = src.shape
    gated = val_col0 is not None
    tc = _conv_tc(chans, col0, val_col0 or 0)
    nblk = chans // tc

    def body(*refs):
        if gated:
            x_ref, v_ref, w_ref, b_ref, dy_ref, dx_ref, dw_ref, db_ref = refs
        else:
            x_ref, w_ref, b_ref, dy_ref, dx_ref, dw_ref, db_ref = refs
        bb = pl.program_id(1)
        x = x_ref[0]
        pre = _conv_pre(x, w_ref, b_ref, taps)
        sig = _sigmoid(pre)
        d = dy_ref[0].astype(F32)

        def gate_part():
            dsil = d * v_ref[0] if gated else d
            dpre = dsil * sig * (1.0 + pre * (1.0 - sig))
            dx = dpre * w_ref[taps - 1:taps, :]
            for j in range(1, taps):
                dx = dx + _shift_up(dpre, j) * w_ref[taps - 1 - j:taps - j, :]
            dx_ref[0] = dx
            rows = [jnp.sum(dpre * _shift_down(x, taps - 1 - k), axis=0, keepdims=True) for k in range(taps)]
            rows += [jnp.zeros_like(rows[0])] * (SUBLANES - taps)
            dw = jnp.concatenate(rows, axis=0)
            db = jnp.sum(dpre, axis=0, keepdims=True)

            @pl.when(bb == 0)
            def _():
                dw_ref[...] = dw
                db_ref[...] = db

            @pl.when(bb > 0)
            def _():
                dw_ref[...] += dw
                db_ref[...] += db

        if gated:
            part = pl.program_id(2)
            pl.when(part == 0)(gate_part)

            @pl.when(part == 1)
            def _():
                dx_ref[0] = d * (pre * sig)
        else:
            gate_part()

    if gated:
        grid, sem = (nblk, bl, 2), ("parallel", "arbitrary", "arbitrary")
        im = lambda f: (lambda j, bb, p: f(j, bb, p))
    else:
        grid, sem = (nblk, bl), ("parallel", "arbitrary")
        im = lambda f: (lambda j, bb: f(j, bb, 0))
    in_specs = [pl.BlockSpec((1, s, tc), im(lambda j, bb, p: (bb, 0, col0 // tc + j)))]
    args = [src]
    if gated:
        in_specs.append(pl.BlockSpec((1, s, tc), im(lambda j, bb, p: (bb, 0, val_col0 // tc + j))))
        args.append(src)
    in_specs += [
        pl.BlockSpec((SUBLANES, tc), im(lambda j, bb, p: (0, j))), pl.BlockSpec((1, tc), im(lambda j, bb, p: (0, j))),
        pl.BlockSpec((1, s, tc), im(lambda j, bb, p: (bb, 0, j))),
    ]
    out_shape = [_sds((bl, s, chans * (2 if gated else 1)), F32), _sds((SUBLANES, chans), F32), _sds((1, chans), F32)]
    out_specs = [
        pl.BlockSpec((1, s, tc), im(lambda j, bb, p: (bb, 0, j + p * nblk))), pl.BlockSpec((SUBLANES, tc), im(lambda j, bb, p: (0, j))),
        pl.BlockSpec((1, tc), im(lambda j, bb, p: (0, j))),
    ]
    return _pcall(body, name, out_shape, grid=grid, in_specs=in_specs, out_specs=out_specs, sem=sem)(*args, w8, b, dy)


def _ssd_chunk(g_idx, states, xs, bm, cm, dtr, z, dtb, alog, dsk, ng):
    gw = xs.shape[1]
    hpg = gw // HEAD_DIM
    dt = jax.nn.softplus(dtr + dtb)
    da = dt * (-jnp.exp(alog))
    causal = _causal(CHUNK)
    acs = _sel_left(causal.astype(F32), da)
    head_of_col = g_idx * hpg + lax.broadcasted_iota(jnp.int32, (LANES, gw), 1) // HEAD_DIM
    expand = (lax.broadcasted_iota(jnp.int32, (LANES, gw), 0) == head_of_col).astype(F32)
    dt_e, acs_e = _sel_right(dt, expand), _sel_right(acs, expand)
    last = lax.broadcasted_iota(jnp.int32, (CHUNK, gw), 0) == CHUNK - 1
    alast_e = jnp.sum(jnp.where(last, acs_e, 0.0), axis=0, keepdims=True)
    xc = xs * dt_e
    xc_st = xc * jnp.exp(alast_e - acs_e)
    decay_out, chunk_decay = jnp.exp(acs_e), jnp.exp(alast_e)
    cb = _dot_nt(cm, bm)
    acs_t = acs.T
    lane = lax.broadcasted_iota(jnp.int32, (1, LANES), 1)
    sub = lax.broadcasted_iota(jnp.int32, (LANES, 1), 0)
    ys, new_states = [], []
    for p in range(gw // LANES):
        sl = slice(p * LANES, (p + 1) * LANES)
        xcp = xc[:, sl]
        y = _dot_nn(cm, states[p]) * decay_out[:, sl]
        for q in range(2):
            head = g_idx * hpg + 2 * p + q
            col = jnp.sum(acs * (lane == head).astype(F32), axis=1, keepdims=True)
            row = jnp.sum(acs_t * (sub == head).astype(F32), axis=0, keepdims=True)
            decay = jnp.where(causal, jnp.exp(jnp.where(causal, col - row, 0.0)), 0.0)
            half = ((lane // HEAD_DIM) == q).astype(F32)
            y = y + _dot_nn(cb * decay, xcp * half)
        ys.append(y)
        new_states.append(states[p] * chunk_decay[:, sl] + _dot_tn(bm, xc_st[:, sl]))
    y = jnp.concatenate(ys, axis=1) + dsk * xs
    gated = y * (z * _sigmoid(z))
    return tuple(new_states), _rms(gated, ng)


def _ssd_specs(d, gw, dt_col, rev, nc):
    ci = (lambda i: nc - 1 - i) if rev else (lambda i: i)
    return [
        pl.BlockSpec((1, CHUNK, gw), lambda g, b, i: (b, ci(i), g)),
        pl.BlockSpec((1, CHUNK, STATE), lambda g, b, i: (b, ci(i), d // STATE + g)),
        pl.BlockSpec((1, CHUNK, STATE), lambda g, b, i: (b, ci(i), d // STATE + SSD_GROUPS + g)),
        pl.BlockSpec((1, CHUNK, LANES), lambda g, b, i: (b, ci(i), dt_col // LANES)),
        pl.BlockSpec((1, CHUNK, gw), lambda g, b, i: (b, ci(i), g)),
        pl.BlockSpec((1, LANES), lambda g, b, i: (0, 0)), pl.BlockSpec((1, LANES), lambda g, b, i: (0, 0)),
        pl.BlockSpec((1, gw), lambda g, b, i: (0, g)), pl.BlockSpec((1, gw), lambda g, b, i: (0, g)),
    ]


def _ssd_fwd(name, xbc, proj, dt_col, dtb, alog, dsk, ng):
    bl, s, cd = xbc.shape
    d = dsk.shape[1]
    gw, nc = d // SSD_GROUPS, s // CHUNK
    npair = gw // LANES

    def body(xs_ref, bm_ref, cm_ref, dt_ref, z_ref, dtb_ref, alog_ref, dsk_ref, ng_ref, y_ref, hp_ref, st_ref):
        g, i = pl.program_id(0), pl.program_id(2)

        @pl.when(i == 0)
        def _():
            st_ref[...] = jnp.zeros_like(st_ref)

        states = tuple(st_ref[p] for p in range(npair))
        hp_ref[0, 0, 0] = st_ref[...]
        new_states, yn = _ssd_chunk(g, states, xs_ref[0], bm_ref[0], cm_ref[0], dt_ref[0], z_ref[0], dtb_ref[...], alog_ref[...], dsk_ref[...], ng_ref[...])
        for p in range(npair):
            st_ref[p] = new_states[p]
        y_ref[0] = yn.astype(y_ref.dtype)

    out_shape = [_sds((bl, s, d), BF16), _sds((SSD_GROUPS, bl, nc, npair, STATE, LANES), F32)]
    out_specs = [
        pl.BlockSpec((1, CHUNK, gw), lambda g, b, i: (b, i, g)),
        pl.BlockSpec((1, 1, 1, npair, STATE, LANES), lambda g, b, i: (g, b, i, 0, 0, 0)),
    ]
    return _pcall(
        body, name, out_shape, grid=(SSD_GROUPS, bl, nc), in_specs=_ssd_specs(d, gw, dt_col, False, nc), out_specs=out_specs,
        scratch=[pltpu.VMEM((npair, STATE, LANES), F32)], sem=("arbitrary", "arbitrary", "arbitrary"),
    )(xbc, xbc, xbc, proj, proj, dtb, alog, dsk, ng)


def _ssd_bwd(name, xbc, proj, dt_col, dtb, alog, dsk, ng, hprev, dy):
    bl, s, cd = xbc.shape
    d = dsk.shape[1]
    gw, nc = d // SSD_GROUPS, s // CHUNK
    npair = gw // LANES

    def body(xs_ref, bm_ref, cm_ref, dt_ref, z_ref, dtb_ref, alog_ref, dsk_ref, ng_ref, hp_ref, dy_ref,
             dxs_ref, dbm_ref, dcm_ref, ddt_ref, dz_ref, ddtb_ref, dalog_ref, ddsk_ref, dng_ref, dst_ref):
        g, b, i = pl.program_id(0), pl.program_id(1), pl.program_id(2)

        @pl.when(i == 0)
        def _():
            dst_ref[...] = jnp.zeros_like(dst_ref)

        states = tuple(hp_ref[0, 0, 0, p] for p in range(npair))
        step = functools.partial(_ssd_chunk, g)
        _, vjp = jax.vjp(step, states, xs_ref[0], bm_ref[0], cm_ref[0], dt_ref[0], z_ref[0], dtb_ref[...], alog_ref[...], dsk_ref[...], ng_ref[...])
        d_states, dxs, dbm, dcm, ddt, dz, ddtb, dalog, ddsk, dng = vjp((tuple(dst_ref[p] for p in range(npair)), dy_ref[0].astype(F32)))
        for p in range(npair):
            dst_ref[p] = d_states[p]
        dxs_ref[0], dbm_ref[0], dcm_ref[0], dz_ref[0] = dxs, dbm, dcm, dz
        ddt_ref[0, 0] = ddt
        first_g = (b == 0) & (i == 0)
        first = first_g & (g == 0)
        for o_ref, val, init in ((ddtb_ref, ddtb, first), (dalog_ref, dalog, first), (ddsk_ref, ddsk, first_g), (dng_ref, dng, first_g)):
            @pl.when(init)
            def _(o_ref=o_ref, val=val):
                o_ref[...] = val

            @pl.when(jnp.logical_not(init))
            def _(o_ref=o_ref, val=val):
                o_ref[...] += val

    rc = lambda i: nc - 1 - i
    in_specs = _ssd_specs(d, gw, dt_col, True, nc) + [
        pl.BlockSpec((1, 1, 1, npair, STATE, LANES), lambda g, b, i: (g, b, rc(i), 0, 0, 0)),
        pl.BlockSpec((1, CHUNK, gw), lambda g, b, i: (b, rc(i), g)),
    ]
    out_shape = [
        _sds((bl, s, d), F32), _sds((bl, s, SSD_GROUPS * STATE), F32), _sds((bl, s, SSD_GROUPS * STATE), F32),
        _sds((SSD_GROUPS, bl, s, LANES), F32), _sds((bl, s, d), F32),
        _sds((1, LANES), F32), _sds((1, LANES), F32), _sds((1, d), F32), _sds((1, d), F32),
    ]
    out_specs = [
        pl.BlockSpec((1, CHUNK, gw), lambda g, b, i: (b, rc(i), g)),
        pl.BlockSpec((1, CHUNK, STATE), lambda g, b, i: (b, rc(i), g)), pl.BlockSpec((1, CHUNK, STATE), lambda g, b, i: (b, rc(i), g)),
        pl.BlockSpec((1, 1, CHUNK, LANES), lambda g, b, i: (g, b, rc(i), 0)),
        pl.BlockSpec((1, CHUNK, gw), lambda g, b, i: (b, rc(i), g)),
        pl.BlockSpec((1, LANES), lambda g, b, i: (0, 0)), pl.BlockSpec((1, LANES), lambda g, b, i: (0, 0)),
        pl.BlockSpec((1, gw), lambda g, b, i: (0, g)), pl.BlockSpec((1, gw), lambda g, b, i: (0, g)),
    ]
    return _pcall(
        body, name, out_shape, grid=(SSD_GROUPS, bl, nc), in_specs=in_specs, out_specs=out_specs,
        scratch=[pltpu.VMEM((npair, STATE, LANES), F32)], sem=("arbitrary", "arbitrary", "arbitrary"),
    )(xbc, xbc, xbc, proj, proj, dtb, alog, dsk, ng, hprev, dy)


def _make_mixer(name, d, cd, n_proj):
    xbc_col, dt_col = 3 * d, 3 * d + cd

    @jax.custom_vjp
    def mixer(proj, conv_w8, conv_b, dtb, alog, dsk, ng, vg, ws, bs_t, og):
        return fwd(proj, conv_w8, conv_b, dtb, alog, dsk, ng, vg, ws, bs_t, og)[0]

    def fwd(proj, conv_w8, conv_b, dtb, alog, dsk, ng, vg, ws, bs_t, og):
        xbc = _conv_act_fwd(name + "_conv_fwd", proj, xbc_col, cd, conv_w8, conv_b, SSD_CONV, None, F32)
        y, hprev = _ssd_fwd(name + "_ssd_fwd", xbc, proj, dt_col, dtb, alog, dsk, ng)
        rows = [(proj, d, 1, True), (proj, d, 2, True)]
        (g_out,) = _row_fwd(name + "_gm_fwd", _gm_tile, rows, [], [vg, ws, bs_t, og], [(d, BF16)], 0, CHUNK)
        ycat = jnp.concatenate([y, g_out], axis=-1)
        return ycat, (proj, conv_w8, conv_b, dtb, alog, dsk, ng, vg, ws, bs_t, og, xbc, hprev)

    def bwd(res, dycat):
        proj, conv_w8, conv_b, dtb, alog, dsk, ng, vg, ws, bs_t, og, xbc, hprev = res
        bl, s, _ = proj.shape
        dy, dg_out = dycat[..., :d], dycat[..., d:]
        dxs, dbm, dcm, ddt2, dz, ddtb, dalog, ddsk, dng = _ssd_bwd(name + "_ssd_bwd", xbc, proj, dt_col, dtb, alog, dsk, ng, hprev, dy)
        dxbc_act = jnp.concatenate([dxs, dbm, dcm], axis=-1)
        dxbc, dw8, dcb = _conv_act_bwd(name + "_conv_bwd", proj, xbc_col, cd, conv_w8, conv_b, SSD_CONV, None, dxbc_act)
        rows = [(proj, d, 1, True), (proj, d, 2, True)]
        (du, dv), _, (dvg, dws, dbs_t, dog) = _row_bwd(name + "_gm_bwd", _gm_tile, rows, [], [vg, ws, bs_t, og], [(d, BF16)], 0, CHUNK, [dg_out])
        ddt = ddt2[0] + ddt2[1]
        pad = jnp.zeros((bl, s, n_proj - dt_col - LANES), F32)
        dproj = jnp.concatenate([dz, du, dv, dxbc, ddt, pad], axis=-1)
        return dproj, dw8, dcb, ddtb, dalog, ddsk, dng, dvg, dws, dbs_t, dog

    mixer.defvjp(fwd, bwd)
    return mixer


def _make_ffn_act(name, f):
    @jax.custom_vjp
    def act(up, w8, b):
        return _conv_act_fwd(name + "_fwd", up, 0, f, w8, b, FF_CONV, f, BF16)

    def fwd(up, w8, b):
        return act(up, w8, b), (up, w8, b)

    def bwd(res, dy):
        up, w8, b = res
        return tuple(_conv_act_bwd(name + "_bwd", up, 0, f, w8, b, FF_CONV, f, dy))

    act.defvjp(fwd, bwd)
    return act


def _position():
    return lax.axis_index("x"), lax.axis_index("y"), lax.axis_index("c")


def _at(ref, idx):
    return ref.at[idx] if len(idx) else ref


def _exchange(name, inp, out_shape, plan):
    x0, y0, c0 = 0, 0, 0
    n_local, n_remote = (len(t) for t in plan(x0, y0, c0))

    def body(in_ref, out_ref, send_sems, recv_sems, local_sems):
        x, y, c = _position()
        local, remote = plan(x, y, c)
        locals_ = [pltpu.make_async_copy(_at(in_ref, s), _at(out_ref, t), local_sems.at[k]) for k, (s, t) in enumerate(local)]
        for cp in locals_:
            cp.start()
        sends = [
            pltpu.make_async_remote_copy(src_ref=_at(in_ref, s), dst_ref=_at(out_ref, t), send_sem=send_sems.at[k], recv_sem=recv_sems.at[k], device_id=peer, device_id_type=MESH)
            for k, (s, t, peer, _) in enumerate(remote)
        ]
        for cp in sends:
            cp.start()
        for k, (s, _, peer, land) in enumerate(remote):
            pltpu.make_async_remote_copy(src_ref=_at(in_ref, s), dst_ref=_at(out_ref, land), send_sem=send_sems.at[k], recv_sem=recv_sems.at[k], device_id=peer, device_id_type=MESH).wait_recv()
        for cp in sends:
            cp.wait_send()
        for cp in locals_:
            cp.wait()

    return pl.pallas_call(
        body, name=name, out_shape=_sds(out_shape, inp.dtype),
        in_specs=[pl.BlockSpec(memory_space=pl.ANY)], out_specs=pl.BlockSpec(memory_space=pl.ANY),
        scratch_shapes=[pltpu.SemaphoreType.DMA((n_remote,)), pltpu.SemaphoreType.DMA((n_remote,)), pltpu.SemaphoreType.DMA((max(n_local, 1),))],
    )(inp)


def _chip_peers(x, y, c):
    return [(1 - x, y, c), (x, 1 - y, c), (1 - x, 1 - y, c)]


def _gather_chips(name, blk):
    def plan(x, y, c):
        me = 2 * x + y
        return [((), (me,))], [((), (me,), p, (2 * p[0] + p[1],)) for p in _chip_peers(x, y, c)]

    return _exchange(name, blk, (4,) + blk.shape, plan)


def _gather_cores(name, blk4):
    n = blk4.shape[0]

    def plan(x, y, c):
        return [((s,), (s, c)) for s in range(n)], [((s,), (s, c), (x, y, 1 - c), (s, 1 - c)) for s in range(n)]

    return _exchange(name, blk4, (n, 2) + blk4.shape[1:], plan)


def _gather_all(name, blk):
    g = _gather_cores(name + "_cores", _gather_chips(name + "_chips", blk))
    return g.reshape((8,) + blk.shape)


def _swap_core_halves(name, g42):
    def plan(x, y, c):
        return [], [((s, 1 - c), (s,), (x, y, 1 - c), (s,)) for s in range(4)]

    return _exchange(name, g42, (4,) + g42.shape[2:], plan)


def _scatter_chips(name, p4):
    def plan(x, y, c):
        me = 2 * x + y
        return [((me,), (me,))], [((2 * p[0] + p[1],), (me,), p, (2 * p[0] + p[1],)) for p in _chip_peers(x, y, c)]

    return _exchange(name, p4, p4.shape, plan)


def _pair_add(name, g42, r4):
    _, _, rh, cols = g42.shape
    tr = _divisors(rh, 512, SUBLANES * 2)[-1]

    def body(c_ref, a_ref, b_ref, o_ref):
        o_ref[0] = (a_ref[0, 0].astype(F32) + b_ref[0].astype(F32)).astype(o_ref.dtype)

    cidx = lax.axis_index("c").astype(jnp.int32).reshape(1)
    return _pcall(
        body, name, _sds(r4.shape, BF16), grid=(4, rh // tr),
        in_specs=[pl.BlockSpec((1, 1, tr, cols), lambda s, i, c_ref: (s, c_ref[0], i, 0)), pl.BlockSpec((1, tr, cols), lambda s, i, c_ref: (s, i, 0))],
        out_specs=pl.BlockSpec((1, tr, cols), lambda s, i, c_ref: (s, i, 0)), sem=("parallel", "parallel"), prefetch=1,
    )(cidx, g42, r4)


def _slot_sum(name, parts):
    n, r, cols = parts.shape
    cap = max(2 * SUBLANES, (4 * 1024 * 1024) // (n * cols * parts.dtype.itemsize))
    tr = _divisors(r, cap, 2 * SUBLANES)[-1]

    def body(p_ref, o_ref):
        acc = p_ref[0].astype(F32)
        for k in range(1, n):
            acc = acc + p_ref[k].astype(F32)
        o_ref[...] = acc

    return _pcall(
        body, name, _sds((r, cols), F32), grid=(r // tr,), in_specs=[pl.BlockSpec((n, tr, cols), lambda i: (0, i, 0))],
        out_specs=pl.BlockSpec((tr, cols), lambda i: (i, 0)), sem=("parallel",),
    )(parts)


def _cast_half(name, shard):
    r, cols = shard.shape
    rh = r // 2
    tr = _divisors(rh, 512, 2 * SUBLANES)[-1]
    nblk = rh // tr

    def body(c_ref, a_ref, o_ref):
        o_ref[...] = a_ref[...].astype(o_ref.dtype)

    cidx = lax.axis_index("c").astype(jnp.int32).reshape(1)
    return _pcall(
        body, name, _sds((rh, cols), BF16), grid=(nblk,), in_specs=[pl.BlockSpec((tr, cols), lambda i, c_ref: (c_ref[0] * nblk + i, 0))],
        out_specs=pl.BlockSpec((tr, cols), lambda i, c_ref: (i, 0)), sem=("parallel",), prefetch=1,
    )(cidx, shard)


def _make_gather_pack(name):
    @jax.custom_vjp
    def gather(shard):
        r, cols = shard.shape
        half = _cast_half(name + "_cast", shard)
        full = _gather_cores(name + "_ag_cores", _gather_chips(name + "_ag_chips", half))
        return full.reshape(4, r, cols)

    def fwd(shard):
        return gather(shard), None

    def bwd(_, g):
        _, r, cols = g.shape
        g42 = g.reshape(4, 2, r // 2, cols)
        other = _swap_core_halves(name + "_rs_cores", g42)
        pair = _pair_add(name + "_rs_pair", g42, other)
        parts = _scatter_chips(name + "_rs_chips", pair)
        mine = _slot_sum(name + "_rs_sum", parts)
        both = _exchange(
            name + "_rs_share", mine, (2,) + mine.shape,
            lambda x, y, c: ([((), (c,))], [((), (c,), (x, y, 1 - c), (1 - c,))]),
        )
        return (both.reshape(r, cols),)

    gather.defvjp(fwd, bwd)
    return gather


def _make_replicated(name):
    @jax.custom_vjp
    def rep(p):
        return p

    def bwd(_, g):
        return (_slot_sum(name + "_sum", _gather_all(name + "_ag", g)),)

    rep.defvjp(lambda p: (p, None), bwd)
    return rep


def _ada_fwd_call(name, c_all, w, b_shard):
    nl, d, ns = w.shape
    nb = c_all.shape[0]

    def body(c_ref, w_ref, b_ref, o_ref):
        cv = c_ref[...]
        o_ref[0] = _dg(cv * _sigmoid(cv), w_ref[0], ((1,), (0,))) + b_ref[0]

    return _pcall(
        body, name, _sds((nl, nb, ns), F32), grid=(nl,),
        in_specs=[pl.BlockSpec((nb, d), lambda l: (0, 0)), pl.BlockSpec((1, d, ns), lambda l: (l, 0, 0)), pl.BlockSpec((1, 1, ns), lambda l: (l, 0, 0))],
        out_specs=pl.BlockSpec((1, nb, ns), lambda l: (l, 0, 0)), sem=("parallel",),
    )(c_all, w, b_shard)


def _ada_bwd_call(name, c_all, dm_shard, dm_all):
    nl, nb, ns = dm_shard.shape
    d = c_all.shape[1]
    nm = dm_all.shape[2]

    def body(c_ref, ds_ref, da_ref, dw_ref, db_ref):
        cv = c_ref[...]
        dw_ref[0] = _dg(cv * _sigmoid(cv), ds_ref[0], ((0,), (0,)))
        db_ref[0] = jnp.sum(da_ref[0], axis=0, keepdims=True)

    return _pcall(
        body, name, [_sds((nl, d, ns), F32), _sds((nl, 1, nm), F32)], grid=(nl,),
        in_specs=[pl.BlockSpec((nb, d), lambda l: (0, 0)), pl.BlockSpec((1, nb, ns), lambda l: (l, 0, 0)), pl.BlockSpec((1, nb, nm), lambda l: (l, 0, 0))],
        out_specs=[pl.BlockSpec((1, d, ns), lambda l: (l, 0, 0)), pl.BlockSpec((1, 1, nm), lambda l: (l, 0, 0))], sem=("parallel",),
    )(c_all, dm_shard, dm_all)


def _make_ada(name, bl):
    def my_chip():
        return 2 * lax.axis_index("x") + lax.axis_index("y")

    def my_dev():
        return 2 * my_chip() + lax.axis_index("c")

    @jax.custom_vjp
    def ada(c_all, w, b):
        nl, d, ns = w.shape
        b_shard = lax.dynamic_slice(b, (0, my_chip() * ns), (nl, ns)).reshape(nl, 1, ns)
        shard = _ada_fwd_call(name + "_fwd", c_all, w, b_shard)
        allc = _gather_chips(name + "_ag", shard)
        mods = jnp.transpose(allc, (1, 2, 0, 3)).reshape(nl, c_all.shape[0], 4 * ns)
        return lax.dynamic_slice(mods, (0, my_dev() * bl, 0), (nl, bl, 4 * ns))

    def fwd(c_all, w, b):
        return ada(c_all, w, b), (c_all, w.shape)

    def bwd(res, dm):
        c_all, (nl, d, ns) = res
        dm_all = _gather_all(name + "_bwd_ag", dm)
        dm_all = jnp.transpose(dm_all, (1, 0, 2, 3)).reshape(nl, 8 * bl, 4 * ns)
        dm_shard = lax.dynamic_slice(dm_all, (0, 0, my_chip() * ns), (nl, 8 * bl, ns))
        dw, db = _ada_bwd_call(name + "_bwd", c_all, dm_shard, dm_all)
        return jnp.zeros_like(c_all), dw, db.reshape(nl, 4 * ns)

    ada.defvjp(fwd, bwd)
    return ada


def _adamw(name, w, g, m, v):
    shape = w.shape
    cols = shape[-1]
    w2, g2, m2, v2 = (t.reshape(-1, cols) for t in (w, g, m, v))
    rows = w2.shape[0]
    cap = max(SUBLANES, (512 * 1024) // max(cols, 1) // SUBLANES * SUBLANES)
    tr = _divisors(rows, cap, SUBLANES)[-1]

    def body(w_ref, g_ref, m_ref, v_ref, d_ref, mo_ref, vo_ref):
        gv = g_ref[...]
        mn = ADAM_B1 * m_ref[...] + (1.0 - ADAM_B1) * gv
        vn = ADAM_B2 * v_ref[...] + (1.0 - ADAM_B2) * (gv * gv)
        m_hat = mn / (1.0 - ADAM_B1 ** ADAM_STEP)
        v_hat = vn / (1.0 - ADAM_B2 ** ADAM_STEP)
        d_ref[...] = -ADAM_LR * (m_hat / (jnp.sqrt(v_hat) + ADAM_EPS) + ADAM_WD * w_ref[...])
        mo_ref[...] = mn
        vo_ref[...] = vn

    spec = pl.BlockSpec((tr, cols), lambda i: (i, 0))
    outs = _pcall(body, name, [_sds((rows, cols), F32)] * 3, grid=(rows // tr,), in_specs=[spec] * 4, out_specs=[spec] * 3, sem=("parallel",))(w2, g2, m2, v2)
    return tuple(o.reshape(shape) for o in outs)


def _flat_pack(arrs, row_mult):
    parts, offs, r = [], [], 0
    for a in arrs:
        n = a.size
        rows = -(-n // PACK_COLS)
        parts.append(jnp.pad(a.reshape(-1), (0, rows * PACK_COLS - n)))
        offs.append((r, rows, a.shape))
        r += rows
    total = _round_up(r, row_mult)
    if total > r:
        parts.append(jnp.zeros(((total - r) * PACK_COLS,), arrs[0].dtype))
    return jnp.concatenate(parts).reshape(total, PACK_COLS), offs


def _flat_unpack(pack, offs):
    out = []
    for r0, rows, shape in offs:
        n = 1
        for t in shape:
            n *= t
        out.append(pack[r0:r0 + rows].reshape(-1)[:n].reshape(shape))
    return out


def _pad_rows(w, rows):
    return jnp.pad(w, ((0, rows - w.shape[0]), (0, 0)))


def _pad_lanes(v):
    return jnp.pad(v, (0, LANES - v.shape[0])).reshape(1, LANES)


BIG = ("w_in", "w_out", "ff_up", "ff_down")
BIG_AXIS = {"w_in": 1, "w_out": 0, "ff_up": 1, "ff_down": 0}
CONVW = ("ssd_conv_w", "ff_conv_w")
SMALL = ("norm1_g", "norm2_g", "ssd_conv_b", "ssd_dt_bias", "ssd_a_log", "ssd_d", "ssd_norm_g", "gm_vnorm_g", "gm_ws", "gm_bs", "gm_out_g", "ff_conv_b")
WEIGHTS = ("ada_w", "ada_b", "norm1_g", "norm2_g", "w_in", "ssd_conv_w", "ssd_conv_b", "ssd_dt_bias", "ssd_a_log", "ssd_d", "ssd_norm_g", "gm_vnorm_g", "gm_ws", "gm_bs", "gm_out_g", "w_out", "ff_up", "ff_conv_w", "ff_conv_b", "ff_down", "final_g")


def kernel(x, c, ada_w, ada_b, norm1_g, norm2_g, w_in, ssd_conv_w, ssd_conv_b, ssd_dt_bias, ssd_a_log, ssd_d, ssd_norm_g, gm_vnorm_g, gm_ws, gm_bs, gm_out_g, w_out, ff_up, ff_conv_w, ff_conv_b, ff_down, final_g, loss_target, m_ada_w, m_ada_b, m_norm1_g, m_norm2_g, m_w_in, m_ssd_conv_w, m_ssd_conv_b, m_ssd_dt_bias, m_ssd_a_log, m_ssd_d, m_ssd_norm_g, m_gm_vnorm_g, m_gm_ws, m_gm_bs, m_gm_out_g, m_w_out, m_ff_up, m_ff_conv_w, m_ff_conv_b, m_ff_down, m_final_g, v_ada_w, v_ada_b, v_norm1_g, v_norm2_g, v_w_in, v_ssd_conv_w, v_ssd_conv_b, v_ssd_dt_bias, v_ssd_a_log, v_ssd_d, v_ssd_norm_g, v_gm_vnorm_g, v_gm_ws, v_gm_bs, v_gm_out_g, v_w_out, v_ff_up, v_ff_conv_w, v_ff_conv_b, v_ff_down, v_final_g):
    given = dict(locals())
    weights = {n: given[n] for n in WEIGHTS}
    bl, s, d = x.shape
    nl = ada_w.shape[0]
    heads = d // HEAD_DIM
    cd = d + 2 * SSD_GROUPS * STATE
    f = ff_down.shape[1] * 4
    n_in = d + cd + heads + 2 * d
    n_proj = _round_up(3 * d + cd + LANES, 2 * LANES)
    tm = _divisors(s, 512)[-1]

    pre_pack, pre_offs = _flat_pack([c] + [weights[n] for n in CONVW], 2 * SUBLANES)
    pre_all = _gather_all("pre_ag", pre_pack)
    pre_vals = [_flat_unpack(pre_all[k], pre_offs) for k in range(8)]
    c_all = jnp.concatenate([pre_vals[k][0] for k in range(8)], axis=0)
    conv_full = {n: jnp.concatenate([pre_vals[2 * k][1 + j] for k in range(4)], axis=-1) for j, n in enumerate(CONVW)}
    ada = _make_ada("ada", bl)
    replicated = _make_replicated("small")

    big_packs, big_offs = [], None
    for l in range(nl):
        pack, big_offs = _flat_pack([weights[n][l] for n in BIG], 4 * SUBLANES)
        big_packs.append(pack)
    small_list = [weights[n][l] for l in range(nl) for n in SMALL] + [conv_full[n][l] for l in range(nl) for n in CONVW] + [final_g]
    small_pack, small_offs = _flat_pack(small_list, 2 * SUBLANES)

    def local_loss(params, x_in):
        ada_w_, ada_b_, packs, small = params
        mods = ada(c_all, ada_w_, ada_b_).reshape(nl, bl, N_MOD, 1, d)
        small_vals = _flat_unpack(replicated(small), small_offs)
        fg = small_vals[-1]
        xcur, pending = x_in, None
        loss = None
        for l in range(nl):
            sp = dict(zip(SMALL, small_vals[l * len(SMALL):(l + 1) * len(SMALL)]))
            sp.update(zip(CONVW, small_vals[nl * len(SMALL) + l * len(CONVW):nl * len(SMALL) + (l + 1) * len(CONVW)]))
            full = _make_gather_pack(f"wg{l}")(packs[l])
            shards = [_flat_unpack(full[k], big_offs) for k in range(4)]
            wfull = {n: jnp.concatenate([shards[k][j] for k in range(4)], axis=BIG_AXIS[n]) for j, n in enumerate(BIG)}
            wi = wfull["w_in"]
            w_in_p = jnp.concatenate(
                [wi[:, :d], wi[:, d + cd + heads:], wi[:, d:d + cd], wi[:, d + cd:d + cd + heads], jnp.zeros((d, n_proj - (3 * d + cd + heads)), BF16)], axis=1)
            sh1, sc1, g1, sh2, sc2, g2 = (mods[l, :, k] for k in range(N_MOD))
            r2 = lambda v: v.reshape(1, -1)
            if pending is None:
                (h,) = _make_row_op(f"nm{l}", _nm_tile, [(d, 0, True)], [(d, BF16)], 0, tm)((xcur,), (sc1, sh1), (r2(sp["norm1_g"]),))
            else:
                xcur, h = _make_row_op(f"rnm{l}a", _rnm_tile, [(d, 0, True), (d, 0, True)], [(d, F32), (d, BF16)], 0, tm)(
                    (xcur, pending[0]), (pending[1], sc1, sh1), (r2(sp["norm1_g"]),))
            proj = _make_mm(f"win{l}", F32)(h, w_in_p)
            ycat = _make_mixer(f"mix{l}", d, cd, n_proj)(
                proj, _pad_rows(sp["ssd_conv_w"], SUBLANES), r2(sp["ssd_conv_b"]), _pad_lanes(sp["ssd_dt_bias"]), _pad_lanes(sp["ssd_a_log"]),
                r2(jnp.repeat(sp["ssd_d"], HEAD_DIM)), r2(sp["ssd_norm_g"]), r2(sp["gm_vnorm_g"]), sp["gm_ws"],
                jnp.pad(sp["gm_bs"].T, ((0, 0), (0, LANES - sp["gm_bs"].shape[0]))), r2(sp["gm_out_g"]))
            mix = _make_mm(f"wout{l}", F32)(ycat, wfull["w_out"])
            xcur, h2 = _make_row_op(f"rnm{l}b", _rnm_tile, [(d, 0, True), (d, 0, True)], [(d, F32), (d, BF16)], 0, tm)(
                (xcur, mix), (g1, sc2, sh2), (r2(sp["norm2_g"]),))
            up = _make_mm(f"ffup{l}", F32)(h2, wfull["ff_up"])
            act = _make_ffn_act(f"ffact{l}", f)(up, _pad_rows(sp["ff_conv_w"], SUBLANES), r2(sp["ff_conv_b"]))
            down = _make_mm(f"ffdown{l}", F32)(act, wfull["ff_down"])
            pending = (down, g2)
        (loss,) = _make_row_op("final", _final_tile, [(d, 0, True), (d, 0, True), (d, 0, False)], [], 1, tm)(
            (xcur, pending[0], loss_target), (pending[1],), (fg.reshape(1, -1),))
        return loss[0, 0]

    params = (ada_w, ada_b, big_packs, small_pack)
    loss_local, (grads, grad_x) = jax.value_and_grad(local_loss, argnums=(0, 1))(params, x)
    loss = lax.psum(loss_local, AXES)
    g_ada_w, g_ada_b, g_packs, g_small = grads

    grad = {"ada_w": g_ada_w, "ada_b": g_ada_b}
    per_layer = [_flat_unpack(g_packs[l], big_offs) for l in range(nl)]
    for j, n in enumerate(BIG):
        grad[n] = jnp.stack([per_layer[l][j] for l in range(nl)])
    g_small_vals = _flat_unpack(g_small, small_offs)
    for j, n in enumerate(SMALL):
        grad[n] = jnp.stack([g_small_vals[l * len(SMALL) + j] for l in range(nl)])
    grad["final_g"] = g_small_vals[-1]
    chip = 2 * lax.axis_index("x") + lax.axis_index("y")
    for j, n in enumerate(CONVW):
        g_full = jnp.stack([g_small_vals[nl * len(SMALL) + l * len(CONVW) + j] for l in range(nl)])
        width = weights[n].shape[-1]
        grad[n] = lax.dynamic_slice_in_dim(g_full, chip * width, width, axis=2)

    delta, new_m, new_v = {}, {}, {}
    for n in ("ada_w",) + BIG:
        delta[n], new_m[n], new_v[n] = _adamw("adam_" + n, weights[n], grad[n], given["m_" + n], given["v_" + n])
    small_names = ("ada_b",) + SMALL + CONVW + ("final_g",)
    packed = [_flat_pack([t[n] for n in small_names], SUBLANES)[0] for t in (weights, grad, {n: given["m_" + n] for n in small_names}, {n: given["v_" + n] for n in small_names})]
    offs = _flat_pack([weights[n] for n in small_names], SUBLANES)[1]
    for res, out in zip(_adamw("adam_small", *packed), (delta, new_m, new_v)):
        for n, val in zip(small_names, _flat_unpack(res, offs)):
            out[n] = val

    return (loss, grad_x, *[grad[n] for n in WEIGHTS], *[delta[n] for n in WEIGHTS], *[new_m[n] for n in WEIGHTS], *[new_v[n] for n in WEIGHTS])
```

```python
import functools

import jax
import jax.numpy as jnp
from jax import lax
from jax.experimental import pallas as pl
from jax.experimental.pallas import tpu as pltpu

F32 = jnp.float32
BF16 = jnp.bfloat16
EPS = 1e-6
CHUNK = 128
HEAD_DIM = 64
STATE = 128
GM_HEAD = 128
SSD_GROUPS = 2
SSD_CONV = 4
FF_CONV = 3
N_MOD = 6
LANES = 128
SUBLANES = 8
V7X_VMEM_LIMIT = 48 * 1024 * 1024
MM_VMEM_BUDGET = 30 * 1024 * 1024
ADAM_LR, ADAM_B1, ADAM_B2, ADAM_EPS, ADAM_WD, ADAM_STEP = 0.001, 0.9, 0.999, 1e-08, 0.01, 10
MESH = pl.DeviceIdType.MESH
AXES = ("x", "y", "c")


def _round_up(n, m):
    return (n + m - 1) // m * m


def _divisors(n, cap, mult=LANES):
    out = [t for t in range(mult, min(n, cap) + 1, mult) if n % t == 0]
    return out or [n]


def _pcall(body, name, out_shape, grid=(), in_specs=None, out_specs=None, scratch=(), sem=None, prefetch=0):
    params = pltpu.CompilerParams(dimension_semantics=sem, vmem_limit_bytes=V7X_VMEM_LIMIT)
    if prefetch:
        spec = pltpu.PrefetchScalarGridSpec(num_scalar_prefetch=prefetch, grid=grid, in_specs=in_specs, out_specs=out_specs, scratch_shapes=list(scratch))
        return pl.pallas_call(body, name=name, out_shape=out_shape, grid_spec=spec, compiler_params=params)
    if in_specs is None:
        return pl.pallas_call(body, name=name, out_shape=out_shape, compiler_params=params)
    return pl.pallas_call(body, name=name, out_shape=out_shape, grid=grid, in_specs=in_specs, out_specs=out_specs, scratch_shapes=list(scratch), compiler_params=params)


def _sds(shape, dtype):
    return jax.ShapeDtypeStruct(tuple(shape), dtype)


def _mm_tiles(m, n, k, a_bytes, b_bytes, o_bytes):
    best, best_key = None, None
    for tm in _divisors(m, 1024):
        for tn in _divisors(n, 2560):
            for tk in _divisors(k, 2560):
                vmem = 2 * (tm * tk * a_bytes + tk * tn * b_bytes + tm * tn * o_bytes) + tm * tn * 4
                if vmem > MM_VMEM_BUDGET or tm * tn * tk > 512 * 1024 * 2560:
                    continue
                key = (tm * tn * tk, tm * tn, tm)
                if best_key is None or key > best_key:
                    best, best_key = (tm, tn, tk), key
    return best


def _matmul(name, a, b, mode, out_dtype):
    if mode == "nn":
        (m, k), n = a.shape, b.shape[1]
    elif mode == "nt":
        (m, k), n = a.shape, b.shape[0]
    else:
        (k, m), n = a.shape, b.shape[1]
    tm, tn, tk = _mm_tiles(m, n, k, a.dtype.itemsize, b.dtype.itemsize, jnp.dtype(out_dtype).itemsize)
    nk = k // tk
    if mode == "nn":
        a_spec = pl.BlockSpec((tm, tk), lambda i, j, kk: (i, kk))
        b_spec = pl.BlockSpec((tk, tn), lambda i, j, kk: (kk, j))
        dims = ((1,), (0,))
    elif mode == "nt":
        a_spec = pl.BlockSpec((tm, tk), lambda i, j, kk: (i, kk))
        b_spec = pl.BlockSpec((tn, tk), lambda i, j, kk: (j, kk))
        dims = ((1,), (1,))
    else:
        a_spec = pl.BlockSpec((tk, tm), lambda i, j, kk: (kk, i))
        b_spec = pl.BlockSpec((tk, tn), lambda i, j, kk: (kk, j))
        dims = ((0,), (0,))

    def body(a_ref, b_ref, o_ref, acc_ref):
        kk = pl.program_id(2)
        p = lax.dot_general(a_ref[...].astype(BF16), b_ref[...].astype(BF16), (dims, ((), ())), preferred_element_type=F32)
        if nk == 1:
            o_ref[...] = p.astype(o_ref.dtype)
        else:
            @pl.when(kk == 0)
            def _():
                acc_ref[...] = p

            @pl.when(kk > 0)
            def _():
                acc_ref[...] += p

            @pl.when(kk == nk - 1)
            def _():
                o_ref[...] = acc_ref[...].astype(o_ref.dtype)

    return _pcall(
        body, name, _sds((m, n), out_dtype), grid=(m // tm, n // tn, nk), in_specs=[a_spec, b_spec],
        out_specs=pl.BlockSpec((tm, tn), lambda i, j, kk: (i, j)), scratch=[pltpu.VMEM((tm, tn), F32)],
        sem=("parallel", "parallel", "arbitrary"),
    )(a, b)


def _make_mm(name, out_dtype):
    @jax.custom_vjp
    def mm(a, w):
        return _matmul(name + "_fwd", a.reshape(-1, a.shape[-1]), w, "nn", out_dtype).reshape(a.shape[:-1] + (w.shape[1],))

    def fwd(a, w):
        return mm(a, w), (a, w)

    def bwd(res, dy):
        a, w = res
        a2, dy2 = a.reshape(-1, a.shape[-1]), dy.reshape(-1, dy.shape[-1])
        da = _matmul(name + "_dx", dy2, w, "nt", a.dtype).reshape(a.shape)
        dw = _matmul(name + "_dw", a2, dy2, "tn", w.dtype)
        return da, dw

    mm.defvjp(fwd, bwd)
    return mm


def _dg(a, b, dims):
    return lax.dot_general(a.astype(BF16), b.astype(BF16), (dims, ((), ())), preferred_element_type=F32)


@jax.custom_vjp
def _dot_nn(a, b):
    return _dg(a, b, ((1,), (0,)))


_dot_nn.defvjp(lambda a, b: (_dot_nn(a, b), (a, b)), lambda r, d: (_dg(d, r[1], ((1,), (1,))), _dg(r[0], d, ((0,), (0,)))))


@jax.custom_vjp
def _dot_nt(a, b):
    return _dg(a, b, ((1,), (1,)))


_dot_nt.defvjp(lambda a, b: (_dot_nt(a, b), (a, b)), lambda r, d: (_dg(d, r[1], ((1,), (0,))), _dg(d, r[0], ((0,), (0,)))))


@jax.custom_vjp
def _dot_tn(a, b):
    return _dg(a, b, ((0,), (0,)))


_dot_tn.defvjp(lambda a, b: (_dot_tn(a, b), (a, b)), lambda r, d: (_dg(r[1], d, ((1,), (1,))), _dg(r[0], d, ((1,), (0,)))))


def _exact_dot(a, c, dims):
    hi = a.astype(BF16)
    r1 = a - hi.astype(F32)
    mid = r1.astype(BF16)
    lo = (r1 - mid.astype(F32)).astype(BF16)
    cb = c.astype(BF16)
    f = lambda t: lax.dot_general(t, cb, (dims, ((), ())), preferred_element_type=F32)
    return f(hi) + f(mid) + f(lo)


@jax.custom_vjp
def _sel_right(a, c):
    return _exact_dot(a, c, ((1,), (0,)))


_sel_right.defvjp(lambda a, c: (_sel_right(a, c), c), lambda c, d: (_exact_dot(d, c, ((1,), (1,))), jnp.zeros_like(c)))


def _exact_dot_left(c, a, dims):
    hi = a.astype(BF16)
    r1 = a - hi.astype(F32)
    mid = r1.astype(BF16)
    lo = (r1 - mid.astype(F32)).astype(BF16)
    cb = c.astype(BF16)
    f = lambda t: lax.dot_general(cb, t, (dims, ((), ())), preferred_element_type=F32)
    return f(hi) + f(mid) + f(lo)


@jax.custom_vjp
def _sel_left(c, a):
    return _exact_dot_left(c, a, ((1,), (0,)))


_sel_left.defvjp(lambda c, a: (_sel_left(c, a), c), lambda c, d: (jnp.zeros_like(c), _exact_dot_left(c, d, ((0,), (0,)))))


def _sigmoid(x):
    return 1.0 / (1.0 + jnp.exp(-x))


def _rms(x, g):
    return x * lax.rsqrt(jnp.mean(x * x, axis=-1, keepdims=True) + EPS) * g


def _gelu(x):
    return 0.5 * x * (1.0 + lax.erf(x * (2.0 ** -0.5)))


def _causal(n):
    return lax.broadcasted_iota(jnp.int32, (n, n), 0) >= lax.broadcasted_iota(jnp.int32, (n, n), 1)


def _row_in_specs(rows, bparams, gparams, tm):
    specs = [pl.BlockSpec((1, tm, w), lambda b, i, cb=cb: (b, i, cb)) for (_, w, cb, _) in rows]
    specs += [pl.BlockSpec((1, 1, p.shape[-1]), lambda b, i: (b, 0, 0)) for p in bparams]
    specs += [pl.BlockSpec(p.shape, lambda b, i, n=p.ndim: (0,) * n) for p in gparams]
    return specs


def _row_vals(refs, n_rows, n_b, n_g):
    vals = [r[0].astype(F32) for r in refs[:n_rows]]
    vals += [r[0].astype(F32) for r in refs[n_rows:n_rows + n_b]]
    vals += [r[...].astype(F32) for r in refs[n_rows + n_b:n_rows + n_b + n_g]]
    return vals


def _row_fwd(name, tile, rows, bparams, gparams, outs, n_sum, tm):
    bl, s = rows[0][0].shape[:2]
    n_in = len(rows) + len(bparams) + len(gparams)

    def body(*refs):
        first = (pl.program_id(0) == 0) & (pl.program_id(1) == 0)
        res = tile(*_row_vals(refs, len(rows), len(bparams), len(gparams)))
        o_refs = refs[n_in:]
        for k in range(len(outs)):
            o_refs[k][0] = res[k].astype(o_refs[k].dtype)
        for k in range(n_sum):
            o_ref, val = o_refs[len(outs) + k], res[len(outs) + k]

            @pl.when(first)
            def _(o_ref=o_ref, val=val):
                o_ref[...] = val

            @pl.when(jnp.logical_not(first))
            def _(o_ref=o_ref, val=val):
                o_ref[...] += val

    out_shape = [_sds((bl, s, w), dt) for (w, dt) in outs] + [_sds((1, 1), F32)] * n_sum
    out_specs = [pl.BlockSpec((1, tm, w), lambda b, i: (b, i, 0)) for (w, _) in outs] + [pl.BlockSpec((1, 1), lambda b, i: (0, 0))] * n_sum
    return _pcall(
        body, name, out_shape, grid=(bl, s // tm), in_specs=_row_in_specs(rows, bparams, gparams, tm), out_specs=out_specs,
        sem=("arbitrary", "arbitrary"),
    )(*[r[0] for r in rows], *bparams, *gparams)


def _row_bwd(name, tile, rows, bparams, gparams, outs, n_sum, tm, cts):
    bl, s = rows[0][0].shape[:2]
    n_r, n_b, n_g = len(rows), len(bparams), len(gparams)
    n_in = n_r + n_b + n_g
    n_ct = len(outs) + n_sum
    grad_rows = [k for k in range(n_r) if rows[k][3]]

    def body(*refs):
        b, i = pl.program_id(0), pl.program_id(1)
        vals = _row_vals(refs, n_r, n_b, n_g)
        ct_refs = refs[n_in:n_in + n_ct]
        ct = [r[0].astype(F32) for r in ct_refs[:len(outs)]] + [r[...] for r in ct_refs[len(outs):]]
        _, vjp = jax.vjp(tile, *vals)
        grads = vjp(tuple(ct))
        o_refs = refs[n_in + n_ct:]
        for j, k in enumerate(grad_rows):
            o_refs[j][0] = grads[k].astype(o_refs[j].dtype)
        for k in range(n_b):
            o_ref, val = o_refs[len(grad_rows) + k], grads[n_r + k]

            @pl.when(i == 0)
            def _(o_ref=o_ref, val=val):
                o_ref[0] = val

            @pl.when(i > 0)
            def _(o_ref=o_ref, val=val):
                o_ref[0] += val

        first = (b == 0) & (i == 0)
        for k in range(n_g):
            o_ref, val = o_refs[len(grad_rows) + n_b + k], grads[n_r + n_b + k]

            @pl.when(first)
            def _(o_ref=o_ref, val=val):
                o_ref[...] = val

            @pl.when(jnp.logical_not(first))
            def _(o_ref=o_ref, val=val):
                o_ref[...] += val

    in_specs = _row_in_specs(rows, bparams, gparams, tm)
    in_specs += [pl.BlockSpec((1, tm, w), lambda b, i: (b, i, 0)) for (w, _) in outs] + [pl.BlockSpec((1, 1), lambda b, i: (0, 0))] * n_sum
    out_shape = [_sds((bl, s, rows[k][1]), rows[k][0].dtype) for k in grad_rows]
    out_shape += [_sds(p.shape, F32) for p in bparams] + [_sds(p.shape, F32) for p in gparams]
    out_specs = [pl.BlockSpec((1, tm, rows[k][1]), lambda b, i: (b, i, 0)) for k in grad_rows]
    out_specs += [pl.BlockSpec((1, 1, p.shape[-1]), lambda b, i: (b, 0, 0)) for p in bparams]
    out_specs += [pl.BlockSpec(p.shape, lambda b, i, n=p.ndim: (0,) * n) for p in gparams]
    res = _pcall(
        body, name, out_shape, grid=(bl, s // tm), in_specs=in_specs, out_specs=out_specs, sem=("arbitrary", "arbitrary"),
    )(*[r[0] for r in rows], *bparams, *gparams, *cts)
    return res[:len(grad_rows)], res[len(grad_rows):len(grad_rows) + n_b], res[len(grad_rows) + n_b:]


def _make_row_op(name, tile, row_cfg, outs, n_sum, tm):
    def pack_rows(arrs):
        return [(a, w, cb, g) for a, (w, cb, g) in zip(arrs, row_cfg)]

    @jax.custom_vjp
    def op(arrs, bparams, gparams):
        return tuple(_row_fwd(name + "_fwd", tile, pack_rows(arrs), list(bparams), list(gparams), outs, n_sum, tm))

    def fwd(arrs, bparams, gparams):
        return op(arrs, bparams, gparams), (arrs, bparams, gparams)

    def bwd(res, cts):
        arrs, bparams, gparams = res
        d_rows, d_b, d_g = _row_bwd(name + "_bwd", tile, pack_rows(arrs), list(bparams), list(gparams), outs, n_sum, tm, list(cts))
        d_rows = list(d_rows)
        full = []
        for a, (w, cb, g) in zip(arrs, row_cfg):
            if g:
                d = d_rows.pop(0)
                assert d.shape == a.shape, "row inputs that want a gradient are passed at their own width"
                full.append(d)
            else:
                full.append(jnp.zeros_like(a))
        return tuple(full), tuple(d_b), tuple(d_g)

    op.defvjp(fwd, bwd)
    return op


def _nm_tile(x, sc, sh, g):
    return (_rms(x, g) * (1.0 + sc) + sh,)


def _rnm_tile(x, o, gate, sc, sh, g):
    xn = x + gate * o
    return xn, _rms(xn, g) * (1.0 + sc) + sh


def _final_tile(x, o, tgt, gate, g):
    e = _rms(x + gate * o, g) - tgt
    return (0.5 * jnp.sum(jnp.mean(e * e, axis=-1, keepdims=True), axis=0, keepdims=True),)


def _gm_tile(u_in, v_in, vg, ws, bs_t, og):
    d = u_in.shape[1]
    u, vn = _gelu(u_in), _rms(_gelu(v_in), vg)
    causal = _causal(CHUNK)
    lane = lax.broadcasted_iota(jnp.int32, (1, LANES), 1)
    parts = []
    for h in range(d // GM_HEAD):
        bias = jnp.sum(bs_t * (lane == h).astype(F32), axis=1, keepdims=True)
        parts.append(_dot_nn(jnp.where(causal, ws[h], 0.0), vn[:, h * GM_HEAD:(h + 1) * GM_HEAD]) + bias)
    return (_rms(u * jnp.concatenate(parts, axis=1), og),)


def _shift_down(x, j):
    if j == 0:
        return x
    row = lax.broadcasted_iota(jnp.int32, x.shape, 0)
    return jnp.where(row >= j, pltpu.roll(x, j, 0), 0.0)


def _shift_up(x, j):
    if j == 0:
        return x
    s = x.shape[0]
    row = lax.broadcasted_iota(jnp.int32, x.shape, 0)
    return jnp.where(row < s - j, pltpu.roll(x, s - j, 0), 0.0)


def _conv_pre(x, w_ref, b_ref, taps):
    acc = x * w_ref[taps - 1:taps, :] + b_ref[...]
    for j in range(1, taps):
        acc = acc + _shift_down(x, j) * w_ref[taps - 1 - j:taps - j, :]
    return acc


def _conv_tc(*widths):
    for tc in (512, 256, 128):
        if all(w % tc == 0 for w in widths):
            return tc
    raise ValueError("conv widths must be multiples of the lane count")


def _conv_act_fwd(name, src, col0, chans, w8, b, taps, val_col0, out_dtype):
    bl, s, _ = src.shape
    gated = val_col0 is not None
    tc = _conv_tc(chans, col0, val_col0 or 0)

    def body(*refs):
        if gated:
            x_ref, v_ref, w_ref, b_ref, o_ref = refs
        else:
            x_ref, w_ref, b_ref, o_ref = refs
        pre = _conv_pre(x_ref[0], w_ref, b_ref, taps)
        y = pre * _sigmoid(pre)
        if gated:
            y = y * v_ref[0]
        o_ref[0] = y.astype(o_ref.dtype)

    in_specs = [pl.BlockSpec((1, s, tc), lambda j, bb: (bb, 0, col0 // tc + j))]
    args = [src]
    if gated:
        in_specs.append(pl.BlockSpec((1, s, tc), lambda j, bb: (bb, 0, val_col0 // tc + j)))
        args.append(src)
    in_specs += [pl.BlockSpec((SUBLANES, tc), lambda j, bb: (0, j)), pl.BlockSpec((1, tc), lambda j, bb: (0, j))]
    return _pcall(
        body, name, _sds((bl, s, chans), out_dtype), grid=(chans // tc, bl), in_specs=in_specs,
        out_specs=pl.BlockSpec((1, s, tc), lambda j, bb: (bb, 0, j)), sem=("parallel", "arbitrary"),
    )(*args, w8, b)


def _conv_act_bwd(name, src, col0, chans, w8, b, taps, val_col0, dy):
    bl, s, _ = src.shape
    gated = val_col0 is not None
    tc = _conv_tc(chans, col0, val_col0 or 0)
    nblk = chans // tc

    def body(*refs):
        if gated:
            x_ref, v_ref, w_ref, b_ref, dy_ref, dx_ref, dw_ref, db_ref = refs
        else:
            x_ref, w_ref, b_ref, dy_ref, dx_ref, dw_ref, db_ref = refs
        bb = pl.program_id(1)
        x = x_ref[0]
        pre = _conv_pre(x, w_ref, b_ref, taps)
        sig = _sigmoid(pre)
        d = dy_ref[0].astype(F32)

        def gate_part():
            dsil = d * v_ref[0] if gated else d
            dpre = dsil * sig * (1.0 + pre * (1.0 - sig))
            dx = dpre * w_ref[taps - 1:taps, :]
            for j in range(1, taps):
                dx = dx + _shift_up(dpre, j) * w_ref[taps - 1 - j:taps - j, :]
            dx_ref[0] = dx
            rows = [jnp.sum(dpre * _shift_down(x, taps - 1 - k), axis=0, keepdims=True) for k in range(taps)]
            rows += [jnp.zeros_like(rows[0])] * (SUBLANES - taps)
            dw = jnp.concatenate(rows, axis=0)
            db = jnp.sum(dpre, axis=0, keepdims=True)

            @pl.when(bb == 0)
            def _():
                dw_ref[...] = dw
                db_ref[...] = db

            @pl.when(bb > 0)
            def _():
                dw_ref[...] += dw
                db_ref[...] += db

        if gated:
            part = pl.program_id(2)
            pl.when(part == 0)(gate_part)

            @pl.when(part == 1)
            def _():
                dx_ref[0] = d * (pre * sig)
        else:
            gate_part()

    if gated:
        grid, sem = (nblk, bl, 2), ("parallel", "arbitrary", "arbitrary")
        im = lambda f: (lambda j, bb, p: f(j, bb, p))
    else:
        grid, sem = (nblk, bl), ("parallel", "arbitrary")
        im = lambda f: (lambda j, bb: f(j, bb, 0))
    in_specs = [pl.BlockSpec((1, s, tc), im(lambda j, bb, p: (bb, 0, col0 // tc + j)))]
    args = [src]
    if gated:
        in_specs.append(pl.BlockSpec((1, s, tc), im(lambda j, bb, p: (bb, 0, val_col0 // tc + j))))
        args.append(src)
    in_specs += [
        pl.BlockSpec((SUBLANES, tc), im(lambda j, bb, p: (0, j))), pl.BlockSpec((1, tc), im(lambda j, bb, p: (0, j))),
        pl.BlockSpec((1, s, tc), im(lambda j, bb, p: (bb, 0, j))),
    ]
    out_shape = [_sds((bl, s, chans * (2 if gated else 1)), F32), _sds((SUBLANES, chans), F32), _sds((1, chans), F32)]
    out_specs = [
        pl.BlockSpec((1, s, tc), im(lambda j, bb, p: (bb, 0, j + p * nblk))), pl.BlockSpec((SUBLANES, tc), im(lambda j, bb, p: (0, j))),
        pl.BlockSpec((1, tc), im(lambda j, bb, p: (0, j))),
    ]
    return _pcall(body, name, out_shape, grid=grid, in_specs=in_specs, out_specs=out_specs, sem=sem)(*args, w8, b, dy)


def _ssd_chunk(g_idx, states, xs, bm, cm, dtr, z, dtb, alog, dsk, ng):
    gw = xs.shape[1]
    hpg = gw // HEAD_DIM
    dt = jax.nn.softplus(dtr + dtb)
    da = dt * (-jnp.exp(alog))
    causal = _causal(CHUNK)
    acs = _sel_left(causal.astype(F32), da)
    head_of_col = g_idx * hpg + lax.broadcasted_iota(jnp.int32, (LANES, gw), 1) // HEAD_DIM
    expand = (lax.broadcasted_iota(jnp.int32, (LANES, gw), 0) == head_of_col).astype(F32)
    dt_e, acs_e = _sel_right(dt, expand), _sel_right(acs, expand)
    last = lax.broadcasted_iota(jnp.int32, (CHUNK, gw), 0) == CHUNK - 1
    alast_e = jnp.sum(jnp.where(last, acs_e, 0.0), axis=0, keepdims=True)
    xc = xs * dt_e
    xc_st = xc * jnp.exp(alast_e - acs_e)
    decay_out, chunk_decay = jnp.exp(acs_e), jnp.exp(alast_e)
    cb = _dot_nt(cm, bm)
    acs_t = acs.T
    lane = lax.broadcasted_iota(jnp.int32, (1, LANES), 1)
    sub = lax.broadcasted_iota(jnp.int32, (LANES, 1), 0)
    ys, new_states = [], []
    for p in range(gw // LANES):
        sl = slice(p * LANES, (p + 1) * LANES)
        xcp = xc[:, sl]
        y = _dot_nn(cm, states[p]) * decay_out[:, sl]
        for q in range(2):
            head = g_idx * hpg + 2 * p + q
            col = jnp.sum(acs * (lane == head).astype(F32), axis=1, keepdims=True)
            row = jnp.sum(acs_t * (sub == head).astype(F32), axis=0, keepdims=True)
            decay = jnp.where(causal, jnp.exp(jnp.where(causal, col - row, 0.0)), 0.0)
            half = ((lane // HEAD_DIM) == q).astype(F32)
            y = y + _dot_nn(cb * decay, xcp * half)
        ys.append(y)
        new_states.append(states[p] * chunk_decay[:, sl] + _dot_tn(bm, xc_st[:, sl]))
    y = jnp.concatenate(ys, axis=1) + dsk * xs
    gated = y * (z * _sigmoid(z))
    return tuple(new_states), _rms(gated, ng)


def _ssd_specs(d, gw, dt_col, rev, nc):
    ci = (lambda i: nc - 1 - i) if rev else (lambda i: i)
    return [
        pl.BlockSpec((1, CHUNK, gw), lambda g, b, i: (b, ci(i), g)),
        pl.BlockSpec((1, CHUNK, STATE), lambda g, b, i: (b, ci(i), d // STATE + g)),
        pl.BlockSpec((1, CHUNK, STATE), lambda g, b, i: (b, ci(i), d // STATE + SSD_GROUPS + g)),
        pl.BlockSpec((1, CHUNK, LANES), lambda g, b, i: (b, ci(i), dt_col // LANES)),
        pl.BlockSpec((1, CHUNK, gw), lambda g, b, i: (b, ci(i), g)),
        pl.BlockSpec((1, LANES), lambda g, b, i: (0, 0)), pl.BlockSpec((1, LANES), lambda g, b, i: (0, 0)),
        pl.BlockSpec((1, gw), lambda g, b, i: (0, g)), pl.BlockSpec((1, gw), lambda g, b, i: (0, g)),
    ]


def _ssd_fwd(name, xbc, proj, dt_col, dtb, alog, dsk, ng):
    bl, s, cd = xbc.shape
    d = dsk.shape[1]
    gw, nc = d // SSD_GROUPS, s // CHUNK
    npair = gw // LANES

    def body(xs_ref, bm_ref, cm_ref, dt_ref, z_ref, dtb_ref, alog_ref, dsk_ref, ng_ref, y_ref, hp_ref, st_ref):
        g, i = pl.program_id(0), pl.program_id(2)

        @pl.when(i == 0)
        def _():
            st_ref[...] = jnp.zeros_like(st_ref)

        states = tuple(st_ref[p] for p in range(npair))
        hp_ref[0, 0, 0] = st_ref[...]
        new_states, yn = _ssd_chunk(g, states, xs_ref[0], bm_ref[0], cm_ref[0], dt_ref[0], z_ref[0], dtb_ref[...], alog_ref[...], dsk_ref[...], ng_ref[...])
        for p in range(npair):
            st_ref[p] = new_states[p]
        y_ref[0] = yn.astype(y_ref.dtype)

    out_shape = [_sds((bl, s, d), BF16), _sds((SSD_GROUPS, bl, nc, npair, STATE, LANES), F32)]
    out_specs = [
        pl.BlockSpec((1, CHUNK, gw), lambda g, b, i: (b, i, g)),
        pl.BlockSpec((1, 1, 1, npair, STATE, LANES), lambda g, b, i: (g, b, i, 0, 0, 0)),
    ]
    return _pcall(
        body, name, out_shape, grid=(SSD_GROUPS, bl, nc), in_specs=_ssd_specs(d, gw, dt_col, False, nc), out_specs=out_specs,
        scratch=[pltpu.VMEM((npair, STATE, LANES), F32)], sem=("arbitrary", "arbitrary", "arbitrary"),
    )(xbc, xbc, xbc, proj, proj, dtb, alog, dsk, ng)


def _ssd_bwd(name, xbc, proj, dt_col, dtb, alog, dsk, ng, hprev, dy):
    bl, s, cd = xbc.shape
    d = dsk.shape[1]
    gw, nc = d // SSD_GROUPS, s // CHUNK
    npair = gw // LANES

    def body(xs_ref, bm_ref, cm_ref, dt_ref, z_ref, dtb_ref, alog_ref, dsk_ref, ng_ref, hp_ref, dy_ref,
             dxs_ref, dbm_ref, dcm_ref, ddt_ref, dz_ref, ddtb_ref, dalog_ref, ddsk_ref, dng_ref, dst_ref):
        g, b, i = pl.program_id(0), pl.program_id(1), pl.program_id(2)

        @pl.when(i == 0)
        def _():
            dst_ref[...] = jnp.zeros_like(dst_ref)

        states = tuple(hp_ref[0, 0, 0, p] for p in range(npair))
        step = functools.partial(_ssd_chunk, g)
        _, vjp = jax.vjp(step, states, xs_ref[0], bm_ref[0], cm_ref[0], dt_ref[0], z_ref[0], dtb_ref[...], alog_ref[...], dsk_ref[...], ng_ref[...])
        d_states, dxs, dbm, dcm, ddt, dz, ddtb, dalog, ddsk, dng = vjp((tuple(dst_ref[p] for p in range(npair)), dy_ref[0].astype(F32)))
        for p in range(npair):
            dst_ref[p] = d_states[p]
        dxs_ref[0], dbm_ref[0], dcm_ref[0], dz_ref[0] = dxs, dbm, dcm, dz
        ddt_ref[0, 0] = ddt
        first_g = (b == 0) & (i == 0)
        first = first_g & (g == 0)
        for o_ref, val, init in ((ddtb_ref, ddtb, first), (dalog_ref, dalog, first), (ddsk_ref, ddsk, first_g), (dng_ref, dng, first_g)):
            @pl.when(init)
            def _(o_ref=o_ref, val=val):
                o_ref[...] = val

            @pl.when(jnp.logical_not(init))
            def _(o_ref=o_ref, val=val):
                o_ref[...] += val

    rc = lambda i: nc - 1 - i
    in_specs = _ssd_specs(d, gw, dt_col, True, nc) + [
        pl.BlockSpec((1, 1, 1, npair, STATE, LANES), lambda g, b, i: (g, b, rc(i), 0, 0, 0)),
        pl.BlockSpec((1, CHUNK, gw), lambda g, b, i: (b, rc(i), g)),
    ]
    out_shape = [
        _sds((bl, s, d), F32), _sds((bl, s, SSD_GROUPS * STATE), F32), _sds((bl, s, SSD_GROUPS * STATE), F32),
        _sds((SSD_GROUPS, bl, s, LANES), F32), _sds((bl, s, d), F32),
        _sds((1, LANES), F32), _sds((1, LANES), F32), _sds((1, d), F32), _sds((1, d), F32),
    ]
    out_specs = [
        pl.BlockSpec((1, CHUNK, gw), lambda g, b, i: (b, rc(i), g)),
        pl.BlockSpec((1, CHUNK, STATE), lambda g, b, i: (b, rc(i), g)), pl.BlockSpec((1, CHUNK, STATE), lambda g, b, i: (b, rc(i), g)),
        pl.BlockSpec((1, 1, CHUNK, LANES), lambda g, b, i: (g, b, rc(i), 0)),
        pl.BlockSpec((1, CHUNK, gw), lambda g, b, i: (b, rc(i), g)),
        pl.BlockSpec((1, LANES), lambda g, b, i: (0, 0)), pl.BlockSpec((1, LANES), lambda g, b, i: (0, 0)),
        pl.BlockSpec((1, gw), lambda g, b, i: (0, g)), pl.BlockSpec((1, gw), lambda g, b, i: (0, g)),
    ]
    return _pcall(
        body, name, out_shape, grid=(SSD_GROUPS, bl, nc), in_specs=in_specs, out_specs=out_specs,
        scratch=[pltpu.VMEM((npair, STATE, LANES), F32)], sem=("arbitrary", "arbitrary", "arbitrary"),
    )(xbc, xbc, xbc, proj, proj, dtb, alog, dsk, ng, hprev, dy)


def _make_mixer(name, d, cd, n_proj):
    xbc_col, dt_col = 3 * d, 3 * d + cd

    @jax.custom_vjp
    def mixer(proj, conv_w8, conv_b, dtb, alog, dsk, ng, vg, ws, bs_t, og):
        return fwd(proj, conv_w8, conv_b, dtb, alog, dsk, ng, vg, ws, bs_t, og)[0]

    def fwd(proj, conv_w8, conv_b, dtb, alog, dsk, ng, vg, ws, bs_t, og):
        xbc = _conv_act_fwd(name + "_conv_fwd", proj, xbc_col, cd, conv_w8, conv_b, SSD_CONV, None, F32)
        y, hprev = _ssd_fwd(name + "_ssd_fwd", xbc, proj, dt_col, dtb, alog, dsk, ng)
        rows = [(proj, d, 1, True), (proj, d, 2, True)]
        (g_out,) = _row_fwd(name + "_gm_fwd", _gm_tile, rows, [], [vg, ws, bs_t, og], [(d, BF16)], 0, CHUNK)
        ycat = jnp.concatenate([y, g_out], axis=-1)
        return ycat, (proj, conv_w8, conv_b, dtb, alog, dsk, ng, vg, ws, bs_t, og, xbc, hprev)

    def bwd(res, dycat):
        proj, conv_w8, conv_b, dtb, alog, dsk, ng, vg, ws, bs_t, og, xbc, hprev = res
        bl, s, _ = proj.shape
        dy, dg_out = dycat[..., :d], dycat[..., d:]
        dxs, dbm, dcm, ddt2, dz, ddtb, dalog, ddsk, dng = _ssd_bwd(name + "_ssd_bwd", xbc, proj, dt_col, dtb, alog, dsk, ng, hprev, dy)
        dxbc_act = jnp.concatenate([dxs, dbm, dcm], axis=-1)
        dxbc, dw8, dcb = _conv_act_bwd(name + "_conv_bwd", proj, xbc_col, cd, conv_w8, conv_b, SSD_CONV, None, dxbc_act)
        rows = [(proj, d, 1, True), (proj, d, 2, True)]
        (du, dv), _, (dvg, dws, dbs_t, dog) = _row_bwd(name + "_gm_bwd", _gm_tile, rows, [], [vg, ws, bs_t, og], [(d, BF16)], 0, CHUNK, [dg_out])
        ddt = ddt2[0] + ddt2[1]
        pad = jnp.zeros((bl, s, n_proj - dt_col - LANES), F32)
        dproj = jnp.concatenate([dz, du, dv, dxbc, ddt, pad], axis=-1)
        return dproj, dw8, dcb, ddtb, dalog, ddsk, dng, dvg, dws, dbs_t, dog

    mixer.defvjp(fwd, bwd)
    return mixer


def _make_ffn_act(name, f):
    @jax.custom_vjp
    def act(up, w8, b):
        return _conv_act_fwd(name + "_fwd", up, 0, f, w8, b, FF_CONV, f, BF16)

    def fwd(up, w8, b):
        return act(up, w8, b), (up, w8, b)

    def bwd(res, dy):
        up, w8, b = res
        return tuple(_conv_act_bwd(name + "_bwd", up, 0, f, w8, b, FF_CONV, f, dy))

    act.defvjp(fwd, bwd)
    return act


def _position():
    return lax.axis_index("x"), lax.axis_index("y"), lax.axis_index("c")


def _at(ref, idx):
    return ref.at[idx] if len(idx) else ref


def _exchange(name, inputs, out_shapes, plan):
    n_in, n_out = len(inputs), len(out_shapes)
    n_copy = len(plan(0, 0, 0))

    def body(*refs):
        in_refs, out_refs = refs[:n_in], refs[n_in:n_in + n_out]
        send_sems, recv_sems = refs[n_in + n_out:]
        x, y, c = _position()
        copies = plan(x, y, c)

        def copy(k, src, dst, peer):
            return pltpu.make_async_remote_copy(src_ref=src, dst_ref=dst, send_sem=send_sems.at[k], recv_sem=recv_sems.at[k], device_id=peer, device_id_type=MESH)

        sends = [copy(k, _at(in_refs[sa], si), _at(out_refs[da], di), peer) for k, (sa, si, da, di, peer, _) in enumerate(copies)]
        for cp in sends:
            cp.start()
        for k, (sa, si, da, _, peer, li) in enumerate(copies):
            copy(k, _at(in_refs[sa], si), _at(out_refs[da], li), peer).wait_recv()
        for cp in sends:
            cp.wait_send()

    any_spec = pl.BlockSpec(memory_space=pl.ANY)
    outs = pl.pallas_call(
        body, name=name, out_shape=[_sds(s, dt) for s, dt in out_shapes], in_specs=[any_spec] * n_in, out_specs=[any_spec] * n_out,
        scratch_shapes=[pltpu.SemaphoreType.DMA((n_copy,)), pltpu.SemaphoreType.DMA((n_copy,))],
    )(*inputs)
    return list(outs)


def _chip_peers(x, y, c):
    return [(1 - x, y, c), (x, 1 - y, c), (1 - x, 1 - y, c)]


def _chip_of(p):
    return 2 * p[0] + p[1]


def _set_slot(slots, me, blk):
    return lax.dynamic_update_slice(slots, blk[None], (me,) + (0,) * blk.ndim)


def _by_core(c, mine, other, axis):
    return jnp.where(c == 0, jnp.stack([mine, other], axis), jnp.stack([other, mine], axis))


def _gather_chips(name, blocks):
    n = len(blocks)

    def plan(x, y, c):
        return [(a, (), a, (2 * x + y,), p, (_chip_of(p),)) for a in range(n) for p in _chip_peers(x, y, c)]

    recv = _exchange(name, blocks, [((4,) + b.shape, b.dtype) for b in blocks], plan)
    x, y, _ = _position()
    return [_set_slot(r, 2 * x + y, b) for r, b in zip(recv, blocks)]


def _gather_two_level(name, blocks):
    n = len(blocks)

    def plan_chips(x, y, c):
        return [(a, (), a, (2 * x + y,), p, (_chip_of(p),)) for a in range(n) for p in _chip_peers(x, y, c)]

    def plan_cores(x, y, c):
        me, sib = 2 * x + y, (x, y, 1 - c)
        own = [(a, (), a, (me,), sib, (me,)) for a in range(n)]
        passed = [(n + a, (_chip_of(p),), a, (_chip_of(p),), sib, (_chip_of(p),)) for a in range(n) for p in _chip_peers(x, y, c)]
        return own + passed

    shapes = [((4,) + b.shape, b.dtype) for b in blocks]
    from_chips = _exchange(name + "_chips", blocks, shapes, plan_chips)
    from_core = _exchange(name + "_cores", list(blocks) + from_chips, shapes, plan_cores)
    x, y, c = _position()
    return [_by_core(c, _set_slot(r1, 2 * x + y, b), r2, 1) for b, r1, r2 in zip(blocks, from_chips, from_core)]


def _pair_add(name, g42, r4):
    _, _, rh, cols = g42.shape
    tr = _divisors(rh, 512, SUBLANES * 2)[-1]

    def body(c_ref, a_ref, b_ref, o_ref):
        o_ref[0] = (a_ref[0, 0].astype(F32) + b_ref[0].astype(F32)).astype(o_ref.dtype)

    cidx = lax.axis_index("c").astype(jnp.int32).reshape(1)
    return _pcall(
        body, name, _sds(r4.shape, BF16), grid=(4, rh // tr),
        in_specs=[pl.BlockSpec((1, 1, tr, cols), lambda s, i, c_ref: (s, c_ref[0], i, 0)), pl.BlockSpec((1, tr, cols), lambda s, i, c_ref: (s, i, 0))],
        out_specs=pl.BlockSpec((1, tr, cols), lambda s, i, c_ref: (s, i, 0)), sem=("parallel", "parallel"), prefetch=1,
    )(cidx, g42, r4)


def _slot_sum(name, parts):
    n, r, cols = parts.shape
    cap = max(2 * SUBLANES, (4 * 1024 * 1024) // (n * cols * parts.dtype.itemsize))
    tr = _divisors(r, cap, 2 * SUBLANES)[-1]

    def body(p_ref, o_ref):
        acc = p_ref[0].astype(F32)
        for k in range(1, n):
            acc = acc + p_ref[k].astype(F32)
        o_ref[...] = acc

    return _pcall(
        body, name, _sds((r, cols), F32), grid=(r // tr,), in_specs=[pl.BlockSpec((n, tr, cols), lambda i: (0, i, 0))],
        out_specs=pl.BlockSpec((tr, cols), lambda i: (i, 0)), sem=("parallel",),
    )(parts)


def _slot_sums(name, parts):
    k = len(parts)

    def body(*refs):
        for p_ref, o_ref in zip(refs[:k], refs[k:]):
            acc = p_ref[0]
            for j in range(1, p_ref.shape[0]):
                acc = acc + p_ref[j]
            o_ref[...] = acc

    return list(_pcall(body, name, [_sds(p.shape[1:], F32) for p in parts])(*parts))


def _reduce_two_level(name, grads):
    n = len(grads)
    x, y, c = _position()
    me = 2 * x + y

    def plan_swap(x, y, c):
        return [(a, (s, 1 - c), a, (s,), (x, y, 1 - c), (s,)) for a in range(n) for s in range(4)]

    def plan_chips(x, y, c):
        return [(a, (_chip_of(p),), a, (2 * x + y,), p, (_chip_of(p),)) for a in range(n) for p in _chip_peers(x, y, c)]

    def plan_share(x, y, c):
        return [(a, (), a, (), (x, y, 1 - c), ()) for a in range(n)]

    other = _exchange(name + "_cores", grads, [((4,) + g.shape[2:], g.dtype) for g in grads], plan_swap)
    pair = [_pair_add(f"{name}_pair{a}", g, o) for a, (g, o) in enumerate(zip(grads, other))]
    recv = _exchange(name + "_chips", pair, [(p.shape, p.dtype) for p in pair], plan_chips)
    parts = [lax.dynamic_update_slice(r, lax.dynamic_slice_in_dim(p, me, 1, 0), (me, 0, 0)) for r, p in zip(recv, pair)]
    mine = [_slot_sum(f"{name}_sum{a}", p) for a, p in enumerate(parts)]
    theirs = _exchange(name + "_share", mine, [(m.shape, m.dtype) for m in mine], plan_share)
    return [_by_core(c, m, t, 0) for m, t in zip(mine, theirs)]


def _make_gather_weights(name):
    @jax.custom_vjp
    def gather(shards):
        c = lax.axis_index("c")
        halves = [lax.dynamic_slice_in_dim(w, c * (w.shape[0] // 2), w.shape[0] // 2, 0).astype(BF16) for w in shards]
        full = _gather_two_level(name + "_ag", halves)
        return tuple(f.reshape((4,) + w.shape) for f, w in zip(full, shards))

    def bwd(_, gs):
        g42 = [g.reshape(4, 2, g.shape[1] // 2, g.shape[2]) for g in gs]
        return (tuple(r.reshape(g.shape[1:]) for r, g in zip(_reduce_two_level(name + "_rs", g42), gs)),)

    gather.defvjp(lambda shards: (gather(shards), None), bwd)
    return gather


def _make_replicated(name):
    @jax.custom_vjp
    def rep(ps):
        return ps

    def bwd(_, gs):
        allg = [g.reshape((8,) + g.shape[2:]) for g in _gather_two_level(name + "_ag", list(gs))]
        return (tuple(_slot_sums(name + "_sum", allg)),)

    rep.defvjp(lambda ps: (ps, None), bwd)
    return rep


def _ada_fwd_call(name, c_all, w, b_shard):
    nl, d, ns = w.shape
    nb = c_all.shape[0]

    def body(c_ref, w_ref, b_ref, o_ref):
        cv = c_ref[...]
        o_ref[0] = _dg(cv * _sigmoid(cv), w_ref[0], ((1,), (0,))) + b_ref[0]

    return _pcall(
        body, name, _sds((nl, nb, ns), F32), grid=(nl,),
        in_specs=[pl.BlockSpec((nb, d), lambda l: (0, 0)), pl.BlockSpec((1, d, ns), lambda l: (l, 0, 0)), pl.BlockSpec((1, 1, ns), lambda l: (l, 0, 0))],
        out_specs=pl.BlockSpec((1, nb, ns), lambda l: (l, 0, 0)), sem=("parallel",),
    )(c_all, w, b_shard)


def _ada_bwd_call(name, c_all, dm_shard, dm_all):
    nl, nb, ns = dm_shard.shape
    d = c_all.shape[1]
    nm = dm_all.shape[2]

    def body(c_ref, ds_ref, da_ref, dw_ref, db_ref):
        cv = c_ref[...]
        dw_ref[0] = _dg(cv * _sigmoid(cv), ds_ref[0], ((0,), (0,)))
        db_ref[0] = jnp.sum(da_ref[0], axis=0, keepdims=True)

    return _pcall(
        body, name, [_sds((nl, d, ns), F32), _sds((nl, 1, nm), F32)], grid=(nl,),
        in_specs=[pl.BlockSpec((nb, d), lambda l: (0, 0)), pl.BlockSpec((1, nb, ns), lambda l: (l, 0, 0)), pl.BlockSpec((1, nb, nm), lambda l: (l, 0, 0))],
        out_specs=[pl.BlockSpec((1, d, ns), lambda l: (l, 0, 0)), pl.BlockSpec((1, 1, nm), lambda l: (l, 0, 0))], sem=("parallel",),
    )(c_all, dm_shard, dm_all)


def _make_ada(name, bl):
    def my_chip():
        return 2 * lax.axis_index("x") + lax.axis_index("y")

    def my_dev():
        return 2 * my_chip() + lax.axis_index("c")

    @jax.custom_vjp
    def ada(c_all, w, b):
        nl, d, ns = w.shape
        b_shard = lax.dynamic_slice(b, (0, my_chip() * ns), (nl, ns)).reshape(nl, 1, ns)
        shard = _ada_fwd_call(name + "_fwd", c_all, w, b_shard)
        (allc,) = _gather_chips(name + "_ag", [shard])
        mods = jnp.transpose(allc, (1, 2, 0, 3)).reshape(nl, c_all.shape[0], 4 * ns)
        return lax.dynamic_slice(mods, (0, my_dev() * bl, 0), (nl, bl, 4 * ns))

    def fwd(c_all, w, b):
        return ada(c_all, w, b), (c_all, w.shape)

    def bwd(res, dm):
        c_all, (nl, d, ns) = res
        (dm_all,) = _gather_two_level(name + "_bwd_ag", [dm])
        dm_all = jnp.transpose(dm_all.reshape((8,) + dm.shape), (1, 0, 2, 3)).reshape(nl, 8 * bl, 4 * ns)
        dm_shard = lax.dynamic_slice(dm_all, (0, 0, my_chip() * ns), (nl, 8 * bl, ns))
        dw, db = _ada_bwd_call(name + "_bwd", c_all, dm_shard, dm_all)
        return jnp.zeros_like(c_all), dw, db.reshape(nl, 4 * ns)

    ada.defvjp(fwd, bwd)
    return ada


def _adamw(name, w, g, m, v):
    shape = w.shape
    cols = shape[-1]
    w2, g2, m2, v2 = (t.reshape(-1, cols) for t in (w, g, m, v))
    rows = w2.shape[0]
    cap = max(SUBLANES, (512 * 1024) // max(cols, 1) // SUBLANES * SUBLANES)
    tr = _divisors(rows, cap, SUBLANES)[-1]

    def body(w_ref, g_ref, m_ref, v_ref, d_ref, mo_ref, vo_ref):
        gv = g_ref[...]
        mn = ADAM_B1 * m_ref[...] + (1.0 - ADAM_B1) * gv
        vn = ADAM_B2 * v_ref[...] + (1.0 - ADAM_B2) * (gv * gv)
        m_hat = mn / (1.0 - ADAM_B1 ** ADAM_STEP)
        v_hat = vn / (1.0 - ADAM_B2 ** ADAM_STEP)
        d_ref[...] = -ADAM_LR * (m_hat / (jnp.sqrt(v_hat) + ADAM_EPS) + ADAM_WD * w_ref[...])
        mo_ref[...] = mn
        vo_ref[...] = vn

    spec = pl.BlockSpec((tr, cols), lambda i: (i, 0))
    outs = _pcall(body, name, [_sds((rows, cols), F32)] * 3, grid=(rows // tr,), in_specs=[spec] * 4, out_specs=[spec] * 3, sem=("parallel",))(w2, g2, m2, v2)
    return tuple(o.reshape(shape) for o in outs)


def _adamw_small(name, ws, gs, ms, vs):
    n = len(ws)

    def body(*refs):
        for k in range(n):
            w_ref, g_ref, m_ref, v_ref = (refs[j * n + k] for j in range(4))
            d_ref, mo_ref, vo_ref = (refs[(4 + j) * n + k] for j in range(3))
            gv = g_ref[...]
            mn = ADAM_B1 * m_ref[...] + (1.0 - ADAM_B1) * gv
            vn = ADAM_B2 * v_ref[...] + (1.0 - ADAM_B2) * (gv * gv)
            m_hat = mn / (1.0 - ADAM_B1 ** ADAM_STEP)
            v_hat = vn / (1.0 - ADAM_B2 ** ADAM_STEP)
            d_ref[...] = -ADAM_LR * (m_hat / (jnp.sqrt(v_hat) + ADAM_EPS) + ADAM_WD * w_ref[...])
            mo_ref[...] = mn
            vo_ref[...] = vn

    outs = _pcall(body, name, [_sds(w.shape, F32) for w in ws] * 3)(*ws, *gs, *ms, *vs)
    return outs[:n], outs[n:2 * n], outs[2 * n:]


def _pad_rows(w, rows):
    return jnp.pad(w, ((0, rows - w.shape[0]), (0, 0)))


def _pad_lanes(v):
    return jnp.pad(v, (0, LANES - v.shape[0])).reshape(1, LANES)


BIG = ("w_in", "w_out", "ff_up", "ff_down")
BIG_AXIS = {"w_in": 1, "w_out": 0, "ff_up": 1, "ff_down": 0}
CONVW = ("ssd_conv_w", "ff_conv_w")
SMALL = ("norm1_g", "norm2_g", "ssd_conv_b", "ssd_dt_bias", "ssd_a_log", "ssd_d", "ssd_norm_g", "gm_vnorm_g", "gm_ws", "gm_bs", "gm_out_g", "ff_conv_b")
WEIGHTS = ("ada_w", "ada_b", "norm1_g", "norm2_g", "w_in", "ssd_conv_w", "ssd_conv_b", "ssd_dt_bias", "ssd_a_log", "ssd_d", "ssd_norm_g", "gm_vnorm_g", "gm_ws", "gm_bs", "gm_out_g", "w_out", "ff_up", "ff_conv_w", "ff_conv_b", "ff_down", "final_g")


def kernel(x, c, ada_w, ada_b, norm1_g, norm2_g, w_in, ssd_conv_w, ssd_conv_b, ssd_dt_bias, ssd_a_log, ssd_d, ssd_norm_g, gm_vnorm_g, gm_ws, gm_bs, gm_out_g, w_out, ff_up, ff_conv_w, ff_conv_b, ff_down, final_g, loss_target, m_ada_w, m_ada_b, m_norm1_g, m_norm2_g, m_w_in, m_ssd_conv_w, m_ssd_conv_b, m_ssd_dt_bias, m_ssd_a_log, m_ssd_d, m_ssd_norm_g, m_gm_vnorm_g, m_gm_ws, m_gm_bs, m_gm_out_g, m_w_out, m_ff_up, m_ff_conv_w, m_ff_conv_b, m_ff_down, m_final_g, v_ada_w, v_ada_b, v_norm1_g, v_norm2_g, v_w_in, v_ssd_conv_w, v_ssd_conv_b, v_ssd_dt_bias, v_ssd_a_log, v_ssd_d, v_ssd_norm_g, v_gm_vnorm_g, v_gm_ws, v_gm_bs, v_gm_out_g, v_w_out, v_ff_up, v_ff_conv_w, v_ff_conv_b, v_ff_down, v_final_g):
    given = dict(locals())
    weights = {n: given[n] for n in WEIGHTS}
    bl, s, d = x.shape
    nl = ada_w.shape[0]
    heads = d // HEAD_DIM
    cd = d + 2 * SSD_GROUPS * STATE
    f = ff_down.shape[1] * 4
    n_in = d + cd + heads + 2 * d
    n_proj = _round_up(3 * d + cd + LANES, 2 * LANES)
    tm = _divisors(s, 512)[-1]

    pre = _gather_two_level("pre_ag", [c] + [weights[n] for n in CONVW])
    c_all = pre[0].reshape(8 * bl, d)
    conv_full = [jnp.concatenate([p[k, 0] for k in range(4)], axis=-1) for p in pre[1:]]
    ada = _make_ada("ada", bl)
    replicated = _make_replicated("small")
    small_in = tuple(weights[n] for n in SMALL) + tuple(conv_full) + (final_g.reshape(1, d),)

    def local_loss(params, x_in):
        ada_w_, ada_b_, bigs, small = params
        mods = ada(c_all, ada_w_, ada_b_).reshape(nl, bl, N_MOD, 1, d)
        small_vals = dict(zip(SMALL + CONVW + ("final_g",), replicated(small)))
        xcur, pending = x_in, None
        loss = None
        for l in range(nl):
            sp = {n: small_vals[n][l] for n in SMALL + CONVW}
            full = _make_gather_weights(f"wg{l}")(tuple(bigs[n][l] for n in BIG))
            wfull = {}
            for n, t in zip(BIG, full):
                wfull[n] = t.reshape(-1, t.shape[2]) if BIG_AXIS[n] == 0 else jnp.concatenate([t[k] for k in range(4)], axis=1)
            wi = wfull["w_in"]
            w_in_p = jnp.concatenate(
                [wi[:, :d], wi[:, d + cd + heads:], wi[:, d:d + cd], wi[:, d + cd:d + cd + heads], jnp.zeros((d, n_proj - (3 * d + cd + heads)), BF16)], axis=1)
            sh1, sc1, g1, sh2, sc2, g2 = (mods[l, :, k] for k in range(N_MOD))
            r2 = lambda v: v.reshape(1, -1)
            if pending is None:
                (h,) = _make_row_op(f"nm{l}", _nm_tile, [(d, 0, True)], [(d, BF16)], 0, tm)((xcur,), (sc1, sh1), (r2(sp["norm1_g"]),))
            else:
                xcur, h = _make_row_op(f"rnm{l}a", _rnm_tile, [(d, 0, True), (d, 0, True)], [(d, F32), (d, BF16)], 0, tm)(
                    (xcur, pending[0]), (pending[1], sc1, sh1), (r2(sp["norm1_g"]),))
            proj = _make_mm(f"win{l}", F32)(h, w_in_p)
            ycat = _make_mixer(f"mix{l}", d, cd, n_proj)(
                proj, _pad_rows(sp["ssd_conv_w"], SUBLANES), r2(sp["ssd_conv_b"]), _pad_lanes(sp["ssd_dt_bias"]), _pad_lanes(sp["ssd_a_log"]),
                r2(jnp.repeat(sp["ssd_d"], HEAD_DIM)), r2(sp["ssd_norm_g"]), r2(sp["gm_vnorm_g"]), sp["gm_ws"],
                jnp.pad(sp["gm_bs"].T, ((0, 0), (0, LANES - sp["gm_bs"].shape[0]))), r2(sp["gm_out_g"]))
            mix = _make_mm(f"wout{l}", F32)(ycat, wfull["w_out"])
            xcur, h2 = _make_row_op(f"rnm{l}b", _rnm_tile, [(d, 0, True), (d, 0, True)], [(d, F32), (d, BF16)], 0, tm)(
                (xcur, mix), (g1, sc2, sh2), (r2(sp["norm2_g"]),))
            up = _make_mm(f"ffup{l}", F32)(h2, wfull["ff_up"])
            act = _make_ffn_act(f"ffact{l}", f)(up, _pad_rows(sp["ff_conv_w"], SUBLANES), r2(sp["ff_conv_b"]))
            down = _make_mm(f"ffdown{l}", F32)(act, wfull["ff_down"])
            pending = (down, g2)
        (loss,) = _make_row_op("final", _final_tile, [(d, 0, True), (d, 0, True), (d, 0, False)], [], 1, tm)(
            (xcur, pending[0], loss_target), (pending[1],), (small_vals["final_g"],))
        return loss[0, 0]

    params = (ada_w, ada_b, {n: weights[n] for n in BIG}, small_in)
    loss_local, (grads, grad_x) = jax.value_and_grad(local_loss, argnums=(0, 1))(params, x)
    loss = lax.psum(loss_local, AXES)
    g_ada_w, g_ada_b, g_bigs, g_small = grads

    grad = {"ada_w": g_ada_w, "ada_b": g_ada_b, **g_bigs}
    grad.update(zip(SMALL + CONVW + ("final_g",), g_small))
    chip = 2 * lax.axis_index("x") + lax.axis_index("y")
    for n in CONVW:
        width = weights[n].shape[-1]
        grad[n] = lax.dynamic_slice_in_dim(grad[n], chip * width, width, axis=2)

    delta, new_m, new_v = {}, {}, {}
    for n in ("ada_w",) + BIG:
        delta[n], new_m[n], new_v[n] = _adamw("adam_" + n, weights[n], grad[n], given["m_" + n], given["v_" + n])
    small_names = ("ada_b",) + SMALL + CONVW + ("final_g",)
    as2d = lambda t: t.reshape(1, -1) if t.ndim == 1 else t
    res = _adamw_small(
        "adam_small", [as2d(weights[n]) for n in small_names], [as2d(grad[n]) for n in small_names],
        [as2d(given["m_" + n]) for n in small_names], [as2d(given["v_" + n]) for n in small_names])
    for out, vals in zip((delta, new_m, new_v), res):
        for n, val in zip(small_names, vals):
            out[n] = val.reshape(weights[n].shape)
    grad["final_g"] = grad["final_g"].reshape(final_g.shape)

    return (loss, grad_x, *[grad[n] for n in WEIGHTS], *[delta[n] for n in WEIGHTS], *[new_m[n] for n in WEIGHTS], *[new_v[n] for n in WEIGHTS])
```

```python
import functools

import jax
import jax.numpy as jnp
from jax import lax
from jax.experimental import pallas as pl
from jax.experimental.pallas import tpu as pltpu

F32 = jnp.float32
BF16 = jnp.bfloat16
EPS = 1e-6
CHUNK = 128
HEAD_DIM = 64
STATE = 128
GM_HEAD = 128
SSD_GROUPS = 2
SSD_CONV = 4
FF_CONV = 3
N_MOD = 6
LANES = 128
SUBLANES = 8
V7X_VMEM_LIMIT = 48 * 1024 * 1024
MM_VMEM_BUDGET = 30 * 1024 * 1024
MM_STEP_MACS = 512 * 1024 * 2560
V7X_MXU_FLOPS = 996e12
V7X_HBM_BYTES_PER_S = 3.3e12
V7X_STEP_SECONDS = 0.35e-6
ADAM_LR, ADAM_B1, ADAM_B2, ADAM_EPS, ADAM_WD, ADAM_STEP = 0.001, 0.9, 0.999, 1e-08, 0.01, 10
MESH = pl.DeviceIdType.MESH
AXES = ("x", "y", "c")


def _round_up(n, m):
    return (n + m - 1) // m * m


def _divisors(n, cap, mult=LANES):
    out = [t for t in range(mult, min(n, cap) + 1, mult) if n % t == 0]
    return out or [n]


def _pcall(body, name, out_shape, grid=(), in_specs=None, out_specs=None, scratch=(), sem=None, prefetch=0):
    params = pltpu.CompilerParams(dimension_semantics=sem, vmem_limit_bytes=V7X_VMEM_LIMIT)
    if prefetch:
        spec = pltpu.PrefetchScalarGridSpec(num_scalar_prefetch=prefetch, grid=grid, in_specs=in_specs, out_specs=out_specs, scratch_shapes=list(scratch))
        return pl.pallas_call(body, name=name, out_shape=out_shape, grid_spec=spec, compiler_params=params)
    if in_specs is None:
        return pl.pallas_call(body, name=name, out_shape=out_shape, compiler_params=params)
    return pl.pallas_call(body, name=name, out_shape=out_shape, grid=grid, in_specs=in_specs, out_specs=out_specs, scratch_shapes=list(scratch), compiler_params=params)


def _sds(shape, dtype):
    return jax.ShapeDtypeStruct(tuple(shape), dtype)


def _mm_tiles(m, n, k, a_bytes, b_bytes, o_bytes):
    best, best_key = None, None
    for tm in _divisors(m, 2048):
        for tn in _divisors(n, 2560):
            for tk in _divisors(k, 2560):
                vmem = 2 * (tm * tk * a_bytes + tk * tn * b_bytes + tm * tn * o_bytes) + tm * tn * 4
                if vmem > MM_VMEM_BUDGET or tm * tn * tk > MM_STEP_MACS:
                    continue
                ni, nj, nk = m // tm, n // tn, k // tk
                a_reads = 1 if nk == 1 else nj
                b_reads = 1 if (nk == 1 and nj == 1) else ni
                hbm = a_reads * m * k * a_bytes + b_reads * k * n * b_bytes + m * n * o_bytes
                t = max(2.0 * m * n * k / V7X_MXU_FLOPS, hbm / V7X_HBM_BYTES_PER_S) + ni * nj * nk * V7X_STEP_SECONDS
                key = (-t, tm * tn * tk)
                if best_key is None or key > best_key:
                    best, best_key = (tm, tn, tk), key
    return best


def _matmul(name, a, b, mode, out_dtype):
    if mode == "nn":
        (m, k), n = a.shape, b.shape[1]
    elif mode == "nt":
        (m, k), n = a.shape, b.shape[0]
    else:
        (k, m), n = a.shape, b.shape[1]
    tm, tn, tk = _mm_tiles(m, n, k, a.dtype.itemsize, b.dtype.itemsize, jnp.dtype(out_dtype).itemsize)
    nk = k // tk
    if mode == "nn":
        a_spec = pl.BlockSpec((tm, tk), lambda i, j, kk: (i, kk))
        b_spec = pl.BlockSpec((tk, tn), lambda i, j, kk: (kk, j))
        dims = ((1,), (0,))
    elif mode == "nt":
        a_spec = pl.BlockSpec((tm, tk), lambda i, j, kk: (i, kk))
        b_spec = pl.BlockSpec((tn, tk), lambda i, j, kk: (j, kk))
        dims = ((1,), (1,))
    else:
        a_spec = pl.BlockSpec((tk, tm), lambda i, j, kk: (kk, i))
        b_spec = pl.BlockSpec((tk, tn), lambda i, j, kk: (kk, j))
        dims = ((0,), (0,))

    def body(a_ref, b_ref, o_ref, acc_ref):
        kk = pl.program_id(2)
        p = lax.dot_general(a_ref[...].astype(BF16), b_ref[...].astype(BF16), (dims, ((), ())), preferred_element_type=F32)
        if nk == 1:
            o_ref[...] = p.astype(o_ref.dtype)
        else:
            @pl.when(kk == 0)
            def _():
                acc_ref[...] = p

            @pl.when(kk > 0)
            def _():
                acc_ref[...] += p

            @pl.when(kk == nk - 1)
            def _():
                o_ref[...] = acc_ref[...].astype(o_ref.dtype)

    return _pcall(
        body, name, _sds((m, n), out_dtype), grid=(m // tm, n // tn, nk), in_specs=[a_spec, b_spec],
        out_specs=pl.BlockSpec((tm, tn), lambda i, j, kk: (i, j)), scratch=[pltpu.VMEM((tm, tn), F32)],
        sem=("parallel", "parallel", "arbitrary"),
    )(a, b)


def _mm(name, a, w, out_dtype):
    return _matmul(name, a.reshape(-1, a.shape[-1]), w, "nn", out_dtype).reshape(a.shape[:-1] + (w.shape[1],))


def _mm_bwd(name, a, w, dy):
    a2, dy2 = a.reshape(-1, a.shape[-1]), dy.reshape(-1, dy.shape[-1])
    return _matmul(name + "_dx", dy2, w, "nt", BF16).reshape(a.shape), _matmul(name + "_dw", a2, dy2, "tn", BF16)


def _dg(a, b, dims):
    return lax.dot_general(a.astype(BF16), b.astype(BF16), (dims, ((), ())), preferred_element_type=F32)


@jax.custom_vjp
def _dot_nn(a, b):
    return _dg(a, b, ((1,), (0,)))


_dot_nn.defvjp(lambda a, b: (_dot_nn(a, b), (a, b)), lambda r, d: (_dg(d, r[1], ((1,), (1,))), _dg(r[0], d, ((0,), (0,)))))


@jax.custom_vjp
def _dot_nt(a, b):
    return _dg(a, b, ((1,), (1,)))


_dot_nt.defvjp(lambda a, b: (_dot_nt(a, b), (a, b)), lambda r, d: (_dg(d, r[1], ((1,), (0,))), _dg(d, r[0], ((0,), (0,)))))


@jax.custom_vjp
def _dot_tn(a, b):
    return _dg(a, b, ((0,), (0,)))


_dot_tn.defvjp(lambda a, b: (_dot_tn(a, b), (a, b)), lambda r, d: (_dg(r[1], d, ((1,), (1,))), _dg(r[0], d, ((1,), (0,)))))


def _exact_dot(a, c, dims):
    hi = a.astype(BF16)
    r1 = a - hi.astype(F32)
    mid = r1.astype(BF16)
    lo = (r1 - mid.astype(F32)).astype(BF16)
    cb = c.astype(BF16)
    f = lambda t: lax.dot_general(t, cb, (dims, ((), ())), preferred_element_type=F32)
    return f(hi) + f(mid) + f(lo)


@jax.custom_vjp
def _sel_right(a, c):
    return _exact_dot(a, c, ((1,), (0,)))


_sel_right.defvjp(lambda a, c: (_sel_right(a, c), c), lambda c, d: (_exact_dot(d, c, ((1,), (1,))), jnp.zeros_like(c)))


def _exact_dot_left(c, a, dims):
    hi = a.astype(BF16)
    r1 = a - hi.astype(F32)
    mid = r1.astype(BF16)
    lo = (r1 - mid.astype(F32)).astype(BF16)
    cb = c.astype(BF16)
    f = lambda t: lax.dot_general(cb, t, (dims, ((), ())), preferred_element_type=F32)
    return f(hi) + f(mid) + f(lo)


@jax.custom_vjp
def _sel_left(c, a):
    return _exact_dot_left(c, a, ((1,), (0,)))


_sel_left.defvjp(lambda c, a: (_sel_left(c, a), c), lambda c, d: (jnp.zeros_like(c), _exact_dot_left(c, d, ((0,), (0,)))))


def _sigmoid(x):
    return 1.0 / (1.0 + jnp.exp(-x))


def _rms(x, g):
    return x * lax.rsqrt(jnp.mean(x * x, axis=-1, keepdims=True) + EPS) * g


def _gelu(x):
    return 0.5 * x * (1.0 + lax.erf(x * (2.0 ** -0.5)))


def _causal(n):
    return lax.broadcasted_iota(jnp.int32, (n, n), 0) >= lax.broadcasted_iota(jnp.int32, (n, n), 1)


def _row_in_specs(rows, bparams, gparams, tm):
    specs = [pl.BlockSpec((1, tm, w), lambda b, i, cb=cb: (b, i, cb)) for (_, w, cb, _) in rows]
    specs += [pl.BlockSpec((1, 1, p.shape[-1]), lambda b, i: (b, 0, 0)) for p in bparams]
    specs += [pl.BlockSpec(p.shape, lambda b, i, n=p.ndim: (0,) * n) for p in gparams]
    return specs


def _row_vals(refs, n_rows, n_b, n_g):
    vals = [r[0].astype(F32) for r in refs[:n_rows]]
    vals += [r[0].astype(F32) for r in refs[n_rows:n_rows + n_b]]
    vals += [r[...].astype(F32) for r in refs[n_rows + n_b:n_rows + n_b + n_g]]
    return vals


def _row_fwd(name, tile, rows, bparams, gparams, outs, n_sum, tm):
    bl, s = rows[0][0].shape[:2]
    n_in = len(rows) + len(bparams) + len(gparams)

    def body(*refs):
        first = (pl.program_id(0) == 0) & (pl.program_id(1) == 0)
        res = tile(*_row_vals(refs, len(rows), len(bparams), len(gparams)))
        o_refs = refs[n_in:]
        for k in range(len(outs)):
            o_refs[k][0] = res[k].astype(o_refs[k].dtype)
        for k in range(n_sum):
            o_ref, val = o_refs[len(outs) + k], res[len(outs) + k]

            @pl.when(first)
            def _(o_ref=o_ref, val=val):
                o_ref[...] = val

            @pl.when(jnp.logical_not(first))
            def _(o_ref=o_ref, val=val):
                o_ref[...] += val

    out_shape = [_sds((bl, s, w), dt) for (w, dt) in outs] + [_sds((1, 1), F32)] * n_sum
    out_specs = [pl.BlockSpec((1, tm, w), lambda b, i: (b, i, 0)) for (w, _) in outs] + [pl.BlockSpec((1, 1), lambda b, i: (0, 0))] * n_sum
    return _pcall(
        body, name, out_shape, grid=(bl, s // tm), in_specs=_row_in_specs(rows, bparams, gparams, tm), out_specs=out_specs,
        sem=("arbitrary", "arbitrary"),
    )(*[r[0] for r in rows], *bparams, *gparams)


def _row_bwd(name, tile, rows, bparams, gparams, outs, n_sum, tm, cts):
    bl, s = rows[0][0].shape[:2]
    n_r, n_b, n_g = len(rows), len(bparams), len(gparams)
    n_in = n_r + n_b + n_g
    n_ct = len(outs) + n_sum
    grad_rows = [k for k in range(n_r) if rows[k][3]]

    def body(*refs):
        b, i = pl.program_id(0), pl.program_id(1)
        vals = _row_vals(refs, n_r, n_b, n_g)
        ct_refs = refs[n_in:n_in + n_ct]
        ct = [r[0].astype(F32) for r in ct_refs[:len(outs)]] + [r[...] for r in ct_refs[len(outs):]]
        _, vjp = jax.vjp(tile, *vals)
        grads = vjp(tuple(ct))
        o_refs = refs[n_in + n_ct:]
        for j, k in enumerate(grad_rows):
            o_refs[j][0] = grads[k].astype(o_refs[j].dtype)
        for k in range(n_b):
            o_ref, val = o_refs[len(grad_rows) + k], grads[n_r + k]

            @pl.when(i == 0)
            def _(o_ref=o_ref, val=val):
                o_ref[0] = val

            @pl.when(i > 0)
            def _(o_ref=o_ref, val=val):
                o_ref[0] += val

        first = (b == 0) & (i == 0)
        for k in range(n_g):
            o_ref, val = o_refs[len(grad_rows) + n_b + k], grads[n_r + n_b + k]

            @pl.when(first)
            def _(o_ref=o_ref, val=val):
                o_ref[...] = val

            @pl.when(jnp.logical_not(first))
            def _(o_ref=o_ref, val=val):
                o_ref[...] += val

    in_specs = _row_in_specs(rows, bparams, gparams, tm)
    in_specs += [pl.BlockSpec((1, tm, w), lambda b, i: (b, i, 0)) for (w, _) in outs] + [pl.BlockSpec((1, 1), lambda b, i: (0, 0))] * n_sum
    out_shape = [_sds((bl, s, rows[k][1]), rows[k][3]) for k in grad_rows]
    out_shape += [_sds(p.shape, F32) for p in bparams] + [_sds(p.shape, F32) for p in gparams]
    out_specs = [pl.BlockSpec((1, tm, rows[k][1]), lambda b, i: (b, i, 0)) for k in grad_rows]
    out_specs += [pl.BlockSpec((1, 1, p.shape[-1]), lambda b, i: (b, 0, 0)) for p in bparams]
    out_specs += [pl.BlockSpec(p.shape, lambda b, i, n=p.ndim: (0,) * n) for p in gparams]
    res = _pcall(
        body, name, out_shape, grid=(bl, s // tm), in_specs=in_specs, out_specs=out_specs, sem=("arbitrary", "arbitrary"),
    )(*[r[0] for r in rows], *bparams, *gparams, *cts)
    return res[:len(grad_rows)], res[len(grad_rows):len(grad_rows) + n_b], res[len(grad_rows) + n_b:]


def _nm_tile(x, sc, sh, g):
    return (_rms(x, g) * (1.0 + sc) + sh,)


def _nm_res_tile(x, sc, sh, g):
    return x, _rms(x, g) * (1.0 + sc) + sh


def _rnm_tile(x, o, gate, sc, sh, g):
    xn = x + gate * o
    return xn, _rms(xn, g) * (1.0 + sc) + sh


def _final_tile(x, o, tgt, gate, g):
    e = _rms(x + gate * o, g) - tgt
    return (0.5 * jnp.sum(jnp.mean(e * e, axis=-1, keepdims=True), axis=0, keepdims=True),)


def _gm_tile(u_in, v_in, vg, ws, bs_t, og):
    d = u_in.shape[1]
    u, vn = _gelu(u_in), _rms(_gelu(v_in), vg)
    causal = _causal(CHUNK)
    lane = lax.broadcasted_iota(jnp.int32, (1, LANES), 1)
    parts = []
    for h in range(d // GM_HEAD):
        bias = jnp.sum(bs_t * (lane == h).astype(F32), axis=1, keepdims=True)
        parts.append(_dot_nn(jnp.where(causal, ws[h], 0.0), vn[:, h * GM_HEAD:(h + 1) * GM_HEAD]) + bias)
    return (_rms(u * jnp.concatenate(parts, axis=1), og),)


def _shift_down(x, j):
    if j == 0:
        return x
    row = lax.broadcasted_iota(jnp.int32, x.shape, 0)
    return jnp.where(row >= j, pltpu.roll(x, j, 0), 0.0)


def _shift_up(x, j):
    if j == 0:
        return x
    s = x.shape[0]
    row = lax.broadcasted_iota(jnp.int32, x.shape, 0)
    return jnp.where(row < s - j, pltpu.roll(x, s - j, 0), 0.0)


def _conv_pre(x, w_ref, b_ref, taps):
    acc = x * w_ref[taps - 1:taps, :] + b_ref[...]
    for j in range(1, taps):
        acc = acc + _shift_down(x, j) * w_ref[taps - 1 - j:taps - j, :]
    return acc


def _conv_tc(*widths):
    for tc in (512, 256, 128):
        if all(w % tc == 0 for w in widths):
            return tc
    raise ValueError("conv widths must be multiples of the lane count")


def _conv_act_fwd(name, src, col0, chans, w8, b, taps, val_col0, out_dtype):
    bl, s, _ = src.shape
    gated = val_col0 is not None
    tc = _conv_tc(chans, col0, val_col0 or 0)

    def body(*refs):
        if gated:
            x_ref, v_ref, w_ref, b_ref, o_ref = refs
        else:
            x_ref, w_ref, b_ref, o_ref = refs
        pre = _conv_pre(x_ref[0], w_ref, b_ref, taps)
        y = pre * _sigmoid(pre)
        if gated:
            y = y * v_ref[0]
        o_ref[0] = y.astype(o_ref.dtype)

    in_specs = [pl.BlockSpec((1, s, tc), lambda j, bb: (bb, 0, col0 // tc + j))]
    args = [src]
    if gated:
        in_specs.append(pl.BlockSpec((1, s, tc), lambda j, bb: (bb, 0, val_col0 // tc + j)))
        args.append(src)
    in_specs += [pl.BlockSpec((SUBLANES, tc), lambda j, bb: (0, j)), pl.BlockSpec((1, tc), lambda j, bb: (0, j))]
    return _pcall(
        body, name, _sds((bl, s, chans), out_dtype), grid=(chans // tc, bl), in_specs=in_specs,
        out_specs=pl.BlockSpec((1, s, tc), lambda j, bb: (bb, 0, j)), sem=("parallel", "arbitrary"),
    )(*args, w8, b)


def _conv_act_bwd(name, src, col0, chans, w8, b, taps, val_col0, dy, dx_dtype):
    bl, s, _ = src.shape
    gated = val_col0 is not None
    tc = _conv_tc(chans, col0, val_col0 or 0)
    nblk = chans // tc

    def body(*refs):
        if gated:
            x_ref, v_ref, w_ref, b_ref, dy_ref, dx_ref, dw_ref, db_ref = refs
        else:
            x_ref, w_ref, b_ref, dy_ref, dx_ref, dw_ref, db_ref = refs
        bb = pl.program_id(1)
        x = x_ref[0]
        pre = _conv_pre(x, w_ref, b_ref, taps)
        sig = _sigmoid(pre)
        d = dy_ref[0].astype(F32)

        def gate_part():
            dsil = d * v_ref[0] if gated else d
            dpre = dsil * sig * (1.0 + pre * (1.0 - sig))
            dx = dpre * w_ref[taps - 1:taps, :]
            for j in range(1, taps):
                dx = dx + _shift_up(dpre, j) * w_ref[taps - 1 - j:taps - j, :]
            dx_ref[0] = dx.astype(dx_ref.dtype)
            rows = [jnp.sum(dpre * _shift_down(x, taps - 1 - k), axis=0, keepdims=True) for k in range(taps)]
            rows += [jnp.zeros_like(rows[0])] * (SUBLANES - taps)
            dw = jnp.concatenate(rows, axis=0)
            db = jnp.sum(dpre, axis=0, keepdims=True)

            @pl.when(bb == 0)
            def _():
                dw_ref[...] = dw
                db_ref[...] = db

            @pl.when(bb > 0)
            def _():
                dw_ref[...] += dw
                db_ref[...] += db

        if gated:
            part = pl.program_id(2)
            pl.when(part == 0)(gate_part)

            @pl.when(part == 1)
            def _():
                dx_ref[0] = (d * (pre * sig)).astype(dx_ref.dtype)
        else:
            gate_part()

    if gated:
        grid, sem = (nblk, bl, 2), ("parallel", "arbitrary", "arbitrary")
        im = lambda f: (lambda j, bb, p: f(j, bb, p))
    else:
        grid, sem = (nblk, bl), ("parallel", "arbitrary")
        im = lambda f: (lambda j, bb: f(j, bb, 0))
    in_specs = [pl.BlockSpec((1, s, tc), im(lambda j, bb, p: (bb, 0, col0 // tc + j)))]
    args = [src]
    if gated:
        in_specs.append(pl.BlockSpec((1, s, tc), im(lambda j, bb, p: (bb, 0, val_col0 // tc + j))))
        args.append(src)
    in_specs += [
        pl.BlockSpec((SUBLANES, tc), im(lambda j, bb, p: (0, j))), pl.BlockSpec((1, tc), im(lambda j, bb, p: (0, j))),
        pl.BlockSpec((1, s, tc), im(lambda j, bb, p: (bb, 0, j))),
    ]
    out_shape = [_sds((bl, s, chans * (2 if gated else 1)), dx_dtype), _sds((SUBLANES, chans), F32), _sds((1, chans), F32)]
    out_specs = [
        pl.BlockSpec((1, s, tc), im(lambda j, bb, p: (bb, 0, j + p * nblk))), pl.BlockSpec((SUBLANES, tc), im(lambda j, bb, p: (0, j))),
        pl.BlockSpec((1, tc), im(lambda j, bb, p: (0, j))),
    ]
    return _pcall(body, name, out_shape, grid=grid, in_specs=in_specs, out_specs=out_specs, sem=sem)(*args, w8, b, dy)


def _ssd_chunk(g_idx, states, xs, bm, cm, dtr, z, dtb, alog, dsk, ng):
    gw = xs.shape[1]
    hpg = gw // HEAD_DIM
    dt = jax.nn.softplus(dtr + dtb)
    da = dt * (-jnp.exp(alog))
    causal = _causal(CHUNK)
    acs = _sel_left(causal.astype(F32), da)
    head_of_col = g_idx * hpg + lax.broadcasted_iota(jnp.int32, (LANES, gw), 1) // HEAD_DIM
    expand = (lax.broadcasted_iota(jnp.int32, (LANES, gw), 0) == head_of_col).astype(F32)
    dt_e, acs_e = _sel_right(dt, expand), _sel_right(acs, expand)
    last = lax.broadcasted_iota(jnp.int32, (CHUNK, gw), 0) == CHUNK - 1
    alast_e = jnp.sum(jnp.where(last, acs_e, 0.0), axis=0, keepdims=True)
    xc = xs * dt_e
    xc_st = xc * jnp.exp(alast_e - acs_e)
    decay_out, chunk_decay = jnp.exp(acs_e), jnp.exp(alast_e)
    cb = _dot_nt(cm, bm)
    acs_t = acs.T
    lane = lax.broadcasted_iota(jnp.int32, (1, LANES), 1)
    sub = lax.broadcasted_iota(jnp.int32, (LANES, 1), 0)
    ys, new_states = [], []
    for p in range(gw // LANES):
        sl = slice(p * LANES, (p + 1) * LANES)
        xcp = xc[:, sl]
        y = _dot_nn(cm, states[p]) * decay_out[:, sl]
        for q in range(2):
            head = g_idx * hpg + 2 * p + q
            col = jnp.sum(acs * (lane == head).astype(F32), axis=1, keepdims=True)
            row = jnp.sum(acs_t * (sub == head).astype(F32), axis=0, keepdims=True)
            decay = jnp.where(causal, jnp.exp(jnp.where(causal, col - row, 0.0)), 0.0)
            half = ((lane // HEAD_DIM) == q).astype(F32)
            y = y + _dot_nn(cb * decay, xcp * half)
        ys.append(y)
        new_states.append(states[p] * chunk_decay[:, sl] + _dot_tn(bm, xc_st[:, sl]))
    y = jnp.concatenate(ys, axis=1) + dsk * xs
    gated = y * (z * _sigmoid(z))
    return tuple(new_states), _rms(gated, ng)


def _ssd_specs(d, gw, dt_col, rev, nc):
    ci = (lambda i: nc - 1 - i) if rev else (lambda i: i)
    return [
        pl.BlockSpec((1, CHUNK, gw), lambda g, b, i: (b, ci(i), g)),
        pl.BlockSpec((1, CHUNK, STATE), lambda g, b, i: (b, ci(i), d // STATE + g)),
        pl.BlockSpec((1, CHUNK, STATE), lambda g, b, i: (b, ci(i), d // STATE + SSD_GROUPS + g)),
        pl.BlockSpec((1, CHUNK, LANES), lambda g, b, i: (b, ci(i), dt_col // LANES)),
        pl.BlockSpec((1, CHUNK, gw), lambda g, b, i: (b, ci(i), g)),
        pl.BlockSpec((1, LANES), lambda g, b, i: (0, 0)), pl.BlockSpec((1, LANES), lambda g, b, i: (0, 0)),
        pl.BlockSpec((1, gw), lambda g, b, i: (0, g)), pl.BlockSpec((1, gw), lambda g, b, i: (0, g)),
    ]


def _ssd_fwd(name, xbc, proj, dt_col, dtb, alog, dsk, ng):
    bl, s, cd = xbc.shape
    d = dsk.shape[1]
    gw, nc = d // SSD_GROUPS, s // CHUNK
    npair = gw // LANES

    def body(xs_ref, bm_ref, cm_ref, dt_ref, z_ref, dtb_ref, alog_ref, dsk_ref, ng_ref, y_ref, hp_ref, st_ref):
        g, i = pl.program_id(0), pl.program_id(2)

        @pl.when(i == 0)
        def _():
            st_ref[...] = jnp.zeros_like(st_ref)

        states = tuple(st_ref[p] for p in range(npair))
        hp_ref[0, 0, 0] = st_ref[...]
        new_states, yn = _ssd_chunk(g, states, xs_ref[0], bm_ref[0], cm_ref[0], dt_ref[0], z_ref[0], dtb_ref[...], alog_ref[...], dsk_ref[...], ng_ref[...])
        for p in range(npair):
            st_ref[p] = new_states[p]
        y_ref[0] = yn.astype(y_ref.dtype)

    out_shape = [_sds((bl, s, d), BF16), _sds((SSD_GROUPS, bl, nc, npair, STATE, LANES), F32)]
    out_specs = [
        pl.BlockSpec((1, CHUNK, gw), lambda g, b, i: (b, i, g)),
        pl.BlockSpec((1, 1, 1, npair, STATE, LANES), lambda g, b, i: (g, b, i, 0, 0, 0)),
    ]
    return _pcall(
        body, name, out_shape, grid=(SSD_GROUPS, bl, nc), in_specs=_ssd_specs(d, gw, dt_col, False, nc), out_specs=out_specs,
        scratch=[pltpu.VMEM((npair, STATE, LANES), F32)], sem=("arbitrary", "arbitrary", "arbitrary"),
    )(xbc, xbc, xbc, proj, proj, dtb, alog, dsk, ng)


def _ssd_bwd(name, xbc, proj, dt_col, dtb, alog, dsk, ng, hprev, dy):
    bl, s, cd = xbc.shape
    d = dsk.shape[1]
    gw, nc = d // SSD_GROUPS, s // CHUNK
    npair = gw // LANES

    def body(xs_ref, bm_ref, cm_ref, dt_ref, z_ref, dtb_ref, alog_ref, dsk_ref, ng_ref, hp_ref, dy_ref,
             dxs_ref, dbm_ref, dcm_ref, ddt_ref, dz_ref, ddtb_ref, dalog_ref, ddsk_ref, dng_ref, dst_ref):
        g, b, i = pl.program_id(0), pl.program_id(1), pl.program_id(2)

        @pl.when(i == 0)
        def _():
            dst_ref[...] = jnp.zeros_like(dst_ref)

        states = tuple(hp_ref[0, 0, 0, p] for p in range(npair))
        step = functools.partial(_ssd_chunk, g)
        _, vjp = jax.vjp(step, states, xs_ref[0], bm_ref[0], cm_ref[0], dt_ref[0], z_ref[0], dtb_ref[...], alog_ref[...], dsk_ref[...], ng_ref[...])
        d_states, dxs, dbm, dcm, ddt, dz, ddtb, dalog, ddsk, dng = vjp((tuple(dst_ref[p] for p in range(npair)), dy_ref[0].astype(F32)))
        for p in range(npair):
            dst_ref[p] = d_states[p]
        dxs_ref[0], dbm_ref[0], dcm_ref[0] = dxs, dbm, dcm
        dz_ref[0] = dz.astype(dz_ref.dtype)
        ddt_ref[0, 0] = ddt
        first_g = (b == 0) & (i == 0)
        first = first_g & (g == 0)
        for o_ref, val, init in ((ddtb_ref, ddtb, first), (dalog_ref, dalog, first), (ddsk_ref, ddsk, first_g), (dng_ref, dng, first_g)):
            @pl.when(init)
            def _(o_ref=o_ref, val=val):
                o_ref[...] = val

            @pl.when(jnp.logical_not(init))
            def _(o_ref=o_ref, val=val):
                o_ref[...] += val

    rc = lambda i: nc - 1 - i
    in_specs = _ssd_specs(d, gw, dt_col, True, nc) + [
        pl.BlockSpec((1, 1, 1, npair, STATE, LANES), lambda g, b, i: (g, b, rc(i), 0, 0, 0)),
        pl.BlockSpec((1, CHUNK, gw), lambda g, b, i: (b, rc(i), g)),
    ]
    out_shape = [
        _sds((bl, s, d), F32), _sds((bl, s, SSD_GROUPS * STATE), F32), _sds((bl, s, SSD_GROUPS * STATE), F32),
        _sds((SSD_GROUPS, bl, s, LANES), F32), _sds((bl, s, d), BF16),
        _sds((1, LANES), F32), _sds((1, LANES), F32), _sds((1, d), F32), _sds((1, d), F32),
    ]
    out_specs = [
        pl.BlockSpec((1, CHUNK, gw), lambda g, b, i: (b, rc(i), g)),
        pl.BlockSpec((1, CHUNK, STATE), lambda g, b, i: (b, rc(i), g)), pl.BlockSpec((1, CHUNK, STATE), lambda g, b, i: (b, rc(i), g)),
        pl.BlockSpec((1, 1, CHUNK, LANES), lambda g, b, i: (g, b, rc(i), 0)),
        pl.BlockSpec((1, CHUNK, gw), lambda g, b, i: (b, rc(i), g)),
        pl.BlockSpec((1, LANES), lambda g, b, i: (0, 0)), pl.BlockSpec((1, LANES), lambda g, b, i: (0, 0)),
        pl.BlockSpec((1, gw), lambda g, b, i: (0, g)), pl.BlockSpec((1, gw), lambda g, b, i: (0, g)),
    ]
    return _pcall(
        body, name, out_shape, grid=(SSD_GROUPS, bl, nc), in_specs=in_specs, out_specs=out_specs,
        scratch=[pltpu.VMEM((npair, STATE, LANES), F32)], sem=("arbitrary", "arbitrary", "arbitrary"),
    )(xbc, xbc, xbc, proj, proj, dtb, alog, dsk, ng, hprev, dy)


def _mixer_fwd(name, proj, d, cd, conv_w8, conv_b, dtb, alog, dsk, ng, vg, ws, bs_t, og):
    xbc = _conv_act_fwd(name + "_conv_fwd", proj, 3 * d, cd, conv_w8, conv_b, SSD_CONV, None, F32)
    y, hprev = _ssd_fwd(name + "_ssd_fwd", xbc, proj, 3 * d + cd, dtb, alog, dsk, ng)
    rows = [(proj, d, 1, BF16), (proj, d, 2, BF16)]
    (g_out,) = _row_fwd(name + "_gm_fwd", _gm_tile, rows, [], [vg, ws, bs_t, og], [(d, BF16)], 0, CHUNK)
    return jnp.concatenate([y, g_out], axis=-1), (xbc, hprev)


def _mixer_bwd(name, proj, d, cd, conv_w8, conv_b, dtb, alog, dsk, ng, vg, ws, bs_t, og, xbc, hprev, dycat):
    bl, s, n_proj = proj.shape
    dt_col = 3 * d + cd
    dy, dg_out = dycat[..., :d], dycat[..., d:]
    dxs, dbm, dcm, ddt2, dz, ddtb, dalog, ddsk, dng = _ssd_bwd(name + "_ssd_bwd", xbc, proj, dt_col, dtb, alog, dsk, ng, hprev, dy)
    dxbc_act = jnp.concatenate([dxs, dbm, dcm], axis=-1)
    dxbc, dw8, dcb = _conv_act_bwd(name + "_conv_bwd", proj, 3 * d, cd, conv_w8, conv_b, SSD_CONV, None, dxbc_act, BF16)
    rows = [(proj, d, 1, BF16), (proj, d, 2, BF16)]
    (du, dv), _, (dvg, dws, dbs_t, dog) = _row_bwd(name + "_gm_bwd", _gm_tile, rows, [], [vg, ws, bs_t, og], [(d, BF16)], 0, CHUNK, [dg_out])
    ddt = (ddt2[0] + ddt2[1]).astype(BF16)
    pad = jnp.zeros((bl, s, n_proj - dt_col - LANES), BF16)
    dproj = jnp.concatenate([dz, du, dv, dxbc, ddt, pad], axis=-1)
    return dproj, (dw8, dcb, ddtb, dalog, ddsk, dng, dvg, dws, dbs_t, dog)


def _position():
    return lax.axis_index("x"), lax.axis_index("y"), lax.axis_index("c")


def _at(ref, idx):
    return ref.at[idx] if len(idx) else ref


def _exchange(name, inputs, out_shapes, plan):
    n_in, n_out = len(inputs), len(out_shapes)
    n_copy = len(plan(0, 0, 0))

    def body(*refs):
        in_refs, out_refs = refs[:n_in], refs[n_in:n_in + n_out]
        send_sems, recv_sems = refs[n_in + n_out:]
        x, y, c = _position()
        copies = plan(x, y, c)

        def copy(k, src, dst, peer):
            return pltpu.make_async_remote_copy(src_ref=src, dst_ref=dst, send_sem=send_sems.at[k], recv_sem=recv_sems.at[k], device_id=peer, device_id_type=MESH)

        sends = [copy(k, _at(in_refs[sa], si), _at(out_refs[da], di), peer) for k, (sa, si, da, di, peer, _) in enumerate(copies)]
        for cp in sends:
            cp.start()
        for k, (sa, si, da, _, peer, li) in enumerate(copies):
            copy(k, _at(in_refs[sa], si), _at(out_refs[da], li), peer).wait_recv()
        for cp in sends:
            cp.wait_send()

    any_spec = pl.BlockSpec(memory_space=pl.ANY)
    outs = pl.pallas_call(
        body, name=name, out_shape=[_sds(s, dt) for s, dt in out_shapes], in_specs=[any_spec] * n_in, out_specs=[any_spec] * n_out,
        scratch_shapes=[pltpu.SemaphoreType.DMA((n_copy,)), pltpu.SemaphoreType.DMA((n_copy,))],
    )(*inputs)
    return list(outs)


def _chip_peers(x, y, c):
    return [(1 - x, y, c), (x, 1 - y, c), (1 - x, 1 - y, c)]


def _chip_of(p):
    return 2 * p[0] + p[1]


def _set_slot(slots, me, blk):
    return lax.dynamic_update_slice(slots, blk[None], (me,) + (0,) * blk.ndim)


def _by_core(c, mine, other, axis):
    return jnp.where(c == 0, jnp.stack([mine, other], axis), jnp.stack([other, mine], axis))


def _gather_chips(name, blocks):
    n = len(blocks)

    def plan(x, y, c):
        return [(a, (), a, (2 * x + y,), p, (_chip_of(p),)) for a in range(n) for p in _chip_peers(x, y, c)]

    recv = _exchange(name, blocks, [((4,) + b.shape, b.dtype) for b in blocks], plan)
    x, y, _ = _position()
    return [_set_slot(r, 2 * x + y, b) for r, b in zip(recv, blocks)]


def _gather_two_level(name, blocks):
    n = len(blocks)

    def plan_chips(x, y, c):
        return [(a, (), a, (2 * x + y,), p, (_chip_of(p),)) for a in range(n) for p in _chip_peers(x, y, c)]

    def plan_cores(x, y, c):
        me, sib = 2 * x + y, (x, y, 1 - c)
        own = [(a, (), a, (me,), sib, (me,)) for a in range(n)]
        passed = [(n + a, (_chip_of(p),), a, (_chip_of(p),), sib, (_chip_of(p),)) for a in range(n) for p in _chip_peers(x, y, c)]
        return own + passed

    shapes = [((4,) + b.shape, b.dtype) for b in blocks]
    from_chips = _exchange(name + "_chips", blocks, shapes, plan_chips)
    from_core = _exchange(name + "_cores", list(blocks) + from_chips, shapes, plan_cores)
    x, y, c = _position()
    return [_by_core(c, _set_slot(r1, 2 * x + y, b), r2, 1) for b, r1, r2 in zip(blocks, from_chips, from_core)]


def _pair_add(name, g42, r4):
    _, _, rh, cols = g42.shape
    tr = _divisors(rh, 512, SUBLANES * 2)[-1]

    def body(c_ref, a_ref, b_ref, o_ref):
        o_ref[0] = (a_ref[0, 0].astype(F32) + b_ref[0].astype(F32)).astype(o_ref.dtype)

    cidx = lax.axis_index("c").astype(jnp.int32).reshape(1)
    return _pcall(
        body, name, _sds(r4.shape, BF16), grid=(4, rh // tr),
        in_specs=[pl.BlockSpec((1, 1, tr, cols), lambda s, i, c_ref: (s, c_ref[0], i, 0)), pl.BlockSpec((1, tr, cols), lambda s, i, c_ref: (s, i, 0))],
        out_specs=pl.BlockSpec((1, tr, cols), lambda s, i, c_ref: (s, i, 0)), sem=("parallel", "parallel"), prefetch=1,
    )(cidx, g42, r4)


def _slot_sum(name, parts):
    n, r, cols = parts.shape
    cap = max(2 * SUBLANES, (4 * 1024 * 1024) // (n * cols * parts.dtype.itemsize))
    tr = _divisors(r, cap, 2 * SUBLANES)[-1]

    def body(p_ref, o_ref):
        acc = p_ref[0].astype(F32)
        for k in range(1, n):
            acc = acc + p_ref[k].astype(F32)
        o_ref[...] = acc

    return _pcall(
        body, name, _sds((r, cols), F32), grid=(r // tr,), in_specs=[pl.BlockSpec((n, tr, cols), lambda i: (0, i, 0))],
        out_specs=pl.BlockSpec((tr, cols), lambda i: (i, 0)), sem=("parallel",),
    )(parts)


def _slot_sums(name, parts):
    k = len(parts)

    def body(*refs):
        for p_ref, o_ref in zip(refs[:k], refs[k:]):
            acc = p_ref[0]
            for j in range(1, p_ref.shape[0]):
                acc = acc + p_ref[j]
            o_ref[...] = acc

    return list(_pcall(body, name, [_sds(p.shape[1:], F32) for p in parts])(*parts))


def _reduce_two_level(name, grads):
    n = len(grads)
    x, y, c = _position()
    me = 2 * x + y

    def plan_swap(x, y, c):
        return [(a, (s, 1 - c), a, (s,), (x, y, 1 - c), (s,)) for a in range(n) for s in range(4)]

    def plan_chips(x, y, c):
        return [(a, (_chip_of(p),), a, (2 * x + y,), p, (_chip_of(p),)) for a in range(n) for p in _chip_peers(x, y, c)]

    def plan_share(x, y, c):
        return [(a, (), a, (), (x, y, 1 - c), ()) for a in range(n)]

    other = _exchange(name + "_cores", grads, [((4,) + g.shape[2:], g.dtype) for g in grads], plan_swap)
    pair = [_pair_add(f"{name}_pair{a}", g, o) for a, (g, o) in enumerate(zip(grads, other))]
    recv = _exchange(name + "_chips", pair, [(p.shape, p.dtype) for p in pair], plan_chips)
    parts = [lax.dynamic_update_slice(r, lax.dynamic_slice_in_dim(p, me, 1, 0), (me, 0, 0)) for r, p in zip(recv, pair)]
    mine = [_slot_sum(f"{name}_sum{a}", p) for a, p in enumerate(parts)]
    theirs = _exchange(name + "_share", mine, [(m.shape, m.dtype) for m in mine], plan_share)
    return [_by_core(c, m, t, 0) for m, t in zip(mine, theirs)]


def _gather_weights(name, shards):
    c = lax.axis_index("c")
    halves = [lax.dynamic_slice_in_dim(w, c * (w.shape[0] // 2), w.shape[0] // 2, 0).astype(BF16) for w in shards]
    return [f.reshape((4,) + w.shape) for f, w in zip(_gather_two_level(name + "_ag", halves), shards)]


def _reduce_weights(name, grads):
    g42 = [g.reshape(4, 2, g.shape[1] // 2, g.shape[2]) for g in grads]
    return [r.reshape(g.shape[1:]) for r, g in zip(_reduce_two_level(name + "_rs", g42), grads)]


def _allreduce_small(name, grads):
    allg = [g.reshape((8,) + g.shape[2:]) for g in _gather_two_level(name + "_ag", list(grads))]
    return _slot_sums(name + "_sum", allg)


def _ada_fwd_call(name, c_all, w, b_shard):
    nl, d, ns = w.shape
    nb = c_all.shape[0]

    def body(c_ref, w_ref, b_ref, o_ref):
        cv = c_ref[...]
        o_ref[0] = _dg(cv * _sigmoid(cv), w_ref[0], ((1,), (0,))) + b_ref[0]

    return _pcall(
        body, name, _sds((nl, nb, ns), F32), grid=(nl,),
        in_specs=[pl.BlockSpec((nb, d), lambda l: (0, 0)), pl.BlockSpec((1, d, ns), lambda l: (l, 0, 0)), pl.BlockSpec((1, 1, ns), lambda l: (l, 0, 0))],
        out_specs=pl.BlockSpec((1, nb, ns), lambda l: (l, 0, 0)), sem=("parallel",),
    )(c_all, w, b_shard)


def _ada_bwd_call(name, c_all, dm_shard, dm_all):
    nl, nb, ns = dm_shard.shape
    d = c_all.shape[1]
    nm = dm_all.shape[2]

    def body(c_ref, ds_ref, da_ref, dw_ref, db_ref):
        cv = c_ref[...]
        dw_ref[0] = _dg(cv * _sigmoid(cv), ds_ref[0], ((0,), (0,)))
        db_ref[0] = jnp.sum(da_ref[0], axis=0, keepdims=True)

    return _pcall(
        body, name, [_sds((nl, d, ns), F32), _sds((nl, 1, nm), F32)], grid=(nl,),
        in_specs=[pl.BlockSpec((nb, d), lambda l: (0, 0)), pl.BlockSpec((1, nb, ns), lambda l: (l, 0, 0)), pl.BlockSpec((1, nb, nm), lambda l: (l, 0, 0))],
        out_specs=[pl.BlockSpec((1, d, ns), lambda l: (l, 0, 0)), pl.BlockSpec((1, 1, nm), lambda l: (l, 0, 0))], sem=("parallel",),
    )(c_all, dm_shard, dm_all)


def _make_ada(name, bl):
    def my_chip():
        return 2 * lax.axis_index("x") + lax.axis_index("y")

    def my_dev():
        return 2 * my_chip() + lax.axis_index("c")

    @jax.custom_vjp
    def ada(c_all, w, b):
        nl, d, ns = w.shape
        b_shard = lax.dynamic_slice(b, (0, my_chip() * ns), (nl, ns)).reshape(nl, 1, ns)
        shard = _ada_fwd_call(name + "_fwd", c_all, w, b_shard)
        (allc,) = _gather_chips(name + "_ag", [shard])
        mods = jnp.transpose(allc, (1, 2, 0, 3)).reshape(nl, c_all.shape[0], 4 * ns)
        return lax.dynamic_slice(mods, (0, my_dev() * bl, 0), (nl, bl, 4 * ns))

    def fwd(c_all, w, b):
        return ada(c_all, w, b), (c_all, w.shape)

    def bwd(res, dm):
        c_all, (nl, d, ns) = res
        (dm_all,) = _gather_two_level(name + "_bwd_ag", [dm])
        dm_all = jnp.transpose(dm_all.reshape((8,) + dm.shape), (1, 0, 2, 3)).reshape(nl, 8 * bl, 4 * ns)
        dm_shard = lax.dynamic_slice(dm_all, (0, 0, my_chip() * ns), (nl, 8 * bl, ns))
        dw, db = _ada_bwd_call(name + "_bwd", c_all, dm_shard, dm_all)
        return jnp.zeros_like(c_all), dw, db.reshape(nl, 4 * ns)

    ada.defvjp(fwd, bwd)
    return ada


def _adamw(name, w, g, m, v):
    shape = w.shape
    cols = shape[-1]
    w2, g2, m2, v2 = (t.reshape(-1, cols) for t in (w, g, m, v))
    rows = w2.shape[0]
    cap = max(SUBLANES, (512 * 1024) // max(cols, 1) // SUBLANES * SUBLANES)
    tr = _divisors(rows, cap, SUBLANES)[-1]

    def body(w_ref, g_ref, m_ref, v_ref, d_ref, mo_ref, vo_ref):
        gv = g_ref[...]
        mn = ADAM_B1 * m_ref[...] + (1.0 - ADAM_B1) * gv
        vn = ADAM_B2 * v_ref[...] + (1.0 - ADAM_B2) * (gv * gv)
        m_hat = mn / (1.0 - ADAM_B1 ** ADAM_STEP)
        v_hat = vn / (1.0 - ADAM_B2 ** ADAM_STEP)
        d_ref[...] = -ADAM_LR * (m_hat / (jnp.sqrt(v_hat) + ADAM_EPS) + ADAM_WD * w_ref[...])
        mo_ref[...] = mn
        vo_ref[...] = vn

    spec = pl.BlockSpec((tr, cols), lambda i: (i, 0))
    outs = _pcall(body, name, [_sds((rows, cols), F32)] * 3, grid=(rows // tr,), in_specs=[spec] * 4, out_specs=[spec] * 3, sem=("parallel",))(w2, g2, m2, v2)
    return tuple(o.reshape(shape) for o in outs)


def _adamw_small(name, ws, gs, ms, vs):
    n = len(ws)

    def body(*refs):
        for k in range(n):
            w_ref, g_ref, m_ref, v_ref = (refs[j * n + k] for j in range(4))
            d_ref, mo_ref, vo_ref = (refs[(4 + j) * n + k] for j in range(3))
            gv = g_ref[...]
            mn = ADAM_B1 * m_ref[...] + (1.0 - ADAM_B1) * gv
            vn = ADAM_B2 * v_ref[...] + (1.0 - ADAM_B2) * (gv * gv)
            m_hat = mn / (1.0 - ADAM_B1 ** ADAM_STEP)
            v_hat = vn / (1.0 - ADAM_B2 ** ADAM_STEP)
            d_ref[...] = -ADAM_LR * (m_hat / (jnp.sqrt(v_hat) + ADAM_EPS) + ADAM_WD * w_ref[...])
            mo_ref[...] = mn
            vo_ref[...] = vn

    outs = _pcall(body, name, [_sds(w.shape, F32) for w in ws] * 3)(*ws, *gs, *ms, *vs)
    return outs[:n], outs[n:2 * n], outs[2 * n:]


def _pad_rows(w, rows):
    return jnp.pad(w, ((0, rows - w.shape[0]), (0, 0)))


def _pad_lanes(v):
    return jnp.pad(v, (0, LANES - v.shape[0])).reshape(1, LANES)


BIG = ("w_in", "w_out", "ff_up", "ff_down")
BIG_AXIS = {"w_in": 1, "w_out": 0, "ff_up": 1, "ff_down": 0}
CONVW = ("ssd_conv_w", "ff_conv_w")
SMALL = ("norm1_g", "norm2_g", "ssd_conv_b", "ssd_dt_bias", "ssd_a_log", "ssd_d", "ssd_norm_g", "gm_vnorm_g", "gm_ws", "gm_bs", "gm_out_g", "ff_conv_b")
WEIGHTS = ("ada_w", "ada_b", "norm1_g", "norm2_g", "w_in", "ssd_conv_w", "ssd_conv_b", "ssd_dt_bias", "ssd_a_log", "ssd_d", "ssd_norm_g", "gm_vnorm_g", "gm_ws", "gm_bs", "gm_out_g", "w_out", "ff_up", "ff_conv_w", "ff_conv_b", "ff_down", "final_g")


def kernel(x, c, ada_w, ada_b, norm1_g, norm2_g, w_in, ssd_conv_w, ssd_conv_b, ssd_dt_bias, ssd_a_log, ssd_d, ssd_norm_g, gm_vnorm_g, gm_ws, gm_bs, gm_out_g, w_out, ff_up, ff_conv_w, ff_conv_b, ff_down, final_g, loss_target, m_ada_w, m_ada_b, m_norm1_g, m_norm2_g, m_w_in, m_ssd_conv_w, m_ssd_conv_b, m_ssd_dt_bias, m_ssd_a_log, m_ssd_d, m_ssd_norm_g, m_gm_vnorm_g, m_gm_ws, m_gm_bs, m_gm_out_g, m_w_out, m_ff_up, m_ff_conv_w, m_ff_conv_b, m_ff_down, m_final_g, v_ada_w, v_ada_b, v_norm1_g, v_norm2_g, v_w_in, v_ssd_conv_w, v_ssd_conv_b, v_ssd_dt_bias, v_ssd_a_log, v_ssd_d, v_ssd_norm_g, v_gm_vnorm_g, v_gm_ws, v_gm_bs, v_gm_out_g, v_w_out, v_ff_up, v_ff_conv_w, v_ff_conv_b, v_ff_down, v_final_g):
    given = dict(locals())
    weights = {n: given[n] for n in WEIGHTS}
    bl, s, d = x.shape
    nl = ada_w.shape[0]
    heads = d // HEAD_DIM
    cd = d + 2 * SSD_GROUPS * STATE
    f = ff_down.shape[1] * 4
    n_in = d + cd + heads + 2 * d
    n_proj = _round_up(3 * d + cd + LANES, 2 * LANES)
    tm = _divisors(s, 512)[-1]

    pre = _gather_two_level("pre_ag", [c] + [weights[n] for n in CONVW])
    c_all = pre[0].reshape(8 * bl, d)
    conv_full = [jnp.concatenate([p[k, 0] for k in range(4)], axis=-1) for p in pre[1:]]
    conv_full = dict(zip(CONVW, conv_full))
    mods_all, ada_vjp = jax.vjp(functools.partial(_make_ada("ada", bl), c_all), ada_w, ada_b)
    mods = mods_all.reshape(nl, bl, N_MOD, 1, d)
    r2 = lambda v: v.reshape(1, -1)
    n_gm = d // GM_HEAD
    fg = r2(final_g)

    def layer_params(l):
        return dict(
            norm1=r2(norm1_g[l]), norm2=r2(norm2_g[l]), conv_w8=_pad_rows(conv_full["ssd_conv_w"][l], SUBLANES), conv_b=r2(ssd_conv_b[l]),
            dtb=_pad_lanes(ssd_dt_bias[l]), alog=_pad_lanes(ssd_a_log[l]), dsk=r2(jnp.repeat(ssd_d[l], HEAD_DIM)), ng=r2(ssd_norm_g[l]),
            vg=r2(gm_vnorm_g[l]), ws=gm_ws[l], bs_t=jnp.pad(gm_bs[l].T, ((0, 0), (0, LANES - n_gm))), og=r2(gm_out_g[l]),
            ff_w8=_pad_rows(conv_full["ff_conv_w"][l], SUBLANES), ff_b=r2(ff_conv_b[l]))

    def mixer_args(p):
        return (p["conv_w8"], p["conv_b"], p["dtb"], p["alog"], p["dsk"], p["ng"], p["vg"], p["ws"], p["bs_t"], p["og"])

    def layer_weights(l):
        full = _gather_weights(f"wg{l}", [weights[n][l] for n in BIG])
        w = {n: t.reshape(-1, t.shape[2]) if BIG_AXIS[n] == 0 else jnp.concatenate([t[k] for k in range(4)], axis=1) for n, t in zip(BIG, full)}
        wi = w["w_in"]
        w["w_in"] = jnp.concatenate(
            [wi[:, :d], wi[:, d + cd + heads:], wi[:, d:d + cd], wi[:, d + cd:d + cd + heads], jnp.zeros((d, n_proj - (3 * d + cd + heads)), BF16)], axis=1)
        return w

    saved, xcur, pending = [], x, None
    for l in range(nl):
        p, w = layer_params(l), layer_weights(l)
        sh1, sc1, g1, sh2, sc2, g2 = (mods[l, :, k] for k in range(N_MOD))
        if pending is None:
            (h,) = _row_fwd(f"nm{l}_fwd", _nm_tile, [(xcur, d, 0, None)], [sc1, sh1], [p["norm1"]], [(d, BF16)], 0, tm)
            x1 = xcur
        else:
            rows = [(xcur, d, 0, None), (pending[0], d, 0, None)]
            x1, h = _row_fwd(f"rnm{l}a_fwd", _rnm_tile, rows, [pending[1], sc1, sh1], [p["norm1"]], [(d, F32), (d, BF16)], 0, tm)
        proj = _mm(f"win{l}_fwd", h, w["w_in"], F32)
        ycat, (xbc, hprev) = _mixer_fwd(f"mix{l}", proj, d, cd, *mixer_args(p))
        mix = _mm(f"wout{l}_fwd", ycat, w["w_out"], F32)
        x2, h2 = _row_fwd(f"rnm{l}b_fwd", _rnm_tile, [(x1, d, 0, None), (mix, d, 0, None)], [g1, sc2, sh2], [p["norm2"]], [(d, F32), (d, BF16)], 0, tm)
        up = _mm(f"ffup{l}_fwd", h2, w["ff_up"], F32)
        act = _conv_act_fwd(f"ffact{l}_fwd", up, 0, f, p["ff_w8"], p["ff_b"], FF_CONV, f, BF16)
        down = _mm(f"ffdown{l}_fwd", act, w["ff_down"], F32)
        saved.append(dict(p=p, w=w, x_in=xcur, pending=pending, h=h, proj=proj, xbc=xbc, hprev=hprev, ycat=ycat, x1=x1, mix=mix, h2=h2, up=up, act=act))
        xcur, pending = x2, (down, g2)
    rows = [(xcur, d, 0, F32), (pending[0], d, 0, BF16), (loss_target, d, 0, None)]
    (loss_local,) = _row_fwd("final_fwd", _final_tile, rows, [pending[1]], [fg], [], 1, tm)
    loss = lax.psum(loss_local[0, 0], AXES)

    (dx2, ddown), (dg2,), (dfg,) = _row_bwd("final_bwd", _final_tile, rows, [pending[1]], [fg], [], 1, tm, [jnp.ones((1, 1), F32)])
    dmods, small_grads, big_grads = [None] * nl, [None] * nl, [None] * nl
    for l in reversed(range(nl)):
        sv = saved[l]
        p, w = sv["p"], sv["w"]
        sh1, sc1, g1, sh2, sc2, g2 = (mods[l, :, k] for k in range(N_MOD))
        dact, dw_down = _mm_bwd(f"ffdown{l}", sv["act"], w["ff_down"], ddown)
        dup, dff_w8, dff_b = _conv_act_bwd(f"ffact{l}_bwd", sv["up"], 0, f, p["ff_w8"], p["ff_b"], FF_CONV, f, dact, BF16)
        dh2, dw_up = _mm_bwd(f"ffup{l}", sv["h2"], w["ff_up"], dup)
        rows = [(sv["x1"], d, 0, F32), (sv["mix"], d, 0, BF16)]
        (dx1, dmix), (dg1, dsc2, dsh2), (dn2,) = _row_bwd(f"rnm{l}b_bwd", _rnm_tile, rows, [g1, sc2, sh2], [p["norm2"]], [(d, F32), (d, BF16)], 0, tm, [dx2, dh2])
        dycat, dw_out = _mm_bwd(f"wout{l}", sv["ycat"], w["w_out"], dmix)
        dproj, (dw8, dcb, ddtb, dalog, ddsk, dng, dvg, dws, dbs_t, dog) = _mixer_bwd(f"mix{l}", sv["proj"], d, cd, *mixer_args(p), sv["xbc"], sv["hprev"], dycat)
        dh, dw_in_p = _mm_bwd(f"win{l}", sv["h"], w["w_in"], dproj)
        if sv["pending"] is None:
            (dx2,), (dsc1, dsh1), (dn1,) = _row_bwd(f"nm{l}_bwd", _nm_res_tile, [(sv["x_in"], d, 0, F32)], [sc1, sh1], [p["norm1"]], [(d, F32), (d, BF16)], 0, tm, [dx1, dh])
        else:
            rows = [(sv["x_in"], d, 0, F32), (sv["pending"][0], d, 0, BF16)]
            (dx2, ddown), (dg2_prev, dsc1, dsh1), (dn1,) = _row_bwd(
                f"rnm{l}a_bwd", _rnm_tile, rows, [sv["pending"][1], sc1, sh1], [p["norm1"]], [(d, F32), (d, BF16)], 0, tm, [dx1, dh])
        dmods[l] = jnp.concatenate([dsh1, dsc1, dg1, dsh2, dsc2, dg2], axis=1).reshape(bl, N_MOD * d)
        if sv["pending"] is not None:
            dg2 = dg2_prev
        small_grads[l] = dict(
            norm1_g=dn1.reshape(d), norm2_g=dn2.reshape(d), ssd_conv_b=dcb.reshape(cd), ssd_dt_bias=ddtb[0, :heads], ssd_a_log=dalog[0, :heads],
            ssd_d=ddsk.reshape(heads, HEAD_DIM).sum(-1), ssd_norm_g=dng.reshape(d), gm_vnorm_g=dvg.reshape(d), gm_ws=dws, gm_bs=dbs_t[:, :n_gm].T,
            gm_out_g=dog.reshape(d), ff_conv_b=dff_b.reshape(f), ssd_conv_w=dw8[:SSD_CONV], ff_conv_w=dff_w8[:FF_CONV])
        dw_in = jnp.concatenate([dw_in_p[:, :d], dw_in_p[:, 3 * d:3 * d + cd], dw_in_p[:, 3 * d + cd:3 * d + cd + heads], dw_in_p[:, d:3 * d]], axis=1)
        by_chip = {}
        for n, g in zip(BIG, (dw_in, dw_out, dw_up, dw_down)):
            width = g.shape[1] // 4
            by_chip[n] = g.reshape(4, g.shape[0] // 4, g.shape[1]) if BIG_AXIS[n] == 0 else jnp.stack([g[:, k * width:(k + 1) * width] for k in range(4)])
        big_grads[l] = _reduce_weights(f"wg{l}", [by_chip[n] for n in BIG])
    grad_x = dx2

    g_ada_w, g_ada_b = ada_vjp(jnp.stack(dmods))
    grad = {"ada_w": g_ada_w, "ada_b": g_ada_b}
    for j, n in enumerate(BIG):
        grad[n] = jnp.stack([big_grads[l][j] for l in range(nl)])
    small_names_r = SMALL + CONVW + ("final_g",)
    summed = _allreduce_small("small", [jnp.stack([small_grads[l][n] for l in range(nl)]) for n in SMALL + CONVW] + [dfg])
    grad.update(zip(small_names_r, summed))
    chip = 2 * lax.axis_index("x") + lax.axis_index("y")
    for n in CONVW:
        width = weights[n].shape[-1]
        grad[n] = lax.dynamic_slice_in_dim(grad[n], chip * width, width, axis=2)

    delta, new_m, new_v = {}, {}, {}
    for n in ("ada_w",) + BIG:
        delta[n], new_m[n], new_v[n] = _adamw("adam_" + n, weights[n], grad[n], given["m_" + n], given["v_" + n])
    small_names = ("ada_b",) + SMALL + CONVW + ("final_g",)
    as2d = lambda t: t.reshape(1, -1) if t.ndim == 1 else t
    res = _adamw_small(
        "adam_small", [as2d(weights[n]) for n in small_names], [as2d(grad[n]) for n in small_names],
        [as2d(given["m_" + n]) for n in small_names], [as2d(given["v_" + n]) for n in small_names])
    for out, vals in zip((delta, new_m, new_v), res):
        for n, val in zip(small_names, vals):
            out[n] = val.reshape(weights[n].shape)
    grad["final_g"] = grad["final_g"].reshape(final_g.shape)

    return (loss, grad_x, *[grad[n] for n in WEIGHTS], *[delta[n] for n in WEIGHTS], *[new_m[n] for n in WEIGHTS], *[new_v[n] for n in WEIGHTS])
```

```python
import functools

import jax
import jax.numpy as jnp
from jax import lax
from jax.experimental import pallas as pl
from jax.experimental.pallas import tpu as pltpu

F32 = jnp.float32
BF16 = jnp.bfloat16
EPS = 1e-6
CHUNK = 128
HEAD_DIM = 64
STATE = 128
GM_HEAD = 128
SSD_GROUPS = 2
SSD_CONV = 4
FF_CONV = 3
N_MOD = 6
LANES = 128
SUBLANES = 8
V7X_VMEM_LIMIT = 48 * 1024 * 1024
MM_VMEM_BUDGET = 30 * 1024 * 1024
MM_STEP_MACS = 512 * 1024 * 2560
V7X_MXU_FLOPS = 996e12
V7X_HBM_BYTES_PER_S = 3.3e12
V7X_STEP_SECONDS = 0.35e-6
ADAM_LR, ADAM_B1, ADAM_B2, ADAM_EPS, ADAM_WD, ADAM_STEP = 0.001, 0.9, 0.999, 1e-08, 0.01, 10
MESH = pl.DeviceIdType.MESH
AXES = ("x", "y", "c")


def _round_up(n, m):
    return (n + m - 1) // m * m


def _divisors(n, cap, mult=LANES):
    out = [t for t in range(mult, min(n, cap) + 1, mult) if n % t == 0]
    return out or [n]


def _pcall(body, name, out_shape, grid=(), in_specs=None, out_specs=None, scratch=(), sem=None, prefetch=0):
    params = pltpu.CompilerParams(dimension_semantics=sem, vmem_limit_bytes=V7X_VMEM_LIMIT)
    if prefetch:
        spec = pltpu.PrefetchScalarGridSpec(num_scalar_prefetch=prefetch, grid=grid, in_specs=in_specs, out_specs=out_specs, scratch_shapes=list(scratch))
        return pl.pallas_call(body, name=name, out_shape=out_shape, grid_spec=spec, compiler_params=params)
    if in_specs is None:
        return pl.pallas_call(body, name=name, out_shape=out_shape, compiler_params=params)
    return pl.pallas_call(body, name=name, out_shape=out_shape, grid=grid, in_specs=in_specs, out_specs=out_specs, scratch_shapes=list(scratch), compiler_params=params)


def _sds(shape, dtype):
    return jax.ShapeDtypeStruct(tuple(shape), dtype)


def _mm_tiles(m, n, k, a_bytes, b_bytes, o_bytes):
    best, best_key = None, None
    for tm in _divisors(m, 2048):
        for tn in _divisors(n, 2560):
            for tk in _divisors(k, 2560):
                vmem = 2 * (tm * tk * a_bytes + tk * tn * b_bytes + tm * tn * o_bytes) + tm * tn * 4
                if vmem > MM_VMEM_BUDGET or tm * tn * tk > MM_STEP_MACS:
                    continue
                ni, nj, nk = m // tm, n // tn, k // tk
                a_reads = 1 if nk == 1 else nj
                b_reads = 1 if (nk == 1 and nj == 1) else ni
                hbm = a_reads * m * k * a_bytes + b_reads * k * n * b_bytes + m * n * o_bytes
                t = max(2.0 * m * n * k / V7X_MXU_FLOPS, hbm / V7X_HBM_BYTES_PER_S) + ni * nj * nk * V7X_STEP_SECONDS
                key = (-t, tm * tn * tk)
                if best_key is None or key > best_key:
                    best, best_key = (tm, tn, tk), key
    return best


def _matmul(name, a, b, mode, out_dtype):
    if mode == "nn":
        (m, k), n = a.shape, b.shape[1]
    elif mode == "nt":
        (m, k), n = a.shape, b.shape[0]
    else:
        (k, m), n = a.shape, b.shape[1]
    tm, tn, tk = _mm_tiles(m, n, k, a.dtype.itemsize, b.dtype.itemsize, jnp.dtype(out_dtype).itemsize)
    nk = k // tk
    if mode == "nn":
        a_spec = pl.BlockSpec((tm, tk), lambda i, j, kk: (i, kk))
        b_spec = pl.BlockSpec((tk, tn), lambda i, j, kk: (kk, j))
        dims = ((1,), (0,))
    elif mode == "nt":
        a_spec = pl.BlockSpec((tm, tk), lambda i, j, kk: (i, kk))
        b_spec = pl.BlockSpec((tn, tk), lambda i, j, kk: (j, kk))
        dims = ((1,), (1,))
    else:
        a_spec = pl.BlockSpec((tk, tm), lambda i, j, kk: (kk, i))
        b_spec = pl.BlockSpec((tk, tn), lambda i, j, kk: (kk, j))
        dims = ((0,), (0,))

    def body(a_ref, b_ref, o_ref, acc_ref):
        kk = pl.program_id(2)
        p = lax.dot_general(a_ref[...].astype(BF16), b_ref[...].astype(BF16), (dims, ((), ())), preferred_element_type=F32)
        if nk == 1:
            o_ref[...] = p.astype(o_ref.dtype)
        else:
            @pl.when(kk == 0)
            def _():
                acc_ref[...] = p

            @pl.when(kk > 0)
            def _():
                acc_ref[...] += p

            @pl.when(kk == nk - 1)
            def _():
                o_ref[...] = acc_ref[...].astype(o_ref.dtype)

    return _pcall(
        body, name, _sds((m, n), out_dtype), grid=(m // tm, n // tn, nk), in_specs=[a_spec, b_spec],
        out_specs=pl.BlockSpec((tm, tn), lambda i, j, kk: (i, j)), scratch=[pltpu.VMEM((tm, tn), F32)],
        sem=("parallel", "parallel", "arbitrary"),
    )(a, b)


def _mm(name, a, w, out_dtype):
    return _matmul(name, a.reshape(-1, a.shape[-1]), w, "nn", out_dtype).reshape(a.shape[:-1] + (w.shape[1],))


def _mm_bwd(name, a, w, dy):
    a2, dy2 = a.reshape(-1, a.shape[-1]), dy.reshape(-1, dy.shape[-1])
    return _matmul(name + "_dx", dy2, w, "nt", BF16).reshape(a.shape), _matmul(name + "_dw", a2, dy2, "tn", BF16)


def _dg(a, b, dims):
    return lax.dot_general(a.astype(BF16), b.astype(BF16), (dims, ((), ())), preferred_element_type=F32)


@jax.custom_vjp
def _dot_nn(a, b):
    return _dg(a, b, ((1,), (0,)))


_dot_nn.defvjp(lambda a, b: (_dot_nn(a, b), (a, b)), lambda r, d: (_dg(d, r[1], ((1,), (1,))), _dg(r[0], d, ((0,), (0,)))))


@jax.custom_vjp
def _dot_nt(a, b):
    return _dg(a, b, ((1,), (1,)))


_dot_nt.defvjp(lambda a, b: (_dot_nt(a, b), (a, b)), lambda r, d: (_dg(d, r[1], ((1,), (0,))), _dg(d, r[0], ((0,), (0,)))))


@jax.custom_vjp
def _dot_tn(a, b):
    return _dg(a, b, ((0,), (0,)))


_dot_tn.defvjp(lambda a, b: (_dot_tn(a, b), (a, b)), lambda r, d: (_dg(r[1], d, ((1,), (1,))), _dg(r[0], d, ((1,), (0,)))))


def _exact_dot(a, c, dims):
    hi = a.astype(BF16)
    r1 = a - hi.astype(F32)
    mid = r1.astype(BF16)
    lo = (r1 - mid.astype(F32)).astype(BF16)
    cb = c.astype(BF16)
    f = lambda t: lax.dot_general(t, cb, (dims, ((), ())), preferred_element_type=F32)
    return f(hi) + f(mid) + f(lo)


@jax.custom_vjp
def _sel_right(a, c):
    return _exact_dot(a, c, ((1,), (0,)))


_sel_right.defvjp(lambda a, c: (_sel_right(a, c), c), lambda c, d: (_exact_dot(d, c, ((1,), (1,))), jnp.zeros_like(c)))


def _exact_dot_left(c, a, dims):
    hi = a.astype(BF16)
    r1 = a - hi.astype(F32)
    mid = r1.astype(BF16)
    lo = (r1 - mid.astype(F32)).astype(BF16)
    cb = c.astype(BF16)
    f = lambda t: lax.dot_general(cb, t, (dims, ((), ())), preferred_element_type=F32)
    return f(hi) + f(mid) + f(lo)


@jax.custom_vjp
def _sel_left(c, a):
    return _exact_dot_left(c, a, ((1,), (0,)))


_sel_left.defvjp(lambda c, a: (_sel_left(c, a), c), lambda c, d: (jnp.zeros_like(c), _exact_dot_left(c, d, ((0,), (0,)))))


def _sigmoid(x):
    return 1.0 / (1.0 + jnp.exp(-x))


def _rms(x, g):
    return x * lax.rsqrt(jnp.mean(x * x, axis=-1, keepdims=True) + EPS) * g


def _gelu(x):
    return 0.5 * x * (1.0 + lax.erf(x * (2.0 ** -0.5)))


def _causal(n):
    return lax.broadcasted_iota(jnp.int32, (n, n), 0) >= lax.broadcasted_iota(jnp.int32, (n, n), 1)


def _row_in_specs(rows, bparams, gparams, tm):
    specs = [pl.BlockSpec((1, tm, w), lambda b, i, cb=cb: (b, i, cb)) for (_, w, cb, _) in rows]
    specs += [pl.BlockSpec((1, 1, p.shape[-1]), lambda b, i: (b, 0, 0)) for p in bparams]
    specs += [pl.BlockSpec(p.shape, lambda b, i, n=p.ndim: (0,) * n) for p in gparams]
    return specs


def _row_vals(refs, n_rows, n_b, n_g):
    vals = [r[0].astype(F32) for r in refs[:n_rows]]
    vals += [r[0].astype(F32) for r in refs[n_rows:n_rows + n_b]]
    vals += [r[...].astype(F32) for r in refs[n_rows + n_b:n_rows + n_b + n_g]]
    return vals


def _row_fwd(name, tile, rows, bparams, gparams, outs, n_sum, tm):
    bl, s = rows[0][0].shape[:2]
    n_in = len(rows) + len(bparams) + len(gparams)

    def body(*refs):
        first = (pl.program_id(0) == 0) & (pl.program_id(1) == 0)
        res = tile(*_row_vals(refs, len(rows), len(bparams), len(gparams)))
        o_refs = refs[n_in:]
        for k in range(len(outs)):
            o_refs[k][0] = res[k].astype(o_refs[k].dtype)
        for k in range(n_sum):
            o_ref, val = o_refs[len(outs) + k], res[len(outs) + k]

            @pl.when(first)
            def _(o_ref=o_ref, val=val):
                o_ref[...] = val

            @pl.when(jnp.logical_not(first))
            def _(o_ref=o_ref, val=val):
                o_ref[...] += val

    out_shape = [_sds((bl, s, w), dt) for (w, dt) in outs] + [_sds((1, 1), F32)] * n_sum
    out_specs = [pl.BlockSpec((1, tm, w), lambda b, i: (b, i, 0)) for (w, _) in outs] + [pl.BlockSpec((1, 1), lambda b, i: (0, 0))] * n_sum
    return _pcall(
        body, name, out_shape, grid=(bl, s // tm), in_specs=_row_in_specs(rows, bparams, gparams, tm), out_specs=out_specs,
        sem=("arbitrary", "arbitrary"),
    )(*[r[0] for r in rows], *bparams, *gparams)


def _row_bwd(name, tile, rows, bparams, gparams, outs, n_sum, tm, cts):
    bl, s = rows[0][0].shape[:2]
    n_r, n_b, n_g = len(rows), len(bparams), len(gparams)
    n_in = n_r + n_b + n_g
    n_ct = len(outs) + n_sum
    grad_rows = [k for k in range(n_r) if rows[k][3]]

    def body(*refs):
        b, i = pl.program_id(0), pl.program_id(1)
        vals = _row_vals(refs, n_r, n_b, n_g)
        ct_refs = refs[n_in:n_in + n_ct]
        ct = [r[0].astype(F32) for r in ct_refs[:len(outs)]] + [r[...] for r in ct_refs[len(outs):]]
        _, vjp = jax.vjp(tile, *vals)
        grads = vjp(tuple(ct))
        o_refs = refs[n_in + n_ct:]
        for j, k in enumerate(grad_rows):
            o_refs[j][0] = grads[k].astype(o_refs[j].dtype)
        for k in range(n_b):
            o_ref, val = o_refs[len(grad_rows) + k], grads[n_r + k]

            @pl.when(i == 0)
            def _(o_ref=o_ref, val=val):
                o_ref[0] = val

            @pl.when(i > 0)
            def _(o_ref=o_ref, val=val):
                o_ref[0] += val

        first = (b == 0) & (i == 0)
        for k in range(n_g):
            o_ref, val = o_refs[len(grad_rows) + n_b + k], grads[n_r + n_b + k]

            @pl.when(first)
            def _(o_ref=o_ref, val=val):
                o_ref[...] = val

            @pl.when(jnp.logical_not(first))
            def _(o_ref=o_ref, val=val):
                o_ref[...] += val

    in_specs = _row_in_specs(rows, bparams, gparams, tm)
    in_specs += [pl.BlockSpec((1, tm, w), lambda b, i: (b, i, 0)) for (w, _) in outs] + [pl.BlockSpec((1, 1), lambda b, i: (0, 0))] * n_sum
    out_shape = [_sds((bl, s, rows[k][1]), rows[k][3]) for k in grad_rows]
    out_shape += [_sds(p.shape, F32) for p in bparams] + [_sds(p.shape, F32) for p in gparams]
    out_specs = [pl.BlockSpec((1, tm, rows[k][1]), lambda b, i: (b, i, 0)) for k in grad_rows]
    out_specs += [pl.BlockSpec((1, 1, p.shape[-1]), lambda b, i: (b, 0, 0)) for p in bparams]
    out_specs += [pl.BlockSpec(p.shape, lambda b, i, n=p.ndim: (0,) * n) for p in gparams]
    res = _pcall(
        body, name, out_shape, grid=(bl, s // tm), in_specs=in_specs, out_specs=out_specs, sem=("arbitrary", "arbitrary"),
    )(*[r[0] for r in rows], *bparams, *gparams, *cts)
    return res[:len(grad_rows)], res[len(grad_rows):len(grad_rows) + n_b], res[len(grad_rows) + n_b:]


def _nm_tile(x, sc, sh, g):
    return (_rms(x, g) * (1.0 + sc) + sh,)


def _nm_res_tile(x, sc, sh, g):
    return x, _rms(x, g) * (1.0 + sc) + sh


def _rnm_tile(x, o, gate, sc, sh, g):
    xn = x + gate * o
    return xn, _rms(xn, g) * (1.0 + sc) + sh


def _final_tile(x, o, tgt, gate, g):
    e = _rms(x + gate * o, g) - tgt
    return (0.5 * jnp.sum(jnp.mean(e * e, axis=-1, keepdims=True), axis=0, keepdims=True),)


def _gm_tile(u_in, v_in, vg, ws, bs_t, og):
    d = u_in.shape[1]
    u, vn = _gelu(u_in), _rms(_gelu(v_in), vg)
    causal = _causal(CHUNK)
    lane = lax.broadcasted_iota(jnp.int32, (1, LANES), 1)
    parts = []
    for h in range(d // GM_HEAD):
        bias = jnp.sum(bs_t * (lane == h).astype(F32), axis=1, keepdims=True)
        parts.append(_dot_nn(jnp.where(causal, ws[h], 0.0), vn[:, h * GM_HEAD:(h + 1) * GM_HEAD]) + bias)
    return (_rms(u * jnp.concatenate(parts, axis=1), og),)


def _shift_down(x, j):
    if j == 0:
        return x
    row = lax.broadcasted_iota(jnp.int32, x.shape, 0)
    return jnp.where(row >= j, pltpu.roll(x, j, 0), 0.0)


def _shift_up(x, j):
    if j == 0:
        return x
    s = x.shape[0]
    row = lax.broadcasted_iota(jnp.int32, x.shape, 0)
    return jnp.where(row < s - j, pltpu.roll(x, s - j, 0), 0.0)


def _conv_pre(x, w_ref, b_ref, taps):
    acc = x * w_ref[taps - 1:taps, :] + b_ref[...]
    for j in range(1, taps):
        acc = acc + _shift_down(x, j) * w_ref[taps - 1 - j:taps - j, :]
    return acc


def _conv_tc(*widths):
    for tc in (512, 256, 128):
        if all(w % tc == 0 for w in widths):
            return tc
    raise ValueError("conv widths must be multiples of the lane count")


def _conv_act_fwd(name, src, col0, chans, w8, b, taps, val_col0, out_dtype):
    bl, s, _ = src.shape
    gated = val_col0 is not None
    tc = _conv_tc(chans, col0, val_col0 or 0)

    def body(*refs):
        if gated:
            x_ref, v_ref, w_ref, b_ref, o_ref = refs
        else:
            x_ref, w_ref, b_ref, o_ref = refs
        pre = _conv_pre(x_ref[0], w_ref, b_ref, taps)
        y = pre * _sigmoid(pre)
        if gated:
            y = y * v_ref[0]
        o_ref[0] = y.astype(o_ref.dtype)

    in_specs = [pl.BlockSpec((1, s, tc), lambda j, bb: (bb, 0, col0 // tc + j))]
    args = [src]
    if gated:
        in_specs.append(pl.BlockSpec((1, s, tc), lambda j, bb: (bb, 0, val_col0 // tc + j)))
        args.append(src)
    in_specs += [pl.BlockSpec((SUBLANES, tc), lambda j, bb: (0, j)), pl.BlockSpec((1, tc), lambda j, bb: (0, j))]
    return _pcall(
        body, name, _sds((bl, s, chans), out_dtype), grid=(chans // tc, bl), in_specs=in_specs,
        out_specs=pl.BlockSpec((1, s, tc), lambda j, bb: (bb, 0, j)), sem=("parallel", "arbitrary"),
    )(*args, w8, b)


def _conv_act_bwd(name, src, col0, chans, w8, b, taps, val_col0, dy, dx_dtype):
    bl, s, _ = src.shape
    gated = val_col0 is not None
    tc = _conv_tc(chans, col0, val_col0 or 0)
    nblk = chans // tc

    def body(*refs):
        if gated:
            x_ref, v_ref, w_ref, b_ref, dy_ref, dx_ref, dw_ref, db_ref = refs
        else:
            x_ref, w_ref, b_ref, dy_ref, dx_ref, dw_ref, db_ref = refs
        bb = pl.program_id(1)
        x = x_ref[0]
        pre = _conv_pre(x, w_ref, b_ref, taps)
        sig = _sigmoid(pre)
        d = dy_ref[0].astype(F32)

        def gate_part():
            dsil = d * v_ref[0] if gated else d
            dpre = dsil * sig * (1.0 + pre * (1.0 - sig))
            dx = dpre * w_ref[taps - 1:taps, :]
            for j in range(1, taps):
                dx = dx + _shift_up(dpre, j) * w_ref[taps - 1 - j:taps - j, :]
            dx_ref[0] = dx.astype(dx_ref.dtype)
            rows = [jnp.sum(dpre * _shift_down(x, taps - 1 - k), axis=0, keepdims=True) for k in range(taps)]
            rows += [jnp.zeros_like(rows[0])] * (SUBLANES - taps)
            dw = jnp.concatenate(rows, axis=0)
            db = jnp.sum(dpre, axis=0, keepdims=True)

            @pl.when(bb == 0)
            def _():
                dw_ref[...] = dw
                db_ref[...] = db

            @pl.when(bb > 0)
            def _():
                dw_ref[...] += dw
                db_ref[...] += db

        if gated:
            part = pl.program_id(2)
            pl.when(part == 0)(gate_part)

            @pl.when(part == 1)
            def _():
                dx_ref[0] = (d * (pre * sig)).astype(dx_ref.dtype)
        else:
            gate_part()

    if gated:
        grid, sem = (nblk, bl, 2), ("parallel", "arbitrary", "arbitrary")
        im = lambda f: (lambda j, bb, p: f(j, bb, p))
    else:
        grid, sem = (nblk, bl), ("parallel", "arbitrary")
        im = lambda f: (lambda j, bb: f(j, bb, 0))
    in_specs = [pl.BlockSpec((1, s, tc), im(lambda j, bb, p: (bb, 0, col0 // tc + j)))]
    args = [src]
    if gated:
        in_specs.append(pl.BlockSpec((1, s, tc), im(lambda j, bb, p: (bb, 0, val_col0 // tc + j))))
        args.append(src)
    in_specs += [
        pl.BlockSpec((SUBLANES, tc), im(lambda j, bb, p: (0, j))), pl.BlockSpec((1, tc), im(lambda j, bb, p: (0, j))),
        pl.BlockSpec((1, s, tc), im(lambda j, bb, p: (bb, 0, j))),
    ]
    out_shape = [_sds((bl, s, chans * (2 if gated else 1)), dx_dtype), _sds((SUBLANES, chans), F32), _sds((1, chans), F32)]
    out_specs = [
        pl.BlockSpec((1, s, tc), im(lambda j, bb, p: (bb, 0, j + p * nblk))), pl.BlockSpec((SUBLANES, tc), im(lambda j, bb, p: (0, j))),
        pl.BlockSpec((1, tc), im(lambda j, bb, p: (0, j))),
    ]
    return _pcall(body, name, out_shape, grid=grid, in_specs=in_specs, out_specs=out_specs, sem=sem)(*args, w8, b, dy)


def _ssd_chunk(g_idx, states, xs, bm, cm, dtr, z, dtb, alog, dsk, ng):
    gw = xs.shape[1]
    hpg = gw // HEAD_DIM
    dt = jax.nn.softplus(dtr + dtb)
    da = dt * (-jnp.exp(alog))
    causal = _causal(CHUNK)
    acs = _sel_left(causal.astype(F32), da)
    head_of_col = g_idx * hpg + lax.broadcasted_iota(jnp.int32, (LANES, gw), 1) // HEAD_DIM
    expand = (lax.broadcasted_iota(jnp.int32, (LANES, gw), 0) == head_of_col).astype(F32)
    dt_e, acs_e = _sel_right(dt, expand), _sel_right(acs, expand)
    last = lax.broadcasted_iota(jnp.int32, (CHUNK, gw), 0) == CHUNK - 1
    alast_e = jnp.sum(jnp.where(last, acs_e, 0.0), axis=0, keepdims=True)
    xc = xs * dt_e
    xc_st = xc * jnp.exp(alast_e - acs_e)
    decay_out, chunk_decay = jnp.exp(acs_e), jnp.exp(alast_e)
    cb = _dot_nt(cm, bm)
    acs_t = acs.T
    lane = lax.broadcasted_iota(jnp.int32, (1, LANES), 1)
    sub = lax.broadcasted_iota(jnp.int32, (LANES, 1), 0)
    ys, new_states = [], []
    for p in range(gw // LANES):
        sl = slice(p * LANES, (p + 1) * LANES)
        xcp = xc[:, sl]
        y = _dot_nn(cm, states[p]) * decay_out[:, sl]
        for q in range(2):
            head = g_idx * hpg + 2 * p + q
            col = jnp.sum(acs * (lane == head).astype(F32), axis=1, keepdims=True)
            row = jnp.sum(acs_t * (sub == head).astype(F32), axis=0, keepdims=True)
            decay = jnp.where(causal, jnp.exp(jnp.where(causal, col - row, 0.0)), 0.0)
            half = ((lane // HEAD_DIM) == q).astype(F32)
            y = y + _dot_nn(cb * decay, xcp * half)
        ys.append(y)
        new_states.append(states[p] * chunk_decay[:, sl] + _dot_tn(bm, xc_st[:, sl]))
    y = jnp.concatenate(ys, axis=1) + dsk * xs
    gated = y * (z * _sigmoid(z))
    return tuple(new_states), _rms(gated, ng)


def _ssd_specs(d, gw, dt_col, rev, nc):
    ci = (lambda i: nc - 1 - i) if rev else (lambda i: i)
    return [
        pl.BlockSpec((1, CHUNK, gw), lambda g, b, i: (b, ci(i), g)),
        pl.BlockSpec((1, CHUNK, STATE), lambda g, b, i: (b, ci(i), d // STATE + g)),
        pl.BlockSpec((1, CHUNK, STATE), lambda g, b, i: (b, ci(i), d // STATE + SSD_GROUPS + g)),
        pl.BlockSpec((1, CHUNK, LANES), lambda g, b, i: (b, ci(i), dt_col // LANES)),
        pl.BlockSpec((1, CHUNK, gw), lambda g, b, i: (b, ci(i), g)),
        pl.BlockSpec((1, LANES), lambda g, b, i: (0, 0)), pl.BlockSpec((1, LANES), lambda g, b, i: (0, 0)),
        pl.BlockSpec((1, gw), lambda g, b, i: (0, g)), pl.BlockSpec((1, gw), lambda g, b, i: (0, g)),
    ]


def _ssd_fwd(name, xbc, proj, dt_col, dtb, alog, dsk, ng):
    bl, s, cd = xbc.shape
    d = dsk.shape[1]
    gw, nc = d // SSD_GROUPS, s // CHUNK
    npair = gw // LANES

    def body(xs_ref, bm_ref, cm_ref, dt_ref, z_ref, dtb_ref, alog_ref, dsk_ref, ng_ref, y_ref, hp_ref, st_ref):
        g, i = pl.program_id(0), pl.program_id(2)

        @pl.when(i == 0)
        def _():
            st_ref[...] = jnp.zeros_like(st_ref)

        states = tuple(st_ref[p] for p in range(npair))
        hp_ref[0, 0, 0] = st_ref[...]
        new_states, yn = _ssd_chunk(g, states, xs_ref[0], bm_ref[0], cm_ref[0], dt_ref[0], z_ref[0], dtb_ref[...], alog_ref[...], dsk_ref[...], ng_ref[...])
        for p in range(npair):
            st_ref[p] = new_states[p]
        y_ref[0] = yn.astype(y_ref.dtype)

    out_shape = [_sds((bl, s, d), BF16), _sds((SSD_GROUPS, bl, nc, npair, STATE, LANES), F32)]
    out_specs = [
        pl.BlockSpec((1, CHUNK, gw), lambda g, b, i: (b, i, g)),
        pl.BlockSpec((1, 1, 1, npair, STATE, LANES), lambda g, b, i: (g, b, i, 0, 0, 0)),
    ]
    return _pcall(
        body, name, out_shape, grid=(SSD_GROUPS, bl, nc), in_specs=_ssd_specs(d, gw, dt_col, False, nc), out_specs=out_specs,
        scratch=[pltpu.VMEM((npair, STATE, LANES), F32)], sem=("arbitrary", "arbitrary", "arbitrary"),
    )(xbc, xbc, xbc, proj, proj, dtb, alog, dsk, ng)


def _ssd_bwd(name, xbc, proj, dt_col, dtb, alog, dsk, ng, hprev, dy):
    bl, s, cd = xbc.shape
    d = dsk.shape[1]
    gw, nc = d // SSD_GROUPS, s // CHUNK
    npair = gw // LANES

    def body(xs_ref, bm_ref, cm_ref, dt_ref, z_ref, dtb_ref, alog_ref, dsk_ref, ng_ref, hp_ref, dy_ref,
             dxs_ref, dbm_ref, dcm_ref, ddt_ref, dz_ref, ddtb_ref, dalog_ref, ddsk_ref, dng_ref, dst_ref):
        g, b, i = pl.program_id(0), pl.program_id(1), pl.program_id(2)

        @pl.when(i == 0)
        def _():
            dst_ref[...] = jnp.zeros_like(dst_ref)

        states = tuple(hp_ref[0, 0, 0, p] for p in range(npair))
        step = functools.partial(_ssd_chunk, g)
        _, vjp = jax.vjp(step, states, xs_ref[0], bm_ref[0], cm_ref[0], dt_ref[0], z_ref[0], dtb_ref[...], alog_ref[...], dsk_ref[...], ng_ref[...])
        d_states, dxs, dbm, dcm, ddt, dz, ddtb, dalog, ddsk, dng = vjp((tuple(dst_ref[p] for p in range(npair)), dy_ref[0].astype(F32)))
        for p in range(npair):
            dst_ref[p] = d_states[p]
        dxs_ref[0], dbm_ref[0], dcm_ref[0] = dxs, dbm, dcm
        dz_ref[0] = dz.astype(dz_ref.dtype)
        ddt_ref[0, 0] = ddt
        first_g = (b == 0) & (i == 0)
        first = first_g & (g == 0)
        for o_ref, val, init in ((ddtb_ref, ddtb, first), (dalog_ref, dalog, first), (ddsk_ref, ddsk, first_g), (dng_ref, dng, first_g)):
            @pl.when(init)
            def _(o_ref=o_ref, val=val):
                o_ref[...] = val

            @pl.when(jnp.logical_not(init))
            def _(o_ref=o_ref, val=val):
                o_ref[...] += val

    rc = lambda i: nc - 1 - i
    in_specs = _ssd_specs(d, gw, dt_col, True, nc) + [
        pl.BlockSpec((1, 1, 1, npair, STATE, LANES), lambda g, b, i: (g, b, rc(i), 0, 0, 0)),
        pl.BlockSpec((1, CHUNK, gw), lambda g, b, i: (b, rc(i), g)),
    ]
    out_shape = [
        _sds((bl, s, d), F32), _sds((bl, s, SSD_GROUPS * STATE), F32), _sds((bl, s, SSD_GROUPS * STATE), F32),
        _sds((SSD_GROUPS, bl, s, LANES), F32), _sds((bl, s, d), BF16),
        _sds((1, LANES), F32), _sds((1, LANES), F32), _sds((1, d), F32), _sds((1, d), F32),
    ]
    out_specs = [
        pl.BlockSpec((1, CHUNK, gw), lambda g, b, i: (b, rc(i), g)),
        pl.BlockSpec((1, CHUNK, STATE), lambda g, b, i: (b, rc(i), g)), pl.BlockSpec((1, CHUNK, STATE), lambda g, b, i: (b, rc(i), g)),
        pl.BlockSpec((1, 1, CHUNK, LANES), lambda g, b, i: (g, b, rc(i), 0)),
        pl.BlockSpec((1, CHUNK, gw), lambda g, b, i: (b, rc(i), g)),
        pl.BlockSpec((1, LANES), lambda g, b, i: (0, 0)), pl.BlockSpec((1, LANES), lambda g, b, i: (0, 0)),
        pl.BlockSpec((1, gw), lambda g, b, i: (0, g)), pl.BlockSpec((1, gw), lambda g, b, i: (0, g)),
    ]
    return _pcall(
        body, name, out_shape, grid=(SSD_GROUPS, bl, nc), in_specs=in_specs, out_specs=out_specs,
        scratch=[pltpu.VMEM((npair, STATE, LANES), F32)], sem=("arbitrary", "arbitrary", "arbitrary"),
    )(xbc, xbc, xbc, proj, proj, dtb, alog, dsk, ng, hprev, dy)


def _mixer_fwd(name, proj, d, cd, conv_w8, conv_b, dtb, alog, dsk, ng, vg, ws, bs_t, og):
    xbc = _conv_act_fwd(name + "_conv_fwd", proj, 3 * d, cd, conv_w8, conv_b, SSD_CONV, None, F32)
    y, hprev = _ssd_fwd(name + "_ssd_fwd", xbc, proj, 3 * d + cd, dtb, alog, dsk, ng)
    rows = [(proj, d, 1, BF16), (proj, d, 2, BF16)]
    (g_out,) = _row_fwd(name + "_gm_fwd", _gm_tile, rows, [], [vg, ws, bs_t, og], [(d, BF16)], 0, CHUNK)
    return jnp.concatenate([y, g_out], axis=-1), (xbc, hprev)


def _mixer_bwd(name, proj, d, cd, conv_w8, conv_b, dtb, alog, dsk, ng, vg, ws, bs_t, og, xbc, hprev, dycat):
    bl, s, n_proj = proj.shape
    dt_col = 3 * d + cd
    dy, dg_out = dycat[..., :d], dycat[..., d:]
    dxs, dbm, dcm, ddt2, dz, ddtb, dalog, ddsk, dng = _ssd_bwd(name + "_ssd_bwd", xbc, proj, dt_col, dtb, alog, dsk, ng, hprev, dy)
    dxbc_act = jnp.concatenate([dxs, dbm, dcm], axis=-1)
    dxbc, dw8, dcb = _conv_act_bwd(name + "_conv_bwd", proj, 3 * d, cd, conv_w8, conv_b, SSD_CONV, None, dxbc_act, BF16)
    rows = [(proj, d, 1, BF16), (proj, d, 2, BF16)]
    (du, dv), _, (dvg, dws, dbs_t, dog) = _row_bwd(name + "_gm_bwd", _gm_tile, rows, [], [vg, ws, bs_t, og], [(d, BF16)], 0, CHUNK, [dg_out])
    ddt = (ddt2[0] + ddt2[1]).astype(BF16)
    pad = jnp.zeros((bl, s, n_proj - dt_col - LANES), BF16)
    dproj = jnp.concatenate([dz, du, dv, dxbc, ddt, pad], axis=-1)
    return dproj, (dw8, dcb, ddtb, dalog, ddsk, dng, dvg, dws, dbs_t, dog)


def _position():
    return lax.axis_index("x"), lax.axis_index("y"), lax.axis_index("c")


def _at(ref, idx):
    return ref.at[idx] if len(idx) else ref


def _exchange(name, inputs, out_shapes, plan):
    n_in, n_out = len(inputs), len(out_shapes)
    n_copy = len(plan(0, 0, 0))

    def body(*refs):
        in_refs, out_refs = refs[:n_in], refs[n_in:n_in + n_out]
        send_sems, recv_sems = refs[n_in + n_out:]
        x, y, c = _position()
        copies = plan(x, y, c)

        def copy(k, src, dst, peer):
            return pltpu.make_async_remote_copy(src_ref=src, dst_ref=dst, send_sem=send_sems.at[k], recv_sem=recv_sems.at[k], device_id=peer, device_id_type=MESH)

        sends = [copy(k, _at(in_refs[sa], si), _at(out_refs[da], di), peer) for k, (sa, si, da, di, peer, _) in enumerate(copies)]
        for cp in sends:
            cp.start()
        for k, (sa, si, da, _, peer, li) in enumerate(copies):
            copy(k, _at(in_refs[sa], si), _at(out_refs[da], li), peer).wait_recv()
        for cp in sends:
            cp.wait_send()

    any_spec = pl.BlockSpec(memory_space=pl.ANY)
    outs = pl.pallas_call(
        body, name=name, out_shape=[_sds(s, dt) for s, dt in out_shapes], in_specs=[any_spec] * n_in, out_specs=[any_spec] * n_out,
        scratch_shapes=[pltpu.SemaphoreType.DMA((n_copy,)), pltpu.SemaphoreType.DMA((n_copy,))],
    )(*inputs)
    return list(outs)


def _exchange_start(name, inputs, out_shapes, plan):
    n_in, n_out = len(inputs), len(out_shapes)
    n_copy = len(plan(0, 0, 0))

    def body(*refs):
        in_refs, land_refs = refs[:n_in], refs[n_in:n_in + n_out]
        send_sems, recv_sems = refs[n_in + n_out:n_in + n_out + 2]
        token = refs[-1]
        x, y, c = _position()
        for k, (sa, si, da, di, peer, _) in enumerate(plan(x, y, c)):
            pltpu.make_async_remote_copy(
                src_ref=_at(in_refs[sa], si), dst_ref=_at(land_refs[da], di), send_sem=send_sems.at[k], recv_sem=recv_sems.at[k],
                device_id=peer, device_id_type=MESH).start()
        token[...] = jnp.zeros_like(token)

    hbm, sem = pl.BlockSpec(memory_space=pltpu.HBM), pl.BlockSpec(memory_space=pltpu.SEMAPHORE)
    lands = [lax.empty(s, dt) for s, dt in out_shapes]
    args = [pltpu.with_memory_space_constraint(a, pltpu.HBM) for a in list(inputs) + lands]
    outs = pl.pallas_call(
        body, name=name,
        out_shape=(pltpu.SemaphoreType.DMA((n_copy,)), pltpu.SemaphoreType.DMA((n_copy,)), *[pltpu.HBM(a.shape, a.dtype) for a in args], _sds((SUBLANES, LANES), F32)),
        in_specs=[hbm] * (n_in + n_out), out_specs=(sem, sem, *[hbm] * (n_in + n_out), pl.BlockSpec(memory_space=pltpu.VMEM)),
        input_output_aliases={i: 2 + i for i in range(n_in + n_out)},
        compiler_params=pltpu.CompilerParams(has_side_effects=pltpu.SideEffectType.DATAFLOW_SIDE_EFFECTING),
    )(*args)
    return dict(name=name, plan=plan, sems=outs[:2], ins=list(outs[2:2 + n_in]), lands=list(outs[2 + n_in:2 + n_in + n_out]), token=outs[-1])


def _exchange_wait(started, after):
    plan, n_in, n_out = started["plan"], len(started["ins"]), len(started["lands"])

    def body(*refs):
        in_refs, land_refs = refs[:n_in], refs[n_in:n_in + n_out]
        send_sems, recv_sems = refs[n_in + n_out:n_in + n_out + 2]
        x, y, c = _position()
        for k, (sa, si, da, _, peer, li) in enumerate(plan(x, y, c)):
            cp = pltpu.make_async_remote_copy(
                src_ref=_at(in_refs[sa], si), dst_ref=_at(land_refs[da], li), send_sem=send_sems.at[k], recv_sem=recv_sems.at[k],
                device_id=peer, device_id_type=MESH)
            cp.wait_send()
            cp.wait_recv()

    hbm, sem = pl.BlockSpec(memory_space=pltpu.HBM), pl.BlockSpec(memory_space=pltpu.SEMAPHORE)
    bufs = started["ins"] + started["lands"]
    outs = pl.pallas_call(
        body, name=started["name"] + "_wait", out_shape=tuple(pltpu.HBM(a.shape, a.dtype) for a in bufs),
        in_specs=[hbm] * len(bufs) + [sem, sem, pl.BlockSpec(memory_space=pl.ANY)], out_specs=tuple([hbm] * len(bufs)),
        input_output_aliases={i: i for i in range(len(bufs))},
        compiler_params=pltpu.CompilerParams(has_side_effects=pltpu.SideEffectType.DATAFLOW_SIDE_EFFECTING),
    )(*bufs, *started["sems"], after)
    return list(outs[:n_in]), list(outs[n_in:])


def _after(value, token):
    return lax.optimization_barrier((value, token))[0]


def _chip_peers(x, y, c):
    return [(1 - x, y, c), (x, 1 - y, c), (1 - x, 1 - y, c)]


def _chip_of(p):
    return 2 * p[0] + p[1]


def _set_slot(slots, me, blk):
    return lax.dynamic_update_slice(slots, blk[None], (me,) + (0,) * blk.ndim)


def _by_core(c, mine, other, axis):
    return jnp.where(c == 0, jnp.stack([mine, other], axis), jnp.stack([other, mine], axis))


def _gather_chips(name, blocks):
    n = len(blocks)

    def plan(x, y, c):
        return [(a, (), a, (2 * x + y,), p, (_chip_of(p),)) for a in range(n) for p in _chip_peers(x, y, c)]

    recv = _exchange(name, blocks, [((4,) + b.shape, b.dtype) for b in blocks], plan)
    x, y, _ = _position()
    return [_set_slot(r, 2 * x + y, b) for r, b in zip(recv, blocks)]


def _plan_gather_chips(n):
    def plan(x, y, c):
        return [(a, (), a, (2 * x + y,), p, (_chip_of(p),)) for a in range(n) for p in _chip_peers(x, y, c)]

    return plan


def _gather_pass_cores(name, blocks, from_chips):
    n = len(blocks)

    def plan_cores(x, y, c):
        me, sib = 2 * x + y, (x, y, 1 - c)
        own = [(a, (), a, (me,), sib, (me,)) for a in range(n)]
        passed = [(n + a, (_chip_of(p),), a, (_chip_of(p),), sib, (_chip_of(p),)) for a in range(n) for p in _chip_peers(x, y, c)]
        return own + passed

    from_core = _exchange(name + "_cores", list(blocks) + list(from_chips), [((4,) + b.shape, b.dtype) for b in blocks], plan_cores)
    x, y, c = _position()
    return [_by_core(c, _set_slot(r1, 2 * x + y, b), r2, 1) for b, r1, r2 in zip(blocks, from_chips, from_core)]


def _gather_two_level(name, blocks):
    from_chips = _exchange(name + "_chips", blocks, [((4,) + b.shape, b.dtype) for b in blocks], _plan_gather_chips(len(blocks)))
    return _gather_pass_cores(name, blocks, from_chips)


def _pair_add(name, g42, r4):
    _, _, rh, cols = g42.shape
    tr = _divisors(rh, 512, SUBLANES * 2)[-1]

    def body(c_ref, a_ref, b_ref, o_ref):
        o_ref[0] = (a_ref[0, 0].astype(F32) + b_ref[0].astype(F32)).astype(o_ref.dtype)

    cidx = lax.axis_index("c").astype(jnp.int32).reshape(1)
    return _pcall(
        body, name, _sds(r4.shape, BF16), grid=(4, rh // tr),
        in_specs=[pl.BlockSpec((1, 1, tr, cols), lambda s, i, c_ref: (s, c_ref[0], i, 0)), pl.BlockSpec((1, tr, cols), lambda s, i, c_ref: (s, i, 0))],
        out_specs=pl.BlockSpec((1, tr, cols), lambda s, i, c_ref: (s, i, 0)), sem=("parallel", "parallel"), prefetch=1,
    )(cidx, g42, r4)


def _slot_sum(name, parts):
    n, r, cols = parts.shape
    cap = max(2 * SUBLANES, (4 * 1024 * 1024) // (n * cols * parts.dtype.itemsize))
    tr = _divisors(r, cap, 2 * SUBLANES)[-1]

    def body(p_ref, o_ref):
        acc = p_ref[0].astype(F32)
        for k in range(1, n):
            acc = acc + p_ref[k].astype(F32)
        o_ref[...] = acc

    return _pcall(
        body, name, _sds((r, cols), F32), grid=(r // tr,), in_specs=[pl.BlockSpec((n, tr, cols), lambda i: (0, i, 0))],
        out_specs=pl.BlockSpec((tr, cols), lambda i: (i, 0)), sem=("parallel",),
    )(parts)


def _slot_sums(name, parts):
    k = len(parts)

    def body(*refs):
        for p_ref, o_ref in zip(refs[:k], refs[k:]):
            acc = p_ref[0]
            for j in range(1, p_ref.shape[0]):
                acc = acc + p_ref[j]
            o_ref[...] = acc

    return list(_pcall(body, name, [_sds(p.shape[1:], F32) for p in parts])(*parts))


def _gather_weights_start(name, shards, tie):
    c = lax.axis_index("c")
    halves = [lax.dynamic_slice_in_dim(w, c * (w.shape[0] // 2), w.shape[0] // 2, 0).astype(BF16) for w in shards]
    halves[0] = _after(halves[0], tie)
    return _exchange_start(name + "_ag_chips", halves, [((4,) + h.shape, h.dtype) for h in halves], _plan_gather_chips(len(halves)))


def _gather_weights_finish(name, started, after):
    halves, from_chips = _exchange_wait(started, after)
    return [f.reshape((4, 2 * h.shape[0], h.shape[1])) for f, h in zip(_gather_pass_cores(name + "_ag", halves, from_chips), halves)]


def _reduce_weights_start(name, grads):
    n = len(grads)
    g42 = [g.reshape(4, 2, g.shape[1] // 2, g.shape[2]) for g in grads]

    def plan_swap(x, y, c):
        return [(a, (s, 1 - c), a, (s,), (x, y, 1 - c), (s,)) for a in range(n) for s in range(4)]

    def plan_chips(x, y, c):
        return [(a, (_chip_of(p),), a, (2 * x + y,), p, (_chip_of(p),)) for a in range(n) for p in _chip_peers(x, y, c)]

    other = _exchange(name + "_rs_cores", g42, [((4,) + g.shape[2:], g.dtype) for g in g42], plan_swap)
    pair = [_pair_add(f"{name}_rs_pair{a}", g, o) for a, (g, o) in enumerate(zip(g42, other))]
    return _exchange_start(name + "_rs_chips", pair, [(p.shape, p.dtype) for p in pair], plan_chips)


def _reduce_weights_finish(name, started, after):
    pair, recv = _exchange_wait(started, after)
    n = len(pair)
    x, y, c = _position()
    me = 2 * x + y

    def plan_share(x, y, c):
        return [(a, (), a, (), (x, y, 1 - c), ()) for a in range(n)]

    parts = [lax.dynamic_update_slice(r, lax.dynamic_slice_in_dim(p, me, 1, 0), (me, 0, 0)) for r, p in zip(recv, pair)]
    mine = [_slot_sum(f"{name}_rs_sum{a}", p) for a, p in enumerate(parts)]
    theirs = _exchange(name + "_rs_share", mine, [(m.shape, m.dtype) for m in mine], plan_share)
    return [_by_core(c, m, t, 0).reshape(2 * m.shape[0], m.shape[1]) for m, t in zip(mine, theirs)]


def _allreduce_small(name, grads):
    allg = [g.reshape((8,) + g.shape[2:]) for g in _gather_two_level(name + "_ag", list(grads))]
    return _slot_sums(name + "_sum", allg)


def _ada_fwd_call(name, c_all, w, b_shard):
    nl, d, ns = w.shape
    nb = c_all.shape[0]

    def body(c_ref, w_ref, b_ref, o_ref):
        cv = c_ref[...]
        o_ref[0] = _dg(cv * _sigmoid(cv), w_ref[0], ((1,), (0,))) + b_ref[0]

    return _pcall(
        body, name, _sds((nl, nb, ns), F32), grid=(nl,),
        in_specs=[pl.BlockSpec((nb, d), lambda l: (0, 0)), pl.BlockSpec((1, d, ns), lambda l: (l, 0, 0)), pl.BlockSpec((1, 1, ns), lambda l: (l, 0, 0))],
        out_specs=pl.BlockSpec((1, nb, ns), lambda l: (l, 0, 0)), sem=("parallel",),
    )(c_all, w, b_shard)


def _ada_bwd_call(name, c_all, dm_shard, dm_all):
    nl, nb, ns = dm_shard.shape
    d = c_all.shape[1]
    nm = dm_all.shape[2]

    def body(c_ref, ds_ref, da_ref, dw_ref, db_ref):
        cv = c_ref[...]
        dw_ref[0] = _dg(cv * _sigmoid(cv), ds_ref[0], ((0,), (0,)))
        db_ref[0] = jnp.sum(da_ref[0], axis=0, keepdims=True)

    return _pcall(
        body, name, [_sds((nl, d, ns), F32), _sds((nl, 1, nm), F32)], grid=(nl,),
        in_specs=[pl.BlockSpec((nb, d), lambda l: (0, 0)), pl.BlockSpec((1, nb, ns), lambda l: (l, 0, 0)), pl.BlockSpec((1, nb, nm), lambda l: (l, 0, 0))],
        out_specs=[pl.BlockSpec((1, d, ns), lambda l: (l, 0, 0)), pl.BlockSpec((1, 1, nm), lambda l: (l, 0, 0))], sem=("parallel",),
    )(c_all, dm_shard, dm_all)


def _make_ada(name, bl):
    def my_chip():
        return 2 * lax.axis_index("x") + lax.axis_index("y")

    def my_dev():
        return 2 * my_chip() + lax.axis_index("c")

    @jax.custom_vjp
    def ada(c_all, w, b):
        nl, d, ns = w.shape
        b_shard = lax.dynamic_slice(b, (0, my_chip() * ns), (nl, ns)).reshape(nl, 1, ns)
        shard = _ada_fwd_call(name + "_fwd", c_all, w, b_shard)
        (allc,) = _gather_chips(name + "_ag", [shard])
        mods = jnp.transpose(allc, (1, 2, 0, 3)).reshape(nl, c_all.shape[0], 4 * ns)
        return lax.dynamic_slice(mods, (0, my_dev() * bl, 0), (nl, bl, 4 * ns))

    def fwd(c_all, w, b):
        return ada(c_all, w, b), (c_all, w.shape)

    def bwd(res, dm):
        c_all, (nl, d, ns) = res
        (dm_all,) = _gather_two_level(name + "_bwd_ag", [dm])
        dm_all = jnp.transpose(dm_all.reshape((8,) + dm.shape), (1, 0, 2, 3)).reshape(nl, 8 * bl, 4 * ns)
        dm_shard = lax.dynamic_slice(dm_all, (0, 0, my_chip() * ns), (nl, 8 * bl, ns))
        dw, db = _ada_bwd_call(name + "_bwd", c_all, dm_shard, dm_all)
        return jnp.zeros_like(c_all), dw, db.reshape(nl, 4 * ns)

    ada.defvjp(fwd, bwd)
    return ada


def _adamw(name, w, g, m, v):
    shape = w.shape
    cols = shape[-1]
    w2, g2, m2, v2 = (t.reshape(-1, cols) for t in (w, g, m, v))
    rows = w2.shape[0]
    cap = max(SUBLANES, (512 * 1024) // max(cols, 1) // SUBLANES * SUBLANES)
    tr = _divisors(rows, cap, SUBLANES)[-1]

    def body(w_ref, g_ref, m_ref, v_ref, d_ref, mo_ref, vo_ref):
        gv = g_ref[...]
        mn = ADAM_B1 * m_ref[...] + (1.0 - ADAM_B1) * gv
        vn = ADAM_B2 * v_ref[...] + (1.0 - ADAM_B2) * (gv * gv)
        m_hat = mn / (1.0 - ADAM_B1 ** ADAM_STEP)
        v_hat = vn / (1.0 - ADAM_B2 ** ADAM_STEP)
        d_ref[...] = -ADAM_LR * (m_hat / (jnp.sqrt(v_hat) + ADAM_EPS) + ADAM_WD * w_ref[...])
        mo_ref[...] = mn
        vo_ref[...] = vn

    spec = pl.BlockSpec((tr, cols), lambda i: (i, 0))
    outs = _pcall(body, name, [_sds((rows, cols), F32)] * 3, grid=(rows // tr,), in_specs=[spec] * 4, out_specs=[spec] * 3, sem=("parallel",))(w2, g2, m2, v2)
    return tuple(o.reshape(shape) for o in outs)


def _adamw_small(name, ws, gs, ms, vs):
    n = len(ws)

    def body(*refs):
        for k in range(n):
            w_ref, g_ref, m_ref, v_ref = (refs[j * n + k] for j in range(4))
            d_ref, mo_ref, vo_ref = (refs[(4 + j) * n + k] for j in range(3))
            gv = g_ref[...]
            mn = ADAM_B1 * m_ref[...] + (1.0 - ADAM_B1) * gv
            vn = ADAM_B2 * v_ref[...] + (1.0 - ADAM_B2) * (gv * gv)
            m_hat = mn / (1.0 - ADAM_B1 ** ADAM_STEP)
            v_hat = vn / (1.0 - ADAM_B2 ** ADAM_STEP)
            d_ref[...] = -ADAM_LR * (m_hat / (jnp.sqrt(v_hat) + ADAM_EPS) + ADAM_WD * w_ref[...])
            mo_ref[...] = mn
            vo_ref[...] = vn

    outs = _pcall(body, name, [_sds(w.shape, F32) for w in ws] * 3)(*ws, *gs, *ms, *vs)
    return outs[:n], outs[n:2 * n], outs[2 * n:]


def _pad_rows(w, rows):
    return jnp.pad(w, ((0, rows - w.shape[0]), (0, 0)))


def _pad_lanes(v):
    return jnp.pad(v, (0, LANES - v.shape[0])).reshape(1, LANES)


BIG = ("w_in", "w_out", "ff_up", "ff_down")
BIG_AXIS = {"w_in": 1, "w_out": 0, "ff_up": 1, "ff_down": 0}
CONVW = ("ssd_conv_w", "ff_conv_w")
SMALL = ("norm1_g", "norm2_g", "ssd_conv_b", "ssd_dt_bias", "ssd_a_log", "ssd_d", "ssd_norm_g", "gm_vnorm_g", "gm_ws", "gm_bs", "gm_out_g", "ff_conv_b")
WEIGHTS = ("ada_w", "ada_b", "norm1_g", "norm2_g", "w_in", "ssd_conv_w", "ssd_conv_b", "ssd_dt_bias", "ssd_a_log", "ssd_d", "ssd_norm_g", "gm_vnorm_g", "gm_ws", "gm_bs", "gm_out_g", "w_out", "ff_up", "ff_conv_w", "ff_conv_b", "ff_down", "final_g")


def kernel(x, c, ada_w, ada_b, norm1_g, norm2_g, w_in, ssd_conv_w, ssd_conv_b, ssd_dt_bias, ssd_a_log, ssd_d, ssd_norm_g, gm_vnorm_g, gm_ws, gm_bs, gm_out_g, w_out, ff_up, ff_conv_w, ff_conv_b, ff_down, final_g, loss_target, m_ada_w, m_ada_b, m_norm1_g, m_norm2_g, m_w_in, m_ssd_conv_w, m_ssd_conv_b, m_ssd_dt_bias, m_ssd_a_log, m_ssd_d, m_ssd_norm_g, m_gm_vnorm_g, m_gm_ws, m_gm_bs, m_gm_out_g, m_w_out, m_ff_up, m_ff_conv_w, m_ff_conv_b, m_ff_down, m_final_g, v_ada_w, v_ada_b, v_norm1_g, v_norm2_g, v_w_in, v_ssd_conv_w, v_ssd_conv_b, v_ssd_dt_bias, v_ssd_a_log, v_ssd_d, v_ssd_norm_g, v_gm_vnorm_g, v_gm_ws, v_gm_bs, v_gm_out_g, v_w_out, v_ff_up, v_ff_conv_w, v_ff_conv_b, v_ff_down, v_final_g):
    given = dict(locals())
    weights = {n: given[n] for n in WEIGHTS}
    bl, s, d = x.shape
    nl = ada_w.shape[0]
    heads = d // HEAD_DIM
    cd = d + 2 * SSD_GROUPS * STATE
    f = ff_down.shape[1] * 4
    n_in = d + cd + heads + 2 * d
    n_proj = _round_up(3 * d + cd + LANES, 2 * LANES)
    tm = _divisors(s, 512)[-1]

    gathering = _gather_weights_start("wg0", [weights[n][0] for n in BIG], c)
    pre = _gather_two_level("pre_ag", [_after(c, gathering["token"])] + [weights[n] for n in CONVW])
    c_all = pre[0].reshape(8 * bl, d)
    conv_full = [jnp.concatenate([p[k, 0] for k in range(4)], axis=-1) for p in pre[1:]]
    conv_full = dict(zip(CONVW, conv_full))
    mods_all, ada_vjp = jax.vjp(functools.partial(_make_ada("ada", bl), c_all), ada_w, ada_b)
    mods = mods_all.reshape(nl, bl, N_MOD, 1, d)
    r2 = lambda v: v.reshape(1, -1)
    n_gm = d // GM_HEAD
    fg = r2(final_g)

    def layer_params(l):
        return dict(
            norm1=r2(norm1_g[l]), norm2=r2(norm2_g[l]), conv_w8=_pad_rows(conv_full["ssd_conv_w"][l], SUBLANES), conv_b=r2(ssd_conv_b[l]),
            dtb=_pad_lanes(ssd_dt_bias[l]), alog=_pad_lanes(ssd_a_log[l]), dsk=r2(jnp.repeat(ssd_d[l], HEAD_DIM)), ng=r2(ssd_norm_g[l]),
            vg=r2(gm_vnorm_g[l]), ws=gm_ws[l], bs_t=jnp.pad(gm_bs[l].T, ((0, 0), (0, LANES - n_gm))), og=r2(gm_out_g[l]),
            ff_w8=_pad_rows(conv_full["ff_conv_w"][l], SUBLANES), ff_b=r2(ff_conv_b[l]))

    def mixer_args(p):
        return (p["conv_w8"], p["conv_b"], p["dtb"], p["alog"], p["dsk"], p["ng"], p["vg"], p["ws"], p["bs_t"], p["og"])

    def layer_weights(full):
        w = {n: t.reshape(-1, t.shape[2]) if BIG_AXIS[n] == 0 else jnp.concatenate([t[k] for k in range(4)], axis=1) for n, t in zip(BIG, full)}
        wi = w["w_in"]
        w["w_in"] = jnp.concatenate(
            [wi[:, :d], wi[:, d + cd + heads:], wi[:, d:d + cd], wi[:, d + cd:d + cd + heads], jnp.zeros((d, n_proj - (3 * d + cd + heads)), BF16)], axis=1)
        return w

    saved, xcur, pending = [], x, None
    for l in range(nl):
        p = layer_params(l)
        w = layer_weights(_gather_weights_finish(f"wg{l}", gathering, mods_all if l == 0 else pending[0]))
        if l + 1 < nl:
            gathering = _gather_weights_start(f"wg{l + 1}", [weights[n][l + 1] for n in BIG], w["w_out"])
            xcur = _after(xcur, gathering["token"])
        sh1, sc1, g1, sh2, sc2, g2 = (mods[l, :, k] for k in range(N_MOD))
        if pending is None:
            (h,) = _row_fwd(f"nm{l}_fwd", _nm_tile, [(xcur, d, 0, None)], [sc1, sh1], [p["norm1"]], [(d, BF16)], 0, tm)
            x1 = xcur
        else:
            rows = [(xcur, d, 0, None), (pending[0], d, 0, None)]
            x1, h = _row_fwd(f"rnm{l}a_fwd", _rnm_tile, rows, [pending[1], sc1, sh1], [p["norm1"]], [(d, F32), (d, BF16)], 0, tm)
        proj = _mm(f"win{l}_fwd", h, w["w_in"], F32)
        ycat, (xbc, hprev) = _mixer_fwd(f"mix{l}", proj, d, cd, *mixer_args(p))
        mix = _mm(f"wout{l}_fwd", ycat, w["w_out"], F32)
        x2, h2 = _row_fwd(f"rnm{l}b_fwd", _rnm_tile, [(x1, d, 0, None), (mix, d, 0, None)], [g1, sc2, sh2], [p["norm2"]], [(d, F32), (d, BF16)], 0, tm)
        up = _mm(f"ffup{l}_fwd", h2, w["ff_up"], F32)
        act = _conv_act_fwd(f"ffact{l}_fwd", up, 0, f, p["ff_w8"], p["ff_b"], FF_CONV, f, BF16)
        down = _mm(f"ffdown{l}_fwd", act, w["ff_down"], F32)
        saved.append(dict(p=p, w=w, x_in=xcur, pending=pending, h=h, proj=proj, xbc=xbc, hprev=hprev, ycat=ycat, x1=x1, mix=mix, h2=h2, up=up, act=act))
        xcur, pending = x2, (down, g2)
    rows = [(xcur, d, 0, F32), (pending[0], d, 0, BF16), (loss_target, d, 0, None)]
    (loss_local,) = _row_fwd("final_fwd", _final_tile, rows, [pending[1]], [fg], [], 1, tm)
    loss = lax.psum(loss_local[0, 0], AXES)

    (dx2, ddown), (dg2,), (dfg,) = _row_bwd("final_bwd", _final_tile, rows, [pending[1]], [fg], [], 1, tm, [jnp.ones((1, 1), F32)])
    dmods, small_grads, big_grads = [None] * nl, [None] * nl, [None] * nl
    reducing = None
    for l in reversed(range(nl)):
        sv = saved[l]
        p, w = sv["p"], sv["w"]
        sh1, sc1, g1, sh2, sc2, g2 = (mods[l, :, k] for k in range(N_MOD))
        if reducing is not None:
            ddown = _after(ddown, reducing["token"])
        dact, dw_down = _mm_bwd(f"ffdown{l}", sv["act"], w["ff_down"], ddown)
        dup, dff_w8, dff_b = _conv_act_bwd(f"ffact{l}_bwd", sv["up"], 0, f, p["ff_w8"], p["ff_b"], FF_CONV, f, dact, BF16)
        dh2, dw_up = _mm_bwd(f"ffup{l}", sv["h2"], w["ff_up"], dup)
        rows = [(sv["x1"], d, 0, F32), (sv["mix"], d, 0, BF16)]
        (dx1, dmix), (dg1, dsc2, dsh2), (dn2,) = _row_bwd(f"rnm{l}b_bwd", _rnm_tile, rows, [g1, sc2, sh2], [p["norm2"]], [(d, F32), (d, BF16)], 0, tm, [dx2, dh2])
        dycat, dw_out = _mm_bwd(f"wout{l}", sv["ycat"], w["w_out"], dmix)
        dproj, (dw8, dcb, ddtb, dalog, ddsk, dng, dvg, dws, dbs_t, dog) = _mixer_bwd(f"mix{l}", sv["proj"], d, cd, *mixer_args(p), sv["xbc"], sv["hprev"], dycat)
        dh, dw_in_p = _mm_bwd(f"win{l}", sv["h"], w["w_in"], dproj)
        if sv["pending"] is None:
            (dx2,), (dsc1, dsh1), (dn1,) = _row_bwd(f"nm{l}_bwd", _nm_res_tile, [(sv["x_in"], d, 0, F32)], [sc1, sh1], [p["norm1"]], [(d, F32), (d, BF16)], 0, tm, [dx1, dh])
        else:
            rows = [(sv["x_in"], d, 0, F32), (sv["pending"][0], d, 0, BF16)]
            (dx2, ddown), (dg2_prev, dsc1, dsh1), (dn1,) = _row_bwd(
                f"rnm{l}a_bwd", _rnm_tile, rows, [sv["pending"][1], sc1, sh1], [p["norm1"]], [(d, F32), (d, BF16)], 0, tm, [dx1, dh])
        dmods[l] = jnp.concatenate([dsh1, dsc1, dg1, dsh2, dsc2, dg2], axis=1).reshape(bl, N_MOD * d)
        if sv["pending"] is not None:
            dg2 = dg2_prev
        small_grads[l] = dict(
            norm1_g=dn1.reshape(d), norm2_g=dn2.reshape(d), ssd_conv_b=dcb.reshape(cd), ssd_dt_bias=ddtb[0, :heads], ssd_a_log=dalog[0, :heads],
            ssd_d=ddsk.reshape(heads, HEAD_DIM).sum(-1), ssd_norm_g=dng.reshape(d), gm_vnorm_g=dvg.reshape(d), gm_ws=dws, gm_bs=dbs_t[:, :n_gm].T,
            gm_out_g=dog.reshape(d), ff_conv_b=dff_b.reshape(f), ssd_conv_w=dw8[:SSD_CONV], ff_conv_w=dff_w8[:FF_CONV])
        dw_in = jnp.concatenate([dw_in_p[:, :d], dw_in_p[:, 3 * d:3 * d + cd], dw_in_p[:, 3 * d + cd:3 * d + cd + heads], dw_in_p[:, d:3 * d]], axis=1)
        by_chip = {}
        for n, g in zip(BIG, (dw_in, dw_out, dw_up, dw_down)):
            width = g.shape[1] // 4
            by_chip[n] = g.reshape(4, g.shape[0] // 4, g.shape[1]) if BIG_AXIS[n] == 0 else jnp.stack([g[:, k * width:(k + 1) * width] for k in range(4)])
        if reducing is not None:
            big_grads[l + 1] = _reduce_weights_finish(f"wg{l + 1}", reducing, dx2)
            by_chip[BIG[0]] = _after(by_chip[BIG[0]], big_grads[l + 1][0])
        reducing = _reduce_weights_start(f"wg{l}", [by_chip[n] for n in BIG])
    grad_x = dx2

    g_ada_w, g_ada_b = ada_vjp(_after(jnp.stack(dmods), reducing["token"]))
    small_names_r = SMALL + CONVW + ("final_g",)
    summed = _allreduce_small("small", [jnp.stack([small_grads[l][n] for l in range(nl)]) for n in SMALL + CONVW] + [dfg])
    big_grads[0] = _reduce_weights_finish("wg0", reducing, summed[0])
    grad = {"ada_w": g_ada_w, "ada_b": g_ada_b}
    for j, n in enumerate(BIG):
        grad[n] = jnp.stack([big_grads[l][j] for l in range(nl)])
    grad.update(zip(small_names_r, summed))
    chip = 2 * lax.axis_index("x") + lax.axis_index("y")
    for n in CONVW:
        width = weights[n].shape[-1]
        grad[n] = lax.dynamic_slice_in_dim(grad[n], chip * width, width, axis=2)

    delta, new_m, new_v = {}, {}, {}
    for n in ("ada_w",) + BIG:
        delta[n], new_m[n], new_v[n] = _adamw("adam_" + n, weights[n], grad[n], given["m_" + n], given["v_" + n])
    small_names = ("ada_b",) + SMALL + CONVW + ("final_g",)
    as2d = lambda t: t.reshape(1, -1) if t.ndim == 1 else t
    res = _adamw_small(
        "adam_small", [as2d(weights[n]) for n in small_names], [as2d(grad[n]) for n in small_names],
        [as2d(given["m_" + n]) for n in small_names], [as2d(given["v_" + n]) for n in small_names])
    for out, vals in zip((delta, new_m, new_v), res):
        for n, val in zip(small_names, vals):
            out[n] = val.reshape(weights[n].shape)
    grad["final_g"] = grad["final_g"].reshape(final_g.shape)

    return (loss, grad_x, *[grad[n] for n in WEIGHTS], *[delta[n] for n in WEIGHTS], *[new_m[n] for n in WEIGHTS], *[new_v[n] for n in WEIGHTS])
```

```python
import functools

import jax
import jax.numpy as jnp
from jax import lax
from jax.experimental import pallas as pl
from jax.experimental.pallas import tpu as pltpu

F32 = jnp.float32
BF16 = jnp.bfloat16
EPS = 1e-6
CHUNK = 128
HEAD_DIM = 64
STATE = 128
GM_HEAD = 128
SSD_GROUPS = 2
SSD_CONV = 4
FF_CONV = 3
N_MOD = 6
LANES = 128
SUBLANES = 8
V7X_VMEM_LIMIT = 48 * 1024 * 1024
MM_VMEM_BUDGET = 30 * 1024 * 1024
MM_STEP_MACS = 512 * 1024 * 2560
V7X_MXU_FLOPS = 996e12
V7X_HBM_BYTES_PER_S = 3.3e12
V7X_STEP_SECONDS = 0.35e-6
ADAM_LR, ADAM_B1, ADAM_B2, ADAM_EPS, ADAM_WD, ADAM_STEP = 0.001, 0.9, 0.999, 1e-08, 0.01, 10
MESH = pl.DeviceIdType.MESH
AXES = ("x", "y", "c")


def _round_up(n, m):
    return (n + m - 1) // m * m


def _divisors(n, cap, mult=LANES):
    out = [t for t in range(mult, min(n, cap) + 1, mult) if n % t == 0]
    return out or [n]


def _pcall(body, name, out_shape, grid=(), in_specs=None, out_specs=None, scratch=(), sem=None, prefetch=0):
    params = pltpu.CompilerParams(dimension_semantics=sem, vmem_limit_bytes=V7X_VMEM_LIMIT)
    if prefetch:
        spec = pltpu.PrefetchScalarGridSpec(num_scalar_prefetch=prefetch, grid=grid, in_specs=in_specs, out_specs=out_specs, scratch_shapes=list(scratch))
        return pl.pallas_call(body, name=name, out_shape=out_shape, grid_spec=spec, compiler_params=params)
    if in_specs is None:
        return pl.pallas_call(body, name=name, out_shape=out_shape, compiler_params=params)
    return pl.pallas_call(body, name=name, out_shape=out_shape, grid=grid, in_specs=in_specs, out_specs=out_specs, scratch_shapes=list(scratch), compiler_params=params)


def _sds(shape, dtype):
    return jax.ShapeDtypeStruct(tuple(shape), dtype)


def _mm_tiles(m, n, k, a_bytes, b_bytes, o_bytes):
    best, best_key = None, None
    for tm in _divisors(m, 2048):
        for tn in _divisors(n, 2560):
            for tk in _divisors(k, 2560):
                vmem = 2 * (tm * tk * a_bytes + tk * tn * b_bytes + tm * tn * o_bytes) + tm * tn * 4
                if vmem > MM_VMEM_BUDGET or tm * tn * tk > MM_STEP_MACS:
                    continue
                ni, nj, nk = m // tm, n // tn, k // tk
                a_reads = 1 if nk == 1 else nj
                b_reads = 1 if (nk == 1 and nj == 1) else ni
                hbm = a_reads * m * k * a_bytes + b_reads * k * n * b_bytes + m * n * o_bytes
                t = max(2.0 * m * n * k / V7X_MXU_FLOPS, hbm / V7X_HBM_BYTES_PER_S) + ni * nj * nk * V7X_STEP_SECONDS
                key = (-t, tm * tn * tk)
                if best_key is None or key > best_key:
                    best, best_key = (tm, tn, tk), key
    return best


def _matmul(name, a, b, mode, out_dtype):
    if mode == "nn":
        (m, k), n = a.shape, b.shape[1]
    elif mode == "nt":
        (m, k), n = a.shape, b.shape[0]
    else:
        (k, m), n = a.shape, b.shape[1]
    tm, tn, tk = _mm_tiles(m, n, k, a.dtype.itemsize, b.dtype.itemsize, jnp.dtype(out_dtype).itemsize)
    nk = k // tk
    if mode == "nn":
        a_spec = pl.BlockSpec((tm, tk), lambda i, j, kk: (i, kk))
        b_spec = pl.BlockSpec((tk, tn), lambda i, j, kk: (kk, j))
        dims = ((1,), (0,))
    elif mode == "nt":
        a_spec = pl.BlockSpec((tm, tk), lambda i, j, kk: (i, kk))
        b_spec = pl.BlockSpec((tn, tk), lambda i, j, kk: (j, kk))
        dims = ((1,), (1,))
    else:
        a_spec = pl.BlockSpec((tk, tm), lambda i, j, kk: (kk, i))
        b_spec = pl.BlockSpec((tk, tn), lambda i, j, kk: (kk, j))
        dims = ((0,), (0,))

    def body(a_ref, b_ref, o_ref, acc_ref):
        kk = pl.program_id(2)
        p = lax.dot_general(a_ref[...].astype(BF16), b_ref[...].astype(BF16), (dims, ((), ())), preferred_element_type=F32)
        if nk == 1:
            o_ref[...] = p.astype(o_ref.dtype)
        else:
            @pl.when(kk == 0)
            def _():
                acc_ref[...] = p

            @pl.when(kk > 0)
            def _():
                acc_ref[...] += p

            @pl.when(kk == nk - 1)
            def _():
                o_ref[...] = acc_ref[...].astype(o_ref.dtype)

    return _pcall(
        body, name, _sds((m, n), out_dtype), grid=(m // tm, n // tn, nk), in_specs=[a_spec, b_spec],
        out_specs=pl.BlockSpec((tm, tn), lambda i, j, kk: (i, j)), scratch=[pltpu.VMEM((tm, tn), F32)],
        sem=("parallel", "parallel", "arbitrary"),
    )(a, b)


def _mm(name, a, w, out_dtype):
    return _matmul(name, a.reshape(-1, a.shape[-1]), w, "nn", out_dtype).reshape(a.shape[:-1] + (w.shape[1],))


def _mm_bwd(name, a, w, dy):
    a2, dy2 = a.reshape(-1, a.shape[-1]), dy.reshape(-1, dy.shape[-1])
    return _matmul(name + "_dx", dy2, w, "nt", BF16).reshape(a.shape), _matmul(name + "_dw", a2, dy2, "tn", BF16)


def _dg(a, b, dims):
    return lax.dot_general(a.astype(BF16), b.astype(BF16), (dims, ((), ())), preferred_element_type=F32)


@jax.custom_vjp
def _dot_nn(a, b):
    return _dg(a, b, ((1,), (0,)))


_dot_nn.defvjp(lambda a, b: (_dot_nn(a, b), (a, b)), lambda r, d: (_dg(d, r[1], ((1,), (1,))), _dg(r[0], d, ((0,), (0,)))))


@jax.custom_vjp
def _dot_nt(a, b):
    return _dg(a, b, ((1,), (1,)))


_dot_nt.defvjp(lambda a, b: (_dot_nt(a, b), (a, b)), lambda r, d: (_dg(d, r[1], ((1,), (0,))), _dg(d, r[0], ((0,), (0,)))))


@jax.custom_vjp
def _dot_tn(a, b):
    return _dg(a, b, ((0,), (0,)))


_dot_tn.defvjp(lambda a, b: (_dot_tn(a, b), (a, b)), lambda r, d: (_dg(r[1], d, ((1,), (1,))), _dg(r[0], d, ((1,), (0,)))))


def _exact_dot(a, c, dims):
    hi = a.astype(BF16)
    r1 = a - hi.astype(F32)
    mid = r1.astype(BF16)
    lo = (r1 - mid.astype(F32)).astype(BF16)
    cb = c.astype(BF16)
    f = lambda t: lax.dot_general(t, cb, (dims, ((), ())), preferred_element_type=F32)
    return f(hi) + f(mid) + f(lo)


@jax.custom_vjp
def _sel_right(a, c):
    return _exact_dot(a, c, ((1,), (0,)))


_sel_right.defvjp(lambda a, c: (_sel_right(a, c), c), lambda c, d: (_exact_dot(d, c, ((1,), (1,))), jnp.zeros_like(c)))


def _exact_dot_left(c, a, dims):
    hi = a.astype(BF16)
    r1 = a - hi.astype(F32)
    mid = r1.astype(BF16)
    lo = (r1 - mid.astype(F32)).astype(BF16)
    cb = c.astype(BF16)
    f = lambda t: lax.dot_general(cb, t, (dims, ((), ())), preferred_element_type=F32)
    return f(hi) + f(mid) + f(lo)


@jax.custom_vjp
def _sel_left(c, a):
    return _exact_dot_left(c, a, ((1,), (0,)))


_sel_left.defvjp(lambda c, a: (_sel_left(c, a), c), lambda c, d: (jnp.zeros_like(c), _exact_dot_left(c, d, ((0,), (0,)))))


def _sigmoid(x):
    return 1.0 / (1.0 + jnp.exp(-x))


def _rms(x, g):
    return x * lax.rsqrt(jnp.mean(x * x, axis=-1, keepdims=True) + EPS) * g


def _gelu(x):
    return 0.5 * x * (1.0 + lax.erf(x * (2.0 ** -0.5)))


def _causal(n):
    return lax.broadcasted_iota(jnp.int32, (n, n), 0) >= lax.broadcasted_iota(jnp.int32, (n, n), 1)


def _row_in_specs(rows, bparams, gparams, tm):
    specs = [pl.BlockSpec((1, tm, w), lambda b, i, cb=cb: (b, i, cb)) for (_, w, cb, _) in rows]
    specs += [pl.BlockSpec((1, 1, p.shape[-1]), lambda b, i: (b, 0, 0)) for p in bparams]
    specs += [pl.BlockSpec(p.shape, lambda b, i, n=p.ndim: (0,) * n) for p in gparams]
    return specs


def _row_vals(refs, n_rows, n_b, n_g):
    vals = [r[0].astype(F32) for r in refs[:n_rows]]
    vals += [r[0].astype(F32) for r in refs[n_rows:n_rows + n_b]]
    vals += [r[...].astype(F32) for r in refs[n_rows + n_b:n_rows + n_b + n_g]]
    return vals


def _row_fwd(name, tile, rows, bparams, gparams, outs, n_sum, tm):
    bl, s = rows[0][0].shape[:2]
    n_in = len(rows) + len(bparams) + len(gparams)

    def body(*refs):
        first = (pl.program_id(0) == 0) & (pl.program_id(1) == 0)
        res = tile(*_row_vals(refs, len(rows), len(bparams), len(gparams)))
        o_refs = refs[n_in:]
        for k in range(len(outs)):
            o_refs[k][0] = res[k].astype(o_refs[k].dtype)
        for k in range(n_sum):
            o_ref, val = o_refs[len(outs) + k], res[len(outs) + k]

            @pl.when(first)
            def _(o_ref=o_ref, val=val):
                o_ref[...] = val

            @pl.when(jnp.logical_not(first))
            def _(o_ref=o_ref, val=val):
                o_ref[...] += val

    out_shape = [_sds((bl, s, w), dt) for (w, dt) in outs] + [_sds((1, 1), F32)] * n_sum
    out_specs = [pl.BlockSpec((1, tm, w), lambda b, i: (b, i, 0)) for (w, _) in outs] + [pl.BlockSpec((1, 1), lambda b, i: (0, 0))] * n_sum
    return _pcall(
        body, name, out_shape, grid=(bl, s // tm), in_specs=_row_in_specs(rows, bparams, gparams, tm), out_specs=out_specs,
        sem=("arbitrary", "arbitrary"),
    )(*[r[0] for r in rows], *bparams, *gparams)


def _row_bwd(name, tile, rows, bparams, gparams, outs, n_sum, tm, cts):
    bl, s = rows[0][0].shape[:2]
    n_r, n_b, n_g = len(rows), len(bparams), len(gparams)
    n_in = n_r + n_b + n_g
    n_ct = len(outs) + n_sum
    grad_rows = [k for k in range(n_r) if rows[k][3]]

    def body(*refs):
        b, i = pl.program_id(0), pl.program_id(1)
        vals = _row_vals(refs, n_r, n_b, n_g)
        ct_refs = refs[n_in:n_in + n_ct]
        ct = [r[0].astype(F32) for r in ct_refs[:len(outs)]] + [r[...] for r in ct_refs[len(outs):]]
        _, vjp = jax.vjp(tile, *vals)
        grads = vjp(tuple(ct))
        o_refs = refs[n_in + n_ct:]
        for j, k in enumerate(grad_rows):
            o_refs[j][0] = grads[k].astype(o_refs[j].dtype)
        for k in range(n_b):
            o_ref, val = o_refs[len(grad_rows) + k], grads[n_r + k]

            @pl.when(i == 0)
            def _(o_ref=o_ref, val=val):
                o_ref[0] = val

            @pl.when(i > 0)
            def _(o_ref=o_ref, val=val):
                o_ref[0] += val

        first = (b == 0) & (i == 0)
        for k in range(n_g):
            o_ref, val = o_refs[len(grad_rows) + n_b + k], grads[n_r + n_b + k]

            @pl.when(first)
            def _(o_ref=o_ref, val=val):
                o_ref[...] = val

            @pl.when(jnp.logical_not(first))
            def _(o_ref=o_ref, val=val):
                o_ref[...] += val

    in_specs = _row_in_specs(rows, bparams, gparams, tm)
    in_specs += [pl.BlockSpec((1, tm, w), lambda b, i: (b, i, 0)) for (w, _) in outs] + [pl.BlockSpec((1, 1), lambda b, i: (0, 0))] * n_sum
    out_shape = [_sds((bl, s, rows[k][1]), rows[k][3]) for k in grad_rows]
    out_shape += [_sds(p.shape, F32) for p in bparams] + [_sds(p.shape, F32) for p in gparams]
    out_specs = [pl.BlockSpec((1, tm, rows[k][1]), lambda b, i: (b, i, 0)) for k in grad_rows]
    out_specs += [pl.BlockSpec((1, 1, p.shape[-1]), lambda b, i: (b, 0, 0)) for p in bparams]
    out_specs += [pl.BlockSpec(p.shape, lambda b, i, n=p.ndim: (0,) * n) for p in gparams]
    res = _pcall(
        body, name, out_shape, grid=(bl, s // tm), in_specs=in_specs, out_specs=out_specs, sem=("arbitrary", "arbitrary"),
    )(*[r[0] for r in rows], *bparams, *gparams, *cts)
    return res[:len(grad_rows)], res[len(grad_rows):len(grad_rows) + n_b], res[len(grad_rows) + n_b:]


def _nm_tile(x, sc, sh, g):
    return (_rms(x, g) * (1.0 + sc) + sh,)


def _nm_res_tile(x, sc, sh, g):
    return x, _rms(x, g) * (1.0 + sc) + sh


def _rnm_tile(x, o, gate, sc, sh, g):
    xn = x + gate * o
    return xn, _rms(xn, g) * (1.0 + sc) + sh


def _final_tile(x, o, tgt, gate, g):
    e = _rms(x + gate * o, g) - tgt
    return (0.5 * jnp.sum(jnp.mean(e * e, axis=-1, keepdims=True), axis=0, keepdims=True),)


def _gm_tile(u_in, v_in, vg, ws, bs_t, og):
    d = u_in.shape[1]
    u, vn = _gelu(u_in), _rms(_gelu(v_in), vg)
    causal = _causal(CHUNK)
    lane = lax.broadcasted_iota(jnp.int32, (1, LANES), 1)
    parts = []
    for h in range(d // GM_HEAD):
        bias = jnp.sum(bs_t * (lane == h).astype(F32), axis=1, keepdims=True)
        parts.append(_dot_nn(jnp.where(causal, ws[h], 0.0), vn[:, h * GM_HEAD:(h + 1) * GM_HEAD]) + bias)
    return (_rms(u * jnp.concatenate(parts, axis=1), og),)


def _shift_down(x, j):
    if j == 0:
        return x
    row = lax.broadcasted_iota(jnp.int32, x.shape, 0)
    return jnp.where(row >= j, pltpu.roll(x, j, 0), 0.0)


def _shift_up(x, j):
    if j == 0:
        return x
    s = x.shape[0]
    row = lax.broadcasted_iota(jnp.int32, x.shape, 0)
    return jnp.where(row < s - j, pltpu.roll(x, s - j, 0), 0.0)


def _conv_pre(x, w_ref, b_ref, taps):
    acc = x * w_ref[taps - 1:taps, :] + b_ref[...]
    for j in range(1, taps):
        acc = acc + _shift_down(x, j) * w_ref[taps - 1 - j:taps - j, :]
    return acc


def _conv_tc(*widths):
    for tc in (512, 256, 128):
        if all(w % tc == 0 for w in widths):
            return tc
    raise ValueError("conv widths must be multiples of the lane count")


def _conv_act_fwd(name, src, col0, chans, w8, b, taps, val_col0, out_dtype):
    bl, s, _ = src.shape
    gated = val_col0 is not None
    tc = _conv_tc(chans, col0, val_col0 or 0)

    def body(*refs):
        if gated:
            x_ref, v_ref, w_ref, b_ref, o_ref = refs
        else:
            x_ref, w_ref, b_ref, o_ref = refs
        pre = _conv_pre(x_ref[0], w_ref, b_ref, taps)
        y = pre * _sigmoid(pre)
        if gated:
            y = y * v_ref[0]
        o_ref[0] = y.astype(o_ref.dtype)

    in_specs = [pl.BlockSpec((1, s, tc), lambda j, bb: (bb, 0, col0 // tc + j))]
    args = [src]
    if gated:
        in_specs.append(pl.BlockSpec((1, s, tc), lambda j, bb: (bb, 0, val_col0 // tc + j)))
        args.append(src)
    in_specs += [pl.BlockSpec((SUBLANES, tc), lambda j, bb: (0, j)), pl.BlockSpec((1, tc), lambda j, bb: (0, j))]
    return _pcall(
        body, name, _sds((bl, s, chans), out_dtype), grid=(chans // tc, bl), in_specs=in_specs,
        out_specs=pl.BlockSpec((1, s, tc), lambda j, bb: (bb, 0, j)), sem=("parallel", "arbitrary"),
    )(*args, w8, b)


def _conv_act_bwd(name, src, col0, chans, w8, b, taps, val_col0, dy, dx_dtype):
    bl, s, _ = src.shape
    gated = val_col0 is not None
    tc = _conv_tc(chans, col0, val_col0 or 0)
    nblk = chans // tc

    def body(*refs):
        if gated:
            x_ref, v_ref, w_ref, b_ref, dy_ref, dx_ref, dw_ref, db_ref = refs
        else:
            x_ref, w_ref, b_ref, dy_ref, dx_ref, dw_ref, db_ref = refs
        bb = pl.program_id(1)
        x = x_ref[0]
        pre = _conv_pre(x, w_ref, b_ref, taps)
        sig = _sigmoid(pre)
        d = dy_ref[0].astype(F32)

        def gate_part():
            dsil = d * v_ref[0] if gated else d
            dpre = dsil * sig * (1.0 + pre * (1.0 - sig))
            dx = dpre * w_ref[taps - 1:taps, :]
            for j in range(1, taps):
                dx = dx + _shift_up(dpre, j) * w_ref[taps - 1 - j:taps - j, :]
            dx_ref[0] = dx.astype(dx_ref.dtype)
            rows = [jnp.sum(dpre * _shift_down(x, taps - 1 - k), axis=0, keepdims=True) for k in range(taps)]
            rows += [jnp.zeros_like(rows[0])] * (SUBLANES - taps)
            dw = jnp.concatenate(rows, axis=0)
            db = jnp.sum(dpre, axis=0, keepdims=True)

            @pl.when(bb == 0)
            def _():
                dw_ref[...] = dw
                db_ref[...] = db

            @pl.when(bb > 0)
            def _():
                dw_ref[...] += dw
                db_ref[...] += db

        if gated:
            part = pl.program_id(2)
            pl.when(part == 0)(gate_part)

            @pl.when(part == 1)
            def _():
                dx_ref[0] = (d * (pre * sig)).astype(dx_ref.dtype)
        else:
            gate_part()

    if gated:
        grid, sem = (nblk, bl, 2), ("parallel", "arbitrary", "arbitrary")
        im = lambda f: (lambda j, bb, p: f(j, bb, p))
    else:
        grid, sem = (nblk, bl), ("parallel", "arbitrary")
        im = lambda f: (lambda j, bb: f(j, bb, 0))
    in_specs = [pl.BlockSpec((1, s, tc), im(lambda j, bb, p: (bb, 0, col0 // tc + j)))]
    args = [src]
    if gated:
        in_specs.append(pl.BlockSpec((1, s, tc), im(lambda j, bb, p: (bb, 0, val_col0 // tc + j))))
        args.append(src)
    in_specs += [
        pl.BlockSpec((SUBLANES, tc), im(lambda j, bb, p: (0, j))), pl.BlockSpec((1, tc), im(lambda j, bb, p: (0, j))),
        pl.BlockSpec((1, s, tc), im(lambda j, bb, p: (bb, 0, j))),
    ]
    out_shape = [_sds((bl, s, chans * (2 if gated else 1)), dx_dtype), _sds((SUBLANES, chans), F32), _sds((1, chans), F32)]
    out_specs = [
        pl.BlockSpec((1, s, tc), im(lambda j, bb, p: (bb, 0, j + p * nblk))), pl.BlockSpec((SUBLANES, tc), im(lambda j, bb, p: (0, j))),
        pl.BlockSpec((1, tc), im(lambda j, bb, p: (0, j))),
    ]
    return _pcall(body, name, out_shape, grid=grid, in_specs=in_specs, out_specs=out_specs, sem=sem)(*args, w8, b, dy)


def _ssd_chunk(g_idx, states, xs, bm, cm, dtr, z, dtb, alog, dsk, ng):
    gw = xs.shape[1]
    hpg = gw // HEAD_DIM
    dt = jax.nn.softplus(dtr + dtb)
    da = dt * (-jnp.exp(alog))
    causal = _causal(CHUNK)
    acs = _sel_left(causal.astype(F32), da)
    head_of_col = g_idx * hpg + lax.broadcasted_iota(jnp.int32, (LANES, gw), 1) // HEAD_DIM
    expand = (lax.broadcasted_iota(jnp.int32, (LANES, gw), 0) == head_of_col).astype(F32)
    dt_e, acs_e = _sel_right(dt, expand), _sel_right(acs, expand)
    last = lax.broadcasted_iota(jnp.int32, (CHUNK, gw), 0) == CHUNK - 1
    alast_e = jnp.sum(jnp.where(last, acs_e, 0.0), axis=0, keepdims=True)
    xc = xs * dt_e
    xc_st = xc * jnp.exp(alast_e - acs_e)
    decay_out, chunk_decay = jnp.exp(acs_e), jnp.exp(alast_e)
    cb = _dot_nt(cm, bm)
    acs_t = acs.T
    lane = lax.broadcasted_iota(jnp.int32, (1, LANES), 1)
    sub = lax.broadcasted_iota(jnp.int32, (LANES, 1), 0)
    ys, new_states = [], []
    for p in range(gw // LANES):
        sl = slice(p * LANES, (p + 1) * LANES)
        xcp = xc[:, sl]
        y = _dot_nn(cm, states[p]) * decay_out[:, sl]
        for q in range(2):
            head = g_idx * hpg + 2 * p + q
            col = jnp.sum(acs * (lane == head).astype(F32), axis=1, keepdims=True)
            row = jnp.sum(acs_t * (sub == head).astype(F32), axis=0, keepdims=True)
            decay = jnp.where(causal, jnp.exp(jnp.where(causal, col - row, 0.0)), 0.0)
            half = ((lane // HEAD_DIM) == q).astype(F32)
            y = y + _dot_nn(cb * decay, xcp * half)
        ys.append(y)
        new_states.append(states[p] * chunk_decay[:, sl] + _dot_tn(bm, xc_st[:, sl]))
    y = jnp.concatenate(ys, axis=1) + dsk * xs
    gated = y * (z * _sigmoid(z))
    return tuple(new_states), _rms(gated, ng)


def _ssd_specs(d, gw, dt_col, rev, nc):
    ci = (lambda i: nc - 1 - i) if rev else (lambda i: i)
    return [
        pl.BlockSpec((1, CHUNK, gw), lambda g, b, i: (b, ci(i), g)),
        pl.BlockSpec((1, CHUNK, STATE), lambda g, b, i: (b, ci(i), d // STATE + g)),
        pl.BlockSpec((1, CHUNK, STATE), lambda g, b, i: (b, ci(i), d // STATE + SSD_GROUPS + g)),
        pl.BlockSpec((1, CHUNK, LANES), lambda g, b, i: (b, ci(i), dt_col // LANES)),
        pl.BlockSpec((1, CHUNK, gw), lambda g, b, i: (b, ci(i), g)),
        pl.BlockSpec((1, LANES), lambda g, b, i: (0, 0)), pl.BlockSpec((1, LANES), lambda g, b, i: (0, 0)),
        pl.BlockSpec((1, gw), lambda g, b, i: (0, g)), pl.BlockSpec((1, gw), lambda g, b, i: (0, g)),
    ]


def _ssd_fwd(name, xbc, proj, dt_col, dtb, alog, dsk, ng):
    bl, s, cd = xbc.shape
    d = dsk.shape[1]
    gw, nc = d // SSD_GROUPS, s // CHUNK
    npair = gw // LANES

    def body(xs_ref, bm_ref, cm_ref, dt_ref, z_ref, dtb_ref, alog_ref, dsk_ref, ng_ref, y_ref, hp_ref, st_ref):
        g, i = pl.program_id(0), pl.program_id(2)

        @pl.when(i == 0)
        def _():
            st_ref[...] = jnp.zeros_like(st_ref)

        states = tuple(st_ref[p] for p in range(npair))
        hp_ref[0, 0, 0] = st_ref[...]
        new_states, yn = _ssd_chunk(g, states, xs_ref[0], bm_ref[0], cm_ref[0], dt_ref[0], z_ref[0], dtb_ref[...], alog_ref[...], dsk_ref[...], ng_ref[...])
        for p in range(npair):
            st_ref[p] = new_states[p]
        y_ref[0] = yn.astype(y_ref.dtype)

    out_shape = [_sds((bl, s, d), BF16), _sds((SSD_GROUPS, bl, nc, npair, STATE, LANES), F32)]
    out_specs = [
        pl.BlockSpec((1, CHUNK, gw), lambda g, b, i: (b, i, g)),
        pl.BlockSpec((1, 1, 1, npair, STATE, LANES), lambda g, b, i: (g, b, i, 0, 0, 0)),
    ]
    return _pcall(
        body, name, out_shape, grid=(SSD_GROUPS, bl, nc), in_specs=_ssd_specs(d, gw, dt_col, False, nc), out_specs=out_specs,
        scratch=[pltpu.VMEM((npair, STATE, LANES), F32)], sem=("arbitrary", "arbitrary", "arbitrary"),
    )(xbc, xbc, xbc, proj, proj, dtb, alog, dsk, ng)


def _ssd_bwd(name, xbc, proj, dt_col, dtb, alog, dsk, ng, hprev, dy):
    bl, s, cd = xbc.shape
    d = dsk.shape[1]
    gw, nc = d // SSD_GROUPS, s // CHUNK
    npair = gw // LANES

    def body(xs_ref, bm_ref, cm_ref, dt_ref, z_ref, dtb_ref, alog_ref, dsk_ref, ng_ref, hp_ref, dy_ref,
             dxs_ref, dbm_ref, dcm_ref, ddt_ref, dz_ref, ddtb_ref, dalog_ref, ddsk_ref, dng_ref, dst_ref):
        g, b, i = pl.program_id(0), pl.program_id(1), pl.program_id(2)

        @pl.when(i == 0)
        def _():
            dst_ref[...] = jnp.zeros_like(dst_ref)

        states = tuple(hp_ref[0, 0, 0, p] for p in range(npair))
        step = functools.partial(_ssd_chunk, g)
        _, vjp = jax.vjp(step, states, xs_ref[0], bm_ref[0], cm_ref[0], dt_ref[0], z_ref[0], dtb_ref[...], alog_ref[...], dsk_ref[...], ng_ref[...])
        d_states, dxs, dbm, dcm, ddt, dz, ddtb, dalog, ddsk, dng = vjp((tuple(dst_ref[p] for p in range(npair)), dy_ref[0].astype(F32)))
        for p in range(npair):
            dst_ref[p] = d_states[p]
        dxs_ref[0], dbm_ref[0], dcm_ref[0] = dxs, dbm, dcm
        dz_ref[0] = dz.astype(dz_ref.dtype)
        ddt_ref[0, 0] = ddt
        first_g = (b == 0) & (i == 0)
        first = first_g & (g == 0)
        for o_ref, val, init in ((ddtb_ref, ddtb, first), (dalog_ref, dalog, first), (ddsk_ref, ddsk, first_g), (dng_ref, dng, first_g)):
            @pl.when(init)
            def _(o_ref=o_ref, val=val):
                o_ref[...] = val

            @pl.when(jnp.logical_not(init))
            def _(o_ref=o_ref, val=val):
                o_ref[...] += val

    rc = lambda i: nc - 1 - i
    in_specs = _ssd_specs(d, gw, dt_col, True, nc) + [
        pl.BlockSpec((1, 1, 1, npair, STATE, LANES), lambda g, b, i: (g, b, rc(i), 0, 0, 0)),
        pl.BlockSpec((1, CHUNK, gw), lambda g, b, i: (b, rc(i), g)),
    ]
    out_shape = [
        _sds((bl, s, d), F32), _sds((bl, s, SSD_GROUPS * STATE), F32), _sds((bl, s, SSD_GROUPS * STATE), F32),
        _sds((SSD_GROUPS, bl, s, LANES), F32), _sds((bl, s, d), BF16),
        _sds((1, LANES), F32), _sds((1, LANES), F32), _sds((1, d), F32), _sds((1, d), F32),
    ]
    out_specs = [
        pl.BlockSpec((1, CHUNK, gw), lambda g, b, i: (b, rc(i), g)),
        pl.BlockSpec((1, CHUNK, STATE), lambda g, b, i: (b, rc(i), g)), pl.BlockSpec((1, CHUNK, STATE), lambda g, b, i: (b, rc(i), g)),
        pl.BlockSpec((1, 1, CHUNK, LANES), lambda g, b, i: (g, b, rc(i), 0)),
        pl.BlockSpec((1, CHUNK, gw), lambda g, b, i: (b, rc(i), g)),
        pl.BlockSpec((1, LANES), lambda g, b, i: (0, 0)), pl.BlockSpec((1, LANES), lambda g, b, i: (0, 0)),
        pl.BlockSpec((1, gw), lambda g, b, i: (0, g)), pl.BlockSpec((1, gw), lambda g, b, i: (0, g)),
    ]
    return _pcall(
        body, name, out_shape, grid=(SSD_GROUPS, bl, nc), in_specs=in_specs, out_specs=out_specs,
        scratch=[pltpu.VMEM((npair, STATE, LANES), F32)], sem=("arbitrary", "arbitrary", "arbitrary"),
    )(xbc, xbc, xbc, proj, proj, dtb, alog, dsk, ng, hprev, dy)


def _mixer_fwd(name, proj, d, cd, conv_w8, conv_b, dtb, alog, dsk, ng, vg, ws, bs_t, og):
    xbc = _conv_act_fwd(name + "_conv_fwd", proj, 3 * d, cd, conv_w8, conv_b, SSD_CONV, None, F32)
    y, hprev = _ssd_fwd(name + "_ssd_fwd", xbc, proj, 3 * d + cd, dtb, alog, dsk, ng)
    rows = [(proj, d, 1, BF16), (proj, d, 2, BF16)]
    (g_out,) = _row_fwd(name + "_gm_fwd", _gm_tile, rows, [], [vg, ws, bs_t, og], [(d, BF16)], 0, CHUNK)
    return jnp.concatenate([y, g_out], axis=-1), (xbc, hprev)


def _mixer_bwd(name, proj, d, cd, conv_w8, conv_b, dtb, alog, dsk, ng, vg, ws, bs_t, og, xbc, hprev, dycat):
    bl, s, n_proj = proj.shape
    dt_col = 3 * d + cd
    dy, dg_out = dycat[..., :d], dycat[..., d:]
    dxs, dbm, dcm, ddt2, dz, ddtb, dalog, ddsk, dng = _ssd_bwd(name + "_ssd_bwd", xbc, proj, dt_col, dtb, alog, dsk, ng, hprev, dy)
    dxbc_act = jnp.concatenate([dxs, dbm, dcm], axis=-1)
    dxbc, dw8, dcb = _conv_act_bwd(name + "_conv_bwd", proj, 3 * d, cd, conv_w8, conv_b, SSD_CONV, None, dxbc_act, BF16)
    rows = [(proj, d, 1, BF16), (proj, d, 2, BF16)]
    (du, dv), _, (dvg, dws, dbs_t, dog) = _row_bwd(name + "_gm_bwd", _gm_tile, rows, [], [vg, ws, bs_t, og], [(d, BF16)], 0, CHUNK, [dg_out])
    ddt = (ddt2[0] + ddt2[1]).astype(BF16)
    pad = jnp.zeros((bl, s, n_proj - dt_col - LANES), BF16)
    dproj = jnp.concatenate([dz, du, dv, dxbc, ddt, pad], axis=-1)
    return dproj, (dw8, dcb, ddtb, dalog, ddsk, dng, dvg, dws, dbs_t, dog)


def _position():
    return lax.axis_index("x"), lax.axis_index("y"), lax.axis_index("c")


def _at(ref, idx):
    return ref.at[idx] if len(idx) else ref


def _exchange(name, inputs, out_shapes, plan):
    n_in, n_out = len(inputs), len(out_shapes)
    n_copy = len(plan(0, 0, 0))

    def body(*refs):
        in_refs, out_refs = refs[:n_in], refs[n_in:n_in + n_out]
        send_sems, recv_sems = refs[n_in + n_out:]
        x, y, c = _position()
        copies = plan(x, y, c)

        def copy(k, src, dst, peer):
            return pltpu.make_async_remote_copy(src_ref=src, dst_ref=dst, send_sem=send_sems.at[k], recv_sem=recv_sems.at[k], device_id=peer, device_id_type=MESH)

        sends = [copy(k, _at(in_refs[sa], si), _at(out_refs[da], di), peer) for k, (sa, si, da, di, peer, _) in enumerate(copies)]
        for cp in sends:
            cp.start()
        for k, (sa, si, da, _, peer, li) in enumerate(copies):
            copy(k, _at(in_refs[sa], si), _at(out_refs[da], li), peer).wait_recv()
        for cp in sends:
            cp.wait_send()

    any_spec = pl.BlockSpec(memory_space=pl.ANY)
    outs = pl.pallas_call(
        body, name=name, out_shape=[_sds(s, dt) for s, dt in out_shapes], in_specs=[any_spec] * n_in, out_specs=[any_spec] * n_out,
        scratch_shapes=[pltpu.SemaphoreType.DMA((n_copy,)), pltpu.SemaphoreType.DMA((n_copy,))],
    )(*inputs)
    return list(outs)


def _exchange_start(name, inputs, out_shapes, plan):
    n_in, n_out = len(inputs), len(out_shapes)
    n_copy = len(plan(0, 0, 0))

    def body(*refs):
        in_refs, land_refs = refs[:n_in], refs[n_in:n_in + n_out]
        send_sems, recv_sems = refs[n_in + n_out:n_in + n_out + 2]
        token = refs[-1]
        x, y, c = _position()
        for k, (sa, si, da, di, peer, _) in enumerate(plan(x, y, c)):
            pltpu.make_async_remote_copy(
                src_ref=_at(in_refs[sa], si), dst_ref=_at(land_refs[da], di), send_sem=send_sems.at[k], recv_sem=recv_sems.at[k],
                device_id=peer, device_id_type=MESH).start()
        token[...] = jnp.zeros_like(token)

    hbm, sem = pl.BlockSpec(memory_space=pltpu.HBM), pl.BlockSpec(memory_space=pltpu.SEMAPHORE)
    lands = [lax.empty(s, dt) for s, dt in out_shapes]
    args = [pltpu.with_memory_space_constraint(a, pltpu.HBM) for a in list(inputs) + lands]
    outs = pl.pallas_call(
        body, name=name,
        out_shape=(pltpu.SemaphoreType.DMA((n_copy,)), pltpu.SemaphoreType.DMA((n_copy,)), *[pltpu.HBM(a.shape, a.dtype) for a in args], _sds((SUBLANES, LANES), F32)),
        in_specs=[hbm] * (n_in + n_out), out_specs=(sem, sem, *[hbm] * (n_in + n_out), pl.BlockSpec(memory_space=pltpu.VMEM)),
        input_output_aliases={i: 2 + i for i in range(n_in + n_out)},
        compiler_params=pltpu.CompilerParams(has_side_effects=pltpu.SideEffectType.DATAFLOW_SIDE_EFFECTING),
    )(*args)
    return dict(name=name, plan=plan, sems=outs[:2], ins=list(outs[2:2 + n_in]), lands=list(outs[2 + n_in:2 + n_in + n_out]), token=outs[-1])


def _exchange_wait(started, after):
    plan, n_in, n_out = started["plan"], len(started["ins"]), len(started["lands"])

    def body(*refs):
        in_refs, land_refs = refs[:n_in], refs[n_in:n_in + n_out]
        send_sems, recv_sems = refs[n_in + n_out:n_in + n_out + 2]
        token = refs[-1]
        x, y, c = _position()
        for k, (sa, si, da, _, peer, li) in enumerate(plan(x, y, c)):
            cp = pltpu.make_async_remote_copy(
                src_ref=_at(in_refs[sa], si), dst_ref=_at(land_refs[da], li), send_sem=send_sems.at[k], recv_sem=recv_sems.at[k],
                device_id=peer, device_id_type=MESH)
            cp.wait_send()
            cp.wait_recv()
        token[...] = jnp.zeros_like(token)

    hbm, sem = pl.BlockSpec(memory_space=pltpu.HBM), pl.BlockSpec(memory_space=pltpu.SEMAPHORE)
    bufs = started["ins"] + started["lands"]
    outs = pl.pallas_call(
        body, name=started["name"] + "_wait", out_shape=(*[pltpu.HBM(a.shape, a.dtype) for a in bufs], _sds((SUBLANES, LANES), F32)),
        in_specs=[hbm] * len(bufs) + [sem, sem, pl.BlockSpec(memory_space=pl.ANY)], out_specs=(*[hbm] * len(bufs), pl.BlockSpec(memory_space=pltpu.VMEM)),
        input_output_aliases={i: i for i in range(len(bufs))},
        compiler_params=pltpu.CompilerParams(has_side_effects=pltpu.SideEffectType.DATAFLOW_SIDE_EFFECTING),
    )(*bufs, *started["sems"], after)
    return list(outs[:n_in]), list(outs[n_in:n_in + n_out]), outs[-1]


def _after(value, token):
    return value + token[0, 0].astype(value.dtype)


def _chip_peers(x, y, c):
    return [(1 - x, y, c), (x, 1 - y, c), (1 - x, 1 - y, c)]


def _chip_of(p):
    return 2 * p[0] + p[1]


def _set_slot(slots, me, blk):
    return lax.dynamic_update_slice(slots, blk[None], (me,) + (0,) * blk.ndim)


def _by_core(c, mine, other, axis):
    return jnp.where(c == 0, jnp.stack([mine, other], axis), jnp.stack([other, mine], axis))


def _gather_chips(name, blocks):
    n = len(blocks)

    def plan(x, y, c):
        return [(a, (), a, (2 * x + y,), p, (_chip_of(p),)) for a in range(n) for p in _chip_peers(x, y, c)]

    recv = _exchange(name, blocks, [((4,) + b.shape, b.dtype) for b in blocks], plan)
    x, y, _ = _position()
    return [_set_slot(r, 2 * x + y, b) for r, b in zip(recv, blocks)]


def _plan_gather_chips(n):
    def plan(x, y, c):
        return [(a, (), a, (2 * x + y,), p, (_chip_of(p),)) for a in range(n) for p in _chip_peers(x, y, c)]

    return plan


def _gather_pass_cores(name, blocks, from_chips):
    n = len(blocks)

    def plan_cores(x, y, c):
        me, sib = 2 * x + y, (x, y, 1 - c)
        own = [(a, (), a, (me,), sib, (me,)) for a in range(n)]
        passed = [(n + a, (_chip_of(p),), a, (_chip_of(p),), sib, (_chip_of(p),)) for a in range(n) for p in _chip_peers(x, y, c)]
        return own + passed

    from_core = _exchange(name + "_cores", list(blocks) + list(from_chips), [((4,) + b.shape, b.dtype) for b in blocks], plan_cores)
    x, y, c = _position()
    return [_by_core(c, _set_slot(r1, 2 * x + y, b), r2, 1) for b, r1, r2 in zip(blocks, from_chips, from_core)]


def _gather_two_level(name, blocks):
    from_chips = _exchange(name + "_chips", blocks, [((4,) + b.shape, b.dtype) for b in blocks], _plan_gather_chips(len(blocks)))
    return _gather_pass_cores(name, blocks, from_chips)


def _pair_add(name, g42, r4):
    _, _, rh, cols = g42.shape
    tr = _divisors(rh, 512, SUBLANES * 2)[-1]

    def body(c_ref, a_ref, b_ref, o_ref):
        o_ref[0] = (a_ref[0, 0].astype(F32) + b_ref[0].astype(F32)).astype(o_ref.dtype)

    cidx = lax.axis_index("c").astype(jnp.int32).reshape(1)
    return _pcall(
        body, name, _sds(r4.shape, BF16), grid=(4, rh // tr),
        in_specs=[pl.BlockSpec((1, 1, tr, cols), lambda s, i, c_ref: (s, c_ref[0], i, 0)), pl.BlockSpec((1, tr, cols), lambda s, i, c_ref: (s, i, 0))],
        out_specs=pl.BlockSpec((1, tr, cols), lambda s, i, c_ref: (s, i, 0)), sem=("parallel", "parallel"), prefetch=1,
    )(cidx, g42, r4)


def _slot_sum(name, parts):
    n, r, cols = parts.shape
    cap = max(2 * SUBLANES, (4 * 1024 * 1024) // (n * cols * parts.dtype.itemsize))
    tr = _divisors(r, cap, 2 * SUBLANES)[-1]

    def body(p_ref, o_ref):
        acc = p_ref[0].astype(F32)
        for k in range(1, n):
            acc = acc + p_ref[k].astype(F32)
        o_ref[...] = acc

    return _pcall(
        body, name, _sds((r, cols), F32), grid=(r // tr,), in_specs=[pl.BlockSpec((n, tr, cols), lambda i: (0, i, 0))],
        out_specs=pl.BlockSpec((tr, cols), lambda i: (i, 0)), sem=("parallel",),
    )(parts)


def _slot_sums(name, parts):
    k = len(parts)

    def body(*refs):
        for p_ref, o_ref in zip(refs[:k], refs[k:]):
            acc = p_ref[0]
            for j in range(1, p_ref.shape[0]):
                acc = acc + p_ref[j]
            o_ref[...] = acc

    return list(_pcall(body, name, [_sds(p.shape[1:], F32) for p in parts])(*parts))


def _gather_weights_start(name, shards, token):
    c = lax.axis_index("c")
    halves = [lax.dynamic_slice_in_dim(w, c * (w.shape[0] // 2), w.shape[0] // 2, 0).astype(BF16) for w in shards]
    if token is not None:
        halves[0] = _after(halves[0], token)
    return _exchange_start(name + "_ag_chips", halves, [((4,) + h.shape, h.dtype) for h in halves], _plan_gather_chips(len(halves)))


def _gather_weights_finish(name, started, after):
    halves, from_chips, token = _exchange_wait(started, after)
    full = [f.reshape((4, 2 * h.shape[0], h.shape[1])) for f, h in zip(_gather_pass_cores(name + "_ag", halves, from_chips), halves)]
    return full, token


def _reduce_weights_start(name, grads, token):
    n = len(grads)
    g42 = [g.reshape(4, 2, g.shape[1] // 2, g.shape[2]) for g in grads]
    if token is not None:
        g42[0] = _after(g42[0], token)

    def plan_swap(x, y, c):
        return [(a, (s, 1 - c), a, (s,), (x, y, 1 - c), (s,)) for a in range(n) for s in range(4)]

    def plan_chips(x, y, c):
        return [(a, (_chip_of(p),), a, (2 * x + y,), p, (_chip_of(p),)) for a in range(n) for p in _chip_peers(x, y, c)]

    other = _exchange(name + "_rs_cores", g42, [((4,) + g.shape[2:], g.dtype) for g in g42], plan_swap)
    pair = [_pair_add(f"{name}_rs_pair{a}", g, o) for a, (g, o) in enumerate(zip(g42, other))]
    return _exchange_start(name + "_rs_chips", pair, [(p.shape, p.dtype) for p in pair], plan_chips)


def _reduce_weights_finish(name, started, after):
    pair, recv, token = _exchange_wait(started, after)
    n = len(pair)
    x, y, c = _position()
    me = 2 * x + y

    def plan_share(x, y, c):
        return [(a, (), a, (), (x, y, 1 - c), ()) for a in range(n)]

    parts = [lax.dynamic_update_slice(r, lax.dynamic_slice_in_dim(p, me, 1, 0), (me, 0, 0)) for r, p in zip(recv, pair)]
    mine = [_slot_sum(f"{name}_rs_sum{a}", p) for a, p in enumerate(parts)]
    theirs = _exchange(name + "_rs_share", mine, [(m.shape, m.dtype) for m in mine], plan_share)
    return [_by_core(c, m, t, 0).reshape(2 * m.shape[0], m.shape[1]) for m, t in zip(mine, theirs)], token


def _allreduce_small(name, grads):
    allg = [g.reshape((8,) + g.shape[2:]) for g in _gather_two_level(name + "_ag", list(grads))]
    return _slot_sums(name + "_sum", allg)


def _ada_fwd_call(name, c_all, w, b_shard):
    nl, d, ns = w.shape
    nb = c_all.shape[0]

    def body(c_ref, w_ref, b_ref, o_ref):
        cv = c_ref[...]
        o_ref[0] = _dg(cv * _sigmoid(cv), w_ref[0], ((1,), (0,))) + b_ref[0]

    return _pcall(
        body, name, _sds((nl, nb, ns), F32), grid=(nl,),
        in_specs=[pl.BlockSpec((nb, d), lambda l: (0, 0)), pl.BlockSpec((1, d, ns), lambda l: (l, 0, 0)), pl.BlockSpec((1, 1, ns), lambda l: (l, 0, 0))],
        out_specs=pl.BlockSpec((1, nb, ns), lambda l: (l, 0, 0)), sem=("parallel",),
    )(c_all, w, b_shard)


def _ada_bwd_call(name, c_all, dm_shard, dm_all):
    nl, nb, ns = dm_shard.shape
    d = c_all.shape[1]
    nm = dm_all.shape[2]

    def body(c_ref, ds_ref, da_ref, dw_ref, db_ref):
        cv = c_ref[...]
        dw_ref[0] = _dg(cv * _sigmoid(cv), ds_ref[0], ((0,), (0,)))
        db_ref[0] = jnp.sum(da_ref[0], axis=0, keepdims=True)

    return _pcall(
        body, name, [_sds((nl, d, ns), F32), _sds((nl, 1, nm), F32)], grid=(nl,),
        in_specs=[pl.BlockSpec((nb, d), lambda l: (0, 0)), pl.BlockSpec((1, nb, ns), lambda l: (l, 0, 0)), pl.BlockSpec((1, nb, nm), lambda l: (l, 0, 0))],
        out_specs=[pl.BlockSpec((1, d, ns), lambda l: (l, 0, 0)), pl.BlockSpec((1, 1, nm), lambda l: (l, 0, 0))], sem=("parallel",),
    )(c_all, dm_shard, dm_all)


def _make_ada(name, bl):
    def my_chip():
        return 2 * lax.axis_index("x") + lax.axis_index("y")

    def my_dev():
        return 2 * my_chip() + lax.axis_index("c")

    @jax.custom_vjp
    def ada(c_all, w, b):
        nl, d, ns = w.shape
        b_shard = lax.dynamic_slice(b, (0, my_chip() * ns), (nl, ns)).reshape(nl, 1, ns)
        shard = _ada_fwd_call(name + "_fwd", c_all, w, b_shard)
        (allc,) = _gather_chips(name + "_ag", [shard])
        mods = jnp.transpose(allc, (1, 2, 0, 3)).reshape(nl, c_all.shape[0], 4 * ns)
        return lax.dynamic_slice(mods, (0, my_dev() * bl, 0), (nl, bl, 4 * ns))

    def fwd(c_all, w, b):
        return ada(c_all, w, b), (c_all, w.shape)

    def bwd(res, dm):
        c_all, (nl, d, ns) = res
        (dm_all,) = _gather_two_level(name + "_bwd_ag", [dm])
        dm_all = jnp.transpose(dm_all.reshape((8,) + dm.shape), (1, 0, 2, 3)).reshape(nl, 8 * bl, 4 * ns)
        dm_shard = lax.dynamic_slice(dm_all, (0, 0, my_chip() * ns), (nl, 8 * bl, ns))
        dw, db = _ada_bwd_call(name + "_bwd", c_all, dm_shard, dm_all)
        return jnp.zeros_like(c_all), dw, db.reshape(nl, 4 * ns)

    ada.defvjp(fwd, bwd)
    return ada


def _adamw(name, w, g, m, v):
    shape = w.shape
    cols = shape[-1]
    w2, g2, m2, v2 = (t.reshape(-1, cols) for t in (w, g, m, v))
    rows = w2.shape[0]
    cap = max(SUBLANES, (512 * 1024) // max(cols, 1) // SUBLANES * SUBLANES)
    tr = _divisors(rows, cap, SUBLANES)[-1]

    def body(w_ref, g_ref, m_ref, v_ref, d_ref, mo_ref, vo_ref):
        gv = g_ref[...]
        mn = ADAM_B1 * m_ref[...] + (1.0 - ADAM_B1) * gv
        vn = ADAM_B2 * v_ref[...] + (1.0 - ADAM_B2) * (gv * gv)
        m_hat = mn / (1.0 - ADAM_B1 ** ADAM_STEP)
        v_hat = vn / (1.0 - ADAM_B2 ** ADAM_STEP)
        d_ref[...] = -ADAM_LR * (m_hat / (jnp.sqrt(v_hat) + ADAM_EPS) + ADAM_WD * w_ref[...])
        mo_ref[...] = mn
        vo_ref[...] = vn

    spec = pl.BlockSpec((tr, cols), lambda i: (i, 0))
    outs = _pcall(body, name, [_sds((rows, cols), F32)] * 3, grid=(rows // tr,), in_specs=[spec] * 4, out_specs=[spec] * 3, sem=("parallel",))(w2, g2, m2, v2)
    return tuple(o.reshape(shape) for o in outs)


def _adamw_small(name, ws, gs, ms, vs):
    n = len(ws)

    def body(*refs):
        for k in range(n):
            w_ref, g_ref, m_ref, v_ref = (refs[j * n + k] for j in range(4))
            d_ref, mo_ref, vo_ref = (refs[(4 + j) * n + k] for j in range(3))
            gv = g_ref[...]
            mn = ADAM_B1 * m_ref[...] + (1.0 - ADAM_B1) * gv
            vn = ADAM_B2 * v_ref[...] + (1.0 - ADAM_B2) * (gv * gv)
            m_hat = mn / (1.0 - ADAM_B1 ** ADAM_STEP)
            v_hat = vn / (1.0 - ADAM_B2 ** ADAM_STEP)
            d_ref[...] = -ADAM_LR * (m_hat / (jnp.sqrt(v_hat) + ADAM_EPS) + ADAM_WD * w_ref[...])
            mo_ref[...] = mn
            vo_ref[...] = vn

    outs = _pcall(body, name, [_sds(w.shape, F32) for w in ws] * 3)(*ws, *gs, *ms, *vs)
    return outs[:n], outs[n:2 * n], outs[2 * n:]


def _pad_rows(w, rows):
    return jnp.pad(w, ((0, rows - w.shape[0]), (0, 0)))


def _pad_lanes(v):
    return jnp.pad(v, (0, LANES - v.shape[0])).reshape(1, LANES)


BIG = ("w_in", "w_out", "ff_up", "ff_down")
BIG_AXIS = {"w_in": 1, "w_out": 0, "ff_up": 1, "ff_down": 0}
CONVW = ("ssd_conv_w", "ff_conv_w")
SMALL = ("norm1_g", "norm2_g", "ssd_conv_b", "ssd_dt_bias", "ssd_a_log", "ssd_d", "ssd_norm_g", "gm_vnorm_g", "gm_ws", "gm_bs", "gm_out_g", "ff_conv_b")
WEIGHTS = ("ada_w", "ada_b", "norm1_g", "norm2_g", "w_in", "ssd_conv_w", "ssd_conv_b", "ssd_dt_bias", "ssd_a_log", "ssd_d", "ssd_norm_g", "gm_vnorm_g", "gm_ws", "gm_bs", "gm_out_g", "w_out", "ff_up", "ff_conv_w", "ff_conv_b", "ff_down", "final_g")


def kernel(x, c, ada_w, ada_b, norm1_g, norm2_g, w_in, ssd_conv_w, ssd_conv_b, ssd_dt_bias, ssd_a_log, ssd_d, ssd_norm_g, gm_vnorm_g, gm_ws, gm_bs, gm_out_g, w_out, ff_up, ff_conv_w, ff_conv_b, ff_down, final_g, loss_target, m_ada_w, m_ada_b, m_norm1_g, m_norm2_g, m_w_in, m_ssd_conv_w, m_ssd_conv_b, m_ssd_dt_bias, m_ssd_a_log, m_ssd_d, m_ssd_norm_g, m_gm_vnorm_g, m_gm_ws, m_gm_bs, m_gm_out_g, m_w_out, m_ff_up, m_ff_conv_w, m_ff_conv_b, m_ff_down, m_final_g, v_ada_w, v_ada_b, v_norm1_g, v_norm2_g, v_w_in, v_ssd_conv_w, v_ssd_conv_b, v_ssd_dt_bias, v_ssd_a_log, v_ssd_d, v_ssd_norm_g, v_gm_vnorm_g, v_gm_ws, v_gm_bs, v_gm_out_g, v_w_out, v_ff_up, v_ff_conv_w, v_ff_conv_b, v_ff_down, v_final_g):
    given = dict(locals())
    weights = {n: given[n] for n in WEIGHTS}
    bl, s, d = x.shape
    nl = ada_w.shape[0]
    heads = d // HEAD_DIM
    cd = d + 2 * SSD_GROUPS * STATE
    f = ff_down.shape[1] * 4
    n_in = d + cd + heads + 2 * d
    n_proj = _round_up(3 * d + cd + LANES, 2 * LANES)
    tm = _divisors(s, 512)[-1]

    gathering = _gather_weights_start("wg0", [weights[n][0] for n in BIG], None)
    pre = _gather_two_level("pre_ag", [_after(c, gathering["token"])] + [weights[n] for n in CONVW])
    c_all = pre[0].reshape(8 * bl, d)
    conv_full = [jnp.concatenate([p[k, 0] for k in range(4)], axis=-1) for p in pre[1:]]
    conv_full = dict(zip(CONVW, conv_full))
    mods_all, ada_vjp = jax.vjp(functools.partial(_make_ada("ada", bl), c_all), ada_w, ada_b)
    mods = mods_all.reshape(nl, bl, N_MOD, 1, d)
    r2 = lambda v: v.reshape(1, -1)
    n_gm = d // GM_HEAD
    fg = r2(final_g)

    def layer_params(l):
        return dict(
            norm1=r2(norm1_g[l]), norm2=r2(norm2_g[l]), conv_w8=_pad_rows(conv_full["ssd_conv_w"][l], SUBLANES), conv_b=r2(ssd_conv_b[l]),
            dtb=_pad_lanes(ssd_dt_bias[l]), alog=_pad_lanes(ssd_a_log[l]), dsk=r2(jnp.repeat(ssd_d[l], HEAD_DIM)), ng=r2(ssd_norm_g[l]),
            vg=r2(gm_vnorm_g[l]), ws=gm_ws[l], bs_t=jnp.pad(gm_bs[l].T, ((0, 0), (0, LANES - n_gm))), og=r2(gm_out_g[l]),
            ff_w8=_pad_rows(conv_full["ff_conv_w"][l], SUBLANES), ff_b=r2(ff_conv_b[l]))

    def mixer_args(p):
        return (p["conv_w8"], p["conv_b"], p["dtb"], p["alog"], p["dsk"], p["ng"], p["vg"], p["ws"], p["bs_t"], p["og"])

    def layer_weights(full):
        w = {n: t.reshape(-1, t.shape[2]) if BIG_AXIS[n] == 0 else jnp.concatenate([t[k] for k in range(4)], axis=1) for n, t in zip(BIG, full)}
        wi = w["w_in"]
        w["w_in"] = jnp.concatenate(
            [wi[:, :d], wi[:, d + cd + heads:], wi[:, d:d + cd], wi[:, d + cd:d + cd + heads], jnp.zeros((d, n_proj - (3 * d + cd + heads)), BF16)], axis=1)
        return w

    saved, xcur, pending = [], x, None
    for l in range(nl):
        p = layer_params(l)
        full, landed = _gather_weights_finish(f"wg{l}", gathering, mods_all if l == 0 else pending[0])
        w = layer_weights(full)
        sh1, sc1, g1, sh2, sc2, g2 = (mods[l, :, k] for k in range(N_MOD))
        if l + 1 < nl:
            gathering = _gather_weights_start(f"wg{l + 1}", [weights[n][l + 1] for n in BIG], landed)
            sc1 = _after(sc1, gathering["token"])
        if pending is None:
            (h,) = _row_fwd(f"nm{l}_fwd", _nm_tile, [(xcur, d, 0, None)], [sc1, sh1], [p["norm1"]], [(d, BF16)], 0, tm)
            x1 = xcur
        else:
            rows = [(xcur, d, 0, None), (pending[0], d, 0, None)]
            x1, h = _row_fwd(f"rnm{l}a_fwd", _rnm_tile, rows, [pending[1], sc1, sh1], [p["norm1"]], [(d, F32), (d, BF16)], 0, tm)
        proj = _mm(f"win{l}_fwd", h, w["w_in"], F32)
        ycat, (xbc, hprev) = _mixer_fwd(f"mix{l}", proj, d, cd, *mixer_args(p))
        mix = _mm(f"wout{l}_fwd", ycat, w["w_out"], F32)
        x2, h2 = _row_fwd(f"rnm{l}b_fwd", _rnm_tile, [(x1, d, 0, None), (mix, d, 0, None)], [g1, sc2, sh2], [p["norm2"]], [(d, F32), (d, BF16)], 0, tm)
        up = _mm(f"ffup{l}_fwd", h2, w["ff_up"], F32)
        act = _conv_act_fwd(f"ffact{l}_fwd", up, 0, f, p["ff_w8"], p["ff_b"], FF_CONV, f, BF16)
        down = _mm(f"ffdown{l}_fwd", act, w["ff_down"], F32)
        saved.append(dict(p=p, w=w, x_in=xcur, pending=pending, h=h, proj=proj, xbc=xbc, hprev=hprev, ycat=ycat, x1=x1, mix=mix, h2=h2, up=up, act=act))
        xcur, pending = x2, (down, g2)
    rows = [(xcur, d, 0, F32), (pending[0], d, 0, BF16), (loss_target, d, 0, None)]
    (loss_local,) = _row_fwd("final_fwd", _final_tile, rows, [pending[1]], [fg], [], 1, tm)
    loss = lax.psum(loss_local[0, 0], AXES)

    (dx2, ddown), (dg2,), (dfg,) = _row_bwd("final_bwd", _final_tile, rows, [pending[1]], [fg], [], 1, tm, [jnp.ones((1, 1), F32)])
    dmods, small_grads, big_grads = [None] * nl, [None] * nl, [None] * nl
    reducing = None
    for l in reversed(range(nl)):
        sv = saved[l]
        p, w = sv["p"], sv["w"]
        sh1, sc1, g1, sh2, sc2, g2 = (mods[l, :, k] for k in range(N_MOD))
        if reducing is not None:
            ddown = _after(ddown, reducing["token"])
        dact, dw_down = _mm_bwd(f"ffdown{l}", sv["act"], w["ff_down"], ddown)
        dup, dff_w8, dff_b = _conv_act_bwd(f"ffact{l}_bwd", sv["up"], 0, f, p["ff_w8"], p["ff_b"], FF_CONV, f, dact, BF16)
        dh2, dw_up = _mm_bwd(f"ffup{l}", sv["h2"], w["ff_up"], dup)
        rows = [(sv["x1"], d, 0, F32), (sv["mix"], d, 0, BF16)]
        (dx1, dmix), (dg1, dsc2, dsh2), (dn2,) = _row_bwd(f"rnm{l}b_bwd", _rnm_tile, rows, [g1, sc2, sh2], [p["norm2"]], [(d, F32), (d, BF16)], 0, tm, [dx2, dh2])
        dycat, dw_out = _mm_bwd(f"wout{l}", sv["ycat"], w["w_out"], dmix)
        dproj, (dw8, dcb, ddtb, dalog, ddsk, dng, dvg, dws, dbs_t, dog) = _mixer_bwd(f"mix{l}", sv["proj"], d, cd, *mixer_args(p), sv["xbc"], sv["hprev"], dycat)
        dh, dw_in_p = _mm_bwd(f"win{l}", sv["h"], w["w_in"], dproj)
        if sv["pending"] is None:
            (dx2,), (dsc1, dsh1), (dn1,) = _row_bwd(f"nm{l}_bwd", _nm_res_tile, [(sv["x_in"], d, 0, F32)], [sc1, sh1], [p["norm1"]], [(d, F32), (d, BF16)], 0, tm, [dx1, dh])
        else:
            rows = [(sv["x_in"], d, 0, F32), (sv["pending"][0], d, 0, BF16)]
            (dx2, ddown), (dg2_prev, dsc1, dsh1), (dn1,) = _row_bwd(
                f"rnm{l}a_bwd", _rnm_tile, rows, [sv["pending"][1], sc1, sh1], [p["norm1"]], [(d, F32), (d, BF16)], 0, tm, [dx1, dh])
        dmods[l] = jnp.concatenate([dsh1, dsc1, dg1, dsh2, dsc2, dg2], axis=1).reshape(bl, N_MOD * d)
        if sv["pending"] is not None:
            dg2 = dg2_prev
        small_grads[l] = dict(
            norm1_g=dn1.reshape(d), norm2_g=dn2.reshape(d), ssd_conv_b=dcb.reshape(cd), ssd_dt_bias=ddtb[0, :heads], ssd_a_log=dalog[0, :heads],
            ssd_d=ddsk.reshape(heads, HEAD_DIM).sum(-1), ssd_norm_g=dng.reshape(d), gm_vnorm_g=dvg.reshape(d), gm_ws=dws, gm_bs=dbs_t[:, :n_gm].T,
            gm_out_g=dog.reshape(d), ff_conv_b=dff_b.reshape(f), ssd_conv_w=dw8[:SSD_CONV], ff_conv_w=dff_w8[:FF_CONV])
        dw_in = jnp.concatenate([dw_in_p[:, :d], dw_in_p[:, 3 * d:3 * d + cd], dw_in_p[:, 3 * d + cd:3 * d + cd + heads], dw_in_p[:, d:3 * d]], axis=1)
        by_chip = {}
        for n, g in zip(BIG, (dw_in, dw_out, dw_up, dw_down)):
            width = g.shape[1] // 4
            by_chip[n] = g.reshape(4, g.shape[0] // 4, g.shape[1]) if BIG_AXIS[n] == 0 else jnp.stack([g[:, k * width:(k + 1) * width] for k in range(4)])
        reduced = None
        if reducing is not None:
            big_grads[l + 1], reduced = _reduce_weights_finish(f"wg{l + 1}", reducing, dx2)
        reducing = _reduce_weights_start(f"wg{l}", [by_chip[n] for n in BIG], reduced)
    grad_x = dx2

    g_ada_w, g_ada_b = ada_vjp(_after(jnp.stack(dmods), reducing["token"]))
    small_names_r = SMALL + CONVW + ("final_g",)
    summed = _allreduce_small("small", [jnp.stack([small_grads[l][n] for l in range(nl)]) for n in SMALL + CONVW] + [dfg])
    big_grads[0], _ = _reduce_weights_finish("wg0", reducing, summed[0])
    grad = {"ada_w": g_ada_w, "ada_b": g_ada_b}
    for j, n in enumerate(BIG):
        grad[n] = jnp.stack([big_grads[l][j] for l in range(nl)])
    grad.update(zip(small_names_r, summed))
    chip = 2 * lax.axis_index("x") + lax.axis_index("y")
    for n in CONVW:
        width = weights[n].shape[-1]
        grad[n] = lax.dynamic_slice_in_dim(grad[n], chip * width, width, axis=2)

    delta, new_m, new_v = {}, {}, {}
    for n in ("ada_w",) + BIG:
        delta[n], new_m[n], new_v[n] = _adamw("adam_" + n, weights[n], grad[n], given["m_" + n], given["v_" + n])
    small_names = ("ada_b",) + SMALL + CONVW + ("final_g",)
    as2d = lambda t: t.reshape(1, -1) if t.ndim == 1 else t
    res = _adamw_small(
        "adam_small", [as2d(weights[n]) for n in small_names], [as2d(grad[n]) for n in small_names],
        [as2d(given["m_" + n]) for n in small_names], [as2d(given["v_" + n]) for n in small_names])
    for out, vals in zip((delta, new_m, new_v), res):
        for n, val in zip(small_names, vals):
            out[n] = val.reshape(weights[n].shape)
    grad["final_g"] = grad["final_g"].reshape(final_g.shape)

    return (loss, grad_x, *[grad[n] for n in WEIGHTS], *[delta[n] for n in WEIGHTS], *[new_m[n] for n in WEIGHTS], *[new_v[n] for n in WEIGHTS])
```

```python
import functools

import jax
import jax.numpy as jnp
from jax import lax
from jax.experimental import pallas as pl
from jax.experimental.pallas import tpu as pltpu

F32 = jnp.float32
BF16 = jnp.bfloat16
EPS = 1e-6
CHUNK = 128
HEAD_DIM = 64
STATE = 128
GM_HEAD = 128
SSD_GROUPS = 2
SSD_CONV = 4
FF_CONV = 3
N_MOD = 6
LANES = 128
SUBLANES = 8
V7X_VMEM_LIMIT = 48 * 1024 * 1024
MM_VMEM_BUDGET = 30 * 1024 * 1024
MM_STEP_MACS = 512 * 1024 * 2560
V7X_MXU_FLOPS = 996e12
V7X_HBM_BYTES_PER_S = 3.3e12
V7X_STEP_SECONDS = 0.35e-6
ADAM_LR, ADAM_B1, ADAM_B2, ADAM_EPS, ADAM_WD, ADAM_STEP = 0.001, 0.9, 0.999, 1e-08, 0.01, 10
MESH = pl.DeviceIdType.MESH
AXES = ("x", "y", "c")


def _round_up(n, m):
    return (n + m - 1) // m * m


def _divisors(n, cap, mult=LANES):
    out = [t for t in range(mult, min(n, cap) + 1, mult) if n % t == 0]
    return out or [n]


def _pcall(body, name, out_shape, grid=(), in_specs=None, out_specs=None, scratch=(), sem=None, prefetch=0):
    params = pltpu.CompilerParams(dimension_semantics=sem, vmem_limit_bytes=V7X_VMEM_LIMIT)
    if prefetch:
        spec = pltpu.PrefetchScalarGridSpec(num_scalar_prefetch=prefetch, grid=grid, in_specs=in_specs, out_specs=out_specs, scratch_shapes=list(scratch))
        return pl.pallas_call(body, name=name, out_shape=out_shape, grid_spec=spec, compiler_params=params)
    if in_specs is None:
        return pl.pallas_call(body, name=name, out_shape=out_shape, compiler_params=params)
    return pl.pallas_call(body, name=name, out_shape=out_shape, grid=grid, in_specs=in_specs, out_specs=out_specs, scratch_shapes=list(scratch), compiler_params=params)


def _sds(shape, dtype):
    return jax.ShapeDtypeStruct(tuple(shape), dtype)


def _mm_tiles(m, n, k, a_bytes, b_bytes, o_bytes):
    best, best_key = None, None
    for tm in _divisors(m, 2048):
        for tn in _divisors(n, 2560):
            for tk in _divisors(k, 2560):
                vmem = 2 * (tm * tk * a_bytes + tk * tn * b_bytes + tm * tn * o_bytes) + tm * tn * 4
                if vmem > MM_VMEM_BUDGET or tm * tn * tk > MM_STEP_MACS:
                    continue
                ni, nj, nk = m // tm, n // tn, k // tk
                a_reads = 1 if nk == 1 else nj
                b_reads = 1 if (nk == 1 and nj == 1) else ni
                hbm = a_reads * m * k * a_bytes + b_reads * k * n * b_bytes + m * n * o_bytes
                t = max(2.0 * m * n * k / V7X_MXU_FLOPS, hbm / V7X_HBM_BYTES_PER_S) + ni * nj * nk * V7X_STEP_SECONDS
                key = (-t, tm * tn * tk)
                if best_key is None or key > best_key:
                    best, best_key = (tm, tn, tk), key
    return best


def _matmul(name, a, b, mode, out_dtype):
    if mode == "nn":
        (m, k), n = a.shape, b.shape[1]
    elif mode == "nt":
        (m, k), n = a.shape, b.shape[0]
    else:
        (k, m), n = a.shape, b.shape[1]
    tm, tn, tk = _mm_tiles(m, n, k, a.dtype.itemsize, b.dtype.itemsize, jnp.dtype(out_dtype).itemsize)
    nk = k // tk
    if mode == "nn":
        a_spec = pl.BlockSpec((tm, tk), lambda i, j, kk: (i, kk))
        b_spec = pl.BlockSpec((tk, tn), lambda i, j, kk: (kk, j))
        dims = ((1,), (0,))
    elif mode == "nt":
        a_spec = pl.BlockSpec((tm, tk), lambda i, j, kk: (i, kk))
        b_spec = pl.BlockSpec((tn, tk), lambda i, j, kk: (j, kk))
        dims = ((1,), (1,))
    else:
        a_spec = pl.BlockSpec((tk, tm), lambda i, j, kk: (kk, i))
        b_spec = pl.BlockSpec((tk, tn), lambda i, j, kk: (kk, j))
        dims = ((0,), (0,))

    def body(a_ref, b_ref, o_ref, acc_ref):
        kk = pl.program_id(2)
        p = lax.dot_general(a_ref[...].astype(BF16), b_ref[...].astype(BF16), (dims, ((), ())), preferred_element_type=F32)
        if nk == 1:
            o_ref[...] = p.astype(o_ref.dtype)
        else:
            @pl.when(kk == 0)
            def _():
                acc_ref[...] = p

            @pl.when(kk > 0)
            def _():
                acc_ref[...] += p

            @pl.when(kk == nk - 1)
            def _():
                o_ref[...] = acc_ref[...].astype(o_ref.dtype)

    return _pcall(
        body, name, _sds((m, n), out_dtype), grid=(m // tm, n // tn, nk), in_specs=[a_spec, b_spec],
        out_specs=pl.BlockSpec((tm, tn), lambda i, j, kk: (i, j)), scratch=[pltpu.VMEM((tm, tn), F32)],
        sem=("parallel", "parallel", "arbitrary"),
    )(a, b)


def _mm(name, a, w, out_dtype):
    return _matmul(name, a.reshape(-1, a.shape[-1]), w, "nn", out_dtype).reshape(a.shape[:-1] + (w.shape[1],))


def _mm_bwd(name, a, w, dy):
    a2, dy2 = a.reshape(-1, a.shape[-1]), dy.reshape(-1, dy.shape[-1])
    return _matmul(name + "_dx", dy2, w, "nt", BF16).reshape(a.shape), _matmul(name + "_dw", a2, dy2, "tn", BF16)


def _dg(a, b, dims):
    return lax.dot_general(a.astype(BF16), b.astype(BF16), (dims, ((), ())), preferred_element_type=F32)


@jax.custom_vjp
def _dot_nn(a, b):
    return _dg(a, b, ((1,), (0,)))


_dot_nn.defvjp(lambda a, b: (_dot_nn(a, b), (a, b)), lambda r, d: (_dg(d, r[1], ((1,), (1,))), _dg(r[0], d, ((0,), (0,)))))


@jax.custom_vjp
def _dot_nt(a, b):
    return _dg(a, b, ((1,), (1,)))


_dot_nt.defvjp(lambda a, b: (_dot_nt(a, b), (a, b)), lambda r, d: (_dg(d, r[1], ((1,), (0,))), _dg(d, r[0], ((0,), (0,)))))


@jax.custom_vjp
def _dot_tn(a, b):
    return _dg(a, b, ((0,), (0,)))


_dot_tn.defvjp(lambda a, b: (_dot_tn(a, b), (a, b)), lambda r, d: (_dg(r[1], d, ((1,), (1,))), _dg(r[0], d, ((1,), (0,)))))


def _exact_dot(a, c, dims):
    hi = a.astype(BF16)
    r1 = a - hi.astype(F32)
    mid = r1.astype(BF16)
    lo = (r1 - mid.astype(F32)).astype(BF16)
    cb = c.astype(BF16)
    f = lambda t: lax.dot_general(t, cb, (dims, ((), ())), preferred_element_type=F32)
    return f(hi) + f(mid) + f(lo)


@jax.custom_vjp
def _sel_right(a, c):
    return _exact_dot(a, c, ((1,), (0,)))


_sel_right.defvjp(lambda a, c: (_sel_right(a, c), c), lambda c, d: (_exact_dot(d, c, ((1,), (1,))), jnp.zeros_like(c)))


def _exact_dot_left(c, a, dims):
    hi = a.astype(BF16)
    r1 = a - hi.astype(F32)
    mid = r1.astype(BF16)
    lo = (r1 - mid.astype(F32)).astype(BF16)
    cb = c.astype(BF16)
    f = lambda t: lax.dot_general(cb, t, (dims, ((), ())), preferred_element_type=F32)
    return f(hi) + f(mid) + f(lo)


@jax.custom_vjp
def _sel_left(c, a):
    return _exact_dot_left(c, a, ((1,), (0,)))


_sel_left.defvjp(lambda c, a: (_sel_left(c, a), c), lambda c, d: (jnp.zeros_like(c), _exact_dot_left(c, d, ((0,), (0,)))))


def _sigmoid(x):
    return 1.0 / (1.0 + jnp.exp(-x))


def _rms(x, g):
    return x * lax.rsqrt(jnp.mean(x * x, axis=-1, keepdims=True) + EPS) * g


def _gelu(x):
    return 0.5 * x * (1.0 + lax.erf(x * (2.0 ** -0.5)))


def _causal(n):
    return lax.broadcasted_iota(jnp.int32, (n, n), 0) >= lax.broadcasted_iota(jnp.int32, (n, n), 1)


def _row_in_specs(rows, bparams, gparams, tm):
    specs = [pl.BlockSpec((1, tm, w), lambda b, i, cb=cb: (b, i, cb)) for (_, w, cb, _) in rows]
    specs += [pl.BlockSpec((1, 1, p.shape[-1]), lambda b, i: (b, 0, 0)) for p in bparams]
    specs += [pl.BlockSpec(p.shape, lambda b, i, n=p.ndim: (0,) * n) for p in gparams]
    return specs


def _row_vals(refs, n_rows, n_b, n_g):
    vals = [r[0].astype(F32) for r in refs[:n_rows]]
    vals += [r[0].astype(F32) for r in refs[n_rows:n_rows + n_b]]
    vals += [r[...].astype(F32) for r in refs[n_rows + n_b:n_rows + n_b + n_g]]
    return vals


def _row_fwd(name, tile, rows, bparams, gparams, outs, n_sum, tm):
    bl, s = rows[0][0].shape[:2]
    n_in = len(rows) + len(bparams) + len(gparams)

    def body(*refs):
        first = (pl.program_id(0) == 0) & (pl.program_id(1) == 0)
        res = tile(*_row_vals(refs, len(rows), len(bparams), len(gparams)))
        o_refs = refs[n_in:]
        for k in range(len(outs)):
            o_refs[k][0] = res[k].astype(o_refs[k].dtype)
        for k in range(n_sum):
            o_ref, val = o_refs[len(outs) + k], res[len(outs) + k]

            @pl.when(first)
            def _(o_ref=o_ref, val=val):
                o_ref[...] = val

            @pl.when(jnp.logical_not(first))
            def _(o_ref=o_ref, val=val):
                o_ref[...] += val

    out_shape = [_sds((bl, s, w), dt) for (w, dt) in outs] + [_sds((1, 1), F32)] * n_sum
    out_specs = [pl.BlockSpec((1, tm, w), lambda b, i: (b, i, 0)) for (w, _) in outs] + [pl.BlockSpec((1, 1), lambda b, i: (0, 0))] * n_sum
    return _pcall(
        body, name, out_shape, grid=(bl, s // tm), in_specs=_row_in_specs(rows, bparams, gparams, tm), out_specs=out_specs,
        sem=("arbitrary", "arbitrary"),
    )(*[r[0] for r in rows], *bparams, *gparams)


def _row_bwd(name, tile, rows, bparams, gparams, outs, n_sum, tm, cts):
    bl, s = rows[0][0].shape[:2]
    n_r, n_b, n_g = len(rows), len(bparams), len(gparams)
    n_in = n_r + n_b + n_g
    n_ct = len(outs) + n_sum
    grad_rows = [k for k in range(n_r) if rows[k][3]]

    def body(*refs):
        b, i = pl.program_id(0), pl.program_id(1)
        vals = _row_vals(refs, n_r, n_b, n_g)
        ct_refs = refs[n_in:n_in + n_ct]
        ct = [r[0].astype(F32) for r in ct_refs[:len(outs)]] + [r[...] for r in ct_refs[len(outs):]]
        _, vjp = jax.vjp(tile, *vals)
        grads = vjp(tuple(ct))
        o_refs = refs[n_in + n_ct:]
        for j, k in enumerate(grad_rows):
            o_refs[j][0] = grads[k].astype(o_refs[j].dtype)
        for k in range(n_b):
            o_ref, val = o_refs[len(grad_rows) + k], grads[n_r + k]

            @pl.when(i == 0)
            def _(o_ref=o_ref, val=val):
                o_ref[0] = val

            @pl.when(i > 0)
            def _(o_ref=o_ref, val=val):
                o_ref[0] += val

        first = (b == 0) & (i == 0)
        for k in range(n_g):
            o_ref, val = o_refs[len(grad_rows) + n_b + k], grads[n_r + n_b + k]

            @pl.when(first)
            def _(o_ref=o_ref, val=val):
                o_ref[...] = val

            @pl.when(jnp.logical_not(first))
            def _(o_ref=o_ref, val=val):
                o_ref[...] += val

    in_specs = _row_in_specs(rows, bparams, gparams, tm)
    in_specs += [pl.BlockSpec((1, tm, w), lambda b, i: (b, i, 0)) for (w, _) in outs] + [pl.BlockSpec((1, 1), lambda b, i: (0, 0))] * n_sum
    out_shape = [_sds((bl, s, rows[k][1]), rows[k][3]) for k in grad_rows]
    out_shape += [_sds(p.shape, F32) for p in bparams] + [_sds(p.shape, F32) for p in gparams]
    out_specs = [pl.BlockSpec((1, tm, rows[k][1]), lambda b, i: (b, i, 0)) for k in grad_rows]
    out_specs += [pl.BlockSpec((1, 1, p.shape[-1]), lambda b, i: (b, 0, 0)) for p in bparams]
    out_specs += [pl.BlockSpec(p.shape, lambda b, i, n=p.ndim: (0,) * n) for p in gparams]
    res = _pcall(
        body, name, out_shape, grid=(bl, s // tm), in_specs=in_specs, out_specs=out_specs, sem=("arbitrary", "arbitrary"),
    )(*[r[0] for r in rows], *bparams, *gparams, *cts)
    return res[:len(grad_rows)], res[len(grad_rows):len(grad_rows) + n_b], res[len(grad_rows) + n_b:]


def _nm_tile(x, sc, sh, g):
    return (_rms(x, g) * (1.0 + sc) + sh,)


def _nm_res_tile(x, sc, sh, g):
    return x, _rms(x, g) * (1.0 + sc) + sh


def _rnm_tile(x, o, gate, sc, sh, g):
    xn = x + gate * o
    return xn, _rms(xn, g) * (1.0 + sc) + sh


def _final_tile(x, o, tgt, gate, g):
    e = _rms(x + gate * o, g) - tgt
    return (0.5 * jnp.sum(jnp.mean(e * e, axis=-1, keepdims=True), axis=0, keepdims=True),)


def _gm_tile(u_in, v_in, vg, ws, bs_t, og):
    d = u_in.shape[1]
    u, vn = _gelu(u_in), _rms(_gelu(v_in), vg)
    causal = _causal(CHUNK)
    lane = lax.broadcasted_iota(jnp.int32, (1, LANES), 1)
    parts = []
    for h in range(d // GM_HEAD):
        bias = jnp.sum(bs_t * (lane == h).astype(F32), axis=1, keepdims=True)
        parts.append(_dot_nn(jnp.where(causal, ws[h], 0.0), vn[:, h * GM_HEAD:(h + 1) * GM_HEAD]) + bias)
    return (_rms(u * jnp.concatenate(parts, axis=1), og),)


def _shift_down(x, j):
    if j == 0:
        return x
    row = lax.broadcasted_iota(jnp.int32, x.shape, 0)
    return jnp.where(row >= j, pltpu.roll(x, j, 0), 0.0)


def _shift_up(x, j):
    if j == 0:
        return x
    s = x.shape[0]
    row = lax.broadcasted_iota(jnp.int32, x.shape, 0)
    return jnp.where(row < s - j, pltpu.roll(x, s - j, 0), 0.0)


def _conv_pre(x, w_ref, b_ref, taps):
    acc = x * w_ref[taps - 1:taps, :] + b_ref[...]
    for j in range(1, taps):
        acc = acc + _shift_down(x, j) * w_ref[taps - 1 - j:taps - j, :]
    return acc


def _conv_tc(*widths):
    for tc in (512, 256, 128):
        if all(w % tc == 0 for w in widths):
            return tc
    raise ValueError("conv widths must be multiples of the lane count")


def _conv_act_fwd(name, src, col0, chans, w8, b, taps, val_col0, out_dtype):
    bl, s, _ = src.shape
    gated = val_col0 is not None
    tc = _conv_tc(chans, col0, val_col0 or 0)

    def body(*refs):
        if gated:
            x_ref, v_ref, w_ref, b_ref, o_ref = refs
        else:
            x_ref, w_ref, b_ref, o_ref = refs
        pre = _conv_pre(x_ref[0], w_ref, b_ref, taps)
        y = pre * _sigmoid(pre)
        if gated:
            y = y * v_ref[0]
        o_ref[0] = y.astype(o_ref.dtype)

    in_specs = [pl.BlockSpec((1, s, tc), lambda j, bb: (bb, 0, col0 // tc + j))]
    args = [src]
    if gated:
        in_specs.append(pl.BlockSpec((1, s, tc), lambda j, bb: (bb, 0, val_col0 // tc + j)))
        args.append(src)
    in_specs += [pl.BlockSpec((SUBLANES, tc), lambda j, bb: (0, j)), pl.BlockSpec((1, tc), lambda j, bb: (0, j))]
    return _pcall(
        body, name, _sds((bl, s, chans), out_dtype), grid=(chans // tc, bl), in_specs=in_specs,
        out_specs=pl.BlockSpec((1, s, tc), lambda j, bb: (bb, 0, j)), sem=("parallel", "arbitrary"),
    )(*args, w8, b)


def _conv_act_bwd(name, src, col0, chans, w8, b, taps, val_col0, dy, dx_dtype):
    bl, s, _ = src.shape
    gated = val_col0 is not None
    tc = _conv_tc(chans, col0, val_col0 or 0)
    nblk = chans // tc

    def body(*refs):
        if gated:
            x_ref, v_ref, w_ref, b_ref, dy_ref, dx_ref, dw_ref, db_ref = refs
        else:
            x_ref, w_ref, b_ref, dy_ref, dx_ref, dw_ref, db_ref = refs
        bb = pl.program_id(1)
        x = x_ref[0]
        pre = _conv_pre(x, w_ref, b_ref, taps)
        sig = _sigmoid(pre)
        d = dy_ref[0].astype(F32)

        def gate_part():
            dsil = d * v_ref[0] if gated else d
            dpre = dsil * sig * (1.0 + pre * (1.0 - sig))
            dx = dpre * w_ref[taps - 1:taps, :]
            for j in range(1, taps):
                dx = dx + _shift_up(dpre, j) * w_ref[taps - 1 - j:taps - j, :]
            dx_ref[0] = dx.astype(dx_ref.dtype)
            rows = [jnp.sum(dpre * _shift_down(x, taps - 1 - k), axis=0, keepdims=True) for k in range(taps)]
            rows += [jnp.zeros_like(rows[0])] * (SUBLANES - taps)
            dw = jnp.concatenate(rows, axis=0)
            db = jnp.sum(dpre, axis=0, keepdims=True)

            @pl.when(bb == 0)
            def _():
                dw_ref[...] = dw
                db_ref[...] = db

            @pl.when(bb > 0)
            def _():
                dw_ref[...] += dw
                db_ref[...] += db

        if gated:
            part = pl.program_id(2)
            pl.when(part == 0)(gate_part)

            @pl.when(part == 1)
            def _():
                dx_ref[0] = (d * (pre * sig)).astype(dx_ref.dtype)
        else:
            gate_part()

    if gated:
        grid, sem = (nblk, bl, 2), ("parallel", "arbitrary", "arbitrary")
        im = lambda f: (lambda j, bb, p: f(j, bb, p))
    else:
        grid, sem = (nblk, bl), ("parallel", "arbitrary")
        im = lambda f: (lambda j, bb: f(j, bb, 0))
    in_specs = [pl.BlockSpec((1, s, tc), im(lambda j, bb, p: (bb, 0, col0 // tc + j)))]
    args = [src]
    if gated:
        in_specs.append(pl.BlockSpec((1, s, tc), im(lambda j, bb, p: (bb, 0, val_col0 // tc + j))))
        args.append(src)
    in_specs += [
        pl.BlockSpec((SUBLANES, tc), im(lambda j, bb, p: (0, j))), pl.BlockSpec((1, tc), im(lambda j, bb, p: (0, j))),
        pl.BlockSpec((1, s, tc), im(lambda j, bb, p: (bb, 0, j))),
    ]
    out_shape = [_sds((bl, s, chans * (2 if gated else 1)), dx_dtype), _sds((SUBLANES, chans), F32), _sds((1, chans), F32)]
    out_specs = [
        pl.BlockSpec((1, s, tc), im(lambda j, bb, p: (bb, 0, j + p * nblk))), pl.BlockSpec((SUBLANES, tc), im(lambda j, bb, p: (0, j))),
        pl.BlockSpec((1, tc), im(lambda j, bb, p: (0, j))),
    ]
    return _pcall(body, name, out_shape, grid=grid, in_specs=in_specs, out_specs=out_specs, sem=sem)(*args, w8, b, dy)


def _ssd_chunk(g_idx, states, xs, bm, cm, dtr, z, dtb, alog, dsk, ng):
    gw = xs.shape[1]
    hpg = gw // HEAD_DIM
    dt = jax.nn.softplus(dtr + dtb)
    da = dt * (-jnp.exp(alog))
    causal = _causal(CHUNK)
    acs = _sel_left(causal.astype(F32), da)
    head_of_col = g_idx * hpg + lax.broadcasted_iota(jnp.int32, (LANES, gw), 1) // HEAD_DIM
    expand = (lax.broadcasted_iota(jnp.int32, (LANES, gw), 0) == head_of_col).astype(F32)
    dt_e, acs_e = _sel_right(dt, expand), _sel_right(acs, expand)
    last = lax.broadcasted_iota(jnp.int32, (CHUNK, gw), 0) == CHUNK - 1
    alast_e = jnp.sum(jnp.where(last, acs_e, 0.0), axis=0, keepdims=True)
    xc = xs * dt_e
    xc_st = xc * jnp.exp(alast_e - acs_e)
    decay_out, chunk_decay = jnp.exp(acs_e), jnp.exp(alast_e)
    cb = _dot_nt(cm, bm)
    acs_t = acs.T
    lane = lax.broadcasted_iota(jnp.int32, (1, LANES), 1)
    sub = lax.broadcasted_iota(jnp.int32, (LANES, 1), 0)
    ys, new_states = [], []
    for p in range(gw // LANES):
        sl = slice(p * LANES, (p + 1) * LANES)
        xcp = xc[:, sl]
        y = _dot_nn(cm, states[p]) * decay_out[:, sl]
        for q in range(2):
            head = g_idx * hpg + 2 * p + q
            col = jnp.sum(acs * (lane == head).astype(F32), axis=1, keepdims=True)
            row = jnp.sum(acs_t * (sub == head).astype(F32), axis=0, keepdims=True)
            decay = jnp.where(causal, jnp.exp(jnp.where(causal, col - row, 0.0)), 0.0)
            half = ((lane // HEAD_DIM) == q).astype(F32)
            y = y + _dot_nn(cb * decay, xcp * half)
        ys.append(y)
        new_states.append(states[p] * chunk_decay[:, sl] + _dot_tn(bm, xc_st[:, sl]))
    y = jnp.concatenate(ys, axis=1) + dsk * xs
    gated = y * (z * _sigmoid(z))
    return tuple(new_states), _rms(gated, ng)


def _ssd_specs(d, gw, dt_col, rev, nc):
    ci = (lambda i: nc - 1 - i) if rev else (lambda i: i)
    return [
        pl.BlockSpec((1, CHUNK, gw), lambda g, b, i: (b, ci(i), g)),
        pl.BlockSpec((1, CHUNK, STATE), lambda g, b, i: (b, ci(i), d // STATE + g)),
        pl.BlockSpec((1, CHUNK, STATE), lambda g, b, i: (b, ci(i), d // STATE + SSD_GROUPS + g)),
        pl.BlockSpec((1, CHUNK, LANES), lambda g, b, i: (b, ci(i), dt_col // LANES)),
        pl.BlockSpec((1, CHUNK, gw), lambda g, b, i: (b, ci(i), g)),
        pl.BlockSpec((1, LANES), lambda g, b, i: (0, 0)), pl.BlockSpec((1, LANES), lambda g, b, i: (0, 0)),
        pl.BlockSpec((1, gw), lambda g, b, i: (0, g)), pl.BlockSpec((1, gw), lambda g, b, i: (0, g)),
    ]


def _ssd_fwd(name, xbc, proj, dt_col, dtb, alog, dsk, ng):
    bl, s, cd = xbc.shape
    d = dsk.shape[1]
    gw, nc = d // SSD_GROUPS, s // CHUNK
    npair = gw // LANES

    def body(xs_ref, bm_ref, cm_ref, dt_ref, z_ref, dtb_ref, alog_ref, dsk_ref, ng_ref, y_ref, hp_ref, st_ref):
        g, i = pl.program_id(0), pl.program_id(2)

        @pl.when(i == 0)
        def _():
            st_ref[...] = jnp.zeros_like(st_ref)

        states = tuple(st_ref[p] for p in range(npair))
        hp_ref[0, 0, 0] = st_ref[...]
        new_states, yn = _ssd_chunk(g, states, xs_ref[0], bm_ref[0], cm_ref[0], dt_ref[0], z_ref[0], dtb_ref[...], alog_ref[...], dsk_ref[...], ng_ref[...])
        for p in range(npair):
            st_ref[p] = new_states[p]
        y_ref[0] = yn.astype(y_ref.dtype)

    out_shape = [_sds((bl, s, d), BF16), _sds((SSD_GROUPS, bl, nc, npair, STATE, LANES), F32)]
    out_specs = [
        pl.BlockSpec((1, CHUNK, gw), lambda g, b, i: (b, i, g)),
        pl.BlockSpec((1, 1, 1, npair, STATE, LANES), lambda g, b, i: (g, b, i, 0, 0, 0)),
    ]
    return _pcall(
        body, name, out_shape, grid=(SSD_GROUPS, bl, nc), in_specs=_ssd_specs(d, gw, dt_col, False, nc), out_specs=out_specs,
        scratch=[pltpu.VMEM((npair, STATE, LANES), F32)], sem=("arbitrary", "arbitrary", "arbitrary"),
    )(xbc, xbc, xbc, proj, proj, dtb, alog, dsk, ng)


def _ssd_bwd(name, xbc, proj, dt_col, dtb, alog, dsk, ng, hprev, dy):
    bl, s, cd = xbc.shape
    d = dsk.shape[1]
    gw, nc = d // SSD_GROUPS, s // CHUNK
    npair = gw // LANES

    def body(xs_ref, bm_ref, cm_ref, dt_ref, z_ref, dtb_ref, alog_ref, dsk_ref, ng_ref, hp_ref, dy_ref,
             dxs_ref, dbm_ref, dcm_ref, ddt_ref, dz_ref, ddtb_ref, dalog_ref, ddsk_ref, dng_ref, dst_ref):
        g, b, i = pl.program_id(0), pl.program_id(1), pl.program_id(2)

        @pl.when(i == 0)
        def _():
            dst_ref[...] = jnp.zeros_like(dst_ref)

        states = tuple(hp_ref[0, 0, 0, p] for p in range(npair))
        step = functools.partial(_ssd_chunk, g)
        _, vjp = jax.vjp(step, states, xs_ref[0], bm_ref[0], cm_ref[0], dt_ref[0], z_ref[0], dtb_ref[...], alog_ref[...], dsk_ref[...], ng_ref[...])
        d_states, dxs, dbm, dcm, ddt, dz, ddtb, dalog, ddsk, dng = vjp((tuple(dst_ref[p] for p in range(npair)), dy_ref[0].astype(F32)))
        for p in range(npair):
            dst_ref[p] = d_states[p]
        dxs_ref[0], dbm_ref[0], dcm_ref[0] = dxs, dbm, dcm
        dz_ref[0] = dz.astype(dz_ref.dtype)
        ddt_ref[0, 0] = ddt
        first_g = (b == 0) & (i == 0)
        first = first_g & (g == 0)
        for o_ref, val, init in ((ddtb_ref, ddtb, first), (dalog_ref, dalog, first), (ddsk_ref, ddsk, first_g), (dng_ref, dng, first_g)):
            @pl.when(init)
            def _(o_ref=o_ref, val=val):
                o_ref[...] = val

            @pl.when(jnp.logical_not(init))
            def _(o_ref=o_ref, val=val):
                o_ref[...] += val

    rc = lambda i: nc - 1 - i
    in_specs = _ssd_specs(d, gw, dt_col, True, nc) + [
        pl.BlockSpec((1, 1, 1, npair, STATE, LANES), lambda g, b, i: (g, b, rc(i), 0, 0, 0)),
        pl.BlockSpec((1, CHUNK, gw), lambda g, b, i: (b, rc(i), g)),
    ]
    out_shape = [
        _sds((bl, s, d), F32), _sds((bl, s, SSD_GROUPS * STATE), F32), _sds((bl, s, SSD_GROUPS * STATE), F32),
        _sds((SSD_GROUPS, bl, s, LANES), F32), _sds((bl, s, d), BF16),
        _sds((1, LANES), F32), _sds((1, LANES), F32), _sds((1, d), F32), _sds((1, d), F32),
    ]
    out_specs = [
        pl.BlockSpec((1, CHUNK, gw), lambda g, b, i: (b, rc(i), g)),
        pl.BlockSpec((1, CHUNK, STATE), lambda g, b, i: (b, rc(i), g)), pl.BlockSpec((1, CHUNK, STATE), lambda g, b, i: (b, rc(i), g)),
        pl.BlockSpec((1, 1, CHUNK, LANES), lambda g, b, i: (g, b, rc(i), 0)),
        pl.BlockSpec((1, CHUNK, gw), lambda g, b, i: (b, rc(i), g)),
        pl.BlockSpec((1, LANES), lambda g, b, i: (0, 0)), pl.BlockSpec((1, LANES), lambda g, b, i: (0, 0)),
        pl.BlockSpec((1, gw), lambda g, b, i: (0, g)), pl.BlockSpec((1, gw), lambda g, b, i: (0, g)),
    ]
    return _pcall(
        body, name, out_shape, grid=(SSD_GROUPS, bl, nc), in_specs=in_specs, out_specs=out_specs,
        scratch=[pltpu.VMEM((npair, STATE, LANES), F32)], sem=("arbitrary", "arbitrary", "arbitrary"),
    )(xbc, xbc, xbc, proj, proj, dtb, alog, dsk, ng, hprev, dy)


def _mixer_fwd(name, proj, d, cd, conv_w8, conv_b, dtb, alog, dsk, ng, vg, ws, bs_t, og):
    xbc = _conv_act_fwd(name + "_conv_fwd", proj, 3 * d, cd, conv_w8, conv_b, SSD_CONV, None, F32)
    y, hprev = _ssd_fwd(name + "_ssd_fwd", xbc, proj, 3 * d + cd, dtb, alog, dsk, ng)
    rows = [(proj, d, 1, BF16), (proj, d, 2, BF16)]
    (g_out,) = _row_fwd(name + "_gm_fwd", _gm_tile, rows, [], [vg, ws, bs_t, og], [(d, BF16)], 0, CHUNK)
    return jnp.concatenate([y, g_out], axis=-1), (xbc, hprev)


def _mixer_bwd(name, proj, d, cd, conv_w8, conv_b, dtb, alog, dsk, ng, vg, ws, bs_t, og, xbc, hprev, dycat):
    bl, s, n_proj = proj.shape
    dt_col = 3 * d + cd
    dy, dg_out = dycat[..., :d], dycat[..., d:]
    dxs, dbm, dcm, ddt2, dz, ddtb, dalog, ddsk, dng = _ssd_bwd(name + "_ssd_bwd", xbc, proj, dt_col, dtb, alog, dsk, ng, hprev, dy)
    dxbc_act = jnp.concatenate([dxs, dbm, dcm], axis=-1)
    dxbc, dw8, dcb = _conv_act_bwd(name + "_conv_bwd", proj, 3 * d, cd, conv_w8, conv_b, SSD_CONV, None, dxbc_act, BF16)
    rows = [(proj, d, 1, BF16), (proj, d, 2, BF16)]
    (du, dv), _, (dvg, dws, dbs_t, dog) = _row_bwd(name + "_gm_bwd", _gm_tile, rows, [], [vg, ws, bs_t, og], [(d, BF16)], 0, CHUNK, [dg_out])
    ddt = (ddt2[0] + ddt2[1]).astype(BF16)
    pad = jnp.zeros((bl, s, n_proj - dt_col - LANES), BF16)
    dproj = jnp.concatenate([dz, du, dv, dxbc, ddt, pad], axis=-1)
    return dproj, (dw8, dcb, ddtb, dalog, ddsk, dng, dvg, dws, dbs_t, dog)


def _position():
    return lax.axis_index("x"), lax.axis_index("y"), lax.axis_index("c")


def _at(ref, idx):
    return ref.at[idx] if len(idx) else ref


def _exchange(name, inputs, out_shapes, plan):
    n_in, n_out = len(inputs), len(out_shapes)
    n_copy = len(plan(0, 0, 0))

    def body(*refs):
        in_refs, out_refs, token = refs[:n_in], refs[n_in:n_in + n_out], refs[n_in + n_out]
        send_sems, recv_sems = refs[n_in + n_out + 1:]
        token[...] = jnp.zeros_like(token)
        x, y, c = _position()
        copies = plan(x, y, c)

        def copy(k, src, dst, peer):
            return pltpu.make_async_remote_copy(src_ref=src, dst_ref=dst, send_sem=send_sems.at[k], recv_sem=recv_sems.at[k], device_id=peer, device_id_type=MESH)

        sends = [copy(k, _at(in_refs[sa], si), _at(out_refs[da], di), peer) for k, (sa, si, da, di, peer, _) in enumerate(copies)]
        for cp in sends:
            cp.start()
        for k, (sa, si, da, _, peer, li) in enumerate(copies):
            copy(k, _at(in_refs[sa], si), _at(out_refs[da], li), peer).wait_recv()
        for cp in sends:
            cp.wait_send()

    any_spec = pl.BlockSpec(memory_space=pl.ANY)
    outs = pl.pallas_call(
        body, name=name, out_shape=[_sds(s, dt) for s, dt in out_shapes] + [_sds((SUBLANES, LANES), F32)], in_specs=[any_spec] * n_in,
        out_specs=[any_spec] * n_out + [pl.BlockSpec(memory_space=pltpu.VMEM)],
        scratch_shapes=[pltpu.SemaphoreType.DMA((n_copy,)), pltpu.SemaphoreType.DMA((n_copy,))],
    )(*inputs)
    return list(outs[:n_out]), outs[n_out]


def _exchange_start(name, inputs, out_shapes, plan):
    n_in, n_out = len(inputs), len(out_shapes)
    n_copy = len(plan(0, 0, 0))

    def body(*refs):
        in_refs, land_refs = refs[:n_in], refs[n_in:n_in + n_out]
        send_sems, recv_sems = refs[n_in + n_out:n_in + n_out + 2]
        token = refs[-1]
        x, y, c = _position()
        for k, (sa, si, da, di, peer, _) in enumerate(plan(x, y, c)):
            pltpu.make_async_remote_copy(
                src_ref=_at(in_refs[sa], si), dst_ref=_at(land_refs[da], di), send_sem=send_sems.at[k], recv_sem=recv_sems.at[k],
                device_id=peer, device_id_type=MESH).start()
        token[...] = jnp.zeros_like(token)

    hbm, sem = pl.BlockSpec(memory_space=pltpu.HBM), pl.BlockSpec(memory_space=pltpu.SEMAPHORE)
    lands = [lax.empty(s, dt) for s, dt in out_shapes]
    args = [pltpu.with_memory_space_constraint(a, pltpu.HBM) for a in list(inputs) + lands]
    outs = pl.pallas_call(
        body, name=name,
        out_shape=(pltpu.SemaphoreType.DMA((n_copy,)), pltpu.SemaphoreType.DMA((n_copy,)), *[pltpu.HBM(a.shape, a.dtype) for a in args], _sds((SUBLANES, LANES), F32)),
        in_specs=[hbm] * (n_in + n_out), out_specs=(sem, sem, *[hbm] * (n_in + n_out), pl.BlockSpec(memory_space=pltpu.VMEM)),
        input_output_aliases={i: 2 + i for i in range(n_in + n_out)},
        compiler_params=pltpu.CompilerParams(has_side_effects=pltpu.SideEffectType.DATAFLOW_SIDE_EFFECTING),
    )(*args)
    return dict(name=name, plan=plan, sems=outs[:2], ins=list(outs[2:2 + n_in]), lands=list(outs[2 + n_in:2 + n_in + n_out]), token=outs[-1])


def _exchange_wait(started, after):
    plan, n_in, n_out = started["plan"], len(started["ins"]), len(started["lands"])

    def body(*refs):
        in_refs, land_refs = refs[:n_in], refs[n_in:n_in + n_out]
        send_sems, recv_sems = refs[n_in + n_out:n_in + n_out + 2]
        token = refs[-1]
        x, y, c = _position()
        for k, (sa, si, da, _, peer, li) in enumerate(plan(x, y, c)):
            cp = pltpu.make_async_remote_copy(
                src_ref=_at(in_refs[sa], si), dst_ref=_at(land_refs[da], li), send_sem=send_sems.at[k], recv_sem=recv_sems.at[k],
                device_id=peer, device_id_type=MESH)
            cp.wait_send()
            cp.wait_recv()
        token[...] = jnp.zeros_like(token)

    hbm, sem = pl.BlockSpec(memory_space=pltpu.HBM), pl.BlockSpec(memory_space=pltpu.SEMAPHORE)
    bufs = started["ins"] + started["lands"]
    outs = pl.pallas_call(
        body, name=started["name"] + "_wait", out_shape=(*[pltpu.HBM(a.shape, a.dtype) for a in bufs], _sds((SUBLANES, LANES), F32)),
        in_specs=[hbm] * len(bufs) + [sem, sem, pl.BlockSpec(memory_space=pl.ANY)], out_specs=(*[hbm] * len(bufs), pl.BlockSpec(memory_space=pltpu.VMEM)),
        input_output_aliases={i: i for i in range(len(bufs))},
        compiler_params=pltpu.CompilerParams(has_side_effects=pltpu.SideEffectType.DATAFLOW_SIDE_EFFECTING),
    )(*bufs, *started["sems"], after)
    return list(outs[:n_in]), list(outs[n_in:n_in + n_out]), outs[-1]


def _after(value, token):
    return value + token[0, 0].astype(value.dtype)


def _chip_peers(x, y, c):
    return [(1 - x, y, c), (x, 1 - y, c), (1 - x, 1 - y, c)]


def _chip_of(p):
    return 2 * p[0] + p[1]


def _set_slot(slots, me, blk):
    return lax.dynamic_update_slice(slots, blk[None], (me,) + (0,) * blk.ndim)


def _by_core(c, mine, other, axis):
    return jnp.where(c == 0, jnp.stack([mine, other], axis), jnp.stack([other, mine], axis))


def _plan_gather_chips(n):
    def plan(x, y, c):
        return [(a, (), a, (2 * x + y,), p, (_chip_of(p),)) for a in range(n) for p in _chip_peers(x, y, c)]

    return plan


def _gather_chips(name, blocks):
    recv, token = _exchange(name, blocks, [((4,) + b.shape, b.dtype) for b in blocks], _plan_gather_chips(len(blocks)))
    x, y, _ = _position()
    return [_set_slot(r, 2 * x + y, b) for r, b in zip(recv, blocks)], token


def _gather_pass_cores(name, blocks, from_chips):
    n = len(blocks)

    def plan_cores(x, y, c):
        me, sib = 2 * x + y, (x, y, 1 - c)
        own = [(a, (), a, (me,), sib, (me,)) for a in range(n)]
        passed = [(n + a, (_chip_of(p),), a, (_chip_of(p),), sib, (_chip_of(p),)) for a in range(n) for p in _chip_peers(x, y, c)]
        return own + passed

    from_core, token = _exchange(name + "_cores", list(blocks) + list(from_chips), [((4,) + b.shape, b.dtype) for b in blocks], plan_cores)
    x, y, c = _position()
    return [_by_core(c, _set_slot(r1, 2 * x + y, b), r2, 1) for b, r1, r2 in zip(blocks, from_chips, from_core)], token


def _gather_two_level(name, blocks):
    from_chips, _ = _exchange(name + "_chips", blocks, [((4,) + b.shape, b.dtype) for b in blocks], _plan_gather_chips(len(blocks)))
    return _gather_pass_cores(name, blocks, from_chips)


def _pair_add(name, g42, r4):
    _, _, rh, cols = g42.shape
    tr = _divisors(rh, 512, SUBLANES * 2)[-1]

    def body(c_ref, a_ref, b_ref, o_ref):
        o_ref[0] = (a_ref[0, 0].astype(F32) + b_ref[0].astype(F32)).astype(o_ref.dtype)

    cidx = lax.axis_index("c").astype(jnp.int32).reshape(1)
    return _pcall(
        body, name, _sds(r4.shape, BF16), grid=(4, rh // tr),
        in_specs=[pl.BlockSpec((1, 1, tr, cols), lambda s, i, c_ref: (s, c_ref[0], i, 0)), pl.BlockSpec((1, tr, cols), lambda s, i, c_ref: (s, i, 0))],
        out_specs=pl.BlockSpec((1, tr, cols), lambda s, i, c_ref: (s, i, 0)), sem=("parallel", "parallel"), prefetch=1,
    )(cidx, g42, r4)


def _slot_sum(name, parts):
    n, r, cols = parts.shape
    cap = max(2 * SUBLANES, (4 * 1024 * 1024) // (n * cols * parts.dtype.itemsize))
    tr = _divisors(r, cap, 2 * SUBLANES)[-1]

    def body(p_ref, o_ref):
        acc = p_ref[0].astype(F32)
        for k in range(1, n):
            acc = acc + p_ref[k].astype(F32)
        o_ref[...] = acc

    return _pcall(
        body, name, _sds((r, cols), F32), grid=(r // tr,), in_specs=[pl.BlockSpec((n, tr, cols), lambda i: (0, i, 0))],
        out_specs=pl.BlockSpec((tr, cols), lambda i: (i, 0)), sem=("parallel",),
    )(parts)


def _slot_sums(name, parts):
    k = len(parts)

    def body(*refs):
        for p_ref, o_ref in zip(refs[:k], refs[k:]):
            acc = p_ref[0]
            for j in range(1, p_ref.shape[0]):
                acc = acc + p_ref[j]
            o_ref[...] = acc

    return list(_pcall(body, name, [_sds(p.shape[1:], F32) for p in parts])(*parts))


def _gather_weights_start(name, shards, token):
    c = lax.axis_index("c")
    halves = [lax.dynamic_slice_in_dim(w, c * (w.shape[0] // 2), w.shape[0] // 2, 0).astype(BF16) for w in shards]
    if token is not None:
        halves[0] = _after(halves[0], token)
    return _exchange_start(name + "_ag_chips", halves, [((4,) + h.shape, h.dtype) for h in halves], _plan_gather_chips(len(halves)))


def _gather_weights_finish(name, started, after):
    halves, from_chips, token = _exchange_wait(started, after)
    both, _ = _gather_pass_cores(name + "_ag", halves, from_chips)
    return [f.reshape((4, 2 * h.shape[0], h.shape[1])) for f, h in zip(both, halves)], token


def _reduce_weights_start(name, grads, token):
    n = len(grads)
    g42 = [g.reshape(4, 2, g.shape[1] // 2, g.shape[2]) for g in grads]
    if token is not None:
        g42[0] = _after(g42[0], token)

    def plan_swap(x, y, c):
        return [(a, (s, 1 - c), a, (s,), (x, y, 1 - c), (s,)) for a in range(n) for s in range(4)]

    def plan_chips(x, y, c):
        return [(a, (_chip_of(p),), a, (2 * x + y,), p, (_chip_of(p),)) for a in range(n) for p in _chip_peers(x, y, c)]

    other, _ = _exchange(name + "_rs_cores", g42, [((4,) + g.shape[2:], g.dtype) for g in g42], plan_swap)
    pair = [_pair_add(f"{name}_rs_pair{a}", g, o) for a, (g, o) in enumerate(zip(g42, other))]
    return _exchange_start(name + "_rs_chips", pair, [(p.shape, p.dtype) for p in pair], plan_chips)


def _reduce_weights_finish(name, started, after):
    pair, recv, token = _exchange_wait(started, after)
    n = len(pair)
    x, y, c = _position()
    me = 2 * x + y

    def plan_share(x, y, c):
        return [(a, (), a, (), (x, y, 1 - c), ()) for a in range(n)]

    parts = [lax.dynamic_update_slice(r, lax.dynamic_slice_in_dim(p, me, 1, 0), (me, 0, 0)) for r, p in zip(recv, pair)]
    mine = [_slot_sum(f"{name}_rs_sum{a}", p) for a, p in enumerate(parts)]
    theirs, _ = _exchange(name + "_rs_share", mine, [(m.shape, m.dtype) for m in mine], plan_share)
    return [_by_core(c, m, t, 0).reshape(2 * m.shape[0], m.shape[1]) for m, t in zip(mine, theirs)], token


def _allreduce_small(name, grads):
    both, _ = _gather_two_level(name + "_ag", list(grads))
    return _slot_sums(name + "_sum", [g.reshape((8,) + g.shape[2:]) for g in both])


def _ada_fwd_call(name, c_all, w, b_shard):
    nl, d, ns = w.shape
    nb = c_all.shape[0]

    def body(c_ref, w_ref, b_ref, o_ref):
        cv = c_ref[...]
        o_ref[0] = _dg(cv * _sigmoid(cv), w_ref[0], ((1,), (0,))) + b_ref[0]

    return _pcall(
        body, name, _sds((nl, nb, ns), F32), grid=(nl,),
        in_specs=[pl.BlockSpec((nb, d), lambda l: (0, 0)), pl.BlockSpec((1, d, ns), lambda l: (l, 0, 0)), pl.BlockSpec((1, 1, ns), lambda l: (l, 0, 0))],
        out_specs=pl.BlockSpec((1, nb, ns), lambda l: (l, 0, 0)), sem=("parallel",),
    )(c_all, w, b_shard)


def _ada_bwd_call(name, c_all, dm_shard, dm_all):
    nl, nb, ns = dm_shard.shape
    d = c_all.shape[1]
    nm = dm_all.shape[2]

    def body(c_ref, ds_ref, da_ref, dw_ref, db_ref):
        cv = c_ref[...]
        dw_ref[0] = _dg(cv * _sigmoid(cv), ds_ref[0], ((0,), (0,)))
        db_ref[0] = jnp.sum(da_ref[0], axis=0, keepdims=True)

    return _pcall(
        body, name, [_sds((nl, d, ns), F32), _sds((nl, 1, nm), F32)], grid=(nl,),
        in_specs=[pl.BlockSpec((nb, d), lambda l: (0, 0)), pl.BlockSpec((1, nb, ns), lambda l: (l, 0, 0)), pl.BlockSpec((1, nb, nm), lambda l: (l, 0, 0))],
        out_specs=[pl.BlockSpec((1, d, ns), lambda l: (l, 0, 0)), pl.BlockSpec((1, 1, nm), lambda l: (l, 0, 0))], sem=("parallel",),
    )(c_all, dm_shard, dm_all)


def _ada_fwd(name, bl, c_all, w, b):
    nl, d, ns = w.shape
    chip = 2 * lax.axis_index("x") + lax.axis_index("y")
    b_shard = lax.dynamic_slice(b, (0, chip * ns), (nl, ns)).reshape(nl, 1, ns)
    shard = _ada_fwd_call(name + "_fwd", c_all, w, b_shard)
    (allc,), token = _gather_chips(name + "_ag", [shard])
    mods = jnp.transpose(allc, (1, 2, 0, 3)).reshape(nl, c_all.shape[0], 4 * ns)
    return lax.dynamic_slice(mods, (0, (2 * chip + lax.axis_index("c")) * bl, 0), (nl, bl, 4 * ns)), token


def _ada_bwd(name, bl, c_all, ns, dm):
    nl = dm.shape[0]
    chip = 2 * lax.axis_index("x") + lax.axis_index("y")
    (dm_all,), _ = _gather_two_level(name + "_bwd_ag", [dm])
    dm_all = jnp.transpose(dm_all.reshape((8,) + dm.shape), (1, 0, 2, 3)).reshape(nl, 8 * bl, 4 * ns)
    dm_shard = lax.dynamic_slice(dm_all, (0, 0, chip * ns), (nl, 8 * bl, ns))
    dw, db = _ada_bwd_call(name + "_bwd", c_all, dm_shard, dm_all)
    return dw, db.reshape(nl, 4 * ns)


def _adamw(name, w, g, m, v):
    shape = w.shape
    cols = shape[-1]
    w2, g2, m2, v2 = (t.reshape(-1, cols) for t in (w, g, m, v))
    rows = w2.shape[0]
    cap = max(SUBLANES, (512 * 1024) // max(cols, 1) // SUBLANES * SUBLANES)
    tr = _divisors(rows, cap, SUBLANES)[-1]

    def body(w_ref, g_ref, m_ref, v_ref, d_ref, mo_ref, vo_ref):
        gv = g_ref[...]
        mn = ADAM_B1 * m_ref[...] + (1.0 - ADAM_B1) * gv
        vn = ADAM_B2 * v_ref[...] + (1.0 - ADAM_B2) * (gv * gv)
        m_hat = mn / (1.0 - ADAM_B1 ** ADAM_STEP)
        v_hat = vn / (1.0 - ADAM_B2 ** ADAM_STEP)
        d_ref[...] = -ADAM_LR * (m_hat / (jnp.sqrt(v_hat) + ADAM_EPS) + ADAM_WD * w_ref[...])
        mo_ref[...] = mn
        vo_ref[...] = vn

    spec = pl.BlockSpec((tr, cols), lambda i: (i, 0))
    outs = _pcall(body, name, [_sds((rows, cols), F32)] * 3, grid=(rows // tr,), in_specs=[spec] * 4, out_specs=[spec] * 3, sem=("parallel",))(w2, g2, m2, v2)
    return tuple(o.reshape(shape) for o in outs)


def _adamw_small(name, ws, gs, ms, vs):
    n = len(ws)

    def body(*refs):
        for k in range(n):
            w_ref, g_ref, m_ref, v_ref = (refs[j * n + k] for j in range(4))
            d_ref, mo_ref, vo_ref = (refs[(4 + j) * n + k] for j in range(3))
            gv = g_ref[...]
            mn = ADAM_B1 * m_ref[...] + (1.0 - ADAM_B1) * gv
            vn = ADAM_B2 * v_ref[...] + (1.0 - ADAM_B2) * (gv * gv)
            m_hat = mn / (1.0 - ADAM_B1 ** ADAM_STEP)
            v_hat = vn / (1.0 - ADAM_B2 ** ADAM_STEP)
            d_ref[...] = -ADAM_LR * (m_hat / (jnp.sqrt(v_hat) + ADAM_EPS) + ADAM_WD * w_ref[...])
            mo_ref[...] = mn
            vo_ref[...] = vn

    outs = _pcall(body, name, [_sds(w.shape, F32) for w in ws] * 3)(*ws, *gs, *ms, *vs)
    return outs[:n], outs[n:2 * n], outs[2 * n:]


def _pad_rows(w, rows):
    return jnp.pad(w, ((0, rows - w.shape[0]), (0, 0)))


def _pad_lanes(v):
    return jnp.pad(v, (0, LANES - v.shape[0])).reshape(1, LANES)


BIG = ("w_in", "w_out", "ff_up", "ff_down")
BIG_AXIS = {"w_in": 1, "w_out": 0, "ff_up": 1, "ff_down": 0}
CONVW = ("ssd_conv_w", "ff_conv_w")
SMALL = ("norm1_g", "norm2_g", "ssd_conv_b", "ssd_dt_bias", "ssd_a_log", "ssd_d", "ssd_norm_g", "gm_vnorm_g", "gm_ws", "gm_bs", "gm_out_g", "ff_conv_b")
WEIGHTS = ("ada_w", "ada_b", "norm1_g", "norm2_g", "w_in", "ssd_conv_w", "ssd_conv_b", "ssd_dt_bias", "ssd_a_log", "ssd_d", "ssd_norm_g", "gm_vnorm_g", "gm_ws", "gm_bs", "gm_out_g", "w_out", "ff_up", "ff_conv_w", "ff_conv_b", "ff_down", "final_g")


def kernel(x, c, ada_w, ada_b, norm1_g, norm2_g, w_in, ssd_conv_w, ssd_conv_b, ssd_dt_bias, ssd_a_log, ssd_d, ssd_norm_g, gm_vnorm_g, gm_ws, gm_bs, gm_out_g, w_out, ff_up, ff_conv_w, ff_conv_b, ff_down, final_g, loss_target, m_ada_w, m_ada_b, m_norm1_g, m_norm2_g, m_w_in, m_ssd_conv_w, m_ssd_conv_b, m_ssd_dt_bias, m_ssd_a_log, m_ssd_d, m_ssd_norm_g, m_gm_vnorm_g, m_gm_ws, m_gm_bs, m_gm_out_g, m_w_out, m_ff_up, m_ff_conv_w, m_ff_conv_b, m_ff_down, m_final_g, v_ada_w, v_ada_b, v_norm1_g, v_norm2_g, v_w_in, v_ssd_conv_w, v_ssd_conv_b, v_ssd_dt_bias, v_ssd_a_log, v_ssd_d, v_ssd_norm_g, v_gm_vnorm_g, v_gm_ws, v_gm_bs, v_gm_out_g, v_w_out, v_ff_up, v_ff_conv_w, v_ff_conv_b, v_ff_down, v_final_g):
    given = dict(locals())
    weights = {n: given[n] for n in WEIGHTS}
    bl, s, d = x.shape
    nl = ada_w.shape[0]
    heads = d // HEAD_DIM
    cd = d + 2 * SSD_GROUPS * STATE
    f = ff_down.shape[1] * 4
    n_in = d + cd + heads + 2 * d
    n_proj = _round_up(3 * d + cd + LANES, 2 * LANES)
    tm = _divisors(s, 512)[-1]

    pre, _ = _gather_two_level("pre_ag", [c] + [weights[n] for n in CONVW])
    c_all = pre[0].reshape(8 * bl, d)
    conv_full = {n: jnp.concatenate([p[k, 0] for k in range(4)], axis=-1) for n, p in zip(CONVW, pre[1:])}
    mods_all, ada_token = _ada_fwd("ada", bl, c_all, ada_w, ada_b)
    mods = mods_all.reshape(nl, bl, N_MOD, 1, d)

    gathering = {}

    def start_gathers(l, token):
        for n in BIG:
            gathering[l, n] = _gather_weights_start(f"wg{l}_{n}", [weights[n][l]], token)
            token = gathering[l, n]["token"]
        return token

    def landed(l, n, after):
        (full,), token = _gather_weights_finish(f"wg{l}_{n}", gathering.pop((l, n)), after)
        return (full.reshape(-1, full.shape[2]) if BIG_AXIS[n] == 0 else jnp.concatenate([full[k] for k in range(4)], axis=1)), token

    start_gathers(0, ada_token)
    r2 = lambda v: v.reshape(1, -1)
    n_gm = d // GM_HEAD
    fg = r2(final_g)

    def layer_params(l):
        return dict(
            norm1=r2(norm1_g[l]), norm2=r2(norm2_g[l]), conv_w8=_pad_rows(conv_full["ssd_conv_w"][l], SUBLANES), conv_b=r2(ssd_conv_b[l]),
            dtb=_pad_lanes(ssd_dt_bias[l]), alog=_pad_lanes(ssd_a_log[l]), dsk=r2(jnp.repeat(ssd_d[l], HEAD_DIM)), ng=r2(ssd_norm_g[l]),
            vg=r2(gm_vnorm_g[l]), ws=gm_ws[l], bs_t=jnp.pad(gm_bs[l].T, ((0, 0), (0, LANES - n_gm))), og=r2(gm_out_g[l]),
            ff_w8=_pad_rows(conv_full["ff_conv_w"][l], SUBLANES), ff_b=r2(ff_conv_b[l]))

    def mixer_args(p):
        return (p["conv_w8"], p["conv_b"], p["dtb"], p["alog"], p["dsk"], p["ng"], p["vg"], p["ws"], p["bs_t"], p["og"])

    def padded_w_in(wi):
        return jnp.concatenate(
            [wi[:, :d], wi[:, d + cd + heads:], wi[:, d:d + cd], wi[:, d + cd:d + cd + heads], jnp.zeros((d, n_proj - (3 * d + cd + heads)), BF16)], axis=1)

    saved, xcur, pending = [], x, None
    for l in range(nl):
        p, w = layer_params(l), {}
        wi, token = landed(l, "w_in", mods_all if l == 0 else pending[0])
        w["w_in"] = padded_w_in(wi)
        sh1, sc1, g1, sh2, sc2, g2 = (mods[l, :, k] for k in range(N_MOD))
        if l + 1 < nl:
            sc1 = _after(sc1, start_gathers(l + 1, token))
        if pending is None:
            (h,) = _row_fwd(f"nm{l}_fwd", _nm_tile, [(xcur, d, 0, None)], [sc1, sh1], [p["norm1"]], [(d, BF16)], 0, tm)
            x1 = xcur
        else:
            rows = [(xcur, d, 0, None), (pending[0], d, 0, None)]
            x1, h = _row_fwd(f"rnm{l}a_fwd", _rnm_tile, rows, [pending[1], sc1, sh1], [p["norm1"]], [(d, F32), (d, BF16)], 0, tm)
        proj = _mm(f"win{l}_fwd", h, w["w_in"], F32)
        ycat, (xbc, hprev) = _mixer_fwd(f"mix{l}", proj, d, cd, *mixer_args(p))
        w["w_out"], _ = landed(l, "w_out", ycat)
        mix = _mm(f"wout{l}_fwd", ycat, w["w_out"], F32)
        x2, h2 = _row_fwd(f"rnm{l}b_fwd", _rnm_tile, [(x1, d, 0, None), (mix, d, 0, None)], [g1, sc2, sh2], [p["norm2"]], [(d, F32), (d, BF16)], 0, tm)
        w["ff_up"], _ = landed(l, "ff_up", h2)
        up = _mm(f"ffup{l}_fwd", h2, w["ff_up"], F32)
        act = _conv_act_fwd(f"ffact{l}_fwd", up, 0, f, p["ff_w8"], p["ff_b"], FF_CONV, f, BF16)
        w["ff_down"], _ = landed(l, "ff_down", act)
        down = _mm(f"ffdown{l}_fwd", act, w["ff_down"], F32)
        saved.append(dict(p=p, w=w, x_in=xcur, pending=pending, h=h, proj=proj, xbc=xbc, hprev=hprev, ycat=ycat, x1=x1, mix=mix, h2=h2, up=up, act=act))
        xcur, pending = x2, (down, g2)
    rows = [(xcur, d, 0, F32), (pending[0], d, 0, BF16), (loss_target, d, 0, None)]
    (loss_local,) = _row_fwd("final_fwd", _final_tile, rows, [pending[1]], [fg], [], 1, tm)
    loss = lax.psum(loss_local[0, 0], AXES)

    (dx2, ddown), (dg2,), (dfg,) = _row_bwd("final_bwd", _final_tile, rows, [pending[1]], [fg], [], 1, tm, [jnp.ones((1, 1), F32)])
    dmods, small_grads = [None] * nl, [None] * nl
    reducing = {}

    def start_reduce(l, n, g):
        width = g.shape[1] // 4
        by_chip = g.reshape(4, g.shape[0] // 4, g.shape[1]) if BIG_AXIS[n] == 0 else jnp.stack([g[:, k * width:(k + 1) * width] for k in range(4)])
        reducing[l, n] = _reduce_weights_start(f"wg{l}_{n}", [by_chip], None)
        return reducing[l, n]["token"]

    for l in reversed(range(nl)):
        sv = saved[l]
        p, w = sv["p"], sv["w"]
        sh1, sc1, g1, sh2, sc2, g2 = (mods[l, :, k] for k in range(N_MOD))
        dact, dw_down = _mm_bwd(f"ffdown{l}", sv["act"], w["ff_down"], ddown)
        ff_b = _after(p["ff_b"], start_reduce(l, "ff_down", dw_down))
        dup, dff_w8, dff_b = _conv_act_bwd(f"ffact{l}_bwd", sv["up"], 0, f, p["ff_w8"], ff_b, FF_CONV, f, dact, BF16)
        dh2, dw_up = _mm_bwd(f"ffup{l}", sv["h2"], w["ff_up"], dup)
        sc2 = _after(sc2, start_reduce(l, "ff_up", dw_up))
        rows = [(sv["x1"], d, 0, F32), (sv["mix"], d, 0, BF16)]
        (dx1, dmix), (dg1, dsc2, dsh2), (dn2,) = _row_bwd(f"rnm{l}b_bwd", _rnm_tile, rows, [g1, sc2, sh2], [p["norm2"]], [(d, F32), (d, BF16)], 0, tm, [dx2, dh2])
        dycat, dw_out = _mm_bwd(f"wout{l}", sv["ycat"], w["w_out"], dmix)
        p_tied = dict(p, dtb=_after(p["dtb"], start_reduce(l, "w_out", dw_out)))
        dproj, (dw8, dcb, ddtb, dalog, ddsk, dng, dvg, dws, dbs_t, dog) = _mixer_bwd(f"mix{l}", sv["proj"], d, cd, *mixer_args(p_tied), sv["xbc"], sv["hprev"], dycat)
        dh, dw_in_p = _mm_bwd(f"win{l}", sv["h"], w["w_in"], dproj)
        dw_in = jnp.concatenate([dw_in_p[:, :d], dw_in_p[:, 3 * d:3 * d + cd], dw_in_p[:, 3 * d + cd:3 * d + cd + heads], dw_in_p[:, d:3 * d]], axis=1)
        sc1 = _after(sc1, start_reduce(l, "w_in", dw_in))
        if sv["pending"] is None:
            (dx2,), (dsc1, dsh1), (dn1,) = _row_bwd(f"nm{l}_bwd", _nm_res_tile, [(sv["x_in"], d, 0, F32)], [sc1, sh1], [p["norm1"]], [(d, F32), (d, BF16)], 0, tm, [dx1, dh])
        else:
            rows = [(sv["x_in"], d, 0, F32), (sv["pending"][0], d, 0, BF16)]
            (dx2, ddown), (dg2_prev, dsc1, dsh1), (dn1,) = _row_bwd(
                f"rnm{l}a_bwd", _rnm_tile, rows, [sv["pending"][1], sc1, sh1], [p["norm1"]], [(d, F32), (d, BF16)], 0, tm, [dx1, dh])
        dmods[l] = jnp.concatenate([dsh1, dsc1, dg1, dsh2, dsc2, dg2], axis=1).reshape(bl, N_MOD * d)
        if sv["pending"] is not None:
            dg2 = dg2_prev
        small_grads[l] = dict(
            norm1_g=dn1.reshape(d), norm2_g=dn2.reshape(d), ssd_conv_b=dcb.reshape(cd), ssd_dt_bias=ddtb[0, :heads], ssd_a_log=dalog[0, :heads],
            ssd_d=ddsk.reshape(heads, HEAD_DIM).sum(-1), ssd_norm_g=dng.reshape(d), gm_vnorm_g=dvg.reshape(d), gm_ws=dws, gm_bs=dbs_t[:, :n_gm].T,
            gm_out_g=dog.reshape(d), ff_conv_b=dff_b.reshape(f), ssd_conv_w=dw8[:SSD_CONV], ff_conv_w=dff_w8[:FF_CONV])
    grad_x = dx2

    g_ada_w, g_ada_b = _ada_bwd("ada", bl, c_all, ada_w.shape[2], jnp.stack(dmods))
    small_names_r = SMALL + CONVW + ("final_g",)
    summed = _allreduce_small("small", [jnp.stack([small_grads[l][n] for l in range(nl)]) for n in SMALL + CONVW] + [dfg])
    grad = {"ada_w": g_ada_w, "ada_b": g_ada_b}
    big_grads = {}
    for l in reversed(range(nl)):
        for n in reversed(BIG):
            (big_grads[l, n],), _ = _reduce_weights_finish(f"wg{l}_{n}", reducing.pop((l, n)), summed[0])
    for n in BIG:
        grad[n] = jnp.stack([big_grads[l, n] for l in range(nl)])
    grad.update(zip(small_names_r, summed))
    chip = 2 * lax.axis_index("x") + lax.axis_index("y")
    for n in CONVW:
        width = weights[n].shape[-1]
        grad[n] = lax.dynamic_slice_in_dim(grad[n], chip * width, width, axis=2)

    delta, new_m, new_v = {}, {}, {}
    for n in ("ada_w",) + BIG:
        delta[n], new_m[n], new_v[n] = _adamw("adam_" + n, weights[n], grad[n], given["m_" + n], given["v_" + n])
    small_names = ("ada_b",) + SMALL + CONVW + ("final_g",)
    as2d = lambda t: t.reshape(1, -1) if t.ndim == 1 else t
    res = _adamw_small(
        "adam_small", [as2d(weights[n]) for n in small_names], [as2d(grad[n]) for n in small_names],
        [as2d(given["m_" + n]) for n in small_names], [as2d(given["v_" + n]) for n in small_names])
    for out, vals in zip((delta, new_m, new_v), res):
        for n, val in zip(small_names, vals):
            out[n] = val.reshape(weights[n].shape)
    grad["final_g"] = grad["final_g"].reshape(final_g.shape)

    return (loss, grad_x, *[grad[n] for n in WEIGHTS], *[delta[n] for n in WEIGHTS], *[new_m[n] for n in WEIGHTS], *[new_v[n] for n in WEIGHTS])
```

```python
import functools

import jax
import jax.numpy as jnp
from jax import lax
from jax.experimental import pallas as pl
from jax.experimental.pallas import tpu as pltpu

F32 = jnp.float32
BF16 = jnp.bfloat16
EPS = 1e-6
CHUNK = 128
HEAD_DIM = 64
STATE = 128
GM_HEAD = 128
SSD_GROUPS = 2
SSD_CONV = 4
FF_CONV = 3
N_MOD = 6
LANES = 128
SUBLANES = 8
CONV_LANES = 256
CONV_ROWS = 32
V7X_VMEM_LIMIT = 48 * 1024 * 1024
MM_VMEM_BUDGET = 30 * 1024 * 1024
MM_STEP_MACS = 512 * 1024 * 2560
V7X_MXU_FLOPS = 996e12
V7X_HBM_BYTES_PER_S = 3.3e12
V7X_STEP_SECONDS = 0.35e-6
ADAM_LR, ADAM_B1, ADAM_B2, ADAM_EPS, ADAM_WD, ADAM_STEP = 0.001, 0.9, 0.999, 1e-08, 0.01, 10
MESH = pl.DeviceIdType.MESH
AXES = ("x", "y", "c")


def _round_up(n, m):
    return (n + m - 1) // m * m


def _divisors(n, cap, mult=LANES):
    out = [t for t in range(mult, min(n, cap) + 1, mult) if n % t == 0]
    return out or [n]


def _pcall(body, name, out_shape, grid=(), in_specs=None, out_specs=None, scratch=(), sem=None, prefetch=0):
    params = pltpu.CompilerParams(dimension_semantics=sem, vmem_limit_bytes=V7X_VMEM_LIMIT)
    if prefetch:
        spec = pltpu.PrefetchScalarGridSpec(num_scalar_prefetch=prefetch, grid=grid, in_specs=in_specs, out_specs=out_specs, scratch_shapes=list(scratch))
        return pl.pallas_call(body, name=name, out_shape=out_shape, grid_spec=spec, compiler_params=params)
    if in_specs is None:
        return pl.pallas_call(body, name=name, out_shape=out_shape, compiler_params=params)
    return pl.pallas_call(body, name=name, out_shape=out_shape, grid=grid, in_specs=in_specs, out_specs=out_specs, scratch_shapes=list(scratch), compiler_params=params)


def _sds(shape, dtype):
    return jax.ShapeDtypeStruct(tuple(shape), dtype)


def _mm_tiles(m, n, k, a_bytes, b_bytes, o_bytes):
    best, best_key = None, None
    for tm in _divisors(m, 2048):
        for tn in _divisors(n, 2560):
            for tk in _divisors(k, 2560):
                vmem = 2 * (tm * tk * a_bytes + tk * tn * b_bytes + tm * tn * o_bytes) + tm * tn * 4
                if vmem > MM_VMEM_BUDGET or tm * tn * tk > MM_STEP_MACS:
                    continue
                ni, nj, nk = m // tm, n // tn, k // tk
                a_reads = 1 if nk == 1 else nj
                b_reads = 1 if (nk == 1 and nj == 1) else ni
                hbm = a_reads * m * k * a_bytes + b_reads * k * n * b_bytes + m * n * o_bytes
                t = max(2.0 * m * n * k / V7X_MXU_FLOPS, hbm / V7X_HBM_BYTES_PER_S) + ni * nj * nk * V7X_STEP_SECONDS
                key = (-t, tm * tn * tk)
                if best_key is None or key > best_key:
                    best, best_key = (tm, tn, tk), key
    return best


def _matmul(name, a, b, mode, out_dtype):
    if mode == "nn":
        (m, k), n = a.shape, b.shape[1]
    elif mode == "nt":
        (m, k), n = a.shape, b.shape[0]
    else:
        (k, m), n = a.shape, b.shape[1]
    tm, tn, tk = _mm_tiles(m, n, k, a.dtype.itemsize, b.dtype.itemsize, jnp.dtype(out_dtype).itemsize)
    nk = k // tk
    if mode == "nn":
        a_spec = pl.BlockSpec((tm, tk), lambda i, j, kk: (i, kk))
        b_spec = pl.BlockSpec((tk, tn), lambda i, j, kk: (kk, j))
        dims = ((1,), (0,))
    elif mode == "nt":
        a_spec = pl.BlockSpec((tm, tk), lambda i, j, kk: (i, kk))
        b_spec = pl.BlockSpec((tn, tk), lambda i, j, kk: (j, kk))
        dims = ((1,), (1,))
    else:
        a_spec = pl.BlockSpec((tk, tm), lambda i, j, kk: (kk, i))
        b_spec = pl.BlockSpec((tk, tn), lambda i, j, kk: (kk, j))
        dims = ((0,), (0,))

    def body(a_ref, b_ref, o_ref, acc_ref):
        kk = pl.program_id(2)
        p = lax.dot_general(a_ref[...].astype(BF16), b_ref[...].astype(BF16), (dims, ((), ())), preferred_element_type=F32)
        if nk == 1:
            o_ref[...] = p.astype(o_ref.dtype)
        else:
            @pl.when(kk == 0)
            def _():
                acc_ref[...] = p

            @pl.when(kk > 0)
            def _():
                acc_ref[...] += p

            @pl.when(kk == nk - 1)
            def _():
                o_ref[...] = acc_ref[...].astype(o_ref.dtype)

    return _pcall(
        body, name, _sds((m, n), out_dtype), grid=(m // tm, n // tn, nk), in_specs=[a_spec, b_spec],
        out_specs=pl.BlockSpec((tm, tn), lambda i, j, kk: (i, j)), scratch=[pltpu.VMEM((tm, tn), F32)],
        sem=("parallel", "parallel", "arbitrary"),
    )(a, b)


def _mm(name, a, w, out_dtype):
    return _matmul(name, a.reshape(-1, a.shape[-1]), w, "nn", out_dtype).reshape(a.shape[:-1] + (w.shape[1],))


def _mm_bwd(name, a, w, dy):
    a2, dy2 = a.reshape(-1, a.shape[-1]), dy.reshape(-1, dy.shape[-1])
    return _matmul(name + "_dx", dy2, w, "nt", BF16).reshape(a.shape), _matmul(name + "_dw", a2, dy2, "tn", BF16)


def _dg(a, b, dims):
    return lax.dot_general(a.astype(BF16), b.astype(BF16), (dims, ((), ())), preferred_element_type=F32)


@jax.custom_vjp
def _dot_nn(a, b):
    return _dg(a, b, ((1,), (0,)))


_dot_nn.defvjp(lambda a, b: (_dot_nn(a, b), (a, b)), lambda r, d: (_dg(d, r[1], ((1,), (1,))), _dg(r[0], d, ((0,), (0,)))))


@jax.custom_vjp
def _dot_nt(a, b):
    return _dg(a, b, ((1,), (1,)))


_dot_nt.defvjp(lambda a, b: (_dot_nt(a, b), (a, b)), lambda r, d: (_dg(d, r[1], ((1,), (0,))), _dg(d, r[0], ((0,), (0,)))))


@jax.custom_vjp
def _dot_tn(a, b):
    return _dg(a, b, ((0,), (0,)))


_dot_tn.defvjp(lambda a, b: (_dot_tn(a, b), (a, b)), lambda r, d: (_dg(r[1], d, ((1,), (1,))), _dg(r[0], d, ((1,), (0,)))))


def _exact_dot(a, c, dims):
    hi = a.astype(BF16)
    r1 = a - hi.astype(F32)
    mid = r1.astype(BF16)
    lo = (r1 - mid.astype(F32)).astype(BF16)
    cb = c.astype(BF16)
    f = lambda t: lax.dot_general(t, cb, (dims, ((), ())), preferred_element_type=F32)
    return f(hi) + f(mid) + f(lo)


@jax.custom_vjp
def _sel_right(a, c):
    return _exact_dot(a, c, ((1,), (0,)))


_sel_right.defvjp(lambda a, c: (_sel_right(a, c), c), lambda c, d: (_exact_dot(d, c, ((1,), (1,))), jnp.zeros_like(c)))


def _exact_dot_left(c, a, dims):
    hi = a.astype(BF16)
    r1 = a - hi.astype(F32)
    mid = r1.astype(BF16)
    lo = (r1 - mid.astype(F32)).astype(BF16)
    cb = c.astype(BF16)
    f = lambda t: lax.dot_general(cb, t, (dims, ((), ())), preferred_element_type=F32)
    return f(hi) + f(mid) + f(lo)


@jax.custom_vjp
def _sel_left(c, a):
    return _exact_dot_left(c, a, ((1,), (0,)))


_sel_left.defvjp(lambda c, a: (_sel_left(c, a), c), lambda c, d: (jnp.zeros_like(c), _exact_dot_left(c, d, ((0,), (0,)))))


def _sigmoid(x):
    return 1.0 / (1.0 + jnp.exp(-x))


def _rms(x, g):
    return x * lax.rsqrt(jnp.mean(x * x, axis=-1, keepdims=True) + EPS) * g


def _gelu(x):
    return 0.5 * x * (1.0 + lax.erf(x * (2.0 ** -0.5)))


def _causal(n):
    return lax.broadcasted_iota(jnp.int32, (n, n), 0) >= lax.broadcasted_iota(jnp.int32, (n, n), 1)


def _row_in_specs(rows, bparams, gparams, tm):
    specs = [pl.BlockSpec((1, tm, w), lambda b, i, cb=cb: (b, i, cb)) for (_, w, cb, _) in rows]
    specs += [pl.BlockSpec((1, 1, p.shape[-1]), lambda b, i: (b, 0, 0)) for p in bparams]
    specs += [pl.BlockSpec(p.shape, lambda b, i, n=p.ndim: (0,) * n) for p in gparams]
    return specs


def _row_vals(refs, n_rows, n_b, n_g):
    vals = [r[0].astype(F32) for r in refs[:n_rows]]
    vals += [r[0].astype(F32) for r in refs[n_rows:n_rows + n_b]]
    vals += [r[...].astype(F32) for r in refs[n_rows + n_b:n_rows + n_b + n_g]]
    return vals


def _row_fwd(name, tile, rows, bparams, gparams, outs, n_sum, tm):
    bl, s = rows[0][0].shape[:2]
    n_in = len(rows) + len(bparams) + len(gparams)

    def body(*refs):
        first = (pl.program_id(0) == 0) & (pl.program_id(1) == 0)
        res = tile(*_row_vals(refs, len(rows), len(bparams), len(gparams)))
        o_refs = refs[n_in:]
        for k in range(len(outs)):
            o_refs[k][0] = res[k].astype(o_refs[k].dtype)
        for k in range(n_sum):
            o_ref, val = o_refs[len(outs) + k], res[len(outs) + k]

            @pl.when(first)
            def _(o_ref=o_ref, val=val):
                o_ref[...] = val

            @pl.when(jnp.logical_not(first))
            def _(o_ref=o_ref, val=val):
                o_ref[...] += val

    out_shape = [_sds((bl, s, w), dt) for (w, dt) in outs] + [_sds((1, 1), F32)] * n_sum
    out_specs = [pl.BlockSpec((1, tm, w), lambda b, i: (b, i, 0)) for (w, _) in outs] + [pl.BlockSpec((1, 1), lambda b, i: (0, 0))] * n_sum
    return _pcall(
        body, name, out_shape, grid=(bl, s // tm), in_specs=_row_in_specs(rows, bparams, gparams, tm), out_specs=out_specs,
        sem=("arbitrary", "arbitrary"),
    )(*[r[0] for r in rows], *bparams, *gparams)


def _row_bwd(name, tile, rows, bparams, gparams, outs, n_sum, tm, cts):
    bl, s = rows[0][0].shape[:2]
    n_r, n_b, n_g = len(rows), len(bparams), len(gparams)
    n_in = n_r + n_b + n_g
    n_ct = len(outs) + n_sum
    grad_rows = [k for k in range(n_r) if rows[k][3]]

    def body(*refs):
        b, i = pl.program_id(0), pl.program_id(1)
        vals = _row_vals(refs, n_r, n_b, n_g)
        ct_refs = refs[n_in:n_in + n_ct]
        ct = [r[0].astype(F32) for r in ct_refs[:len(outs)]] + [r[...] for r in ct_refs[len(outs):]]
        _, vjp = jax.vjp(tile, *vals)
        grads = vjp(tuple(ct))
        o_refs = refs[n_in + n_ct:]
        for j, k in enumerate(grad_rows):
            o_refs[j][0] = grads[k].astype(o_refs[j].dtype)
        for k in range(n_b):
            o_ref, val = o_refs[len(grad_rows) + k], grads[n_r + k]

            @pl.when(i == 0)
            def _(o_ref=o_ref, val=val):
                o_ref[0] = val

            @pl.when(i > 0)
            def _(o_ref=o_ref, val=val):
                o_ref[0] += val

        first = (b == 0) & (i == 0)
        for k in range(n_g):
            o_ref, val = o_refs[len(grad_rows) + n_b + k], grads[n_r + n_b + k]

            @pl.when(first)
            def _(o_ref=o_ref, val=val):
                o_ref[...] = val

            @pl.when(jnp.logical_not(first))
            def _(o_ref=o_ref, val=val):
                o_ref[...] += val

    in_specs = _row_in_specs(rows, bparams, gparams, tm)
    in_specs += [pl.BlockSpec((1, tm, w), lambda b, i: (b, i, 0)) for (w, _) in outs] + [pl.BlockSpec((1, 1), lambda b, i: (0, 0))] * n_sum
    out_shape = [_sds((bl, s, rows[k][1]), rows[k][3]) for k in grad_rows]
    out_shape += [_sds(p.shape, F32) for p in bparams] + [_sds(p.shape, F32) for p in gparams]
    out_specs = [pl.BlockSpec((1, tm, rows[k][1]), lambda b, i: (b, i, 0)) for k in grad_rows]
    out_specs += [pl.BlockSpec((1, 1, p.shape[-1]), lambda b, i: (b, 0, 0)) for p in bparams]
    out_specs += [pl.BlockSpec(p.shape, lambda b, i, n=p.ndim: (0,) * n) for p in gparams]
    res = _pcall(
        body, name, out_shape, grid=(bl, s // tm), in_specs=in_specs, out_specs=out_specs, sem=("arbitrary", "arbitrary"),
    )(*[r[0] for r in rows], *bparams, *gparams, *cts)
    return res[:len(grad_rows)], res[len(grad_rows):len(grad_rows) + n_b], res[len(grad_rows) + n_b:]


def _nm_tile(x, sc, sh, g):
    return (_rms(x, g) * (1.0 + sc) + sh,)


def _nm_res_tile(x, sc, sh, g):
    return x, _rms(x, g) * (1.0 + sc) + sh


def _rnm_tile(x, o, gate, sc, sh, g):
    xn = x + gate * o
    return xn, _rms(xn, g) * (1.0 + sc) + sh


def _final_tile(x, o, tgt, gate, g):
    e = _rms(x + gate * o, g) - tgt
    return (0.5 * jnp.sum(jnp.mean(e * e, axis=-1, keepdims=True), axis=0, keepdims=True),)


def _gm_tile(u_in, v_in, vg, ws, bs_t, og):
    d = u_in.shape[1]
    u, vn = _gelu(u_in), _rms(_gelu(v_in), vg)
    causal = _causal(CHUNK)
    lane = lax.broadcasted_iota(jnp.int32, (1, LANES), 1)
    parts = []
    for h in range(d // GM_HEAD):
        bias = jnp.sum(bs_t * (lane == h).astype(F32), axis=1, keepdims=True)
        parts.append(_dot_nn(jnp.where(causal, ws[h], 0.0), vn[:, h * GM_HEAD:(h + 1) * GM_HEAD]) + bias)
    return (_rms(u * jnp.concatenate(parts, axis=1), og),)


def _conv_window(ref, r0, rows, c0, tc, seq, before, after):
    parts = []
    if before:
        p0 = pl.multiple_of(jnp.maximum(r0 - SUBLANES, 0), SUBLANES)
        parts.append(jnp.where(r0 > 0, ref[0, pl.ds(p0, SUBLANES), pl.ds(c0, tc)].astype(F32), 0.0))
    parts.append(ref[0, pl.ds(r0, rows), pl.ds(c0, tc)].astype(F32))
    if after:
        n0 = pl.multiple_of(jnp.minimum(r0 + rows, seq - SUBLANES), SUBLANES)
        parts.append(jnp.where(r0 + rows < seq, ref[0, pl.ds(n0, SUBLANES), pl.ds(c0, tc)].astype(F32), 0.0))
    return jnp.concatenate(parts, axis=0) if len(parts) > 1 else parts[0]


def _conv_taps_pre(xe, w, b, taps, rows):
    pre = xe[SUBLANES:SUBLANES + rows] * w[taps - 1] + b
    for j in range(1, taps):
        pre = pre + pltpu.roll(xe, j, 0)[SUBLANES:SUBLANES + rows] * w[taps - 1 - j]
    return pre


def _conv_fwd(name, src, col0, chans, w8, b, taps, gated, out_dtype):
    bl, s, _ = src.shape
    tc = CONV_LANES
    xw = 2 * tc if gated else tc

    def body(x_ref, w_ref, b_ref, o_ref):
        w = [w_ref[k:k + 1, :] for k in range(taps)]
        bias = b_ref[...]

        def step(c, carry):
            r0 = pl.multiple_of(c * CONV_ROWS, CONV_ROWS)
            xe = _conv_window(x_ref, r0, CONV_ROWS, 0, tc, s, True, False)
            pre = _conv_taps_pre(xe, w, bias, taps, CONV_ROWS)
            y = pre * _sigmoid(pre)
            if gated:
                y = y * x_ref[0, pl.ds(r0, CONV_ROWS), pl.ds(tc, tc)]
            o_ref[0, pl.ds(r0, CONV_ROWS), :] = y.astype(o_ref.dtype)
            return carry

        lax.fori_loop(0, s // CONV_ROWS, step, 0)

    first = 0 if gated else col0 // tc
    return _pcall(
        body, name, _sds((bl, s, chans), out_dtype), grid=(chans // tc, bl),
        in_specs=[pl.BlockSpec((1, s, xw), lambda j, bb: (bb, 0, first + j)), pl.BlockSpec((SUBLANES, tc), lambda j, bb: (0, j)), pl.BlockSpec((1, tc), lambda j, bb: (0, j))],
        out_specs=pl.BlockSpec((1, s, tc), lambda j, bb: (bb, 0, j)), sem=("parallel", "arbitrary"),
    )(src, w8, b)


def _conv_bwd(name, src, col0, chans, w8, b, taps, gated, dy, dx_dtype):
    bl, s, _ = src.shape
    tc = CONV_LANES
    xw = 2 * tc if gated else tc
    ext = CONV_ROWS + SUBLANES

    def body(x_ref, w_ref, b_ref, dy_ref, dx_ref, dw_ref, db_ref):
        bb = pl.program_id(1)
        w = [w_ref[k:k + 1, :] for k in range(taps)]
        bias = b_ref[...]

        def fold(v):
            acc = v[0:SUBLANES]
            for i in range(1, CONV_ROWS // SUBLANES):
                acc = acc + v[i * SUBLANES:(i + 1) * SUBLANES]
            return acc

        def step(c, carry):
            r0 = pl.multiple_of(c * CONV_ROWS, CONV_ROWS)
            xe = _conv_window(x_ref, r0, CONV_ROWS, 0, tc, s, True, True)
            pre = _conv_taps_pre(xe, w, bias, taps, ext)
            sig = _sigmoid(pre)
            d = _conv_window(dy_ref, r0, CONV_ROWS, 0, tc, s, False, True)
            dsil = d * _conv_window(x_ref, r0, CONV_ROWS, tc, tc, s, False, True) if gated else d
            dpre = dsil * sig * (1.0 + pre * (1.0 - sig))
            dx = dpre[:CONV_ROWS] * w[taps - 1]
            for j in range(1, taps):
                dx = dx + pltpu.roll(dpre, ext - j, 0)[:CONV_ROWS] * w[taps - 1 - j]
            dx_ref[0, pl.ds(r0, CONV_ROWS), pl.ds(0, tc)] = dx.astype(dx_ref.dtype)
            if gated:
                dx_ref[0, pl.ds(r0, CONV_ROWS), pl.ds(tc, tc)] = (d[:CONV_ROWS] * (pre * sig)[:CONV_ROWS]).astype(dx_ref.dtype)
            here = dpre[:CONV_ROWS]
            sums = [fold(here * (pltpu.roll(xe, taps - 1 - k, 0) if k < taps - 1 else xe)[SUBLANES:ext]) + carry[k] for k in range(taps)]
            return tuple(sums) + (fold(here) + carry[taps],)

        zero = jnp.zeros((SUBLANES, tc), F32)
        sums = lax.fori_loop(0, s // CONV_ROWS, step, (zero,) * (taps + 1))
        rows = [jnp.sum(t, axis=0, keepdims=True) for t in sums]
        dw = jnp.concatenate(rows[:taps] + [jnp.zeros_like(rows[0])] * (SUBLANES - taps), axis=0)

        @pl.when(bb == 0)
        def _():
            dw_ref[...] = dw
            db_ref[...] = rows[taps]

        @pl.when(bb > 0)
        def _():
            dw_ref[...] += dw
            db_ref[...] += rows[taps]

    first = 0 if gated else col0 // tc
    out_shape = [_sds((bl, s, chans * (2 if gated else 1)), dx_dtype), _sds((SUBLANES, chans), F32), _sds((1, chans), F32)]
    return _pcall(
        body, name, out_shape, grid=(chans // tc, bl),
        in_specs=[
            pl.BlockSpec((1, s, xw), lambda j, bb: (bb, 0, first + j)), pl.BlockSpec((SUBLANES, tc), lambda j, bb: (0, j)),
            pl.BlockSpec((1, tc), lambda j, bb: (0, j)), pl.BlockSpec((1, s, tc), lambda j, bb: (bb, 0, j)),
        ],
        out_specs=[pl.BlockSpec((1, s, xw), lambda j, bb: (bb, 0, j)), pl.BlockSpec((SUBLANES, tc), lambda j, bb: (0, j)), pl.BlockSpec((1, tc), lambda j, bb: (0, j))],
        sem=("parallel", "arbitrary"),
    )(src, w8, b, dy)


def _gate_value_blocks(w, f):
    lead = w.shape[:-1]
    return jnp.swapaxes(w.reshape(lead + (2, f // CONV_LANES, CONV_LANES)), -3, -2).reshape(lead + (2 * f,))


def _gate_value_columns(w, f):
    lead = w.shape[:-1]
    return jnp.swapaxes(w.reshape(lead + (f // CONV_LANES, 2, CONV_LANES)), -3, -2).reshape(lead + (2 * f,))


def _ssd_chunk(g_idx, states, xs, bm, cm, dtr, z, dtb, alog, dsk, ng):
    gw = xs.shape[1]
    hpg = gw // HEAD_DIM
    dt = jax.nn.softplus(dtr + dtb)
    da = dt * (-jnp.exp(alog))
    causal = _causal(CHUNK)
    acs = _sel_left(causal.astype(F32), da)
    head_of_col = g_idx * hpg + lax.broadcasted_iota(jnp.int32, (LANES, gw), 1) // HEAD_DIM
    expand = (lax.broadcasted_iota(jnp.int32, (LANES, gw), 0) == head_of_col).astype(F32)
    dt_e, acs_e = _sel_right(dt, expand), _sel_right(acs, expand)
    last = lax.broadcasted_iota(jnp.int32, (CHUNK, gw), 0) == CHUNK - 1
    alast_e = jnp.sum(jnp.where(last, acs_e, 0.0), axis=0, keepdims=True)
    xc = xs * dt_e
    xc_st = xc * jnp.exp(alast_e - acs_e)
    decay_out, chunk_decay = jnp.exp(acs_e), jnp.exp(alast_e)
    cb = _dot_nt(cm, bm)
    acs_t = acs.T
    lane = lax.broadcasted_iota(jnp.int32, (1, LANES), 1)
    sub = lax.broadcasted_iota(jnp.int32, (LANES, 1), 0)
    ys, new_states = [], []
    for p in range(gw // LANES):
        sl = slice(p * LANES, (p + 1) * LANES)
        xcp = xc[:, sl]
        y = _dot_nn(cm, states[p]) * decay_out[:, sl]
        for q in range(2):
            head = g_idx * hpg + 2 * p + q
            col = jnp.sum(acs * (lane == head).astype(F32), axis=1, keepdims=True)
            row = jnp.sum(acs_t * (sub == head).astype(F32), axis=0, keepdims=True)
            decay = jnp.where(causal, jnp.exp(jnp.where(causal, col - row, 0.0)), 0.0)
            half = ((lane // HEAD_DIM) == q).astype(F32)
            y = y + _dot_nn(cb * decay, xcp * half)
        ys.append(y)
        new_states.append(states[p] * chunk_decay[:, sl] + _dot_tn(bm, xc_st[:, sl]))
    y = jnp.concatenate(ys, axis=1) + dsk * xs
    gated = y * (z * _sigmoid(z))
    return tuple(new_states), _rms(gated, ng)


def _ssd_specs(d, gw, dt_col, rev, nc):
    ci = (lambda i: nc - 1 - i) if rev else (lambda i: i)
    return [
        pl.BlockSpec((1, CHUNK, gw), lambda g, b, i: (b, ci(i), g)),
        pl.BlockSpec((1, CHUNK, STATE), lambda g, b, i: (b, ci(i), d // STATE + g)),
        pl.BlockSpec((1, CHUNK, STATE), lambda g, b, i: (b, ci(i), d // STATE + SSD_GROUPS + g)),
        pl.BlockSpec((1, CHUNK, LANES), lambda g, b, i: (b, ci(i), dt_col // LANES)),
        pl.BlockSpec((1, CHUNK, gw), lambda g, b, i: (b, ci(i), g)),
        pl.BlockSpec((1, LANES), lambda g, b, i: (0, 0)), pl.BlockSpec((1, LANES), lambda g, b, i: (0, 0)),
        pl.BlockSpec((1, gw), lambda g, b, i: (0, g)), pl.BlockSpec((1, gw), lambda g, b, i: (0, g)),
    ]


def _ssd_fwd(name, xbc, proj, dt_col, dtb, alog, dsk, ng):
    bl, s, cd = xbc.shape
    d = dsk.shape[1]
    gw, nc = d // SSD_GROUPS, s // CHUNK
    npair = gw // LANES

    def body(xs_ref, bm_ref, cm_ref, dt_ref, z_ref, dtb_ref, alog_ref, dsk_ref, ng_ref, y_ref, hp_ref, st_ref):
        g, i = pl.program_id(0), pl.program_id(2)

        @pl.when(i == 0)
        def _():
            st_ref[...] = jnp.zeros_like(st_ref)

        states = tuple(st_ref[p] for p in range(npair))
        hp_ref[0, 0, 0] = st_ref[...]
        new_states, yn = _ssd_chunk(g, states, xs_ref[0], bm_ref[0], cm_ref[0], dt_ref[0], z_ref[0], dtb_ref[...], alog_ref[...], dsk_ref[...], ng_ref[...])
        for p in range(npair):
            st_ref[p] = new_states[p]
        y_ref[0] = yn.astype(y_ref.dtype)

    out_shape = [_sds((bl, s, d), BF16), _sds((SSD_GROUPS, bl, nc, npair, STATE, LANES), F32)]
    out_specs = [
        pl.BlockSpec((1, CHUNK, gw), lambda g, b, i: (b, i, g)),
        pl.BlockSpec((1, 1, 1, npair, STATE, LANES), lambda g, b, i: (g, b, i, 0, 0, 0)),
    ]
    return _pcall(
        body, name, out_shape, grid=(SSD_GROUPS, bl, nc), in_specs=_ssd_specs(d, gw, dt_col, False, nc), out_specs=out_specs,
        scratch=[pltpu.VMEM((npair, STATE, LANES), F32)], sem=("arbitrary", "arbitrary", "arbitrary"),
    )(xbc, xbc, xbc, proj, proj, dtb, alog, dsk, ng)


def _ssd_bwd(name, xbc, proj, dt_col, dtb, alog, dsk, ng, hprev, dy):
    bl, s, cd = xbc.shape
    d = dsk.shape[1]
    gw, nc = d // SSD_GROUPS, s // CHUNK
    npair = gw // LANES

    def body(xs_ref, bm_ref, cm_ref, dt_ref, z_ref, dtb_ref, alog_ref, dsk_ref, ng_ref, hp_ref, dy_ref,
             dxs_ref, dbm_ref, dcm_ref, ddt_ref, dz_ref, ddtb_ref, dalog_ref, ddsk_ref, dng_ref, dst_ref):
        g, b, i = pl.program_id(0), pl.program_id(1), pl.program_id(2)

        @pl.when(i == 0)
        def _():
            dst_ref[...] = jnp.zeros_like(dst_ref)

        states = tuple(hp_ref[0, 0, 0, p] for p in range(npair))
        step = functools.partial(_ssd_chunk, g)
        _, vjp = jax.vjp(step, states, xs_ref[0], bm_ref[0], cm_ref[0], dt_ref[0], z_ref[0], dtb_ref[...], alog_ref[...], dsk_ref[...], ng_ref[...])
        d_states, dxs, dbm, dcm, ddt, dz, ddtb, dalog, ddsk, dng = vjp((tuple(dst_ref[p] for p in range(npair)), dy_ref[0].astype(F32)))
        for p in range(npair):
            dst_ref[p] = d_states[p]
        dxs_ref[0], dbm_ref[0], dcm_ref[0] = dxs, dbm, dcm
        dz_ref[0] = dz.astype(dz_ref.dtype)
        ddt_ref[0, 0] = ddt
        first_g = (b == 0) & (i == 0)
        first = first_g & (g == 0)
        for o_ref, val, init in ((ddtb_ref, ddtb, first), (dalog_ref, dalog, first), (ddsk_ref, ddsk, first_g), (dng_ref, dng, first_g)):
            @pl.when(init)
            def _(o_ref=o_ref, val=val):
                o_ref[...] = val

            @pl.when(jnp.logical_not(init))
            def _(o_ref=o_ref, val=val):
                o_ref[...] += val

    rc = lambda i: nc - 1 - i
    in_specs = _ssd_specs(d, gw, dt_col, True, nc) + [
        pl.BlockSpec((1, 1, 1, npair, STATE, LANES), lambda g, b, i: (g, b, rc(i), 0, 0, 0)),
        pl.BlockSpec((1, CHUNK, gw), lambda g, b, i: (b, rc(i), g)),
    ]
    out_shape = [
        _sds((bl, s, d), F32), _sds((bl, s, SSD_GROUPS * STATE), F32), _sds((bl, s, SSD_GROUPS * STATE), F32),
        _sds((SSD_GROUPS, bl, s, LANES), F32), _sds((bl, s, d), BF16),
        _sds((1, LANES), F32), _sds((1, LANES), F32), _sds((1, d), F32), _sds((1, d), F32),
    ]
    out_specs = [
        pl.BlockSpec((1, CHUNK, gw), lambda g, b, i: (b, rc(i), g)),
        pl.BlockSpec((1, CHUNK, STATE), lambda g, b, i: (b, rc(i), g)), pl.BlockSpec((1, CHUNK, STATE), lambda g, b, i: (b, rc(i), g)),
        pl.BlockSpec((1, 1, CHUNK, LANES), lambda g, b, i: (g, b, rc(i), 0)),
        pl.BlockSpec((1, CHUNK, gw), lambda g, b, i: (b, rc(i), g)),
        pl.BlockSpec((1, LANES), lambda g, b, i: (0, 0)), pl.BlockSpec((1, LANES), lambda g, b, i: (0, 0)),
        pl.BlockSpec((1, gw), lambda g, b, i: (0, g)), pl.BlockSpec((1, gw), lambda g, b, i: (0, g)),
    ]
    return _pcall(
        body, name, out_shape, grid=(SSD_GROUPS, bl, nc), in_specs=in_specs, out_specs=out_specs,
        scratch=[pltpu.VMEM((npair, STATE, LANES), F32)], sem=("arbitrary", "arbitrary", "arbitrary"),
    )(xbc, xbc, xbc, proj, proj, dtb, alog, dsk, ng, hprev, dy)


def _mixer_fwd(name, proj, d, cd, conv_w8, conv_b, dtb, alog, dsk, ng, vg, ws, bs_t, og):
    xbc = _conv_fwd(name + "_conv_fwd", proj, 3 * d, cd, conv_w8, conv_b, SSD_CONV, False, F32)
    y, hprev = _ssd_fwd(name + "_ssd_fwd", xbc, proj, 3 * d + cd, dtb, alog, dsk, ng)
    rows = [(proj, d, 1, BF16), (proj, d, 2, BF16)]
    (g_out,) = _row_fwd(name + "_gm_fwd", _gm_tile, rows, [], [vg, ws, bs_t, og], [(d, BF16)], 0, CHUNK)
    return jnp.concatenate([y, g_out], axis=-1), (xbc, hprev)


def _mixer_bwd(name, proj, d, cd, conv_w8, conv_b, dtb, alog, dsk, ng, vg, ws, bs_t, og, xbc, hprev, dycat):
    bl, s, n_proj = proj.shape
    dt_col = 3 * d + cd
    dy, dg_out = dycat[..., :d], dycat[..., d:]
    dxs, dbm, dcm, ddt2, dz, ddtb, dalog, ddsk, dng = _ssd_bwd(name + "_ssd_bwd", xbc, proj, dt_col, dtb, alog, dsk, ng, hprev, dy)
    dxbc_act = jnp.concatenate([dxs, dbm, dcm], axis=-1)
    dxbc, dw8, dcb = _conv_bwd(name + "_conv_bwd", proj, 3 * d, cd, conv_w8, conv_b, SSD_CONV, False, dxbc_act, BF16)
    rows = [(proj, d, 1, BF16), (proj, d, 2, BF16)]
    (du, dv), _, (dvg, dws, dbs_t, dog) = _row_bwd(name + "_gm_bwd", _gm_tile, rows, [], [vg, ws, bs_t, og], [(d, BF16)], 0, CHUNK, [dg_out])
    ddt = (ddt2[0] + ddt2[1]).astype(BF16)
    pad = jnp.zeros((bl, s, n_proj - dt_col - LANES), BF16)
    dproj = jnp.concatenate([dz, du, dv, dxbc, ddt, pad], axis=-1)
    return dproj, (dw8, dcb, ddtb, dalog, ddsk, dng, dvg, dws, dbs_t, dog)


def _position():
    return lax.axis_index("x"), lax.axis_index("y"), lax.axis_index("c")


def _at(ref, idx):
    return ref.at[idx] if len(idx) else ref


def _exchange(name, inputs, out_shapes, plan, inplace=False):
    if inplace:
        out_shapes = [(a.shape, a.dtype) for a in inputs]
    n_in, n_out = len(inputs), len(out_shapes)
    n_copy = len(plan(0, 0, 0))

    def body(*refs):
        in_refs, out_refs, token = refs[:n_in], refs[n_in:n_in + n_out], refs[n_in + n_out]
        send_sems, recv_sems = refs[n_in + n_out + 1:]
        token[...] = jnp.zeros_like(token)
        x, y, c = _position()
        copies = plan(x, y, c)

        def copy(k, src, dst, peer):
            return pltpu.make_async_remote_copy(src_ref=src, dst_ref=dst, send_sem=send_sems.at[k], recv_sem=recv_sems.at[k], device_id=peer, device_id_type=MESH)

        src_refs = out_refs if inplace else in_refs
        sends = [copy(k, _at(src_refs[sa], si), _at(out_refs[da], di), peer) for k, (sa, si, da, di, peer, _) in enumerate(copies)]
        for cp in sends:
            cp.start()
        for k, (sa, si, da, _, peer, li) in enumerate(copies):
            copy(k, _at(src_refs[sa], si), _at(out_refs[da], li), peer).wait_recv()
        for cp in sends:
            cp.wait_send()

    any_spec = pl.BlockSpec(memory_space=pl.ANY)
    outs = pl.pallas_call(
        body, name=name, out_shape=[_sds(s, dt) for s, dt in out_shapes] + [_sds((SUBLANES, LANES), F32)], in_specs=[any_spec] * n_in,
        out_specs=[any_spec] * n_out + [pl.BlockSpec(memory_space=pltpu.VMEM)],
        scratch_shapes=[pltpu.SemaphoreType.DMA((n_copy,)), pltpu.SemaphoreType.DMA((n_copy,))],
        input_output_aliases={i: i for i in range(n_in)} if inplace else {},
    )(*inputs)
    return list(outs[:n_out]), outs[n_out]


def _exchange_start(name, inputs, out_shapes, plan):
    n_in, n_out = len(inputs), len(out_shapes)
    n_copy = len(plan(0, 0, 0))

    def body(*refs):
        in_refs, land_refs = refs[:n_in], refs[n_in:n_in + n_out]
        send_sems, recv_sems = refs[n_in + n_out:n_in + n_out + 2]
        token = refs[-1]
        x, y, c = _position()
        for k, (sa, si, da, di, peer, _) in enumerate(plan(x, y, c)):
            pltpu.make_async_remote_copy(
                src_ref=_at(in_refs[sa], si), dst_ref=_at(land_refs[da], di), send_sem=send_sems.at[k], recv_sem=recv_sems.at[k],
                device_id=peer, device_id_type=MESH).start()
        token[...] = jnp.zeros_like(token)

    hbm, sem = pl.BlockSpec(memory_space=pltpu.HBM), pl.BlockSpec(memory_space=pltpu.SEMAPHORE)
    lands = [lax.empty(s, dt) for s, dt in out_shapes]
    args = [pltpu.with_memory_space_constraint(a, pltpu.HBM) for a in list(inputs) + lands]
    outs = pl.pallas_call(
        body, name=name,
        out_shape=(pltpu.SemaphoreType.DMA((n_copy,)), pltpu.SemaphoreType.DMA((n_copy,)), *[pltpu.HBM(a.shape, a.dtype) for a in args], _sds((SUBLANES, LANES), F32)),
        in_specs=[hbm] * (n_in + n_out), out_specs=(sem, sem, *[hbm] * (n_in + n_out), pl.BlockSpec(memory_space=pltpu.VMEM)),
        input_output_aliases={i: 2 + i for i in range(n_in + n_out)},
        compiler_params=pltpu.CompilerParams(has_side_effects=pltpu.SideEffectType.DATAFLOW_SIDE_EFFECTING),
    )(*args)
    return dict(name=name, plan=plan, sems=outs[:2], ins=list(outs[2:2 + n_in]), lands=list(outs[2 + n_in:2 + n_in + n_out]), token=outs[-1])


def _exchange_wait(started, after):
    plan, n_in, n_out = started["plan"], len(started["ins"]), len(started["lands"])

    def body(*refs):
        in_refs, land_refs = refs[:n_in], refs[n_in:n_in + n_out]
        send_sems, recv_sems = refs[n_in + n_out:n_in + n_out + 2]
        token = refs[-1]
        x, y, c = _position()
        for k, (sa, si, da, _, peer, li) in enumerate(plan(x, y, c)):
            cp = pltpu.make_async_remote_copy(
                src_ref=_at(in_refs[sa], si), dst_ref=_at(land_refs[da], li), send_sem=send_sems.at[k], recv_sem=recv_sems.at[k],
                device_id=peer, device_id_type=MESH)
            cp.wait_send()
            cp.wait_recv()
        token[...] = jnp.zeros_like(token)

    hbm, sem = pl.BlockSpec(memory_space=pltpu.HBM), pl.BlockSpec(memory_space=pltpu.SEMAPHORE)
    bufs = started["ins"] + started["lands"]
    outs = pl.pallas_call(
        body, name=started["name"] + "_wait", out_shape=(*[pltpu.HBM(a.shape, a.dtype) for a in bufs], _sds((SUBLANES, LANES), F32)),
        in_specs=[hbm] * len(bufs) + [sem, sem, pl.BlockSpec(memory_space=pl.ANY)], out_specs=(*[hbm] * len(bufs), pl.BlockSpec(memory_space=pltpu.VMEM)),
        input_output_aliases={i: i for i in range(len(bufs))},
        compiler_params=pltpu.CompilerParams(has_side_effects=pltpu.SideEffectType.DATAFLOW_SIDE_EFFECTING),
    )(*bufs, *started["sems"], after)
    return list(outs[:n_in]), list(outs[n_in:n_in + n_out]), outs[-1]


def _after(value, token):
    return value + token[0, 0].astype(value.dtype)


def _chip_peers(x, y, c):
    return [(1 - x, y, c), (x, 1 - y, c), (1 - x, 1 - y, c)]


def _chip_of(p):
    return 2 * p[0] + p[1]


def _set_slot(slots, me, blk):
    return lax.dynamic_update_slice(slots, blk[None], (me,) + (0,) * blk.ndim)


def _by_core(c, mine, other, axis):
    return jnp.where(c == 0, jnp.stack([mine, other], axis), jnp.stack([other, mine], axis))


def _plan_gather_chips(n):
    def plan(x, y, c):
        return [(a, (), a, (2 * x + y,), p, (_chip_of(p),)) for a in range(n) for p in _chip_peers(x, y, c)]

    return plan


def _gather_chips(name, blocks):
    recv, token = _exchange(name, blocks, [((4,) + b.shape, b.dtype) for b in blocks], _plan_gather_chips(len(blocks)))
    x, y, _ = _position()
    return [_set_slot(r, 2 * x + y, b) for r, b in zip(recv, blocks)], token


def _gather_pass_cores(name, blocks, from_chips):
    n = len(blocks)

    def plan_cores(x, y, c):
        me, sib = 2 * x + y, (x, y, 1 - c)
        own = [(a, (), a, (me,), sib, (me,)) for a in range(n)]
        passed = [(n + a, (_chip_of(p),), a, (_chip_of(p),), sib, (_chip_of(p),)) for a in range(n) for p in _chip_peers(x, y, c)]
        return own + passed

    from_core, token = _exchange(name + "_cores", list(blocks) + list(from_chips), [((4,) + b.shape, b.dtype) for b in blocks], plan_cores)
    x, y, c = _position()
    return [_by_core(c, _set_slot(r1, 2 * x + y, b), r2, 1) for b, r1, r2 in zip(blocks, from_chips, from_core)], token


def _gather_two_level(name, blocks):
    from_chips, _ = _exchange(name + "_chips", blocks, [((4,) + b.shape, b.dtype) for b in blocks], _plan_gather_chips(len(blocks)))
    return _gather_pass_cores(name, blocks, from_chips)


def _pair_add(name, g42, r4):
    _, _, rh, cols = g42.shape
    tr = _divisors(rh, 512, SUBLANES * 2)[-1]

    def body(c_ref, a_ref, b_ref, o_ref):
        o_ref[0] = (a_ref[0, 0].astype(F32) + b_ref[0].astype(F32)).astype(o_ref.dtype)

    cidx = lax.axis_index("c").astype(jnp.int32).reshape(1)
    return _pcall(
        body, name, _sds(r4.shape, BF16), grid=(4, rh // tr),
        in_specs=[pl.BlockSpec((1, 1, tr, cols), lambda s, i, c_ref: (s, c_ref[0], i, 0)), pl.BlockSpec((1, tr, cols), lambda s, i, c_ref: (s, i, 0))],
        out_specs=pl.BlockSpec((1, tr, cols), lambda s, i, c_ref: (s, i, 0)), sem=("parallel", "parallel"), prefetch=1,
    )(cidx, g42, r4)


def _slot_sum(name, parts):
    n, r, cols = parts.shape
    cap = max(2 * SUBLANES, (4 * 1024 * 1024) // (n * cols * parts.dtype.itemsize))
    tr = _divisors(r, cap, 2 * SUBLANES)[-1]

    def body(p_ref, o_ref):
        acc = p_ref[0].astype(F32)
        for k in range(1, n):
            acc = acc + p_ref[k].astype(F32)
        o_ref[...] = acc

    return _pcall(
        body, name, _sds((r, cols), F32), grid=(r // tr,), in_specs=[pl.BlockSpec((n, tr, cols), lambda i: (0, i, 0))],
        out_specs=pl.BlockSpec((tr, cols), lambda i: (i, 0)), sem=("parallel",),
    )(parts)


def _slot_sums(name, parts):
    k = len(parts)

    def body(*refs):
        for p_ref, o_ref in zip(refs[:k], refs[k:]):
            acc = p_ref[0]
            for j in range(1, p_ref.shape[0]):
                acc = acc + p_ref[j]
            o_ref[...] = acc

    return list(_pcall(body, name, [_sds(p.shape[1:], F32) for p in parts])(*parts))


def _gather_weights_start(name, shards, token):
    c = lax.axis_index("c")
    halves = [lax.dynamic_slice_in_dim(w, c * (w.shape[0] // 2), w.shape[0] // 2, 0).astype(BF16) for w in shards]
    if token is not None:
        halves[0] = _after(halves[0], token)
    n = len(halves)

    def plan(x, y, c):
        return [(a, (), a, (2 * x + y, c), p, (_chip_of(p), c)) for a in range(n) for p in _chip_peers(x, y, c)]

    return _exchange_start(name + "_ag_chips", halves, [((4, 2) + h.shape, h.dtype) for h in halves], plan)


def _gather_weights_finish(name, started, after):
    halves, slots, token = _exchange_wait(started, after)
    n = len(halves)
    x, y, c = _position()
    slots = [lax.dynamic_update_slice(s, h[None, None], (2 * x + y, c, 0, 0)) for s, h in zip(slots, halves)]

    def plan_cores(x, y, c):
        return [(a, (s, c), a, (s, c), (x, y, 1 - c), (s, 1 - c)) for a in range(n) for s in range(4)]

    full, _ = _exchange(name + "_ag_cores", slots, None, plan_cores, inplace=True)
    return [f.reshape((4, 2 * h.shape[0], h.shape[1])) for f, h in zip(full, halves)], token


def _reduce_weights_start(name, grads, token):
    n = len(grads)
    g42 = [g.reshape(4, 2, g.shape[1] // 2, g.shape[2]) for g in grads]
    if token is not None:
        g42[0] = _after(g42[0], token)

    def plan_swap(x, y, c):
        return [(a, (s, 1 - c), a, (s,), (x, y, 1 - c), (s,)) for a in range(n) for s in range(4)]

    def plan_chips(x, y, c):
        return [(a, (_chip_of(p),), a, (2 * x + y,), p, (_chip_of(p),)) for a in range(n) for p in _chip_peers(x, y, c)]

    other, _ = _exchange(name + "_rs_cores", g42, [((4,) + g.shape[2:], g.dtype) for g in g42], plan_swap)
    pair = [_pair_add(f"{name}_rs_pair{a}", g, o) for a, (g, o) in enumerate(zip(g42, other))]
    return _exchange_start(name + "_rs_chips", pair, [(p.shape, p.dtype) for p in pair], plan_chips)


def _reduce_weights_finish(name, started, after):
    pair, recv, token = _exchange_wait(started, after)
    n = len(pair)
    x, y, c = _position()
    me = 2 * x + y

    def plan_share(x, y, c):
        return [(a, (), a, (), (x, y, 1 - c), ()) for a in range(n)]

    parts = [lax.dynamic_update_slice(r, lax.dynamic_slice_in_dim(p, me, 1, 0), (me, 0, 0)) for r, p in zip(recv, pair)]
    mine = [_slot_sum(f"{name}_rs_sum{a}", p) for a, p in enumerate(parts)]
    theirs, _ = _exchange(name + "_rs_share", mine, [(m.shape, m.dtype) for m in mine], plan_share)
    return [_by_core(c, m, t, 0).reshape(2 * m.shape[0], m.shape[1]) for m, t in zip(mine, theirs)], token


def _allreduce_small(name, grads):
    both, _ = _gather_two_level(name + "_ag", list(grads))
    return _slot_sums(name + "_sum", [g.reshape((8,) + g.shape[2:]) for g in both])


def _ada_fwd_call(name, c_all, w, b_shard):
    nl, d, ns = w.shape
    nb = c_all.shape[0]

    def body(c_ref, w_ref, b_ref, o_ref):
        cv = c_ref[...]
        o_ref[0] = _dg(cv * _sigmoid(cv), w_ref[0], ((1,), (0,))) + b_ref[0]

    return _pcall(
        body, name, _sds((nl, nb, ns), F32), grid=(nl,),
        in_specs=[pl.BlockSpec((nb, d), lambda l: (0, 0)), pl.BlockSpec((1, d, ns), lambda l: (l, 0, 0)), pl.BlockSpec((1, 1, ns), lambda l: (l, 0, 0))],
        out_specs=pl.BlockSpec((1, nb, ns), lambda l: (l, 0, 0)), sem=("parallel",),
    )(c_all, w, b_shard)


def _ada_bwd_call(name, c_all, dm_shard, dm_all):
    nl, nb, ns = dm_shard.shape
    d = c_all.shape[1]
    nm = dm_all.shape[2]

    def body(c_ref, ds_ref, da_ref, dw_ref, db_ref):
        cv = c_ref[...]
        dw_ref[0] = _dg(cv * _sigmoid(cv), ds_ref[0], ((0,), (0,)))
        db_ref[0] = jnp.sum(da_ref[0], axis=0, keepdims=True)

    return _pcall(
        body, name, [_sds((nl, d, ns), F32), _sds((nl, 1, nm), F32)], grid=(nl,),
        in_specs=[pl.BlockSpec((nb, d), lambda l: (0, 0)), pl.BlockSpec((1, nb, ns), lambda l: (l, 0, 0)), pl.BlockSpec((1, nb, nm), lambda l: (l, 0, 0))],
        out_specs=[pl.BlockSpec((1, d, ns), lambda l: (l, 0, 0)), pl.BlockSpec((1, 1, nm), lambda l: (l, 0, 0))], sem=("parallel",),
    )(c_all, dm_shard, dm_all)


def _ada_fwd(name, bl, c_all, w, b):
    nl, d, ns = w.shape
    chip = 2 * lax.axis_index("x") + lax.axis_index("y")
    b_shard = lax.dynamic_slice(b, (0, chip * ns), (nl, ns)).reshape(nl, 1, ns)
    shard = _ada_fwd_call(name + "_fwd", c_all, w, b_shard)
    (allc,), token = _gather_chips(name + "_ag", [shard])
    mods = jnp.transpose(allc, (1, 2, 0, 3)).reshape(nl, c_all.shape[0], 4 * ns)
    return lax.dynamic_slice(mods, (0, (2 * chip + lax.axis_index("c")) * bl, 0), (nl, bl, 4 * ns)), token


def _ada_bwd(name, bl, c_all, ns, dm):
    nl = dm.shape[0]
    chip = 2 * lax.axis_index("x") + lax.axis_index("y")
    (dm_all,), _ = _gather_two_level(name + "_bwd_ag", [dm])
    dm_all = jnp.transpose(dm_all.reshape((8,) + dm.shape), (1, 0, 2, 3)).reshape(nl, 8 * bl, 4 * ns)
    dm_shard = lax.dynamic_slice(dm_all, (0, 0, chip * ns), (nl, 8 * bl, ns))
    dw, db = _ada_bwd_call(name + "_bwd", c_all, dm_shard, dm_all)
    return dw, db.reshape(nl, 4 * ns)


def _adamw(name, w, g, m, v):
    shape = w.shape
    cols = shape[-1]
    w2, g2, m2, v2 = (t.reshape(-1, cols) for t in (w, g, m, v))
    rows = w2.shape[0]
    cap = max(SUBLANES, (512 * 1024) // max(cols, 1) // SUBLANES * SUBLANES)
    tr = _divisors(rows, cap, SUBLANES)[-1]

    def body(w_ref, g_ref, m_ref, v_ref, d_ref, mo_ref, vo_ref):
        gv = g_ref[...]
        mn = ADAM_B1 * m_ref[...] + (1.0 - ADAM_B1) * gv
        vn = ADAM_B2 * v_ref[...] + (1.0 - ADAM_B2) * (gv * gv)
        m_hat = mn / (1.0 - ADAM_B1 ** ADAM_STEP)
        v_hat = vn / (1.0 - ADAM_B2 ** ADAM_STEP)
        d_ref[...] = -ADAM_LR * (m_hat / (jnp.sqrt(v_hat) + ADAM_EPS) + ADAM_WD * w_ref[...])
        mo_ref[...] = mn
        vo_ref[...] = vn

    spec = pl.BlockSpec((tr, cols), lambda i: (i, 0))
    outs = _pcall(body, name, [_sds((rows, cols), F32)] * 3, grid=(rows // tr,), in_specs=[spec] * 4, out_specs=[spec] * 3, sem=("parallel",))(w2, g2, m2, v2)
    return tuple(o.reshape(shape) for o in outs)


def _adamw_small(name, ws, gs, ms, vs):
    n = len(ws)

    def body(*refs):
        for k in range(n):
            w_ref, g_ref, m_ref, v_ref = (refs[j * n + k] for j in range(4))
            d_ref, mo_ref, vo_ref = (refs[(4 + j) * n + k] for j in range(3))
            gv = g_ref[...]
            mn = ADAM_B1 * m_ref[...] + (1.0 - ADAM_B1) * gv
            vn = ADAM_B2 * v_ref[...] + (1.0 - ADAM_B2) * (gv * gv)
            m_hat = mn / (1.0 - ADAM_B1 ** ADAM_STEP)
            v_hat = vn / (1.0 - ADAM_B2 ** ADAM_STEP)
            d_ref[...] = -ADAM_LR * (m_hat / (jnp.sqrt(v_hat) + ADAM_EPS) + ADAM_WD * w_ref[...])
            mo_ref[...] = mn
            vo_ref[...] = vn

    outs = _pcall(body, name, [_sds(w.shape, F32) for w in ws] * 3)(*ws, *gs, *ms, *vs)
    return outs[:n], outs[n:2 * n], outs[2 * n:]


def _pad_rows(w, rows):
    return jnp.pad(w, ((0, rows - w.shape[0]), (0, 0)))


def _pad_lanes(v):
    return jnp.pad(v, (0, LANES - v.shape[0])).reshape(1, LANES)


BIG = ("w_in", "w_out", "ff_up", "ff_down")
BIG_AXIS = {"w_in": 1, "w_out": 0, "ff_up": 1, "ff_down": 0}
CONVW = ("ssd_conv_w", "ff_conv_w")
SMALL = ("norm1_g", "norm2_g", "ssd_conv_b", "ssd_dt_bias", "ssd_a_log", "ssd_d", "ssd_norm_g", "gm_vnorm_g", "gm_ws", "gm_bs", "gm_out_g", "ff_conv_b")
WEIGHTS = ("ada_w", "ada_b", "norm1_g", "norm2_g", "w_in", "ssd_conv_w", "ssd_conv_b", "ssd_dt_bias", "ssd_a_log", "ssd_d", "ssd_norm_g", "gm_vnorm_g", "gm_ws", "gm_bs", "gm_out_g", "w_out", "ff_up", "ff_conv_w", "ff_conv_b", "ff_down", "final_g")


def kernel(x, c, ada_w, ada_b, norm1_g, norm2_g, w_in, ssd_conv_w, ssd_conv_b, ssd_dt_bias, ssd_a_log, ssd_d, ssd_norm_g, gm_vnorm_g, gm_ws, gm_bs, gm_out_g, w_out, ff_up, ff_conv_w, ff_conv_b, ff_down, final_g, loss_target, m_ada_w, m_ada_b, m_norm1_g, m_norm2_g, m_w_in, m_ssd_conv_w, m_ssd_conv_b, m_ssd_dt_bias, m_ssd_a_log, m_ssd_d, m_ssd_norm_g, m_gm_vnorm_g, m_gm_ws, m_gm_bs, m_gm_out_g, m_w_out, m_ff_up, m_ff_conv_w, m_ff_conv_b, m_ff_down, m_final_g, v_ada_w, v_ada_b, v_norm1_g, v_norm2_g, v_w_in, v_ssd_conv_w, v_ssd_conv_b, v_ssd_dt_bias, v_ssd_a_log, v_ssd_d, v_ssd_norm_g, v_gm_vnorm_g, v_gm_ws, v_gm_bs, v_gm_out_g, v_w_out, v_ff_up, v_ff_conv_w, v_ff_conv_b, v_ff_down, v_final_g):
    given = dict(locals())
    weights = {n: given[n] for n in WEIGHTS}
    bl, s, d = x.shape
    nl = ada_w.shape[0]
    heads = d // HEAD_DIM
    cd = d + 2 * SSD_GROUPS * STATE
    f = ff_down.shape[1] * 4
    n_in = d + cd + heads + 2 * d
    n_proj = _round_up(3 * d + cd + LANES, 2 * LANES)
    tm = _divisors(s, 512)[-1]

    pre, _ = _gather_two_level("pre_ag", [c] + [weights[n] for n in CONVW])
    c_all = pre[0].reshape(8 * bl, d)
    conv_full = {n: jnp.concatenate([p[k, 0] for k in range(4)], axis=-1) for n, p in zip(CONVW, pre[1:])}
    mods_all, ada_token = _ada_fwd("ada", bl, c_all, ada_w, ada_b)
    mods = mods_all.reshape(nl, bl, N_MOD, 1, d)

    gathering = {}

    def start_gathers(l, token):
        for n in BIG:
            gathering[l, n] = _gather_weights_start(f"wg{l}_{n}", [weights[n][l]], token)
            token = gathering[l, n]["token"]
        return token

    def landed(l, n, after):
        (full,), token = _gather_weights_finish(f"wg{l}_{n}", gathering.pop((l, n)), after)
        return (full.reshape(-1, full.shape[2]) if BIG_AXIS[n] == 0 else jnp.concatenate([full[k] for k in range(4)], axis=1)), token

    start_gathers(0, ada_token)
    r2 = lambda v: v.reshape(1, -1)
    n_gm = d // GM_HEAD
    fg = r2(final_g)

    def layer_params(l):
        return dict(
            norm1=r2(norm1_g[l]), norm2=r2(norm2_g[l]), conv_w8=_pad_rows(conv_full["ssd_conv_w"][l], SUBLANES), conv_b=r2(ssd_conv_b[l]),
            dtb=_pad_lanes(ssd_dt_bias[l]), alog=_pad_lanes(ssd_a_log[l]), dsk=r2(jnp.repeat(ssd_d[l], HEAD_DIM)), ng=r2(ssd_norm_g[l]),
            vg=r2(gm_vnorm_g[l]), ws=gm_ws[l], bs_t=jnp.pad(gm_bs[l].T, ((0, 0), (0, LANES - n_gm))), og=r2(gm_out_g[l]),
            ff_w8=_pad_rows(conv_full["ff_conv_w"][l], SUBLANES), ff_b=r2(ff_conv_b[l]))

    def mixer_args(p):
        return (p["conv_w8"], p["conv_b"], p["dtb"], p["alog"], p["dsk"], p["ng"], p["vg"], p["ws"], p["bs_t"], p["og"])

    def padded_w_in(wi):
        return jnp.concatenate(
            [wi[:, :d], wi[:, d + cd + heads:], wi[:, d:d + cd], wi[:, d + cd:d + cd + heads], jnp.zeros((d, n_proj - (3 * d + cd + heads)), BF16)], axis=1)

    saved, xcur, pending = [], x, None
    for l in range(nl):
        p, w = layer_params(l), {}
        wi, token = landed(l, "w_in", mods_all if l == 0 else pending[0])
        w["w_in"] = padded_w_in(wi)
        sh1, sc1, g1, sh2, sc2, g2 = (mods[l, :, k] for k in range(N_MOD))
        if l + 1 < nl:
            sc1 = _after(sc1, start_gathers(l + 1, token))
        if pending is None:
            (h,) = _row_fwd(f"nm{l}_fwd", _nm_tile, [(xcur, d, 0, None)], [sc1, sh1], [p["norm1"]], [(d, BF16)], 0, tm)
            x1 = xcur
        else:
            rows = [(xcur, d, 0, None), (pending[0], d, 0, None)]
            x1, h = _row_fwd(f"rnm{l}a_fwd", _rnm_tile, rows, [pending[1], sc1, sh1], [p["norm1"]], [(d, F32), (d, BF16)], 0, tm)
        proj = _mm(f"win{l}_fwd", h, w["w_in"], F32)
        ycat, (xbc, hprev) = _mixer_fwd(f"mix{l}", proj, d, cd, *mixer_args(p))
        w["w_out"], _ = landed(l, "w_out", ycat)
        mix = _mm(f"wout{l}_fwd", ycat, w["w_out"], F32)
        x2, h2 = _row_fwd(f"rnm{l}b_fwd", _rnm_tile, [(x1, d, 0, None), (mix, d, 0, None)], [g1, sc2, sh2], [p["norm2"]], [(d, F32), (d, BF16)], 0, tm)
        w_up, _ = landed(l, "ff_up", h2)
        w["ff_up"] = _gate_value_blocks(w_up, f)
        up = _mm(f"ffup{l}_fwd", h2, w["ff_up"], F32)
        act = _conv_fwd(f"ffact{l}_fwd", up, 0, f, p["ff_w8"], p["ff_b"], FF_CONV, True, BF16)
        w["ff_down"], _ = landed(l, "ff_down", act)
        down = _mm(f"ffdown{l}_fwd", act, w["ff_down"], F32)
        saved.append(dict(p=p, w=w, x_in=xcur, pending=pending, h=h, proj=proj, xbc=xbc, hprev=hprev, ycat=ycat, x1=x1, mix=mix, h2=h2, up=up, act=act))
        xcur, pending = x2, (down, g2)
    rows = [(xcur, d, 0, F32), (pending[0], d, 0, BF16), (loss_target, d, 0, None)]
    (loss_local,) = _row_fwd("final_fwd", _final_tile, rows, [pending[1]], [fg], [], 1, tm)
    loss = lax.psum(loss_local[0, 0], AXES)

    (dx2, ddown), (dg2,), (dfg,) = _row_bwd("final_bwd", _final_tile, rows, [pending[1]], [fg], [], 1, tm, [jnp.ones((1, 1), F32)])
    dmods, small_grads = [None] * nl, [None] * nl
    reducing = {}

    def start_reduce(l, n, g):
        width = g.shape[1] // 4
        by_chip = g.reshape(4, g.shape[0] // 4, g.shape[1]) if BIG_AXIS[n] == 0 else jnp.stack([g[:, k * width:(k + 1) * width] for k in range(4)])
        reducing[l, n] = _reduce_weights_start(f"wg{l}_{n}", [by_chip], None)
        return reducing[l, n]["token"]

    for l in reversed(range(nl)):
        sv = saved[l]
        p, w = sv["p"], sv["w"]
        sh1, sc1, g1, sh2, sc2, g2 = (mods[l, :, k] for k in range(N_MOD))
        dact, dw_down = _mm_bwd(f"ffdown{l}", sv["act"], w["ff_down"], ddown)
        ff_b = _after(p["ff_b"], start_reduce(l, "ff_down", dw_down))
        dup, dff_w8, dff_b = _conv_bwd(f"ffact{l}_bwd", sv["up"], 0, f, p["ff_w8"], ff_b, FF_CONV, True, dact, BF16)
        dh2, dw_up = _mm_bwd(f"ffup{l}", sv["h2"], w["ff_up"], dup)
        sc2 = _after(sc2, start_reduce(l, "ff_up", _gate_value_columns(dw_up, f)))
        rows = [(sv["x1"], d, 0, F32), (sv["mix"], d, 0, BF16)]
        (dx1, dmix), (dg1, dsc2, dsh2), (dn2,) = _row_bwd(f"rnm{l}b_bwd", _rnm_tile, rows, [g1, sc2, sh2], [p["norm2"]], [(d, F32), (d, BF16)], 0, tm, [dx2, dh2])
        dycat, dw_out = _mm_bwd(f"wout{l}", sv["ycat"], w["w_out"], dmix)
        p_tied = dict(p, dtb=_after(p["dtb"], start_reduce(l, "w_out", dw_out)))
        dproj, (dw8, dcb, ddtb, dalog, ddsk, dng, dvg, dws, dbs_t, dog) = _mixer_bwd(f"mix{l}", sv["proj"], d, cd, *mixer_args(p_tied), sv["xbc"], sv["hprev"], dycat)
        dh, dw_in_p = _mm_bwd(f"win{l}", sv["h"], w["w_in"], dproj)
        dw_in = jnp.concatenate([dw_in_p[:, :d], dw_in_p[:, 3 * d:3 * d + cd], dw_in_p[:, 3 * d + cd:3 * d + cd + heads], dw_in_p[:, d:3 * d]], axis=1)
        sc1 = _after(sc1, start_reduce(l, "w_in", dw_in))
        if sv["pending"] is None:
            (dx2,), (dsc1, dsh1), (dn1,) = _row_bwd(f"nm{l}_bwd", _nm_res_tile, [(sv["x_in"], d, 0, F32)], [sc1, sh1], [p["norm1"]], [(d, F32), (d, BF16)], 0, tm, [dx1, dh])
        else:
            rows = [(sv["x_in"], d, 0, F32), (sv["pending"][0], d, 0, BF16)]
            (dx2, ddown), (dg2_prev, dsc1, dsh1), (dn1,) = _row_bwd(
                f"rnm{l}a_bwd", _rnm_tile, rows, [sv["pending"][1], sc1, sh1], [p["norm1"]], [(d, F32), (d, BF16)], 0, tm, [dx1, dh])
        dmods[l] = jnp.concatenate([dsh1, dsc1, dg1, dsh2, dsc2, dg2], axis=1).reshape(bl, N_MOD * d)
        if sv["pending"] is not None:
            dg2 = dg2_prev
        small_grads[l] = dict(
            norm1_g=dn1.reshape(d), norm2_g=dn2.reshape(d), ssd_conv_b=dcb.reshape(cd), ssd_dt_bias=ddtb[0, :heads], ssd_a_log=dalog[0, :heads],
            ssd_d=ddsk.reshape(heads, HEAD_DIM).sum(-1), ssd_norm_g=dng.reshape(d), gm_vnorm_g=dvg.reshape(d), gm_ws=dws, gm_bs=dbs_t[:, :n_gm].T,
            gm_out_g=dog.reshape(d), ff_conv_b=dff_b.reshape(f), ssd_conv_w=dw8[:SSD_CONV], ff_conv_w=dff_w8[:FF_CONV])
    grad_x = dx2

    g_ada_w, g_ada_b = _ada_bwd("ada", bl, c_all, ada_w.shape[2], jnp.stack(dmods))
    small_names_r = SMALL + CONVW + ("final_g",)
    summed = _allreduce_small("small", [jnp.stack([small_grads[l][n] for l in range(nl)]) for n in SMALL + CONVW] + [dfg])
    grad = {"ada_w": g_ada_w, "ada_b": g_ada_b}
    big_grads = {}
    for l in reversed(range(nl)):
        for n in reversed(BIG):
            (big_grads[l, n],), _ = _reduce_weights_finish(f"wg{l}_{n}", reducing.pop((l, n)), summed[0])
    for n in BIG:
        grad[n] = jnp.stack([big_grads[l, n] for l in range(nl)])
    grad.update(zip(small_names_r, summed))
    chip = 2 * lax.axis_index("x") + lax.axis_index("y")
    for n in CONVW:
        width = weights[n].shape[-1]
        grad[n] = lax.dynamic_slice_in_dim(grad[n], chip * width, width, axis=2)

    delta, new_m, new_v = {}, {}, {}
    for n in ("ada_w",) + BIG:
        delta[n], new_m[n], new_v[n] = _adamw("adam_" + n, weights[n], grad[n], given["m_" + n], given["v_" + n])
    small_names = ("ada_b",) + SMALL + CONVW + ("final_g",)
    as2d = lambda t: t.reshape(1, -1) if t.ndim == 1 else t
    res = _adamw_small(
        "adam_small", [as2d(weights[n]) for n in small_names], [as2d(grad[n]) for n in small_names],
        [as2d(given["m_" + n]) for n in small_names], [as2d(given["v_" + n]) for n in small_names])
    for out, vals in zip((delta, new_m, new_v), res):
        for n, val in zip(small_names, vals):
            out[n] = val.reshape(weights[n].shape)
    grad["final_g"] = grad["final_g"].reshape(final_g.shape)

    return (loss, grad_x, *[grad[n] for n in WEIGHTS], *[delta[n] for n in WEIGHTS], *[new_m[n] for n in WEIGHTS], *[new_v[n] for n in WEIGHTS])
```

```python
import functools

import jax
import jax.numpy as jnp
from jax import lax
from jax.experimental import pallas as pl
from jax.experimental.pallas import tpu as pltpu

F32 = jnp.float32
BF16 = jnp.bfloat16
EPS = 1e-6
CHUNK = 128
HEAD_DIM = 64
STATE = 128
GM_HEAD = 128
SSD_GROUPS = 2
SSD_CONV = 4
FF_CONV = 3
N_MOD = 6
LANES = 128
SUBLANES = 8
CONV_LANES = 256
CONV_ROWS = 32
V7X_VMEM_LIMIT = 48 * 1024 * 1024
MM_VMEM_BUDGET = 30 * 1024 * 1024
MM_STEP_MACS = 512 * 1024 * 2560
V7X_MXU_FLOPS = 996e12
V7X_HBM_BYTES_PER_S = 3.3e12
V7X_STEP_SECONDS = 0.35e-6
ADAM_LR, ADAM_B1, ADAM_B2, ADAM_EPS, ADAM_WD, ADAM_STEP = 0.001, 0.9, 0.999, 1e-08, 0.01, 10
MESH = pl.DeviceIdType.MESH
AXES = ("x", "y", "c")


def _round_up(n, m):
    return (n + m - 1) // m * m


def _divisors(n, cap, mult=LANES):
    out = [t for t in range(mult, min(n, cap) + 1, mult) if n % t == 0]
    return out or [n]


def _pcall(body, name, out_shape, grid=(), in_specs=None, out_specs=None, scratch=(), sem=None, prefetch=0):
    params = pltpu.CompilerParams(dimension_semantics=sem, vmem_limit_bytes=V7X_VMEM_LIMIT)
    if prefetch:
        spec = pltpu.PrefetchScalarGridSpec(num_scalar_prefetch=prefetch, grid=grid, in_specs=in_specs, out_specs=out_specs, scratch_shapes=list(scratch))
        return pl.pallas_call(body, name=name, out_shape=out_shape, grid_spec=spec, compiler_params=params)
    if in_specs is None:
        return pl.pallas_call(body, name=name, out_shape=out_shape, compiler_params=params)
    return pl.pallas_call(body, name=name, out_shape=out_shape, grid=grid, in_specs=in_specs, out_specs=out_specs, scratch_shapes=list(scratch), compiler_params=params)


def _sds(shape, dtype):
    return jax.ShapeDtypeStruct(tuple(shape), dtype)


def _mm_tiles(m, n, k, a_bytes, b_bytes, o_bytes):
    best, best_key = None, None
    for tm in _divisors(m, 2048):
        for tn in _divisors(n, 2560):
            for tk in _divisors(k, 2560):
                vmem = 2 * (tm * tk * a_bytes + tk * tn * b_bytes + tm * tn * o_bytes) + tm * tn * 4
                if vmem > MM_VMEM_BUDGET or tm * tn * tk > MM_STEP_MACS:
                    continue
                ni, nj, nk = m // tm, n // tn, k // tk
                a_reads = 1 if nk == 1 else nj
                b_reads = 1 if (nk == 1 and nj == 1) else ni
                hbm = a_reads * m * k * a_bytes + b_reads * k * n * b_bytes + m * n * o_bytes
                t = max(2.0 * m * n * k / V7X_MXU_FLOPS, hbm / V7X_HBM_BYTES_PER_S) + ni * nj * nk * V7X_STEP_SECONDS
                key = (-t, tm * tn * tk)
                if best_key is None or key > best_key:
                    best, best_key = (tm, tn, tk), key
    return best


def _matmul(name, a, b, mode, out_dtype):
    if mode == "nn":
        (m, k), n = a.shape, b.shape[1]
    elif mode == "nt":
        (m, k), n = a.shape, b.shape[0]
    else:
        (k, m), n = a.shape, b.shape[1]
    tm, tn, tk = _mm_tiles(m, n, k, a.dtype.itemsize, b.dtype.itemsize, jnp.dtype(out_dtype).itemsize)
    nk = k // tk
    if mode == "nn":
        a_spec = pl.BlockSpec((tm, tk), lambda i, j, kk: (i, kk))
        b_spec = pl.BlockSpec((tk, tn), lambda i, j, kk: (kk, j))
        dims = ((1,), (0,))
    elif mode == "nt":
        a_spec = pl.BlockSpec((tm, tk), lambda i, j, kk: (i, kk))
        b_spec = pl.BlockSpec((tn, tk), lambda i, j, kk: (j, kk))
        dims = ((1,), (1,))
    else:
        a_spec = pl.BlockSpec((tk, tm), lambda i, j, kk: (kk, i))
        b_spec = pl.BlockSpec((tk, tn), lambda i, j, kk: (kk, j))
        dims = ((0,), (0,))

    def body(a_ref, b_ref, o_ref, acc_ref):
        kk = pl.program_id(2)
        p = lax.dot_general(a_ref[...].astype(BF16), b_ref[...].astype(BF16), (dims, ((), ())), preferred_element_type=F32)
        if nk == 1:
            o_ref[...] = p.astype(o_ref.dtype)
        else:
            @pl.when(kk == 0)
            def _():
                acc_ref[...] = p

            @pl.when(kk > 0)
            def _():
                acc_ref[...] += p

            @pl.when(kk == nk - 1)
            def _():
                o_ref[...] = acc_ref[...].astype(o_ref.dtype)

    return _pcall(
        body, name, _sds((m, n), out_dtype), grid=(m // tm, n // tn, nk), in_specs=[a_spec, b_spec],
        out_specs=pl.BlockSpec((tm, tn), lambda i, j, kk: (i, j)), scratch=[pltpu.VMEM((tm, tn), F32)],
        sem=("parallel", "parallel", "arbitrary"),
    )(a, b)


def _mm(name, a, w, out_dtype):
    return _matmul(name, a.reshape(-1, a.shape[-1]), w, "nn", out_dtype).reshape(a.shape[:-1] + (w.shape[1],))


def _mm_bwd(name, a, w, dy):
    a2, dy2 = a.reshape(-1, a.shape[-1]), dy.reshape(-1, dy.shape[-1])
    return _matmul(name + "_dx", dy2, w, "nt", BF16).reshape(a.shape), _matmul(name + "_dw", a2, dy2, "tn", BF16)


def _dg(a, b, dims):
    return lax.dot_general(a.astype(BF16), b.astype(BF16), (dims, ((), ())), preferred_element_type=F32)


@jax.custom_vjp
def _dot_nn(a, b):
    return _dg(a, b, ((1,), (0,)))


_dot_nn.defvjp(lambda a, b: (_dot_nn(a, b), (a, b)), lambda r, d: (_dg(d, r[1], ((1,), (1,))), _dg(r[0], d, ((0,), (0,)))))


@jax.custom_vjp
def _dot_nt(a, b):
    return _dg(a, b, ((1,), (1,)))


_dot_nt.defvjp(lambda a, b: (_dot_nt(a, b), (a, b)), lambda r, d: (_dg(d, r[1], ((1,), (0,))), _dg(d, r[0], ((0,), (0,)))))


@jax.custom_vjp
def _dot_tn(a, b):
    return _dg(a, b, ((0,), (0,)))


_dot_tn.defvjp(lambda a, b: (_dot_tn(a, b), (a, b)), lambda r, d: (_dg(r[1], d, ((1,), (1,))), _dg(r[0], d, ((1,), (0,)))))


def _exact_dot(a, c, dims):
    hi = a.astype(BF16)
    r1 = a - hi.astype(F32)
    mid = r1.astype(BF16)
    lo = (r1 - mid.astype(F32)).astype(BF16)
    cb = c.astype(BF16)
    f = lambda t: lax.dot_general(t, cb, (dims, ((), ())), preferred_element_type=F32)
    return f(hi) + f(mid) + f(lo)


@jax.custom_vjp
def _sel_right(a, c):
    return _exact_dot(a, c, ((1,), (0,)))


_sel_right.defvjp(lambda a, c: (_sel_right(a, c), c), lambda c, d: (_exact_dot(d, c, ((1,), (1,))), jnp.zeros_like(c)))


def _exact_dot_left(c, a, dims):
    hi = a.astype(BF16)
    r1 = a - hi.astype(F32)
    mid = r1.astype(BF16)
    lo = (r1 - mid.astype(F32)).astype(BF16)
    cb = c.astype(BF16)
    f = lambda t: lax.dot_general(cb, t, (dims, ((), ())), preferred_element_type=F32)
    return f(hi) + f(mid) + f(lo)


@jax.custom_vjp
def _sel_left(c, a):
    return _exact_dot_left(c, a, ((1,), (0,)))


_sel_left.defvjp(lambda c, a: (_sel_left(c, a), c), lambda c, d: (jnp.zeros_like(c), _exact_dot_left(c, d, ((0,), (0,)))))


def _sigmoid(x):
    return 1.0 / (1.0 + jnp.exp(-x))


def _rms(x, g):
    return x * lax.rsqrt(jnp.mean(x * x, axis=-1, keepdims=True) + EPS) * g


def _gelu(x):
    return 0.5 * x * (1.0 + lax.erf(x * (2.0 ** -0.5)))


def _causal(n):
    return lax.broadcasted_iota(jnp.int32, (n, n), 0) >= lax.broadcasted_iota(jnp.int32, (n, n), 1)


def _row_in_specs(rows, bparams, gparams, tm):
    specs = [pl.BlockSpec((1, tm, w), lambda b, i, cb=cb: (b, i, cb)) for (_, w, cb, _) in rows]
    specs += [pl.BlockSpec((1, 1, p.shape[-1]), lambda b, i: (b, 0, 0)) for p in bparams]
    specs += [pl.BlockSpec(p.shape, lambda b, i, n=p.ndim: (0,) * n) for p in gparams]
    return specs


def _row_vals(refs, n_rows, n_b, n_g):
    vals = [r[0].astype(F32) for r in refs[:n_rows]]
    vals += [r[0].astype(F32) for r in refs[n_rows:n_rows + n_b]]
    vals += [r[...].astype(F32) for r in refs[n_rows + n_b:n_rows + n_b + n_g]]
    return vals


def _row_fwd(name, tile, rows, bparams, gparams, outs, n_sum, tm):
    bl, s = rows[0][0].shape[:2]
    n_in = len(rows) + len(bparams) + len(gparams)

    def body(*refs):
        first = (pl.program_id(0) == 0) & (pl.program_id(1) == 0)
        res = tile(*_row_vals(refs, len(rows), len(bparams), len(gparams)))
        o_refs = refs[n_in:]
        for k in range(len(outs)):
            o_refs[k][0] = res[k].astype(o_refs[k].dtype)
        for k in range(n_sum):
            o_ref, val = o_refs[len(outs) + k], res[len(outs) + k]

            @pl.when(first)
            def _(o_ref=o_ref, val=val):
                o_ref[...] = val

            @pl.when(jnp.logical_not(first))
            def _(o_ref=o_ref, val=val):
                o_ref[...] += val

    out_shape = [_sds((bl, s, w), dt) for (w, dt) in outs] + [_sds((1, 1), F32)] * n_sum
    out_specs = [pl.BlockSpec((1, tm, w), lambda b, i: (b, i, 0)) for (w, _) in outs] + [pl.BlockSpec((1, 1), lambda b, i: (0, 0))] * n_sum
    return _pcall(
        body, name, out_shape, grid=(bl, s // tm), in_specs=_row_in_specs(rows, bparams, gparams, tm), out_specs=out_specs,
        sem=("arbitrary", "arbitrary"),
    )(*[r[0] for r in rows], *bparams, *gparams)


def _row_bwd(name, tile, rows, bparams, gparams, outs, n_sum, tm, cts):
    bl, s = rows[0][0].shape[:2]
    n_r, n_b, n_g = len(rows), len(bparams), len(gparams)
    n_in = n_r + n_b + n_g
    n_ct = len(outs) + n_sum
    grad_rows = [k for k in range(n_r) if rows[k][3]]

    def body(*refs):
        b, i = pl.program_id(0), pl.program_id(1)
        vals = _row_vals(refs, n_r, n_b, n_g)
        ct_refs = refs[n_in:n_in + n_ct]
        ct = [r[0].astype(F32) for r in ct_refs[:len(outs)]] + [r[...] for r in ct_refs[len(outs):]]
        _, vjp = jax.vjp(tile, *vals)
        grads = vjp(tuple(ct))
        o_refs = refs[n_in + n_ct:]
        for j, k in enumerate(grad_rows):
            o_refs[j][0] = grads[k].astype(o_refs[j].dtype)
        for k in range(n_b):
            o_ref, val = o_refs[len(grad_rows) + k], grads[n_r + k]

            @pl.when(i == 0)
            def _(o_ref=o_ref, val=val):
                o_ref[0] = val

            @pl.when(i > 0)
            def _(o_ref=o_ref, val=val):
                o_ref[0] += val

        first = (b == 0) & (i == 0)
        for k in range(n_g):
            o_ref, val = o_refs[len(grad_rows) + n_b + k], grads[n_r + n_b + k]

            @pl.when(first)
            def _(o_ref=o_ref, val=val):
                o_ref[...] = val

            @pl.when(jnp.logical_not(first))
            def _(o_ref=o_ref, val=val):
                o_ref[...] += val

    in_specs = _row_in_specs(rows, bparams, gparams, tm)
    in_specs += [pl.BlockSpec((1, tm, w), lambda b, i: (b, i, 0)) for (w, _) in outs] + [pl.BlockSpec((1, 1), lambda b, i: (0, 0))] * n_sum
    out_shape = [_sds((bl, s, rows[k][1]), rows[k][3]) for k in grad_rows]
    out_shape += [_sds(p.shape, F32) for p in bparams] + [_sds(p.shape, F32) for p in gparams]
    out_specs = [pl.BlockSpec((1, tm, rows[k][1]), lambda b, i: (b, i, 0)) for k in grad_rows]
    out_specs += [pl.BlockSpec((1, 1, p.shape[-1]), lambda b, i: (b, 0, 0)) for p in bparams]
    out_specs += [pl.BlockSpec(p.shape, lambda b, i, n=p.ndim: (0,) * n) for p in gparams]
    res = _pcall(
        body, name, out_shape, grid=(bl, s // tm), in_specs=in_specs, out_specs=out_specs, sem=("arbitrary", "arbitrary"),
    )(*[r[0] for r in rows], *bparams, *gparams, *cts)
    return res[:len(grad_rows)], res[len(grad_rows):len(grad_rows) + n_b], res[len(grad_rows) + n_b:]


def _nm_tile(x, sc, sh, g):
    return (_rms(x, g) * (1.0 + sc) + sh,)


def _nm_res_tile(x, sc, sh, g):
    return x, _rms(x, g) * (1.0 + sc) + sh


def _rnm_tile(x, o, gate, sc, sh, g):
    xn = x + gate * o
    return xn, _rms(xn, g) * (1.0 + sc) + sh


def _final_tile(x, o, tgt, gate, g):
    e = _rms(x + gate * o, g) - tgt
    return (0.5 * jnp.sum(jnp.mean(e * e, axis=-1, keepdims=True), axis=0, keepdims=True),)


def _gm_tile(u_in, v_in, vg, ws, bs_t, og):
    d = u_in.shape[1]
    u, vn = _gelu(u_in), _rms(_gelu(v_in), vg)
    causal = _causal(CHUNK)
    lane = lax.broadcasted_iota(jnp.int32, (1, LANES), 1)
    parts = []
    for h in range(d // GM_HEAD):
        bias = jnp.sum(bs_t * (lane == h).astype(F32), axis=1, keepdims=True)
        parts.append(_dot_nn(jnp.where(causal, ws[h], 0.0), vn[:, h * GM_HEAD:(h + 1) * GM_HEAD]) + bias)
    return (_rms(u * jnp.concatenate(parts, axis=1), og),)


def _conv_window(ref, r0, rows, c0, tc, seq, before, after):
    parts = []
    if before:
        p0 = pl.multiple_of(jnp.maximum(r0 - SUBLANES, 0), SUBLANES)
        parts.append(jnp.where(r0 > 0, ref[0, pl.ds(p0, SUBLANES), pl.ds(c0, tc)].astype(F32), 0.0))
    parts.append(ref[0, pl.ds(r0, rows), pl.ds(c0, tc)].astype(F32))
    if after:
        n0 = pl.multiple_of(jnp.minimum(r0 + rows, seq - SUBLANES), SUBLANES)
        parts.append(jnp.where(r0 + rows < seq, ref[0, pl.ds(n0, SUBLANES), pl.ds(c0, tc)].astype(F32), 0.0))
    return jnp.concatenate(parts, axis=0) if len(parts) > 1 else parts[0]


def _conv_taps_pre(xe, w, b, taps, rows):
    pre = xe[SUBLANES:SUBLANES + rows] * w[taps - 1] + b
    for j in range(1, taps):
        pre = pre + pltpu.roll(xe, j, 0)[SUBLANES:SUBLANES + rows] * w[taps - 1 - j]
    return pre


def _conv_fwd(name, src, col0, chans, w8, b, taps, gated, out_dtype):
    bl, s, _ = src.shape
    tc = CONV_LANES
    xw = 2 * tc if gated else tc

    def body(x_ref, w_ref, b_ref, o_ref):
        w = [w_ref[k:k + 1, :] for k in range(taps)]
        bias = b_ref[...]

        def step(c, carry):
            r0 = pl.multiple_of(c * CONV_ROWS, CONV_ROWS)
            xe = _conv_window(x_ref, r0, CONV_ROWS, 0, tc, s, True, False)
            pre = _conv_taps_pre(xe, w, bias, taps, CONV_ROWS)
            y = pre * _sigmoid(pre)
            if gated:
                y = y * x_ref[0, pl.ds(r0, CONV_ROWS), pl.ds(tc, tc)]
            o_ref[0, pl.ds(r0, CONV_ROWS), :] = y.astype(o_ref.dtype)
            return carry

        lax.fori_loop(0, s // CONV_ROWS, step, 0, unroll=4)

    first = 0 if gated else col0 // tc
    return _pcall(
        body, name, _sds((bl, s, chans), out_dtype), grid=(chans // tc, bl),
        in_specs=[pl.BlockSpec((1, s, xw), lambda j, bb: (bb, 0, first + j)), pl.BlockSpec((SUBLANES, tc), lambda j, bb: (0, j)), pl.BlockSpec((1, tc), lambda j, bb: (0, j))],
        out_specs=pl.BlockSpec((1, s, tc), lambda j, bb: (bb, 0, j)), sem=("parallel", "arbitrary"),
    )(src, w8, b)


def _conv_bwd(name, src, col0, chans, w8, b, taps, gated, dy, dx_dtype):
    bl, s, _ = src.shape
    tc = CONV_LANES
    xw = 2 * tc if gated else tc
    ext = CONV_ROWS + SUBLANES

    def body(x_ref, w_ref, b_ref, dy_ref, dx_ref, dw_ref, db_ref):
        bb = pl.program_id(1)
        w = [w_ref[k:k + 1, :] for k in range(taps)]
        bias = b_ref[...]

        def fold(v):
            acc = v[0:SUBLANES]
            for i in range(1, CONV_ROWS // SUBLANES):
                acc = acc + v[i * SUBLANES:(i + 1) * SUBLANES]
            return acc

        def step(c, carry):
            r0 = pl.multiple_of(c * CONV_ROWS, CONV_ROWS)
            xe = _conv_window(x_ref, r0, CONV_ROWS, 0, tc, s, True, True)
            pre = _conv_taps_pre(xe, w, bias, taps, ext)
            sig = _sigmoid(pre)
            d = _conv_window(dy_ref, r0, CONV_ROWS, 0, tc, s, False, True)
            dsil = d * _conv_window(x_ref, r0, CONV_ROWS, tc, tc, s, False, True) if gated else d
            dpre = dsil * sig * (1.0 + pre * (1.0 - sig))
            dx = dpre[:CONV_ROWS] * w[taps - 1]
            for j in range(1, taps):
                dx = dx + pltpu.roll(dpre, ext - j, 0)[:CONV_ROWS] * w[taps - 1 - j]
            dx_ref[0, pl.ds(r0, CONV_ROWS), pl.ds(0, tc)] = dx.astype(dx_ref.dtype)
            if gated:
                dx_ref[0, pl.ds(r0, CONV_ROWS), pl.ds(tc, tc)] = (d[:CONV_ROWS] * (pre * sig)[:CONV_ROWS]).astype(dx_ref.dtype)
            here = dpre[:CONV_ROWS]
            sums = [fold(here * (pltpu.roll(xe, taps - 1 - k, 0) if k < taps - 1 else xe)[SUBLANES:ext]) + carry[k] for k in range(taps)]
            return tuple(sums) + (fold(here) + carry[taps],)

        zero = jnp.zeros((SUBLANES, tc), F32)
        sums = lax.fori_loop(0, s // CONV_ROWS, step, (zero,) * (taps + 1), unroll=2)
        rows = [jnp.sum(t, axis=0, keepdims=True) for t in sums]
        dw = jnp.concatenate(rows[:taps] + [jnp.zeros_like(rows[0])] * (SUBLANES - taps), axis=0)

        @pl.when(bb == 0)
        def _():
            dw_ref[...] = dw
            db_ref[...] = rows[taps]

        @pl.when(bb > 0)
        def _():
            dw_ref[...] += dw
            db_ref[...] += rows[taps]

    first = 0 if gated else col0 // tc
    out_shape = [_sds((bl, s, chans * (2 if gated else 1)), dx_dtype), _sds((SUBLANES, chans), F32), _sds((1, chans), F32)]
    return _pcall(
        body, name, out_shape, grid=(chans // tc, bl),
        in_specs=[
            pl.BlockSpec((1, s, xw), lambda j, bb: (bb, 0, first + j)), pl.BlockSpec((SUBLANES, tc), lambda j, bb: (0, j)),
            pl.BlockSpec((1, tc), lambda j, bb: (0, j)), pl.BlockSpec((1, s, tc), lambda j, bb: (bb, 0, j)),
        ],
        out_specs=[pl.BlockSpec((1, s, xw), lambda j, bb: (bb, 0, j)), pl.BlockSpec((SUBLANES, tc), lambda j, bb: (0, j)), pl.BlockSpec((1, tc), lambda j, bb: (0, j))],
        sem=("parallel", "arbitrary"),
    )(src, w8, b, dy)


def _gate_value_blocks(name, w, f, inverse=False):
    nb = f // CONV_LANES
    src = (lambda j: (0, 2 * (j % nb) + j // nb)) if inverse else (lambda j: (0, (j % 2) * nb + j // 2))

    def body(x_ref, o_ref):
        o_ref[...] = x_ref[...]

    return _pcall(
        body, name, _sds(w.shape, w.dtype), grid=(2 * nb,), in_specs=[pl.BlockSpec((w.shape[0], CONV_LANES), src)],
        out_specs=pl.BlockSpec((w.shape[0], CONV_LANES), lambda j: (0, j)), sem=("parallel",),
    )(w)


def _ssd_chunk(g_idx, states, xs, bm, cm, dtr, z, dtb, alog, dsk, ng):
    gw = xs.shape[1]
    hpg = gw // HEAD_DIM
    dt = jax.nn.softplus(dtr + dtb)
    da = dt * (-jnp.exp(alog))
    causal = _causal(CHUNK)
    acs = _sel_left(causal.astype(F32), da)
    head_of_col = g_idx * hpg + lax.broadcasted_iota(jnp.int32, (LANES, gw), 1) // HEAD_DIM
    expand = (lax.broadcasted_iota(jnp.int32, (LANES, gw), 0) == head_of_col).astype(F32)
    dt_e, acs_e = _sel_right(dt, expand), _sel_right(acs, expand)
    last = lax.broadcasted_iota(jnp.int32, (CHUNK, gw), 0) == CHUNK - 1
    alast_e = jnp.sum(jnp.where(last, acs_e, 0.0), axis=0, keepdims=True)
    xc = xs * dt_e
    xc_st = xc * jnp.exp(alast_e - acs_e)
    decay_out, chunk_decay = jnp.exp(acs_e), jnp.exp(alast_e)
    cb = _dot_nt(cm, bm)
    acs_t = acs.T
    lane = lax.broadcasted_iota(jnp.int32, (1, LANES), 1)
    sub = lax.broadcasted_iota(jnp.int32, (LANES, 1), 0)
    ys, new_states = [], []
    for p in range(gw // LANES):
        sl = slice(p * LANES, (p + 1) * LANES)
        xcp = xc[:, sl]
        y = _dot_nn(cm, states[p]) * decay_out[:, sl]
        for q in range(2):
            head = g_idx * hpg + 2 * p + q
            col = jnp.sum(acs * (lane == head).astype(F32), axis=1, keepdims=True)
            row = jnp.sum(acs_t * (sub == head).astype(F32), axis=0, keepdims=True)
            decay = jnp.where(causal, jnp.exp(jnp.where(causal, col - row, 0.0)), 0.0)
            half = ((lane // HEAD_DIM) == q).astype(F32)
            y = y + _dot_nn(cb * decay, xcp * half)
        ys.append(y)
        new_states.append(states[p] * chunk_decay[:, sl] + _dot_tn(bm, xc_st[:, sl]))
    y = jnp.concatenate(ys, axis=1) + dsk * xs
    gated = y * (z * _sigmoid(z))
    return tuple(new_states), _rms(gated, ng)


def _ssd_specs(d, gw, dt_col, rev, nc):
    ci = (lambda i: nc - 1 - i) if rev else (lambda i: i)
    return [
        pl.BlockSpec((1, CHUNK, gw), lambda g, b, i: (b, ci(i), g)),
        pl.BlockSpec((1, CHUNK, STATE), lambda g, b, i: (b, ci(i), d // STATE + g)),
        pl.BlockSpec((1, CHUNK, STATE), lambda g, b, i: (b, ci(i), d // STATE + SSD_GROUPS + g)),
        pl.BlockSpec((1, CHUNK, LANES), lambda g, b, i: (b, ci(i), dt_col // LANES)),
        pl.BlockSpec((1, CHUNK, gw), lambda g, b, i: (b, ci(i), g)),
        pl.BlockSpec((1, LANES), lambda g, b, i: (0, 0)), pl.BlockSpec((1, LANES), lambda g, b, i: (0, 0)),
        pl.BlockSpec((1, gw), lambda g, b, i: (0, g)), pl.BlockSpec((1, gw), lambda g, b, i: (0, g)),
    ]


def _ssd_fwd(name, xbc, proj, dt_col, dtb, alog, dsk, ng):
    bl, s, cd = xbc.shape
    d = dsk.shape[1]
    gw, nc = d // SSD_GROUPS, s // CHUNK
    npair = gw // LANES

    def body(xs_ref, bm_ref, cm_ref, dt_ref, z_ref, dtb_ref, alog_ref, dsk_ref, ng_ref, y_ref, hp_ref, st_ref):
        g, i = pl.program_id(0), pl.program_id(2)

        @pl.when(i == 0)
        def _():
            st_ref[...] = jnp.zeros_like(st_ref)

        states = tuple(st_ref[p] for p in range(npair))
        hp_ref[0, 0, 0] = st_ref[...]
        new_states, yn = _ssd_chunk(g, states, xs_ref[0], bm_ref[0], cm_ref[0], dt_ref[0], z_ref[0], dtb_ref[...], alog_ref[...], dsk_ref[...], ng_ref[...])
        for p in range(npair):
            st_ref[p] = new_states[p]
        y_ref[0] = yn.astype(y_ref.dtype)

    out_shape = [_sds((bl, s, d), BF16), _sds((SSD_GROUPS, bl, nc, npair, STATE, LANES), F32)]
    out_specs = [
        pl.BlockSpec((1, CHUNK, gw), lambda g, b, i: (b, i, g)),
        pl.BlockSpec((1, 1, 1, npair, STATE, LANES), lambda g, b, i: (g, b, i, 0, 0, 0)),
    ]
    return _pcall(
        body, name, out_shape, grid=(SSD_GROUPS, bl, nc), in_specs=_ssd_specs(d, gw, dt_col, False, nc), out_specs=out_specs,
        scratch=[pltpu.VMEM((npair, STATE, LANES), F32)], sem=("arbitrary", "arbitrary", "arbitrary"),
    )(xbc, xbc, xbc, proj, proj, dtb, alog, dsk, ng)


def _ssd_bwd(name, xbc, proj, dt_col, dtb, alog, dsk, ng, hprev, dy):
    bl, s, cd = xbc.shape
    d = dsk.shape[1]
    gw, nc = d // SSD_GROUPS, s // CHUNK
    npair = gw // LANES

    def body(xs_ref, bm_ref, cm_ref, dt_ref, z_ref, dtb_ref, alog_ref, dsk_ref, ng_ref, hp_ref, dy_ref,
             dxs_ref, dbm_ref, dcm_ref, ddt_ref, dz_ref, ddtb_ref, dalog_ref, ddsk_ref, dng_ref, dst_ref):
        g, b, i = pl.program_id(0), pl.program_id(1), pl.program_id(2)

        @pl.when(i == 0)
        def _():
            dst_ref[...] = jnp.zeros_like(dst_ref)

        states = tuple(hp_ref[0, 0, 0, p] for p in range(npair))
        step = functools.partial(_ssd_chunk, g)
        _, vjp = jax.vjp(step, states, xs_ref[0], bm_ref[0], cm_ref[0], dt_ref[0], z_ref[0], dtb_ref[...], alog_ref[...], dsk_ref[...], ng_ref[...])
        d_states, dxs, dbm, dcm, ddt, dz, ddtb, dalog, ddsk, dng = vjp((tuple(dst_ref[p] for p in range(npair)), dy_ref[0].astype(F32)))
        for p in range(npair):
            dst_ref[p] = d_states[p]
        dxs_ref[0], dbm_ref[0], dcm_ref[0] = dxs, dbm, dcm
        dz_ref[0] = dz.astype(dz_ref.dtype)
        ddt_ref[0, 0] = ddt
        first_g = (b == 0) & (i == 0)
        first = first_g & (g == 0)
        for o_ref, val, init in ((ddtb_ref, ddtb, first), (dalog_ref, dalog, first), (ddsk_ref, ddsk, first_g), (dng_ref, dng, first_g)):
            @pl.when(init)
            def _(o_ref=o_ref, val=val):
                o_ref[...] = val

            @pl.when(jnp.logical_not(init))
            def _(o_ref=o_ref, val=val):
                o_ref[...] += val

    rc = lambda i: nc - 1 - i
    in_specs = _ssd_specs(d, gw, dt_col, True, nc) + [
        pl.BlockSpec((1, 1, 1, npair, STATE, LANES), lambda g, b, i: (g, b, rc(i), 0, 0, 0)),
        pl.BlockSpec((1, CHUNK, gw), lambda g, b, i: (b, rc(i), g)),
    ]
    out_shape = [
        _sds((bl, s, d), F32), _sds((bl, s, SSD_GROUPS * STATE), F32), _sds((bl, s, SSD_GROUPS * STATE), F32),
        _sds((SSD_GROUPS, bl, s, LANES), F32), _sds((bl, s, d), BF16),
        _sds((1, LANES), F32), _sds((1, LANES), F32), _sds((1, d), F32), _sds((1, d), F32),
    ]
    out_specs = [
        pl.BlockSpec((1, CHUNK, gw), lambda g, b, i: (b, rc(i), g)),
        pl.BlockSpec((1, CHUNK, STATE), lambda g, b, i: (b, rc(i), g)), pl.BlockSpec((1, CHUNK, STATE), lambda g, b, i: (b, rc(i), g)),
        pl.BlockSpec((1, 1, CHUNK, LANES), lambda g, b, i: (g, b, rc(i), 0)),
        pl.BlockSpec((1, CHUNK, gw), lambda g, b, i: (b, rc(i), g)),
        pl.BlockSpec((1, LANES), lambda g, b, i: (0, 0)), pl.BlockSpec((1, LANES), lambda g, b, i: (0, 0)),
        pl.BlockSpec((1, gw), lambda g, b, i: (0, g)), pl.BlockSpec((1, gw), lambda g, b, i: (0, g)),
    ]
    return _pcall(
        body, name, out_shape, grid=(SSD_GROUPS, bl, nc), in_specs=in_specs, out_specs=out_specs,
        scratch=[pltpu.VMEM((npair, STATE, LANES), F32)], sem=("arbitrary", "arbitrary", "arbitrary"),
    )(xbc, xbc, xbc, proj, proj, dtb, alog, dsk, ng, hprev, dy)


def _mixer_fwd(name, proj, d, cd, conv_w8, conv_b, dtb, alog, dsk, ng, vg, ws, bs_t, og):
    xbc = _conv_fwd(name + "_conv_fwd", proj, 3 * d, cd, conv_w8, conv_b, SSD_CONV, False, F32)
    y, hprev = _ssd_fwd(name + "_ssd_fwd", xbc, proj, 3 * d + cd, dtb, alog, dsk, ng)
    rows = [(proj, d, 1, BF16), (proj, d, 2, BF16)]
    (g_out,) = _row_fwd(name + "_gm_fwd", _gm_tile, rows, [], [vg, ws, bs_t, og], [(d, BF16)], 0, CHUNK)
    return jnp.concatenate([y, g_out], axis=-1), (xbc, hprev)


def _mixer_bwd(name, proj, d, cd, conv_w8, conv_b, dtb, alog, dsk, ng, vg, ws, bs_t, og, xbc, hprev, dycat):
    bl, s, n_proj = proj.shape
    dt_col = 3 * d + cd
    dy, dg_out = dycat[..., :d], dycat[..., d:]
    dxs, dbm, dcm, ddt2, dz, ddtb, dalog, ddsk, dng = _ssd_bwd(name + "_ssd_bwd", xbc, proj, dt_col, dtb, alog, dsk, ng, hprev, dy)
    dxbc_act = jnp.concatenate([dxs, dbm, dcm], axis=-1)
    dxbc, dw8, dcb = _conv_bwd(name + "_conv_bwd", proj, 3 * d, cd, conv_w8, conv_b, SSD_CONV, False, dxbc_act, BF16)
    rows = [(proj, d, 1, BF16), (proj, d, 2, BF16)]
    (du, dv), _, (dvg, dws, dbs_t, dog) = _row_bwd(name + "_gm_bwd", _gm_tile, rows, [], [vg, ws, bs_t, og], [(d, BF16)], 0, CHUNK, [dg_out])
    ddt = (ddt2[0] + ddt2[1]).astype(BF16)
    pad = jnp.zeros((bl, s, n_proj - dt_col - LANES), BF16)
    dproj = jnp.concatenate([dz, du, dv, dxbc, ddt, pad], axis=-1)
    return dproj, (dw8, dcb, ddtb, dalog, ddsk, dng, dvg, dws, dbs_t, dog)


def _position():
    return lax.axis_index("x"), lax.axis_index("y"), lax.axis_index("c")


def _at(ref, idx):
    return ref.at[idx] if len(idx) else ref


def _exchange(name, inputs, out_shapes, plan, inplace=False):
    if inplace:
        out_shapes = [(a.shape, a.dtype) for a in inputs]
    n_in, n_out = len(inputs), len(out_shapes)
    n_copy = len(plan(0, 0, 0))

    def body(*refs):
        in_refs, out_refs, token = refs[:n_in], refs[n_in:n_in + n_out], refs[n_in + n_out]
        send_sems, recv_sems = refs[n_in + n_out + 1:]
        token[...] = jnp.zeros_like(token)
        x, y, c = _position()
        copies = plan(x, y, c)

        def copy(k, src, dst, peer):
            return pltpu.make_async_remote_copy(src_ref=src, dst_ref=dst, send_sem=send_sems.at[k], recv_sem=recv_sems.at[k], device_id=peer, device_id_type=MESH)

        src_refs = out_refs if inplace else in_refs
        sends = [copy(k, _at(src_refs[sa], si), _at(out_refs[da], di), peer) for k, (sa, si, da, di, peer, _) in enumerate(copies)]
        for cp in sends:
            cp.start()
        for k, (sa, si, da, _, peer, li) in enumerate(copies):
            copy(k, _at(src_refs[sa], si), _at(out_refs[da], li), peer).wait_recv()
        for cp in sends:
            cp.wait_send()

    any_spec = pl.BlockSpec(memory_space=pl.ANY)
    outs = pl.pallas_call(
        body, name=name, out_shape=[_sds(s, dt) for s, dt in out_shapes] + [_sds((SUBLANES, LANES), F32)], in_specs=[any_spec] * n_in,
        out_specs=[any_spec] * n_out + [pl.BlockSpec(memory_space=pltpu.VMEM)],
        scratch_shapes=[pltpu.SemaphoreType.DMA((n_copy,)), pltpu.SemaphoreType.DMA((n_copy,))],
        input_output_aliases={i: i for i in range(n_in)} if inplace else {},
    )(*inputs)
    return list(outs[:n_out]), outs[n_out]


def _exchange_start(name, inputs, out_shapes, plan):
    n_in, n_out = len(inputs), len(out_shapes)
    n_copy = len(plan(0, 0, 0))

    def body(*refs):
        in_refs, land_refs = refs[:n_in], refs[n_in:n_in + n_out]
        send_sems, recv_sems = refs[n_in + n_out:n_in + n_out + 2]
        token = refs[-1]
        x, y, c = _position()
        for k, (sa, si, da, di, peer, _) in enumerate(plan(x, y, c)):
            pltpu.make_async_remote_copy(
                src_ref=_at(in_refs[sa], si), dst_ref=_at(land_refs[da], di), send_sem=send_sems.at[k], recv_sem=recv_sems.at[k],
                device_id=peer, device_id_type=MESH).start()
        token[...] = jnp.zeros_like(token)

    hbm, sem = pl.BlockSpec(memory_space=pltpu.HBM), pl.BlockSpec(memory_space=pltpu.SEMAPHORE)
    lands = [lax.empty(s, dt) for s, dt in out_shapes]
    args = [pltpu.with_memory_space_constraint(a, pltpu.HBM) for a in list(inputs) + lands]
    outs = pl.pallas_call(
        body, name=name,
        out_shape=(pltpu.SemaphoreType.DMA((n_copy,)), pltpu.SemaphoreType.DMA((n_copy,)), *[pltpu.HBM(a.shape, a.dtype) for a in args], _sds((SUBLANES, LANES), F32)),
        in_specs=[hbm] * (n_in + n_out), out_specs=(sem, sem, *[hbm] * (n_in + n_out), pl.BlockSpec(memory_space=pltpu.VMEM)),
        input_output_aliases={i: 2 + i for i in range(n_in + n_out)},
        compiler_params=pltpu.CompilerParams(has_side_effects=pltpu.SideEffectType.DATAFLOW_SIDE_EFFECTING),
    )(*args)
    return dict(name=name, plan=plan, sems=outs[:2], ins=list(outs[2:2 + n_in]), lands=list(outs[2 + n_in:2 + n_in + n_out]), token=outs[-1])


def _exchange_wait(started, after):
    plan, n_in, n_out = started["plan"], len(started["ins"]), len(started["lands"])

    def body(*refs):
        in_refs, land_refs = refs[:n_in], refs[n_in:n_in + n_out]
        send_sems, recv_sems = refs[n_in + n_out:n_in + n_out + 2]
        token = refs[-1]
        x, y, c = _position()
        for k, (sa, si, da, _, peer, li) in enumerate(plan(x, y, c)):
            cp = pltpu.make_async_remote_copy(
                src_ref=_at(in_refs[sa], si), dst_ref=_at(land_refs[da], li), send_sem=send_sems.at[k], recv_sem=recv_sems.at[k],
                device_id=peer, device_id_type=MESH)
            cp.wait_send()
            cp.wait_recv()
        token[...] = jnp.zeros_like(token)

    hbm, sem = pl.BlockSpec(memory_space=pltpu.HBM), pl.BlockSpec(memory_space=pltpu.SEMAPHORE)
    bufs = started["ins"] + started["lands"]
    outs = pl.pallas_call(
        body, name=started["name"] + "_wait", out_shape=(*[pltpu.HBM(a.shape, a.dtype) for a in bufs], _sds((SUBLANES, LANES), F32)),
        in_specs=[hbm] * len(bufs) + [sem, sem, pl.BlockSpec(memory_space=pl.ANY)], out_specs=(*[hbm] * len(bufs), pl.BlockSpec(memory_space=pltpu.VMEM)),
        input_output_aliases={i: i for i in range(len(bufs))},
        compiler_params=pltpu.CompilerParams(has_side_effects=pltpu.SideEffectType.DATAFLOW_SIDE_EFFECTING),
    )(*bufs, *started["sems"], after)
    return list(outs[:n_in]), list(outs[n_in:n_in + n_out]), outs[-1]


def _after(value, token):
    return value + token[0, 0].astype(value.dtype)


def _chip_peers(x, y, c):
    return [(1 - x, y, c), (x, 1 - y, c), (1 - x, 1 - y, c)]


def _chip_of(p):
    return 2 * p[0] + p[1]


def _set_slot(slots, me, blk):
    return lax.dynamic_update_slice(slots, blk[None], (me,) + (0,) * blk.ndim)


def _by_core(c, mine, other, axis):
    return jnp.where(c == 0, jnp.stack([mine, other], axis), jnp.stack([other, mine], axis))


def _plan_gather_chips(n):
    def plan(x, y, c):
        return [(a, (), a, (2 * x + y,), p, (_chip_of(p),)) for a in range(n) for p in _chip_peers(x, y, c)]

    return plan


def _gather_chips(name, blocks):
    recv, token = _exchange(name, blocks, [((4,) + b.shape, b.dtype) for b in blocks], _plan_gather_chips(len(blocks)))
    x, y, _ = _position()
    return [_set_slot(r, 2 * x + y, b) for r, b in zip(recv, blocks)], token


def _gather_pass_cores(name, blocks, from_chips):
    n = len(blocks)

    def plan_cores(x, y, c):
        me, sib = 2 * x + y, (x, y, 1 - c)
        own = [(a, (), a, (me,), sib, (me,)) for a in range(n)]
        passed = [(n + a, (_chip_of(p),), a, (_chip_of(p),), sib, (_chip_of(p),)) for a in range(n) for p in _chip_peers(x, y, c)]
        return own + passed

    from_core, token = _exchange(name + "_cores", list(blocks) + list(from_chips), [((4,) + b.shape, b.dtype) for b in blocks], plan_cores)
    x, y, c = _position()
    return [_by_core(c, _set_slot(r1, 2 * x + y, b), r2, 1) for b, r1, r2 in zip(blocks, from_chips, from_core)], token


def _gather_two_level(name, blocks):
    from_chips, _ = _exchange(name + "_chips", blocks, [((4,) + b.shape, b.dtype) for b in blocks], _plan_gather_chips(len(blocks)))
    return _gather_pass_cores(name, blocks, from_chips)


def _pair_add(name, g42, r4):
    _, _, rh, cols = g42.shape
    tr = _divisors(rh, 512, SUBLANES * 2)[-1]

    def body(c_ref, a_ref, b_ref, o_ref):
        o_ref[0] = (a_ref[0, 0].astype(F32) + b_ref[0].astype(F32)).astype(o_ref.dtype)

    cidx = lax.axis_index("c").astype(jnp.int32).reshape(1)
    return _pcall(
        body, name, _sds(r4.shape, BF16), grid=(4, rh // tr),
        in_specs=[pl.BlockSpec((1, 1, tr, cols), lambda s, i, c_ref: (s, c_ref[0], i, 0)), pl.BlockSpec((1, tr, cols), lambda s, i, c_ref: (s, i, 0))],
        out_specs=pl.BlockSpec((1, tr, cols), lambda s, i, c_ref: (s, i, 0)), sem=("parallel", "parallel"), prefetch=1,
    )(cidx, g42, r4)


def _slot_sum(name, parts):
    n, r, cols = parts.shape
    cap = max(2 * SUBLANES, (4 * 1024 * 1024) // (n * cols * parts.dtype.itemsize))
    tr = _divisors(r, cap, 2 * SUBLANES)[-1]

    def body(p_ref, o_ref):
        acc = p_ref[0].astype(F32)
        for k in range(1, n):
            acc = acc + p_ref[k].astype(F32)
        o_ref[...] = acc

    return _pcall(
        body, name, _sds((r, cols), F32), grid=(r // tr,), in_specs=[pl.BlockSpec((n, tr, cols), lambda i: (0, i, 0))],
        out_specs=pl.BlockSpec((tr, cols), lambda i: (i, 0)), sem=("parallel",),
    )(parts)


def _slot_sums(name, parts):
    k = len(parts)

    def body(*refs):
        for p_ref, o_ref in zip(refs[:k], refs[k:]):
            acc = p_ref[0]
            for j in range(1, p_ref.shape[0]):
                acc = acc + p_ref[j]
            o_ref[...] = acc

    return list(_pcall(body, name, [_sds(p.shape[1:], F32) for p in parts])(*parts))


def _gather_weights_start(name, shards, token):
    c = lax.axis_index("c")
    halves = [lax.dynamic_slice_in_dim(w, c * (w.shape[0] // 2), w.shape[0] // 2, 0).astype(BF16) for w in shards]
    if token is not None:
        halves[0] = _after(halves[0], token)
    n = len(halves)

    def plan(x, y, c):
        return [(a, (), a, (2 * x + y, c), p, (_chip_of(p), c)) for a in range(n) for p in _chip_peers(x, y, c)]

    return _exchange_start(name + "_ag_chips", halves, [((4, 2) + h.shape, h.dtype) for h in halves], plan)


def _gather_weights_finish(name, started, after):
    halves, slots, token = _exchange_wait(started, after)
    n = len(halves)
    x, y, c = _position()
    slots = [lax.dynamic_update_slice(s, h[None, None], (2 * x + y, c, 0, 0)) for s, h in zip(slots, halves)]

    def plan_cores(x, y, c):
        return [(a, (s, c), a, (s, c), (x, y, 1 - c), (s, 1 - c)) for a in range(n) for s in range(4)]

    full, _ = _exchange(name + "_ag_cores", slots, None, plan_cores, inplace=True)
    return [f.reshape((4, 2 * h.shape[0], h.shape[1])) for f, h in zip(full, halves)], token


def _reduce_weights_start(name, grads, token):
    n = len(grads)
    g42 = [g.reshape(4, 2, g.shape[1] // 2, g.shape[2]) for g in grads]
    if token is not None:
        g42[0] = _after(g42[0], token)

    def plan_swap(x, y, c):
        return [(a, (s, 1 - c), a, (s,), (x, y, 1 - c), (s,)) for a in range(n) for s in range(4)]

    def plan_chips(x, y, c):
        return [(a, (_chip_of(p),), a, (2 * x + y,), p, (_chip_of(p),)) for a in range(n) for p in _chip_peers(x, y, c)]

    other, _ = _exchange(name + "_rs_cores", g42, [((4,) + g.shape[2:], g.dtype) for g in g42], plan_swap)
    pair = [_pair_add(f"{name}_rs_pair{a}", g, o) for a, (g, o) in enumerate(zip(g42, other))]
    return _exchange_start(name + "_rs_chips", pair, [(p.shape, p.dtype) for p in pair], plan_chips)


def _reduce_weights_finish(name, started, after):
    pair, recv, token = _exchange_wait(started, after)
    n = len(pair)
    x, y, c = _position()
    me = 2 * x + y

    def plan_share(x, y, c):
        return [(a, (), a, (), (x, y, 1 - c), ()) for a in range(n)]

    parts = [lax.dynamic_update_slice(r, lax.dynamic_slice_in_dim(p, me, 1, 0), (me, 0, 0)) for r, p in zip(recv, pair)]
    mine = [_slot_sum(f"{name}_rs_sum{a}", p) for a, p in enumerate(parts)]
    theirs, _ = _exchange(name + "_rs_share", mine, [(m.shape, m.dtype) for m in mine], plan_share)
    return [_by_core(c, m, t, 0).reshape(2 * m.shape[0], m.shape[1]) for m, t in zip(mine, theirs)], token


def _allreduce_small(name, grads):
    both, _ = _gather_two_level(name + "_ag", list(grads))
    return _slot_sums(name + "_sum", [g.reshape((8,) + g.shape[2:]) for g in both])


def _ada_fwd_call(name, c_all, w, b_shard):
    nl, d, ns = w.shape
    nb = c_all.shape[0]

    def body(c_ref, w_ref, b_ref, o_ref):
        cv = c_ref[...]
        o_ref[0] = _dg(cv * _sigmoid(cv), w_ref[0], ((1,), (0,))) + b_ref[0]

    return _pcall(
        body, name, _sds((nl, nb, ns), F32), grid=(nl,),
        in_specs=[pl.BlockSpec((nb, d), lambda l: (0, 0)), pl.BlockSpec((1, d, ns), lambda l: (l, 0, 0)), pl.BlockSpec((1, 1, ns), lambda l: (l, 0, 0))],
        out_specs=pl.BlockSpec((1, nb, ns), lambda l: (l, 0, 0)), sem=("parallel",),
    )(c_all, w, b_shard)


def _ada_bwd_call(name, c_all, dm_shard, dm_all):
    nl, nb, ns = dm_shard.shape
    d = c_all.shape[1]
    nm = dm_all.shape[2]

    def body(c_ref, ds_ref, da_ref, dw_ref, db_ref):
        cv = c_ref[...]
        dw_ref[0] = _dg(cv * _sigmoid(cv), ds_ref[0], ((0,), (0,)))
        db_ref[0] = jnp.sum(da_ref[0], axis=0, keepdims=True)

    return _pcall(
        body, name, [_sds((nl, d, ns), F32), _sds((nl, 1, nm), F32)], grid=(nl,),
        in_specs=[pl.BlockSpec((nb, d), lambda l: (0, 0)), pl.BlockSpec((1, nb, ns), lambda l: (l, 0, 0)), pl.BlockSpec((1, nb, nm), lambda l: (l, 0, 0))],
        out_specs=[pl.BlockSpec((1, d, ns), lambda l: (l, 0, 0)), pl.BlockSpec((1, 1, nm), lambda l: (l, 0, 0))], sem=("parallel",),
    )(c_all, dm_shard, dm_all)


def _ada_fwd(name, bl, c_all, w, b):
    nl, d, ns = w.shape
    chip = 2 * lax.axis_index("x") + lax.axis_index("y")
    b_shard = lax.dynamic_slice(b, (0, chip * ns), (nl, ns)).reshape(nl, 1, ns)
    shard = _ada_fwd_call(name + "_fwd", c_all, w, b_shard)
    (allc,), token = _gather_chips(name + "_ag", [shard])
    mods = jnp.transpose(allc, (1, 2, 0, 3)).reshape(nl, c_all.shape[0], 4 * ns)
    return lax.dynamic_slice(mods, (0, (2 * chip + lax.axis_index("c")) * bl, 0), (nl, bl, 4 * ns)), token


def _ada_bwd(name, bl, c_all, ns, dm):
    nl = dm.shape[0]
    chip = 2 * lax.axis_index("x") + lax.axis_index("y")
    (dm_all,), _ = _gather_two_level(name + "_bwd_ag", [dm])
    dm_all = jnp.transpose(dm_all.reshape((8,) + dm.shape), (1, 0, 2, 3)).reshape(nl, 8 * bl, 4 * ns)
    dm_shard = lax.dynamic_slice(dm_all, (0, 0, chip * ns), (nl, 8 * bl, ns))
    dw, db = _ada_bwd_call(name + "_bwd", c_all, dm_shard, dm_all)
    return dw, db.reshape(nl, 4 * ns)


def _adamw(name, w, g, m, v):
    shape = w.shape
    cols = shape[-1]
    w2, g2, m2, v2 = (t.reshape(-1, cols) for t in (w, g, m, v))
    rows = w2.shape[0]
    cap = max(SUBLANES, (512 * 1024) // max(cols, 1) // SUBLANES * SUBLANES)
    tr = _divisors(rows, cap, SUBLANES)[-1]

    def body(w_ref, g_ref, m_ref, v_ref, d_ref, mo_ref, vo_ref):
        gv = g_ref[...]
        mn = ADAM_B1 * m_ref[...] + (1.0 - ADAM_B1) * gv
        vn = ADAM_B2 * v_ref[...] + (1.0 - ADAM_B2) * (gv * gv)
        m_hat = mn / (1.0 - ADAM_B1 ** ADAM_STEP)
        v_hat = vn / (1.0 - ADAM_B2 ** ADAM_STEP)
        d_ref[...] = -ADAM_LR * (m_hat / (jnp.sqrt(v_hat) + ADAM_EPS) + ADAM_WD * w_ref[...])
        mo_ref[...] = mn
        vo_ref[...] = vn

    spec = pl.BlockSpec((tr, cols), lambda i: (i, 0))
    outs = _pcall(body, name, [_sds((rows, cols), F32)] * 3, grid=(rows // tr,), in_specs=[spec] * 4, out_specs=[spec] * 3, sem=("parallel",))(w2, g2, m2, v2)
    return tuple(o.reshape(shape) for o in outs)


def _adamw_small(name, ws, gs, ms, vs):
    n = len(ws)

    def body(*refs):
        for k in range(n):
            w_ref, g_ref, m_ref, v_ref = (refs[j * n + k] for j in range(4))
            d_ref, mo_ref, vo_ref = (refs[(4 + j) * n + k] for j in range(3))
            gv = g_ref[...]
            mn = ADAM_B1 * m_ref[...] + (1.0 - ADAM_B1) * gv
            vn = ADAM_B2 * v_ref[...] + (1.0 - ADAM_B2) * (gv * gv)
            m_hat = mn / (1.0 - ADAM_B1 ** ADAM_STEP)
            v_hat = vn / (1.0 - ADAM_B2 ** ADAM_STEP)
            d_ref[...] = -ADAM_LR * (m_hat / (jnp.sqrt(v_hat) + ADAM_EPS) + ADAM_WD * w_ref[...])
            mo_ref[...] = mn
            vo_ref[...] = vn

    outs = _pcall(body, name, [_sds(w.shape, F32) for w in ws] * 3)(*ws, *gs, *ms, *vs)
    return outs[:n], outs[n:2 * n], outs[2 * n:]


def _pad_rows(w, rows):
    return jnp.pad(w, ((0, rows - w.shape[0]), (0, 0)))


def _pad_lanes(v):
    return jnp.pad(v, (0, LANES - v.shape[0])).reshape(1, LANES)


BIG = ("w_in", "w_out", "ff_up", "ff_down")
BIG_AXIS = {"w_in": 1, "w_out": 0, "ff_up": 1, "ff_down": 0}
CONVW = ("ssd_conv_w", "ff_conv_w")
SMALL = ("norm1_g", "norm2_g", "ssd_conv_b", "ssd_dt_bias", "ssd_a_log", "ssd_d", "ssd_norm_g", "gm_vnorm_g", "gm_ws", "gm_bs", "gm_out_g", "ff_conv_b")
WEIGHTS = ("ada_w", "ada_b", "norm1_g", "norm2_g", "w_in", "ssd_conv_w", "ssd_conv_b", "ssd_dt_bias", "ssd_a_log", "ssd_d", "ssd_norm_g", "gm_vnorm_g", "gm_ws", "gm_bs", "gm_out_g", "w_out", "ff_up", "ff_conv_w", "ff_conv_b", "ff_down", "final_g")


def kernel(x, c, ada_w, ada_b, norm1_g, norm2_g, w_in, ssd_conv_w, ssd_conv_b, ssd_dt_bias, ssd_a_log, ssd_d, ssd_norm_g, gm_vnorm_g, gm_ws, gm_bs, gm_out_g, w_out, ff_up, ff_conv_w, ff_conv_b, ff_down, final_g, loss_target, m_ada_w, m_ada_b, m_norm1_g, m_norm2_g, m_w_in, m_ssd_conv_w, m_ssd_conv_b, m_ssd_dt_bias, m_ssd_a_log, m_ssd_d, m_ssd_norm_g, m_gm_vnorm_g, m_gm_ws, m_gm_bs, m_gm_out_g, m_w_out, m_ff_up, m_ff_conv_w, m_ff_conv_b, m_ff_down, m_final_g, v_ada_w, v_ada_b, v_norm1_g, v_norm2_g, v_w_in, v_ssd_conv_w, v_ssd_conv_b, v_ssd_dt_bias, v_ssd_a_log, v_ssd_d, v_ssd_norm_g, v_gm_vnorm_g, v_gm_ws, v_gm_bs, v_gm_out_g, v_w_out, v_ff_up, v_ff_conv_w, v_ff_conv_b, v_ff_down, v_final_g):
    given = dict(locals())
    weights = {n: given[n] for n in WEIGHTS}
    bl, s, d = x.shape
    nl = ada_w.shape[0]
    heads = d // HEAD_DIM
    cd = d + 2 * SSD_GROUPS * STATE
    f = ff_down.shape[1] * 4
    n_in = d + cd + heads + 2 * d
    n_proj = _round_up(3 * d + cd + LANES, 2 * LANES)
    tm = _divisors(s, 512)[-1]

    pre, _ = _gather_two_level("pre_ag", [c] + [weights[n] for n in CONVW])
    c_all = pre[0].reshape(8 * bl, d)
    conv_full = {n: jnp.concatenate([p[k, 0] for k in range(4)], axis=-1) for n, p in zip(CONVW, pre[1:])}
    mods_all, ada_token = _ada_fwd("ada", bl, c_all, ada_w, ada_b)
    mods = mods_all.reshape(nl, bl, N_MOD, 1, d)

    gathering = {}

    def start_gathers(l, token):
        for n in BIG:
            gathering[l, n] = _gather_weights_start(f"wg{l}_{n}", [weights[n][l]], token)
            token = gathering[l, n]["token"]
        return token

    def landed(l, n, after):
        (full,), token = _gather_weights_finish(f"wg{l}_{n}", gathering.pop((l, n)), after)
        return (full.reshape(-1, full.shape[2]) if BIG_AXIS[n] == 0 else jnp.concatenate([full[k] for k in range(4)], axis=1)), token

    start_gathers(0, ada_token)
    r2 = lambda v: v.reshape(1, -1)
    n_gm = d // GM_HEAD
    fg = r2(final_g)

    def layer_params(l):
        return dict(
            norm1=r2(norm1_g[l]), norm2=r2(norm2_g[l]), conv_w8=_pad_rows(conv_full["ssd_conv_w"][l], SUBLANES), conv_b=r2(ssd_conv_b[l]),
            dtb=_pad_lanes(ssd_dt_bias[l]), alog=_pad_lanes(ssd_a_log[l]), dsk=r2(jnp.repeat(ssd_d[l], HEAD_DIM)), ng=r2(ssd_norm_g[l]),
            vg=r2(gm_vnorm_g[l]), ws=gm_ws[l], bs_t=jnp.pad(gm_bs[l].T, ((0, 0), (0, LANES - n_gm))), og=r2(gm_out_g[l]),
            ff_w8=_pad_rows(conv_full["ff_conv_w"][l], SUBLANES), ff_b=r2(ff_conv_b[l]))

    def mixer_args(p):
        return (p["conv_w8"], p["conv_b"], p["dtb"], p["alog"], p["dsk"], p["ng"], p["vg"], p["ws"], p["bs_t"], p["og"])

    def padded_w_in(wi):
        return jnp.concatenate(
            [wi[:, :d], wi[:, d + cd + heads:], wi[:, d:d + cd], wi[:, d + cd:d + cd + heads], jnp.zeros((d, n_proj - (3 * d + cd + heads)), BF16)], axis=1)

    saved, xcur, pending = [], x, None
    for l in range(nl):
        p, w = layer_params(l), {}
        wi, token = landed(l, "w_in", mods_all if l == 0 else pending[0])
        w["w_in"] = padded_w_in(wi)
        sh1, sc1, g1, sh2, sc2, g2 = (mods[l, :, k] for k in range(N_MOD))
        if l + 1 < nl:
            sc1 = _after(sc1, start_gathers(l + 1, token))
        if pending is None:
            (h,) = _row_fwd(f"nm{l}_fwd", _nm_tile, [(xcur, d, 0, None)], [sc1, sh1], [p["norm1"]], [(d, BF16)], 0, tm)
            x1 = xcur
        else:
            rows = [(xcur, d, 0, None), (pending[0], d, 0, None)]
            x1, h = _row_fwd(f"rnm{l}a_fwd", _rnm_tile, rows, [pending[1], sc1, sh1], [p["norm1"]], [(d, F32), (d, BF16)], 0, tm)
        proj = _mm(f"win{l}_fwd", h, w["w_in"], F32)
        ycat, (xbc, hprev) = _mixer_fwd(f"mix{l}", proj, d, cd, *mixer_args(p))
        w["w_out"], _ = landed(l, "w_out", ycat)
        mix = _mm(f"wout{l}_fwd", ycat, w["w_out"], F32)
        x2, h2 = _row_fwd(f"rnm{l}b_fwd", _rnm_tile, [(x1, d, 0, None), (mix, d, 0, None)], [g1, sc2, sh2], [p["norm2"]], [(d, F32), (d, BF16)], 0, tm)
        w_up, _ = landed(l, "ff_up", h2)
        w["ff_up"] = _gate_value_blocks(f"ffup{l}_blocks", w_up, f)
        up = _mm(f"ffup{l}_fwd", h2, w["ff_up"], F32)
        act = _conv_fwd(f"ffact{l}_fwd", up, 0, f, p["ff_w8"], p["ff_b"], FF_CONV, True, BF16)
        w["ff_down"], _ = landed(l, "ff_down", act)
        down = _mm(f"ffdown{l}_fwd", act, w["ff_down"], F32)
        saved.append(dict(p=p, w=w, x_in=xcur, pending=pending, h=h, proj=proj, xbc=xbc, hprev=hprev, ycat=ycat, x1=x1, mix=mix, h2=h2, up=up, act=act))
        xcur, pending = x2, (down, g2)
    rows = [(xcur, d, 0, F32), (pending[0], d, 0, BF16), (loss_target, d, 0, None)]
    (loss_local,) = _row_fwd("final_fwd", _final_tile, rows, [pending[1]], [fg], [], 1, tm)
    loss = lax.psum(loss_local[0, 0], AXES)

    (dx2, ddown), (dg2,), (dfg,) = _row_bwd("final_bwd", _final_tile, rows, [pending[1]], [fg], [], 1, tm, [jnp.ones((1, 1), F32)])
    dmods, small_grads = [None] * nl, [None] * nl
    reducing = {}

    def start_reduce(l, n, g):
        width = g.shape[1] // 4
        by_chip = g.reshape(4, g.shape[0] // 4, g.shape[1]) if BIG_AXIS[n] == 0 else jnp.stack([g[:, k * width:(k + 1) * width] for k in range(4)])
        reducing[l, n] = _reduce_weights_start(f"wg{l}_{n}", [by_chip], None)
        return reducing[l, n]["token"]

    for l in reversed(range(nl)):
        sv = saved[l]
        p, w = sv["p"], sv["w"]
        sh1, sc1, g1, sh2, sc2, g2 = (mods[l, :, k] for k in range(N_MOD))
        dact, dw_down = _mm_bwd(f"ffdown{l}", sv["act"], w["ff_down"], ddown)
        ff_b = _after(p["ff_b"], start_reduce(l, "ff_down", dw_down))
        dup, dff_w8, dff_b = _conv_bwd(f"ffact{l}_bwd", sv["up"], 0, f, p["ff_w8"], ff_b, FF_CONV, True, dact, BF16)
        dh2, dw_up = _mm_bwd(f"ffup{l}", sv["h2"], w["ff_up"], dup)
        sc2 = _after(sc2, start_reduce(l, "ff_up", _gate_value_blocks(f"ffup{l}_columns", dw_up, f, inverse=True)))
        rows = [(sv["x1"], d, 0, F32), (sv["mix"], d, 0, BF16)]
        (dx1, dmix), (dg1, dsc2, dsh2), (dn2,) = _row_bwd(f"rnm{l}b_bwd", _rnm_tile, rows, [g1, sc2, sh2], [p["norm2"]], [(d, F32), (d, BF16)], 0, tm, [dx2, dh2])
        dycat, dw_out = _mm_bwd(f"wout{l}", sv["ycat"], w["w_out"], dmix)
        p_tied = dict(p, dtb=_after(p["dtb"], start_reduce(l, "w_out", dw_out)))
        dproj, (dw8, dcb, ddtb, dalog, ddsk, dng, dvg, dws, dbs_t, dog) = _mixer_bwd(f"mix{l}", sv["proj"], d, cd, *mixer_args(p_tied), sv["xbc"], sv["hprev"], dycat)
        dh, dw_in_p = _mm_bwd(f"win{l}", sv["h"], w["w_in"], dproj)
        dw_in = jnp.concatenate([dw_in_p[:, :d], dw_in_p[:, 3 * d:3 * d + cd], dw_in_p[:, 3 * d + cd:3 * d + cd + heads], dw_in_p[:, d:3 * d]], axis=1)
        sc1 = _after(sc1, start_reduce(l, "w_in", dw_in))
        if sv["pending"] is None:
            (dx2,), (dsc1, dsh1), (dn1,) = _row_bwd(f"nm{l}_bwd", _nm_res_tile, [(sv["x_in"], d, 0, F32)], [sc1, sh1], [p["norm1"]], [(d, F32), (d, BF16)], 0, tm, [dx1, dh])
        else:
            rows = [(sv["x_in"], d, 0, F32), (sv["pending"][0], d, 0, BF16)]
            (dx2, ddown), (dg2_prev, dsc1, dsh1), (dn1,) = _row_bwd(
                f"rnm{l}a_bwd", _rnm_tile, rows, [sv["pending"][1], sc1, sh1], [p["norm1"]], [(d, F32), (d, BF16)], 0, tm, [dx1, dh])
        dmods[l] = jnp.concatenate([dsh1, dsc1, dg1, dsh2, dsc2, dg2], axis=1).reshape(bl, N_MOD * d)
        if sv["pending"] is not None:
            dg2 = dg2_prev
        small_grads[l] = dict(
            norm1_g=dn1.reshape(d), norm2_g=dn2.reshape(d), ssd_conv_b=dcb.reshape(cd), ssd_dt_bias=ddtb[0, :heads], ssd_a_log=dalog[0, :heads],
            ssd_d=ddsk.reshape(heads, HEAD_DIM).sum(-1), ssd_norm_g=dng.reshape(d), gm_vnorm_g=dvg.reshape(d), gm_ws=dws, gm_bs=dbs_t[:, :n_gm].T,
            gm_out_g=dog.reshape(d), ff_conv_b=dff_b.reshape(f), ssd_conv_w=dw8[:SSD_CONV], ff_conv_w=dff_w8[:FF_CONV])
    grad_x = dx2

    g_ada_w, g_ada_b = _ada_bwd("ada", bl, c_all, ada_w.shape[2], jnp.stack(dmods))
    small_names_r = SMALL + CONVW + ("final_g",)
    summed = _allreduce_small("small", [jnp.stack([small_grads[l][n] for l in range(nl)]) for n in SMALL + CONVW] + [dfg])
    grad = {"ada_w": g_ada_w, "ada_b": g_ada_b}
    big_grads = {}
    for l in reversed(range(nl)):
        for n in reversed(BIG):
            (big_grads[l, n],), _ = _reduce_weights_finish(f"wg{l}_{n}", reducing.pop((l, n)), summed[0])
    for n in BIG:
        grad[n] = jnp.stack([big_grads[l, n] for l in range(nl)])
    grad.update(zip(small_names_r, summed))
    chip = 2 * lax.axis_index("x") + lax.axis_index("y")
    for n in CONVW:
        width = weights[n].shape[-1]
        grad[n] = lax.dynamic_slice_in_dim(grad[n], chip * width, width, axis=2)

    delta, new_m, new_v = {}, {}, {}
    for n in ("ada_w",) + BIG:
        delta[n], new_m[n], new_v[n] = _adamw("adam_" + n, weights[n], grad[n], given["m_" + n], given["v_" + n])
    small_names = ("ada_b",) + SMALL + CONVW + ("final_g",)
    as2d = lambda t: t.reshape(1, -1) if t.ndim == 1 else t
    res = _adamw_small(
        "adam_small", [as2d(weights[n]) for n in small_names], [as2d(grad[n]) for n in small_names],
        [as2d(given["m_" + n]) for n in small_names], [as2d(given["v_" + n]) for n in small_names])
    for out, vals in zip((delta, new_m, new_v), res):
        for n, val in zip(small_names, vals):
            out[n] = val.reshape(weights[n].shape)
    grad["final_g"] = grad["final_g"].reshape(final_g.shape)

    return (loss, grad_x, *[grad[n] for n in WEIGHTS], *[delta[n] for n in WEIGHTS], *[new_m[n] for n in WEIGHTS], *[new_v[n] for n in WEIGHTS])
```

```python
import functools

import jax
import jax.numpy as jnp
from jax import lax
from jax.experimental import pallas as pl
from jax.experimental.pallas import tpu as pltpu

F32 = jnp.float32
BF16 = jnp.bfloat16
EPS = 1e-6
CHUNK = 128
HEAD_DIM = 64
STATE = 128
GM_HEAD = 128
SSD_GROUPS = 2
SSD_CONV = 4
FF_CONV = 3
N_MOD = 6
LANES = 128
SUBLANES = 8
CONV_LANES = 256
CONV_ROWS = 32
V7X_VMEM_LIMIT = 48 * 1024 * 1024
MM_VMEM_BUDGET = 30 * 1024 * 1024
MM_STEP_MACS = 2048 * 1024 * 1024
MM_ACC_SECONDS_PER_VREG = 1.4e-9
V7X_MXU_FLOPS = 996e12
V7X_HBM_BYTES_PER_S = 3.3e12
V7X_STEP_SECONDS = 0.35e-6
ADAM_LR, ADAM_B1, ADAM_B2, ADAM_EPS, ADAM_WD, ADAM_STEP = 0.001, 0.9, 0.999, 1e-08, 0.01, 10
MESH = pl.DeviceIdType.MESH
AXES = ("x", "y", "c")


def _round_up(n, m):
    return (n + m - 1) // m * m


def _divisors(n, cap, mult=LANES):
    out = [t for t in range(mult, min(n, cap) + 1, mult) if n % t == 0]
    return out or [n]


def _pcall(body, name, out_shape, grid=(), in_specs=None, out_specs=None, scratch=(), sem=None, prefetch=0):
    params = pltpu.CompilerParams(dimension_semantics=sem, vmem_limit_bytes=V7X_VMEM_LIMIT)
    if prefetch:
        spec = pltpu.PrefetchScalarGridSpec(num_scalar_prefetch=prefetch, grid=grid, in_specs=in_specs, out_specs=out_specs, scratch_shapes=list(scratch))
        return pl.pallas_call(body, name=name, out_shape=out_shape, grid_spec=spec, compiler_params=params)
    if in_specs is None:
        return pl.pallas_call(body, name=name, out_shape=out_shape, compiler_params=params)
    return pl.pallas_call(body, name=name, out_shape=out_shape, grid=grid, in_specs=in_specs, out_specs=out_specs, scratch_shapes=list(scratch), compiler_params=params)


def _sds(shape, dtype):
    return jax.ShapeDtypeStruct(tuple(shape), dtype)


def _mm_tiles(m, n, k, a_bytes, b_bytes, o_bytes):
    best, best_key = None, None
    for tm in _divisors(m, 2048):
        for tn in _divisors(n, 2560):
            for tk in _divisors(k, 2560):
                vmem = 2 * (tm * tk * a_bytes + tk * tn * b_bytes + tm * tn * o_bytes) + tm * tn * 4
                if vmem > MM_VMEM_BUDGET or tm * tn * tk > MM_STEP_MACS:
                    continue
                ni, nj, nk = m // tm, n // tn, k // tk
                a_reads = 1 if nk == 1 else nj
                b_reads = 1 if (nk == 1 and nj == 1) else ni
                hbm = a_reads * m * k * a_bytes + b_reads * k * n * b_bytes + m * n * o_bytes
                t = max(2.0 * m * n * k / V7X_MXU_FLOPS, hbm / V7X_HBM_BYTES_PER_S) + ni * nj * nk * V7X_STEP_SECONDS
                if nk > 1:
                    t += ni * nj * nk * (tm * tn // (SUBLANES * LANES)) * MM_ACC_SECONDS_PER_VREG
                key = (-t, tm * tn * tk)
                if best_key is None or key > best_key:
                    best, best_key = (tm, tn, tk), key
    return best


def _matmul(name, a, b, mode, out_dtype):
    if mode == "nn":
        (m, k), n = a.shape, b.shape[1]
    elif mode == "nt":
        (m, k), n = a.shape, b.shape[0]
    else:
        (k, m), n = a.shape, b.shape[1]
    tm, tn, tk = _mm_tiles(m, n, k, a.dtype.itemsize, b.dtype.itemsize, jnp.dtype(out_dtype).itemsize)
    nk = k // tk
    if mode == "nn":
        a_spec = pl.BlockSpec((tm, tk), lambda i, j, kk: (i, kk))
        b_spec = pl.BlockSpec((tk, tn), lambda i, j, kk: (kk, j))
        dims = ((1,), (0,))
    elif mode == "nt":
        a_spec = pl.BlockSpec((tm, tk), lambda i, j, kk: (i, kk))
        b_spec = pl.BlockSpec((tn, tk), lambda i, j, kk: (j, kk))
        dims = ((1,), (1,))
    else:
        a_spec = pl.BlockSpec((tk, tm), lambda i, j, kk: (kk, i))
        b_spec = pl.BlockSpec((tk, tn), lambda i, j, kk: (kk, j))
        dims = ((0,), (0,))

    def body(a_ref, b_ref, o_ref, acc_ref):
        kk = pl.program_id(2)
        p = lax.dot_general(a_ref[...].astype(BF16), b_ref[...].astype(BF16), (dims, ((), ())), preferred_element_type=F32)
        if nk == 1:
            o_ref[...] = p.astype(o_ref.dtype)
        else:
            @pl.when(kk == 0)
            def _():
                acc_ref[...] = p

            @pl.when(kk > 0)
            def _():
                acc_ref[...] += p

            @pl.when(kk == nk - 1)
            def _():
                o_ref[...] = acc_ref[...].astype(o_ref.dtype)

    return _pcall(
        body, name, _sds((m, n), out_dtype), grid=(m // tm, n // tn, nk), in_specs=[a_spec, b_spec],
        out_specs=pl.BlockSpec((tm, tn), lambda i, j, kk: (i, j)), scratch=[pltpu.VMEM((tm, tn), F32)],
        sem=("parallel", "parallel", "arbitrary"),
    )(a, b)


def _mm(name, a, w, out_dtype):
    return _matmul(name, a.reshape(-1, a.shape[-1]), w, "nn", out_dtype).reshape(a.shape[:-1] + (w.shape[1],))


def _mm_bwd(name, a, w, dy):
    a2, dy2 = a.reshape(-1, a.shape[-1]), dy.reshape(-1, dy.shape[-1])
    return _matmul(name + "_dx", dy2, w, "nt", BF16).reshape(a.shape), _matmul(name + "_dw", a2, dy2, "tn", BF16)


def _dg(a, b, dims):
    return lax.dot_general(a.astype(BF16), b.astype(BF16), (dims, ((), ())), preferred_element_type=F32)


@jax.custom_vjp
def _dot_nn(a, b):
    return _dg(a, b, ((1,), (0,)))


_dot_nn.defvjp(lambda a, b: (_dot_nn(a, b), (a, b)), lambda r, d: (_dg(d, r[1], ((1,), (1,))), _dg(r[0], d, ((0,), (0,)))))


@jax.custom_vjp
def _dot_nt(a, b):
    return _dg(a, b, ((1,), (1,)))


_dot_nt.defvjp(lambda a, b: (_dot_nt(a, b), (a, b)), lambda r, d: (_dg(d, r[1], ((1,), (0,))), _dg(d, r[0], ((0,), (0,)))))


@jax.custom_vjp
def _dot_tn(a, b):
    return _dg(a, b, ((0,), (0,)))


_dot_tn.defvjp(lambda a, b: (_dot_tn(a, b), (a, b)), lambda r, d: (_dg(r[1], d, ((1,), (1,))), _dg(r[0], d, ((1,), (0,)))))


def _exact_dot(a, c, dims):
    hi = a.astype(BF16)
    r1 = a - hi.astype(F32)
    mid = r1.astype(BF16)
    lo = (r1 - mid.astype(F32)).astype(BF16)
    cb = c.astype(BF16)
    f = lambda t: lax.dot_general(t, cb, (dims, ((), ())), preferred_element_type=F32)
    return f(hi) + f(mid) + f(lo)


@jax.custom_vjp
def _sel_right(a, c):
    return _exact_dot(a, c, ((1,), (0,)))


_sel_right.defvjp(lambda a, c: (_sel_right(a, c), c), lambda c, d: (_exact_dot(d, c, ((1,), (1,))), jnp.zeros_like(c)))


def _exact_dot_left(c, a, dims):
    hi = a.astype(BF16)
    r1 = a - hi.astype(F32)
    mid = r1.astype(BF16)
    lo = (r1 - mid.astype(F32)).astype(BF16)
    cb = c.astype(BF16)
    f = lambda t: lax.dot_general(cb, t, (dims, ((), ())), preferred_element_type=F32)
    return f(hi) + f(mid) + f(lo)


@jax.custom_vjp
def _sel_left(c, a):
    return _exact_dot_left(c, a, ((1,), (0,)))


_sel_left.defvjp(lambda c, a: (_sel_left(c, a), c), lambda c, d: (jnp.zeros_like(c), _exact_dot_left(c, d, ((0,), (0,)))))


def _sigmoid(x):
    return 1.0 / (1.0 + jnp.exp(-x))


def _rms(x, g):
    return x * lax.rsqrt(jnp.mean(x * x, axis=-1, keepdims=True) + EPS) * g


def _gelu(x):
    return 0.5 * x * (1.0 + lax.erf(x * (2.0 ** -0.5)))


def _causal(n):
    return lax.broadcasted_iota(jnp.int32, (n, n), 0) >= lax.broadcasted_iota(jnp.int32, (n, n), 1)


def _row_in_specs(rows, bparams, gparams, tm):
    specs = [pl.BlockSpec((1, tm, w), lambda b, i, cb=cb: (b, i, cb)) for (_, w, cb, _) in rows]
    specs += [pl.BlockSpec((1, 1, p.shape[-1]), lambda b, i: (b, 0, 0)) for p in bparams]
    specs += [pl.BlockSpec(p.shape, lambda b, i, n=p.ndim: (0,) * n) for p in gparams]
    return specs


def _row_vals(refs, n_rows, n_b, n_g):
    vals = [r[0].astype(F32) for r in refs[:n_rows]]
    vals += [r[0].astype(F32) for r in refs[n_rows:n_rows + n_b]]
    vals += [r[...].astype(F32) for r in refs[n_rows + n_b:n_rows + n_b + n_g]]
    return vals


def _row_fwd(name, tile, rows, bparams, gparams, outs, n_sum, tm):
    bl, s = rows[0][0].shape[:2]
    n_in = len(rows) + len(bparams) + len(gparams)

    def body(*refs):
        first = (pl.program_id(0) == 0) & (pl.program_id(1) == 0)
        res = tile(*_row_vals(refs, len(rows), len(bparams), len(gparams)))
        o_refs = refs[n_in:]
        for k in range(len(outs)):
            o_refs[k][0] = res[k].astype(o_refs[k].dtype)
        for k in range(n_sum):
            o_ref, val = o_refs[len(outs) + k], res[len(outs) + k]

            @pl.when(first)
            def _(o_ref=o_ref, val=val):
                o_ref[...] = val

            @pl.when(jnp.logical_not(first))
            def _(o_ref=o_ref, val=val):
                o_ref[...] += val

    out_shape = [_sds((bl, s, w), dt) for (w, dt) in outs] + [_sds((1, 1), F32)] * n_sum
    out_specs = [pl.BlockSpec((1, tm, w), lambda b, i: (b, i, 0)) for (w, _) in outs] + [pl.BlockSpec((1, 1), lambda b, i: (0, 0))] * n_sum
    return _pcall(
        body, name, out_shape, grid=(bl, s // tm), in_specs=_row_in_specs(rows, bparams, gparams, tm), out_specs=out_specs,
        sem=("arbitrary", "arbitrary"),
    )(*[r[0] for r in rows], *bparams, *gparams)


def _row_bwd(name, tile, rows, bparams, gparams, outs, n_sum, tm, cts):
    bl, s = rows[0][0].shape[:2]
    n_r, n_b, n_g = len(rows), len(bparams), len(gparams)
    n_in = n_r + n_b + n_g
    n_ct = len(outs) + n_sum
    grad_rows = [k for k in range(n_r) if rows[k][3]]

    def body(*refs):
        b, i = pl.program_id(0), pl.program_id(1)
        vals = _row_vals(refs, n_r, n_b, n_g)
        ct_refs = refs[n_in:n_in + n_ct]
        ct = [r[0].astype(F32) for r in ct_refs[:len(outs)]] + [r[...] for r in ct_refs[len(outs):]]
        _, vjp = jax.vjp(tile, *vals)
        grads = vjp(tuple(ct))
        o_refs = refs[n_in + n_ct:]
        for j, k in enumerate(grad_rows):
            o_refs[j][0] = grads[k].astype(o_refs[j].dtype)
        for k in range(n_b):
            o_ref, val = o_refs[len(grad_rows) + k], grads[n_r + k]

            @pl.when(i == 0)
            def _(o_ref=o_ref, val=val):
                o_ref[0] = val

            @pl.when(i > 0)
            def _(o_ref=o_ref, val=val):
                o_ref[0] += val

        first = (b == 0) & (i == 0)
        for k in range(n_g):
            o_ref, val = o_refs[len(grad_rows) + n_b + k], grads[n_r + n_b + k]

            @pl.when(first)
            def _(o_ref=o_ref, val=val):
                o_ref[...] = val

            @pl.when(jnp.logical_not(first))
            def _(o_ref=o_ref, val=val):
                o_ref[...] += val

    in_specs = _row_in_specs(rows, bparams, gparams, tm)
    in_specs += [pl.BlockSpec((1, tm, w), lambda b, i: (b, i, 0)) for (w, _) in outs] + [pl.BlockSpec((1, 1), lambda b, i: (0, 0))] * n_sum
    out_shape = [_sds((bl, s, rows[k][1]), rows[k][3]) for k in grad_rows]
    out_shape += [_sds(p.shape, F32) for p in bparams] + [_sds(p.shape, F32) for p in gparams]
    out_specs = [pl.BlockSpec((1, tm, rows[k][1]), lambda b, i: (b, i, 0)) for k in grad_rows]
    out_specs += [pl.BlockSpec((1, 1, p.shape[-1]), lambda b, i: (b, 0, 0)) for p in bparams]
    out_specs += [pl.BlockSpec(p.shape, lambda b, i, n=p.ndim: (0,) * n) for p in gparams]
    res = _pcall(
        body, name, out_shape, grid=(bl, s // tm), in_specs=in_specs, out_specs=out_specs, sem=("arbitrary", "arbitrary"),
    )(*[r[0] for r in rows], *bparams, *gparams, *cts)
    return res[:len(grad_rows)], res[len(grad_rows):len(grad_rows) + n_b], res[len(grad_rows) + n_b:]


def _nm_tile(x, sc, sh, g):
    return (_rms(x, g) * (1.0 + sc) + sh,)


def _nm_res_tile(x, sc, sh, g):
    return x, _rms(x, g) * (1.0 + sc) + sh


def _rnm_tile(x, o, gate, sc, sh, g):
    xn = x + gate * o
    return xn, _rms(xn, g) * (1.0 + sc) + sh


def _final_tile(x, o, tgt, gate, g):
    e = _rms(x + gate * o, g) - tgt
    return (0.5 * jnp.sum(jnp.mean(e * e, axis=-1, keepdims=True), axis=0, keepdims=True),)


def _gm_tile(u_in, v_in, vg, ws, bs_t, og):
    d = u_in.shape[1]
    u, vn = _gelu(u_in), _rms(_gelu(v_in), vg)
    causal = _causal(CHUNK)
    lane = lax.broadcasted_iota(jnp.int32, (1, LANES), 1)
    parts = []
    for h in range(d // GM_HEAD):
        bias = jnp.sum(bs_t * (lane == h).astype(F32), axis=1, keepdims=True)
        parts.append(_dot_nn(jnp.where(causal, ws[h], 0.0), vn[:, h * GM_HEAD:(h + 1) * GM_HEAD]) + bias)
    return (_rms(u * jnp.concatenate(parts, axis=1), og),)


def _conv_window(ref, r0, rows, c0, tc, seq, before, after):
    parts = []
    if before:
        p0 = pl.multiple_of(jnp.maximum(r0 - SUBLANES, 0), SUBLANES)
        parts.append(jnp.where(r0 > 0, ref[0, pl.ds(p0, SUBLANES), pl.ds(c0, tc)].astype(F32), 0.0))
    parts.append(ref[0, pl.ds(r0, rows), pl.ds(c0, tc)].astype(F32))
    if after:
        n0 = pl.multiple_of(jnp.minimum(r0 + rows, seq - SUBLANES), SUBLANES)
        parts.append(jnp.where(r0 + rows < seq, ref[0, pl.ds(n0, SUBLANES), pl.ds(c0, tc)].astype(F32), 0.0))
    return jnp.concatenate(parts, axis=0) if len(parts) > 1 else parts[0]


def _conv_taps_pre(xe, w, b, taps, rows):
    pre = xe[SUBLANES:SUBLANES + rows] * w[taps - 1] + b
    for j in range(1, taps):
        pre = pre + pltpu.roll(xe, j, 0)[SUBLANES:SUBLANES + rows] * w[taps - 1 - j]
    return pre


def _conv_fwd(name, src, col0, chans, w8, b, taps, gated, out_dtype):
    bl, s, _ = src.shape
    tc = CONV_LANES
    xw = 2 * tc if gated else tc

    def body(x_ref, w_ref, b_ref, o_ref):
        w = [w_ref[k:k + 1, :] for k in range(taps)]
        bias = b_ref[...]

        def step(c, carry):
            r0 = pl.multiple_of(c * CONV_ROWS, CONV_ROWS)
            xe = _conv_window(x_ref, r0, CONV_ROWS, 0, tc, s, True, False)
            pre = _conv_taps_pre(xe, w, bias, taps, CONV_ROWS)
            y = pre * _sigmoid(pre)
            if gated:
                y = y * x_ref[0, pl.ds(r0, CONV_ROWS), pl.ds(tc, tc)]
            o_ref[0, pl.ds(r0, CONV_ROWS), :] = y.astype(o_ref.dtype)
            return carry

        lax.fori_loop(0, s // CONV_ROWS, step, 0, unroll=4)

    first = 0 if gated else col0 // tc
    return _pcall(
        body, name, _sds((bl, s, chans), out_dtype), grid=(chans // tc, bl),
        in_specs=[pl.BlockSpec((1, s, xw), lambda j, bb: (bb, 0, first + j)), pl.BlockSpec((SUBLANES, tc), lambda j, bb: (0, j)), pl.BlockSpec((1, tc), lambda j, bb: (0, j))],
        out_specs=pl.BlockSpec((1, s, tc), lambda j, bb: (bb, 0, j)), sem=("parallel", "arbitrary"),
    )(src, w8, b)


def _conv_bwd(name, src, col0, chans, w8, b, taps, gated, dy, dx_dtype):
    bl, s, _ = src.shape
    tc = CONV_LANES
    xw = 2 * tc if gated else tc
    ext = CONV_ROWS + SUBLANES

    def body(x_ref, w_ref, b_ref, dy_ref, dx_ref, dw_ref, db_ref):
        bb = pl.program_id(1)
        w = [w_ref[k:k + 1, :] for k in range(taps)]
        bias = b_ref[...]

        def fold(v):
            acc = v[0:SUBLANES]
            for i in range(1, CONV_ROWS // SUBLANES):
                acc = acc + v[i * SUBLANES:(i + 1) * SUBLANES]
            return acc

        def step(c, carry):
            r0 = pl.multiple_of(c * CONV_ROWS, CONV_ROWS)
            xe = _conv_window(x_ref, r0, CONV_ROWS, 0, tc, s, True, True)
            pre = _conv_taps_pre(xe, w, bias, taps, ext)
            sig = _sigmoid(pre)
            d = _conv_window(dy_ref, r0, CONV_ROWS, 0, tc, s, False, True)
            dsil = d * _conv_window(x_ref, r0, CONV_ROWS, tc, tc, s, False, True) if gated else d
            dpre = dsil * sig * (1.0 + pre * (1.0 - sig))
            dx = dpre[:CONV_ROWS] * w[taps - 1]
            for j in range(1, taps):
                dx = dx + pltpu.roll(dpre, ext - j, 0)[:CONV_ROWS] * w[taps - 1 - j]
            dx_ref[0, pl.ds(r0, CONV_ROWS), pl.ds(0, tc)] = dx.astype(dx_ref.dtype)
            if gated:
                dx_ref[0, pl.ds(r0, CONV_ROWS), pl.ds(tc, tc)] = (d[:CONV_ROWS] * (pre * sig)[:CONV_ROWS]).astype(dx_ref.dtype)
            here = dpre[:CONV_ROWS]
            sums = [fold(here * (pltpu.roll(xe, taps - 1 - k, 0) if k < taps - 1 else xe)[SUBLANES:ext]) + carry[k] for k in range(taps)]
            return tuple(sums) + (fold(here) + carry[taps],)

        zero = jnp.zeros((SUBLANES, tc), F32)
        sums = lax.fori_loop(0, s // CONV_ROWS, step, (zero,) * (taps + 1), unroll=2)
        rows = [jnp.sum(t, axis=0, keepdims=True) for t in sums]
        dw = jnp.concatenate(rows[:taps] + [jnp.zeros_like(rows[0])] * (SUBLANES - taps), axis=0)

        @pl.when(bb == 0)
        def _():
            dw_ref[...] = dw
            db_ref[...] = rows[taps]

        @pl.when(bb > 0)
        def _():
            dw_ref[...] += dw
            db_ref[...] += rows[taps]

    first = 0 if gated else col0 // tc
    out_shape = [_sds((bl, s, chans * (2 if gated else 1)), dx_dtype), _sds((SUBLANES, chans), F32), _sds((1, chans), F32)]
    return _pcall(
        body, name, out_shape, grid=(chans // tc, bl),
        in_specs=[
            pl.BlockSpec((1, s, xw), lambda j, bb: (bb, 0, first + j)), pl.BlockSpec((SUBLANES, tc), lambda j, bb: (0, j)),
            pl.BlockSpec((1, tc), lambda j, bb: (0, j)), pl.BlockSpec((1, s, tc), lambda j, bb: (bb, 0, j)),
        ],
        out_specs=[pl.BlockSpec((1, s, xw), lambda j, bb: (bb, 0, j)), pl.BlockSpec((SUBLANES, tc), lambda j, bb: (0, j)), pl.BlockSpec((1, tc), lambda j, bb: (0, j))],
        sem=("parallel", "arbitrary"),
    )(src, w8, b, dy)


def _gate_value_blocks(name, w, f, inverse=False):
    nb = f // CONV_LANES
    src = (lambda j: (0, 2 * (j % nb) + j // nb)) if inverse else (lambda j: (0, (j % 2) * nb + j // 2))

    def body(x_ref, o_ref):
        o_ref[...] = x_ref[...]

    return _pcall(
        body, name, _sds(w.shape, w.dtype), grid=(2 * nb,), in_specs=[pl.BlockSpec((w.shape[0], CONV_LANES), src)],
        out_specs=pl.BlockSpec((w.shape[0], CONV_LANES), lambda j: (0, j)), sem=("parallel",),
    )(w)


def _ssd_chunk(g_idx, states, xs, bm, cm, dtr, z, dtb, alog, dsk, ng):
    gw = xs.shape[1]
    hpg = gw // HEAD_DIM
    dt = jax.nn.softplus(dtr + dtb)
    da = dt * (-jnp.exp(alog))
    causal = _causal(CHUNK)
    acs = _sel_left(causal.astype(F32), da)
    head_of_col = g_idx * hpg + lax.broadcasted_iota(jnp.int32, (LANES, gw), 1) // HEAD_DIM
    expand = (lax.broadcasted_iota(jnp.int32, (LANES, gw), 0) == head_of_col).astype(F32)
    dt_e, acs_e = _sel_right(dt, expand), _sel_right(acs, expand)
    last = lax.broadcasted_iota(jnp.int32, (CHUNK, gw), 0) == CHUNK - 1
    alast_e = jnp.sum(jnp.where(last, acs_e, 0.0), axis=0, keepdims=True)
    xc = xs * dt_e
    xc_st = xc * jnp.exp(alast_e - acs_e)
    decay_out, chunk_decay = jnp.exp(acs_e), jnp.exp(alast_e)
    cb = _dot_nt(cm, bm)
    acs_t = acs.T
    lane = lax.broadcasted_iota(jnp.int32, (1, LANES), 1)
    sub = lax.broadcasted_iota(jnp.int32, (LANES, 1), 0)
    ys, new_states = [], []
    for p in range(gw // LANES):
        sl = slice(p * LANES, (p + 1) * LANES)
        xcp = xc[:, sl]
        y = _dot_nn(cm, states[p]) * decay_out[:, sl]
        for q in range(2):
            head = g_idx * hpg + 2 * p + q
            col = jnp.sum(acs * (lane == head).astype(F32), axis=1, keepdims=True)
            row = jnp.sum(acs_t * (sub == head).astype(F32), axis=0, keepdims=True)
            decay = jnp.where(causal, jnp.exp(jnp.where(causal, col - row, 0.0)), 0.0)
            half = ((lane // HEAD_DIM) == q).astype(F32)
            y = y + _dot_nn(cb * decay, xcp * half)
        ys.append(y)
        new_states.append(states[p] * chunk_decay[:, sl] + _dot_tn(bm, xc_st[:, sl]))
    y = jnp.concatenate(ys, axis=1) + dsk * xs
    gated = y * (z * _sigmoid(z))
    return tuple(new_states), _rms(gated, ng)


def _ssd_specs(d, gw, dt_col, rev, nc):
    ci = (lambda i: nc - 1 - i) if rev else (lambda i: i)
    return [
        pl.BlockSpec((1, CHUNK, gw), lambda g, b, i: (b, ci(i), g)),
        pl.BlockSpec((1, CHUNK, STATE), lambda g, b, i: (b, ci(i), d // STATE + g)),
        pl.BlockSpec((1, CHUNK, STATE), lambda g, b, i: (b, ci(i), d // STATE + SSD_GROUPS + g)),
        pl.BlockSpec((1, CHUNK, LANES), lambda g, b, i: (b, ci(i), dt_col // LANES)),
        pl.BlockSpec((1, CHUNK, gw), lambda g, b, i: (b, ci(i), g)),
        pl.BlockSpec((1, LANES), lambda g, b, i: (0, 0)), pl.BlockSpec((1, LANES), lambda g, b, i: (0, 0)),
        pl.BlockSpec((1, gw), lambda g, b, i: (0, g)), pl.BlockSpec((1, gw), lambda g, b, i: (0, g)),
    ]


def _ssd_fwd(name, xbc, proj, dt_col, dtb, alog, dsk, ng):
    bl, s, cd = xbc.shape
    d = dsk.shape[1]
    gw, nc = d // SSD_GROUPS, s // CHUNK
    npair = gw // LANES

    def body(xs_ref, bm_ref, cm_ref, dt_ref, z_ref, dtb_ref, alog_ref, dsk_ref, ng_ref, y_ref, hp_ref, st_ref):
        g, i = pl.program_id(0), pl.program_id(2)

        @pl.when(i == 0)
        def _():
            st_ref[...] = jnp.zeros_like(st_ref)

        states = tuple(st_ref[p] for p in range(npair))
        hp_ref[0, 0, 0] = st_ref[...]
        new_states, yn = _ssd_chunk(g, states, xs_ref[0], bm_ref[0], cm_ref[0], dt_ref[0], z_ref[0], dtb_ref[...], alog_ref[...], dsk_ref[...], ng_ref[...])
        for p in range(npair):
            st_ref[p] = new_states[p]
        y_ref[0] = yn.astype(y_ref.dtype)

    out_shape = [_sds((bl, s, d), BF16), _sds((SSD_GROUPS, bl, nc, npair, STATE, LANES), F32)]
    out_specs = [
        pl.BlockSpec((1, CHUNK, gw), lambda g, b, i: (b, i, g)),
        pl.BlockSpec((1, 1, 1, npair, STATE, LANES), lambda g, b, i: (g, b, i, 0, 0, 0)),
    ]
    return _pcall(
        body, name, out_shape, grid=(SSD_GROUPS, bl, nc), in_specs=_ssd_specs(d, gw, dt_col, False, nc), out_specs=out_specs,
        scratch=[pltpu.VMEM((npair, STATE, LANES), F32)], sem=("arbitrary", "arbitrary", "arbitrary"),
    )(xbc, xbc, xbc, proj, proj, dtb, alog, dsk, ng)


def _ssd_bwd(name, xbc, proj, dt_col, dtb, alog, dsk, ng, hprev, dy):
    bl, s, cd = xbc.shape
    d = dsk.shape[1]
    gw, nc = d // SSD_GROUPS, s // CHUNK
    npair = gw // LANES

    def body(xs_ref, bm_ref, cm_ref, dt_ref, z_ref, dtb_ref, alog_ref, dsk_ref, ng_ref, hp_ref, dy_ref,
             dxs_ref, dbm_ref, dcm_ref, ddt_ref, dz_ref, ddtb_ref, dalog_ref, ddsk_ref, dng_ref, dst_ref):
        g, b, i = pl.program_id(0), pl.program_id(1), pl.program_id(2)

        @pl.when(i == 0)
        def _():
            dst_ref[...] = jnp.zeros_like(dst_ref)

        states = tuple(hp_ref[0, 0, 0, p] for p in range(npair))
        step = functools.partial(_ssd_chunk, g)
        _, vjp = jax.vjp(step, states, xs_ref[0], bm_ref[0], cm_ref[0], dt_ref[0], z_ref[0], dtb_ref[...], alog_ref[...], dsk_ref[...], ng_ref[...])
        d_states, dxs, dbm, dcm, ddt, dz, ddtb, dalog, ddsk, dng = vjp((tuple(dst_ref[p] for p in range(npair)), dy_ref[0].astype(F32)))
        for p in range(npair):
            dst_ref[p] = d_states[p]
        dxs_ref[0], dbm_ref[0], dcm_ref[0] = dxs, dbm, dcm
        dz_ref[0] = dz.astype(dz_ref.dtype)
        ddt_ref[0, 0] = ddt
        first_g = (b == 0) & (i == 0)
        first = first_g & (g == 0)
        for o_ref, val, init in ((ddtb_ref, ddtb, first), (dalog_ref, dalog, first), (ddsk_ref, ddsk, first_g), (dng_ref, dng, first_g)):
            @pl.when(init)
            def _(o_ref=o_ref, val=val):
                o_ref[...] = val

            @pl.when(jnp.logical_not(init))
            def _(o_ref=o_ref, val=val):
                o_ref[...] += val

    rc = lambda i: nc - 1 - i
    in_specs = _ssd_specs(d, gw, dt_col, True, nc) + [
        pl.BlockSpec((1, 1, 1, npair, STATE, LANES), lambda g, b, i: (g, b, rc(i), 0, 0, 0)),
        pl.BlockSpec((1, CHUNK, gw), lambda g, b, i: (b, rc(i), g)),
    ]
    out_shape = [
        _sds((bl, s, d), F32), _sds((bl, s, SSD_GROUPS * STATE), F32), _sds((bl, s, SSD_GROUPS * STATE), F32),
        _sds((SSD_GROUPS, bl, s, LANES), F32), _sds((bl, s, d), BF16),
        _sds((1, LANES), F32), _sds((1, LANES), F32), _sds((1, d), F32), _sds((1, d), F32),
    ]
    out_specs = [
        pl.BlockSpec((1, CHUNK, gw), lambda g, b, i: (b, rc(i), g)),
        pl.BlockSpec((1, CHUNK, STATE), lambda g, b, i: (b, rc(i), g)), pl.BlockSpec((1, CHUNK, STATE), lambda g, b, i: (b, rc(i), g)),
        pl.BlockSpec((1, 1, CHUNK, LANES), lambda g, b, i: (g, b, rc(i), 0)),
        pl.BlockSpec((1, CHUNK, gw), lambda g, b, i: (b, rc(i), g)),
        pl.BlockSpec((1, LANES), lambda g, b, i: (0, 0)), pl.BlockSpec((1, LANES), lambda g, b, i: (0, 0)),
        pl.BlockSpec((1, gw), lambda g, b, i: (0, g)), pl.BlockSpec((1, gw), lambda g, b, i: (0, g)),
    ]
    return _pcall(
        body, name, out_shape, grid=(SSD_GROUPS, bl, nc), in_specs=in_specs, out_specs=out_specs,
        scratch=[pltpu.VMEM((npair, STATE, LANES), F32)], sem=("arbitrary", "arbitrary", "arbitrary"),
    )(xbc, xbc, xbc, proj, proj, dtb, alog, dsk, ng, hprev, dy)


def _mixer_fwd(name, proj, d, cd, conv_w8, conv_b, dtb, alog, dsk, ng, vg, ws, bs_t, og):
    xbc = _conv_fwd(name + "_conv_fwd", proj, 3 * d, cd, conv_w8, conv_b, SSD_CONV, False, F32)
    y, hprev = _ssd_fwd(name + "_ssd_fwd", xbc, proj, 3 * d + cd, dtb, alog, dsk, ng)
    rows = [(proj, d, 1, BF16), (proj, d, 2, BF16)]
    (g_out,) = _row_fwd(name + "_gm_fwd", _gm_tile, rows, [], [vg, ws, bs_t, og], [(d, BF16)], 0, CHUNK)
    return jnp.concatenate([y, g_out], axis=-1), (xbc, hprev)


def _mixer_bwd(name, proj, d, cd, conv_w8, conv_b, dtb, alog, dsk, ng, vg, ws, bs_t, og, xbc, hprev, dycat):
    bl, s, n_proj = proj.shape
    dt_col = 3 * d + cd
    dy, dg_out = dycat[..., :d], dycat[..., d:]
    dxs, dbm, dcm, ddt2, dz, ddtb, dalog, ddsk, dng = _ssd_bwd(name + "_ssd_bwd", xbc, proj, dt_col, dtb, alog, dsk, ng, hprev, dy)
    dxbc_act = jnp.concatenate([dxs, dbm, dcm], axis=-1)
    dxbc, dw8, dcb = _conv_bwd(name + "_conv_bwd", proj, 3 * d, cd, conv_w8, conv_b, SSD_CONV, False, dxbc_act, BF16)
    rows = [(proj, d, 1, BF16), (proj, d, 2, BF16)]
    (du, dv), _, (dvg, dws, dbs_t, dog) = _row_bwd(name + "_gm_bwd", _gm_tile, rows, [], [vg, ws, bs_t, og], [(d, BF16)], 0, CHUNK, [dg_out])
    ddt = (ddt2[0] + ddt2[1]).astype(BF16)
    pad = jnp.zeros((bl, s, n_proj - dt_col - LANES), BF16)
    dproj = jnp.concatenate([dz, du, dv, dxbc, ddt, pad], axis=-1)
    return dproj, (dw8, dcb, ddtb, dalog, ddsk, dng, dvg, dws, dbs_t, dog)


def _position():
    return lax.axis_index("x"), lax.axis_index("y"), lax.axis_index("c")


def _at(ref, idx):
    return ref.at[idx] if len(idx) else ref


def _exchange(name, inputs, out_shapes, plan, inplace=False):
    if inplace:
        out_shapes = [(a.shape, a.dtype) for a in inputs]
    n_in, n_out = len(inputs), len(out_shapes)
    n_copy = len(plan(0, 0, 0))

    def body(*refs):
        in_refs, out_refs, token = refs[:n_in], refs[n_in:n_in + n_out], refs[n_in + n_out]
        send_sems, recv_sems = refs[n_in + n_out + 1:]
        token[...] = jnp.zeros_like(token)
        x, y, c = _position()
        copies = plan(x, y, c)

        def copy(k, src, dst, peer):
            return pltpu.make_async_remote_copy(src_ref=src, dst_ref=dst, send_sem=send_sems.at[k], recv_sem=recv_sems.at[k], device_id=peer, device_id_type=MESH)

        src_refs = out_refs if inplace else in_refs
        sends = [copy(k, _at(src_refs[sa], si), _at(out_refs[da], di), peer) for k, (sa, si, da, di, peer, _) in enumerate(copies)]
        for cp in sends:
            cp.start()
        for k, (sa, si, da, _, peer, li) in enumerate(copies):
            copy(k, _at(src_refs[sa], si), _at(out_refs[da], li), peer).wait_recv()
        for cp in sends:
            cp.wait_send()

    any_spec = pl.BlockSpec(memory_space=pl.ANY)
    outs = pl.pallas_call(
        body, name=name, out_shape=[_sds(s, dt) for s, dt in out_shapes] + [_sds((SUBLANES, LANES), F32)], in_specs=[any_spec] * n_in,
        out_specs=[any_spec] * n_out + [pl.BlockSpec(memory_space=pltpu.VMEM)],
        scratch_shapes=[pltpu.SemaphoreType.DMA((n_copy,)), pltpu.SemaphoreType.DMA((n_copy,))],
        input_output_aliases={i: i for i in range(n_in)} if inplace else {},
    )(*inputs)
    return list(outs[:n_out]), outs[n_out]


def _exchange_start(name, inputs, out_shapes, plan):
    n_in, n_out = len(inputs), len(out_shapes)
    n_copy = len(plan(0, 0, 0))

    def body(*refs):
        in_refs, land_refs = refs[:n_in], refs[n_in:n_in + n_out]
        send_sems, recv_sems = refs[n_in + n_out:n_in + n_out + 2]
        token = refs[-1]
        x, y, c = _position()
        for k, (sa, si, da, di, peer, _) in enumerate(plan(x, y, c)):
            pltpu.make_async_remote_copy(
                src_ref=_at(in_refs[sa], si), dst_ref=_at(land_refs[da], di), send_sem=send_sems.at[k], recv_sem=recv_sems.at[k],
                device_id=peer, device_id_type=MESH).start()
        token[...] = jnp.zeros_like(token)

    hbm, sem = pl.BlockSpec(memory_space=pltpu.HBM), pl.BlockSpec(memory_space=pltpu.SEMAPHORE)
    lands = [lax.empty(s, dt) for s, dt in out_shapes]
    args = [pltpu.with_memory_space_constraint(a, pltpu.HBM) for a in list(inputs) + lands]
    outs = pl.pallas_call(
        body, name=name,
        out_shape=(pltpu.SemaphoreType.DMA((n_copy,)), pltpu.SemaphoreType.DMA((n_copy,)), *[pltpu.HBM(a.shape, a.dtype) for a in args], _sds((SUBLANES, LANES), F32)),
        in_specs=[hbm] * (n_in + n_out), out_specs=(sem, sem, *[hbm] * (n_in + n_out), pl.BlockSpec(memory_space=pltpu.VMEM)),
        input_output_aliases={i: 2 + i for i in range(n_in + n_out)},
        compiler_params=pltpu.CompilerParams(has_side_effects=pltpu.SideEffectType.DATAFLOW_SIDE_EFFECTING),
    )(*args)
    return dict(name=name, plan=plan, sems=outs[:2], ins=list(outs[2:2 + n_in]), lands=list(outs[2 + n_in:2 + n_in + n_out]), token=outs[-1])


def _exchange_wait(started, after):
    plan, n_in, n_out = started["plan"], len(started["ins"]), len(started["lands"])

    def body(*refs):
        in_refs, land_refs = refs[:n_in], refs[n_in:n_in + n_out]
        send_sems, recv_sems = refs[n_in + n_out:n_in + n_out + 2]
        token = refs[-1]
        x, y, c = _position()
        for k, (sa, si, da, _, peer, li) in enumerate(plan(x, y, c)):
            cp = pltpu.make_async_remote_copy(
                src_ref=_at(in_refs[sa], si), dst_ref=_at(land_refs[da], li), send_sem=send_sems.at[k], recv_sem=recv_sems.at[k],
                device_id=peer, device_id_type=MESH)
            cp.wait_send()
            cp.wait_recv()
        token[...] = jnp.zeros_like(token)

    hbm, sem = pl.BlockSpec(memory_space=pltpu.HBM), pl.BlockSpec(memory_space=pltpu.SEMAPHORE)
    bufs = started["ins"] + started["lands"]
    outs = pl.pallas_call(
        body, name=started["name"] + "_wait", out_shape=(*[pltpu.HBM(a.shape, a.dtype) for a in bufs], _sds((SUBLANES, LANES), F32)),
        in_specs=[hbm] * len(bufs) + [sem, sem, pl.BlockSpec(memory_space=pl.ANY)], out_specs=(*[hbm] * len(bufs), pl.BlockSpec(memory_space=pltpu.VMEM)),
        input_output_aliases={i: i for i in range(len(bufs))},
        compiler_params=pltpu.CompilerParams(has_side_effects=pltpu.SideEffectType.DATAFLOW_SIDE_EFFECTING),
    )(*bufs, *started["sems"], after)
    return list(outs[:n_in]), list(outs[n_in:n_in + n_out]), outs[-1]


def _after(value, token):
    return value + token[0, 0].astype(value.dtype)


def _chip_peers(x, y, c):
    return [(1 - x, y, c), (x, 1 - y, c), (1 - x, 1 - y, c)]


def _chip_of(p):
    return 2 * p[0] + p[1]


def _set_slot(slots, me, blk):
    return lax.dynamic_update_slice(slots, blk[None], (me,) + (0,) * blk.ndim)


def _by_core(c, mine, other, axis):
    return jnp.where(c == 0, jnp.stack([mine, other], axis), jnp.stack([other, mine], axis))


def _plan_gather_chips(n):
    def plan(x, y, c):
        return [(a, (), a, (2 * x + y,), p, (_chip_of(p),)) for a in range(n) for p in _chip_peers(x, y, c)]

    return plan


def _gather_chips(name, blocks):
    recv, token = _exchange(name, blocks, [((4,) + b.shape, b.dtype) for b in blocks], _plan_gather_chips(len(blocks)))
    x, y, _ = _position()
    return [_set_slot(r, 2 * x + y, b) for r, b in zip(recv, blocks)], token


def _gather_pass_cores(name, blocks, from_chips):
    n = len(blocks)

    def plan_cores(x, y, c):
        me, sib = 2 * x + y, (x, y, 1 - c)
        own = [(a, (), a, (me,), sib, (me,)) for a in range(n)]
        passed = [(n + a, (_chip_of(p),), a, (_chip_of(p),), sib, (_chip_of(p),)) for a in range(n) for p in _chip_peers(x, y, c)]
        return own + passed

    from_core, token = _exchange(name + "_cores", list(blocks) + list(from_chips), [((4,) + b.shape, b.dtype) for b in blocks], plan_cores)
    x, y, c = _position()
    return [_by_core(c, _set_slot(r1, 2 * x + y, b), r2, 1) for b, r1, r2 in zip(blocks, from_chips, from_core)], token


def _gather_two_level(name, blocks):
    from_chips, _ = _exchange(name + "_chips", blocks, [((4,) + b.shape, b.dtype) for b in blocks], _plan_gather_chips(len(blocks)))
    return _gather_pass_cores(name, blocks, from_chips)


def _pair_add(name, g42, r4):
    _, _, rh, cols = g42.shape
    tr = _divisors(rh, 512, SUBLANES * 2)[-1]

    def body(c_ref, a_ref, b_ref, o_ref):
        o_ref[0] = (a_ref[0, 0].astype(F32) + b_ref[0].astype(F32)).astype(o_ref.dtype)

    cidx = lax.axis_index("c").astype(jnp.int32).reshape(1)
    return _pcall(
        body, name, _sds(r4.shape, BF16), grid=(4, rh // tr),
        in_specs=[pl.BlockSpec((1, 1, tr, cols), lambda s, i, c_ref: (s, c_ref[0], i, 0)), pl.BlockSpec((1, tr, cols), lambda s, i, c_ref: (s, i, 0))],
        out_specs=pl.BlockSpec((1, tr, cols), lambda s, i, c_ref: (s, i, 0)), sem=("parallel", "parallel"), prefetch=1,
    )(cidx, g42, r4)


def _slot_sum(name, parts):
    n, r, cols = parts.shape
    cap = max(2 * SUBLANES, (4 * 1024 * 1024) // (n * cols * parts.dtype.itemsize))
    tr = _divisors(r, cap, 2 * SUBLANES)[-1]

    def body(p_ref, o_ref):
        acc = p_ref[0].astype(F32)
        for k in range(1, n):
            acc = acc + p_ref[k].astype(F32)
        o_ref[...] = acc

    return _pcall(
        body, name, _sds((r, cols), F32), grid=(r // tr,), in_specs=[pl.BlockSpec((n, tr, cols), lambda i: (0, i, 0))],
        out_specs=pl.BlockSpec((tr, cols), lambda i: (i, 0)), sem=("parallel",),
    )(parts)


def _slot_sums(name, parts):
    k = len(parts)

    def body(*refs):
        for p_ref, o_ref in zip(refs[:k], refs[k:]):
            acc = p_ref[0]
            for j in range(1, p_ref.shape[0]):
                acc = acc + p_ref[j]
            o_ref[...] = acc

    return list(_pcall(body, name, [_sds(p.shape[1:], F32) for p in parts])(*parts))


def _gather_weights_start(name, shards, token):
    c = lax.axis_index("c")
    halves = [lax.dynamic_slice_in_dim(w, c * (w.shape[0] // 2), w.shape[0] // 2, 0).astype(BF16) for w in shards]
    if token is not None:
        halves[0] = _after(halves[0], token)
    n = len(halves)

    def plan(x, y, c):
        return [(a, (), a, (2 * x + y, c), p, (_chip_of(p), c)) for a in range(n) for p in _chip_peers(x, y, c)]

    return _exchange_start(name + "_ag_chips", halves, [((4, 2) + h.shape, h.dtype) for h in halves], plan)


def _gather_weights_finish(name, started, after):
    halves, slots, token = _exchange_wait(started, after)
    n = len(halves)
    x, y, c = _position()
    slots = [lax.dynamic_update_slice(s, h[None, None], (2 * x + y, c, 0, 0)) for s, h in zip(slots, halves)]

    def plan_cores(x, y, c):
        return [(a, (s, c), a, (s, c), (x, y, 1 - c), (s, 1 - c)) for a in range(n) for s in range(4)]

    full, _ = _exchange(name + "_ag_cores", slots, None, plan_cores, inplace=True)
    return [f.reshape((4, 2 * h.shape[0], h.shape[1])) for f, h in zip(full, halves)], token


def _reduce_weights_start(name, grads, token):
    n = len(grads)
    g42 = [g.reshape(4, 2, g.shape[1] // 2, g.shape[2]) for g in grads]
    if token is not None:
        g42[0] = _after(g42[0], token)

    def plan_swap(x, y, c):
        return [(a, (s, 1 - c), a, (s,), (x, y, 1 - c), (s,)) for a in range(n) for s in range(4)]

    def plan_chips(x, y, c):
        return [(a, (_chip_of(p),), a, (2 * x + y,), p, (_chip_of(p),)) for a in range(n) for p in _chip_peers(x, y, c)]

    other, _ = _exchange(name + "_rs_cores", g42, [((4,) + g.shape[2:], g.dtype) for g in g42], plan_swap)
    pair = [_pair_add(f"{name}_rs_pair{a}", g, o) for a, (g, o) in enumerate(zip(g42, other))]
    return _exchange_start(name + "_rs_chips", pair, [(p.shape, p.dtype) for p in pair], plan_chips)


def _reduce_weights_finish(name, started, after):
    pair, recv, token = _exchange_wait(started, after)
    n = len(pair)
    x, y, c = _position()
    me = 2 * x + y

    def plan_share(x, y, c):
        return [(a, (c,), a, (c,), (x, y, 1 - c), (1 - c,)) for a in range(n)]

    parts = [lax.dynamic_update_slice(r, lax.dynamic_slice_in_dim(p, me, 1, 0), (me, 0, 0)) for r, p in zip(recv, pair)]
    mine = [_slot_sum(f"{name}_rs_sum{a}", p) for a, p in enumerate(parts)]
    both = [lax.dynamic_update_slice(lax.empty((2,) + m.shape, m.dtype), m[None], (c, 0, 0)) for m in mine]
    both, _ = _exchange(name + "_rs_share", both, None, plan_share, inplace=True)
    return [b.reshape(2 * b.shape[1], b.shape[2]) for b in both], token


def _allreduce_small(name, grads):
    both, _ = _gather_two_level(name + "_ag", list(grads))
    return _slot_sums(name + "_sum", [g.reshape((8,) + g.shape[2:]) for g in both])


def _ada_fwd_call(name, c_all, w, b_shard):
    nl, d, ns = w.shape
    nb = c_all.shape[0]

    def body(c_ref, w_ref, b_ref, o_ref):
        cv = c_ref[...]
        o_ref[0] = _dg(cv * _sigmoid(cv), w_ref[0], ((1,), (0,))) + b_ref[0]

    return _pcall(
        body, name, _sds((nl, nb, ns), F32), grid=(nl,),
        in_specs=[pl.BlockSpec((nb, d), lambda l: (0, 0)), pl.BlockSpec((1, d, ns), lambda l: (l, 0, 0)), pl.BlockSpec((1, 1, ns), lambda l: (l, 0, 0))],
        out_specs=pl.BlockSpec((1, nb, ns), lambda l: (l, 0, 0)), sem=("parallel",),
    )(c_all, w, b_shard)


def _ada_bwd_call(name, c_all, dm_shard, dm_all):
    nl, nb, ns = dm_shard.shape
    d = c_all.shape[1]
    nm = dm_all.shape[2]

    def body(c_ref, ds_ref, da_ref, dw_ref, db_ref):
        cv = c_ref[...]
        dw_ref[0] = _dg(cv * _sigmoid(cv), ds_ref[0], ((0,), (0,)))
        db_ref[0] = jnp.sum(da_ref[0], axis=0, keepdims=True)

    return _pcall(
        body, name, [_sds((nl, d, ns), F32), _sds((nl, 1, nm), F32)], grid=(nl,),
        in_specs=[pl.BlockSpec((nb, d), lambda l: (0, 0)), pl.BlockSpec((1, nb, ns), lambda l: (l, 0, 0)), pl.BlockSpec((1, nb, nm), lambda l: (l, 0, 0))],
        out_specs=[pl.BlockSpec((1, d, ns), lambda l: (l, 0, 0)), pl.BlockSpec((1, 1, nm), lambda l: (l, 0, 0))], sem=("parallel",),
    )(c_all, dm_shard, dm_all)


def _ada_fwd(name, bl, c_all, w, b):
    nl, d, ns = w.shape
    chip = 2 * lax.axis_index("x") + lax.axis_index("y")
    b_shard = lax.dynamic_slice(b, (0, chip * ns), (nl, ns)).reshape(nl, 1, ns)
    shard = _ada_fwd_call(name + "_fwd", c_all, w, b_shard)
    (allc,), token = _gather_chips(name + "_ag", [shard])
    mods = jnp.transpose(allc, (1, 2, 0, 3)).reshape(nl, c_all.shape[0], 4 * ns)
    return lax.dynamic_slice(mods, (0, (2 * chip + lax.axis_index("c")) * bl, 0), (nl, bl, 4 * ns)), token


def _ada_bwd(name, bl, c_all, ns, dm):
    nl = dm.shape[0]
    chip = 2 * lax.axis_index("x") + lax.axis_index("y")
    (dm_all,), _ = _gather_two_level(name + "_bwd_ag", [dm])
    dm_all = jnp.transpose(dm_all.reshape((8,) + dm.shape), (1, 0, 2, 3)).reshape(nl, 8 * bl, 4 * ns)
    dm_shard = lax.dynamic_slice(dm_all, (0, 0, chip * ns), (nl, 8 * bl, ns))
    dw, db = _ada_bwd_call(name + "_bwd", c_all, dm_shard, dm_all)
    return dw, db.reshape(nl, 4 * ns)


def _adamw(name, w, g, m, v):
    shape = w.shape
    cols = shape[-1]
    w2, g2, m2, v2 = (t.reshape(-1, cols) for t in (w, g, m, v))
    rows = w2.shape[0]
    cap = max(SUBLANES, (512 * 1024) // max(cols, 1) // SUBLANES * SUBLANES)
    tr = _divisors(rows, cap, SUBLANES)[-1]

    def body(w_ref, g_ref, m_ref, v_ref, d_ref, mo_ref, vo_ref):
        gv = g_ref[...]
        mn = ADAM_B1 * m_ref[...] + (1.0 - ADAM_B1) * gv
        vn = ADAM_B2 * v_ref[...] + (1.0 - ADAM_B2) * (gv * gv)
        m_hat = mn / (1.0 - ADAM_B1 ** ADAM_STEP)
        v_hat = vn / (1.0 - ADAM_B2 ** ADAM_STEP)
        d_ref[...] = -ADAM_LR * (m_hat / (jnp.sqrt(v_hat) + ADAM_EPS) + ADAM_WD * w_ref[...])
        mo_ref[...] = mn
        vo_ref[...] = vn

    spec = pl.BlockSpec((tr, cols), lambda i: (i, 0))
    outs = _pcall(body, name, [_sds((rows, cols), F32)] * 3, grid=(rows // tr,), in_specs=[spec] * 4, out_specs=[spec] * 3, sem=("parallel",))(w2, g2, m2, v2)
    return tuple(o.reshape(shape) for o in outs)


def _adamw_small(name, ws, gs, ms, vs):
    n = len(ws)

    def body(*refs):
        for k in range(n):
            w_ref, g_ref, m_ref, v_ref = (refs[j * n + k] for j in range(4))
            d_ref, mo_ref, vo_ref = (refs[(4 + j) * n + k] for j in range(3))
            gv = g_ref[...]
            mn = ADAM_B1 * m_ref[...] + (1.0 - ADAM_B1) * gv
            vn = ADAM_B2 * v_ref[...] + (1.0 - ADAM_B2) * (gv * gv)
            m_hat = mn / (1.0 - ADAM_B1 ** ADAM_STEP)
            v_hat = vn / (1.0 - ADAM_B2 ** ADAM_STEP)
            d_ref[...] = -ADAM_LR * (m_hat / (jnp.sqrt(v_hat) + ADAM_EPS) + ADAM_WD * w_ref[...])
            mo_ref[...] = mn
            vo_ref[...] = vn

    outs = _pcall(body, name, [_sds(w.shape, F32) for w in ws] * 3)(*ws, *gs, *ms, *vs)
    return outs[:n], outs[n:2 * n], outs[2 * n:]


def _pad_rows(w, rows):
    return jnp.pad(w, ((0, rows - w.shape[0]), (0, 0)))


def _pad_lanes(v):
    return jnp.pad(v, (0, LANES - v.shape[0])).reshape(1, LANES)


BIG = ("w_in", "w_out", "ff_up", "ff_down")
BIG_AXIS = {"w_in": 1, "w_out": 0, "ff_up": 1, "ff_down": 0}
CONVW = ("ssd_conv_w", "ff_conv_w")
SMALL = ("norm1_g", "norm2_g", "ssd_conv_b", "ssd_dt_bias", "ssd_a_log", "ssd_d", "ssd_norm_g", "gm_vnorm_g", "gm_ws", "gm_bs", "gm_out_g", "ff_conv_b")
WEIGHTS = ("ada_w", "ada_b", "norm1_g", "norm2_g", "w_in", "ssd_conv_w", "ssd_conv_b", "ssd_dt_bias", "ssd_a_log", "ssd_d", "ssd_norm_g", "gm_vnorm_g", "gm_ws", "gm_bs", "gm_out_g", "w_out", "ff_up", "ff_conv_w", "ff_conv_b", "ff_down", "final_g")


def kernel(x, c, ada_w, ada_b, norm1_g, norm2_g, w_in, ssd_conv_w, ssd_conv_b, ssd_dt_bias, ssd_a_log, ssd_d, ssd_norm_g, gm_vnorm_g, gm_ws, gm_bs, gm_out_g, w_out, ff_up, ff_conv_w, ff_conv_b, ff_down, final_g, loss_target, m_ada_w, m_ada_b, m_norm1_g, m_norm2_g, m_w_in, m_ssd_conv_w, m_ssd_conv_b, m_ssd_dt_bias, m_ssd_a_log, m_ssd_d, m_ssd_norm_g, m_gm_vnorm_g, m_gm_ws, m_gm_bs, m_gm_out_g, m_w_out, m_ff_up, m_ff_conv_w, m_ff_conv_b, m_ff_down, m_final_g, v_ada_w, v_ada_b, v_norm1_g, v_norm2_g, v_w_in, v_ssd_conv_w, v_ssd_conv_b, v_ssd_dt_bias, v_ssd_a_log, v_ssd_d, v_ssd_norm_g, v_gm_vnorm_g, v_gm_ws, v_gm_bs, v_gm_out_g, v_w_out, v_ff_up, v_ff_conv_w, v_ff_conv_b, v_ff_down, v_final_g):
    given = dict(locals())
    weights = {n: given[n] for n in WEIGHTS}
    bl, s, d = x.shape
    nl = ada_w.shape[0]
    heads = d // HEAD_DIM
    cd = d + 2 * SSD_GROUPS * STATE
    f = ff_down.shape[1] * 4
    n_in = d + cd + heads + 2 * d
    n_proj = _round_up(3 * d + cd + LANES, 2 * LANES)
    tm = _divisors(s, 512)[-1]

    pre, _ = _gather_two_level("pre_ag", [c] + [weights[n] for n in CONVW])
    c_all = pre[0].reshape(8 * bl, d)
    conv_full = {n: jnp.concatenate([p[k, 0] for k in range(4)], axis=-1) for n, p in zip(CONVW, pre[1:])}
    mods_all, ada_token = _ada_fwd("ada", bl, c_all, ada_w, ada_b)
    mods = mods_all.reshape(nl, bl, N_MOD, 1, d)

    gathering = {}

    def start_gathers(l, token):
        for n in BIG:
            gathering[l, n] = _gather_weights_start(f"wg{l}_{n}", [weights[n][l]], token)
            token = gathering[l, n]["token"]
        return token

    def landed(l, n, after):
        (full,), token = _gather_weights_finish(f"wg{l}_{n}", gathering.pop((l, n)), after)
        if n == "w_in":
            return full, token
        return (full.reshape(-1, full.shape[2]) if BIG_AXIS[n] == 0 else jnp.concatenate([full[k] for k in range(4)], axis=1)), token

    issued = start_gathers(0, ada_token)
    r2 = lambda v: v.reshape(1, -1)
    n_gm = d // GM_HEAD
    fg = r2(final_g)

    def layer_params(l):
        return dict(
            norm1=r2(norm1_g[l]), norm2=r2(norm2_g[l]), conv_w8=_pad_rows(conv_full["ssd_conv_w"][l], SUBLANES), conv_b=r2(ssd_conv_b[l]),
            dtb=_pad_lanes(ssd_dt_bias[l]), alog=_pad_lanes(ssd_a_log[l]), dsk=r2(jnp.repeat(ssd_d[l], HEAD_DIM)), ng=r2(ssd_norm_g[l]),
            vg=r2(gm_vnorm_g[l]), ws=gm_ws[l], bs_t=jnp.pad(gm_bs[l].T, ((0, 0), (0, LANES - n_gm))), og=r2(gm_out_g[l]),
            ff_w8=_pad_rows(conv_full["ff_conv_w"][l], SUBLANES), ff_b=r2(ff_conv_b[l]))

    def mixer_args(p):
        return (p["conv_w8"], p["conv_b"], p["dtb"], p["alog"], p["dsk"], p["ng"], p["vg"], p["ws"], p["bs_t"], p["og"])

    def padded_w_in(full):
        ns = full.shape[2]

        def cols(a, b):
            return [full[k][:, max(a, k * ns) - k * ns:min(b, (k + 1) * ns) - k * ns] for k in range(4) if max(a, k * ns) < min(b, (k + 1) * ns)]

        parts = cols(0, d) + cols(d + cd + heads, n_in) + cols(d, d + cd) + cols(d + cd, d + cd + heads)
        return jnp.concatenate(parts + [jnp.zeros((d, n_proj - (3 * d + cd + heads)), BF16)], axis=1)

    saved, xcur, pending = [], x, None
    for l in range(nl):
        p, w = layer_params(l), {}
        wi, token = landed(l, "w_in", mods_all if l == 0 else pending[0])
        w["w_in"] = padded_w_in(wi)
        sh1, sc1, g1, sh2, sc2, g2 = (mods[l, :, k] for k in range(N_MOD))
        if l + 1 < nl:
            issued = start_gathers(l + 1, token + issued)
        sc1 = _after(sc1, issued)
        if pending is None:
            (h,) = _row_fwd(f"nm{l}_fwd", _nm_tile, [(xcur, d, 0, None)], [sc1, sh1], [p["norm1"]], [(d, BF16)], 0, tm)
            x1 = xcur
        else:
            rows = [(xcur, d, 0, None), (pending[0], d, 0, None)]
            x1, h = _row_fwd(f"rnm{l}a_fwd", _rnm_tile, rows, [pending[1], sc1, sh1], [p["norm1"]], [(d, F32), (d, BF16)], 0, tm)
        proj = _mm(f"win{l}_fwd", h, w["w_in"], F32)
        ycat, (xbc, hprev) = _mixer_fwd(f"mix{l}", proj, d, cd, *mixer_args(p))
        w["w_out"], _ = landed(l, "w_out", ycat)
        mix = _mm(f"wout{l}_fwd", ycat, w["w_out"], F32)
        x2, h2 = _row_fwd(f"rnm{l}b_fwd", _rnm_tile, [(x1, d, 0, None), (mix, d, 0, None)], [g1, sc2, sh2], [p["norm2"]], [(d, F32), (d, BF16)], 0, tm)
        w_up, _ = landed(l, "ff_up", h2)
        w["ff_up"] = _gate_value_blocks(f"ffup{l}_blocks", w_up, f)
        up = _mm(f"ffup{l}_fwd", h2, w["ff_up"], F32)
        act = _conv_fwd(f"ffact{l}_fwd", up, 0, f, p["ff_w8"], p["ff_b"], FF_CONV, True, BF16)
        w["ff_down"], _ = landed(l, "ff_down", act)
        down = _mm(f"ffdown{l}_fwd", act, w["ff_down"], F32)
        saved.append(dict(p=p, w=w, x_in=xcur, pending=pending, h=h, proj=proj, xbc=xbc, hprev=hprev, ycat=ycat, x1=x1, mix=mix, h2=h2, up=up, act=act))
        xcur, pending = x2, (down, g2)
    rows = [(xcur, d, 0, F32), (pending[0], d, 0, BF16), (loss_target, d, 0, None)]
    (loss_local,) = _row_fwd("final_fwd", _final_tile, rows, [pending[1]], [fg], [], 1, tm)
    loss = lax.psum(loss_local[0, 0], AXES)

    (dx2, ddown), (dg2,), (dfg,) = _row_bwd("final_bwd", _final_tile, rows, [pending[1]], [fg], [], 1, tm, [jnp.ones((1, 1), F32)])
    dmods, small_grads = [None] * nl, [None] * nl
    reducing = {}

    def start_reduce(l, n, g):
        width = g.shape[1] // 4
        by_chip = g.reshape(4, g.shape[0] // 4, g.shape[1]) if BIG_AXIS[n] == 0 else jnp.stack([g[:, k * width:(k + 1) * width] for k in range(4)])
        reducing[l, n] = _reduce_weights_start(f"wg{l}_{n}", [by_chip], None)
        return reducing[l, n]["token"]

    for l in reversed(range(nl)):
        sv = saved[l]
        p, w = sv["p"], sv["w"]
        sh1, sc1, g1, sh2, sc2, g2 = (mods[l, :, k] for k in range(N_MOD))
        dact, dw_down = _mm_bwd(f"ffdown{l}", sv["act"], w["ff_down"], ddown)
        ff_b = _after(p["ff_b"], start_reduce(l, "ff_down", dw_down))
        dup, dff_w8, dff_b = _conv_bwd(f"ffact{l}_bwd", sv["up"], 0, f, p["ff_w8"], ff_b, FF_CONV, True, dact, BF16)
        dh2, dw_up = _mm_bwd(f"ffup{l}", sv["h2"], w["ff_up"], dup)
        sc2 = _after(sc2, start_reduce(l, "ff_up", _gate_value_blocks(f"ffup{l}_columns", dw_up, f, inverse=True)))
        rows = [(sv["x1"], d, 0, F32), (sv["mix"], d, 0, BF16)]
        (dx1, dmix), (dg1, dsc2, dsh2), (dn2,) = _row_bwd(f"rnm{l}b_bwd", _rnm_tile, rows, [g1, sc2, sh2], [p["norm2"]], [(d, F32), (d, BF16)], 0, tm, [dx2, dh2])
        dycat, dw_out = _mm_bwd(f"wout{l}", sv["ycat"], w["w_out"], dmix)
        p_tied = dict(p, dtb=_after(p["dtb"], start_reduce(l, "w_out", dw_out)))
        dproj, (dw8, dcb, ddtb, dalog, ddsk, dng, dvg, dws, dbs_t, dog) = _mixer_bwd(f"mix{l}", sv["proj"], d, cd, *mixer_args(p_tied), sv["xbc"], sv["hprev"], dycat)
        dh, dw_in_p = _mm_bwd(f"win{l}", sv["h"], w["w_in"], dproj)
        dw_in = jnp.concatenate([dw_in_p[:, :d], dw_in_p[:, 3 * d:3 * d + cd], dw_in_p[:, 3 * d + cd:3 * d + cd + heads], dw_in_p[:, d:3 * d]], axis=1)
        sc1 = _after(sc1, start_reduce(l, "w_in", dw_in))
        if sv["pending"] is None:
            (dx2,), (dsc1, dsh1), (dn1,) = _row_bwd(f"nm{l}_bwd", _nm_res_tile, [(sv["x_in"], d, 0, F32)], [sc1, sh1], [p["norm1"]], [(d, F32), (d, BF16)], 0, tm, [dx1, dh])
        else:
            rows = [(sv["x_in"], d, 0, F32), (sv["pending"][0], d, 0, BF16)]
            (dx2, ddown), (dg2_prev, dsc1, dsh1), (dn1,) = _row_bwd(
                f"rnm{l}a_bwd", _rnm_tile, rows, [sv["pending"][1], sc1, sh1], [p["norm1"]], [(d, F32), (d, BF16)], 0, tm, [dx1, dh])
        dmods[l] = jnp.concatenate([dsh1, dsc1, dg1, dsh2, dsc2, dg2], axis=1).reshape(bl, N_MOD * d)
        if sv["pending"] is not None:
            dg2 = dg2_prev
        small_grads[l] = dict(
            norm1_g=dn1.reshape(d), norm2_g=dn2.reshape(d), ssd_conv_b=dcb.reshape(cd), ssd_dt_bias=ddtb[0, :heads], ssd_a_log=dalog[0, :heads],
            ssd_d=ddsk.reshape(heads, HEAD_DIM).sum(-1), ssd_norm_g=dng.reshape(d), gm_vnorm_g=dvg.reshape(d), gm_ws=dws, gm_bs=dbs_t[:, :n_gm].T,
            gm_out_g=dog.reshape(d), ff_conv_b=dff_b.reshape(f), ssd_conv_w=dw8[:SSD_CONV], ff_conv_w=dff_w8[:FF_CONV])
    grad_x = dx2

    g_ada_w, g_ada_b = _ada_bwd("ada", bl, c_all, ada_w.shape[2], jnp.stack(dmods))
    small_names_r = SMALL + CONVW + ("final_g",)
    summed = _allreduce_small("small", [jnp.stack([small_grads[l][n] for l in range(nl)]) for n in SMALL + CONVW] + [dfg])
    grad = {"ada_w": g_ada_w, "ada_b": g_ada_b}
    big_grads = {}
    for l in reversed(range(nl)):
        for n in reversed(BIG):
            (big_grads[l, n],), _ = _reduce_weights_finish(f"wg{l}_{n}", reducing.pop((l, n)), summed[0])
    for n in BIG:
        grad[n] = jnp.stack([big_grads[l, n] for l in range(nl)])
    grad.update(zip(small_names_r, summed))
    chip = 2 * lax.axis_index("x") + lax.axis_index("y")
    for n in CONVW:
        width = weights[n].shape[-1]
        grad[n] = lax.dynamic_slice_in_dim(grad[n], chip * width, width, axis=2)

    delta, new_m, new_v = {}, {}, {}
    for n in ("ada_w",) + BIG:
        delta[n], new_m[n], new_v[n] = _adamw("adam_" + n, weights[n], grad[n], given["m_" + n], given["v_" + n])
    small_names = ("ada_b",) + SMALL + CONVW + ("final_g",)
    as2d = lambda t: t.reshape(1, -1) if t.ndim == 1 else t
    res = _adamw_small(
        "adam_small", [as2d(weights[n]) for n in small_names], [as2d(grad[n]) for n in small_names],
        [as2d(given["m_" + n]) for n in small_names], [as2d(given["v_" + n]) for n in small_names])
    for out, vals in zip((delta, new_m, new_v), res):
        for n, val in zip(small_names, vals):
            out[n] = val.reshape(weights[n].shape)
    grad["final_g"] = grad["final_g"].reshape(final_g.shape)

    return (loss, grad_x, *[grad[n] for n in WEIGHTS], *[delta[n] for n in WEIGHTS], *[new_m[n] for n in WEIGHTS], *[new_v[n] for n in WEIGHTS])
```

```python
import functools

import jax
import jax.numpy as jnp
from jax import lax
from jax.experimental import pallas as pl
from jax.experimental.pallas import tpu as pltpu

F32 = jnp.float32
BF16 = jnp.bfloat16
EPS = 1e-6
CHUNK = 128
HEAD_DIM = 64
STATE = 128
GM_HEAD = 128
SSD_GROUPS = 2
SSD_CONV = 4
FF_CONV = 3
N_MOD = 6
LANES = 128
SUBLANES = 8
CONV_LANES = 256
CONV_ROWS = 32
V7X_VMEM_LIMIT = 48 * 1024 * 1024
MM_VMEM_BUDGET = 30 * 1024 * 1024
MM_STEP_MACS = 2048 * 1024 * 1024
MM_ACC_SECONDS_PER_VREG = 1.4e-9
V7X_MXU_FLOPS = 996e12
V7X_HBM_BYTES_PER_S = 3.3e12
V7X_STEP_SECONDS = 0.35e-6
ADAM_LR, ADAM_B1, ADAM_B2, ADAM_EPS, ADAM_WD, ADAM_STEP = 0.001, 0.9, 0.999, 1e-08, 0.01, 10
MESH = pl.DeviceIdType.MESH
AXES = ("x", "y", "c")


def _round_up(n, m):
    return (n + m - 1) // m * m


def _divisors(n, cap, mult=LANES):
    out = [t for t in range(mult, min(n, cap) + 1, mult) if n % t == 0]
    return out or [n]


def _pcall(body, name, out_shape, grid=(), in_specs=None, out_specs=None, scratch=(), sem=None, prefetch=0):
    params = pltpu.CompilerParams(dimension_semantics=sem, vmem_limit_bytes=V7X_VMEM_LIMIT)
    if prefetch:
        spec = pltpu.PrefetchScalarGridSpec(num_scalar_prefetch=prefetch, grid=grid, in_specs=in_specs, out_specs=out_specs, scratch_shapes=list(scratch))
        return pl.pallas_call(body, name=name, out_shape=out_shape, grid_spec=spec, compiler_params=params)
    if in_specs is None:
        return pl.pallas_call(body, name=name, out_shape=out_shape, compiler_params=params)
    return pl.pallas_call(body, name=name, out_shape=out_shape, grid=grid, in_specs=in_specs, out_specs=out_specs, scratch_shapes=list(scratch), compiler_params=params)


def _sds(shape, dtype):
    return jax.ShapeDtypeStruct(tuple(shape), dtype)


def _mm_tiles(m, n, k, a_bytes, b_bytes, o_bytes):
    best, best_key = None, None
    for tm in _divisors(m, 2048):
        for tn in _divisors(n, 2560):
            for tk in _divisors(k, 2560):
                vmem = 2 * (tm * tk * a_bytes + tk * tn * b_bytes + tm * tn * o_bytes) + tm * tn * 4
                if vmem > MM_VMEM_BUDGET or tm * tn * tk > MM_STEP_MACS:
                    continue
                ni, nj, nk = m // tm, n // tn, k // tk
                a_reads = 1 if nk == 1 else nj
                b_reads = 1 if (nk == 1 and nj == 1) else ni
                hbm = a_reads * m * k * a_bytes + b_reads * k * n * b_bytes + m * n * o_bytes
                t = max(2.0 * m * n * k / V7X_MXU_FLOPS, hbm / V7X_HBM_BYTES_PER_S) + ni * nj * nk * V7X_STEP_SECONDS
                if nk > 1:
                    t += ni * nj * nk * (tm * tn // (SUBLANES * LANES)) * MM_ACC_SECONDS_PER_VREG
                key = (-t, tm * tn * tk)
                if best_key is None or key > best_key:
                    best, best_key = (tm, tn, tk), key
    return best


def _matmul(name, a, b, mode, out_dtype):
    if mode == "nn":
        (m, k), n = a.shape, b.shape[1]
    elif mode == "nt":
        (m, k), n = a.shape, b.shape[0]
    else:
        (k, m), n = a.shape, b.shape[1]
    tm, tn, tk = _mm_tiles(m, n, k, a.dtype.itemsize, b.dtype.itemsize, jnp.dtype(out_dtype).itemsize)
    nk = k // tk
    if mode == "nn":
        a_spec = pl.BlockSpec((tm, tk), lambda i, j, kk: (i, kk))
        b_spec = pl.BlockSpec((tk, tn), lambda i, j, kk: (kk, j))
        dims = ((1,), (0,))
    elif mode == "nt":
        a_spec = pl.BlockSpec((tm, tk), lambda i, j, kk: (i, kk))
        b_spec = pl.BlockSpec((tn, tk), lambda i, j, kk: (j, kk))
        dims = ((1,), (1,))
    else:
        a_spec = pl.BlockSpec((tk, tm), lambda i, j, kk: (kk, i))
        b_spec = pl.BlockSpec((tk, tn), lambda i, j, kk: (kk, j))
        dims = ((0,), (0,))

    def body(a_ref, b_ref, o_ref, acc_ref):
        kk = pl.program_id(2)
        p = lax.dot_general(a_ref[...].astype(BF16), b_ref[...].astype(BF16), (dims, ((), ())), preferred_element_type=F32)
        if nk == 1:
            o_ref[...] = p.astype(o_ref.dtype)
        else:
            @pl.when(kk == 0)
            def _():
                acc_ref[...] = p

            @pl.when(kk > 0)
            def _():
                acc_ref[...] += p

            @pl.when(kk == nk - 1)
            def _():
                o_ref[...] = acc_ref[...].astype(o_ref.dtype)

    return _pcall(
        body, name, _sds((m, n), out_dtype), grid=(m // tm, n // tn, nk), in_specs=[a_spec, b_spec],
        out_specs=pl.BlockSpec((tm, tn), lambda i, j, kk: (i, j)), scratch=[pltpu.VMEM((tm, tn), F32)],
        sem=("parallel", "parallel", "arbitrary"),
    )(a, b)


def _mm(name, a, w, out_dtype):
    return _matmul(name, a.reshape(-1, a.shape[-1]), w, "nn", out_dtype).reshape(a.shape[:-1] + (w.shape[1],))


def _mm_bwd(name, a, w, dy):
    a2, dy2 = a.reshape(-1, a.shape[-1]), dy.reshape(-1, dy.shape[-1])
    return _matmul(name + "_dx", dy2, w, "nt", BF16).reshape(a.shape), _matmul(name + "_dw", a2, dy2, "tn", BF16)


def _dg(a, b, dims):
    return lax.dot_general(a.astype(BF16), b.astype(BF16), (dims, ((), ())), preferred_element_type=F32)


@jax.custom_vjp
def _dot_nn(a, b):
    return _dg(a, b, ((1,), (0,)))


_dot_nn.defvjp(lambda a, b: (_dot_nn(a, b), (a, b)), lambda r, d: (_dg(d, r[1], ((1,), (1,))), _dg(r[0], d, ((0,), (0,)))))


@jax.custom_vjp
def _dot_nt(a, b):
    return _dg(a, b, ((1,), (1,)))


_dot_nt.defvjp(lambda a, b: (_dot_nt(a, b), (a, b)), lambda r, d: (_dg(d, r[1], ((1,), (0,))), _dg(d, r[0], ((0,), (0,)))))


@jax.custom_vjp
def _dot_tn(a, b):
    return _dg(a, b, ((0,), (0,)))


_dot_tn.defvjp(lambda a, b: (_dot_tn(a, b), (a, b)), lambda r, d: (_dg(r[1], d, ((1,), (1,))), _dg(r[0], d, ((1,), (0,)))))


def _exact_dot(a, c, dims):
    hi = a.astype(BF16)
    r1 = a - hi.astype(F32)
    mid = r1.astype(BF16)
    lo = (r1 - mid.astype(F32)).astype(BF16)
    cb = c.astype(BF16)
    f = lambda t: lax.dot_general(t, cb, (dims, ((), ())), preferred_element_type=F32)
    return f(hi) + f(mid) + f(lo)


@jax.custom_vjp
def _sel_right(a, c):
    return _exact_dot(a, c, ((1,), (0,)))


_sel_right.defvjp(lambda a, c: (_sel_right(a, c), c), lambda c, d: (_exact_dot(d, c, ((1,), (1,))), jnp.zeros_like(c)))


def _exact_dot_left(c, a, dims):
    hi = a.astype(BF16)
    r1 = a - hi.astype(F32)
    mid = r1.astype(BF16)
    lo = (r1 - mid.astype(F32)).astype(BF16)
    cb = c.astype(BF16)
    f = lambda t: lax.dot_general(cb, t, (dims, ((), ())), preferred_element_type=F32)
    return f(hi) + f(mid) + f(lo)


@jax.custom_vjp
def _sel_left(c, a):
    return _exact_dot_left(c, a, ((1,), (0,)))


_sel_left.defvjp(lambda c, a: (_sel_left(c, a), c), lambda c, d: (jnp.zeros_like(c), _exact_dot_left(c, d, ((0,), (0,)))))


def _sigmoid(x):
    return 1.0 / (1.0 + jnp.exp(-x))


def _rms(x, g):
    return x * lax.rsqrt(jnp.mean(x * x, axis=-1, keepdims=True) + EPS) * g


def _gelu(x):
    return 0.5 * x * (1.0 + lax.erf(x * (2.0 ** -0.5)))


def _causal(n):
    return lax.broadcasted_iota(jnp.int32, (n, n), 0) >= lax.broadcasted_iota(jnp.int32, (n, n), 1)


def _row_in_specs(rows, bparams, gparams, tm):
    specs = [pl.BlockSpec((1, tm, w), lambda b, i, cb=cb: (b, i, cb)) for (_, w, cb, _) in rows]
    specs += [pl.BlockSpec((1, 1, p.shape[-1]), lambda b, i: (b, 0, 0)) for p in bparams]
    specs += [pl.BlockSpec(p.shape, lambda b, i, n=p.ndim: (0,) * n) for p in gparams]
    return specs


def _row_vals(refs, n_rows, n_b, n_g):
    vals = [r[0].astype(F32) for r in refs[:n_rows]]
    vals += [r[0].astype(F32) for r in refs[n_rows:n_rows + n_b]]
    vals += [r[...].astype(F32) for r in refs[n_rows + n_b:n_rows + n_b + n_g]]
    return vals


def _row_fwd(name, tile, rows, bparams, gparams, outs, n_sum, tm):
    bl, s = rows[0][0].shape[:2]
    n_in = len(rows) + len(bparams) + len(gparams)

    def body(*refs):
        first = (pl.program_id(0) == 0) & (pl.program_id(1) == 0)
        res = tile(*_row_vals(refs, len(rows), len(bparams), len(gparams)))
        o_refs = refs[n_in:]
        for k in range(len(outs)):
            o_refs[k][0] = res[k].astype(o_refs[k].dtype)
        for k in range(n_sum):
            o_ref, val = o_refs[len(outs) + k], res[len(outs) + k]

            @pl.when(first)
            def _(o_ref=o_ref, val=val):
                o_ref[...] = val

            @pl.when(jnp.logical_not(first))
            def _(o_ref=o_ref, val=val):
                o_ref[...] += val

    out_shape = [_sds((bl, s, w), dt) for (w, dt) in outs] + [_sds((1, 1), F32)] * n_sum
    out_specs = [pl.BlockSpec((1, tm, w), lambda b, i: (b, i, 0)) for (w, _) in outs] + [pl.BlockSpec((1, 1), lambda b, i: (0, 0))] * n_sum
    return _pcall(
        body, name, out_shape, grid=(bl, s // tm), in_specs=_row_in_specs(rows, bparams, gparams, tm), out_specs=out_specs,
        sem=("arbitrary", "arbitrary"),
    )(*[r[0] for r in rows], *bparams, *gparams)


def _row_bwd(name, tile, rows, bparams, gparams, outs, n_sum, tm, cts):
    bl, s = rows[0][0].shape[:2]
    n_r, n_b, n_g = len(rows), len(bparams), len(gparams)
    n_in = n_r + n_b + n_g
    n_ct = len(outs) + n_sum
    grad_rows = [k for k in range(n_r) if rows[k][3]]

    def body(*refs):
        b, i = pl.program_id(0), pl.program_id(1)
        vals = _row_vals(refs, n_r, n_b, n_g)
        ct_refs = refs[n_in:n_in + n_ct]
        ct = [r[0].astype(F32) for r in ct_refs[:len(outs)]] + [r[...] for r in ct_refs[len(outs):]]
        _, vjp = jax.vjp(tile, *vals)
        grads = vjp(tuple(ct))
        o_refs = refs[n_in + n_ct:]
        for j, k in enumerate(grad_rows):
            o_refs[j][0] = grads[k].astype(o_refs[j].dtype)
        for k in range(n_b):
            o_ref, val = o_refs[len(grad_rows) + k], grads[n_r + k]

            @pl.when(i == 0)
            def _(o_ref=o_ref, val=val):
                o_ref[0] = val

            @pl.when(i > 0)
            def _(o_ref=o_ref, val=val):
                o_ref[0] += val

        first = (b == 0) & (i == 0)
        for k in range(n_g):
            o_ref, val = o_refs[len(grad_rows) + n_b + k], grads[n_r + n_b + k]

            @pl.when(first)
            def _(o_ref=o_ref, val=val):
                o_ref[...] = val

            @pl.when(jnp.logical_not(first))
            def _(o_ref=o_ref, val=val):
                o_ref[...] += val

    in_specs = _row_in_specs(rows, bparams, gparams, tm)
    in_specs += [pl.BlockSpec((1, tm, w), lambda b, i: (b, i, 0)) for (w, _) in outs] + [pl.BlockSpec((1, 1), lambda b, i: (0, 0))] * n_sum
    out_shape = [_sds((bl, s, rows[k][1]), rows[k][3]) for k in grad_rows]
    out_shape += [_sds(p.shape, F32) for p in bparams] + [_sds(p.shape, F32) for p in gparams]
    out_specs = [pl.BlockSpec((1, tm, rows[k][1]), lambda b, i: (b, i, 0)) for k in grad_rows]
    out_specs += [pl.BlockSpec((1, 1, p.shape[-1]), lambda b, i: (b, 0, 0)) for p in bparams]
    out_specs += [pl.BlockSpec(p.shape, lambda b, i, n=p.ndim: (0,) * n) for p in gparams]
    res = _pcall(
        body, name, out_shape, grid=(bl, s // tm), in_specs=in_specs, out_specs=out_specs, sem=("arbitrary", "arbitrary"),
    )(*[r[0] for r in rows], *bparams, *gparams, *cts)
    return res[:len(grad_rows)], res[len(grad_rows):len(grad_rows) + n_b], res[len(grad_rows) + n_b:]


def _nm_tile(x, sc, sh, g):
    return (_rms(x, g) * (1.0 + sc) + sh,)


def _nm_res_tile(x, sc, sh, g):
    return x, _rms(x, g) * (1.0 + sc) + sh


def _rnm_tile(x, o, gate, sc, sh, g):
    xn = x + gate * o
    return xn, _rms(xn, g) * (1.0 + sc) + sh


def _final_tile(x, o, tgt, gate, g):
    e = _rms(x + gate * o, g) - tgt
    return (0.5 * jnp.sum(jnp.mean(e * e, axis=-1, keepdims=True), axis=0, keepdims=True),)


def _gm_tile(u_in, v_in, vg, ws, bs_t, og):
    d = u_in.shape[1]
    u, vn = _gelu(u_in), _rms(_gelu(v_in), vg)
    causal = _causal(CHUNK)
    lane = lax.broadcasted_iota(jnp.int32, (1, LANES), 1)
    parts = []
    for h in range(d // GM_HEAD):
        bias = jnp.sum(bs_t * (lane == h).astype(F32), axis=1, keepdims=True)
        parts.append(_dot_nn(jnp.where(causal, ws[h], 0.0), vn[:, h * GM_HEAD:(h + 1) * GM_HEAD]) + bias)
    return (_rms(u * jnp.concatenate(parts, axis=1), og),)


def _conv_window(ref, r0, rows, c0, tc, seq, before, after):
    parts = []
    if before:
        p0 = pl.multiple_of(jnp.maximum(r0 - SUBLANES, 0), SUBLANES)
        parts.append(jnp.where(r0 > 0, ref[0, pl.ds(p0, SUBLANES), pl.ds(c0, tc)].astype(F32), 0.0))
    parts.append(ref[0, pl.ds(r0, rows), pl.ds(c0, tc)].astype(F32))
    if after:
        n0 = pl.multiple_of(jnp.minimum(r0 + rows, seq - SUBLANES), SUBLANES)
        parts.append(jnp.where(r0 + rows < seq, ref[0, pl.ds(n0, SUBLANES), pl.ds(c0, tc)].astype(F32), 0.0))
    return jnp.concatenate(parts, axis=0) if len(parts) > 1 else parts[0]


def _conv_taps_pre(xe, w, b, taps, rows):
    pre = xe[SUBLANES:SUBLANES + rows] * w[taps - 1] + b
    for j in range(1, taps):
        pre = pre + pltpu.roll(xe, j, 0)[SUBLANES:SUBLANES + rows] * w[taps - 1 - j]
    return pre


def _conv_fwd(name, src, col0, chans, w8, b, taps, gated, out_dtype):
    bl, s, _ = src.shape
    tc = CONV_LANES
    xw = 2 * tc if gated else tc

    def body(x_ref, w_ref, b_ref, o_ref):
        w = [w_ref[k:k + 1, :] for k in range(taps)]
        bias = b_ref[...]

        def step(c, carry):
            r0 = pl.multiple_of(c * CONV_ROWS, CONV_ROWS)
            xe = _conv_window(x_ref, r0, CONV_ROWS, 0, tc, s, True, False)
            pre = _conv_taps_pre(xe, w, bias, taps, CONV_ROWS)
            y = pre * _sigmoid(pre)
            if gated:
                y = y * x_ref[0, pl.ds(r0, CONV_ROWS), pl.ds(tc, tc)]
            o_ref[0, pl.ds(r0, CONV_ROWS), :] = y.astype(o_ref.dtype)
            return carry

        lax.fori_loop(0, s // CONV_ROWS, step, 0, unroll=4)

    first = 0 if gated else col0 // tc
    return _pcall(
        body, name, _sds((bl, s, chans), out_dtype), grid=(chans // tc, bl),
        in_specs=[pl.BlockSpec((1, s, xw), lambda j, bb: (bb, 0, first + j)), pl.BlockSpec((SUBLANES, tc), lambda j, bb: (0, j)), pl.BlockSpec((1, tc), lambda j, bb: (0, j))],
        out_specs=pl.BlockSpec((1, s, tc), lambda j, bb: (bb, 0, j)), sem=("parallel", "arbitrary"),
    )(src, w8, b)


def _conv_bwd(name, src, col0, chans, w8, b, taps, gated, dy, dx_dtype):
    bl, s, _ = src.shape
    tc = CONV_LANES
    xw = 2 * tc if gated else tc
    ext = CONV_ROWS + SUBLANES

    def body(x_ref, w_ref, b_ref, dy_ref, dx_ref, dw_ref, db_ref):
        bb = pl.program_id(1)
        w = [w_ref[k:k + 1, :] for k in range(taps)]
        bias = b_ref[...]

        def fold(v):
            acc = v[0:SUBLANES]
            for i in range(1, CONV_ROWS // SUBLANES):
                acc = acc + v[i * SUBLANES:(i + 1) * SUBLANES]
            return acc

        def step(c, carry):
            r0 = pl.multiple_of(c * CONV_ROWS, CONV_ROWS)
            xe = _conv_window(x_ref, r0, CONV_ROWS, 0, tc, s, True, True)
            pre = _conv_taps_pre(xe, w, bias, taps, ext)
            sig = _sigmoid(pre)
            d = _conv_window(dy_ref, r0, CONV_ROWS, 0, tc, s, False, True)
            dsil = d * _conv_window(x_ref, r0, CONV_ROWS, tc, tc, s, False, True) if gated else d
            dpre = dsil * sig * (1.0 + pre * (1.0 - sig))
            dx = dpre[:CONV_ROWS] * w[taps - 1]
            for j in range(1, taps):
                dx = dx + pltpu.roll(dpre, ext - j, 0)[:CONV_ROWS] * w[taps - 1 - j]
            dx_ref[0, pl.ds(r0, CONV_ROWS), pl.ds(0, tc)] = dx.astype(dx_ref.dtype)
            if gated:
                dx_ref[0, pl.ds(r0, CONV_ROWS), pl.ds(tc, tc)] = (d[:CONV_ROWS] * (pre * sig)[:CONV_ROWS]).astype(dx_ref.dtype)
            here = dpre[:CONV_ROWS]
            sums = [fold(here * (pltpu.roll(xe, taps - 1 - k, 0) if k < taps - 1 else xe)[SUBLANES:ext]) + carry[k] for k in range(taps)]
            return tuple(sums) + (fold(here) + carry[taps],)

        zero = jnp.zeros((SUBLANES, tc), F32)
        sums = lax.fori_loop(0, s // CONV_ROWS, step, (zero,) * (taps + 1), unroll=2)
        rows = [jnp.sum(t, axis=0, keepdims=True) for t in sums]
        dw = jnp.concatenate(rows[:taps] + [jnp.zeros_like(rows[0])] * (SUBLANES - taps), axis=0)

        @pl.when(bb == 0)
        def _():
            dw_ref[...] = dw
            db_ref[...] = rows[taps]

        @pl.when(bb > 0)
        def _():
            dw_ref[...] += dw
            db_ref[...] += rows[taps]

    first = 0 if gated else col0 // tc
    out_shape = [_sds((bl, s, chans * (2 if gated else 1)), dx_dtype), _sds((SUBLANES, chans), F32), _sds((1, chans), F32)]
    return _pcall(
        body, name, out_shape, grid=(chans // tc, bl),
        in_specs=[
            pl.BlockSpec((1, s, xw), lambda j, bb: (bb, 0, first + j)), pl.BlockSpec((SUBLANES, tc), lambda j, bb: (0, j)),
            pl.BlockSpec((1, tc), lambda j, bb: (0, j)), pl.BlockSpec((1, s, tc), lambda j, bb: (bb, 0, j)),
        ],
        out_specs=[pl.BlockSpec((1, s, xw), lambda j, bb: (bb, 0, j)), pl.BlockSpec((SUBLANES, tc), lambda j, bb: (0, j)), pl.BlockSpec((1, tc), lambda j, bb: (0, j))],
        sem=("parallel", "arbitrary"),
    )(src, w8, b, dy)


def _gate_value_blocks(name, w, f, inverse=False):
    nb = f // CONV_LANES
    src = (lambda j: (0, 2 * (j % nb) + j // nb)) if inverse else (lambda j: (0, (j % 2) * nb + j // 2))

    def body(x_ref, o_ref):
        o_ref[...] = x_ref[...]

    return _pcall(
        body, name, _sds(w.shape, w.dtype), grid=(2 * nb,), in_specs=[pl.BlockSpec((w.shape[0], CONV_LANES), src)],
        out_specs=pl.BlockSpec((w.shape[0], CONV_LANES), lambda j: (0, j)), sem=("parallel",),
    )(w)


def _ssd_chunk(g_idx, states, xs, bm, cm, dtr, z, dtb, alog, dsk, ng):
    gw = xs.shape[1]
    hpg = gw // HEAD_DIM
    dt = jax.nn.softplus(dtr + dtb)
    da = dt * (-jnp.exp(alog))
    causal = _causal(CHUNK)
    acs = _sel_left(causal.astype(F32), da)
    head_of_col = g_idx * hpg + lax.broadcasted_iota(jnp.int32, (LANES, gw), 1) // HEAD_DIM
    expand = (lax.broadcasted_iota(jnp.int32, (LANES, gw), 0) == head_of_col).astype(F32)
    dt_e, acs_e = _sel_right(dt, expand), _sel_right(acs, expand)
    last = lax.broadcasted_iota(jnp.int32, (CHUNK, gw), 0) == CHUNK - 1
    alast_e = jnp.sum(jnp.where(last, acs_e, 0.0), axis=0, keepdims=True)
    xc = xs * dt_e
    xc_st = xc * jnp.exp(alast_e - acs_e)
    decay_out, chunk_decay = jnp.exp(acs_e), jnp.exp(alast_e)
    cb = _dot_nt(cm, bm)
    acs_t = acs.T
    lane = lax.broadcasted_iota(jnp.int32, (1, LANES), 1)
    sub = lax.broadcasted_iota(jnp.int32, (LANES, 1), 0)
    ys, new_states = [], []
    for p in range(gw // LANES):
        sl = slice(p * LANES, (p + 1) * LANES)
        xcp = xc[:, sl]
        y = _dot_nn(cm, states[p]) * decay_out[:, sl]
        for q in range(2):
            head = g_idx * hpg + 2 * p + q
            col = jnp.sum(acs * (lane == head).astype(F32), axis=1, keepdims=True)
            row = jnp.sum(acs_t * (sub == head).astype(F32), axis=0, keepdims=True)
            decay = jnp.where(causal, jnp.exp(jnp.where(causal, col - row, 0.0)), 0.0)
            half = ((lane // HEAD_DIM) == q).astype(F32)
            y = y + _dot_nn(cb * decay, xcp * half)
        ys.append(y)
        new_states.append(states[p] * chunk_decay[:, sl] + _dot_tn(bm, xc_st[:, sl]))
    y = jnp.concatenate(ys, axis=1) + dsk * xs
    gated = y * (z * _sigmoid(z))
    return tuple(new_states), _rms(gated, ng)


def _ssd_specs(d, gw, dt_col, rev, nc):
    ci = (lambda i: nc - 1 - i) if rev else (lambda i: i)
    return [
        pl.BlockSpec((1, CHUNK, gw), lambda g, b, i: (b, ci(i), g)),
        pl.BlockSpec((1, CHUNK, STATE), lambda g, b, i: (b, ci(i), d // STATE + g)),
        pl.BlockSpec((1, CHUNK, STATE), lambda g, b, i: (b, ci(i), d // STATE + SSD_GROUPS + g)),
        pl.BlockSpec((1, CHUNK, LANES), lambda g, b, i: (b, ci(i), dt_col // LANES)),
        pl.BlockSpec((1, CHUNK, gw), lambda g, b, i: (b, ci(i), g)),
        pl.BlockSpec((1, LANES), lambda g, b, i: (0, 0)), pl.BlockSpec((1, LANES), lambda g, b, i: (0, 0)),
        pl.BlockSpec((1, gw), lambda g, b, i: (0, g)), pl.BlockSpec((1, gw), lambda g, b, i: (0, g)),
    ]


def _ssd_fwd(name, xbc, proj, dt_col, dtb, alog, dsk, ng):
    bl, s, cd = xbc.shape
    d = dsk.shape[1]
    gw, nc = d // SSD_GROUPS, s // CHUNK
    npair = gw // LANES

    def body(xs_ref, bm_ref, cm_ref, dt_ref, z_ref, dtb_ref, alog_ref, dsk_ref, ng_ref, y_ref, hp_ref, st_ref):
        g, i = pl.program_id(0), pl.program_id(2)

        @pl.when(i == 0)
        def _():
            st_ref[...] = jnp.zeros_like(st_ref)

        states = tuple(st_ref[p] for p in range(npair))
        hp_ref[0, 0, 0] = st_ref[...]
        new_states, yn = _ssd_chunk(g, states, xs_ref[0], bm_ref[0], cm_ref[0], dt_ref[0], z_ref[0], dtb_ref[...], alog_ref[...], dsk_ref[...], ng_ref[...])
        for p in range(npair):
            st_ref[p] = new_states[p]
        y_ref[0] = yn.astype(y_ref.dtype)

    out_shape = [_sds((bl, s, d), BF16), _sds((SSD_GROUPS, bl, nc, npair, STATE, LANES), F32)]
    out_specs = [
        pl.BlockSpec((1, CHUNK, gw), lambda g, b, i: (b, i, g)),
        pl.BlockSpec((1, 1, 1, npair, STATE, LANES), lambda g, b, i: (g, b, i, 0, 0, 0)),
    ]
    return _pcall(
        body, name, out_shape, grid=(SSD_GROUPS, bl, nc), in_specs=_ssd_specs(d, gw, dt_col, False, nc), out_specs=out_specs,
        scratch=[pltpu.VMEM((npair, STATE, LANES), F32)], sem=("arbitrary", "arbitrary", "arbitrary"),
    )(xbc, xbc, xbc, proj, proj, dtb, alog, dsk, ng)


def _ssd_bwd(name, xbc, proj, dt_col, dtb, alog, dsk, ng, hprev, dy):
    bl, s, cd = xbc.shape
    d = dsk.shape[1]
    gw, nc = d // SSD_GROUPS, s // CHUNK
    npair = gw // LANES

    def body(xs_ref, bm_ref, cm_ref, dt_ref, z_ref, dtb_ref, alog_ref, dsk_ref, ng_ref, hp_ref, dy_ref,
             dxs_ref, dbm_ref, dcm_ref, ddt_ref, dz_ref, ddtb_ref, dalog_ref, ddsk_ref, dng_ref, dst_ref):
        g, b, i = pl.program_id(0), pl.program_id(1), pl.program_id(2)

        @pl.when(i == 0)
        def _():
            dst_ref[...] = jnp.zeros_like(dst_ref)

        states = tuple(hp_ref[0, 0, 0, p] for p in range(npair))
        step = functools.partial(_ssd_chunk, g)
        _, vjp = jax.vjp(step, states, xs_ref[0], bm_ref[0], cm_ref[0], dt_ref[0], z_ref[0], dtb_ref[...], alog_ref[...], dsk_ref[...], ng_ref[...])
        d_states, dxs, dbm, dcm, ddt, dz, ddtb, dalog, ddsk, dng = vjp((tuple(dst_ref[p] for p in range(npair)), dy_ref[0].astype(F32)))
        for p in range(npair):
            dst_ref[p] = d_states[p]
        dxs_ref[0], dbm_ref[0], dcm_ref[0] = dxs, dbm, dcm
        dz_ref[0] = dz.astype(dz_ref.dtype)
        ddt_ref[0, 0] = ddt
        first_g = (b == 0) & (i == 0)
        first = first_g & (g == 0)
        for o_ref, val, init in ((ddtb_ref, ddtb, first), (dalog_ref, dalog, first), (ddsk_ref, ddsk, first_g), (dng_ref, dng, first_g)):
            @pl.when(init)
            def _(o_ref=o_ref, val=val):
                o_ref[...] = val

            @pl.when(jnp.logical_not(init))
            def _(o_ref=o_ref, val=val):
                o_ref[...] += val

    rc = lambda i: nc - 1 - i
    in_specs = _ssd_specs(d, gw, dt_col, True, nc) + [
        pl.BlockSpec((1, 1, 1, npair, STATE, LANES), lambda g, b, i: (g, b, rc(i), 0, 0, 0)),
        pl.BlockSpec((1, CHUNK, gw), lambda g, b, i: (b, rc(i), g)),
    ]
    out_shape = [
        _sds((bl, s, d), F32), _sds((bl, s, SSD_GROUPS * STATE), F32), _sds((bl, s, SSD_GROUPS * STATE), F32),
        _sds((SSD_GROUPS, bl, s, LANES), F32), _sds((bl, s, d), BF16),
        _sds((1, LANES), F32), _sds((1, LANES), F32), _sds((1, d), F32), _sds((1, d), F32),
    ]
    out_specs = [
        pl.BlockSpec((1, CHUNK, gw), lambda g, b, i: (b, rc(i), g)),
        pl.BlockSpec((1, CHUNK, STATE), lambda g, b, i: (b, rc(i), g)), pl.BlockSpec((1, CHUNK, STATE), lambda g, b, i: (b, rc(i), g)),
        pl.BlockSpec((1, 1, CHUNK, LANES), lambda g, b, i: (g, b, rc(i), 0)),
        pl.BlockSpec((1, CHUNK, gw), lambda g, b, i: (b, rc(i), g)),
        pl.BlockSpec((1, LANES), lambda g, b, i: (0, 0)), pl.BlockSpec((1, LANES), lambda g, b, i: (0, 0)),
        pl.BlockSpec((1, gw), lambda g, b, i: (0, g)), pl.BlockSpec((1, gw), lambda g, b, i: (0, g)),
    ]
    return _pcall(
        body, name, out_shape, grid=(SSD_GROUPS, bl, nc), in_specs=in_specs, out_specs=out_specs,
        scratch=[pltpu.VMEM((npair, STATE, LANES), F32)], sem=("arbitrary", "arbitrary", "arbitrary"),
    )(xbc, xbc, xbc, proj, proj, dtb, alog, dsk, ng, hprev, dy)


def _mixer_fwd(name, proj, d, cd, conv_w8, conv_b, dtb, alog, dsk, ng, vg, ws, bs_t, og):
    xbc = _conv_fwd(name + "_conv_fwd", proj, 3 * d, cd, conv_w8, conv_b, SSD_CONV, False, F32)
    y, hprev = _ssd_fwd(name + "_ssd_fwd", xbc, proj, 3 * d + cd, dtb, alog, dsk, ng)
    rows = [(proj, d, 1, BF16), (proj, d, 2, BF16)]
    (g_out,) = _row_fwd(name + "_gm_fwd", _gm_tile, rows, [], [vg, ws, bs_t, og], [(d, BF16)], 0, CHUNK)
    return jnp.concatenate([y, g_out], axis=-1), (xbc, hprev)


def _mixer_bwd(name, proj, d, cd, conv_w8, conv_b, dtb, alog, dsk, ng, vg, ws, bs_t, og, xbc, hprev, dycat):
    bl, s, n_proj = proj.shape
    dt_col = 3 * d + cd
    dy, dg_out = dycat[..., :d], dycat[..., d:]
    dxs, dbm, dcm, ddt2, dz, ddtb, dalog, ddsk, dng = _ssd_bwd(name + "_ssd_bwd", xbc, proj, dt_col, dtb, alog, dsk, ng, hprev, dy)
    dxbc_act = jnp.concatenate([dxs, dbm, dcm], axis=-1)
    dxbc, dw8, dcb = _conv_bwd(name + "_conv_bwd", proj, 3 * d, cd, conv_w8, conv_b, SSD_CONV, False, dxbc_act, BF16)
    rows = [(proj, d, 1, BF16), (proj, d, 2, BF16)]
    (du, dv), _, (dvg, dws, dbs_t, dog) = _row_bwd(name + "_gm_bwd", _gm_tile, rows, [], [vg, ws, bs_t, og], [(d, BF16)], 0, CHUNK, [dg_out])
    ddt = (ddt2[0] + ddt2[1]).astype(BF16)
    pad = jnp.zeros((bl, s, n_proj - dt_col - LANES), BF16)
    dproj = jnp.concatenate([dz, du, dv, dxbc, ddt, pad], axis=-1)
    return dproj, (dw8, dcb, ddtb, dalog, ddsk, dng, dvg, dws, dbs_t, dog)


def _position():
    return lax.axis_index("x"), lax.axis_index("y"), lax.axis_index("c")


def _at(ref, idx):
    return ref.at[idx] if len(idx) else ref


def _exchange(name, inputs, out_shapes, plan, inplace=False):
    if inplace:
        out_shapes = [(a.shape, a.dtype) for a in inputs]
    n_in, n_out = len(inputs), len(out_shapes)
    n_copy = len(plan(0, 0, 0))

    def body(*refs):
        in_refs, out_refs, token = refs[:n_in], refs[n_in:n_in + n_out], refs[n_in + n_out]
        send_sems, recv_sems = refs[n_in + n_out + 1:]
        token[...] = jnp.zeros_like(token)
        x, y, c = _position()
        copies = plan(x, y, c)

        def copy(k, src, dst, peer):
            return pltpu.make_async_remote_copy(src_ref=src, dst_ref=dst, send_sem=send_sems.at[k], recv_sem=recv_sems.at[k], device_id=peer, device_id_type=MESH)

        src_refs = out_refs if inplace else in_refs
        sends = [copy(k, _at(src_refs[sa], si), _at(out_refs[da], di), peer) for k, (sa, si, da, di, peer, _) in enumerate(copies)]
        for cp in sends:
            cp.start()
        for k, (sa, si, da, _, peer, li) in enumerate(copies):
            copy(k, _at(src_refs[sa], si), _at(out_refs[da], li), peer).wait_recv()
        for cp in sends:
            cp.wait_send()

    any_spec = pl.BlockSpec(memory_space=pl.ANY)
    outs = pl.pallas_call(
        body, name=name, out_shape=[_sds(s, dt) for s, dt in out_shapes] + [_sds((SUBLANES, LANES), F32)], in_specs=[any_spec] * n_in,
        out_specs=[any_spec] * n_out + [pl.BlockSpec(memory_space=pltpu.VMEM)],
        scratch_shapes=[pltpu.SemaphoreType.DMA((n_copy,)), pltpu.SemaphoreType.DMA((n_copy,))],
        input_output_aliases={i: i for i in range(n_in)} if inplace else {},
    )(*inputs)
    return list(outs[:n_out]), outs[n_out]


def _exchange_start(name, inputs, out_shapes, plan):
    n_in, n_out = len(inputs), len(out_shapes)
    n_copy = len(plan(0, 0, 0))

    def body(*refs):
        in_refs, land_refs = refs[:n_in], refs[n_in:n_in + n_out]
        send_sems, recv_sems = refs[n_in + n_out:n_in + n_out + 2]
        token = refs[-1]
        x, y, c = _position()
        for k, (sa, si, da, di, peer, _) in enumerate(plan(x, y, c)):
            pltpu.make_async_remote_copy(
                src_ref=_at(in_refs[sa], si), dst_ref=_at(land_refs[da], di), send_sem=send_sems.at[k], recv_sem=recv_sems.at[k],
                device_id=peer, device_id_type=MESH).start()
        token[...] = jnp.zeros_like(token)

    hbm, sem = pl.BlockSpec(memory_space=pltpu.HBM), pl.BlockSpec(memory_space=pltpu.SEMAPHORE)
    lands = [lax.empty(s, dt) for s, dt in out_shapes]
    args = [pltpu.with_memory_space_constraint(a, pltpu.HBM) for a in list(inputs) + lands]
    outs = pl.pallas_call(
        body, name=name,
        out_shape=(pltpu.SemaphoreType.DMA((n_copy,)), pltpu.SemaphoreType.DMA((n_copy,)), *[pltpu.HBM(a.shape, a.dtype) for a in args], _sds((SUBLANES, LANES), F32)),
        in_specs=[hbm] * (n_in + n_out), out_specs=(sem, sem, *[hbm] * (n_in + n_out), pl.BlockSpec(memory_space=pltpu.VMEM)),
        input_output_aliases={i: 2 + i for i in range(n_in + n_out)},
        compiler_params=pltpu.CompilerParams(has_side_effects=pltpu.SideEffectType.DATAFLOW_SIDE_EFFECTING),
    )(*args)
    return dict(name=name, plan=plan, sems=outs[:2], ins=list(outs[2:2 + n_in]), lands=list(outs[2 + n_in:2 + n_in + n_out]), token=outs[-1])


def _exchange_wait(started, after):
    plan, n_in, n_out = started["plan"], len(started["ins"]), len(started["lands"])

    def body(*refs):
        in_refs, land_refs = refs[:n_in], refs[n_in:n_in + n_out]
        send_sems, recv_sems = refs[n_in + n_out:n_in + n_out + 2]
        token = refs[-1]
        x, y, c = _position()
        for k, (sa, si, da, _, peer, li) in enumerate(plan(x, y, c)):
            cp = pltpu.make_async_remote_copy(
                src_ref=_at(in_refs[sa], si), dst_ref=_at(land_refs[da], li), send_sem=send_sems.at[k], recv_sem=recv_sems.at[k],
                device_id=peer, device_id_type=MESH)
            cp.wait_send()
            cp.wait_recv()
        token[...] = jnp.zeros_like(token)

    hbm, sem = pl.BlockSpec(memory_space=pltpu.HBM), pl.BlockSpec(memory_space=pltpu.SEMAPHORE)
    bufs = started["ins"] + started["lands"]
    outs = pl.pallas_call(
        body, name=started["name"] + "_wait", out_shape=(*[pltpu.HBM(a.shape, a.dtype) for a in bufs], _sds((SUBLANES, LANES), F32)),
        in_specs=[hbm] * len(bufs) + [sem, sem, pl.BlockSpec(memory_space=pl.ANY)], out_specs=(*[hbm] * len(bufs), pl.BlockSpec(memory_space=pltpu.VMEM)),
        input_output_aliases={i: i for i in range(len(bufs))},
        compiler_params=pltpu.CompilerParams(has_side_effects=pltpu.SideEffectType.DATAFLOW_SIDE_EFFECTING),
    )(*bufs, *started["sems"], after)
    return list(outs[:n_in]), list(outs[n_in:n_in + n_out]), outs[-1]


def _after(value, token):
    return value + token[0, 0].astype(value.dtype)


def _chip_peers(x, y, c):
    return [(1 - x, y, c), (x, 1 - y, c), (1 - x, 1 - y, c)]


def _chip_of(p):
    return 2 * p[0] + p[1]


def _set_slot(slots, me, blk):
    return lax.dynamic_update_slice(slots, blk[None], (me,) + (0,) * blk.ndim)


def _by_core(c, mine, other, axis):
    return jnp.where(c == 0, jnp.stack([mine, other], axis), jnp.stack([other, mine], axis))


def _plan_gather_chips(n):
    def plan(x, y, c):
        return [(a, (), a, (2 * x + y,), p, (_chip_of(p),)) for a in range(n) for p in _chip_peers(x, y, c)]

    return plan


def _gather_chips(name, blocks):
    recv, token = _exchange(name, blocks, [((4,) + b.shape, b.dtype) for b in blocks], _plan_gather_chips(len(blocks)))
    x, y, _ = _position()
    return [_set_slot(r, 2 * x + y, b) for r, b in zip(recv, blocks)], token


def _gather_pass_cores(name, blocks, from_chips):
    n = len(blocks)

    def plan_cores(x, y, c):
        me, sib = 2 * x + y, (x, y, 1 - c)
        own = [(a, (), a, (me,), sib, (me,)) for a in range(n)]
        passed = [(n + a, (_chip_of(p),), a, (_chip_of(p),), sib, (_chip_of(p),)) for a in range(n) for p in _chip_peers(x, y, c)]
        return own + passed

    from_core, token = _exchange(name + "_cores", list(blocks) + list(from_chips), [((4,) + b.shape, b.dtype) for b in blocks], plan_cores)
    x, y, c = _position()
    return [_by_core(c, _set_slot(r1, 2 * x + y, b), r2, 1) for b, r1, r2 in zip(blocks, from_chips, from_core)], token


def _gather_two_level(name, blocks):
    from_chips, _ = _exchange(name + "_chips", blocks, [((4,) + b.shape, b.dtype) for b in blocks], _plan_gather_chips(len(blocks)))
    return _gather_pass_cores(name, blocks, from_chips)


def _pair_add(name, g42, r4):
    _, _, rh, cols = g42.shape
    tr = _divisors(rh, 512, SUBLANES * 2)[-1]

    def body(c_ref, a_ref, b_ref, o_ref):
        o_ref[0] = (a_ref[0, 0].astype(F32) + b_ref[0].astype(F32)).astype(o_ref.dtype)

    cidx = lax.axis_index("c").astype(jnp.int32).reshape(1)
    return _pcall(
        body, name, _sds(r4.shape, BF16), grid=(4, rh // tr),
        in_specs=[pl.BlockSpec((1, 1, tr, cols), lambda s, i, c_ref: (s, c_ref[0], i, 0)), pl.BlockSpec((1, tr, cols), lambda s, i, c_ref: (s, i, 0))],
        out_specs=pl.BlockSpec((1, tr, cols), lambda s, i, c_ref: (s, i, 0)), sem=("parallel", "parallel"), prefetch=1,
    )(cidx, g42, r4)


def _slot_sum(name, parts):
    n, r, cols = parts.shape
    cap = max(2 * SUBLANES, (4 * 1024 * 1024) // (n * cols * parts.dtype.itemsize))
    tr = _divisors(r, cap, 2 * SUBLANES)[-1]

    def body(p_ref, o_ref):
        acc = p_ref[0].astype(F32)
        for k in range(1, n):
            acc = acc + p_ref[k].astype(F32)
        o_ref[...] = acc

    return _pcall(
        body, name, _sds((r, cols), F32), grid=(r // tr,), in_specs=[pl.BlockSpec((n, tr, cols), lambda i: (0, i, 0))],
        out_specs=pl.BlockSpec((tr, cols), lambda i: (i, 0)), sem=("parallel",),
    )(parts)


def _slot_sums(name, parts):
    k = len(parts)

    def body(*refs):
        for p_ref, o_ref in zip(refs[:k], refs[k:]):
            acc = p_ref[0]
            for j in range(1, p_ref.shape[0]):
                acc = acc + p_ref[j]
            o_ref[...] = acc

    return list(_pcall(body, name, [_sds(p.shape[1:], F32) for p in parts])(*parts))


def _gather_weights_start(name, shards, token):
    c = lax.axis_index("c")
    halves = [lax.dynamic_slice_in_dim(w, c * (w.shape[0] // 2), w.shape[0] // 2, 0).astype(BF16) for w in shards]
    if token is not None:
        halves[0] = _after(halves[0], token)
    n = len(halves)

    def plan(x, y, c):
        return [(a, (), a, (2 * x + y, c), p, (_chip_of(p), c)) for a in range(n) for p in _chip_peers(x, y, c)]

    return _exchange_start(name + "_ag_chips", halves, [((4, 2) + h.shape, h.dtype) for h in halves], plan)


def _gather_weights_finish(name, started, after):
    halves, slots, token = _exchange_wait(started, after)
    n = len(halves)
    x, y, c = _position()
    slots = [lax.dynamic_update_slice(s, h[None, None], (2 * x + y, c, 0, 0)) for s, h in zip(slots, halves)]

    def plan_cores(x, y, c):
        return [(a, (s, c), a, (s, c), (x, y, 1 - c), (s, 1 - c)) for a in range(n) for s in range(4)]

    full, _ = _exchange(name + "_ag_cores", slots, None, plan_cores, inplace=True)
    return [f.reshape((4, 2 * h.shape[0], h.shape[1])) for f, h in zip(full, halves)], token


def _reduce_weights_start(name, grads, token):
    n = len(grads)
    g42 = [g.reshape(4, 2, g.shape[1] // 2, g.shape[2]) for g in grads]
    if token is not None:
        g42[0] = _after(g42[0], token)

    def plan_swap(x, y, c):
        return [(a, (s, 1 - c), a, (s,), (x, y, 1 - c), (s,)) for a in range(n) for s in range(4)]

    def plan_chips(x, y, c):
        return [(a, (_chip_of(p),), a, (2 * x + y,), p, (_chip_of(p),)) for a in range(n) for p in _chip_peers(x, y, c)]

    other, _ = _exchange(name + "_rs_cores", g42, [((4,) + g.shape[2:], g.dtype) for g in g42], plan_swap)
    pair = [_pair_add(f"{name}_rs_pair{a}", g, o) for a, (g, o) in enumerate(zip(g42, other))]
    return _exchange_start(name + "_rs_chips", pair, [(p.shape, p.dtype) for p in pair], plan_chips)


def _reduce_weights_finish(name, started, after):
    pair, recv, token = _exchange_wait(started, after)
    n = len(pair)
    x, y, c = _position()
    me = 2 * x + y

    def plan_share(x, y, c):
        return [(a, (c,), a, (c,), (x, y, 1 - c), (1 - c,)) for a in range(n)]

    parts = [lax.dynamic_update_slice(r, lax.dynamic_slice_in_dim(p, me, 1, 0), (me, 0, 0)) for r, p in zip(recv, pair)]
    mine = [_slot_sum(f"{name}_rs_sum{a}", p) for a, p in enumerate(parts)]
    both = [lax.dynamic_update_slice(lax.empty((2,) + m.shape, m.dtype), m[None], (c, 0, 0)) for m in mine]
    both, _ = _exchange(name + "_rs_share", both, None, plan_share, inplace=True)
    return [b.reshape(2 * b.shape[1], b.shape[2]) for b in both], token


def _allreduce_small(name, grads):
    both, token = _gather_two_level(name + "_ag", list(grads))
    return _slot_sums(name + "_sum", [g.reshape((8,) + g.shape[2:]) for g in both]), token


def _ada_fwd_call(name, c_all, w, b_shard):
    nl, d, ns = w.shape
    nb = c_all.shape[0]

    def body(c_ref, w_ref, b_ref, o_ref):
        cv = c_ref[...]
        o_ref[0] = _dg(cv * _sigmoid(cv), w_ref[0], ((1,), (0,))) + b_ref[0]

    return _pcall(
        body, name, _sds((nl, nb, ns), F32), grid=(nl,),
        in_specs=[pl.BlockSpec((nb, d), lambda l: (0, 0)), pl.BlockSpec((1, d, ns), lambda l: (l, 0, 0)), pl.BlockSpec((1, 1, ns), lambda l: (l, 0, 0))],
        out_specs=pl.BlockSpec((1, nb, ns), lambda l: (l, 0, 0)), sem=("parallel",),
    )(c_all, w, b_shard)


def _ada_bwd_call(name, c_all, dm_shard, dm_all):
    nl, nb, ns = dm_shard.shape
    d = c_all.shape[1]
    nm = dm_all.shape[2]

    def body(c_ref, ds_ref, da_ref, dw_ref, db_ref):
        cv = c_ref[...]
        dw_ref[0] = _dg(cv * _sigmoid(cv), ds_ref[0], ((0,), (0,)))
        db_ref[0] = jnp.sum(da_ref[0], axis=0, keepdims=True)

    return _pcall(
        body, name, [_sds((nl, d, ns), F32), _sds((nl, 1, nm), F32)], grid=(nl,),
        in_specs=[pl.BlockSpec((nb, d), lambda l: (0, 0)), pl.BlockSpec((1, nb, ns), lambda l: (l, 0, 0)), pl.BlockSpec((1, nb, nm), lambda l: (l, 0, 0))],
        out_specs=[pl.BlockSpec((1, d, ns), lambda l: (l, 0, 0)), pl.BlockSpec((1, 1, nm), lambda l: (l, 0, 0))], sem=("parallel",),
    )(c_all, dm_shard, dm_all)


def _ada_fwd(name, bl, c_all, w, b):
    nl, d, ns = w.shape
    chip = 2 * lax.axis_index("x") + lax.axis_index("y")
    b_shard = lax.dynamic_slice(b, (0, chip * ns), (nl, ns)).reshape(nl, 1, ns)
    shard = _ada_fwd_call(name + "_fwd", c_all, w, b_shard)
    (allc,), token = _gather_chips(name + "_ag", [shard])
    mods = jnp.transpose(allc, (1, 2, 0, 3)).reshape(nl, c_all.shape[0], 4 * ns)
    return lax.dynamic_slice(mods, (0, (2 * chip + lax.axis_index("c")) * bl, 0), (nl, bl, 4 * ns)), token


def _ada_bwd(name, bl, c_all, ns, dm):
    nl = dm.shape[0]
    chip = 2 * lax.axis_index("x") + lax.axis_index("y")
    (dm_all,), token = _gather_two_level(name + "_bwd_ag", [dm])
    dm_all = jnp.transpose(dm_all.reshape((8,) + dm.shape), (1, 0, 2, 3)).reshape(nl, 8 * bl, 4 * ns)
    dm_shard = lax.dynamic_slice(dm_all, (0, 0, chip * ns), (nl, 8 * bl, ns))
    dw, db = _ada_bwd_call(name + "_bwd", c_all, dm_shard, dm_all)
    return dw, db.reshape(nl, 4 * ns), token


def _adamw(name, w, g, m, v):
    shape = w.shape
    cols = shape[-1]
    w2, g2, m2, v2 = (t.reshape(-1, cols) for t in (w, g, m, v))
    rows = w2.shape[0]
    cap = max(SUBLANES, (512 * 1024) // max(cols, 1) // SUBLANES * SUBLANES)
    tr = _divisors(rows, cap, SUBLANES)[-1]

    def body(w_ref, g_ref, m_ref, v_ref, d_ref, mo_ref, vo_ref):
        gv = g_ref[...]
        mn = ADAM_B1 * m_ref[...] + (1.0 - ADAM_B1) * gv
        vn = ADAM_B2 * v_ref[...] + (1.0 - ADAM_B2) * (gv * gv)
        m_hat = mn / (1.0 - ADAM_B1 ** ADAM_STEP)
        v_hat = vn / (1.0 - ADAM_B2 ** ADAM_STEP)
        d_ref[...] = -ADAM_LR * (m_hat / (jnp.sqrt(v_hat) + ADAM_EPS) + ADAM_WD * w_ref[...])
        mo_ref[...] = mn
        vo_ref[...] = vn

    spec = pl.BlockSpec((tr, cols), lambda i: (i, 0))
    outs = _pcall(body, name, [_sds((rows, cols), F32)] * 3, grid=(rows // tr,), in_specs=[spec] * 4, out_specs=[spec] * 3, sem=("parallel",))(w2, g2, m2, v2)
    return tuple(o.reshape(shape) for o in outs)


def _adamw_small(name, ws, gs, ms, vs):
    n = len(ws)

    def body(*refs):
        for k in range(n):
            w_ref, g_ref, m_ref, v_ref = (refs[j * n + k] for j in range(4))
            d_ref, mo_ref, vo_ref = (refs[(4 + j) * n + k] for j in range(3))
            gv = g_ref[...]
            mn = ADAM_B1 * m_ref[...] + (1.0 - ADAM_B1) * gv
            vn = ADAM_B2 * v_ref[...] + (1.0 - ADAM_B2) * (gv * gv)
            m_hat = mn / (1.0 - ADAM_B1 ** ADAM_STEP)
            v_hat = vn / (1.0 - ADAM_B2 ** ADAM_STEP)
            d_ref[...] = -ADAM_LR * (m_hat / (jnp.sqrt(v_hat) + ADAM_EPS) + ADAM_WD * w_ref[...])
            mo_ref[...] = mn
            vo_ref[...] = vn

    outs = _pcall(body, name, [_sds(w.shape, F32) for w in ws] * 3)(*ws, *gs, *ms, *vs)
    return outs[:n], outs[n:2 * n], outs[2 * n:]


def _pad_rows(w, rows):
    return jnp.pad(w, ((0, rows - w.shape[0]), (0, 0)))


def _pad_lanes(v):
    return jnp.pad(v, (0, LANES - v.shape[0])).reshape(1, LANES)


BIG = ("w_in", "w_out", "ff_up", "ff_down")
BIG_AXIS = {"w_in": 1, "w_out": 0, "ff_up": 1, "ff_down": 0}
CONVW = ("ssd_conv_w", "ff_conv_w")
SMALL = ("norm1_g", "norm2_g", "ssd_conv_b", "ssd_dt_bias", "ssd_a_log", "ssd_d", "ssd_norm_g", "gm_vnorm_g", "gm_ws", "gm_bs", "gm_out_g", "ff_conv_b")
WEIGHTS = ("ada_w", "ada_b", "norm1_g", "norm2_g", "w_in", "ssd_conv_w", "ssd_conv_b", "ssd_dt_bias", "ssd_a_log", "ssd_d", "ssd_norm_g", "gm_vnorm_g", "gm_ws", "gm_bs", "gm_out_g", "w_out", "ff_up", "ff_conv_w", "ff_conv_b", "ff_down", "final_g")


def kernel(x, c, ada_w, ada_b, norm1_g, norm2_g, w_in, ssd_conv_w, ssd_conv_b, ssd_dt_bias, ssd_a_log, ssd_d, ssd_norm_g, gm_vnorm_g, gm_ws, gm_bs, gm_out_g, w_out, ff_up, ff_conv_w, ff_conv_b, ff_down, final_g, loss_target, m_ada_w, m_ada_b, m_norm1_g, m_norm2_g, m_w_in, m_ssd_conv_w, m_ssd_conv_b, m_ssd_dt_bias, m_ssd_a_log, m_ssd_d, m_ssd_norm_g, m_gm_vnorm_g, m_gm_ws, m_gm_bs, m_gm_out_g, m_w_out, m_ff_up, m_ff_conv_w, m_ff_conv_b, m_ff_down, m_final_g, v_ada_w, v_ada_b, v_norm1_g, v_norm2_g, v_w_in, v_ssd_conv_w, v_ssd_conv_b, v_ssd_dt_bias, v_ssd_a_log, v_ssd_d, v_ssd_norm_g, v_gm_vnorm_g, v_gm_ws, v_gm_bs, v_gm_out_g, v_w_out, v_ff_up, v_ff_conv_w, v_ff_conv_b, v_ff_down, v_final_g):
    given = dict(locals())
    weights = {n: given[n] for n in WEIGHTS}
    bl, s, d = x.shape
    nl = ada_w.shape[0]
    heads = d // HEAD_DIM
    cd = d + 2 * SSD_GROUPS * STATE
    f = ff_down.shape[1] * 4
    n_in = d + cd + heads + 2 * d
    n_proj = _round_up(3 * d + cd + LANES, 2 * LANES)
    tm = _divisors(s, 512)[-1]

    gathering = {}

    def start_gathers(l, names, token):
        for n in names:
            gathering[l, n] = _gather_weights_start(f"wg{l}_{n}", [weights[n][l]], token)
            token = gathering[l, n]["token"]
        return token

    issued = start_gathers(0, BIG[:1], None)
    pre, _ = _gather_two_level("pre_ag", [_after(c, issued)] + [weights[n] for n in CONVW])
    c_all = pre[0].reshape(8 * bl, d)
    conv_full = {n: jnp.concatenate([p[k, 0] for k in range(4)], axis=-1) for n, p in zip(CONVW, pre[1:])}
    mods_all, ada_token = _ada_fwd("ada", bl, c_all, ada_w, ada_b)
    mods = mods_all.reshape(nl, bl, N_MOD, 1, d)

    def landed(l, n, after):
        (full,), token = _gather_weights_finish(f"wg{l}_{n}", gathering.pop((l, n)), after)
        if n == "w_in":
            return full, token
        return (full.reshape(-1, full.shape[2]) if BIG_AXIS[n] == 0 else jnp.concatenate([full[k] for k in range(4)], axis=1)), token

    issued = start_gathers(0, BIG[1:], ada_token)
    r2 = lambda v: v.reshape(1, -1)
    n_gm = d // GM_HEAD
    fg = r2(final_g)

    def layer_params(l):
        return dict(
            norm1=r2(norm1_g[l]), norm2=r2(norm2_g[l]), conv_w8=_pad_rows(conv_full["ssd_conv_w"][l], SUBLANES), conv_b=r2(ssd_conv_b[l]),
            dtb=_pad_lanes(ssd_dt_bias[l]), alog=_pad_lanes(ssd_a_log[l]), dsk=r2(jnp.repeat(ssd_d[l], HEAD_DIM)), ng=r2(ssd_norm_g[l]),
            vg=r2(gm_vnorm_g[l]), ws=gm_ws[l], bs_t=jnp.pad(gm_bs[l].T, ((0, 0), (0, LANES - n_gm))), og=r2(gm_out_g[l]),
            ff_w8=_pad_rows(conv_full["ff_conv_w"][l], SUBLANES), ff_b=r2(ff_conv_b[l]))

    def mixer_args(p):
        return (p["conv_w8"], p["conv_b"], p["dtb"], p["alog"], p["dsk"], p["ng"], p["vg"], p["ws"], p["bs_t"], p["og"])

    def padded_w_in(full):
        ns = full.shape[2]

        def cols(a, b):
            return [full[k][:, max(a, k * ns) - k * ns:min(b, (k + 1) * ns) - k * ns] for k in range(4) if max(a, k * ns) < min(b, (k + 1) * ns)]

        parts = cols(0, d) + cols(d + cd + heads, n_in) + cols(d, d + cd) + cols(d + cd, d + cd + heads)
        return jnp.concatenate(parts + [jnp.zeros((d, n_proj - (3 * d + cd + heads)), BF16)], axis=1)

    saved, xcur, pending = [], x, None
    for l in range(nl):
        p, w = layer_params(l), {}
        wi, token = landed(l, "w_in", mods_all if l == 0 else pending[0])
        w["w_in"] = padded_w_in(wi)
        sh1, sc1, g1, sh2, sc2, g2 = (mods[l, :, k] for k in range(N_MOD))
        if l + 1 < nl:
            issued = start_gathers(l + 1, BIG, token + issued)
        sc1 = _after(sc1, issued)
        if pending is None:
            (h,) = _row_fwd(f"nm{l}_fwd", _nm_tile, [(xcur, d, 0, None)], [sc1, sh1], [p["norm1"]], [(d, BF16)], 0, tm)
            x1 = xcur
        else:
            rows = [(xcur, d, 0, None), (pending[0], d, 0, None)]
            x1, h = _row_fwd(f"rnm{l}a_fwd", _rnm_tile, rows, [pending[1], sc1, sh1], [p["norm1"]], [(d, F32), (d, BF16)], 0, tm)
        proj = _mm(f"win{l}_fwd", h, w["w_in"], F32)
        ycat, (xbc, hprev) = _mixer_fwd(f"mix{l}", proj, d, cd, *mixer_args(p))
        w["w_out"], _ = landed(l, "w_out", ycat)
        mix = _mm(f"wout{l}_fwd", ycat, w["w_out"], F32)
        x2, h2 = _row_fwd(f"rnm{l}b_fwd", _rnm_tile, [(x1, d, 0, None), (mix, d, 0, None)], [g1, sc2, sh2], [p["norm2"]], [(d, F32), (d, BF16)], 0, tm)
        w_up, _ = landed(l, "ff_up", h2)
        w["ff_up"] = _gate_value_blocks(f"ffup{l}_blocks", w_up, f)
        up = _mm(f"ffup{l}_fwd", h2, w["ff_up"], F32)
        act = _conv_fwd(f"ffact{l}_fwd", up, 0, f, p["ff_w8"], p["ff_b"], FF_CONV, True, BF16)
        w["ff_down"], _ = landed(l, "ff_down", act)
        down = _mm(f"ffdown{l}_fwd", act, w["ff_down"], F32)
        saved.append(dict(p=p, w=w, x_in=xcur, pending=pending, h=h, proj=proj, xbc=xbc, hprev=hprev, ycat=ycat, x1=x1, mix=mix, h2=h2, up=up, act=act))
        xcur, pending = x2, (down, g2)
    rows = [(xcur, d, 0, F32), (pending[0], d, 0, BF16), (loss_target, d, 0, None)]
    (loss_local,) = _row_fwd("final_fwd", _final_tile, rows, [pending[1]], [fg], [], 1, tm)
    loss = lax.psum(loss_local[0, 0], AXES)

    (dx2, ddown), (dg2,), (dfg,) = _row_bwd("final_bwd", _final_tile, rows, [pending[1]], [fg], [], 1, tm, [jnp.ones((1, 1), F32)])
    dmods, small_grads = [None] * nl, [None] * nl
    reducing = {}

    def start_reduce(l, n, g):
        width = g.shape[1] // 4
        by_chip = g.reshape(4, g.shape[0] // 4, g.shape[1]) if BIG_AXIS[n] == 0 else jnp.stack([g[:, k * width:(k + 1) * width] for k in range(4)])
        reducing[l, n] = _reduce_weights_start(f"wg{l}_{n}", [by_chip], None)
        return reducing[l, n]["token"]

    for l in reversed(range(nl)):
        sv = saved[l]
        p, w = sv["p"], sv["w"]
        sh1, sc1, g1, sh2, sc2, g2 = (mods[l, :, k] for k in range(N_MOD))
        dact, dw_down = _mm_bwd(f"ffdown{l}", sv["act"], w["ff_down"], ddown)
        ff_b = _after(p["ff_b"], start_reduce(l, "ff_down", dw_down))
        dup, dff_w8, dff_b = _conv_bwd(f"ffact{l}_bwd", sv["up"], 0, f, p["ff_w8"], ff_b, FF_CONV, True, dact, BF16)
        dh2, dw_up = _mm_bwd(f"ffup{l}", sv["h2"], w["ff_up"], dup)
        sc2 = _after(sc2, start_reduce(l, "ff_up", _gate_value_blocks(f"ffup{l}_columns", dw_up, f, inverse=True)))
        rows = [(sv["x1"], d, 0, F32), (sv["mix"], d, 0, BF16)]
        (dx1, dmix), (dg1, dsc2, dsh2), (dn2,) = _row_bwd(f"rnm{l}b_bwd", _rnm_tile, rows, [g1, sc2, sh2], [p["norm2"]], [(d, F32), (d, BF16)], 0, tm, [dx2, dh2])
        dycat, dw_out = _mm_bwd(f"wout{l}", sv["ycat"], w["w_out"], dmix)
        p_tied = dict(p, dtb=_after(p["dtb"], start_reduce(l, "w_out", dw_out)))
        dproj, (dw8, dcb, ddtb, dalog, ddsk, dng, dvg, dws, dbs_t, dog) = _mixer_bwd(f"mix{l}", sv["proj"], d, cd, *mixer_args(p_tied), sv["xbc"], sv["hprev"], dycat)
        dh, dw_in_p = _mm_bwd(f"win{l}", sv["h"], w["w_in"], dproj)
        dw_in = jnp.concatenate([dw_in_p[:, :d], dw_in_p[:, 3 * d:3 * d + cd], dw_in_p[:, 3 * d + cd:3 * d + cd + heads], dw_in_p[:, d:3 * d]], axis=1)
        if l > 0:
            sc1 = _after(sc1, start_reduce(l, "w_in", dw_in))
        if sv["pending"] is None:
            (dx2,), (dsc1, dsh1), (dn1,) = _row_bwd(f"nm{l}_bwd", _nm_res_tile, [(sv["x_in"], d, 0, F32)], [sc1, sh1], [p["norm1"]], [(d, F32), (d, BF16)], 0, tm, [dx1, dh])
        else:
            rows = [(sv["x_in"], d, 0, F32), (sv["pending"][0], d, 0, BF16)]
            (dx2, ddown), (dg2_prev, dsc1, dsh1), (dn1,) = _row_bwd(
                f"rnm{l}a_bwd", _rnm_tile, rows, [sv["pending"][1], sc1, sh1], [p["norm1"]], [(d, F32), (d, BF16)], 0, tm, [dx1, dh])
        dmods[l] = jnp.concatenate([dsh1, dsc1, dg1, dsh2, dsc2, dg2], axis=1).reshape(bl, N_MOD * d)
        if sv["pending"] is not None:
            dg2 = dg2_prev
        small_grads[l] = dict(
            norm1_g=dn1.reshape(d), norm2_g=dn2.reshape(d), ssd_conv_b=dcb.reshape(cd), ssd_dt_bias=ddtb[0, :heads], ssd_a_log=dalog[0, :heads],
            ssd_d=ddsk.reshape(heads, HEAD_DIM).sum(-1), ssd_norm_g=dng.reshape(d), gm_vnorm_g=dvg.reshape(d), gm_ws=dws, gm_bs=dbs_t[:, :n_gm].T,
            gm_out_g=dog.reshape(d), ff_conv_b=dff_b.reshape(f), ssd_conv_w=dw8[:SSD_CONV], ff_conv_w=dff_w8[:FF_CONV])
    grad_x = dx2

    g_ada_w, g_ada_b, ada_token = _ada_bwd("ada", bl, c_all, ada_w.shape[2], jnp.stack(dmods))
    small_names_r = SMALL + CONVW + ("final_g",)
    summed, small_token = _allreduce_small("small", [jnp.stack([small_grads[l][n] for l in range(nl)]) for n in SMALL + CONVW] + [dfg])
    start_reduce(0, "w_in", _after(dw_in, ada_token + small_token))
    grad = {"ada_w": g_ada_w, "ada_b": g_ada_b}
    grad.update(zip(small_names_r, summed))
    chip = 2 * lax.axis_index("x") + lax.axis_index("y")
    for n in CONVW:
        width = weights[n].shape[-1]
        grad[n] = lax.dynamic_slice_in_dim(grad[n], chip * width, width, axis=2)

    delta, new_m, new_v = {}, {}, {}
    big_grads, done = {}, summed[0]
    for n in ("ada_w",) + tuple(reversed(BIG)):
        if n != "ada_w":
            for l in reversed(range(nl)):
                (big_grads[l, n],), _ = _reduce_weights_finish(f"wg{l}_{n}", reducing.pop((l, n)), done)
            grad[n] = jnp.stack([big_grads[l, n] for l in range(nl)])
        delta[n], new_m[n], new_v[n] = _adamw("adam_" + n, weights[n], grad[n], given["m_" + n], given["v_" + n])
        done = delta[n]
    small_names = ("ada_b",) + SMALL + CONVW + ("final_g",)
    as2d = lambda t: t.reshape(1, -1) if t.ndim == 1 else t
    res = _adamw_small(
        "adam_small", [as2d(weights[n]) for n in small_names], [as2d(grad[n]) for n in small_names],
        [as2d(given["m_" + n]) for n in small_names], [as2d(given["v_" + n]) for n in small_names])
    for out, vals in zip((delta, new_m, new_v), res):
        for n, val in zip(small_names, vals):
            out[n] = val.reshape(weights[n].shape)
    grad["final_g"] = grad["final_g"].reshape(final_g.shape)

    return (loss, grad_x, *[grad[n] for n in WEIGHTS], *[delta[n] for n in WEIGHTS], *[new_m[n] for n in WEIGHTS], *[new_v[n] for n in WEIGHTS])
```

```python
import functools

import jax
import jax.numpy as jnp
from jax import lax
from jax.experimental import pallas as pl
from jax.experimental.pallas import tpu as pltpu

F32 = jnp.float32
BF16 = jnp.bfloat16
EPS = 1e-6
CHUNK = 128
HEAD_DIM = 64
STATE = 128
GM_HEAD = 128
SSD_GROUPS = 2
SSD_CONV = 4
FF_CONV = 3
N_MOD = 6
LANES = 128
SUBLANES = 8
SMALL_BF16_SIZE = 1 << 16
CONV_LANES = 256
CONV_ROWS = 32
V7X_VMEM_LIMIT = 48 * 1024 * 1024
MM_VMEM_BUDGET = 30 * 1024 * 1024
MM_STEP_MACS = 2048 * 1024 * 1024
MM_ACC_SECONDS_PER_VREG = 1.4e-9
V7X_MXU_FLOPS = 996e12
V7X_HBM_BYTES_PER_S = 3.3e12
V7X_STEP_SECONDS = 0.35e-6
ADAM_LR, ADAM_B1, ADAM_B2, ADAM_EPS, ADAM_WD, ADAM_STEP = 0.001, 0.9, 0.999, 1e-08, 0.01, 10
MESH = pl.DeviceIdType.MESH
AXES = ("x", "y", "c")


def _round_up(n, m):
    return (n + m - 1) // m * m


def _divisors(n, cap, mult=LANES):
    out = [t for t in range(mult, min(n, cap) + 1, mult) if n % t == 0]
    return out or [n]


def _pcall(body, name, out_shape, grid=(), in_specs=None, out_specs=None, scratch=(), sem=None, prefetch=0):
    params = pltpu.CompilerParams(dimension_semantics=sem, vmem_limit_bytes=V7X_VMEM_LIMIT)
    if prefetch:
        spec = pltpu.PrefetchScalarGridSpec(num_scalar_prefetch=prefetch, grid=grid, in_specs=in_specs, out_specs=out_specs, scratch_shapes=list(scratch))
        return pl.pallas_call(body, name=name, out_shape=out_shape, grid_spec=spec, compiler_params=params)
    if in_specs is None:
        return pl.pallas_call(body, name=name, out_shape=out_shape, compiler_params=params)
    return pl.pallas_call(body, name=name, out_shape=out_shape, grid=grid, in_specs=in_specs, out_specs=out_specs, scratch_shapes=list(scratch), compiler_params=params)


def _sds(shape, dtype):
    return jax.ShapeDtypeStruct(tuple(shape), dtype)


def _mm_tiles(m, n, k, a_bytes, b_bytes, o_bytes):
    best, best_key = None, None
    for tm in _divisors(m, 2048):
        for tn in _divisors(n, 2560):
            for tk in _divisors(k, 2560):
                vmem = 2 * (tm * tk * a_bytes + tk * tn * b_bytes + tm * tn * o_bytes) + tm * tn * 4
                if vmem > MM_VMEM_BUDGET or tm * tn * tk > MM_STEP_MACS:
                    continue
                ni, nj, nk = m // tm, n // tn, k // tk
                a_reads = 1 if nk == 1 else nj
                b_reads = 1 if (nk == 1 and nj == 1) else ni
                hbm = a_reads * m * k * a_bytes + b_reads * k * n * b_bytes + m * n * o_bytes
                t = max(2.0 * m * n * k / V7X_MXU_FLOPS, hbm / V7X_HBM_BYTES_PER_S) + ni * nj * nk * V7X_STEP_SECONDS
                if nk > 1:
                    t += ni * nj * nk * (tm * tn // (SUBLANES * LANES)) * MM_ACC_SECONDS_PER_VREG
                key = (-t, tm * tn * tk)
                if best_key is None or key > best_key:
                    best, best_key = (tm, tn, tk), key
    return best


def _matmul(name, a, b, mode, out_dtype):
    if mode == "nn":
        (m, k), n = a.shape, b.shape[1]
    elif mode == "nt":
        (m, k), n = a.shape, b.shape[0]
    else:
        (k, m), n = a.shape, b.shape[1]
    tm, tn, tk = _mm_tiles(m, n, k, a.dtype.itemsize, b.dtype.itemsize, jnp.dtype(out_dtype).itemsize)
    nk = k // tk
    if mode == "nn":
        a_spec = pl.BlockSpec((tm, tk), lambda i, j, kk: (i, kk))
        b_spec = pl.BlockSpec((tk, tn), lambda i, j, kk: (kk, j))
        dims = ((1,), (0,))
    elif mode == "nt":
        a_spec = pl.BlockSpec((tm, tk), lambda i, j, kk: (i, kk))
        b_spec = pl.BlockSpec((tn, tk), lambda i, j, kk: (j, kk))
        dims = ((1,), (1,))
    else:
        a_spec = pl.BlockSpec((tk, tm), lambda i, j, kk: (kk, i))
        b_spec = pl.BlockSpec((tk, tn), lambda i, j, kk: (kk, j))
        dims = ((0,), (0,))

    def body(a_ref, b_ref, o_ref, acc_ref):
        kk = pl.program_id(2)
        p = lax.dot_general(a_ref[...].astype(BF16), b_ref[...].astype(BF16), (dims, ((), ())), preferred_element_type=F32)
        if nk == 1:
            o_ref[...] = p.astype(o_ref.dtype)
        else:
            @pl.when(kk == 0)
            def _():
                acc_ref[...] = p

            @pl.when(kk > 0)
            def _():
                acc_ref[...] += p

            @pl.when(kk == nk - 1)
            def _():
                o_ref[...] = acc_ref[...].astype(o_ref.dtype)

    return _pcall(
        body, name, _sds((m, n), out_dtype), grid=(m // tm, n // tn, nk), in_specs=[a_spec, b_spec],
        out_specs=pl.BlockSpec((tm, tn), lambda i, j, kk: (i, j)), scratch=[pltpu.VMEM((tm, tn), F32)],
        sem=("parallel", "parallel", "arbitrary"),
    )(a, b)


def _mm(name, a, w, out_dtype):
    return _matmul(name, a.reshape(-1, a.shape[-1]), w, "nn", out_dtype).reshape(a.shape[:-1] + (w.shape[1],))


def _mm_bwd(name, a, w, dy):
    a2, dy2 = a.reshape(-1, a.shape[-1]), dy.reshape(-1, dy.shape[-1])
    return _matmul(name + "_dx", dy2, w, "nt", BF16).reshape(a.shape), _matmul(name + "_dw", a2, dy2, "tn", BF16)


def _dg(a, b, dims):
    return lax.dot_general(a.astype(BF16), b.astype(BF16), (dims, ((), ())), preferred_element_type=F32)


@jax.custom_vjp
def _dot_nn(a, b):
    return _dg(a, b, ((1,), (0,)))


_dot_nn.defvjp(lambda a, b: (_dot_nn(a, b), (a, b)), lambda r, d: (_dg(d, r[1], ((1,), (1,))), _dg(r[0], d, ((0,), (0,)))))


@jax.custom_vjp
def _dot_nt(a, b):
    return _dg(a, b, ((1,), (1,)))


_dot_nt.defvjp(lambda a, b: (_dot_nt(a, b), (a, b)), lambda r, d: (_dg(d, r[1], ((1,), (0,))), _dg(d, r[0], ((0,), (0,)))))


@jax.custom_vjp
def _dot_tn(a, b):
    return _dg(a, b, ((0,), (0,)))


_dot_tn.defvjp(lambda a, b: (_dot_tn(a, b), (a, b)), lambda r, d: (_dg(r[1], d, ((1,), (1,))), _dg(r[0], d, ((1,), (0,)))))


def _exact_dot(a, c, dims):
    hi = a.astype(BF16)
    r1 = a - hi.astype(F32)
    mid = r1.astype(BF16)
    lo = (r1 - mid.astype(F32)).astype(BF16)
    cb = c.astype(BF16)
    f = lambda t: lax.dot_general(t, cb, (dims, ((), ())), preferred_element_type=F32)
    return f(hi) + f(mid) + f(lo)


@jax.custom_vjp
def _sel_right(a, c):
    return _exact_dot(a, c, ((1,), (0,)))


_sel_right.defvjp(lambda a, c: (_sel_right(a, c), c), lambda c, d: (_exact_dot(d, c, ((1,), (1,))), jnp.zeros_like(c)))


def _exact_dot_left(c, a, dims):
    hi = a.astype(BF16)
    r1 = a - hi.astype(F32)
    mid = r1.astype(BF16)
    lo = (r1 - mid.astype(F32)).astype(BF16)
    cb = c.astype(BF16)
    f = lambda t: lax.dot_general(cb, t, (dims, ((), ())), preferred_element_type=F32)
    return f(hi) + f(mid) + f(lo)


@jax.custom_vjp
def _sel_left(c, a):
    return _exact_dot_left(c, a, ((1,), (0,)))


_sel_left.defvjp(lambda c, a: (_sel_left(c, a), c), lambda c, d: (jnp.zeros_like(c), _exact_dot_left(c, d, ((0,), (0,)))))


def _sigmoid(x):
    return 1.0 / (1.0 + jnp.exp(-x))


def _rms(x, g):
    return x * lax.rsqrt(jnp.mean(x * x, axis=-1, keepdims=True) + EPS) * g


def _gelu(x):
    return 0.5 * x * (1.0 + lax.erf(x * (2.0 ** -0.5)))


def _causal(n):
    return lax.broadcasted_iota(jnp.int32, (n, n), 0) >= lax.broadcasted_iota(jnp.int32, (n, n), 1)


def _row_in_specs(rows, bparams, gparams, tm):
    specs = [pl.BlockSpec((1, tm, w), lambda b, i, cb=cb: (b, i, cb)) for (_, w, cb, _) in rows]
    specs += [pl.BlockSpec((1, 1, p.shape[-1]), lambda b, i: (b, 0, 0)) for p in bparams]
    specs += [pl.BlockSpec(p.shape, lambda b, i, n=p.ndim: (0,) * n) for p in gparams]
    return specs


def _row_vals(refs, n_rows, n_b, n_g):
    vals = [r[0].astype(F32) for r in refs[:n_rows]]
    vals += [r[0].astype(F32) for r in refs[n_rows:n_rows + n_b]]
    vals += [r[...].astype(F32) for r in refs[n_rows + n_b:n_rows + n_b + n_g]]
    return vals


def _row_fwd(name, tile, rows, bparams, gparams, outs, n_sum, tm):
    bl, s = rows[0][0].shape[:2]
    n_in = len(rows) + len(bparams) + len(gparams)

    def body(*refs):
        first = (pl.program_id(0) == 0) & (pl.program_id(1) == 0)
        res = tile(*_row_vals(refs, len(rows), len(bparams), len(gparams)))
        o_refs = refs[n_in:]
        for k in range(len(outs)):
            o_refs[k][0] = res[k].astype(o_refs[k].dtype)
        for k in range(n_sum):
            o_ref, val = o_refs[len(outs) + k], res[len(outs) + k]

            @pl.when(first)
            def _(o_ref=o_ref, val=val):
                o_ref[...] = val

            @pl.when(jnp.logical_not(first))
            def _(o_ref=o_ref, val=val):
                o_ref[...] += val

    out_shape = [_sds((bl, s, w), dt) for (w, dt) in outs] + [_sds((1, 1), F32)] * n_sum
    out_specs = [pl.BlockSpec((1, tm, w), lambda b, i: (b, i, 0)) for (w, _) in outs] + [pl.BlockSpec((1, 1), lambda b, i: (0, 0))] * n_sum
    return _pcall(
        body, name, out_shape, grid=(bl, s // tm), in_specs=_row_in_specs(rows, bparams, gparams, tm), out_specs=out_specs,
        sem=("arbitrary", "arbitrary"),
    )(*[r[0] for r in rows], *bparams, *gparams)


def _row_bwd(name, tile, rows, bparams, gparams, outs, n_sum, tm, cts):
    bl, s = rows[0][0].shape[:2]
    n_r, n_b, n_g = len(rows), len(bparams), len(gparams)
    n_in = n_r + n_b + n_g
    n_ct = len(outs) + n_sum
    grad_rows = [k for k in range(n_r) if rows[k][3]]

    def body(*refs):
        b, i = pl.program_id(0), pl.program_id(1)
        vals = _row_vals(refs, n_r, n_b, n_g)
        ct_refs = refs[n_in:n_in + n_ct]
        ct = [r[0].astype(F32) for r in ct_refs[:len(outs)]] + [r[...] for r in ct_refs[len(outs):]]
        _, vjp = jax.vjp(tile, *vals)
        grads = vjp(tuple(ct))
        o_refs = refs[n_in + n_ct:]
        for j, k in enumerate(grad_rows):
            o_refs[j][0] = grads[k].astype(o_refs[j].dtype)
        for k in range(n_b):
            o_ref, val = o_refs[len(grad_rows) + k], grads[n_r + k]

            @pl.when(i == 0)
            def _(o_ref=o_ref, val=val):
                o_ref[0] = val

            @pl.when(i > 0)
            def _(o_ref=o_ref, val=val):
                o_ref[0] += val

        first = (b == 0) & (i == 0)
        for k in range(n_g):
            o_ref, val = o_refs[len(grad_rows) + n_b + k], grads[n_r + n_b + k]

            @pl.when(first)
            def _(o_ref=o_ref, val=val):
                o_ref[...] = val

            @pl.when(jnp.logical_not(first))
            def _(o_ref=o_ref, val=val):
                o_ref[...] += val

    in_specs = _row_in_specs(rows, bparams, gparams, tm)
    in_specs += [pl.BlockSpec((1, tm, w), lambda b, i: (b, i, 0)) for (w, _) in outs] + [pl.BlockSpec((1, 1), lambda b, i: (0, 0))] * n_sum
    out_shape = [_sds((bl, s, rows[k][1]), rows[k][3]) for k in grad_rows]
    out_shape += [_sds(p.shape, F32) for p in bparams] + [_sds(p.shape, F32) for p in gparams]
    out_specs = [pl.BlockSpec((1, tm, rows[k][1]), lambda b, i: (b, i, 0)) for k in grad_rows]
    out_specs += [pl.BlockSpec((1, 1, p.shape[-1]), lambda b, i: (b, 0, 0)) for p in bparams]
    out_specs += [pl.BlockSpec(p.shape, lambda b, i, n=p.ndim: (0,) * n) for p in gparams]
    res = _pcall(
        body, name, out_shape, grid=(bl, s // tm), in_specs=in_specs, out_specs=out_specs, sem=("arbitrary", "arbitrary"),
    )(*[r[0] for r in rows], *bparams, *gparams, *cts)
    return res[:len(grad_rows)], res[len(grad_rows):len(grad_rows) + n_b], res[len(grad_rows) + n_b:]


def _nm_tile(x, sc, sh, g):
    return (_rms(x, g) * (1.0 + sc) + sh,)


def _nm_res_tile(x, sc, sh, g):
    return x, _rms(x, g) * (1.0 + sc) + sh


def _rnm_tile(x, o, gate, sc, sh, g):
    xn = x + gate * o
    return xn, _rms(xn, g) * (1.0 + sc) + sh


def _final_tile(x, o, tgt, gate, g):
    e = _rms(x + gate * o, g) - tgt
    return (0.5 * jnp.sum(jnp.mean(e * e, axis=-1, keepdims=True), axis=0, keepdims=True),)


def _gm_tile(u_in, v_in, vg, ws, bs_t, og):
    d = u_in.shape[1]
    u, vn = _gelu(u_in), _rms(_gelu(v_in), vg)
    causal = _causal(CHUNK)
    lane = lax.broadcasted_iota(jnp.int32, (1, LANES), 1)
    parts = []
    for h in range(d // GM_HEAD):
        bias = jnp.sum(bs_t * (lane == h).astype(F32), axis=1, keepdims=True)
        parts.append(_dot_nn(jnp.where(causal, ws[h], 0.0), vn[:, h * GM_HEAD:(h + 1) * GM_HEAD]) + bias)
    return (_rms(u * jnp.concatenate(parts, axis=1), og),)


def _conv_window(ref, r0, rows, c0, tc, seq, before, after):
    parts = []
    if before:
        p0 = pl.multiple_of(jnp.maximum(r0 - SUBLANES, 0), SUBLANES)
        parts.append(jnp.where(r0 > 0, ref[0, pl.ds(p0, SUBLANES), pl.ds(c0, tc)].astype(F32), 0.0))
    parts.append(ref[0, pl.ds(r0, rows), pl.ds(c0, tc)].astype(F32))
    if after:
        n0 = pl.multiple_of(jnp.minimum(r0 + rows, seq - SUBLANES), SUBLANES)
        parts.append(jnp.where(r0 + rows < seq, ref[0, pl.ds(n0, SUBLANES), pl.ds(c0, tc)].astype(F32), 0.0))
    return jnp.concatenate(parts, axis=0) if len(parts) > 1 else parts[0]


def _conv_taps_pre(xe, w, b, taps, rows):
    pre = xe[SUBLANES:SUBLANES + rows] * w[taps - 1] + b
    for j in range(1, taps):
        pre = pre + pltpu.roll(xe, j, 0)[SUBLANES:SUBLANES + rows] * w[taps - 1 - j]
    return pre


def _conv_fwd(name, src, col0, chans, w8, b, taps, gated, out_dtype):
    bl, s, _ = src.shape
    tc = CONV_LANES
    xw = 2 * tc if gated else tc

    def body(x_ref, w_ref, b_ref, o_ref):
        w = [w_ref[k:k + 1, :] for k in range(taps)]
        bias = b_ref[...]

        def step(c, carry):
            r0 = pl.multiple_of(c * CONV_ROWS, CONV_ROWS)
            xe = _conv_window(x_ref, r0, CONV_ROWS, 0, tc, s, True, False)
            pre = _conv_taps_pre(xe, w, bias, taps, CONV_ROWS)
            y = pre * _sigmoid(pre)
            if gated:
                y = y * x_ref[0, pl.ds(r0, CONV_ROWS), pl.ds(tc, tc)]
            o_ref[0, pl.ds(r0, CONV_ROWS), :] = y.astype(o_ref.dtype)
            return carry

        lax.fori_loop(0, s // CONV_ROWS, step, 0, unroll=4)

    first = 0 if gated else col0 // tc
    return _pcall(
        body, name, _sds((bl, s, chans), out_dtype), grid=(chans // tc, bl),
        in_specs=[pl.BlockSpec((1, s, xw), lambda j, bb: (bb, 0, first + j)), pl.BlockSpec((SUBLANES, tc), lambda j, bb: (0, j)), pl.BlockSpec((1, tc), lambda j, bb: (0, j))],
        out_specs=pl.BlockSpec((1, s, tc), lambda j, bb: (bb, 0, j)), sem=("parallel", "arbitrary"),
    )(src, w8, b)


def _conv_bwd(name, src, col0, chans, w8, b, taps, gated, dy, dx_dtype):
    bl, s, _ = src.shape
    tc = CONV_LANES
    xw = 2 * tc if gated else tc
    ext = CONV_ROWS + SUBLANES

    def body(x_ref, w_ref, b_ref, dy_ref, dx_ref, dw_ref, db_ref):
        bb = pl.program_id(1)
        w = [w_ref[k:k + 1, :] for k in range(taps)]
        bias = b_ref[...]

        def fold(v):
            acc = v[0:SUBLANES]
            for i in range(1, CONV_ROWS // SUBLANES):
                acc = acc + v[i * SUBLANES:(i + 1) * SUBLANES]
            return acc

        def step(c, carry):
            r0 = pl.multiple_of(c * CONV_ROWS, CONV_ROWS)
            xe = _conv_window(x_ref, r0, CONV_ROWS, 0, tc, s, True, True)
            pre = _conv_taps_pre(xe, w, bias, taps, ext)
            sig = _sigmoid(pre)
            d = _conv_window(dy_ref, r0, CONV_ROWS, 0, tc, s, False, True)
            dsil = d * _conv_window(x_ref, r0, CONV_ROWS, tc, tc, s, False, True) if gated else d
            dpre = dsil * sig * (1.0 + pre * (1.0 - sig))
            dx = dpre[:CONV_ROWS] * w[taps - 1]
            for j in range(1, taps):
                dx = dx + pltpu.roll(dpre, ext - j, 0)[:CONV_ROWS] * w[taps - 1 - j]
            dx_ref[0, pl.ds(r0, CONV_ROWS), pl.ds(0, tc)] = dx.astype(dx_ref.dtype)
            if gated:
                dx_ref[0, pl.ds(r0, CONV_ROWS), pl.ds(tc, tc)] = (d[:CONV_ROWS] * (pre * sig)[:CONV_ROWS]).astype(dx_ref.dtype)
            here = dpre[:CONV_ROWS]
            sums = [fold(here * (pltpu.roll(xe, taps - 1 - k, 0) if k < taps - 1 else xe)[SUBLANES:ext]) + carry[k] for k in range(taps)]
            return tuple(sums) + (fold(here) + carry[taps],)

        zero = jnp.zeros((SUBLANES, tc), F32)
        sums = lax.fori_loop(0, s // CONV_ROWS, step, (zero,) * (taps + 1), unroll=2)
        rows = [jnp.sum(t, axis=0, keepdims=True) for t in sums]
        dw = jnp.concatenate(rows[:taps] + [jnp.zeros_like(rows[0])] * (SUBLANES - taps), axis=0)

        @pl.when(bb == 0)
        def _():
            dw_ref[...] = dw
            db_ref[...] = rows[taps]

        @pl.when(bb > 0)
        def _():
            dw_ref[...] += dw
            db_ref[...] += rows[taps]

    first = 0 if gated else col0 // tc
    out_shape = [_sds((bl, s, chans * (2 if gated else 1)), dx_dtype), _sds((SUBLANES, chans), F32), _sds((1, chans), F32)]
    return _pcall(
        body, name, out_shape, grid=(chans // tc, bl),
        in_specs=[
            pl.BlockSpec((1, s, xw), lambda j, bb: (bb, 0, first + j)), pl.BlockSpec((SUBLANES, tc), lambda j, bb: (0, j)),
            pl.BlockSpec((1, tc), lambda j, bb: (0, j)), pl.BlockSpec((1, s, tc), lambda j, bb: (bb, 0, j)),
        ],
        out_specs=[pl.BlockSpec((1, s, xw), lambda j, bb: (bb, 0, j)), pl.BlockSpec((SUBLANES, tc), lambda j, bb: (0, j)), pl.BlockSpec((1, tc), lambda j, bb: (0, j))],
        sem=("parallel", "arbitrary"),
    )(src, w8, b, dy)


def _gate_value_blocks(name, w, f, inverse=False):
    nb = f // CONV_LANES
    src = (lambda j: (0, 2 * (j % nb) + j // nb)) if inverse else (lambda j: (0, (j % 2) * nb + j // 2))

    def body(x_ref, o_ref):
        o_ref[...] = x_ref[...]

    return _pcall(
        body, name, _sds(w.shape, w.dtype), grid=(2 * nb,), in_specs=[pl.BlockSpec((w.shape[0], CONV_LANES), src)],
        out_specs=pl.BlockSpec((w.shape[0], CONV_LANES), lambda j: (0, j)), sem=("parallel",),
    )(w)


def _ssd_chunk(g_idx, states, xs, bm, cm, dtr, z, dtb, alog, dsk, ng):
    gw = xs.shape[1]
    hpg = gw // HEAD_DIM
    dt = jax.nn.softplus(dtr + dtb)
    da = dt * (-jnp.exp(alog))
    causal = _causal(CHUNK)
    acs = _sel_left(causal.astype(F32), da)
    head_of_col = g_idx * hpg + lax.broadcasted_iota(jnp.int32, (LANES, gw), 1) // HEAD_DIM
    expand = (lax.broadcasted_iota(jnp.int32, (LANES, gw), 0) == head_of_col).astype(F32)
    dt_e, acs_e = _sel_right(dt, expand), _sel_right(acs, expand)
    last = lax.broadcasted_iota(jnp.int32, (CHUNK, gw), 0) == CHUNK - 1
    alast_e = jnp.sum(jnp.where(last, acs_e, 0.0), axis=0, keepdims=True)
    xc = xs * dt_e
    xc_st = xc * jnp.exp(alast_e - acs_e)
    decay_out, chunk_decay = jnp.exp(acs_e), jnp.exp(alast_e)
    cb = _dot_nt(cm, bm)
    acs_t = acs.T
    lane = lax.broadcasted_iota(jnp.int32, (1, LANES), 1)
    sub = lax.broadcasted_iota(jnp.int32, (LANES, 1), 0)
    ys, new_states = [], []
    for p in range(gw // LANES):
        sl = slice(p * LANES, (p + 1) * LANES)
        xcp = xc[:, sl]
        y = _dot_nn(cm, states[p]) * decay_out[:, sl]
        for q in range(2):
            head = g_idx * hpg + 2 * p + q
            col = jnp.sum(acs * (lane == head).astype(F32), axis=1, keepdims=True)
            row = jnp.sum(acs_t * (sub == head).astype(F32), axis=0, keepdims=True)
            decay = jnp.where(causal, jnp.exp(jnp.where(causal, col - row, 0.0)), 0.0)
            half = ((lane // HEAD_DIM) == q).astype(F32)
            y = y + _dot_nn(cb * decay, xcp * half)
        ys.append(y)
        new_states.append(states[p] * chunk_decay[:, sl] + _dot_tn(bm, xc_st[:, sl]))
    y = jnp.concatenate(ys, axis=1) + dsk * xs
    gated = y * (z * _sigmoid(z))
    return tuple(new_states), _rms(gated, ng)


def _ssd_specs(d, gw, dt_col, rev, nc):
    ci = (lambda i: nc - 1 - i) if rev else (lambda i: i)
    return [
        pl.BlockSpec((1, CHUNK, gw), lambda g, b, i: (b, ci(i), g)),
        pl.BlockSpec((1, CHUNK, STATE), lambda g, b, i: (b, ci(i), d // STATE + g)),
        pl.BlockSpec((1, CHUNK, STATE), lambda g, b, i: (b, ci(i), d // STATE + SSD_GROUPS + g)),
        pl.BlockSpec((1, CHUNK, LANES), lambda g, b, i: (b, ci(i), dt_col // LANES)),
        pl.BlockSpec((1, CHUNK, gw), lambda g, b, i: (b, ci(i), g)),
        pl.BlockSpec((1, LANES), lambda g, b, i: (0, 0)), pl.BlockSpec((1, LANES), lambda g, b, i: (0, 0)),
        pl.BlockSpec((1, gw), lambda g, b, i: (0, g)), pl.BlockSpec((1, gw), lambda g, b, i: (0, g)),
    ]


def _ssd_fwd(name, xbc, proj, dt_col, dtb, alog, dsk, ng):
    bl, s, cd = xbc.shape
    d = dsk.shape[1]
    gw, nc = d // SSD_GROUPS, s // CHUNK
    npair = gw // LANES

    def body(xs_ref, bm_ref, cm_ref, dt_ref, z_ref, dtb_ref, alog_ref, dsk_ref, ng_ref, y_ref, hp_ref, st_ref):
        g, i = pl.program_id(0), pl.program_id(2)

        @pl.when(i == 0)
        def _():
            st_ref[...] = jnp.zeros_like(st_ref)

        states = tuple(st_ref[p] for p in range(npair))
        hp_ref[0, 0, 0] = st_ref[...]
        new_states, yn = _ssd_chunk(g, states, xs_ref[0], bm_ref[0], cm_ref[0], dt_ref[0], z_ref[0], dtb_ref[...], alog_ref[...], dsk_ref[...], ng_ref[...])
        for p in range(npair):
            st_ref[p] = new_states[p]
        y_ref[0] = yn.astype(y_ref.dtype)

    out_shape = [_sds((bl, s, d), BF16), _sds((SSD_GROUPS, bl, nc, npair, STATE, LANES), F32)]
    out_specs = [
        pl.BlockSpec((1, CHUNK, gw), lambda g, b, i: (b, i, g)),
        pl.BlockSpec((1, 1, 1, npair, STATE, LANES), lambda g, b, i: (g, b, i, 0, 0, 0)),
    ]
    return _pcall(
        body, name, out_shape, grid=(SSD_GROUPS, bl, nc), in_specs=_ssd_specs(d, gw, dt_col, False, nc), out_specs=out_specs,
        scratch=[pltpu.VMEM((npair, STATE, LANES), F32)], sem=("arbitrary", "arbitrary", "arbitrary"),
    )(xbc, xbc, xbc, proj, proj, dtb, alog, dsk, ng)


def _ssd_bwd(name, xbc, proj, dt_col, dtb, alog, dsk, ng, hprev, dy):
    bl, s, cd = xbc.shape
    d = dsk.shape[1]
    gw, nc = d // SSD_GROUPS, s // CHUNK
    npair = gw // LANES

    def body(xs_ref, bm_ref, cm_ref, dt_ref, z_ref, dtb_ref, alog_ref, dsk_ref, ng_ref, hp_ref, dy_ref,
             dxs_ref, dbm_ref, dcm_ref, ddt_ref, dz_ref, ddtb_ref, dalog_ref, ddsk_ref, dng_ref, dst_ref):
        g, b, i = pl.program_id(0), pl.program_id(1), pl.program_id(2)

        @pl.when(i == 0)
        def _():
            dst_ref[...] = jnp.zeros_like(dst_ref)

        states = tuple(hp_ref[0, 0, 0, p] for p in range(npair))
        step = functools.partial(_ssd_chunk, g)
        _, vjp = jax.vjp(step, states, xs_ref[0], bm_ref[0], cm_ref[0], dt_ref[0], z_ref[0], dtb_ref[...], alog_ref[...], dsk_ref[...], ng_ref[...])
        d_states, dxs, dbm, dcm, ddt, dz, ddtb, dalog, ddsk, dng = vjp((tuple(dst_ref[p] for p in range(npair)), dy_ref[0].astype(F32)))
        for p in range(npair):
            dst_ref[p] = d_states[p]
        dxs_ref[0], dbm_ref[0], dcm_ref[0] = dxs, dbm, dcm
        dz_ref[0] = dz.astype(dz_ref.dtype)
        ddt_ref[0, 0] = ddt
        first_g = (b == 0) & (i == 0)
        first = first_g & (g == 0)
        for o_ref, val, init in ((ddtb_ref, ddtb, first), (dalog_ref, dalog, first), (ddsk_ref, ddsk, first_g), (dng_ref, dng, first_g)):
            @pl.when(init)
            def _(o_ref=o_ref, val=val):
                o_ref[...] = val

            @pl.when(jnp.logical_not(init))
            def _(o_ref=o_ref, val=val):
                o_ref[...] += val

    rc = lambda i: nc - 1 - i
    in_specs = _ssd_specs(d, gw, dt_col, True, nc) + [
        pl.BlockSpec((1, 1, 1, npair, STATE, LANES), lambda g, b, i: (g, b, rc(i), 0, 0, 0)),
        pl.BlockSpec((1, CHUNK, gw), lambda g, b, i: (b, rc(i), g)),
    ]
    out_shape = [
        _sds((bl, s, d), F32), _sds((bl, s, SSD_GROUPS * STATE), F32), _sds((bl, s, SSD_GROUPS * STATE), F32),
        _sds((SSD_GROUPS, bl, s, LANES), F32), _sds((bl, s, d), BF16),
        _sds((1, LANES), F32), _sds((1, LANES), F32), _sds((1, d), F32), _sds((1, d), F32),
    ]
    out_specs = [
        pl.BlockSpec((1, CHUNK, gw), lambda g, b, i: (b, rc(i), g)),
        pl.BlockSpec((1, CHUNK, STATE), lambda g, b, i: (b, rc(i), g)), pl.BlockSpec((1, CHUNK, STATE), lambda g, b, i: (b, rc(i), g)),
        pl.BlockSpec((1, 1, CHUNK, LANES), lambda g, b, i: (g, b, rc(i), 0)),
        pl.BlockSpec((1, CHUNK, gw), lambda g, b, i: (b, rc(i), g)),
        pl.BlockSpec((1, LANES), lambda g, b, i: (0, 0)), pl.BlockSpec((1, LANES), lambda g, b, i: (0, 0)),
        pl.BlockSpec((1, gw), lambda g, b, i: (0, g)), pl.BlockSpec((1, gw), lambda g, b, i: (0, g)),
    ]
    return _pcall(
        body, name, out_shape, grid=(SSD_GROUPS, bl, nc), in_specs=in_specs, out_specs=out_specs,
        scratch=[pltpu.VMEM((npair, STATE, LANES), F32)], sem=("arbitrary", "arbitrary", "arbitrary"),
    )(xbc, xbc, xbc, proj, proj, dtb, alog, dsk, ng, hprev, dy)


def _mixer_fwd(name, proj, d, cd, conv_w8, conv_b, dtb, alog, dsk, ng, vg, ws, bs_t, og):
    xbc = _conv_fwd(name + "_conv_fwd", proj, 3 * d, cd, conv_w8, conv_b, SSD_CONV, False, F32)
    y, hprev = _ssd_fwd(name + "_ssd_fwd", xbc, proj, 3 * d + cd, dtb, alog, dsk, ng)
    rows = [(proj, d, 1, BF16), (proj, d, 2, BF16)]
    (g_out,) = _row_fwd(name + "_gm_fwd", _gm_tile, rows, [], [vg, ws, bs_t, og], [(d, BF16)], 0, CHUNK)
    return jnp.concatenate([y, g_out], axis=-1), (xbc, hprev)


def _mixer_bwd(name, proj, d, cd, conv_w8, conv_b, dtb, alog, dsk, ng, vg, ws, bs_t, og, xbc, hprev, dycat):
    bl, s, n_proj = proj.shape
    dt_col = 3 * d + cd
    dy, dg_out = dycat[..., :d], dycat[..., d:]
    dxs, dbm, dcm, ddt2, dz, ddtb, dalog, ddsk, dng = _ssd_bwd(name + "_ssd_bwd", xbc, proj, dt_col, dtb, alog, dsk, ng, hprev, dy)
    dxbc_act = jnp.concatenate([dxs, dbm, dcm], axis=-1)
    dxbc, dw8, dcb = _conv_bwd(name + "_conv_bwd", proj, 3 * d, cd, conv_w8, conv_b, SSD_CONV, False, dxbc_act, BF16)
    rows = [(proj, d, 1, BF16), (proj, d, 2, BF16)]
    (du, dv), _, (dvg, dws, dbs_t, dog) = _row_bwd(name + "_gm_bwd", _gm_tile, rows, [], [vg, ws, bs_t, og], [(d, BF16)], 0, CHUNK, [dg_out])
    ddt = (ddt2[0] + ddt2[1]).astype(BF16)
    pad = jnp.zeros((bl, s, n_proj - dt_col - LANES), BF16)
    dproj = jnp.concatenate([dz, du, dv, dxbc, ddt, pad], axis=-1)
    return dproj, (dw8, dcb, ddtb, dalog, ddsk, dng, dvg, dws, dbs_t, dog)


def _position():
    return lax.axis_index("x"), lax.axis_index("y"), lax.axis_index("c")


def _at(ref, idx):
    return ref.at[idx] if len(idx) else ref


def _exchange(name, inputs, out_shapes, plan, inplace=False):
    if inplace:
        out_shapes = [(a.shape, a.dtype) for a in inputs]
    n_in, n_out = len(inputs), len(out_shapes)
    n_copy = len(plan(0, 0, 0))

    def body(*refs):
        in_refs, out_refs, token = refs[:n_in], refs[n_in:n_in + n_out], refs[n_in + n_out]
        send_sems, recv_sems = refs[n_in + n_out + 1:]
        token[...] = jnp.zeros_like(token)
        x, y, c = _position()
        copies = plan(x, y, c)

        def copy(k, src, dst, peer):
            return pltpu.make_async_remote_copy(src_ref=src, dst_ref=dst, send_sem=send_sems.at[k], recv_sem=recv_sems.at[k], device_id=peer, device_id_type=MESH)

        src_refs = out_refs if inplace else in_refs
        sends = [copy(k, _at(src_refs[sa], si), _at(out_refs[da], di), peer) for k, (sa, si, da, di, peer, _) in enumerate(copies)]
        for cp in sends:
            cp.start()
        for k, (sa, si, da, _, peer, li) in enumerate(copies):
            copy(k, _at(src_refs[sa], si), _at(out_refs[da], li), peer).wait_recv()
        for cp in sends:
            cp.wait_send()

    any_spec = pl.BlockSpec(memory_space=pl.ANY)
    outs = pl.pallas_call(
        body, name=name, out_shape=[_sds(s, dt) for s, dt in out_shapes] + [_sds((SUBLANES, LANES), F32)], in_specs=[any_spec] * n_in,
        out_specs=[any_spec] * n_out + [pl.BlockSpec(memory_space=pltpu.VMEM)],
        scratch_shapes=[pltpu.SemaphoreType.DMA((n_copy,)), pltpu.SemaphoreType.DMA((n_copy,))],
        input_output_aliases={i: i for i in range(n_in)} if inplace else {},
    )(*inputs)
    return list(outs[:n_out]), outs[n_out]


def _exchange_start(name, inputs, out_shapes, plan):
    n_in, n_out = len(inputs), len(out_shapes)
    n_copy = len(plan(0, 0, 0))

    def body(*refs):
        in_refs, land_refs = refs[:n_in], refs[n_in:n_in + n_out]
        send_sems, recv_sems = refs[n_in + n_out:n_in + n_out + 2]
        token = refs[-1]
        x, y, c = _position()
        for k, (sa, si, da, di, peer, _) in enumerate(plan(x, y, c)):
            pltpu.make_async_remote_copy(
                src_ref=_at(in_refs[sa], si), dst_ref=_at(land_refs[da], di), send_sem=send_sems.at[k], recv_sem=recv_sems.at[k],
                device_id=peer, device_id_type=MESH).start()
        token[...] = jnp.zeros_like(token)

    hbm, sem = pl.BlockSpec(memory_space=pltpu.HBM), pl.BlockSpec(memory_space=pltpu.SEMAPHORE)
    lands = [lax.empty(s, dt) for s, dt in out_shapes]
    args = [pltpu.with_memory_space_constraint(a, pltpu.HBM) for a in list(inputs) + lands]
    outs = pl.pallas_call(
        body, name=name,
        out_shape=(pltpu.SemaphoreType.DMA((n_copy,)), pltpu.SemaphoreType.DMA((n_copy,)), *[pltpu.HBM(a.shape, a.dtype) for a in args], _sds((SUBLANES, LANES), F32)),
        in_specs=[hbm] * (n_in + n_out), out_specs=(sem, sem, *[hbm] * (n_in + n_out), pl.BlockSpec(memory_space=pltpu.VMEM)),
        input_output_aliases={i: 2 + i for i in range(n_in + n_out)},
        compiler_params=pltpu.CompilerParams(has_side_effects=pltpu.SideEffectType.DATAFLOW_SIDE_EFFECTING),
    )(*args)
    return dict(name=name, plan=plan, sems=outs[:2], ins=list(outs[2:2 + n_in]), lands=list(outs[2 + n_in:2 + n_in + n_out]), token=outs[-1])


def _exchange_wait(started, after):
    plan, n_in, n_out = started["plan"], len(started["ins"]), len(started["lands"])

    def body(*refs):
        in_refs, land_refs = refs[:n_in], refs[n_in:n_in + n_out]
        send_sems, recv_sems = refs[n_in + n_out:n_in + n_out + 2]
        token = refs[-1]
        x, y, c = _position()
        for k, (sa, si, da, _, peer, li) in enumerate(plan(x, y, c)):
            cp = pltpu.make_async_remote_copy(
                src_ref=_at(in_refs[sa], si), dst_ref=_at(land_refs[da], li), send_sem=send_sems.at[k], recv_sem=recv_sems.at[k],
                device_id=peer, device_id_type=MESH)
            cp.wait_send()
            cp.wait_recv()
        token[...] = jnp.zeros_like(token)

    hbm, sem = pl.BlockSpec(memory_space=pltpu.HBM), pl.BlockSpec(memory_space=pltpu.SEMAPHORE)
    bufs = started["ins"] + started["lands"]
    outs = pl.pallas_call(
        body, name=started["name"] + "_wait", out_shape=(*[pltpu.HBM(a.shape, a.dtype) for a in bufs], _sds((SUBLANES, LANES), F32)),
        in_specs=[hbm] * len(bufs) + [sem, sem, pl.BlockSpec(memory_space=pl.ANY)], out_specs=(*[hbm] * len(bufs), pl.BlockSpec(memory_space=pltpu.VMEM)),
        input_output_aliases={i: i for i in range(len(bufs))},
        compiler_params=pltpu.CompilerParams(has_side_effects=pltpu.SideEffectType.DATAFLOW_SIDE_EFFECTING),
    )(*bufs, *started["sems"], after)
    return list(outs[:n_in]), list(outs[n_in:n_in + n_out]), outs[-1]


def _after(value, token):
    return value + token[0, 0].astype(value.dtype)


def _chip_peers(x, y, c):
    return [(1 - x, y, c), (x, 1 - y, c), (1 - x, 1 - y, c)]


def _chip_of(p):
    return 2 * p[0] + p[1]


def _set_slot(slots, me, blk):
    return lax.dynamic_update_slice(slots, blk[None], (me,) + (0,) * blk.ndim)


def _by_core(c, mine, other, axis):
    return jnp.where(c == 0, jnp.stack([mine, other], axis), jnp.stack([other, mine], axis))


def _plan_gather_chips(n):
    def plan(x, y, c):
        return [(a, (), a, (2 * x + y,), p, (_chip_of(p),)) for a in range(n) for p in _chip_peers(x, y, c)]

    return plan


def _gather_chips(name, blocks):
    recv, token = _exchange(name, blocks, [((4,) + b.shape, b.dtype) for b in blocks], _plan_gather_chips(len(blocks)))
    x, y, _ = _position()
    return [_set_slot(r, 2 * x + y, b) for r, b in zip(recv, blocks)], token


def _gather_pass_cores(name, blocks, from_chips):
    n = len(blocks)

    def plan_cores(x, y, c):
        me, sib = 2 * x + y, (x, y, 1 - c)
        own = [(a, (), a, (me,), sib, (me,)) for a in range(n)]
        passed = [(n + a, (_chip_of(p),), a, (_chip_of(p),), sib, (_chip_of(p),)) for a in range(n) for p in _chip_peers(x, y, c)]
        return own + passed

    from_core, token = _exchange(name + "_cores", list(blocks) + list(from_chips), [((4,) + b.shape, b.dtype) for b in blocks], plan_cores)
    x, y, c = _position()
    return [_by_core(c, _set_slot(r1, 2 * x + y, b), r2, 1) for b, r1, r2 in zip(blocks, from_chips, from_core)], token


def _gather_two_level(name, blocks):
    from_chips, _ = _exchange(name + "_chips", blocks, [((4,) + b.shape, b.dtype) for b in blocks], _plan_gather_chips(len(blocks)))
    return _gather_pass_cores(name, blocks, from_chips)


def _pair_add(name, g42, r4):
    _, _, rh, cols = g42.shape
    tr = _divisors(rh, 512, SUBLANES * 2)[-1]

    def body(c_ref, a_ref, b_ref, o_ref):
        o_ref[0] = (a_ref[0, 0].astype(F32) + b_ref[0].astype(F32)).astype(o_ref.dtype)

    cidx = lax.axis_index("c").astype(jnp.int32).reshape(1)
    return _pcall(
        body, name, _sds(r4.shape, BF16), grid=(4, rh // tr),
        in_specs=[pl.BlockSpec((1, 1, tr, cols), lambda s, i, c_ref: (s, c_ref[0], i, 0)), pl.BlockSpec((1, tr, cols), lambda s, i, c_ref: (s, i, 0))],
        out_specs=pl.BlockSpec((1, tr, cols), lambda s, i, c_ref: (s, i, 0)), sem=("parallel", "parallel"), prefetch=1,
    )(cidx, g42, r4)


def _slot_sum(name, parts):
    n, r, cols = parts.shape
    cap = max(2 * SUBLANES, (4 * 1024 * 1024) // (n * cols * parts.dtype.itemsize))
    tr = _divisors(r, cap, 2 * SUBLANES)[-1]

    def body(p_ref, o_ref):
        acc = p_ref[0].astype(F32)
        for k in range(1, n):
            acc = acc + p_ref[k].astype(F32)
        o_ref[...] = acc

    return _pcall(
        body, name, _sds((r, cols), F32), grid=(r // tr,), in_specs=[pl.BlockSpec((n, tr, cols), lambda i: (0, i, 0))],
        out_specs=pl.BlockSpec((tr, cols), lambda i: (i, 0)), sem=("parallel",),
    )(parts)


def _slot_sums(name, parts):
    k = len(parts)

    def body(*refs):
        for p_ref, o_ref in zip(refs[:k], refs[k:]):
            acc = p_ref[0].astype(F32)
            for j in range(1, p_ref.shape[0]):
                acc = acc + p_ref[j].astype(F32)
            o_ref[...] = acc

    return list(_pcall(body, name, [_sds(p.shape[1:], F32) for p in parts])(*parts))


def _gather_weights_start(name, shards, token):
    c = lax.axis_index("c")
    halves = [lax.dynamic_slice_in_dim(w, c * (w.shape[0] // 2), w.shape[0] // 2, 0).astype(BF16) for w in shards]
    if token is not None:
        halves[0] = _after(halves[0], token)
    n = len(halves)

    def plan(x, y, c):
        return [(a, (), a, (2 * x + y, c), p, (_chip_of(p), c)) for a in range(n) for p in _chip_peers(x, y, c)]

    return _exchange_start(name + "_ag_chips", halves, [((4, 2) + h.shape, h.dtype) for h in halves], plan)


def _gather_weights_finish(name, started, after):
    halves, slots, token = _exchange_wait(started, after)
    n = len(halves)
    x, y, c = _position()
    slots = [lax.dynamic_update_slice(s, h[None, None], (2 * x + y, c, 0, 0)) for s, h in zip(slots, halves)]

    def plan_cores(x, y, c):
        return [(a, (s, c), a, (s, c), (x, y, 1 - c), (s, 1 - c)) for a in range(n) for s in range(4)]

    full, _ = _exchange(name + "_ag_cores", slots, None, plan_cores, inplace=True)
    return [f.reshape((4, 2 * h.shape[0], h.shape[1])) for f, h in zip(full, halves)], token


def _reduce_weights_start(name, grads, token):
    n = len(grads)
    g42 = [g.reshape(4, 2, g.shape[1] // 2, g.shape[2]) for g in grads]
    if token is not None:
        g42[0] = _after(g42[0], token)

    def plan_swap(x, y, c):
        return [(a, (s, 1 - c), a, (s,), (x, y, 1 - c), (s,)) for a in range(n) for s in range(4)]

    def plan_chips(x, y, c):
        return [(a, (_chip_of(p),), a, (2 * x + y,), p, (_chip_of(p),)) for a in range(n) for p in _chip_peers(x, y, c)]

    other, _ = _exchange(name + "_rs_cores", g42, [((4,) + g.shape[2:], g.dtype) for g in g42], plan_swap)
    pair = [_pair_add(f"{name}_rs_pair{a}", g, o) for a, (g, o) in enumerate(zip(g42, other))]
    return _exchange_start(name + "_rs_chips", pair, [(p.shape, p.dtype) for p in pair], plan_chips)


def _reduce_weights_finish(name, started, after):
    pair, recv, token = _exchange_wait(started, after)
    n = len(pair)
    x, y, c = _position()
    me = 2 * x + y

    def plan_share(x, y, c):
        return [(a, (c,), a, (c,), (x, y, 1 - c), (1 - c,)) for a in range(n)]

    parts = [lax.dynamic_update_slice(r, lax.dynamic_slice_in_dim(p, me, 1, 0), (me, 0, 0)) for r, p in zip(recv, pair)]
    mine = [_slot_sum(f"{name}_rs_sum{a}", p) for a, p in enumerate(parts)]
    both = [lax.dynamic_update_slice(lax.empty((2,) + m.shape, m.dtype), m[None], (c, 0, 0)) for m in mine]
    both, _ = _exchange(name + "_rs_share", both, None, plan_share, inplace=True)
    return [b.reshape(2 * b.shape[1], b.shape[2]) for b in both], token


def _allreduce_small(name, grads):
    both, token = _gather_two_level(name + "_ag", [g.astype(BF16) if g.size >= SMALL_BF16_SIZE else g for g in grads])
    return _slot_sums(name + "_sum", [g.reshape((8,) + g.shape[2:]) for g in both]), token


def _ada_fwd_call(name, c_all, w, b_shard):
    nl, d, ns = w.shape
    nb = c_all.shape[0]

    def body(c_ref, w_ref, b_ref, o_ref):
        cv = c_ref[...]
        o_ref[0] = _dg(cv * _sigmoid(cv), w_ref[0], ((1,), (0,))) + b_ref[0]

    return _pcall(
        body, name, _sds((nl, nb, ns), F32), grid=(nl,),
        in_specs=[pl.BlockSpec((nb, d), lambda l: (0, 0)), pl.BlockSpec((1, d, ns), lambda l: (l, 0, 0)), pl.BlockSpec((1, 1, ns), lambda l: (l, 0, 0))],
        out_specs=pl.BlockSpec((1, nb, ns), lambda l: (l, 0, 0)), sem=("parallel",),
    )(c_all, w, b_shard)


def _ada_bwd_call(name, c_all, dm_shard, dm_all):
    nl, nb, ns = dm_shard.shape
    d = c_all.shape[1]
    nm = dm_all.shape[2]

    def body(c_ref, ds_ref, da_ref, dw_ref, db_ref):
        cv = c_ref[...]
        dw_ref[0] = _dg(cv * _sigmoid(cv), ds_ref[0], ((0,), (0,)))
        db_ref[0] = jnp.sum(da_ref[0], axis=0, keepdims=True)

    return _pcall(
        body, name, [_sds((nl, d, ns), F32), _sds((nl, 1, nm), F32)], grid=(nl,),
        in_specs=[pl.BlockSpec((nb, d), lambda l: (0, 0)), pl.BlockSpec((1, nb, ns), lambda l: (l, 0, 0)), pl.BlockSpec((1, nb, nm), lambda l: (l, 0, 0))],
        out_specs=[pl.BlockSpec((1, d, ns), lambda l: (l, 0, 0)), pl.BlockSpec((1, 1, nm), lambda l: (l, 0, 0))], sem=("parallel",),
    )(c_all, dm_shard, dm_all)


def _ada_fwd(name, bl, c_all, w, b):
    nl, d, ns = w.shape
    chip = 2 * lax.axis_index("x") + lax.axis_index("y")
    b_shard = lax.dynamic_slice(b, (0, chip * ns), (nl, ns)).reshape(nl, 1, ns)
    shard = _ada_fwd_call(name + "_fwd", c_all, w, b_shard)
    (allc,), token = _gather_chips(name + "_ag", [shard])
    mods = jnp.transpose(allc, (1, 2, 0, 3)).reshape(nl, c_all.shape[0], 4 * ns)
    return lax.dynamic_slice(mods, (0, (2 * chip + lax.axis_index("c")) * bl, 0), (nl, bl, 4 * ns)), token


def _ada_bwd(name, bl, c_all, ns, dm):
    nl = dm.shape[0]
    chip = 2 * lax.axis_index("x") + lax.axis_index("y")
    (dm_all,), token = _gather_two_level(name + "_bwd_ag", [dm])
    dm_all = jnp.transpose(dm_all.reshape((8,) + dm.shape), (1, 0, 2, 3)).reshape(nl, 8 * bl, 4 * ns)
    dm_shard = lax.dynamic_slice(dm_all, (0, 0, chip * ns), (nl, 8 * bl, ns))
    dw, db = _ada_bwd_call(name + "_bwd", c_all, dm_shard, dm_all)
    return dw, db.reshape(nl, 4 * ns), token


def _adamw(name, w, g, m, v):
    shape = w.shape
    cols = shape[-1]
    w2, g2, m2, v2 = (t.reshape(-1, cols) for t in (w, g, m, v))
    rows = w2.shape[0]
    cap = max(SUBLANES, (512 * 1024) // max(cols, 1) // SUBLANES * SUBLANES)
    tr = _divisors(rows, cap, SUBLANES)[-1]

    def body(w_ref, g_ref, m_ref, v_ref, d_ref, mo_ref, vo_ref):
        gv = g_ref[...]
        mn = ADAM_B1 * m_ref[...] + (1.0 - ADAM_B1) * gv
        vn = ADAM_B2 * v_ref[...] + (1.0 - ADAM_B2) * (gv * gv)
        m_hat = mn / (1.0 - ADAM_B1 ** ADAM_STEP)
        v_hat = vn / (1.0 - ADAM_B2 ** ADAM_STEP)
        d_ref[...] = -ADAM_LR * (m_hat / (jnp.sqrt(v_hat) + ADAM_EPS) + ADAM_WD * w_ref[...])
        mo_ref[...] = mn
        vo_ref[...] = vn

    spec = pl.BlockSpec((tr, cols), lambda i: (i, 0))
    outs = _pcall(body, name, [_sds((rows, cols), F32)] * 3, grid=(rows // tr,), in_specs=[spec] * 4, out_specs=[spec] * 3, sem=("parallel",))(w2, g2, m2, v2)
    return tuple(o.reshape(shape) for o in outs)


def _adamw_small(name, ws, gs, ms, vs):
    n = len(ws)

    def body(*refs):
        for k in range(n):
            w_ref, g_ref, m_ref, v_ref = (refs[j * n + k] for j in range(4))
            d_ref, mo_ref, vo_ref = (refs[(4 + j) * n + k] for j in range(3))
            gv = g_ref[...]
            mn = ADAM_B1 * m_ref[...] + (1.0 - ADAM_B1) * gv
            vn = ADAM_B2 * v_ref[...] + (1.0 - ADAM_B2) * (gv * gv)
            m_hat = mn / (1.0 - ADAM_B1 ** ADAM_STEP)
            v_hat = vn / (1.0 - ADAM_B2 ** ADAM_STEP)
            d_ref[...] = -ADAM_LR * (m_hat / (jnp.sqrt(v_hat) + ADAM_EPS) + ADAM_WD * w_ref[...])
            mo_ref[...] = mn
            vo_ref[...] = vn

    outs = _pcall(body, name, [_sds(w.shape, F32) for w in ws] * 3)(*ws, *gs, *ms, *vs)
    return outs[:n], outs[n:2 * n], outs[2 * n:]


def _pad_rows(w, rows):
    return jnp.pad(w, ((0, rows - w.shape[0]), (0, 0)))


def _pad_lanes(v):
    return jnp.pad(v, (0, LANES - v.shape[0])).reshape(1, LANES)


BIG = ("w_in", "w_out", "ff_up", "ff_down")
BIG_AXIS = {"w_in": 1, "w_out": 0, "ff_up": 1, "ff_down": 0}
CONVW = ("ssd_conv_w", "ff_conv_w")
SMALL = ("norm1_g", "norm2_g", "ssd_conv_b", "ssd_dt_bias", "ssd_a_log", "ssd_d", "ssd_norm_g", "gm_vnorm_g", "gm_ws", "gm_bs", "gm_out_g", "ff_conv_b")
WEIGHTS = ("ada_w", "ada_b", "norm1_g", "norm2_g", "w_in", "ssd_conv_w", "ssd_conv_b", "ssd_dt_bias", "ssd_a_log", "ssd_d", "ssd_norm_g", "gm_vnorm_g", "gm_ws", "gm_bs", "gm_out_g", "w_out", "ff_up", "ff_conv_w", "ff_conv_b", "ff_down", "final_g")


def kernel(x, c, ada_w, ada_b, norm1_g, norm2_g, w_in, ssd_conv_w, ssd_conv_b, ssd_dt_bias, ssd_a_log, ssd_d, ssd_norm_g, gm_vnorm_g, gm_ws, gm_bs, gm_out_g, w_out, ff_up, ff_conv_w, ff_conv_b, ff_down, final_g, loss_target, m_ada_w, m_ada_b, m_norm1_g, m_norm2_g, m_w_in, m_ssd_conv_w, m_ssd_conv_b, m_ssd_dt_bias, m_ssd_a_log, m_ssd_d, m_ssd_norm_g, m_gm_vnorm_g, m_gm_ws, m_gm_bs, m_gm_out_g, m_w_out, m_ff_up, m_ff_conv_w, m_ff_conv_b, m_ff_down, m_final_g, v_ada_w, v_ada_b, v_norm1_g, v_norm2_g, v_w_in, v_ssd_conv_w, v_ssd_conv_b, v_ssd_dt_bias, v_ssd_a_log, v_ssd_d, v_ssd_norm_g, v_gm_vnorm_g, v_gm_ws, v_gm_bs, v_gm_out_g, v_w_out, v_ff_up, v_ff_conv_w, v_ff_conv_b, v_ff_down, v_final_g):
    given = dict(locals())
    weights = {n: given[n] for n in WEIGHTS}
    bl, s, d = x.shape
    nl = ada_w.shape[0]
    heads = d // HEAD_DIM
    cd = d + 2 * SSD_GROUPS * STATE
    f = ff_down.shape[1] * 4
    n_in = d + cd + heads + 2 * d
    n_proj = _round_up(3 * d + cd + LANES, 2 * LANES)
    tm = _divisors(s, 512)[-1]

    gathering = {}

    def start_gathers(l, names, token):
        for n in names:
            gathering[l, n] = _gather_weights_start(f"wg{l}_{n}", [weights[n][l]], token)
            token = gathering[l, n]["token"]
        return token

    issued = start_gathers(0, BIG[:1], None)
    pre, _ = _gather_two_level("pre_ag", [_after(c, issued)] + [weights[n] for n in CONVW])
    c_all = pre[0].reshape(8 * bl, d)
    conv_full = {n: jnp.concatenate([p[k, 0] for k in range(4)], axis=-1) for n, p in zip(CONVW, pre[1:])}
    mods_all, ada_token = _ada_fwd("ada", bl, c_all, ada_w, ada_b)
    mods = mods_all.reshape(nl, bl, N_MOD, 1, d)

    def landed(l, n, after):
        (full,), token = _gather_weights_finish(f"wg{l}_{n}", gathering.pop((l, n)), after)
        if n == "w_in":
            return full, token
        return (full.reshape(-1, full.shape[2]) if BIG_AXIS[n] == 0 else jnp.concatenate([full[k] for k in range(4)], axis=1)), token

    issued = start_gathers(0, BIG[1:], ada_token)
    r2 = lambda v: v.reshape(1, -1)
    n_gm = d // GM_HEAD
    fg = r2(final_g)

    def layer_params(l):
        return dict(
            norm1=r2(norm1_g[l]), norm2=r2(norm2_g[l]), conv_w8=_pad_rows(conv_full["ssd_conv_w"][l], SUBLANES), conv_b=r2(ssd_conv_b[l]),
            dtb=_pad_lanes(ssd_dt_bias[l]), alog=_pad_lanes(ssd_a_log[l]), dsk=r2(jnp.repeat(ssd_d[l], HEAD_DIM)), ng=r2(ssd_norm_g[l]),
            vg=r2(gm_vnorm_g[l]), ws=gm_ws[l], bs_t=jnp.pad(gm_bs[l].T, ((0, 0), (0, LANES - n_gm))), og=r2(gm_out_g[l]),
            ff_w8=_pad_rows(conv_full["ff_conv_w"][l], SUBLANES), ff_b=r2(ff_conv_b[l]))

    def mixer_args(p):
        return (p["conv_w8"], p["conv_b"], p["dtb"], p["alog"], p["dsk"], p["ng"], p["vg"], p["ws"], p["bs_t"], p["og"])

    def padded_w_in(full):
        ns = full.shape[2]

        def cols(a, b):
            return [full[k][:, max(a, k * ns) - k * ns:min(b, (k + 1) * ns) - k * ns] for k in range(4) if max(a, k * ns) < min(b, (k + 1) * ns)]

        parts = cols(0, d) + cols(d + cd + heads, n_in) + cols(d, d + cd) + cols(d + cd, d + cd + heads)
        return jnp.concatenate(parts + [jnp.zeros((d, n_proj - (3 * d + cd + heads)), BF16)], axis=1)

    saved, xcur, pending = [], x, None
    for l in range(nl):
        p, w = layer_params(l), {}
        wi, token = landed(l, "w_in", mods_all if l == 0 else pending[0])
        w["w_in"] = padded_w_in(wi)
        sh1, sc1, g1, sh2, sc2, g2 = (mods[l, :, k] for k in range(N_MOD))
        if l + 1 < nl:
            issued = start_gathers(l + 1, BIG, token + issued)
        sc1 = _after(sc1, issued)
        if pending is None:
            (h,) = _row_fwd(f"nm{l}_fwd", _nm_tile, [(xcur, d, 0, None)], [sc1, sh1], [p["norm1"]], [(d, BF16)], 0, tm)
            x1 = xcur
        else:
            rows = [(xcur, d, 0, None), (pending[0], d, 0, None)]
            x1, h = _row_fwd(f"rnm{l}a_fwd", _rnm_tile, rows, [pending[1], sc1, sh1], [p["norm1"]], [(d, F32), (d, BF16)], 0, tm)
        proj = _mm(f"win{l}_fwd", h, w["w_in"], F32)
        ycat, (xbc, hprev) = _mixer_fwd(f"mix{l}", proj, d, cd, *mixer_args(p))
        w["w_out"], _ = landed(l, "w_out", ycat)
        mix = _mm(f"wout{l}_fwd", ycat, w["w_out"], F32)
        x2, h2 = _row_fwd(f"rnm{l}b_fwd", _rnm_tile, [(x1, d, 0, None), (mix, d, 0, None)], [g1, sc2, sh2], [p["norm2"]], [(d, F32), (d, BF16)], 0, tm)
        w_up, _ = landed(l, "ff_up", h2)
        w["ff_up"] = _gate_value_blocks(f"ffup{l}_blocks", w_up, f)
        up = _mm(f"ffup{l}_fwd", h2, w["ff_up"], F32)
        act = _conv_fwd(f"ffact{l}_fwd", up, 0, f, p["ff_w8"], p["ff_b"], FF_CONV, True, BF16)
        w["ff_down"], _ = landed(l, "ff_down", act)
        down = _mm(f"ffdown{l}_fwd", act, w["ff_down"], F32)
        saved.append(dict(p=p, w=w, x_in=xcur, pending=pending, h=h, proj=proj, xbc=xbc, hprev=hprev, ycat=ycat, x1=x1, mix=mix, h2=h2, up=up, act=act))
        xcur, pending = x2, (down, g2)
    rows = [(xcur, d, 0, F32), (pending[0], d, 0, BF16), (loss_target, d, 0, None)]
    (loss_local,) = _row_fwd("final_fwd", _final_tile, rows, [pending[1]], [fg], [], 1, tm)
    loss = lax.psum(loss_local[0, 0], AXES)

    (dx2, ddown), (dg2,), (dfg,) = _row_bwd("final_bwd", _final_tile, rows, [pending[1]], [fg], [], 1, tm, [jnp.ones((1, 1), F32)])
    dmods, small_grads = [None] * nl, [None] * nl
    reducing = {}

    def start_reduce(l, n, g):
        width = g.shape[1] // 4
        by_chip = g.reshape(4, g.shape[0] // 4, g.shape[1]) if BIG_AXIS[n] == 0 else jnp.stack([g[:, k * width:(k + 1) * width] for k in range(4)])
        reducing[l, n] = _reduce_weights_start(f"wg{l}_{n}", [by_chip], None)
        return reducing[l, n]["token"]

    for l in reversed(range(nl)):
        sv = saved[l]
        p, w = sv["p"], sv["w"]
        sh1, sc1, g1, sh2, sc2, g2 = (mods[l, :, k] for k in range(N_MOD))
        dact, dw_down = _mm_bwd(f"ffdown{l}", sv["act"], w["ff_down"], ddown)
        ff_b = _after(p["ff_b"], start_reduce(l, "ff_down", dw_down))
        dup, dff_w8, dff_b = _conv_bwd(f"ffact{l}_bwd", sv["up"], 0, f, p["ff_w8"], ff_b, FF_CONV, True, dact, BF16)
        dh2, dw_up = _mm_bwd(f"ffup{l}", sv["h2"], w["ff_up"], dup)
        sc2 = _after(sc2, start_reduce(l, "ff_up", _gate_value_blocks(f"ffup{l}_columns", dw_up, f, inverse=True)))
        rows = [(sv["x1"], d, 0, F32), (sv["mix"], d, 0, BF16)]
        (dx1, dmix), (dg1, dsc2, dsh2), (dn2,) = _row_bwd(f"rnm{l}b_bwd", _rnm_tile, rows, [g1, sc2, sh2], [p["norm2"]], [(d, F32), (d, BF16)], 0, tm, [dx2, dh2])
        dycat, dw_out = _mm_bwd(f"wout{l}", sv["ycat"], w["w_out"], dmix)
        p_tied = dict(p, dtb=_after(p["dtb"], start_reduce(l, "w_out", dw_out)))
        dproj, (dw8, dcb, ddtb, dalog, ddsk, dng, dvg, dws, dbs_t, dog) = _mixer_bwd(f"mix{l}", sv["proj"], d, cd, *mixer_args(p_tied), sv["xbc"], sv["hprev"], dycat)
        dh, dw_in_p = _mm_bwd(f"win{l}", sv["h"], w["w_in"], dproj)
        dw_in = jnp.concatenate([dw_in_p[:, :d], dw_in_p[:, 3 * d:3 * d + cd], dw_in_p[:, 3 * d + cd:3 * d + cd + heads], dw_in_p[:, d:3 * d]], axis=1)
        if l > 0:
            sc1 = _after(sc1, start_reduce(l, "w_in", dw_in))
        if sv["pending"] is None:
            (dx2,), (dsc1, dsh1), (dn1,) = _row_bwd(f"nm{l}_bwd", _nm_res_tile, [(sv["x_in"], d, 0, F32)], [sc1, sh1], [p["norm1"]], [(d, F32), (d, BF16)], 0, tm, [dx1, dh])
        else:
            rows = [(sv["x_in"], d, 0, F32), (sv["pending"][0], d, 0, BF16)]
            (dx2, ddown), (dg2_prev, dsc1, dsh1), (dn1,) = _row_bwd(
                f"rnm{l}a_bwd", _rnm_tile, rows, [sv["pending"][1], sc1, sh1], [p["norm1"]], [(d, F32), (d, BF16)], 0, tm, [dx1, dh])
        dmods[l] = jnp.concatenate([dsh1, dsc1, dg1, dsh2, dsc2, dg2], axis=1).reshape(bl, N_MOD * d)
        if sv["pending"] is not None:
            dg2 = dg2_prev
        small_grads[l] = dict(
            norm1_g=dn1.reshape(d), norm2_g=dn2.reshape(d), ssd_conv_b=dcb.reshape(cd), ssd_dt_bias=ddtb[0, :heads], ssd_a_log=dalog[0, :heads],
            ssd_d=ddsk.reshape(heads, HEAD_DIM).sum(-1), ssd_norm_g=dng.reshape(d), gm_vnorm_g=dvg.reshape(d), gm_ws=dws, gm_bs=dbs_t[:, :n_gm].T,
            gm_out_g=dog.reshape(d), ff_conv_b=dff_b.reshape(f), ssd_conv_w=dw8[:SSD_CONV], ff_conv_w=dff_w8[:FF_CONV])
    grad_x = dx2

    g_ada_w, g_ada_b, ada_token = _ada_bwd("ada", bl, c_all, ada_w.shape[2], jnp.stack(dmods))
    small_names_r = SMALL + CONVW + ("final_g",)
    summed, small_token = _allreduce_small("small", [jnp.stack([small_grads[l][n] for l in range(nl)]) for n in SMALL + CONVW] + [dfg])
    travelling = start_reduce(0, "w_in", _after(dw_in, ada_token + small_token))
    grad = {"ada_w": _after(g_ada_w, travelling), "ada_b": g_ada_b}
    grad.update(zip(small_names_r, summed))
    chip = 2 * lax.axis_index("x") + lax.axis_index("y")
    for n in CONVW:
        width = weights[n].shape[-1]
        grad[n] = lax.dynamic_slice_in_dim(grad[n], chip * width, width, axis=2)

    delta, new_m, new_v = {}, {}, {}
    big_grads, done = {}, _after(summed[0], travelling)
    for n in ("ada_w",) + tuple(reversed(BIG)):
        if n != "ada_w":
            for l in reversed(range(nl)):
                (big_grads[l, n],), _ = _reduce_weights_finish(f"wg{l}_{n}", reducing.pop((l, n)), done)
            grad[n] = jnp.stack([big_grads[l, n] for l in range(nl)])
        delta[n], new_m[n], new_v[n] = _adamw("adam_" + n, weights[n], grad[n], given["m_" + n], given["v_" + n])
        done = delta[n]
    small_names = ("ada_b",) + SMALL + CONVW + ("final_g",)
    as2d = lambda t: t.reshape(1, -1) if t.ndim == 1 else t
    res = _adamw_small(
        "adam_small", [as2d(weights[n]) for n in small_names], [as2d(grad[n]) for n in small_names],
        [as2d(given["m_" + n]) for n in small_names], [as2d(given["v_" + n]) for n in small_names])
    for out, vals in zip((delta, new_m, new_v), res):
        for n, val in zip(small_names, vals):
            out[n] = val.reshape(weights[n].shape)
    grad["final_g"] = grad["final_g"].reshape(final_g.shape)

    return (loss, grad_x, *[grad[n] for n in WEIGHTS], *[delta[n] for n in WEIGHTS], *[new_m[n] for n in WEIGHTS], *[new_v[n] for n in WEIGHTS])
```

```python
import functools

import jax
import jax.numpy as jnp
from jax import lax
from jax.experimental import pallas as pl
from jax.experimental.pallas import tpu as pltpu

F32 = jnp.float32
BF16 = jnp.bfloat16
EPS = 1e-6
CHUNK = 128
HEAD_DIM = 64
STATE = 128
GM_HEAD = 128
SSD_GROUPS = 2
SSD_CONV = 4
FF_CONV = 3
N_MOD = 6
LANES = 128
SUBLANES = 8
SMALL_BF16_SIZE = 1 << 16
CONV_LANES = 256
CONV_ROWS = 32
V7X_VMEM_LIMIT = 48 * 1024 * 1024
MM_VMEM_BUDGET = 30 * 1024 * 1024
MM_STEP_MACS = 2048 * 1024 * 1024
MM_ACC_SECONDS_PER_VREG = 1.4e-9
V7X_MXU_FLOPS = 996e12
V7X_HBM_BYTES_PER_S = 3.3e12
V7X_STEP_SECONDS = 0.35e-6
ADAM_LR, ADAM_B1, ADAM_B2, ADAM_EPS, ADAM_WD, ADAM_STEP = 0.001, 0.9, 0.999, 1e-08, 0.01, 10
MESH = pl.DeviceIdType.MESH
AXES = ("x", "y", "c")


def _round_up(n, m):
    return (n + m - 1) // m * m


def _divisors(n, cap, mult=LANES):
    out = [t for t in range(mult, min(n, cap) + 1, mult) if n % t == 0]
    return out or [n]


def _pcall(body, name, out_shape, grid=(), in_specs=None, out_specs=None, scratch=(), sem=None, prefetch=0):
    params = pltpu.CompilerParams(dimension_semantics=sem, vmem_limit_bytes=V7X_VMEM_LIMIT)
    if prefetch:
        spec = pltpu.PrefetchScalarGridSpec(num_scalar_prefetch=prefetch, grid=grid, in_specs=in_specs, out_specs=out_specs, scratch_shapes=list(scratch))
        return pl.pallas_call(body, name=name, out_shape=out_shape, grid_spec=spec, compiler_params=params)
    if in_specs is None:
        return pl.pallas_call(body, name=name, out_shape=out_shape, compiler_params=params)
    return pl.pallas_call(body, name=name, out_shape=out_shape, grid=grid, in_specs=in_specs, out_specs=out_specs, scratch_shapes=list(scratch), compiler_params=params)


def _sds(shape, dtype):
    return jax.ShapeDtypeStruct(tuple(shape), dtype)


def _mm_tiles(m, n, k, a_bytes, b_bytes, o_bytes):
    best, best_key = None, None
    for tm in _divisors(m, 2048):
        for tn in _divisors(n, 2560):
            for tk in _divisors(k, 2560):
                vmem = 2 * (tm * tk * a_bytes + tk * tn * b_bytes + tm * tn * o_bytes) + tm * tn * 4
                if vmem > MM_VMEM_BUDGET or tm * tn * tk > MM_STEP_MACS:
                    continue
                ni, nj, nk = m // tm, n // tn, k // tk
                a_reads = 1 if nk == 1 else nj
                b_reads = 1 if (nk == 1 and nj == 1) else ni
                hbm = a_reads * m * k * a_bytes + b_reads * k * n * b_bytes + m * n * o_bytes
                t = max(2.0 * m * n * k / V7X_MXU_FLOPS, hbm / V7X_HBM_BYTES_PER_S) + ni * nj * nk * V7X_STEP_SECONDS
                if nk > 1:
                    t += ni * nj * nk * (tm * tn // (SUBLANES * LANES)) * MM_ACC_SECONDS_PER_VREG
                key = (-t, tm * tn * tk)
                if best_key is None or key > best_key:
                    best, best_key = (tm, tn, tk), key
    return best


def _matmul(name, a, b, mode, out_dtype):
    if mode == "nn":
        (m, k), n = a.shape, b.shape[1]
    elif mode == "nt":
        (m, k), n = a.shape, b.shape[0]
    else:
        (k, m), n = a.shape, b.shape[1]
    tm, tn, tk = _mm_tiles(m, n, k, a.dtype.itemsize, b.dtype.itemsize, jnp.dtype(out_dtype).itemsize)
    nk = k // tk
    if mode == "nn":
        a_spec = pl.BlockSpec((tm, tk), lambda i, j, kk: (i, kk))
        b_spec = pl.BlockSpec((tk, tn), lambda i, j, kk: (kk, j))
        dims = ((1,), (0,))
    elif mode == "nt":
        a_spec = pl.BlockSpec((tm, tk), lambda i, j, kk: (i, kk))
        b_spec = pl.BlockSpec((tn, tk), lambda i, j, kk: (j, kk))
        dims = ((1,), (1,))
    else:
        a_spec = pl.BlockSpec((tk, tm), lambda i, j, kk: (kk, i))
        b_spec = pl.BlockSpec((tk, tn), lambda i, j, kk: (kk, j))
        dims = ((0,), (0,))

    def body(a_ref, b_ref, o_ref, acc_ref):
        kk = pl.program_id(2)
        p = lax.dot_general(a_ref[...].astype(BF16), b_ref[...].astype(BF16), (dims, ((), ())), preferred_element_type=F32)
        if nk == 1:
            o_ref[...] = p.astype(o_ref.dtype)
        else:
            @pl.when(kk == 0)
            def _():
                acc_ref[...] = p

            @pl.when(kk > 0)
            def _():
                acc_ref[...] += p

            @pl.when(kk == nk - 1)
            def _():
                o_ref[...] = acc_ref[...].astype(o_ref.dtype)

    return _pcall(
        body, name, _sds((m, n), out_dtype), grid=(m // tm, n // tn, nk), in_specs=[a_spec, b_spec],
        out_specs=pl.BlockSpec((tm, tn), lambda i, j, kk: (i, j)), scratch=[pltpu.VMEM((tm, tn), F32)],
        sem=("parallel", "parallel", "arbitrary"),
    )(a, b)


def _mm(name, a, w, out_dtype):
    return _matmul(name, a.reshape(-1, a.shape[-1]), w, "nn", out_dtype).reshape(a.shape[:-1] + (w.shape[1],))


def _mm_bwd(name, a, w, dy):
    a2, dy2 = a.reshape(-1, a.shape[-1]), dy.reshape(-1, dy.shape[-1])
    return _matmul(name + "_dx", dy2, w, "nt", BF16).reshape(a.shape), _matmul(name + "_dw", a2, dy2, "tn", BF16)


def _dg(a, b, dims):
    return lax.dot_general(a.astype(BF16), b.astype(BF16), (dims, ((), ())), preferred_element_type=F32)


@jax.custom_vjp
def _dot_nn(a, b):
    return _dg(a, b, ((1,), (0,)))


_dot_nn.defvjp(lambda a, b: (_dot_nn(a, b), (a, b)), lambda r, d: (_dg(d, r[1], ((1,), (1,))), _dg(r[0], d, ((0,), (0,)))))


@jax.custom_vjp
def _dot_nt(a, b):
    return _dg(a, b, ((1,), (1,)))


_dot_nt.defvjp(lambda a, b: (_dot_nt(a, b), (a, b)), lambda r, d: (_dg(d, r[1], ((1,), (0,))), _dg(d, r[0], ((0,), (0,)))))


@jax.custom_vjp
def _dot_tn(a, b):
    return _dg(a, b, ((0,), (0,)))


_dot_tn.defvjp(lambda a, b: (_dot_tn(a, b), (a, b)), lambda r, d: (_dg(r[1], d, ((1,), (1,))), _dg(r[0], d, ((1,), (0,)))))


def _exact_dot(a, c, dims):
    hi = a.astype(BF16)
    r1 = a - hi.astype(F32)
    mid = r1.astype(BF16)
    lo = (r1 - mid.astype(F32)).astype(BF16)
    cb = c.astype(BF16)
    f = lambda t: lax.dot_general(t, cb, (dims, ((), ())), preferred_element_type=F32)
    return f(hi) + f(mid) + f(lo)


@jax.custom_vjp
def _sel_right(a, c):
    return _exact_dot(a, c, ((1,), (0,)))


_sel_right.defvjp(lambda a, c: (_sel_right(a, c), c), lambda c, d: (_exact_dot(d, c, ((1,), (1,))), jnp.zeros_like(c)))


def _exact_dot_left(c, a, dims):
    hi = a.astype(BF16)
    r1 = a - hi.astype(F32)
    mid = r1.astype(BF16)
    lo = (r1 - mid.astype(F32)).astype(BF16)
    cb = c.astype(BF16)
    f = lambda t: lax.dot_general(cb, t, (dims, ((), ())), preferred_element_type=F32)
    return f(hi) + f(mid) + f(lo)


@jax.custom_vjp
def _sel_left(c, a):
    return _exact_dot_left(c, a, ((1,), (0,)))


_sel_left.defvjp(lambda c, a: (_sel_left(c, a), c), lambda c, d: (jnp.zeros_like(c), _exact_dot_left(c, d, ((0,), (0,)))))


def _sigmoid(x):
    return 1.0 / (1.0 + jnp.exp(-x))


def _rms(x, g):
    return x * lax.rsqrt(jnp.mean(x * x, axis=-1, keepdims=True) + EPS) * g


def _gelu(x):
    return 0.5 * x * (1.0 + lax.erf(x * (2.0 ** -0.5)))


def _causal(n):
    return lax.broadcasted_iota(jnp.int32, (n, n), 0) >= lax.broadcasted_iota(jnp.int32, (n, n), 1)


def _row_in_specs(rows, bparams, gparams, tm):
    specs = [pl.BlockSpec((1, tm, w), lambda b, i, cb=cb: (b, i, cb)) for (_, w, cb, _) in rows]
    specs += [pl.BlockSpec((1, 1, p.shape[-1]), lambda b, i: (b, 0, 0)) for p in bparams]
    specs += [pl.BlockSpec(p.shape, lambda b, i, n=p.ndim: (0,) * n) for p in gparams]
    return specs


def _row_vals(refs, n_rows, n_b, n_g):
    vals = [r[0].astype(F32) for r in refs[:n_rows]]
    vals += [r[0].astype(F32) for r in refs[n_rows:n_rows + n_b]]
    vals += [r[...].astype(F32) for r in refs[n_rows + n_b:n_rows + n_b + n_g]]
    return vals


def _row_fwd(name, tile, rows, bparams, gparams, outs, n_sum, tm):
    bl, s = rows[0][0].shape[:2]
    n_in = len(rows) + len(bparams) + len(gparams)

    def body(*refs):
        first = (pl.program_id(0) == 0) & (pl.program_id(1) == 0)
        res = tile(*_row_vals(refs, len(rows), len(bparams), len(gparams)))
        o_refs = refs[n_in:]
        for k in range(len(outs)):
            o_refs[k][0] = res[k].astype(o_refs[k].dtype)
        for k in range(n_sum):
            o_ref, val = o_refs[len(outs) + k], res[len(outs) + k]

            @pl.when(first)
            def _(o_ref=o_ref, val=val):
                o_ref[...] = val

            @pl.when(jnp.logical_not(first))
            def _(o_ref=o_ref, val=val):
                o_ref[...] += val

    out_shape = [_sds((bl, s, w), dt) for (w, dt) in outs] + [_sds((1, 1), F32)] * n_sum
    out_specs = [pl.BlockSpec((1, tm, w), lambda b, i: (b, i, 0)) for (w, _) in outs] + [pl.BlockSpec((1, 1), lambda b, i: (0, 0))] * n_sum
    return _pcall(
        body, name, out_shape, grid=(bl, s // tm), in_specs=_row_in_specs(rows, bparams, gparams, tm), out_specs=out_specs,
        sem=("arbitrary", "arbitrary"),
    )(*[r[0] for r in rows], *bparams, *gparams)


def _row_bwd(name, tile, rows, bparams, gparams, outs, n_sum, tm, cts):
    bl, s = rows[0][0].shape[:2]
    n_r, n_b, n_g = len(rows), len(bparams), len(gparams)
    n_in = n_r + n_b + n_g
    n_ct = len(outs) + n_sum
    grad_rows = [k for k in range(n_r) if rows[k][3]]

    def body(*refs):
        b, i = pl.program_id(0), pl.program_id(1)
        vals = _row_vals(refs, n_r, n_b, n_g)
        ct_refs = refs[n_in:n_in + n_ct]
        ct = [r[0].astype(F32) for r in ct_refs[:len(outs)]] + [r[...] for r in ct_refs[len(outs):]]
        _, vjp = jax.vjp(tile, *vals)
        grads = vjp(tuple(ct))
        o_refs = refs[n_in + n_ct:]
        for j, k in enumerate(grad_rows):
            o_refs[j][0] = grads[k].astype(o_refs[j].dtype)
        for k in range(n_b):
            o_ref, val = o_refs[len(grad_rows) + k], grads[n_r + k]

            @pl.when(i == 0)
            def _(o_ref=o_ref, val=val):
                o_ref[0] = val

            @pl.when(i > 0)
            def _(o_ref=o_ref, val=val):
                o_ref[0] += val

        first = (b == 0) & (i == 0)
        for k in range(n_g):
            o_ref, val = o_refs[len(grad_rows) + n_b + k], grads[n_r + n_b + k]

            @pl.when(first)
            def _(o_ref=o_ref, val=val):
                o_ref[...] = val

            @pl.when(jnp.logical_not(first))
            def _(o_ref=o_ref, val=val):
                o_ref[...] += val

    in_specs = _row_in_specs(rows, bparams, gparams, tm)
    in_specs += [pl.BlockSpec((1, tm, w), lambda b, i: (b, i, 0)) for (w, _) in outs] + [pl.BlockSpec((1, 1), lambda b, i: (0, 0))] * n_sum
    out_shape = [_sds((bl, s, rows[k][1]), rows[k][3]) for k in grad_rows]
    out_shape += [_sds(p.shape, F32) for p in bparams] + [_sds(p.shape, F32) for p in gparams]
    out_specs = [pl.BlockSpec((1, tm, rows[k][1]), lambda b, i: (b, i, 0)) for k in grad_rows]
    out_specs += [pl.BlockSpec((1, 1, p.shape[-1]), lambda b, i: (b, 0, 0)) for p in bparams]
    out_specs += [pl.BlockSpec(p.shape, lambda b, i, n=p.ndim: (0,) * n) for p in gparams]
    res = _pcall(
        body, name, out_shape, grid=(bl, s // tm), in_specs=in_specs, out_specs=out_specs, sem=("arbitrary", "arbitrary"),
    )(*[r[0] for r in rows], *bparams, *gparams, *cts)
    return res[:len(grad_rows)], res[len(grad_rows):len(grad_rows) + n_b], res[len(grad_rows) + n_b:]


def _nm_tile(x, sc, sh, g):
    return (_rms(x, g) * (1.0 + sc) + sh,)


def _nm_res_tile(x, sc, sh, g):
    return x, _rms(x, g) * (1.0 + sc) + sh


def _rnm_tile(x, o, gate, sc, sh, g):
    xn = x + gate * o
    return xn, _rms(xn, g) * (1.0 + sc) + sh


def _final_tile(x, o, tgt, gate, g):
    e = _rms(x + gate * o, g) - tgt
    return (0.5 * jnp.sum(jnp.mean(e * e, axis=-1, keepdims=True), axis=0, keepdims=True),)


def _gm_tile(u_in, v_in, vg, ws, bs_t, og):
    d = u_in.shape[1]
    u, vn = _gelu(u_in), _rms(_gelu(v_in), vg)
    causal = _causal(CHUNK)
    lane = lax.broadcasted_iota(jnp.int32, (1, LANES), 1)
    parts = []
    for h in range(d // GM_HEAD):
        bias = jnp.sum(bs_t * (lane == h).astype(F32), axis=1, keepdims=True)
        parts.append(_dot_nn(jnp.where(causal, ws[h], 0.0), vn[:, h * GM_HEAD:(h + 1) * GM_HEAD]) + bias)
    return (_rms(u * jnp.concatenate(parts, axis=1), og),)


def _conv_window(ref, r0, rows, c0, tc, seq, before, after):
    parts = []
    if before:
        p0 = pl.multiple_of(jnp.maximum(r0 - SUBLANES, 0), SUBLANES)
        parts.append(jnp.where(r0 > 0, ref[0, pl.ds(p0, SUBLANES), pl.ds(c0, tc)].astype(F32), 0.0))
    parts.append(ref[0, pl.ds(r0, rows), pl.ds(c0, tc)].astype(F32))
    if after:
        n0 = pl.multiple_of(jnp.minimum(r0 + rows, seq - SUBLANES), SUBLANES)
        parts.append(jnp.where(r0 + rows < seq, ref[0, pl.ds(n0, SUBLANES), pl.ds(c0, tc)].astype(F32), 0.0))
    return jnp.concatenate(parts, axis=0) if len(parts) > 1 else parts[0]


def _conv_taps_pre(xe, w, b, taps, rows):
    pre = xe[SUBLANES:SUBLANES + rows] * w[taps - 1] + b
    for j in range(1, taps):
        pre = pre + pltpu.roll(xe, j, 0)[SUBLANES:SUBLANES + rows] * w[taps - 1 - j]
    return pre


def _conv_fwd(name, src, col0, chans, w8, b, taps, gated, out_dtype):
    bl, s, _ = src.shape
    tc = CONV_LANES
    xw = 2 * tc if gated else tc

    def body(x_ref, w_ref, b_ref, o_ref):
        w = [w_ref[k:k + 1, :] for k in range(taps)]
        bias = b_ref[...]

        def step(c, carry):
            r0 = pl.multiple_of(c * CONV_ROWS, CONV_ROWS)
            xe = _conv_window(x_ref, r0, CONV_ROWS, 0, tc, s, True, False)
            pre = _conv_taps_pre(xe, w, bias, taps, CONV_ROWS)
            y = pre * _sigmoid(pre)
            if gated:
                y = y * x_ref[0, pl.ds(r0, CONV_ROWS), pl.ds(tc, tc)]
            o_ref[0, pl.ds(r0, CONV_ROWS), :] = y.astype(o_ref.dtype)
            return carry

        lax.fori_loop(0, s // CONV_ROWS, step, 0, unroll=4)

    first = 0 if gated else col0 // tc
    return _pcall(
        body, name, _sds((bl, s, chans), out_dtype), grid=(chans // tc, bl),
        in_specs=[pl.BlockSpec((1, s, xw), lambda j, bb: (bb, 0, first + j)), pl.BlockSpec((SUBLANES, tc), lambda j, bb: (0, j)), pl.BlockSpec((1, tc), lambda j, bb: (0, j))],
        out_specs=pl.BlockSpec((1, s, tc), lambda j, bb: (bb, 0, j)), sem=("parallel", "arbitrary"),
    )(src, w8, b)


def _conv_bwd(name, src, col0, chans, w8, b, taps, gated, dy, dx_dtype):
    bl, s, _ = src.shape
    tc = CONV_LANES
    xw = 2 * tc if gated else tc
    ext = CONV_ROWS + SUBLANES

    def body(x_ref, w_ref, b_ref, dy_ref, dx_ref, dw_ref, db_ref):
        bb = pl.program_id(1)
        w = [w_ref[k:k + 1, :] for k in range(taps)]
        bias = b_ref[...]

        def fold(v):
            acc = v[0:SUBLANES]
            for i in range(1, CONV_ROWS // SUBLANES):
                acc = acc + v[i * SUBLANES:(i + 1) * SUBLANES]
            return acc

        def step(c, carry):
            r0 = pl.multiple_of(c * CONV_ROWS, CONV_ROWS)
            xe = _conv_window(x_ref, r0, CONV_ROWS, 0, tc, s, True, True)
            pre = _conv_taps_pre(xe, w, bias, taps, ext)
            sig = _sigmoid(pre)
            d = _conv_window(dy_ref, r0, CONV_ROWS, 0, tc, s, False, True)
            dsil = d * _conv_window(x_ref, r0, CONV_ROWS, tc, tc, s, False, True) if gated else d
            dpre = dsil * sig * (1.0 + pre * (1.0 - sig))
            dx = dpre[:CONV_ROWS] * w[taps - 1]
            for j in range(1, taps):
                dx = dx + pltpu.roll(dpre, ext - j, 0)[:CONV_ROWS] * w[taps - 1 - j]
            dx_ref[0, pl.ds(r0, CONV_ROWS), pl.ds(0, tc)] = dx.astype(dx_ref.dtype)
            if gated:
                dx_ref[0, pl.ds(r0, CONV_ROWS), pl.ds(tc, tc)] = (d[:CONV_ROWS] * (pre * sig)[:CONV_ROWS]).astype(dx_ref.dtype)
            here = dpre[:CONV_ROWS]
            sums = [fold(here * (pltpu.roll(xe, taps - 1 - k, 0) if k < taps - 1 else xe)[SUBLANES:ext]) + carry[k] for k in range(taps)]
            return tuple(sums) + (fold(here) + carry[taps],)

        zero = jnp.zeros((SUBLANES, tc), F32)
        sums = lax.fori_loop(0, s // CONV_ROWS, step, (zero,) * (taps + 1), unroll=2)
        rows = [jnp.sum(t, axis=0, keepdims=True) for t in sums]
        dw = jnp.concatenate(rows[:taps] + [jnp.zeros_like(rows[0])] * (SUBLANES - taps), axis=0)

        @pl.when(bb == 0)
        def _():
            dw_ref[...] = dw
            db_ref[...] = rows[taps]

        @pl.when(bb > 0)
        def _():
            dw_ref[...] += dw
            db_ref[...] += rows[taps]

    first = 0 if gated else col0 // tc
    out_shape = [_sds((bl, s, chans * (2 if gated else 1)), dx_dtype), _sds((SUBLANES, chans), F32), _sds((1, chans), F32)]
    return _pcall(
        body, name, out_shape, grid=(chans // tc, bl),
        in_specs=[
            pl.BlockSpec((1, s, xw), lambda j, bb: (bb, 0, first + j)), pl.BlockSpec((SUBLANES, tc), lambda j, bb: (0, j)),
            pl.BlockSpec((1, tc), lambda j, bb: (0, j)), pl.BlockSpec((1, s, tc), lambda j, bb: (bb, 0, j)),
        ],
        out_specs=[pl.BlockSpec((1, s, xw), lambda j, bb: (bb, 0, j)), pl.BlockSpec((SUBLANES, tc), lambda j, bb: (0, j)), pl.BlockSpec((1, tc), lambda j, bb: (0, j))],
        sem=("parallel", "arbitrary"),
    )(src, w8, b, dy)


def _gate_value_blocks(name, w, f, inverse=False):
    nb = f // CONV_LANES
    src = (lambda j: (0, 2 * (j % nb) + j // nb)) if inverse else (lambda j: (0, (j % 2) * nb + j // 2))

    def body(x_ref, o_ref):
        o_ref[...] = x_ref[...]

    return _pcall(
        body, name, _sds(w.shape, w.dtype), grid=(2 * nb,), in_specs=[pl.BlockSpec((w.shape[0], CONV_LANES), src)],
        out_specs=pl.BlockSpec((w.shape[0], CONV_LANES), lambda j: (0, j)), sem=("parallel",),
    )(w)


def _ssd_chunk(g_idx, states, xs, bm, cm, dtr, z, dtb, alog, dsk, ng):
    gw = xs.shape[1]
    hpg = gw // HEAD_DIM
    dt = jax.nn.softplus(dtr + dtb)
    da = dt * (-jnp.exp(alog))
    causal = _causal(CHUNK)
    acs = _sel_left(causal.astype(F32), da)
    head_of_col = g_idx * hpg + lax.broadcasted_iota(jnp.int32, (LANES, gw), 1) // HEAD_DIM
    expand = (lax.broadcasted_iota(jnp.int32, (LANES, gw), 0) == head_of_col).astype(F32)
    dt_e, acs_e = _sel_right(dt, expand), _sel_right(acs, expand)
    last = lax.broadcasted_iota(jnp.int32, (CHUNK, gw), 0) == CHUNK - 1
    alast_e = jnp.sum(jnp.where(last, acs_e, 0.0), axis=0, keepdims=True)
    xc = xs * dt_e
    xc_st = xc * jnp.exp(alast_e - acs_e)
    decay_out, chunk_decay = jnp.exp(acs_e), jnp.exp(alast_e)
    cb = _dot_nt(cm, bm)
    acs_t = acs.T
    lane = lax.broadcasted_iota(jnp.int32, (1, LANES), 1)
    sub = lax.broadcasted_iota(jnp.int32, (LANES, 1), 0)
    ys, new_states = [], []
    for p in range(gw // LANES):
        sl = slice(p * LANES, (p + 1) * LANES)
        xcp = xc[:, sl]
        y = _dot_nn(cm, states[p]) * decay_out[:, sl]
        for q in range(2):
            head = g_idx * hpg + 2 * p + q
            col = jnp.sum(acs * (lane == head).astype(F32), axis=1, keepdims=True)
            row = jnp.sum(acs_t * (sub == head).astype(F32), axis=0, keepdims=True)
            decay = jnp.where(causal, jnp.exp(jnp.where(causal, col - row, 0.0)), 0.0)
            half = ((lane // HEAD_DIM) == q).astype(F32)
            y = y + _dot_nn(cb * decay, xcp * half)
        ys.append(y)
        new_states.append(states[p] * chunk_decay[:, sl] + _dot_tn(bm, xc_st[:, sl]))
    y = jnp.concatenate(ys, axis=1) + dsk * xs
    gated = y * (z * _sigmoid(z))
    return tuple(new_states), _rms(gated, ng)


def _ssd_specs(d, gw, dt_col, rev, nc):
    ci = (lambda i: nc - 1 - i) if rev else (lambda i: i)
    return [
        pl.BlockSpec((1, CHUNK, gw), lambda g, b, i: (b, ci(i), g)),
        pl.BlockSpec((1, CHUNK, STATE), lambda g, b, i: (b, ci(i), d // STATE + g)),
        pl.BlockSpec((1, CHUNK, STATE), lambda g, b, i: (b, ci(i), d // STATE + SSD_GROUPS + g)),
        pl.BlockSpec((1, CHUNK, LANES), lambda g, b, i: (b, ci(i), dt_col // LANES)),
        pl.BlockSpec((1, CHUNK, gw), lambda g, b, i: (b, ci(i), g)),
        pl.BlockSpec((1, LANES), lambda g, b, i: (0, 0)), pl.BlockSpec((1, LANES), lambda g, b, i: (0, 0)),
        pl.BlockSpec((1, gw), lambda g, b, i: (0, g)), pl.BlockSpec((1, gw), lambda g, b, i: (0, g)),
    ]


def _ssd_fwd(name, xbc, proj, dt_col, dtb, alog, dsk, ng):
    bl, s, cd = xbc.shape
    d = dsk.shape[1]
    gw, nc = d // SSD_GROUPS, s // CHUNK
    npair = gw // LANES

    def body(xs_ref, bm_ref, cm_ref, dt_ref, z_ref, dtb_ref, alog_ref, dsk_ref, ng_ref, y_ref, hp_ref, st_ref):
        g, i = pl.program_id(0), pl.program_id(2)

        @pl.when(i == 0)
        def _():
            st_ref[...] = jnp.zeros_like(st_ref)

        states = tuple(st_ref[p] for p in range(npair))
        hp_ref[0, 0, 0] = st_ref[...]
        new_states, yn = _ssd_chunk(g, states, xs_ref[0], bm_ref[0], cm_ref[0], dt_ref[0], z_ref[0], dtb_ref[...], alog_ref[...], dsk_ref[...], ng_ref[...])
        for p in range(npair):
            st_ref[p] = new_states[p]
        y_ref[0] = yn.astype(y_ref.dtype)

    out_shape = [_sds((bl, s, d), BF16), _sds((SSD_GROUPS, bl, nc, npair, STATE, LANES), F32)]
    out_specs = [
        pl.BlockSpec((1, CHUNK, gw), lambda g, b, i: (b, i, g)),
        pl.BlockSpec((1, 1, 1, npair, STATE, LANES), lambda g, b, i: (g, b, i, 0, 0, 0)),
    ]
    return _pcall(
        body, name, out_shape, grid=(SSD_GROUPS, bl, nc), in_specs=_ssd_specs(d, gw, dt_col, False, nc), out_specs=out_specs,
        scratch=[pltpu.VMEM((npair, STATE, LANES), F32)], sem=("arbitrary", "arbitrary", "arbitrary"),
    )(xbc, xbc, xbc, proj, proj, dtb, alog, dsk, ng)


def _ssd_bwd(name, xbc, proj, dt_col, dtb, alog, dsk, ng, hprev, dy):
    bl, s, cd = xbc.shape
    d = dsk.shape[1]
    gw, nc = d // SSD_GROUPS, s // CHUNK
    npair = gw // LANES

    def body(xs_ref, bm_ref, cm_ref, dt_ref, z_ref, dtb_ref, alog_ref, dsk_ref, ng_ref, hp_ref, dy_ref,
             dxs_ref, dbm_ref, dcm_ref, ddt_ref, dz_ref, ddtb_ref, dalog_ref, ddsk_ref, dng_ref, dst_ref):
        g, b, i = pl.program_id(0), pl.program_id(1), pl.program_id(2)

        @pl.when(i == 0)
        def _():
            dst_ref[...] = jnp.zeros_like(dst_ref)

        states = tuple(hp_ref[0, 0, 0, p] for p in range(npair))
        step = functools.partial(_ssd_chunk, g)
        _, vjp = jax.vjp(step, states, xs_ref[0], bm_ref[0], cm_ref[0], dt_ref[0], z_ref[0], dtb_ref[...], alog_ref[...], dsk_ref[...], ng_ref[...])
        d_states, dxs, dbm, dcm, ddt, dz, ddtb, dalog, ddsk, dng = vjp((tuple(dst_ref[p] for p in range(npair)), dy_ref[0].astype(F32)))
        for p in range(npair):
            dst_ref[p] = d_states[p]
        dxs_ref[0], dbm_ref[0], dcm_ref[0] = dxs, dbm, dcm
        dz_ref[0] = dz.astype(dz_ref.dtype)
        ddt_ref[0, 0] = ddt
        first_g = (b == 0) & (i == 0)
        first = first_g & (g == 0)
        for o_ref, val, init in ((ddtb_ref, ddtb, first), (dalog_ref, dalog, first), (ddsk_ref, ddsk, first_g), (dng_ref, dng, first_g)):
            @pl.when(init)
            def _(o_ref=o_ref, val=val):
                o_ref[...] = val

            @pl.when(jnp.logical_not(init))
            def _(o_ref=o_ref, val=val):
                o_ref[...] += val

    rc = lambda i: nc - 1 - i
    in_specs = _ssd_specs(d, gw, dt_col, True, nc) + [
        pl.BlockSpec((1, 1, 1, npair, STATE, LANES), lambda g, b, i: (g, b, rc(i), 0, 0, 0)),
        pl.BlockSpec((1, CHUNK, gw), lambda g, b, i: (b, rc(i), g)),
    ]
    out_shape = [
        _sds((bl, s, d), F32), _sds((bl, s, SSD_GROUPS * STATE), F32), _sds((bl, s, SSD_GROUPS * STATE), F32),
        _sds((SSD_GROUPS, bl, s, LANES), F32), _sds((bl, s, d), BF16),
        _sds((1, LANES), F32), _sds((1, LANES), F32), _sds((1, d), F32), _sds((1, d), F32),
    ]
    out_specs = [
        pl.BlockSpec((1, CHUNK, gw), lambda g, b, i: (b, rc(i), g)),
        pl.BlockSpec((1, CHUNK, STATE), lambda g, b, i: (b, rc(i), g)), pl.BlockSpec((1, CHUNK, STATE), lambda g, b, i: (b, rc(i), g)),
        pl.BlockSpec((1, 1, CHUNK, LANES), lambda g, b, i: (g, b, rc(i), 0)),
        pl.BlockSpec((1, CHUNK, gw), lambda g, b, i: (b, rc(i), g)),
        pl.BlockSpec((1, LANES), lambda g, b, i: (0, 0)), pl.BlockSpec((1, LANES), lambda g, b, i: (0, 0)),
        pl.BlockSpec((1, gw), lambda g, b, i: (0, g)), pl.BlockSpec((1, gw), lambda g, b, i: (0, g)),
    ]
    return _pcall(
        body, name, out_shape, grid=(SSD_GROUPS, bl, nc), in_specs=in_specs, out_specs=out_specs,
        scratch=[pltpu.VMEM((npair, STATE, LANES), F32)], sem=("arbitrary", "arbitrary", "arbitrary"),
    )(xbc, xbc, xbc, proj, proj, dtb, alog, dsk, ng, hprev, dy)


def _mixer_fwd(name, proj, d, cd, conv_w8, conv_b, dtb, alog, dsk, ng, vg, ws, bs_t, og):
    xbc = _conv_fwd(name + "_conv_fwd", proj, 3 * d, cd, conv_w8, conv_b, SSD_CONV, False, F32)
    y, hprev = _ssd_fwd(name + "_ssd_fwd", xbc, proj, 3 * d + cd, dtb, alog, dsk, ng)
    rows = [(proj, d, 1, BF16), (proj, d, 2, BF16)]
    (g_out,) = _row_fwd(name + "_gm_fwd", _gm_tile, rows, [], [vg, ws, bs_t, og], [(d, BF16)], 0, CHUNK)
    return jnp.concatenate([y, g_out], axis=-1), (xbc, hprev)


def _mixer_bwd(name, proj, d, cd, conv_w8, conv_b, dtb, alog, dsk, ng, vg, ws, bs_t, og, xbc, hprev, dycat):
    bl, s, n_proj = proj.shape
    dt_col = 3 * d + cd
    dy, dg_out = dycat[..., :d], dycat[..., d:]
    dxs, dbm, dcm, ddt2, dz, ddtb, dalog, ddsk, dng = _ssd_bwd(name + "_ssd_bwd", xbc, proj, dt_col, dtb, alog, dsk, ng, hprev, dy)
    dxbc_act = jnp.concatenate([dxs, dbm, dcm], axis=-1)
    dxbc, dw8, dcb = _conv_bwd(name + "_conv_bwd", proj, 3 * d, cd, conv_w8, conv_b, SSD_CONV, False, dxbc_act, BF16)
    rows = [(proj, d, 1, BF16), (proj, d, 2, BF16)]
    (du, dv), _, (dvg, dws, dbs_t, dog) = _row_bwd(name + "_gm_bwd", _gm_tile, rows, [], [vg, ws, bs_t, og], [(d, BF16)], 0, CHUNK, [dg_out])
    ddt = (ddt2[0] + ddt2[1]).astype(BF16)
    pad = jnp.zeros((bl, s, n_proj - dt_col - LANES), BF16)
    dproj = jnp.concatenate([dz, du, dv, dxbc, ddt, pad], axis=-1)
    return dproj, (dw8, dcb, ddtb, dalog, ddsk, dng, dvg, dws, dbs_t, dog)


def _position():
    return lax.axis_index("x"), lax.axis_index("y"), lax.axis_index("c")


def _at(ref, idx):
    return ref.at[idx] if len(idx) else ref


def _exchange(name, inputs, out_shapes, plan, inplace=False, after=None):
    if inplace:
        out_shapes = [(a.shape, a.dtype) for a in inputs]
    n_in, n_out = len(inputs), len(out_shapes)
    n_after = 0 if after is None else 1
    n_copy = len(plan(0, 0, 0))

    def body(*refs):
        in_refs, out_refs, token = refs[:n_in], refs[n_in + n_after:n_in + n_after + n_out], refs[n_in + n_after + n_out]
        send_sems, recv_sems = refs[n_in + n_after + n_out + 1:]
        token[...] = jnp.zeros_like(token)
        x, y, c = _position()
        copies = plan(x, y, c)

        def copy(k, src, dst, peer):
            return pltpu.make_async_remote_copy(src_ref=src, dst_ref=dst, send_sem=send_sems.at[k], recv_sem=recv_sems.at[k], device_id=peer, device_id_type=MESH)

        src_refs = out_refs if inplace else in_refs
        sends = [copy(k, _at(src_refs[sa], si), _at(out_refs[da], di), peer) for k, (sa, si, da, di, peer, _) in enumerate(copies)]
        for cp in sends:
            cp.start()
        for k, (sa, si, da, _, peer, li) in enumerate(copies):
            copy(k, _at(src_refs[sa], si), _at(out_refs[da], li), peer).wait_recv()
        for cp in sends:
            cp.wait_send()

    any_spec = pl.BlockSpec(memory_space=pl.ANY)
    outs = pl.pallas_call(
        body, name=name, out_shape=[_sds(s, dt) for s, dt in out_shapes] + [_sds((SUBLANES, LANES), F32)], in_specs=[any_spec] * (n_in + n_after),
        out_specs=[any_spec] * n_out + [pl.BlockSpec(memory_space=pltpu.VMEM)],
        scratch_shapes=[pltpu.SemaphoreType.DMA((n_copy,)), pltpu.SemaphoreType.DMA((n_copy,))],
        input_output_aliases={i: i for i in range(n_in)} if inplace else {},
    )(*inputs, *([] if after is None else [after]))
    return list(outs[:n_out]), outs[n_out]


def _exchange_start(name, inputs, out_shapes, plan):
    n_in, n_out = len(inputs), len(out_shapes)
    n_copy = len(plan(0, 0, 0))

    def body(*refs):
        in_refs, land_refs = refs[:n_in], refs[n_in:n_in + n_out]
        send_sems, recv_sems = refs[n_in + n_out:n_in + n_out + 2]
        token = refs[-1]
        x, y, c = _position()
        for k, (sa, si, da, di, peer, _) in enumerate(plan(x, y, c)):
            pltpu.make_async_remote_copy(
                src_ref=_at(in_refs[sa], si), dst_ref=_at(land_refs[da], di), send_sem=send_sems.at[k], recv_sem=recv_sems.at[k],
                device_id=peer, device_id_type=MESH).start()
        token[...] = jnp.zeros_like(token)

    hbm, sem = pl.BlockSpec(memory_space=pltpu.HBM), pl.BlockSpec(memory_space=pltpu.SEMAPHORE)
    lands = [lax.empty(s, dt) for s, dt in out_shapes]
    args = [pltpu.with_memory_space_constraint(a, pltpu.HBM) for a in list(inputs) + lands]
    outs = pl.pallas_call(
        body, name=name,
        out_shape=(pltpu.SemaphoreType.DMA((n_copy,)), pltpu.SemaphoreType.DMA((n_copy,)), *[pltpu.HBM(a.shape, a.dtype) for a in args], _sds((SUBLANES, LANES), F32)),
        in_specs=[hbm] * (n_in + n_out), out_specs=(sem, sem, *[hbm] * (n_in + n_out), pl.BlockSpec(memory_space=pltpu.VMEM)),
        input_output_aliases={i: 2 + i for i in range(n_in + n_out)},
        compiler_params=pltpu.CompilerParams(has_side_effects=pltpu.SideEffectType.DATAFLOW_SIDE_EFFECTING),
    )(*args)
    return dict(name=name, plan=plan, sems=outs[:2], ins=list(outs[2:2 + n_in]), lands=list(outs[2 + n_in:2 + n_in + n_out]), token=outs[-1])


def _exchange_wait(started, after):
    plan, n_in, n_out = started["plan"], len(started["ins"]), len(started["lands"])

    def body(*refs):
        in_refs, land_refs = refs[:n_in], refs[n_in:n_in + n_out]
        send_sems, recv_sems = refs[n_in + n_out:n_in + n_out + 2]
        token = refs[-1]
        x, y, c = _position()
        for k, (sa, si, da, _, peer, li) in enumerate(plan(x, y, c)):
            cp = pltpu.make_async_remote_copy(
                src_ref=_at(in_refs[sa], si), dst_ref=_at(land_refs[da], li), send_sem=send_sems.at[k], recv_sem=recv_sems.at[k],
                device_id=peer, device_id_type=MESH)
            cp.wait_send()
            cp.wait_recv()
        token[...] = jnp.zeros_like(token)

    hbm, sem = pl.BlockSpec(memory_space=pltpu.HBM), pl.BlockSpec(memory_space=pltpu.SEMAPHORE)
    bufs = started["ins"] + started["lands"]
    outs = pl.pallas_call(
        body, name=started["name"] + "_wait", out_shape=(*[pltpu.HBM(a.shape, a.dtype) for a in bufs], _sds((SUBLANES, LANES), F32)),
        in_specs=[hbm] * len(bufs) + [sem, sem, pl.BlockSpec(memory_space=pl.ANY)], out_specs=(*[hbm] * len(bufs), pl.BlockSpec(memory_space=pltpu.VMEM)),
        input_output_aliases={i: i for i in range(len(bufs))},
        compiler_params=pltpu.CompilerParams(has_side_effects=pltpu.SideEffectType.DATAFLOW_SIDE_EFFECTING),
    )(*bufs, *started["sems"], after)
    return list(outs[:n_in]), list(outs[n_in:n_in + n_out]), outs[-1]


def _after(value, token):
    return value + token[0, 0].astype(value.dtype)


def _chip_peers(x, y, c):
    return [(1 - x, y, c), (x, 1 - y, c), (1 - x, 1 - y, c)]


def _chip_of(p):
    return 2 * p[0] + p[1]


def _set_slot(slots, me, blk):
    return lax.dynamic_update_slice(slots, blk[None], (me,) + (0,) * blk.ndim)


def _by_core(c, mine, other, axis):
    return jnp.where(c == 0, jnp.stack([mine, other], axis), jnp.stack([other, mine], axis))


def _plan_gather_chips(n):
    def plan(x, y, c):
        return [(a, (), a, (2 * x + y,), p, (_chip_of(p),)) for a in range(n) for p in _chip_peers(x, y, c)]

    return plan


def _gather_chips(name, blocks):
    recv, token = _exchange(name, blocks, [((4,) + b.shape, b.dtype) for b in blocks], _plan_gather_chips(len(blocks)))
    x, y, _ = _position()
    return [_set_slot(r, 2 * x + y, b) for r, b in zip(recv, blocks)], token


def _gather_pass_cores(name, blocks, from_chips):
    n = len(blocks)

    def plan_cores(x, y, c):
        me, sib = 2 * x + y, (x, y, 1 - c)
        own = [(a, (), a, (me,), sib, (me,)) for a in range(n)]
        passed = [(n + a, (_chip_of(p),), a, (_chip_of(p),), sib, (_chip_of(p),)) for a in range(n) for p in _chip_peers(x, y, c)]
        return own + passed

    from_core, token = _exchange(name + "_cores", list(blocks) + list(from_chips), [((4,) + b.shape, b.dtype) for b in blocks], plan_cores)
    x, y, c = _position()
    return [_by_core(c, _set_slot(r1, 2 * x + y, b), r2, 1) for b, r1, r2 in zip(blocks, from_chips, from_core)], token


def _gather_two_level(name, blocks):
    from_chips, _ = _exchange(name + "_chips", blocks, [((4,) + b.shape, b.dtype) for b in blocks], _plan_gather_chips(len(blocks)))
    return _gather_pass_cores(name, blocks, from_chips)


def _pair_add(name, g42, r4):
    _, _, rh, cols = g42.shape
    tr = _divisors(rh, 512, SUBLANES * 2)[-1]

    def body(c_ref, a_ref, b_ref, o_ref):
        o_ref[0] = (a_ref[0, 0].astype(F32) + b_ref[0].astype(F32)).astype(o_ref.dtype)

    cidx = lax.axis_index("c").astype(jnp.int32).reshape(1)
    return _pcall(
        body, name, _sds(r4.shape, BF16), grid=(4, rh // tr),
        in_specs=[pl.BlockSpec((1, 1, tr, cols), lambda s, i, c_ref: (s, c_ref[0], i, 0)), pl.BlockSpec((1, tr, cols), lambda s, i, c_ref: (s, i, 0))],
        out_specs=pl.BlockSpec((1, tr, cols), lambda s, i, c_ref: (s, i, 0)), sem=("parallel", "parallel"), prefetch=1,
    )(cidx, g42, r4)


def _slot_sum(name, parts):
    n, r, cols = parts.shape
    cap = max(2 * SUBLANES, (4 * 1024 * 1024) // (n * cols * parts.dtype.itemsize))
    tr = _divisors(r, cap, 2 * SUBLANES)[-1]

    def body(p_ref, o_ref):
        acc = p_ref[0].astype(F32)
        for k in range(1, n):
            acc = acc + p_ref[k].astype(F32)
        o_ref[...] = acc

    return _pcall(
        body, name, _sds((r, cols), F32), grid=(r // tr,), in_specs=[pl.BlockSpec((n, tr, cols), lambda i: (0, i, 0))],
        out_specs=pl.BlockSpec((tr, cols), lambda i: (i, 0)), sem=("parallel",),
    )(parts)


def _slot_sums(name, parts):
    k = len(parts)

    def body(*refs):
        for p_ref, o_ref in zip(refs[:k], refs[k:]):
            acc = p_ref[0].astype(F32)
            for j in range(1, p_ref.shape[0]):
                acc = acc + p_ref[j].astype(F32)
            o_ref[...] = acc

    return list(_pcall(body, name, [_sds(p.shape[1:], F32) for p in parts])(*parts))


def _gather_weights_start(name, shards, token):
    c = lax.axis_index("c")
    halves = [lax.dynamic_slice_in_dim(w, c * (w.shape[0] // 2), w.shape[0] // 2, 0).astype(BF16) for w in shards]
    if token is not None:
        halves[0] = _after(halves[0], token)
    n = len(halves)

    def plan(x, y, c):
        return [(a, (), a, (2 * x + y, c), p, (_chip_of(p), c)) for a in range(n) for p in _chip_peers(x, y, c)]

    return _exchange_start(name + "_ag_chips", halves, [((4, 2) + h.shape, h.dtype) for h in halves], plan)


def _gather_weights_finish(name, started, after):
    halves, slots, token = _exchange_wait(started, after)
    n = len(halves)
    x, y, c = _position()
    slots = [lax.dynamic_update_slice(s, h[None, None], (2 * x + y, c, 0, 0)) for s, h in zip(slots, halves)]

    def plan_cores(x, y, c):
        return [(a, (s, c), a, (s, c), (x, y, 1 - c), (s, 1 - c)) for a in range(n) for s in range(4)]

    full, _ = _exchange(name + "_ag_cores", slots, None, plan_cores, inplace=True)
    return [f.reshape((4, 2 * h.shape[0], h.shape[1])) for f, h in zip(full, halves)], token


def _reduce_weights_start(name, grads, token):
    n = len(grads)
    g42 = [g.reshape(4, 2, g.shape[1] // 2, g.shape[2]) for g in grads]

    def plan_swap(x, y, c):
        return [(a, (s, 1 - c), a, (s,), (x, y, 1 - c), (s,)) for a in range(n) for s in range(4)]

    def plan_chips(x, y, c):
        return [(a, (_chip_of(p),), a, (2 * x + y,), p, (_chip_of(p),)) for a in range(n) for p in _chip_peers(x, y, c)]

    other, _ = _exchange(name + "_rs_cores", g42, [((4,) + g.shape[2:], g.dtype) for g in g42], plan_swap, after=token)
    pair = [_pair_add(f"{name}_rs_pair{a}", g, o) for a, (g, o) in enumerate(zip(g42, other))]
    return _exchange_start(name + "_rs_chips", pair, [(p.shape, p.dtype) for p in pair], plan_chips)


def _reduce_weights_finish(name, started, after):
    pair, recv, token = _exchange_wait(started, after)
    n = len(pair)
    x, y, c = _position()
    me = 2 * x + y

    def plan_share(x, y, c):
        return [(a, (c,), a, (c,), (x, y, 1 - c), (1 - c,)) for a in range(n)]

    parts = [lax.dynamic_update_slice(r, lax.dynamic_slice_in_dim(p, me, 1, 0), (me, 0, 0)) for r, p in zip(recv, pair)]
    mine = [_slot_sum(f"{name}_rs_sum{a}", p) for a, p in enumerate(parts)]
    both = [lax.dynamic_update_slice(lax.empty((2,) + m.shape, m.dtype), m[None], (c, 0, 0)) for m in mine]
    both, _ = _exchange(name + "_rs_share", both, None, plan_share, inplace=True)
    return [b.reshape(2 * b.shape[1], b.shape[2]) for b in both], token


def _allreduce_small(name, grads):
    both, token = _gather_two_level(name + "_ag", [g.astype(BF16) if g.size >= SMALL_BF16_SIZE else g for g in grads])
    return _slot_sums(name + "_sum", [g.reshape((8,) + g.shape[2:]) for g in both]), token


def _ada_fwd_call(name, c_all, w, b_shard):
    nl, d, ns = w.shape
    nb = c_all.shape[0]

    def body(c_ref, w_ref, b_ref, o_ref):
        cv = c_ref[...]
        o_ref[0] = _dg(cv * _sigmoid(cv), w_ref[0], ((1,), (0,))) + b_ref[0]

    return _pcall(
        body, name, _sds((nl, nb, ns), F32), grid=(nl,),
        in_specs=[pl.BlockSpec((nb, d), lambda l: (0, 0)), pl.BlockSpec((1, d, ns), lambda l: (l, 0, 0)), pl.BlockSpec((1, 1, ns), lambda l: (l, 0, 0))],
        out_specs=pl.BlockSpec((1, nb, ns), lambda l: (l, 0, 0)), sem=("parallel",),
    )(c_all, w, b_shard)


def _ada_bwd_call(name, c_all, dm_shard, dm_all):
    nl, nb, ns = dm_shard.shape
    d = c_all.shape[1]
    nm = dm_all.shape[2]

    def body(c_ref, ds_ref, da_ref, dw_ref, db_ref):
        cv = c_ref[...]
        dw_ref[0] = _dg(cv * _sigmoid(cv), ds_ref[0], ((0,), (0,)))
        db_ref[0] = jnp.sum(da_ref[0], axis=0, keepdims=True)

    return _pcall(
        body, name, [_sds((nl, d, ns), F32), _sds((nl, 1, nm), F32)], grid=(nl,),
        in_specs=[pl.BlockSpec((nb, d), lambda l: (0, 0)), pl.BlockSpec((1, nb, ns), lambda l: (l, 0, 0)), pl.BlockSpec((1, nb, nm), lambda l: (l, 0, 0))],
        out_specs=[pl.BlockSpec((1, d, ns), lambda l: (l, 0, 0)), pl.BlockSpec((1, 1, nm), lambda l: (l, 0, 0))], sem=("parallel",),
    )(c_all, dm_shard, dm_all)


def _ada_fwd(name, bl, c_all, w, b):
    nl, d, ns = w.shape
    chip = 2 * lax.axis_index("x") + lax.axis_index("y")
    b_shard = lax.dynamic_slice(b, (0, chip * ns), (nl, ns)).reshape(nl, 1, ns)
    shard = _ada_fwd_call(name + "_fwd", c_all, w, b_shard)
    (allc,), token = _gather_chips(name + "_ag", [shard])
    mods = jnp.transpose(allc, (1, 2, 0, 3)).reshape(nl, c_all.shape[0], 4 * ns)
    return lax.dynamic_slice(mods, (0, (2 * chip + lax.axis_index("c")) * bl, 0), (nl, bl, 4 * ns)), token


def _ada_bwd(name, bl, c_all, ns, dm):
    nl = dm.shape[0]
    chip = 2 * lax.axis_index("x") + lax.axis_index("y")
    (dm_all,), token = _gather_two_level(name + "_bwd_ag", [dm])
    dm_all = jnp.transpose(dm_all.reshape((8,) + dm.shape), (1, 0, 2, 3)).reshape(nl, 8 * bl, 4 * ns)
    dm_shard = lax.dynamic_slice(dm_all, (0, 0, chip * ns), (nl, 8 * bl, ns))
    dw, db = _ada_bwd_call(name + "_bwd", c_all, dm_shard, dm_all)
    return dw, db.reshape(nl, 4 * ns), token


def _adamw(name, w, g, m, v):
    shape = w.shape
    cols = shape[-1]
    w2, g2, m2, v2 = (t.reshape(-1, cols) for t in (w, g, m, v))
    rows = w2.shape[0]
    cap = max(SUBLANES, (512 * 1024) // max(cols, 1) // SUBLANES * SUBLANES)
    tr = _divisors(rows, cap, SUBLANES)[-1]

    def body(w_ref, g_ref, m_ref, v_ref, d_ref, mo_ref, vo_ref):
        gv = g_ref[...]
        mn = ADAM_B1 * m_ref[...] + (1.0 - ADAM_B1) * gv
        vn = ADAM_B2 * v_ref[...] + (1.0 - ADAM_B2) * (gv * gv)
        m_hat = mn / (1.0 - ADAM_B1 ** ADAM_STEP)
        v_hat = vn / (1.0 - ADAM_B2 ** ADAM_STEP)
        d_ref[...] = -ADAM_LR * (m_hat / (jnp.sqrt(v_hat) + ADAM_EPS) + ADAM_WD * w_ref[...])
        mo_ref[...] = mn
        vo_ref[...] = vn

    spec = pl.BlockSpec((tr, cols), lambda i: (i, 0))
    outs = _pcall(body, name, [_sds((rows, cols), F32)] * 3, grid=(rows // tr,), in_specs=[spec] * 4, out_specs=[spec] * 3, sem=("parallel",))(w2, g2, m2, v2)
    return tuple(o.reshape(shape) for o in outs)


def _adamw_small(name, ws, gs, ms, vs):
    n = len(ws)

    def body(*refs):
        for k in range(n):
            w_ref, g_ref, m_ref, v_ref = (refs[j * n + k] for j in range(4))
            d_ref, mo_ref, vo_ref = (refs[(4 + j) * n + k] for j in range(3))
            gv = g_ref[...]
            mn = ADAM_B1 * m_ref[...] + (1.0 - ADAM_B1) * gv
            vn = ADAM_B2 * v_ref[...] + (1.0 - ADAM_B2) * (gv * gv)
            m_hat = mn / (1.0 - ADAM_B1 ** ADAM_STEP)
            v_hat = vn / (1.0 - ADAM_B2 ** ADAM_STEP)
            d_ref[...] = -ADAM_LR * (m_hat / (jnp.sqrt(v_hat) + ADAM_EPS) + ADAM_WD * w_ref[...])
            mo_ref[...] = mn
            vo_ref[...] = vn

    outs = _pcall(body, name, [_sds(w.shape, F32) for w in ws] * 3)(*ws, *gs, *ms, *vs)
    return outs[:n], outs[n:2 * n], outs[2 * n:]


def _pad_rows(w, rows):
    return jnp.pad(w, ((0, rows - w.shape[0]), (0, 0)))


def _pad_lanes(v):
    return jnp.pad(v, (0, LANES - v.shape[0])).reshape(1, LANES)


BIG = ("w_in", "w_out", "ff_up", "ff_down")
BIG_AXIS = {"w_in": 1, "w_out": 0, "ff_up": 1, "ff_down": 0}
CONVW = ("ssd_conv_w", "ff_conv_w")
SMALL = ("norm1_g", "norm2_g", "ssd_conv_b", "ssd_dt_bias", "ssd_a_log", "ssd_d", "ssd_norm_g", "gm_vnorm_g", "gm_ws", "gm_bs", "gm_out_g", "ff_conv_b")
WEIGHTS = ("ada_w", "ada_b", "norm1_g", "norm2_g", "w_in", "ssd_conv_w", "ssd_conv_b", "ssd_dt_bias", "ssd_a_log", "ssd_d", "ssd_norm_g", "gm_vnorm_g", "gm_ws", "gm_bs", "gm_out_g", "w_out", "ff_up", "ff_conv_w", "ff_conv_b", "ff_down", "final_g")


def kernel(x, c, ada_w, ada_b, norm1_g, norm2_g, w_in, ssd_conv_w, ssd_conv_b, ssd_dt_bias, ssd_a_log, ssd_d, ssd_norm_g, gm_vnorm_g, gm_ws, gm_bs, gm_out_g, w_out, ff_up, ff_conv_w, ff_conv_b, ff_down, final_g, loss_target, m_ada_w, m_ada_b, m_norm1_g, m_norm2_g, m_w_in, m_ssd_conv_w, m_ssd_conv_b, m_ssd_dt_bias, m_ssd_a_log, m_ssd_d, m_ssd_norm_g, m_gm_vnorm_g, m_gm_ws, m_gm_bs, m_gm_out_g, m_w_out, m_ff_up, m_ff_conv_w, m_ff_conv_b, m_ff_down, m_final_g, v_ada_w, v_ada_b, v_norm1_g, v_norm2_g, v_w_in, v_ssd_conv_w, v_ssd_conv_b, v_ssd_dt_bias, v_ssd_a_log, v_ssd_d, v_ssd_norm_g, v_gm_vnorm_g, v_gm_ws, v_gm_bs, v_gm_out_g, v_w_out, v_ff_up, v_ff_conv_w, v_ff_conv_b, v_ff_down, v_final_g):
    given = dict(locals())
    weights = {n: given[n] for n in WEIGHTS}
    bl, s, d = x.shape
    nl = ada_w.shape[0]
    heads = d // HEAD_DIM
    cd = d + 2 * SSD_GROUPS * STATE
    f = ff_down.shape[1] * 4
    n_in = d + cd + heads + 2 * d
    n_proj = _round_up(3 * d + cd + LANES, 2 * LANES)
    tm = _divisors(s, 512)[-1]

    gathering = {}

    def start_gathers(l, names, token):
        for n in names:
            gathering[l, n] = _gather_weights_start(f"wg{l}_{n}", [weights[n][l]], token)
            token = gathering[l, n]["token"]
        return token

    pre, _ = _gather_two_level("pre_ag", [c] + [weights[n] for n in CONVW])
    c_all = pre[0].reshape(8 * bl, d)
    conv_full = {n: jnp.concatenate([p[k, 0] for k in range(4)], axis=-1) for n, p in zip(CONVW, pre[1:])}
    mods_all, ada_token = _ada_fwd("ada", bl, c_all, ada_w, ada_b)
    mods = mods_all.reshape(nl, bl, N_MOD, 1, d)

    def landed(l, n, after):
        (full,), token = _gather_weights_finish(f"wg{l}_{n}", gathering.pop((l, n)), after)
        if n == "w_in":
            return full, token
        return (full.reshape(-1, full.shape[2]) if BIG_AXIS[n] == 0 else jnp.concatenate([full[k] for k in range(4)], axis=1)), token

    issued = start_gathers(0, BIG, ada_token)
    r2 = lambda v: v.reshape(1, -1)
    n_gm = d // GM_HEAD
    fg = r2(final_g)

    def layer_params(l):
        return dict(
            norm1=r2(norm1_g[l]), norm2=r2(norm2_g[l]), conv_w8=_pad_rows(conv_full["ssd_conv_w"][l], SUBLANES), conv_b=r2(ssd_conv_b[l]),
            dtb=_pad_lanes(ssd_dt_bias[l]), alog=_pad_lanes(ssd_a_log[l]), dsk=r2(jnp.repeat(ssd_d[l], HEAD_DIM)), ng=r2(ssd_norm_g[l]),
            vg=r2(gm_vnorm_g[l]), ws=gm_ws[l], bs_t=jnp.pad(gm_bs[l].T, ((0, 0), (0, LANES - n_gm))), og=r2(gm_out_g[l]),
            ff_w8=_pad_rows(conv_full["ff_conv_w"][l], SUBLANES), ff_b=r2(ff_conv_b[l]))

    def mixer_args(p):
        return (p["conv_w8"], p["conv_b"], p["dtb"], p["alog"], p["dsk"], p["ng"], p["vg"], p["ws"], p["bs_t"], p["og"])

    def padded_w_in(full):
        ns = full.shape[2]

        def cols(a, b):
            return [full[k][:, max(a, k * ns) - k * ns:min(b, (k + 1) * ns) - k * ns] for k in range(4) if max(a, k * ns) < min(b, (k + 1) * ns)]

        parts = cols(0, d) + cols(d + cd + heads, n_in) + cols(d, d + cd) + cols(d + cd, d + cd + heads)
        return jnp.concatenate(parts + [jnp.zeros((d, n_proj - (3 * d + cd + heads)), BF16)], axis=1)

    saved, xcur, pending = [], x, None
    for l in range(nl):
        p, w = layer_params(l), {}
        wi, token = landed(l, "w_in", mods_all if l == 0 else pending[0])
        w["w_in"] = padded_w_in(wi)
        sh1, sc1, g1, sh2, sc2, g2 = (mods[l, :, k] for k in range(N_MOD))
        if l + 1 < nl:
            issued = start_gathers(l + 1, BIG, token + issued)
        sc1 = _after(sc1, issued)
        if pending is None:
            (h,) = _row_fwd(f"nm{l}_fwd", _nm_tile, [(xcur, d, 0, None)], [sc1, sh1], [p["norm1"]], [(d, BF16)], 0, tm)
            x1 = xcur
        else:
            rows = [(xcur, d, 0, None), (pending[0], d, 0, None)]
            x1, h = _row_fwd(f"rnm{l}a_fwd", _rnm_tile, rows, [pending[1], sc1, sh1], [p["norm1"]], [(d, F32), (d, BF16)], 0, tm)
        proj = _mm(f"win{l}_fwd", h, w["w_in"], F32)
        ycat, (xbc, hprev) = _mixer_fwd(f"mix{l}", proj, d, cd, *mixer_args(p))
        w["w_out"], _ = landed(l, "w_out", ycat)
        mix = _mm(f"wout{l}_fwd", ycat, w["w_out"], F32)
        x2, h2 = _row_fwd(f"rnm{l}b_fwd", _rnm_tile, [(x1, d, 0, None), (mix, d, 0, None)], [g1, sc2, sh2], [p["norm2"]], [(d, F32), (d, BF16)], 0, tm)
        w_up, _ = landed(l, "ff_up", h2)
        w["ff_up"] = _gate_value_blocks(f"ffup{l}_blocks", w_up, f)
        up = _mm(f"ffup{l}_fwd", h2, w["ff_up"], F32)
        act = _conv_fwd(f"ffact{l}_fwd", up, 0, f, p["ff_w8"], p["ff_b"], FF_CONV, True, BF16)
        w["ff_down"], _ = landed(l, "ff_down", act)
        down = _mm(f"ffdown{l}_fwd", act, w["ff_down"], F32)
        saved.append(dict(p=p, w=w, x_in=xcur, pending=pending, h=h, proj=proj, xbc=xbc, hprev=hprev, ycat=ycat, x1=x1, mix=mix, h2=h2, up=up, act=act))
        xcur, pending = x2, (down, g2)
    rows = [(xcur, d, 0, F32), (pending[0], d, 0, BF16), (loss_target, d, 0, None)]
    (loss_local,) = _row_fwd("final_fwd", _final_tile, rows, [pending[1]], [fg], [], 1, tm)
    loss = lax.psum(loss_local[0, 0], AXES)

    (dx2, ddown), (dg2,), (dfg,) = _row_bwd("final_bwd", _final_tile, rows, [pending[1]], [fg], [], 1, tm, [jnp.ones((1, 1), F32)])
    dmods, small_grads = [None] * nl, [None] * nl
    reducing = {}

    def start_reduce(l, n, g, token=None):
        width = g.shape[1] // 4
        by_chip = g.reshape(4, g.shape[0] // 4, g.shape[1]) if BIG_AXIS[n] == 0 else jnp.stack([g[:, k * width:(k + 1) * width] for k in range(4)])
        reducing[l, n] = _reduce_weights_start(f"wg{l}_{n}", [by_chip], token)
        return reducing[l, n]["token"]

    for l in reversed(range(nl)):
        sv = saved[l]
        p, w = sv["p"], sv["w"]
        sh1, sc1, g1, sh2, sc2, g2 = (mods[l, :, k] for k in range(N_MOD))
        dact, dw_down = _mm_bwd(f"ffdown{l}", sv["act"], w["ff_down"], ddown)
        ff_b = _after(p["ff_b"], start_reduce(l, "ff_down", dw_down))
        dup, dff_w8, dff_b = _conv_bwd(f"ffact{l}_bwd", sv["up"], 0, f, p["ff_w8"], ff_b, FF_CONV, True, dact, BF16)
        dh2, dw_up = _mm_bwd(f"ffup{l}", sv["h2"], w["ff_up"], dup)
        sc2 = _after(sc2, start_reduce(l, "ff_up", _gate_value_blocks(f"ffup{l}_columns", dw_up, f, inverse=True)))
        rows = [(sv["x1"], d, 0, F32), (sv["mix"], d, 0, BF16)]
        (dx1, dmix), (dg1, dsc2, dsh2), (dn2,) = _row_bwd(f"rnm{l}b_bwd", _rnm_tile, rows, [g1, sc2, sh2], [p["norm2"]], [(d, F32), (d, BF16)], 0, tm, [dx2, dh2])
        dycat, dw_out = _mm_bwd(f"wout{l}", sv["ycat"], w["w_out"], dmix)
        p_tied = dict(p, dtb=_after(p["dtb"], start_reduce(l, "w_out", dw_out)))
        dproj, (dw8, dcb, ddtb, dalog, ddsk, dng, dvg, dws, dbs_t, dog) = _mixer_bwd(f"mix{l}", sv["proj"], d, cd, *mixer_args(p_tied), sv["xbc"], sv["hprev"], dycat)
        dh, dw_in_p = _mm_bwd(f"win{l}", sv["h"], w["w_in"], dproj)
        dw_in = jnp.concatenate([dw_in_p[:, :d], dw_in_p[:, 3 * d:3 * d + cd], dw_in_p[:, 3 * d + cd:3 * d + cd + heads], dw_in_p[:, d:3 * d]], axis=1)
        if l > 0:
            sc1 = _after(sc1, start_reduce(l, "w_in", dw_in))
        if sv["pending"] is None:
            (dx2,), (dsc1, dsh1), (dn1,) = _row_bwd(f"nm{l}_bwd", _nm_res_tile, [(sv["x_in"], d, 0, F32)], [sc1, sh1], [p["norm1"]], [(d, F32), (d, BF16)], 0, tm, [dx1, dh])
        else:
            rows = [(sv["x_in"], d, 0, F32), (sv["pending"][0], d, 0, BF16)]
            (dx2, ddown), (dg2_prev, dsc1, dsh1), (dn1,) = _row_bwd(
                f"rnm{l}a_bwd", _rnm_tile, rows, [sv["pending"][1], sc1, sh1], [p["norm1"]], [(d, F32), (d, BF16)], 0, tm, [dx1, dh])
        dmods[l] = jnp.concatenate([dsh1, dsc1, dg1, dsh2, dsc2, dg2], axis=1).reshape(bl, N_MOD * d)
        if sv["pending"] is not None:
            dg2 = dg2_prev
        small_grads[l] = dict(
            norm1_g=dn1.reshape(d), norm2_g=dn2.reshape(d), ssd_conv_b=dcb.reshape(cd), ssd_dt_bias=ddtb[0, :heads], ssd_a_log=dalog[0, :heads],
            ssd_d=ddsk.reshape(heads, HEAD_DIM).sum(-1), ssd_norm_g=dng.reshape(d), gm_vnorm_g=dvg.reshape(d), gm_ws=dws, gm_bs=dbs_t[:, :n_gm].T,
            gm_out_g=dog.reshape(d), ff_conv_b=dff_b.reshape(f), ssd_conv_w=dw8[:SSD_CONV], ff_conv_w=dff_w8[:FF_CONV])
    grad_x = dx2

    g_ada_w, g_ada_b, ada_token = _ada_bwd("ada", bl, c_all, ada_w.shape[2], jnp.stack(dmods))
    small_names_r = SMALL + CONVW + ("final_g",)
    summed, small_token = _allreduce_small("small", [jnp.stack([small_grads[l][n] for l in range(nl)]) for n in SMALL + CONVW] + [dfg])
    travelling = start_reduce(0, "w_in", dw_in, ada_token + small_token)
    grad = {"ada_w": g_ada_w, "ada_b": g_ada_b}
    grad.update(zip(small_names_r, summed))
    chip = 2 * lax.axis_index("x") + lax.axis_index("y")
    for n in CONVW:
        width = weights[n].shape[-1]
        grad[n] = lax.dynamic_slice_in_dim(grad[n], chip * width, width, axis=2)

    delta, new_m, new_v = {}, {}, {}
    big_grads, done = {}, _after(summed[0], travelling)
    for n in ("ada_w",) + tuple(reversed(BIG)):
        if n != "ada_w":
            for l in reversed(range(nl)):
                (big_grads[l, n],), _ = _reduce_weights_finish(f"wg{l}_{n}", reducing.pop((l, n)), done)
            grad[n] = jnp.stack([big_grads[l, n] for l in range(nl)])
        delta[n], new_m[n], new_v[n] = _adamw("adam_" + n, weights[n], grad[n], given["m_" + n], given["v_" + n])
        done = delta[n]
    small_names = ("ada_b",) + SMALL + CONVW + ("final_g",)
    as2d = lambda t: t.reshape(1, -1) if t.ndim == 1 else t
    res = _adamw_small(
        "adam_small", [as2d(weights[n]) for n in small_names], [as2d(grad[n]) for n in small_names],
        [as2d(given["m_" + n]) for n in small_names], [as2d(given["v_" + n]) for n in small_names])
    for out, vals in zip((delta, new_m, new_v), res):
        for n, val in zip(small_names, vals):
            out[n] = val.reshape(weights[n].shape)
    grad["final_g"] = grad["final_g"].reshape(final_g.shape)

    return (loss, grad_x, *[grad[n] for n in WEIGHTS], *[delta[n] for n in WEIGHTS], *[new_m[n] for n in WEIGHTS], *[new_v[n] for n in WEIGHTS])
```

```python
import functools

import jax
import jax.numpy as jnp
from jax import lax
from jax.experimental import pallas as pl
from jax.experimental.pallas import tpu as pltpu

F32 = jnp.float32
BF16 = jnp.bfloat16
EPS = 1e-6
CHUNK = 128
HEAD_DIM = 64
STATE = 128
GM_HEAD = 128
SSD_GROUPS = 2
SSD_CONV = 4
FF_CONV = 3
N_MOD = 6
LANES = 128
SUBLANES = 8
SMALL_BF16_SIZE = 1 << 16
CONV_LANES = 256
CONV_ROWS = 32
V7X_VMEM_LIMIT = 48 * 1024 * 1024
MM_VMEM_BUDGET = 30 * 1024 * 1024
MM_STEP_MACS = 2048 * 1024 * 1024
MM_ACC_SECONDS_PER_VREG = 1.4e-9
V7X_MXU_FLOPS = 996e12
V7X_HBM_BYTES_PER_S = 3.3e12
V7X_STEP_SECONDS = 0.35e-6
ADAM_LR, ADAM_B1, ADAM_B2, ADAM_EPS, ADAM_WD, ADAM_STEP = 0.001, 0.9, 0.999, 1e-08, 0.01, 10
MESH = pl.DeviceIdType.MESH
AXES = ("x", "y", "c")


def _round_up(n, m):
    return (n + m - 1) // m * m


def _divisors(n, cap, mult=LANES):
    out = [t for t in range(mult, min(n, cap) + 1, mult) if n % t == 0]
    return out or [n]


def _pcall(body, name, out_shape, grid=(), in_specs=None, out_specs=None, scratch=(), sem=None, prefetch=0):
    params = pltpu.CompilerParams(dimension_semantics=sem, vmem_limit_bytes=V7X_VMEM_LIMIT)
    if prefetch:
        spec = pltpu.PrefetchScalarGridSpec(num_scalar_prefetch=prefetch, grid=grid, in_specs=in_specs, out_specs=out_specs, scratch_shapes=list(scratch))
        return pl.pallas_call(body, name=name, out_shape=out_shape, grid_spec=spec, compiler_params=params)
    if in_specs is None:
        return pl.pallas_call(body, name=name, out_shape=out_shape, compiler_params=params)
    return pl.pallas_call(body, name=name, out_shape=out_shape, grid=grid, in_specs=in_specs, out_specs=out_specs, scratch_shapes=list(scratch), compiler_params=params)


def _sds(shape, dtype):
    return jax.ShapeDtypeStruct(tuple(shape), dtype)


def _mm_tiles(m, n, k, a_bytes, b_bytes, o_bytes):
    best, best_key = None, None
    for tm in _divisors(m, 2048):
        for tn in _divisors(n, 2560):
            for tk in _divisors(k, 2560):
                vmem = 2 * (tm * tk * a_bytes + tk * tn * b_bytes + tm * tn * o_bytes) + tm * tn * 4
                if vmem > MM_VMEM_BUDGET or tm * tn * tk > MM_STEP_MACS:
                    continue
                ni, nj, nk = m // tm, n // tn, k // tk
                a_reads = 1 if nk == 1 else nj
                b_reads = 1 if (nk == 1 and nj == 1) else ni
                hbm = a_reads * m * k * a_bytes + b_reads * k * n * b_bytes + m * n * o_bytes
                t = max(2.0 * m * n * k / V7X_MXU_FLOPS, hbm / V7X_HBM_BYTES_PER_S) + ni * nj * nk * V7X_STEP_SECONDS
                if nk > 1:
                    t += ni * nj * nk * (tm * tn // (SUBLANES * LANES)) * MM_ACC_SECONDS_PER_VREG
                key = (-t, tm * tn * tk)
                if best_key is None or key > best_key:
                    best, best_key = (tm, tn, tk), key
    return best


def _matmul(name, a, b, mode, out_dtype):
    if mode == "nn":
        (m, k), n = a.shape, b.shape[1]
    elif mode == "nt":
        (m, k), n = a.shape, b.shape[0]
    else:
        (k, m), n = a.shape, b.shape[1]
    tm, tn, tk = _mm_tiles(m, n, k, a.dtype.itemsize, b.dtype.itemsize, jnp.dtype(out_dtype).itemsize)
    nk = k // tk
    if mode == "nn":
        a_spec = pl.BlockSpec((tm, tk), lambda i, j, kk: (i, kk))
        b_spec = pl.BlockSpec((tk, tn), lambda i, j, kk: (kk, j))
        dims = ((1,), (0,))
    elif mode == "nt":
        a_spec = pl.BlockSpec((tm, tk), lambda i, j, kk: (i, kk))
        b_spec = pl.BlockSpec((tn, tk), lambda i, j, kk: (j, kk))
        dims = ((1,), (1,))
    else:
        a_spec = pl.BlockSpec((tk, tm), lambda i, j, kk: (kk, i))
        b_spec = pl.BlockSpec((tk, tn), lambda i, j, kk: (kk, j))
        dims = ((0,), (0,))

    def body(a_ref, b_ref, o_ref, acc_ref):
        kk = pl.program_id(2)
        p = lax.dot_general(a_ref[...].astype(BF16), b_ref[...].astype(BF16), (dims, ((), ())), preferred_element_type=F32)
        if nk == 1:
            o_ref[...] = p.astype(o_ref.dtype)
        else:
            @pl.when(kk == 0)
            def _():
                acc_ref[...] = p

            @pl.when(kk > 0)
            def _():
                acc_ref[...] += p

            @pl.when(kk == nk - 1)
            def _():
                o_ref[...] = acc_ref[...].astype(o_ref.dtype)

    return _pcall(
        body, name, _sds((m, n), out_dtype), grid=(m // tm, n // tn, nk), in_specs=[a_spec, b_spec],
        out_specs=pl.BlockSpec((tm, tn), lambda i, j, kk: (i, j)), scratch=[pltpu.VMEM((tm, tn), F32)],
        sem=("parallel", "parallel", "arbitrary"),
    )(a, b)


def _mm(name, a, w, out_dtype):
    return _matmul(name, a.reshape(-1, a.shape[-1]), w, "nn", out_dtype).reshape(a.shape[:-1] + (w.shape[1],))


def _mm_bwd(name, a, w, dy):
    a2, dy2 = a.reshape(-1, a.shape[-1]), dy.reshape(-1, dy.shape[-1])
    return _matmul(name + "_dx", dy2, w, "nt", BF16).reshape(a.shape), _matmul(name + "_dw", a2, dy2, "tn", BF16)


def _dg(a, b, dims):
    return lax.dot_general(a.astype(BF16), b.astype(BF16), (dims, ((), ())), preferred_element_type=F32)


@jax.custom_vjp
def _dot_nn(a, b):
    return _dg(a, b, ((1,), (0,)))


_dot_nn.defvjp(lambda a, b: (_dot_nn(a, b), (a, b)), lambda r, d: (_dg(d, r[1], ((1,), (1,))), _dg(r[0], d, ((0,), (0,)))))


@jax.custom_vjp
def _dot_nt(a, b):
    return _dg(a, b, ((1,), (1,)))


_dot_nt.defvjp(lambda a, b: (_dot_nt(a, b), (a, b)), lambda r, d: (_dg(d, r[1], ((1,), (0,))), _dg(d, r[0], ((0,), (0,)))))


@jax.custom_vjp
def _dot_tn(a, b):
    return _dg(a, b, ((0,), (0,)))


_dot_tn.defvjp(lambda a, b: (_dot_tn(a, b), (a, b)), lambda r, d: (_dg(r[1], d, ((1,), (1,))), _dg(r[0], d, ((1,), (0,)))))


def _exact_dot(a, c, dims):
    hi = a.astype(BF16)
    r1 = a - hi.astype(F32)
    mid = r1.astype(BF16)
    lo = (r1 - mid.astype(F32)).astype(BF16)
    cb = c.astype(BF16)
    f = lambda t: lax.dot_general(t, cb, (dims, ((), ())), preferred_element_type=F32)
    return f(hi) + f(mid) + f(lo)


@jax.custom_vjp
def _sel_right(a, c):
    return _exact_dot(a, c, ((1,), (0,)))


_sel_right.defvjp(lambda a, c: (_sel_right(a, c), c), lambda c, d: (_exact_dot(d, c, ((1,), (1,))), jnp.zeros_like(c)))


def _exact_dot_left(c, a, dims):
    hi = a.astype(BF16)
    r1 = a - hi.astype(F32)
    mid = r1.astype(BF16)
    lo = (r1 - mid.astype(F32)).astype(BF16)
    cb = c.astype(BF16)
    f = lambda t: lax.dot_general(cb, t, (dims, ((), ())), preferred_element_type=F32)
    return f(hi) + f(mid) + f(lo)


@jax.custom_vjp
def _sel_left(c, a):
    return _exact_dot_left(c, a, ((1,), (0,)))


_sel_left.defvjp(lambda c, a: (_sel_left(c, a), c), lambda c, d: (jnp.zeros_like(c), _exact_dot_left(c, d, ((0,), (0,)))))


def _sigmoid(x):
    return 1.0 / (1.0 + jnp.exp(-x))


def _rms(x, g):
    return x * lax.rsqrt(jnp.mean(x * x, axis=-1, keepdims=True) + EPS) * g


def _gelu(x):
    return 0.5 * x * (1.0 + lax.erf(x * (2.0 ** -0.5)))


def _causal(n):
    return lax.broadcasted_iota(jnp.int32, (n, n), 0) >= lax.broadcasted_iota(jnp.int32, (n, n), 1)


def _row_in_specs(rows, bparams, gparams, tm):
    specs = [pl.BlockSpec((1, tm, w), lambda b, i, cb=cb: (b, i, cb)) for (_, w, cb, _) in rows]
    specs += [pl.BlockSpec((1, 1, p.shape[-1]), lambda b, i: (b, 0, 0)) for p in bparams]
    specs += [pl.BlockSpec(p.shape, lambda b, i, n=p.ndim: (0,) * n) for p in gparams]
    return specs


def _row_vals(refs, n_rows, n_b, n_g):
    vals = [r[0].astype(F32) for r in refs[:n_rows]]
    vals += [r[0].astype(F32) for r in refs[n_rows:n_rows + n_b]]
    vals += [r[...].astype(F32) for r in refs[n_rows + n_b:n_rows + n_b + n_g]]
    return vals


def _row_fwd(name, tile, rows, bparams, gparams, outs, n_sum, tm):
    bl, s = rows[0][0].shape[:2]
    n_in = len(rows) + len(bparams) + len(gparams)

    def body(*refs):
        first = (pl.program_id(0) == 0) & (pl.program_id(1) == 0)
        res = tile(*_row_vals(refs, len(rows), len(bparams), len(gparams)))
        o_refs = refs[n_in:]
        for k in range(len(outs)):
            o_refs[k][0] = res[k].astype(o_refs[k].dtype)
        for k in range(n_sum):
            o_ref, val = o_refs[len(outs) + k], res[len(outs) + k]

            @pl.when(first)
            def _(o_ref=o_ref, val=val):
                o_ref[...] = val

            @pl.when(jnp.logical_not(first))
            def _(o_ref=o_ref, val=val):
                o_ref[...] += val

    out_shape = [_sds((bl, s, w), dt) for (w, dt) in outs] + [_sds((1, 1), F32)] * n_sum
    out_specs = [pl.BlockSpec((1, tm, w), lambda b, i: (b, i, 0)) for (w, _) in outs] + [pl.BlockSpec((1, 1), lambda b, i: (0, 0))] * n_sum
    return _pcall(
        body, name, out_shape, grid=(bl, s // tm), in_specs=_row_in_specs(rows, bparams, gparams, tm), out_specs=out_specs,
        sem=("arbitrary", "arbitrary"),
    )(*[r[0] for r in rows], *bparams, *gparams)


def _row_bwd(name, tile, rows, bparams, gparams, outs, n_sum, tm, cts):
    bl, s = rows[0][0].shape[:2]
    n_r, n_b, n_g = len(rows), len(bparams), len(gparams)
    n_in = n_r + n_b + n_g
    n_ct = len(outs) + n_sum
    grad_rows = [k for k in range(n_r) if rows[k][3]]

    def body(*refs):
        b, i = pl.program_id(0), pl.program_id(1)
        vals = _row_vals(refs, n_r, n_b, n_g)
        ct_refs = refs[n_in:n_in + n_ct]
        ct = [r[0].astype(F32) for r in ct_refs[:len(outs)]] + [r[...] for r in ct_refs[len(outs):]]
        _, vjp = jax.vjp(tile, *vals)
        grads = vjp(tuple(ct))
        o_refs = refs[n_in + n_ct:]
        for j, k in enumerate(grad_rows):
            o_refs[j][0] = grads[k].astype(o_refs[j].dtype)
        for k in range(n_b):
            o_ref, val = o_refs[len(grad_rows) + k], grads[n_r + k]

            @pl.when(i == 0)
            def _(o_ref=o_ref, val=val):
                o_ref[0] = val

            @pl.when(i > 0)
            def _(o_ref=o_ref, val=val):
                o_ref[0] += val

        first = (b == 0) & (i == 0)
        for k in range(n_g):
            o_ref, val = o_refs[len(grad_rows) + n_b + k], grads[n_r + n_b + k]

            @pl.when(first)
            def _(o_ref=o_ref, val=val):
                o_ref[...] = val

            @pl.when(jnp.logical_not(first))
            def _(o_ref=o_ref, val=val):
                o_ref[...] += val

    in_specs = _row_in_specs(rows, bparams, gparams, tm)
    in_specs += [pl.BlockSpec((1, tm, w), lambda b, i: (b, i, 0)) for (w, _) in outs] + [pl.BlockSpec((1, 1), lambda b, i: (0, 0))] * n_sum
    out_shape = [_sds((bl, s, rows[k][1]), rows[k][3]) for k in grad_rows]
    out_shape += [_sds(p.shape, F32) for p in bparams] + [_sds(p.shape, F32) for p in gparams]
    out_specs = [pl.BlockSpec((1, tm, rows[k][1]), lambda b, i: (b, i, 0)) for k in grad_rows]
    out_specs += [pl.BlockSpec((1, 1, p.shape[-1]), lambda b, i: (b, 0, 0)) for p in bparams]
    out_specs += [pl.BlockSpec(p.shape, lambda b, i, n=p.ndim: (0,) * n) for p in gparams]
    res = _pcall(
        body, name, out_shape, grid=(bl, s // tm), in_specs=in_specs, out_specs=out_specs, sem=("arbitrary", "arbitrary"),
    )(*[r[0] for r in rows], *bparams, *gparams, *cts)
    return res[:len(grad_rows)], res[len(grad_rows):len(grad_rows) + n_b], res[len(grad_rows) + n_b:]


def _nm_tile(x, sc, sh, g):
    return (_rms(x, g) * (1.0 + sc) + sh,)


def _nm_res_tile(x, sc, sh, g):
    return x, _rms(x, g) * (1.0 + sc) + sh


def _rnm_tile(x, o, gate, sc, sh, g):
    xn = x + gate * o
    return xn, _rms(xn, g) * (1.0 + sc) + sh


def _final_tile(x, o, tgt, gate, g):
    e = _rms(x + gate * o, g) - tgt
    return (0.5 * jnp.sum(jnp.mean(e * e, axis=-1, keepdims=True), axis=0, keepdims=True),)


def _gm_tile(u_in, v_in, vg, ws, bs_t, og):
    d = u_in.shape[1]
    u, vn = _gelu(u_in), _rms(_gelu(v_in), vg)
    causal = _causal(CHUNK)
    lane = lax.broadcasted_iota(jnp.int32, (1, LANES), 1)
    parts = []
    for h in range(d // GM_HEAD):
        bias = jnp.sum(bs_t * (lane == h).astype(F32), axis=1, keepdims=True)
        parts.append(_dot_nn(jnp.where(causal, ws[h], 0.0), vn[:, h * GM_HEAD:(h + 1) * GM_HEAD]) + bias)
    return (_rms(u * jnp.concatenate(parts, axis=1), og),)


def _conv_window(ref, r0, rows, c0, tc, seq, before, after):
    parts = []
    if before:
        p0 = pl.multiple_of(jnp.maximum(r0 - SUBLANES, 0), SUBLANES)
        parts.append(jnp.where(r0 > 0, ref[0, pl.ds(p0, SUBLANES), pl.ds(c0, tc)].astype(F32), 0.0))
    parts.append(ref[0, pl.ds(r0, rows), pl.ds(c0, tc)].astype(F32))
    if after:
        n0 = pl.multiple_of(jnp.minimum(r0 + rows, seq - SUBLANES), SUBLANES)
        parts.append(jnp.where(r0 + rows < seq, ref[0, pl.ds(n0, SUBLANES), pl.ds(c0, tc)].astype(F32), 0.0))
    return jnp.concatenate(parts, axis=0) if len(parts) > 1 else parts[0]


def _conv_taps_pre(xe, w, b, taps, rows):
    pre = xe[SUBLANES:SUBLANES + rows] * w[taps - 1] + b
    for j in range(1, taps):
        pre = pre + pltpu.roll(xe, j, 0)[SUBLANES:SUBLANES + rows] * w[taps - 1 - j]
    return pre


def _conv_fwd(name, src, col0, chans, w8, b, taps, gated, out_dtype):
    bl, s, _ = src.shape
    tc = CONV_LANES
    xw = 2 * tc if gated else tc

    def body(x_ref, w_ref, b_ref, o_ref):
        w = [w_ref[k:k + 1, :] for k in range(taps)]
        bias = b_ref[...]

        def step(c, carry):
            r0 = pl.multiple_of(c * CONV_ROWS, CONV_ROWS)
            xe = _conv_window(x_ref, r0, CONV_ROWS, 0, tc, s, True, False)
            pre = _conv_taps_pre(xe, w, bias, taps, CONV_ROWS)
            y = pre * _sigmoid(pre)
            if gated:
                y = y * x_ref[0, pl.ds(r0, CONV_ROWS), pl.ds(tc, tc)]
            o_ref[0, pl.ds(r0, CONV_ROWS), :] = y.astype(o_ref.dtype)
            return carry

        lax.fori_loop(0, s // CONV_ROWS, step, 0, unroll=4)

    first = 0 if gated else col0 // tc
    return _pcall(
        body, name, _sds((bl, s, chans), out_dtype), grid=(chans // tc, bl),
        in_specs=[pl.BlockSpec((1, s, xw), lambda j, bb: (bb, 0, first + j)), pl.BlockSpec((SUBLANES, tc), lambda j, bb: (0, j)), pl.BlockSpec((1, tc), lambda j, bb: (0, j))],
        out_specs=pl.BlockSpec((1, s, tc), lambda j, bb: (bb, 0, j)), sem=("parallel", "arbitrary"),
    )(src, w8, b)


def _conv_bwd(name, src, col0, chans, w8, b, taps, gated, dy, dx_dtype):
    bl, s, _ = src.shape
    tc = CONV_LANES
    xw = 2 * tc if gated else tc
    ext = CONV_ROWS + SUBLANES

    def body(x_ref, w_ref, b_ref, dy_ref, dx_ref, dw_ref, db_ref):
        bb = pl.program_id(1)
        w = [w_ref[k:k + 1, :] for k in range(taps)]
        bias = b_ref[...]

        def fold(v):
            acc = v[0:SUBLANES]
            for i in range(1, CONV_ROWS // SUBLANES):
                acc = acc + v[i * SUBLANES:(i + 1) * SUBLANES]
            return acc

        def step(c, carry):
            r0 = pl.multiple_of(c * CONV_ROWS, CONV_ROWS)
            xe = _conv_window(x_ref, r0, CONV_ROWS, 0, tc, s, True, True)
            pre = _conv_taps_pre(xe, w, bias, taps, ext)
            sig = _sigmoid(pre)
            d = _conv_window(dy_ref, r0, CONV_ROWS, 0, tc, s, False, True)
            dsil = d * _conv_window(x_ref, r0, CONV_ROWS, tc, tc, s, False, True) if gated else d
            dpre = dsil * sig * (1.0 + pre * (1.0 - sig))
            dx = dpre[:CONV_ROWS] * w[taps - 1]
            for j in range(1, taps):
                dx = dx + pltpu.roll(dpre, ext - j, 0)[:CONV_ROWS] * w[taps - 1 - j]
            dx_ref[0, pl.ds(r0, CONV_ROWS), pl.ds(0, tc)] = dx.astype(dx_ref.dtype)
            if gated:
                dx_ref[0, pl.ds(r0, CONV_ROWS), pl.ds(tc, tc)] = (d[:CONV_ROWS] * (pre * sig)[:CONV_ROWS]).astype(dx_ref.dtype)
            here = dpre[:CONV_ROWS]
            sums = [fold(here * (pltpu.roll(xe, taps - 1 - k, 0) if k < taps - 1 else xe)[SUBLANES:ext]) + carry[k] for k in range(taps)]
            return tuple(sums) + (fold(here) + carry[taps],)

        zero = jnp.zeros((SUBLANES, tc), F32)
        sums = lax.fori_loop(0, s // CONV_ROWS, step, (zero,) * (taps + 1), unroll=2)
        rows = [jnp.sum(t, axis=0, keepdims=True) for t in sums]
        dw = jnp.concatenate(rows[:taps] + [jnp.zeros_like(rows[0])] * (SUBLANES - taps), axis=0)

        @pl.when(bb == 0)
        def _():
            dw_ref[...] = dw
            db_ref[...] = rows[taps]

        @pl.when(bb > 0)
        def _():
            dw_ref[...] += dw
            db_ref[...] += rows[taps]

    first = 0 if gated else col0 // tc
    out_shape = [_sds((bl, s, chans * (2 if gated else 1)), dx_dtype), _sds((SUBLANES, chans), F32), _sds((1, chans), F32)]
    return _pcall(
        body, name, out_shape, grid=(chans // tc, bl),
        in_specs=[
            pl.BlockSpec((1, s, xw), lambda j, bb: (bb, 0, first + j)), pl.BlockSpec((SUBLANES, tc), lambda j, bb: (0, j)),
            pl.BlockSpec((1, tc), lambda j, bb: (0, j)), pl.BlockSpec((1, s, tc), lambda j, bb: (bb, 0, j)),
        ],
        out_specs=[pl.BlockSpec((1, s, xw), lambda j, bb: (bb, 0, j)), pl.BlockSpec((SUBLANES, tc), lambda j, bb: (0, j)), pl.BlockSpec((1, tc), lambda j, bb: (0, j))],
        sem=("parallel", "arbitrary"),
    )(src, w8, b, dy)


def _gate_value_blocks(name, w, f, inverse=False):
    nb = f // CONV_LANES
    src = (lambda j: (0, 2 * (j % nb) + j // nb)) if inverse else (lambda j: (0, (j % 2) * nb + j // 2))

    def body(x_ref, o_ref):
        o_ref[...] = x_ref[...]

    return _pcall(
        body, name, _sds(w.shape, w.dtype), grid=(2 * nb,), in_specs=[pl.BlockSpec((w.shape[0], CONV_LANES), src)],
        out_specs=pl.BlockSpec((w.shape[0], CONV_LANES), lambda j: (0, j)), sem=("parallel",),
    )(w)


def _ssd_step(states, xs, bm, cm, dtr, z, dtb, alog, dsk, ng):
    gw = xs.shape[1] // SSD_GROUPS
    hpg, npair = gw // HEAD_DIM, gw // LANES
    dt = jax.nn.softplus(dtr + dtb)
    da = dt * (-jnp.exp(alog))
    causal = _causal(CHUNK)
    acs = _sel_left(causal.astype(F32), da)
    acs_t = acs.T
    lane = lax.broadcasted_iota(jnp.int32, (1, LANES), 1)
    sub = lax.broadcasted_iota(jnp.int32, (LANES, 1), 0)
    last = lax.broadcasted_iota(jnp.int32, (CHUNK, gw), 0) == CHUNK - 1
    outs, new_states = [], []
    for g in range(SSD_GROUPS):
        xg, zg = xs[:, g * gw:(g + 1) * gw], z[:, g * gw:(g + 1) * gw]
        bg, cg = bm[:, g * STATE:(g + 1) * STATE], cm[:, g * STATE:(g + 1) * STATE]
        head_of_col = g * hpg + lax.broadcasted_iota(jnp.int32, (LANES, gw), 1) // HEAD_DIM
        expand = (lax.broadcasted_iota(jnp.int32, (LANES, gw), 0) == head_of_col).astype(F32)
        dt_e, acs_e = _sel_right(dt, expand), _sel_right(acs, expand)
        alast_e = jnp.sum(jnp.where(last, acs_e, 0.0), axis=0, keepdims=True)
        xc = xg * dt_e
        xc_st = xc * jnp.exp(alast_e - acs_e)
        decay_out, chunk_decay = jnp.exp(acs_e), jnp.exp(alast_e)
        cb = _dot_nt(cg, bg)
        ys = []
        for p in range(npair):
            sl = slice(p * LANES, (p + 1) * LANES)
            state = states[g * npair + p]
            y = _dot_nn(cg, state) * decay_out[:, sl]
            for q in range(2):
                head = g * hpg + 2 * p + q
                col = jnp.sum(acs * (lane == head).astype(F32), axis=1, keepdims=True)
                row = jnp.sum(acs_t * (sub == head).astype(F32), axis=0, keepdims=True)
                decay = jnp.where(causal, jnp.exp(jnp.where(causal, col - row, 0.0)), 0.0)
                y = y + _dot_nn(cb * decay, xc[:, sl] * ((lane // HEAD_DIM) == q).astype(F32))
            ys.append(y)
            new_states.append(state * chunk_decay[:, sl] + _dot_tn(bg, xc_st[:, sl]))
        y = jnp.concatenate(ys, axis=1) + dsk[:, g * gw:(g + 1) * gw] * xg
        outs.append(_rms(y * (zg * _sigmoid(zg)), ng[:, g * gw:(g + 1) * gw]))
    return tuple(new_states), jnp.concatenate(outs, axis=1)


def _ssd_scan_specs(d, dt_col, nc, rev):
    ci = (lambda i: nc - 1 - i) if rev else (lambda i: i)
    bc = SSD_GROUPS * STATE
    return [
        pl.BlockSpec((1, CHUNK, d), lambda b, i: (b, ci(i), 0)),
        pl.BlockSpec((1, CHUNK, bc), lambda b, i: (b, ci(i), d // bc)),
        pl.BlockSpec((1, CHUNK, bc), lambda b, i: (b, ci(i), d // bc + 1)),
        pl.BlockSpec((1, CHUNK, LANES), lambda b, i: (b, ci(i), dt_col // LANES)),
        pl.BlockSpec((1, CHUNK, d), lambda b, i: (b, ci(i), 0)),
        pl.BlockSpec((1, LANES), lambda b, i: (0, 0)), pl.BlockSpec((1, LANES), lambda b, i: (0, 0)),
        pl.BlockSpec((1, d), lambda b, i: (0, 0)), pl.BlockSpec((1, d), lambda b, i: (0, 0)),
    ]


def _ssd_scan_fwd(name, xbc, proj, dt_col, dtb, alog, dsk, ng):
    bl, s, _ = xbc.shape
    d = dsk.shape[1]
    nc, nst = s // CHUNK, d // LANES

    def body(xs_ref, bm_ref, cm_ref, dt_ref, z_ref, dtb_ref, alog_ref, dsk_ref, ng_ref, y_ref, hp_ref, st_ref):
        @pl.when(pl.program_id(1) == 0)
        def _():
            st_ref[...] = jnp.zeros_like(st_ref)

        states = tuple(st_ref[p] for p in range(nst))
        hp_ref[0, 0] = st_ref[...]
        new_states, yn = _ssd_step(states, xs_ref[0], bm_ref[0], cm_ref[0], dt_ref[0], z_ref[0], dtb_ref[...], alog_ref[...], dsk_ref[...], ng_ref[...])
        for p in range(nst):
            st_ref[p] = new_states[p]
        y_ref[0] = yn.astype(y_ref.dtype)

    return _pcall(
        body, name, [_sds((bl, s, d), BF16), _sds((bl, nc, nst, STATE, LANES), F32)], grid=(bl, nc), in_specs=_ssd_scan_specs(d, dt_col, nc, False),
        out_specs=[pl.BlockSpec((1, CHUNK, d), lambda b, i: (b, i, 0)), pl.BlockSpec((1, 1, nst, STATE, LANES), lambda b, i: (b, i, 0, 0, 0))],
        scratch=[pltpu.VMEM((nst, STATE, LANES), F32)], sem=("arbitrary", "arbitrary"),
    )(xbc, xbc, xbc, proj, proj, dtb, alog, dsk, ng)


def _ssd_scan_bwd(name, xbc, proj, dt_col, dtb, alog, dsk, ng, hprev, dy):
    bl, s, _ = xbc.shape
    d = dsk.shape[1]
    nc, nst, bc = s // CHUNK, d // LANES, SSD_GROUPS * STATE

    def body(xs_ref, bm_ref, cm_ref, dt_ref, z_ref, dtb_ref, alog_ref, dsk_ref, ng_ref, hp_ref, dy_ref,
             dxs_ref, dbm_ref, dcm_ref, ddt_ref, dz_ref, ddtb_ref, dalog_ref, ddsk_ref, dng_ref, dst_ref):
        b, i = pl.program_id(0), pl.program_id(1)

        @pl.when(i == 0)
        def _():
            dst_ref[...] = jnp.zeros_like(dst_ref)

        states = tuple(hp_ref[0, 0, p] for p in range(nst))
        _, vjp = jax.vjp(_ssd_step, states, xs_ref[0], bm_ref[0], cm_ref[0], dt_ref[0], z_ref[0], dtb_ref[...], alog_ref[...], dsk_ref[...], ng_ref[...])
        d_states, dxs, dbm, dcm, ddt, dz, ddtb, dalog, ddsk, dng = vjp((tuple(dst_ref[p] for p in range(nst)), dy_ref[0].astype(F32)))
        for p in range(nst):
            dst_ref[p] = d_states[p]
        dxs_ref[0], dbm_ref[0], dcm_ref[0], ddt_ref[0] = dxs, dbm, dcm, ddt
        dz_ref[0] = dz.astype(dz_ref.dtype)
        first = (b == 0) & (i == 0)
        for o_ref, val in ((ddtb_ref, ddtb), (dalog_ref, dalog), (ddsk_ref, ddsk), (dng_ref, dng)):
            @pl.when(first)
            def _(o_ref=o_ref, val=val):
                o_ref[...] = val

            @pl.when(jnp.logical_not(first))
            def _(o_ref=o_ref, val=val):
                o_ref[...] += val

    rc = lambda i: nc - 1 - i
    in_specs = _ssd_scan_specs(d, dt_col, nc, True) + [
        pl.BlockSpec((1, 1, nst, STATE, LANES), lambda b, i: (b, rc(i), 0, 0, 0)), pl.BlockSpec((1, CHUNK, d), lambda b, i: (b, rc(i), 0)),
    ]
    out_shape = [
        _sds((bl, s, d), F32), _sds((bl, s, bc), F32), _sds((bl, s, bc), F32), _sds((bl, s, LANES), F32), _sds((bl, s, d), BF16),
        _sds((1, LANES), F32), _sds((1, LANES), F32), _sds((1, d), F32), _sds((1, d), F32),
    ]
    row = lambda w: pl.BlockSpec((1, CHUNK, w), lambda b, i: (b, rc(i), 0))
    whole = lambda w: pl.BlockSpec((1, w), lambda b, i: (0, 0))
    out_specs = [row(d), row(bc), row(bc), row(LANES), row(d), whole(LANES), whole(LANES), whole(d), whole(d)]
    return _pcall(
        body, name, out_shape, grid=(bl, nc), in_specs=in_specs, out_specs=out_specs,
        scratch=[pltpu.VMEM((nst, STATE, LANES), F32)], sem=("arbitrary", "arbitrary"),
    )(xbc, xbc, xbc, proj, proj, dtb, alog, dsk, ng, hprev, dy)


def _mixer_fwd(name, proj, d, cd, conv_w8, conv_b, dtb, alog, dsk, ng, vg, ws, bs_t, og):
    xbc = _conv_fwd(name + "_conv_fwd", proj, 3 * d, cd, conv_w8, conv_b, SSD_CONV, False, F32)
    y, hprev = _ssd_scan_fwd(name + "_ssd_fwd", xbc, proj, 3 * d + cd, dtb, alog, dsk, ng)
    rows = [(proj, d, 1, BF16), (proj, d, 2, BF16)]
    (g_out,) = _row_fwd(name + "_gm_fwd", _gm_tile, rows, [], [vg, ws, bs_t, og], [(d, BF16)], 0, CHUNK)
    return jnp.concatenate([y, g_out], axis=-1), (xbc, hprev)


def _mixer_bwd(name, proj, d, cd, conv_w8, conv_b, dtb, alog, dsk, ng, vg, ws, bs_t, og, xbc, hprev, dycat):
    bl, s, n_proj = proj.shape
    dt_col = 3 * d + cd
    dy, dg_out = dycat[..., :d], dycat[..., d:]
    dxs, dbm, dcm, ddt, dz, ddtb, dalog, ddsk, dng = _ssd_scan_bwd(name + "_ssd_bwd", xbc, proj, dt_col, dtb, alog, dsk, ng, hprev, dy)
    dxbc_act = jnp.concatenate([dxs, dbm, dcm], axis=-1)
    dxbc, dw8, dcb = _conv_bwd(name + "_conv_bwd", proj, 3 * d, cd, conv_w8, conv_b, SSD_CONV, False, dxbc_act, BF16)
    rows = [(proj, d, 1, BF16), (proj, d, 2, BF16)]
    (du, dv), _, (dvg, dws, dbs_t, dog) = _row_bwd(name + "_gm_bwd", _gm_tile, rows, [], [vg, ws, bs_t, og], [(d, BF16)], 0, CHUNK, [dg_out])
    pad = jnp.zeros((bl, s, n_proj - dt_col - LANES), BF16)
    dproj = jnp.concatenate([dz, du, dv, dxbc, ddt.astype(BF16), pad], axis=-1)
    return dproj, (dw8, dcb, ddtb, dalog, ddsk, dng, dvg, dws, dbs_t, dog)


def _position():
    return lax.axis_index("x"), lax.axis_index("y"), lax.axis_index("c")


def _at(ref, idx):
    return ref.at[idx] if len(idx) else ref


def _exchange(name, inputs, out_shapes, plan, inplace=False, after=None):
    if inplace:
        out_shapes = [(a.shape, a.dtype) for a in inputs]
    n_in, n_out = len(inputs), len(out_shapes)
    n_after = 0 if after is None else 1
    n_copy = len(plan(0, 0, 0))

    def body(*refs):
        in_refs, out_refs, token = refs[:n_in], refs[n_in + n_after:n_in + n_after + n_out], refs[n_in + n_after + n_out]
        send_sems, recv_sems = refs[n_in + n_after + n_out + 1:]
        token[...] = jnp.zeros_like(token)
        x, y, c = _position()
        copies = plan(x, y, c)

        def copy(k, src, dst, peer):
            return pltpu.make_async_remote_copy(src_ref=src, dst_ref=dst, send_sem=send_sems.at[k], recv_sem=recv_sems.at[k], device_id=peer, device_id_type=MESH)

        src_refs = out_refs if inplace else in_refs
        sends = [copy(k, _at(src_refs[sa], si), _at(out_refs[da], di), peer) for k, (sa, si, da, di, peer, _) in enumerate(copies)]
        for cp in sends:
            cp.start()
        for k, (sa, si, da, _, peer, li) in enumerate(copies):
            copy(k, _at(src_refs[sa], si), _at(out_refs[da], li), peer).wait_recv()
        for cp in sends:
            cp.wait_send()

    any_spec = pl.BlockSpec(memory_space=pl.ANY)
    outs = pl.pallas_call(
        body, name=name, out_shape=[_sds(s, dt) for s, dt in out_shapes] + [_sds((SUBLANES, LANES), F32)], in_specs=[any_spec] * (n_in + n_after),
        out_specs=[any_spec] * n_out + [pl.BlockSpec(memory_space=pltpu.VMEM)],
        scratch_shapes=[pltpu.SemaphoreType.DMA((n_copy,)), pltpu.SemaphoreType.DMA((n_copy,))],
        input_output_aliases={i: i for i in range(n_in)} if inplace else {},
    )(*inputs, *([] if after is None else [after]))
    return list(outs[:n_out]), outs[n_out]


def _exchange_start(name, inputs, out_shapes, plan):
    n_in, n_out = len(inputs), len(out_shapes)
    n_copy = len(plan(0, 0, 0))

    def body(*refs):
        in_refs, land_refs = refs[:n_in], refs[n_in:n_in + n_out]
        send_sems, recv_sems = refs[n_in + n_out:n_in + n_out + 2]
        token = refs[-1]
        x, y, c = _position()
        for k, (sa, si, da, di, peer, _) in enumerate(plan(x, y, c)):
            pltpu.make_async_remote_copy(
                src_ref=_at(in_refs[sa], si), dst_ref=_at(land_refs[da], di), send_sem=send_sems.at[k], recv_sem=recv_sems.at[k],
                device_id=peer, device_id_type=MESH).start()
        token[...] = jnp.zeros_like(token)

    hbm, sem = pl.BlockSpec(memory_space=pltpu.HBM), pl.BlockSpec(memory_space=pltpu.SEMAPHORE)
    lands = [lax.empty(s, dt) for s, dt in out_shapes]
    args = [pltpu.with_memory_space_constraint(a, pltpu.HBM) for a in list(inputs) + lands]
    outs = pl.pallas_call(
        body, name=name,
        out_shape=(pltpu.SemaphoreType.DMA((n_copy,)), pltpu.SemaphoreType.DMA((n_copy,)), *[pltpu.HBM(a.shape, a.dtype) for a in args], _sds((SUBLANES, LANES), F32)),
        in_specs=[hbm] * (n_in + n_out), out_specs=(sem, sem, *[hbm] * (n_in + n_out), pl.BlockSpec(memory_space=pltpu.VMEM)),
        input_output_aliases={i: 2 + i for i in range(n_in + n_out)},
        compiler_params=pltpu.CompilerParams(has_side_effects=pltpu.SideEffectType.DATAFLOW_SIDE_EFFECTING),
    )(*args)
    return dict(name=name, plan=plan, sems=outs[:2], ins=list(outs[2:2 + n_in]), lands=list(outs[2 + n_in:2 + n_in + n_out]), token=outs[-1])


def _exchange_wait(started, after):
    plan, n_in, n_out = started["plan"], len(started["ins"]), len(started["lands"])

    def body(*refs):
        in_refs, land_refs = refs[:n_in], refs[n_in:n_in + n_out]
        send_sems, recv_sems = refs[n_in + n_out:n_in + n_out + 2]
        token = refs[-1]
        x, y, c = _position()
        for k, (sa, si, da, _, peer, li) in enumerate(plan(x, y, c)):
            cp = pltpu.make_async_remote_copy(
                src_ref=_at(in_refs[sa], si), dst_ref=_at(land_refs[da], li), send_sem=send_sems.at[k], recv_sem=recv_sems.at[k],
                device_id=peer, device_id_type=MESH)
            cp.wait_send()
            cp.wait_recv()
        token[...] = jnp.zeros_like(token)

    hbm, sem = pl.BlockSpec(memory_space=pltpu.HBM), pl.BlockSpec(memory_space=pltpu.SEMAPHORE)
    bufs = started["ins"] + started["lands"]
    outs = pl.pallas_call(
        body, name=started["name"] + "_wait", out_shape=(*[pltpu.HBM(a.shape, a.dtype) for a in bufs], _sds((SUBLANES, LANES), F32)),
        in_specs=[hbm] * len(bufs) + [sem, sem, pl.BlockSpec(memory_space=pl.ANY)], out_specs=(*[hbm] * len(bufs), pl.BlockSpec(memory_space=pltpu.VMEM)),
        input_output_aliases={i: i for i in range(len(bufs))},
        compiler_params=pltpu.CompilerParams(has_side_effects=pltpu.SideEffectType.DATAFLOW_SIDE_EFFECTING),
    )(*bufs, *started["sems"], after)
    return list(outs[:n_in]), list(outs[n_in:n_in + n_out]), outs[-1]


def _after(value, token):
    return value + token[0, 0].astype(value.dtype)


def _chip_peers(x, y, c):
    return [(1 - x, y, c), (x, 1 - y, c), (1 - x, 1 - y, c)]


def _chip_of(p):
    return 2 * p[0] + p[1]


def _set_slot(slots, me, blk):
    return lax.dynamic_update_slice(slots, blk[None], (me,) + (0,) * blk.ndim)


def _by_core(c, mine, other, axis):
    return jnp.where(c == 0, jnp.stack([mine, other], axis), jnp.stack([other, mine], axis))


def _plan_gather_chips(n):
    def plan(x, y, c):
        return [(a, (), a, (2 * x + y,), p, (_chip_of(p),)) for a in range(n) for p in _chip_peers(x, y, c)]

    return plan


def _gather_chips(name, blocks):
    recv, token = _exchange(name, blocks, [((4,) + b.shape, b.dtype) for b in blocks], _plan_gather_chips(len(blocks)))
    x, y, _ = _position()
    return [_set_slot(r, 2 * x + y, b) for r, b in zip(recv, blocks)], token


def _gather_pass_cores(name, blocks, from_chips):
    n = len(blocks)

    def plan_cores(x, y, c):
        me, sib = 2 * x + y, (x, y, 1 - c)
        own = [(a, (), a, (me,), sib, (me,)) for a in range(n)]
        passed = [(n + a, (_chip_of(p),), a, (_chip_of(p),), sib, (_chip_of(p),)) for a in range(n) for p in _chip_peers(x, y, c)]
        return own + passed

    from_core, token = _exchange(name + "_cores", list(blocks) + list(from_chips), [((4,) + b.shape, b.dtype) for b in blocks], plan_cores)
    x, y, c = _position()
    return [_by_core(c, _set_slot(r1, 2 * x + y, b), r2, 1) for b, r1, r2 in zip(blocks, from_chips, from_core)], token


def _gather_two_level(name, blocks):
    from_chips, _ = _exchange(name + "_chips", blocks, [((4,) + b.shape, b.dtype) for b in blocks], _plan_gather_chips(len(blocks)))
    return _gather_pass_cores(name, blocks, from_chips)


def _pair_add(name, g42, r4):
    _, _, rh, cols = g42.shape
    tr = _divisors(rh, 512, SUBLANES * 2)[-1]

    def body(c_ref, a_ref, b_ref, o_ref):
        o_ref[0] = (a_ref[0, 0].astype(F32) + b_ref[0].astype(F32)).astype(o_ref.dtype)

    cidx = lax.axis_index("c").astype(jnp.int32).reshape(1)
    return _pcall(
        body, name, _sds(r4.shape, BF16), grid=(4, rh // tr),
        in_specs=[pl.BlockSpec((1, 1, tr, cols), lambda s, i, c_ref: (s, c_ref[0], i, 0)), pl.BlockSpec((1, tr, cols), lambda s, i, c_ref: (s, i, 0))],
        out_specs=pl.BlockSpec((1, tr, cols), lambda s, i, c_ref: (s, i, 0)), sem=("parallel", "parallel"), prefetch=1,
    )(cidx, g42, r4)


def _slot_sum(name, parts):
    n, r, cols = parts.shape
    cap = max(2 * SUBLANES, (4 * 1024 * 1024) // (n * cols * parts.dtype.itemsize))
    tr = _divisors(r, cap, 2 * SUBLANES)[-1]

    def body(p_ref, o_ref):
        acc = p_ref[0].astype(F32)
        for k in range(1, n):
            acc = acc + p_ref[k].astype(F32)
        o_ref[...] = acc

    return _pcall(
        body, name, _sds((r, cols), F32), grid=(r // tr,), in_specs=[pl.BlockSpec((n, tr, cols), lambda i: (0, i, 0))],
        out_specs=pl.BlockSpec((tr, cols), lambda i: (i, 0)), sem=("parallel",),
    )(parts)


def _slot_sums(name, parts):
    k = len(parts)

    def body(*refs):
        for p_ref, o_ref in zip(refs[:k], refs[k:]):
            acc = p_ref[0].astype(F32)
            for j in range(1, p_ref.shape[0]):
                acc = acc + p_ref[j].astype(F32)
            o_ref[...] = acc

    return list(_pcall(body, name, [_sds(p.shape[1:], F32) for p in parts])(*parts))


def _gather_weights_start(name, shards, token):
    c = lax.axis_index("c")
    halves = [lax.dynamic_slice_in_dim(w, c * (w.shape[0] // 2), w.shape[0] // 2, 0).astype(BF16) for w in shards]
    if token is not None:
        halves[0] = _after(halves[0], token)
    n = len(halves)

    def plan(x, y, c):
        return [(a, (), a, (2 * x + y, c), p, (_chip_of(p), c)) for a in range(n) for p in _chip_peers(x, y, c)]

    return _exchange_start(name + "_ag_chips", halves, [((4, 2) + h.shape, h.dtype) for h in halves], plan)


def _gather_weights_finish(name, started, after):
    halves, slots, token = _exchange_wait(started, after)
    n = len(halves)
    x, y, c = _position()
    slots = [lax.dynamic_update_slice(s, h[None, None], (2 * x + y, c, 0, 0)) for s, h in zip(slots, halves)]

    def plan_cores(x, y, c):
        return [(a, (s, c), a, (s, c), (x, y, 1 - c), (s, 1 - c)) for a in range(n) for s in range(4)]

    full, _ = _exchange(name + "_ag_cores", slots, None, plan_cores, inplace=True)
    return [f.reshape((4, 2 * h.shape[0], h.shape[1])) for f, h in zip(full, halves)], token


def _reduce_weights_start(name, grads, token):
    n = len(grads)
    g42 = [g.reshape(4, 2, g.shape[1] // 2, g.shape[2]) for g in grads]

    def plan_swap(x, y, c):
        return [(a, (s, 1 - c), a, (s,), (x, y, 1 - c), (s,)) for a in range(n) for s in range(4)]

    def plan_chips(x, y, c):
        return [(a, (_chip_of(p),), a, (2 * x + y,), p, (_chip_of(p),)) for a in range(n) for p in _chip_peers(x, y, c)]

    other, _ = _exchange(name + "_rs_cores", g42, [((4,) + g.shape[2:], g.dtype) for g in g42], plan_swap, after=token)
    pair = [_pair_add(f"{name}_rs_pair{a}", g, o) for a, (g, o) in enumerate(zip(g42, other))]
    return _exchange_start(name + "_rs_chips", pair, [(p.shape, p.dtype) for p in pair], plan_chips)


def _reduce_weights_finish(name, started, after):
    pair, recv, token = _exchange_wait(started, after)
    n = len(pair)
    x, y, c = _position()
    me = 2 * x + y

    def plan_share(x, y, c):
        return [(a, (c,), a, (c,), (x, y, 1 - c), (1 - c,)) for a in range(n)]

    parts = [lax.dynamic_update_slice(r, lax.dynamic_slice_in_dim(p, me, 1, 0), (me, 0, 0)) for r, p in zip(recv, pair)]
    mine = [_slot_sum(f"{name}_rs_sum{a}", p) for a, p in enumerate(parts)]
    both = [lax.dynamic_update_slice(lax.empty((2,) + m.shape, m.dtype), m[None], (c, 0, 0)) for m in mine]
    both, _ = _exchange(name + "_rs_share", both, None, plan_share, inplace=True)
    return [b.reshape(2 * b.shape[1], b.shape[2]) for b in both], token


def _allreduce_small(name, grads):
    both, token = _gather_two_level(name + "_ag", [g.astype(BF16) if g.size >= SMALL_BF16_SIZE else g for g in grads])
    return _slot_sums(name + "_sum", [g.reshape((8,) + g.shape[2:]) for g in both]), token


def _ada_fwd_call(name, c_all, w, b_shard):
    nl, d, ns = w.shape
    nb = c_all.shape[0]

    def body(c_ref, w_ref, b_ref, o_ref):
        cv = c_ref[...]
        o_ref[0] = _dg(cv * _sigmoid(cv), w_ref[0], ((1,), (0,))) + b_ref[0]

    return _pcall(
        body, name, _sds((nl, nb, ns), F32), grid=(nl,),
        in_specs=[pl.BlockSpec((nb, d), lambda l: (0, 0)), pl.BlockSpec((1, d, ns), lambda l: (l, 0, 0)), pl.BlockSpec((1, 1, ns), lambda l: (l, 0, 0))],
        out_specs=pl.BlockSpec((1, nb, ns), lambda l: (l, 0, 0)), sem=("parallel",),
    )(c_all, w, b_shard)


def _ada_bwd_call(name, c_all, dm_shard, dm_all):
    nl, nb, ns = dm_shard.shape
    d = c_all.shape[1]
    nm = dm_all.shape[2]

    def body(c_ref, ds_ref, da_ref, dw_ref, db_ref):
        cv = c_ref[...]
        dw_ref[0] = _dg(cv * _sigmoid(cv), ds_ref[0], ((0,), (0,)))
        db_ref[0] = jnp.sum(da_ref[0], axis=0, keepdims=True)

    return _pcall(
        body, name, [_sds((nl, d, ns), F32), _sds((nl, 1, nm), F32)], grid=(nl,),
        in_specs=[pl.BlockSpec((nb, d), lambda l: (0, 0)), pl.BlockSpec((1, nb, ns), lambda l: (l, 0, 0)), pl.BlockSpec((1, nb, nm), lambda l: (l, 0, 0))],
        out_specs=[pl.BlockSpec((1, d, ns), lambda l: (l, 0, 0)), pl.BlockSpec((1, 1, nm), lambda l: (l, 0, 0))], sem=("parallel",),
    )(c_all, dm_shard, dm_all)


def _ada_fwd(name, bl, c_all, w, b):
    nl, d, ns = w.shape
    chip = 2 * lax.axis_index("x") + lax.axis_index("y")
    b_shard = lax.dynamic_slice(b, (0, chip * ns), (nl, ns)).reshape(nl, 1, ns)
    shard = _ada_fwd_call(name + "_fwd", c_all, w, b_shard)
    (allc,), token = _gather_chips(name + "_ag", [shard])
    mods = jnp.transpose(allc, (1, 2, 0, 3)).reshape(nl, c_all.shape[0], 4 * ns)
    return lax.dynamic_slice(mods, (0, (2 * chip + lax.axis_index("c")) * bl, 0), (nl, bl, 4 * ns)), token


def _ada_bwd(name, bl, c_all, ns, dm):
    nl = dm.shape[0]
    chip = 2 * lax.axis_index("x") + lax.axis_index("y")
    (dm_all,), token = _gather_two_level(name + "_bwd_ag", [dm])
    dm_all = jnp.transpose(dm_all.reshape((8,) + dm.shape), (1, 0, 2, 3)).reshape(nl, 8 * bl, 4 * ns)
    dm_shard = lax.dynamic_slice(dm_all, (0, 0, chip * ns), (nl, 8 * bl, ns))
    dw, db = _ada_bwd_call(name + "_bwd", c_all, dm_shard, dm_all)
    return dw, db.reshape(nl, 4 * ns), token


def _adamw(name, w, g, m, v):
    shape = w.shape
    cols = shape[-1]
    w2, g2, m2, v2 = (t.reshape(-1, cols) for t in (w, g, m, v))
    rows = w2.shape[0]
    cap = max(SUBLANES, (512 * 1024) // max(cols, 1) // SUBLANES * SUBLANES)
    tr = _divisors(rows, cap, SUBLANES)[-1]

    def body(w_ref, g_ref, m_ref, v_ref, d_ref, mo_ref, vo_ref):
        gv = g_ref[...]
        mn = ADAM_B1 * m_ref[...] + (1.0 - ADAM_B1) * gv
        vn = ADAM_B2 * v_ref[...] + (1.0 - ADAM_B2) * (gv * gv)
        m_hat = mn / (1.0 - ADAM_B1 ** ADAM_STEP)
        v_hat = vn / (1.0 - ADAM_B2 ** ADAM_STEP)
        d_ref[...] = -ADAM_LR * (m_hat / (jnp.sqrt(v_hat) + ADAM_EPS) + ADAM_WD * w_ref[...])
        mo_ref[...] = mn
        vo_ref[...] = vn

    spec = pl.BlockSpec((tr, cols), lambda i: (i, 0))
    outs = _pcall(body, name, [_sds((rows, cols), F32)] * 3, grid=(rows // tr,), in_specs=[spec] * 4, out_specs=[spec] * 3, sem=("parallel",))(w2, g2, m2, v2)
    return tuple(o.reshape(shape) for o in outs)


def _adamw_small(name, ws, gs, ms, vs):
    n = len(ws)

    def body(*refs):
        for k in range(n):
            w_ref, g_ref, m_ref, v_ref = (refs[j * n + k] for j in range(4))
            d_ref, mo_ref, vo_ref = (refs[(4 + j) * n + k] for j in range(3))
            gv = g_ref[...]
            mn = ADAM_B1 * m_ref[...] + (1.0 - ADAM_B1) * gv
            vn = ADAM_B2 * v_ref[...] + (1.0 - ADAM_B2) * (gv * gv)
            m_hat = mn / (1.0 - ADAM_B1 ** ADAM_STEP)
            v_hat = vn / (1.0 - ADAM_B2 ** ADAM_STEP)
            d_ref[...] = -ADAM_LR * (m_hat / (jnp.sqrt(v_hat) + ADAM_EPS) + ADAM_WD * w_ref[...])
            mo_ref[...] = mn
            vo_ref[...] = vn

    outs = _pcall(body, name, [_sds(w.shape, F32) for w in ws] * 3)(*ws, *gs, *ms, *vs)
    return outs[:n], outs[n:2 * n], outs[2 * n:]


def _pad_rows(w, rows):
    return jnp.pad(w, ((0, rows - w.shape[0]), (0, 0)))


def _pad_lanes(v):
    return jnp.pad(v, (0, LANES - v.shape[0])).reshape(1, LANES)


BIG = ("w_in", "w_out", "ff_up", "ff_down")
BIG_AXIS = {"w_in": 1, "w_out": 0, "ff_up": 1, "ff_down": 0}
CONVW = ("ssd_conv_w", "ff_conv_w")
SMALL = ("norm1_g", "norm2_g", "ssd_conv_b", "ssd_dt_bias", "ssd_a_log", "ssd_d", "ssd_norm_g", "gm_vnorm_g", "gm_ws", "gm_bs", "gm_out_g", "ff_conv_b")
WEIGHTS = ("ada_w", "ada_b", "norm1_g", "norm2_g", "w_in", "ssd_conv_w", "ssd_conv_b", "ssd_dt_bias", "ssd_a_log", "ssd_d", "ssd_norm_g", "gm_vnorm_g", "gm_ws", "gm_bs", "gm_out_g", "w_out", "ff_up", "ff_conv_w", "ff_conv_b", "ff_down", "final_g")


def kernel(x, c, ada_w, ada_b, norm1_g, norm2_g, w_in, ssd_conv_w, ssd_conv_b, ssd_dt_bias, ssd_a_log, ssd_d, ssd_norm_g, gm_vnorm_g, gm_ws, gm_bs, gm_out_g, w_out, ff_up, ff_conv_w, ff_conv_b, ff_down, final_g, loss_target, m_ada_w, m_ada_b, m_norm1_g, m_norm2_g, m_w_in, m_ssd_conv_w, m_ssd_conv_b, m_ssd_dt_bias, m_ssd_a_log, m_ssd_d, m_ssd_norm_g, m_gm_vnorm_g, m_gm_ws, m_gm_bs, m_gm_out_g, m_w_out, m_ff_up, m_ff_conv_w, m_ff_conv_b, m_ff_down, m_final_g, v_ada_w, v_ada_b, v_norm1_g, v_norm2_g, v_w_in, v_ssd_conv_w, v_ssd_conv_b, v_ssd_dt_bias, v_ssd_a_log, v_ssd_d, v_ssd_norm_g, v_gm_vnorm_g, v_gm_ws, v_gm_bs, v_gm_out_g, v_w_out, v_ff_up, v_ff_conv_w, v_ff_conv_b, v_ff_down, v_final_g):
    given = dict(locals())
    weights = {n: given[n] for n in WEIGHTS}
    bl, s, d = x.shape
    nl = ada_w.shape[0]
    heads = d // HEAD_DIM
    cd = d + 2 * SSD_GROUPS * STATE
    f = ff_down.shape[1] * 4
    n_in = d + cd + heads + 2 * d
    n_proj = _round_up(3 * d + cd + LANES, 2 * LANES)
    tm = _divisors(s, 512)[-1]

    gathering = {}

    def start_gathers(l, names, token):
        for n in names:
            gathering[l, n] = _gather_weights_start(f"wg{l}_{n}", [weights[n][l]], token)
            token = gathering[l, n]["token"]
        return token

    pre, _ = _gather_two_level("pre_ag", [c] + [weights[n] for n in CONVW])
    c_all = pre[0].reshape(8 * bl, d)
    conv_full = {n: jnp.concatenate([p[k, 0] for k in range(4)], axis=-1) for n, p in zip(CONVW, pre[1:])}
    mods_all, ada_token = _ada_fwd("ada", bl, c_all, ada_w, ada_b)
    mods = mods_all.reshape(nl, bl, N_MOD, 1, d)

    def landed(l, n, after):
        (full,), token = _gather_weights_finish(f"wg{l}_{n}", gathering.pop((l, n)), after)
        if n == "w_in":
            return full, token
        return (full.reshape(-1, full.shape[2]) if BIG_AXIS[n] == 0 else jnp.concatenate([full[k] for k in range(4)], axis=1)), token

    issued = start_gathers(0, BIG, ada_token)
    r2 = lambda v: v.reshape(1, -1)
    n_gm = d // GM_HEAD
    fg = r2(final_g)

    def layer_params(l):
        return dict(
            norm1=r2(norm1_g[l]), norm2=r2(norm2_g[l]), conv_w8=_pad_rows(conv_full["ssd_conv_w"][l], SUBLANES), conv_b=r2(ssd_conv_b[l]),
            dtb=_pad_lanes(ssd_dt_bias[l]), alog=_pad_lanes(ssd_a_log[l]), dsk=r2(jnp.repeat(ssd_d[l], HEAD_DIM)), ng=r2(ssd_norm_g[l]),
            vg=r2(gm_vnorm_g[l]), ws=gm_ws[l], bs_t=jnp.pad(gm_bs[l].T, ((0, 0), (0, LANES - n_gm))), og=r2(gm_out_g[l]),
            ff_w8=_pad_rows(conv_full["ff_conv_w"][l], SUBLANES), ff_b=r2(ff_conv_b[l]))

    def mixer_args(p):
        return (p["conv_w8"], p["conv_b"], p["dtb"], p["alog"], p["dsk"], p["ng"], p["vg"], p["ws"], p["bs_t"], p["og"])

    def padded_w_in(full):
        ns = full.shape[2]

        def cols(a, b):
            return [full[k][:, max(a, k * ns) - k * ns:min(b, (k + 1) * ns) - k * ns] for k in range(4) if max(a, k * ns) < min(b, (k + 1) * ns)]

        parts = cols(0, d) + cols(d + cd + heads, n_in) + cols(d, d + cd) + cols(d + cd, d + cd + heads)
        return jnp.concatenate(parts + [jnp.zeros((d, n_proj - (3 * d + cd + heads)), BF16)], axis=1)

    saved, xcur, pending = [], x, None
    for l in range(nl):
        p, w = layer_params(l), {}
        wi, token = landed(l, "w_in", mods_all if l == 0 else pending[0])
        w["w_in"] = padded_w_in(wi)
        sh1, sc1, g1, sh2, sc2, g2 = (mods[l, :, k] for k in range(N_MOD))
        if l + 1 < nl:
            issued = start_gathers(l + 1, BIG, token + issued)
        sc1 = _after(sc1, issued)
        if pending is None:
            (h,) = _row_fwd(f"nm{l}_fwd", _nm_tile, [(xcur, d, 0, None)], [sc1, sh1], [p["norm1"]], [(d, BF16)], 0, tm)
            x1 = xcur
        else:
            rows = [(xcur, d, 0, None), (pending[0], d, 0, None)]
            x1, h = _row_fwd(f"rnm{l}a_fwd", _rnm_tile, rows, [pending[1], sc1, sh1], [p["norm1"]], [(d, F32), (d, BF16)], 0, tm)
        proj = _mm(f"win{l}_fwd", h, w["w_in"], F32)
        ycat, (xbc, hprev) = _mixer_fwd(f"mix{l}", proj, d, cd, *mixer_args(p))
        w["w_out"], _ = landed(l, "w_out", ycat)
        mix = _mm(f"wout{l}_fwd", ycat, w["w_out"], F32)
        x2, h2 = _row_fwd(f"rnm{l}b_fwd", _rnm_tile, [(x1, d, 0, None), (mix, d, 0, None)], [g1, sc2, sh2], [p["norm2"]], [(d, F32), (d, BF16)], 0, tm)
        w_up, _ = landed(l, "ff_up", h2)
        w["ff_up"] = _gate_value_blocks(f"ffup{l}_blocks", w_up, f)
        up = _mm(f"ffup{l}_fwd", h2, w["ff_up"], F32)
        act = _conv_fwd(f"ffact{l}_fwd", up, 0, f, p["ff_w8"], p["ff_b"], FF_CONV, True, BF16)
        w["ff_down"], _ = landed(l, "ff_down", act)
        down = _mm(f"ffdown{l}_fwd", act, w["ff_down"], F32)
        saved.append(dict(p=p, w=w, x_in=xcur, pending=pending, h=h, proj=proj, xbc=xbc, hprev=hprev, ycat=ycat, x1=x1, mix=mix, h2=h2, up=up, act=act))
        xcur, pending = x2, (down, g2)
    rows = [(xcur, d, 0, F32), (pending[0], d, 0, BF16), (loss_target, d, 0, None)]
    (loss_local,) = _row_fwd("final_fwd", _final_tile, rows, [pending[1]], [fg], [], 1, tm)
    loss = lax.psum(loss_local[0, 0], AXES)

    (dx2, ddown), (dg2,), (dfg,) = _row_bwd("final_bwd", _final_tile, rows, [pending[1]], [fg], [], 1, tm, [jnp.ones((1, 1), F32)])
    dmods, small_grads = [None] * nl, [None] * nl
    reducing = {}

    def start_reduce(l, n, g, token=None):
        width = g.shape[1] // 4
        by_chip = g.reshape(4, g.shape[0] // 4, g.shape[1]) if BIG_AXIS[n] == 0 else jnp.stack([g[:, k * width:(k + 1) * width] for k in range(4)])
        reducing[l, n] = _reduce_weights_start(f"wg{l}_{n}", [by_chip], token)
        return reducing[l, n]["token"]

    for l in reversed(range(nl)):
        sv = saved[l]
        p, w = sv["p"], sv["w"]
        sh1, sc1, g1, sh2, sc2, g2 = (mods[l, :, k] for k in range(N_MOD))
        dact, dw_down = _mm_bwd(f"ffdown{l}", sv["act"], w["ff_down"], ddown)
        ff_b = _after(p["ff_b"], start_reduce(l, "ff_down", dw_down))
        dup, dff_w8, dff_b = _conv_bwd(f"ffact{l}_bwd", sv["up"], 0, f, p["ff_w8"], ff_b, FF_CONV, True, dact, BF16)
        dh2, dw_up = _mm_bwd(f"ffup{l}", sv["h2"], w["ff_up"], dup)
        sc2 = _after(sc2, start_reduce(l, "ff_up", _gate_value_blocks(f"ffup{l}_columns", dw_up, f, inverse=True)))
        rows = [(sv["x1"], d, 0, F32), (sv["mix"], d, 0, BF16)]
        (dx1, dmix), (dg1, dsc2, dsh2), (dn2,) = _row_bwd(f"rnm{l}b_bwd", _rnm_tile, rows, [g1, sc2, sh2], [p["norm2"]], [(d, F32), (d, BF16)], 0, tm, [dx2, dh2])
        dycat, dw_out = _mm_bwd(f"wout{l}", sv["ycat"], w["w_out"], dmix)
        p_tied = dict(p, dtb=_after(p["dtb"], start_reduce(l, "w_out", dw_out)))
        dproj, (dw8, dcb, ddtb, dalog, ddsk, dng, dvg, dws, dbs_t, dog) = _mixer_bwd(f"mix{l}", sv["proj"], d, cd, *mixer_args(p_tied), sv["xbc"], sv["hprev"], dycat)
        dh, dw_in_p = _mm_bwd(f"win{l}", sv["h"], w["w_in"], dproj)
        dw_in = jnp.concatenate([dw_in_p[:, :d], dw_in_p[:, 3 * d:3 * d + cd], dw_in_p[:, 3 * d + cd:3 * d + cd + heads], dw_in_p[:, d:3 * d]], axis=1)
        if l > 0:
            sc1 = _after(sc1, start_reduce(l, "w_in", dw_in))
        if sv["pending"] is None:
            (dx2,), (dsc1, dsh1), (dn1,) = _row_bwd(f"nm{l}_bwd", _nm_res_tile, [(sv["x_in"], d, 0, F32)], [sc1, sh1], [p["norm1"]], [(d, F32), (d, BF16)], 0, tm, [dx1, dh])
        else:
            rows = [(sv["x_in"], d, 0, F32), (sv["pending"][0], d, 0, BF16)]
            (dx2, ddown), (dg2_prev, dsc1, dsh1), (dn1,) = _row_bwd(
                f"rnm{l}a_bwd", _rnm_tile, rows, [sv["pending"][1], sc1, sh1], [p["norm1"]], [(d, F32), (d, BF16)], 0, tm, [dx1, dh])
        dmods[l] = jnp.concatenate([dsh1, dsc1, dg1, dsh2, dsc2, dg2], axis=1).reshape(bl, N_MOD * d)
        if sv["pending"] is not None:
            dg2 = dg2_prev
        small_grads[l] = dict(
            norm1_g=dn1.reshape(d), norm2_g=dn2.reshape(d), ssd_conv_b=dcb.reshape(cd), ssd_dt_bias=ddtb[0, :heads], ssd_a_log=dalog[0, :heads],
            ssd_d=ddsk.reshape(heads, HEAD_DIM).sum(-1), ssd_norm_g=dng.reshape(d), gm_vnorm_g=dvg.reshape(d), gm_ws=dws, gm_bs=dbs_t[:, :n_gm].T,
            gm_out_g=dog.reshape(d), ff_conv_b=dff_b.reshape(f), ssd_conv_w=dw8[:SSD_CONV], ff_conv_w=dff_w8[:FF_CONV])
    grad_x = dx2

    g_ada_w, g_ada_b, ada_token = _ada_bwd("ada", bl, c_all, ada_w.shape[2], jnp.stack(dmods))
    small_names_r = SMALL + CONVW + ("final_g",)
    summed, small_token = _allreduce_small("small", [jnp.stack([small_grads[l][n] for l in range(nl)]) for n in SMALL + CONVW] + [dfg])
    travelling = start_reduce(0, "w_in", dw_in, ada_token + small_token)
    grad = {"ada_w": g_ada_w, "ada_b": g_ada_b}
    grad.update(zip(small_names_r, summed))
    chip = 2 * lax.axis_index("x") + lax.axis_index("y")
    for n in CONVW:
        width = weights[n].shape[-1]
        grad[n] = lax.dynamic_slice_in_dim(grad[n], chip * width, width, axis=2)

    delta, new_m, new_v = {}, {}, {}
    big_grads, done = {}, _after(summed[0], travelling)
    for n in ("ada_w",) + tuple(reversed(BIG)):
        if n != "ada_w":
            for l in reversed(range(nl)):
                (big_grads[l, n],), _ = _reduce_weights_finish(f"wg{l}_{n}", reducing.pop((l, n)), done)
            grad[n] = jnp.stack([big_grads[l, n] for l in range(nl)])
        delta[n], new_m[n], new_v[n] = _adamw("adam_" + n, weights[n], grad[n], given["m_" + n], given["v_" + n])
        if n != "ada_w":
            done = delta[n]
    small_names = ("ada_b",) + SMALL + CONVW + ("final_g",)
    as2d = lambda t: t.reshape(1, -1) if t.ndim == 1 else t
    res = _adamw_small(
        "adam_small", [as2d(weights[n]) for n in small_names], [as2d(grad[n]) for n in small_names],
        [as2d(given["m_" + n]) for n in small_names], [as2d(given["v_" + n]) for n in small_names])
    for out, vals in zip((delta, new_m, new_v), res):
        for n, val in zip(small_names, vals):
            out[n] = val.reshape(weights[n].shape)
    grad["final_g"] = grad["final_g"].reshape(final_g.shape)

    return (loss, grad_x, *[grad[n] for n in WEIGHTS], *[delta[n] for n in WEIGHTS], *[new_m[n] for n in WEIGHTS], *[new_v[n] for n in WEIGHTS])
```

```python
import jax
import jax.numpy as jnp
from jax import lax
from jax.experimental import pallas as pl
from jax.experimental.pallas import tpu as pltpu

F32 = jnp.float32
BF16 = jnp.bfloat16
EPS = 1e-6
CHUNK = 128
HEAD_DIM = 64
STATE = 128
GM_HEAD = 128
SSD_GROUPS = 2
SSD_CONV = 4
FF_CONV = 3
N_MOD = 6
LANES = 128
SUBLANES = 8
SMALL_BF16_SIZE = 1 << 16
CONV_LANES = 256
CONV_ROWS = 32
V7X_VMEM_LIMIT = 48 * 1024 * 1024
MM_VMEM_BUDGET = 30 * 1024 * 1024
MM_STEP_MACS = 2048 * 1024 * 1024
MM_ACC_SECONDS_PER_VREG = 1.4e-9
V7X_MXU_FLOPS = 996e12
V7X_HBM_BYTES_PER_S = 3.3e12
V7X_STEP_SECONDS = 0.35e-6
ADAM_LR, ADAM_B1, ADAM_B2, ADAM_EPS, ADAM_WD, ADAM_STEP = 0.001, 0.9, 0.999, 1e-08, 0.01, 10
MESH = pl.DeviceIdType.MESH
AXES = ("x", "y", "c")


def _round_up(n, m):
    return (n + m - 1) // m * m


def _divisors(n, cap, mult=LANES):
    out = [t for t in range(mult, min(n, cap) + 1, mult) if n % t == 0]
    return out or [n]


def _pcall(body, name, out_shape, grid=(), in_specs=None, out_specs=None, scratch=(), sem=None, prefetch=0):
    params = pltpu.CompilerParams(dimension_semantics=sem, vmem_limit_bytes=V7X_VMEM_LIMIT)
    if prefetch:
        spec = pltpu.PrefetchScalarGridSpec(num_scalar_prefetch=prefetch, grid=grid, in_specs=in_specs, out_specs=out_specs, scratch_shapes=list(scratch))
        return pl.pallas_call(body, name=name, out_shape=out_shape, grid_spec=spec, compiler_params=params)
    if in_specs is None:
        return pl.pallas_call(body, name=name, out_shape=out_shape, compiler_params=params)
    return pl.pallas_call(body, name=name, out_shape=out_shape, grid=grid, in_specs=in_specs, out_specs=out_specs, scratch_shapes=list(scratch), compiler_params=params)


def _sds(shape, dtype):
    return jax.ShapeDtypeStruct(tuple(shape), dtype)


def _mm_tiles(m, n, k, a_bytes, b_bytes, o_bytes):
    best, best_key = None, None
    for tm in _divisors(m, 2048):
        for tn in _divisors(n, 2560):
            for tk in _divisors(k, 2560):
                vmem = 2 * (tm * tk * a_bytes + tk * tn * b_bytes + tm * tn * o_bytes) + tm * tn * 4
                if vmem > MM_VMEM_BUDGET or tm * tn * tk > MM_STEP_MACS:
                    continue
                ni, nj, nk = m // tm, n // tn, k // tk
                a_reads = 1 if nk == 1 else nj
                b_reads = 1 if (nk == 1 and nj == 1) else ni
                hbm = a_reads * m * k * a_bytes + b_reads * k * n * b_bytes + m * n * o_bytes
                t = max(2.0 * m * n * k / V7X_MXU_FLOPS, hbm / V7X_HBM_BYTES_PER_S) + ni * nj * nk * V7X_STEP_SECONDS
                if nk > 1:
                    t += ni * nj * nk * (tm * tn // (SUBLANES * LANES)) * MM_ACC_SECONDS_PER_VREG
                key = (-t, tm * tn * tk)
                if best_key is None or key > best_key:
                    best, best_key = (tm, tn, tk), key
    return best


def _matmul(name, a, b, mode, out_dtype):
    if mode == "nn":
        (m, k), n = a.shape, b.shape[1]
    elif mode == "nt":
        (m, k), n = a.shape, b.shape[0]
    else:
        (k, m), n = a.shape, b.shape[1]
    tm, tn, tk = _mm_tiles(m, n, k, a.dtype.itemsize, b.dtype.itemsize, jnp.dtype(out_dtype).itemsize)
    nk = k // tk
    if mode == "nn":
        a_spec = pl.BlockSpec((tm, tk), lambda i, j, kk: (i, kk))
        b_spec = pl.BlockSpec((tk, tn), lambda i, j, kk: (kk, j))
        dims = ((1,), (0,))
    elif mode == "nt":
        a_spec = pl.BlockSpec((tm, tk), lambda i, j, kk: (i, kk))
        b_spec = pl.BlockSpec((tn, tk), lambda i, j, kk: (j, kk))
        dims = ((1,), (1,))
    else:
        a_spec = pl.BlockSpec((tk, tm), lambda i, j, kk: (kk, i))
        b_spec = pl.BlockSpec((tk, tn), lambda i, j, kk: (kk, j))
        dims = ((0,), (0,))

    def body(a_ref, b_ref, o_ref, acc_ref):
        kk = pl.program_id(2)
        p = lax.dot_general(a_ref[...].astype(BF16), b_ref[...].astype(BF16), (dims, ((), ())), preferred_element_type=F32)
        if nk == 1:
            o_ref[...] = p.astype(o_ref.dtype)
        else:
            @pl.when(kk == 0)
            def _():
                acc_ref[...] = p

            @pl.when(kk > 0)
            def _():
                acc_ref[...] += p

            @pl.when(kk == nk - 1)
            def _():
                o_ref[...] = acc_ref[...].astype(o_ref.dtype)

    return _pcall(
        body, name, _sds((m, n), out_dtype), grid=(m // tm, n // tn, nk), in_specs=[a_spec, b_spec],
        out_specs=pl.BlockSpec((tm, tn), lambda i, j, kk: (i, j)), scratch=[pltpu.VMEM((tm, tn), F32)],
        sem=("parallel", "parallel", "arbitrary"),
    )(a, b)


def _mm(name, a, w, out_dtype):
    return _matmul(name, a.reshape(-1, a.shape[-1]), w, "nn", out_dtype).reshape(a.shape[:-1] + (w.shape[1],))


def _mm_bwd(name, a, w, dy):
    a2, dy2 = a.reshape(-1, a.shape[-1]), dy.reshape(-1, dy.shape[-1])
    return _matmul(name + "_dx", dy2, w, "nt", BF16).reshape(a.shape), _matmul(name + "_dw", a2, dy2, "tn", BF16)


def _dg(a, b, dims):
    return lax.dot_general(a.astype(BF16), b.astype(BF16), (dims, ((), ())), preferred_element_type=F32)


@jax.custom_vjp
def _dot_nn(a, b):
    return _dg(a, b, ((1,), (0,)))


_dot_nn.defvjp(lambda a, b: (_dot_nn(a, b), (a, b)), lambda r, d: (_dg(d, r[1], ((1,), (1,))), _dg(r[0], d, ((0,), (0,)))))


@jax.custom_vjp
def _dot_nt(a, b):
    return _dg(a, b, ((1,), (1,)))


_dot_nt.defvjp(lambda a, b: (_dot_nt(a, b), (a, b)), lambda r, d: (_dg(d, r[1], ((1,), (0,))), _dg(d, r[0], ((0,), (0,)))))


@jax.custom_vjp
def _dot_tn(a, b):
    return _dg(a, b, ((0,), (0,)))


_dot_tn.defvjp(lambda a, b: (_dot_tn(a, b), (a, b)), lambda r, d: (_dg(r[1], d, ((1,), (1,))), _dg(r[0], d, ((1,), (0,)))))


def _exact_dot(a, c, dims):
    hi = a.astype(BF16)
    r1 = a - hi.astype(F32)
    mid = r1.astype(BF16)
    lo = (r1 - mid.astype(F32)).astype(BF16)
    cb = c.astype(BF16)
    f = lambda t: lax.dot_general(t, cb, (dims, ((), ())), preferred_element_type=F32)
    return f(hi) + f(mid) + f(lo)


@jax.custom_vjp
def _sel_right(a, c):
    return _exact_dot(a, c, ((1,), (0,)))


_sel_right.defvjp(lambda a, c: (_sel_right(a, c), c), lambda c, d: (_exact_dot(d, c, ((1,), (1,))), jnp.zeros_like(c)))


def _exact_dot_left(c, a, dims):
    hi = a.astype(BF16)
    r1 = a - hi.astype(F32)
    mid = r1.astype(BF16)
    lo = (r1 - mid.astype(F32)).astype(BF16)
    cb = c.astype(BF16)
    f = lambda t: lax.dot_general(cb, t, (dims, ((), ())), preferred_element_type=F32)
    return f(hi) + f(mid) + f(lo)


@jax.custom_vjp
def _sel_left(c, a):
    return _exact_dot_left(c, a, ((1,), (0,)))


_sel_left.defvjp(lambda c, a: (_sel_left(c, a), c), lambda c, d: (jnp.zeros_like(c), _exact_dot_left(c, d, ((0,), (0,)))))


def _sigmoid(x):
    return 1.0 / (1.0 + jnp.exp(-x))


def _rms(x, g):
    return x * lax.rsqrt(jnp.mean(x * x, axis=-1, keepdims=True) + EPS) * g


def _gelu(x):
    return 0.5 * x * (1.0 + lax.erf(x * (2.0 ** -0.5)))


def _causal(n):
    return lax.broadcasted_iota(jnp.int32, (n, n), 0) >= lax.broadcasted_iota(jnp.int32, (n, n), 1)


def _row_in_specs(rows, bparams, gparams, tm):
    specs = [pl.BlockSpec((1, tm, w), lambda b, i, cb=cb: (b, i, cb)) for (_, w, cb, _) in rows]
    specs += [pl.BlockSpec((1, 1, p.shape[-1]), lambda b, i: (b, 0, 0)) for p in bparams]
    specs += [pl.BlockSpec(p.shape, lambda b, i, n=p.ndim: (0,) * n) for p in gparams]
    return specs


def _row_vals(refs, n_rows, n_b, n_g):
    vals = [r[0].astype(F32) for r in refs[:n_rows]]
    vals += [r[0].astype(F32) for r in refs[n_rows:n_rows + n_b]]
    vals += [r[...].astype(F32) for r in refs[n_rows + n_b:n_rows + n_b + n_g]]
    return vals


def _row_fwd(name, tile, rows, bparams, gparams, outs, n_sum, tm):
    bl, s = rows[0][0].shape[:2]
    n_in = len(rows) + len(bparams) + len(gparams)

    def body(*refs):
        first = (pl.program_id(0) == 0) & (pl.program_id(1) == 0)
        res = tile(*_row_vals(refs, len(rows), len(bparams), len(gparams)))
        o_refs = refs[n_in:]
        for k in range(len(outs)):
            o_refs[k][0] = res[k].astype(o_refs[k].dtype)
        for k in range(n_sum):
            o_ref, val = o_refs[len(outs) + k], res[len(outs) + k]

            @pl.when(first)
            def _(o_ref=o_ref, val=val):
                o_ref[...] = val

            @pl.when(jnp.logical_not(first))
            def _(o_ref=o_ref, val=val):
                o_ref[...] += val

    out_shape = [_sds((bl, s, w), dt) for (w, dt) in outs] + [_sds((1, 1), F32)] * n_sum
    out_specs = [pl.BlockSpec((1, tm, w), lambda b, i: (b, i, 0)) for (w, _) in outs] + [pl.BlockSpec((1, 1), lambda b, i: (0, 0))] * n_sum
    return _pcall(
        body, name, out_shape, grid=(bl, s // tm), in_specs=_row_in_specs(rows, bparams, gparams, tm), out_specs=out_specs,
        sem=("arbitrary", "arbitrary"),
    )(*[r[0] for r in rows], *bparams, *gparams)


def _row_bwd(name, tile, rows, bparams, gparams, outs, n_sum, tm, cts, emit_sums=False):
    bl, s = rows[0][0].shape[:2]
    n_r, n_b, n_g = len(rows), len(bparams), len(gparams)
    n_extra = n_sum if emit_sums else 0
    n_in = n_r + n_b + n_g
    n_ct = len(outs) + n_sum
    grad_rows = [k for k in range(n_r) if rows[k][3]]

    def body(*refs):
        b, i = pl.program_id(0), pl.program_id(1)
        vals = _row_vals(refs, n_r, n_b, n_g)
        ct_refs = refs[n_in:n_in + n_ct]
        ct = [r[0].astype(F32) for r in ct_refs[:len(outs)]] + [r[...] for r in ct_refs[len(outs):]]
        primal, vjp = jax.vjp(tile, *vals)
        grads = tuple(vjp(tuple(ct))) + (tuple(primal[len(outs):]) if emit_sums else ())
        o_refs = refs[n_in + n_ct:]
        for j, k in enumerate(grad_rows):
            o_refs[j][0] = grads[k].astype(o_refs[j].dtype)
        for k in range(n_b):
            o_ref, val = o_refs[len(grad_rows) + k], grads[n_r + k]

            @pl.when(i == 0)
            def _(o_ref=o_ref, val=val):
                o_ref[0] = val

            @pl.when(i > 0)
            def _(o_ref=o_ref, val=val):
                o_ref[0] += val

        first = (b == 0) & (i == 0)
        for k in range(n_g + n_extra):
            o_ref, val = o_refs[len(grad_rows) + n_b + k], grads[n_r + n_b + k]

            @pl.when(first)
            def _(o_ref=o_ref, val=val):
                o_ref[...] = val

            @pl.when(jnp.logical_not(first))
            def _(o_ref=o_ref, val=val):
                o_ref[...] += val

    one = pl.BlockSpec((1, 1), lambda b, i: (0, 0))
    in_specs = _row_in_specs(rows, bparams, gparams, tm)
    in_specs += [pl.BlockSpec((1, tm, w), lambda b, i: (b, i, 0)) for (w, _) in outs] + [one] * n_sum
    out_shape = [_sds((bl, s, rows[k][1]), rows[k][3]) for k in grad_rows]
    out_shape += [_sds(p.shape, F32) for p in bparams] + [_sds(p.shape, F32) for p in gparams] + [_sds((1, 1), F32)] * n_extra
    out_specs = [pl.BlockSpec((1, tm, rows[k][1]), lambda b, i: (b, i, 0)) for k in grad_rows]
    out_specs += [pl.BlockSpec((1, 1, p.shape[-1]), lambda b, i: (b, 0, 0)) for p in bparams]
    out_specs += [pl.BlockSpec(p.shape, lambda b, i, n=p.ndim: (0,) * n) for p in gparams] + [one] * n_extra
    res = _pcall(
        body, name, out_shape, grid=(bl, s // tm), in_specs=in_specs, out_specs=out_specs, sem=("arbitrary", "arbitrary"),
    )(*[r[0] for r in rows], *bparams, *gparams, *cts)
    n0, n1, n2 = len(grad_rows), len(grad_rows) + n_b, len(grad_rows) + n_b + n_g
    return (res[:n0], res[n0:n1], res[n1:n2]) + ((res[n2:],) if emit_sums else ())


def _nm_tile(x, sc, sh, g):
    return (_rms(x, g) * (1.0 + sc) + sh,)


def _nm_res_tile(x, sc, sh, g):
    return x, _rms(x, g) * (1.0 + sc) + sh


def _rnm_tile(x, o, gate, sc, sh, g):
    xn = x + gate * o
    return xn, _rms(xn, g) * (1.0 + sc) + sh


def _final_tile(x, o, tgt, gate, g):
    e = _rms(x + gate * o, g) - tgt
    return (0.5 * jnp.sum(jnp.mean(e * e, axis=-1, keepdims=True), axis=0, keepdims=True),)


def _gm_tile(u_in, v_in, vg, ws, bs_t, og):
    d = u_in.shape[1]
    u, vn = _gelu(u_in), _rms(_gelu(v_in), vg)
    causal = _causal(CHUNK)
    lane = lax.broadcasted_iota(jnp.int32, (1, LANES), 1)
    parts = []
    for h in range(d // GM_HEAD):
        bias = jnp.sum(bs_t * (lane == h).astype(F32), axis=1, keepdims=True)
        parts.append(_dot_nn(jnp.where(causal, ws[h], 0.0), vn[:, h * GM_HEAD:(h + 1) * GM_HEAD]) + bias)
    return (_rms(u * jnp.concatenate(parts, axis=1), og),)


def _conv_window(ref, r0, rows, c0, tc, seq, before, after):
    parts = []
    if before:
        p0 = pl.multiple_of(jnp.maximum(r0 - SUBLANES, 0), SUBLANES)
        parts.append(jnp.where(r0 > 0, ref[0, pl.ds(p0, SUBLANES), pl.ds(c0, tc)].astype(F32), 0.0))
    parts.append(ref[0, pl.ds(r0, rows), pl.ds(c0, tc)].astype(F32))
    if after:
        n0 = pl.multiple_of(jnp.minimum(r0 + rows, seq - SUBLANES), SUBLANES)
        parts.append(jnp.where(r0 + rows < seq, ref[0, pl.ds(n0, SUBLANES), pl.ds(c0, tc)].astype(F32), 0.0))
    return jnp.concatenate(parts, axis=0) if len(parts) > 1 else parts[0]


def _conv_taps_pre(xe, w, b, taps, rows):
    pre = xe[SUBLANES:SUBLANES + rows] * w[taps - 1] + b
    for j in range(1, taps):
        pre = pre + pltpu.roll(xe, j, 0)[SUBLANES:SUBLANES + rows] * w[taps - 1 - j]
    return pre


def _conv_fwd(name, src, col0, chans, w8, b, taps, gated, out_dtype):
    bl, s, _ = src.shape
    tc = CONV_LANES
    xw = 2 * tc if gated else tc

    def body(x_ref, w_ref, b_ref, o_ref):
        w = [w_ref[k:k + 1, :] for k in range(taps)]
        bias = b_ref[...]

        def step(c, carry):
            r0 = pl.multiple_of(c * CONV_ROWS, CONV_ROWS)
            xe = _conv_window(x_ref, r0, CONV_ROWS, 0, tc, s, True, False)
            pre = _conv_taps_pre(xe, w, bias, taps, CONV_ROWS)
            y = pre * _sigmoid(pre)
            if gated:
                y = y * x_ref[0, pl.ds(r0, CONV_ROWS), pl.ds(tc, tc)]
            o_ref[0, pl.ds(r0, CONV_ROWS), :] = y.astype(o_ref.dtype)
            return carry

        lax.fori_loop(0, s // CONV_ROWS, step, 0, unroll=4)

    first = 0 if gated else col0 // tc
    return _pcall(
        body, name, _sds((bl, s, chans), out_dtype), grid=(chans // tc, bl),
        in_specs=[pl.BlockSpec((1, s, xw), lambda j, bb: (bb, 0, first + j)), pl.BlockSpec((SUBLANES, tc), lambda j, bb: (0, j)), pl.BlockSpec((1, tc), lambda j, bb: (0, j))],
        out_specs=pl.BlockSpec((1, s, tc), lambda j, bb: (bb, 0, j)), sem=("parallel", "arbitrary"),
    )(src, w8, b)


def _conv_bwd(name, src, col0, chans, w8, b, taps, gated, dy, dx_dtype):
    bl, s, _ = src.shape
    tc = CONV_LANES
    xw = 2 * tc if gated else tc
    ext = CONV_ROWS + SUBLANES

    def body(x_ref, w_ref, b_ref, dy_ref, dx_ref, dw_ref, db_ref):
        bb = pl.program_id(1)
        w = [w_ref[k:k + 1, :] for k in range(taps)]
        bias = b_ref[...]

        def fold(v):
            acc = v[0:SUBLANES]
            for i in range(1, CONV_ROWS // SUBLANES):
                acc = acc + v[i * SUBLANES:(i + 1) * SUBLANES]
            return acc

        def step(c, carry):
            r0 = pl.multiple_of(c * CONV_ROWS, CONV_ROWS)
            xe = _conv_window(x_ref, r0, CONV_ROWS, 0, tc, s, True, True)
            pre = _conv_taps_pre(xe, w, bias, taps, ext)
            sig = _sigmoid(pre)
            d = _conv_window(dy_ref, r0, CONV_ROWS, 0, tc, s, False, True)
            dsil = d * _conv_window(x_ref, r0, CONV_ROWS, tc, tc, s, False, True) if gated else d
            dpre = dsil * sig * (1.0 + pre * (1.0 - sig))
            dx = dpre[:CONV_ROWS] * w[taps - 1]
            for j in range(1, taps):
                dx = dx + pltpu.roll(dpre, ext - j, 0)[:CONV_ROWS] * w[taps - 1 - j]
            dx_ref[0, pl.ds(r0, CONV_ROWS), pl.ds(0, tc)] = dx.astype(dx_ref.dtype)
            if gated:
                dx_ref[0, pl.ds(r0, CONV_ROWS), pl.ds(tc, tc)] = (d[:CONV_ROWS] * (pre * sig)[:CONV_ROWS]).astype(dx_ref.dtype)
            here = dpre[:CONV_ROWS]
            sums = [fold(here * (pltpu.roll(xe, taps - 1 - k, 0) if k < taps - 1 else xe)[SUBLANES:ext]) + carry[k] for k in range(taps)]
            return tuple(sums) + (fold(here) + carry[taps],)

        zero = jnp.zeros((SUBLANES, tc), F32)
        sums = lax.fori_loop(0, s // CONV_ROWS, step, (zero,) * (taps + 1), unroll=2)
        rows = [jnp.sum(t, axis=0, keepdims=True) for t in sums]
        dw = jnp.concatenate(rows[:taps] + [jnp.zeros_like(rows[0])] * (SUBLANES - taps), axis=0)

        @pl.when(bb == 0)
        def _():
            dw_ref[...] = dw
            db_ref[...] = rows[taps]

        @pl.when(bb > 0)
        def _():
            dw_ref[...] += dw
            db_ref[...] += rows[taps]

    first = 0 if gated else col0 // tc
    out_shape = [_sds((bl, s, chans * (2 if gated else 1)), dx_dtype), _sds((SUBLANES, chans), F32), _sds((1, chans), F32)]
    return _pcall(
        body, name, out_shape, grid=(chans // tc, bl),
        in_specs=[
            pl.BlockSpec((1, s, xw), lambda j, bb: (bb, 0, first + j)), pl.BlockSpec((SUBLANES, tc), lambda j, bb: (0, j)),
            pl.BlockSpec((1, tc), lambda j, bb: (0, j)), pl.BlockSpec((1, s, tc), lambda j, bb: (bb, 0, j)),
        ],
        out_specs=[pl.BlockSpec((1, s, xw), lambda j, bb: (bb, 0, j)), pl.BlockSpec((SUBLANES, tc), lambda j, bb: (0, j)), pl.BlockSpec((1, tc), lambda j, bb: (0, j))],
        sem=("parallel", "arbitrary"),
    )(src, w8, b, dy)


def _gate_value_blocks(name, w, f, inverse=False):
    nb = f // CONV_LANES
    src = (lambda j: (0, 2 * (j % nb) + j // nb)) if inverse else (lambda j: (0, (j % 2) * nb + j // 2))

    def body(x_ref, o_ref):
        o_ref[...] = x_ref[...]

    return _pcall(
        body, name, _sds(w.shape, w.dtype), grid=(2 * nb,), in_specs=[pl.BlockSpec((w.shape[0], CONV_LANES), src)],
        out_specs=pl.BlockSpec((w.shape[0], CONV_LANES), lambda j: (0, j)), sem=("parallel",),
    )(w)


def _ssd_step(states, xs, bm, cm, dtr, z, dtb, alog, dsk, ng):
    gw = xs.shape[1] // SSD_GROUPS
    hpg, npair = gw // HEAD_DIM, gw // LANES
    dt = jax.nn.softplus(dtr + dtb)
    da = dt * (-jnp.exp(alog))
    causal = _causal(CHUNK)
    acs = _sel_left(causal.astype(F32), da)
    acs_t = acs.T
    lane = lax.broadcasted_iota(jnp.int32, (1, LANES), 1)
    sub = lax.broadcasted_iota(jnp.int32, (LANES, 1), 0)
    last = lax.broadcasted_iota(jnp.int32, (CHUNK, gw), 0) == CHUNK - 1
    outs, new_states = [], []
    for g in range(SSD_GROUPS):
        xg, zg = xs[:, g * gw:(g + 1) * gw], z[:, g * gw:(g + 1) * gw]
        bg, cg = bm[:, g * STATE:(g + 1) * STATE], cm[:, g * STATE:(g + 1) * STATE]
        head_of_col = g * hpg + lax.broadcasted_iota(jnp.int32, (LANES, gw), 1) // HEAD_DIM
        expand = (lax.broadcasted_iota(jnp.int32, (LANES, gw), 0) == head_of_col).astype(F32)
        dt_e, acs_e = _sel_right(dt, expand), _sel_right(acs, expand)
        alast_e = jnp.sum(jnp.where(last, acs_e, 0.0), axis=0, keepdims=True)
        xc = xg * dt_e
        xc_st = xc * jnp.exp(alast_e - acs_e)
        decay_out, chunk_decay = jnp.exp(acs_e), jnp.exp(alast_e)
        cb = _dot_nt(cg, bg)
        ys = []
        for p in range(npair):
            sl = slice(p * LANES, (p + 1) * LANES)
            state = states[g * npair + p]
            y = _dot_nn(cg, state) * decay_out[:, sl]
            for q in range(2):
                head = g * hpg + 2 * p + q
                col = jnp.sum(acs * (lane == head).astype(F32), axis=1, keepdims=True)
                row = jnp.sum(acs_t * (sub == head).astype(F32), axis=0, keepdims=True)
                decay = jnp.where(causal, jnp.exp(jnp.where(causal, col - row, 0.0)), 0.0)
                y = y + _dot_nn(cb * decay, xc[:, sl] * ((lane // HEAD_DIM) == q).astype(F32))
            ys.append(y)
            new_states.append(state * chunk_decay[:, sl] + _dot_tn(bg, xc_st[:, sl]))
        y = jnp.concatenate(ys, axis=1) + dsk[:, g * gw:(g + 1) * gw] * xg
        outs.append(_rms(y * (zg * _sigmoid(zg)), ng[:, g * gw:(g + 1) * gw]))
    return tuple(new_states), jnp.concatenate(outs, axis=1)


def _ssd_scan_specs(d, dt_col, nc, rev):
    ci = (lambda i: nc - 1 - i) if rev else (lambda i: i)
    bc = SSD_GROUPS * STATE
    return [
        pl.BlockSpec((1, CHUNK, d), lambda b, i: (b, ci(i), 0)),
        pl.BlockSpec((1, CHUNK, bc), lambda b, i: (b, ci(i), d // bc)),
        pl.BlockSpec((1, CHUNK, bc), lambda b, i: (b, ci(i), d // bc + 1)),
        pl.BlockSpec((1, CHUNK, LANES), lambda b, i: (b, ci(i), dt_col // LANES)),
        pl.BlockSpec((1, CHUNK, d), lambda b, i: (b, ci(i), 0)),
        pl.BlockSpec((1, LANES), lambda b, i: (0, 0)), pl.BlockSpec((1, LANES), lambda b, i: (0, 0)),
        pl.BlockSpec((1, d), lambda b, i: (0, 0)), pl.BlockSpec((1, d), lambda b, i: (0, 0)),
    ]


def _ssd_scan_fwd(name, xbc, proj, dt_col, dtb, alog, dsk, ng):
    bl, s, _ = xbc.shape
    d = dsk.shape[1]
    nc, nst = s // CHUNK, d // LANES

    def body(xs_ref, bm_ref, cm_ref, dt_ref, z_ref, dtb_ref, alog_ref, dsk_ref, ng_ref, y_ref, hp_ref, st_ref):
        @pl.when(pl.program_id(1) == 0)
        def _():
            st_ref[...] = jnp.zeros_like(st_ref)

        states = tuple(st_ref[p] for p in range(nst))
        hp_ref[0, 0] = st_ref[...]
        new_states, yn = _ssd_step(states, xs_ref[0], bm_ref[0], cm_ref[0], dt_ref[0], z_ref[0], dtb_ref[...], alog_ref[...], dsk_ref[...], ng_ref[...])
        for p in range(nst):
            st_ref[p] = new_states[p]
        y_ref[0] = yn.astype(y_ref.dtype)

    return _pcall(
        body, name, [_sds((bl, s, d), BF16), _sds((bl, nc, nst, STATE, LANES), F32)], grid=(bl, nc), in_specs=_ssd_scan_specs(d, dt_col, nc, False),
        out_specs=[pl.BlockSpec((1, CHUNK, d), lambda b, i: (b, i, 0)), pl.BlockSpec((1, 1, nst, STATE, LANES), lambda b, i: (b, i, 0, 0, 0))],
        scratch=[pltpu.VMEM((nst, STATE, LANES), F32)], sem=("arbitrary", "arbitrary"),
    )(xbc, xbc, xbc, proj, proj, dtb, alog, dsk, ng)


def _ssd_scan_bwd(name, xbc, proj, dt_col, dtb, alog, dsk, ng, hprev, dy):
    bl, s, _ = xbc.shape
    d = dsk.shape[1]
    nc, nst, bc = s // CHUNK, d // LANES, SSD_GROUPS * STATE

    def body(xs_ref, bm_ref, cm_ref, dt_ref, z_ref, dtb_ref, alog_ref, dsk_ref, ng_ref, hp_ref, dy_ref,
             dxs_ref, dbm_ref, dcm_ref, ddt_ref, dz_ref, ddtb_ref, dalog_ref, ddsk_ref, dng_ref, dst_ref):
        b, i = pl.program_id(0), pl.program_id(1)

        @pl.when(i == 0)
        def _():
            dst_ref[...] = jnp.zeros_like(dst_ref)

        states = tuple(hp_ref[0, 0, p] for p in range(nst))
        _, vjp = jax.vjp(_ssd_step, states, xs_ref[0], bm_ref[0], cm_ref[0], dt_ref[0], z_ref[0], dtb_ref[...], alog_ref[...], dsk_ref[...], ng_ref[...])
        d_states, dxs, dbm, dcm, ddt, dz, ddtb, dalog, ddsk, dng = vjp((tuple(dst_ref[p] for p in range(nst)), dy_ref[0].astype(F32)))
        for p in range(nst):
            dst_ref[p] = d_states[p]
        dxs_ref[0], dbm_ref[0], dcm_ref[0], ddt_ref[0] = dxs, dbm, dcm, ddt
        dz_ref[0] = dz.astype(dz_ref.dtype)
        first = (b == 0) & (i == 0)
        for o_ref, val in ((ddtb_ref, ddtb), (dalog_ref, dalog), (ddsk_ref, ddsk), (dng_ref, dng)):
            @pl.when(first)
            def _(o_ref=o_ref, val=val):
                o_ref[...] = val

            @pl.when(jnp.logical_not(first))
            def _(o_ref=o_ref, val=val):
                o_ref[...] += val

    rc = lambda i: nc - 1 - i
    in_specs = _ssd_scan_specs(d, dt_col, nc, True) + [
        pl.BlockSpec((1, 1, nst, STATE, LANES), lambda b, i: (b, rc(i), 0, 0, 0)), pl.BlockSpec((1, CHUNK, d), lambda b, i: (b, rc(i), 0)),
    ]
    out_shape = [
        _sds((bl, s, d), F32), _sds((bl, s, bc), F32), _sds((bl, s, bc), F32), _sds((bl, s, LANES), F32), _sds((bl, s, d), BF16),
        _sds((1, LANES), F32), _sds((1, LANES), F32), _sds((1, d), F32), _sds((1, d), F32),
    ]
    row = lambda w: pl.BlockSpec((1, CHUNK, w), lambda b, i: (b, rc(i), 0))
    whole = lambda w: pl.BlockSpec((1, w), lambda b, i: (0, 0))
    out_specs = [row(d), row(bc), row(bc), row(LANES), row(d), whole(LANES), whole(LANES), whole(d), whole(d)]
    return _pcall(
        body, name, out_shape, grid=(bl, nc), in_specs=in_specs, out_specs=out_specs,
        scratch=[pltpu.VMEM((nst, STATE, LANES), F32)], sem=("arbitrary", "arbitrary"),
    )(xbc, xbc, xbc, proj, proj, dtb, alog, dsk, ng, hprev, dy)


def _mixer_fwd(name, proj, d, cd, conv_w8, conv_b, dtb, alog, dsk, ng, vg, ws, bs_t, og):
    xbc = _conv_fwd(name + "_conv_fwd", proj, 3 * d, cd, conv_w8, conv_b, SSD_CONV, False, F32)
    y, hprev = _ssd_scan_fwd(name + "_ssd_fwd", xbc, proj, 3 * d + cd, dtb, alog, dsk, ng)
    rows = [(proj, d, 1, BF16), (proj, d, 2, BF16)]
    (g_out,) = _row_fwd(name + "_gm_fwd", _gm_tile, rows, [], [vg, ws, bs_t, og], [(d, BF16)], 0, CHUNK)
    return jnp.concatenate([y, g_out], axis=-1), (xbc, hprev)


def _mixer_bwd(name, proj, d, cd, conv_w8, conv_b, dtb, alog, dsk, ng, vg, ws, bs_t, og, xbc, hprev, dycat):
    bl, s, n_proj = proj.shape
    dt_col = 3 * d + cd
    dy, dg_out = dycat[..., :d], dycat[..., d:]
    dxs, dbm, dcm, ddt, dz, ddtb, dalog, ddsk, dng = _ssd_scan_bwd(name + "_ssd_bwd", xbc, proj, dt_col, dtb, alog, dsk, ng, hprev, dy)
    dxbc_act = jnp.concatenate([dxs, dbm, dcm], axis=-1)
    dxbc, dw8, dcb = _conv_bwd(name + "_conv_bwd", proj, 3 * d, cd, conv_w8, conv_b, SSD_CONV, False, dxbc_act, BF16)
    rows = [(proj, d, 1, BF16), (proj, d, 2, BF16)]
    (du, dv), _, (dvg, dws, dbs_t, dog) = _row_bwd(name + "_gm_bwd", _gm_tile, rows, [], [vg, ws, bs_t, og], [(d, BF16)], 0, CHUNK, [dg_out])
    pad = jnp.zeros((bl, s, n_proj - dt_col - LANES), BF16)
    dproj = jnp.concatenate([dz, du, dv, dxbc, ddt.astype(BF16), pad], axis=-1)
    return dproj, (dw8, dcb, ddtb, dalog, ddsk, dng, dvg, dws, dbs_t, dog)


def _position():
    return lax.axis_index("x"), lax.axis_index("y"), lax.axis_index("c")


def _at(ref, idx):
    return ref.at[idx] if len(idx) else ref


def _exchange(name, inputs, out_shapes, plan, inplace=False, after=None):
    if inplace:
        out_shapes = [(a.shape, a.dtype) for a in inputs]
    n_in, n_out = len(inputs), len(out_shapes)
    n_after = 0 if after is None else 1
    n_copy = len(plan(0, 0, 0))

    def body(*refs):
        in_refs, out_refs, token = refs[:n_in], refs[n_in + n_after:n_in + n_after + n_out], refs[n_in + n_after + n_out]
        send_sems, recv_sems = refs[n_in + n_after + n_out + 1:]
        token[...] = jnp.zeros_like(token)
        x, y, c = _position()
        copies = plan(x, y, c)

        def copy(k, src, dst, peer):
            return pltpu.make_async_remote_copy(src_ref=src, dst_ref=dst, send_sem=send_sems.at[k], recv_sem=recv_sems.at[k], device_id=peer, device_id_type=MESH)

        src_refs = out_refs if inplace else in_refs
        sends = [copy(k, _at(src_refs[sa], si), _at(out_refs[da], di), peer) for k, (sa, si, da, di, peer, _) in enumerate(copies)]
        for cp in sends:
            cp.start()
        for k, (sa, si, da, _, peer, li) in enumerate(copies):
            copy(k, _at(src_refs[sa], si), _at(out_refs[da], li), peer).wait_recv()
        for cp in sends:
            cp.wait_send()

    any_spec = pl.BlockSpec(memory_space=pl.ANY)
    outs = pl.pallas_call(
        body, name=name, out_shape=[_sds(s, dt) for s, dt in out_shapes] + [_sds((SUBLANES, LANES), F32)], in_specs=[any_spec] * (n_in + n_after),
        out_specs=[any_spec] * n_out + [pl.BlockSpec(memory_space=pltpu.VMEM)],
        scratch_shapes=[pltpu.SemaphoreType.DMA((n_copy,)), pltpu.SemaphoreType.DMA((n_copy,))],
        input_output_aliases={i: i for i in range(n_in)} if inplace else {},
    )(*inputs, *([] if after is None else [after]))
    return list(outs[:n_out]), outs[n_out]


def _exchange_start(name, inputs, out_shapes, plan):
    n_in, n_out = len(inputs), len(out_shapes)
    n_copy = len(plan(0, 0, 0))

    def body(*refs):
        in_refs, land_refs = refs[:n_in], refs[n_in:n_in + n_out]
        send_sems, recv_sems = refs[n_in + n_out:n_in + n_out + 2]
        token = refs[-1]
        x, y, c = _position()
        for k, (sa, si, da, di, peer, _) in enumerate(plan(x, y, c)):
            pltpu.make_async_remote_copy(
                src_ref=_at(in_refs[sa], si), dst_ref=_at(land_refs[da], di), send_sem=send_sems.at[k], recv_sem=recv_sems.at[k],
                device_id=peer, device_id_type=MESH).start()
        token[...] = jnp.zeros_like(token)

    hbm, sem = pl.BlockSpec(memory_space=pltpu.HBM), pl.BlockSpec(memory_space=pltpu.SEMAPHORE)
    lands = [lax.empty(s, dt) for s, dt in out_shapes]
    args = [pltpu.with_memory_space_constraint(a, pltpu.HBM) for a in list(inputs) + lands]
    outs = pl.pallas_call(
        body, name=name,
        out_shape=(pltpu.SemaphoreType.DMA((n_copy,)), pltpu.SemaphoreType.DMA((n_copy,)), *[pltpu.HBM(a.shape, a.dtype) for a in args], _sds((SUBLANES, LANES), F32)),
        in_specs=[hbm] * (n_in + n_out), out_specs=(sem, sem, *[hbm] * (n_in + n_out), pl.BlockSpec(memory_space=pltpu.VMEM)),
        input_output_aliases={i: 2 + i for i in range(n_in + n_out)},
        compiler_params=pltpu.CompilerParams(has_side_effects=pltpu.SideEffectType.DATAFLOW_SIDE_EFFECTING),
    )(*args)
    return dict(name=name, plan=plan, sems=outs[:2], ins=list(outs[2:2 + n_in]), lands=list(outs[2 + n_in:2 + n_in + n_out]), token=outs[-1])


def _exchange_wait(started, after):
    plan, n_in, n_out = started["plan"], len(started["ins"]), len(started["lands"])

    def body(*refs):
        in_refs, land_refs = refs[:n_in], refs[n_in:n_in + n_out]
        send_sems, recv_sems = refs[n_in + n_out:n_in + n_out + 2]
        token = refs[-1]
        x, y, c = _position()
        for k, (sa, si, da, _, peer, li) in enumerate(plan(x, y, c)):
            cp = pltpu.make_async_remote_copy(
                src_ref=_at(in_refs[sa], si), dst_ref=_at(land_refs[da], li), send_sem=send_sems.at[k], recv_sem=recv_sems.at[k],
                device_id=peer, device_id_type=MESH)
            cp.wait_send()
            cp.wait_recv()
        token[...] = jnp.zeros_like(token)

    hbm, sem = pl.BlockSpec(memory_space=pltpu.HBM), pl.BlockSpec(memory_space=pltpu.SEMAPHORE)
    bufs = started["ins"] + started["lands"]
    outs = pl.pallas_call(
        body, name=started["name"] + "_wait", out_shape=(*[pltpu.HBM(a.shape, a.dtype) for a in bufs], _sds((SUBLANES, LANES), F32)),
        in_specs=[hbm] * len(bufs) + [sem, sem, pl.BlockSpec(memory_space=pl.ANY)], out_specs=(*[hbm] * len(bufs), pl.BlockSpec(memory_space=pltpu.VMEM)),
        input_output_aliases={i: i for i in range(len(bufs))},
        compiler_params=pltpu.CompilerParams(has_side_effects=pltpu.SideEffectType.DATAFLOW_SIDE_EFFECTING),
    )(*bufs, *started["sems"], after)
    return list(outs[:n_in]), list(outs[n_in:n_in + n_out]), outs[-1]


def _after(value, token):
    return value + token[0, 0].astype(value.dtype)


def _chip_peers(x, y, c):
    return [(1 - x, y, c), (x, 1 - y, c), (1 - x, 1 - y, c)]


def _chip_of(p):
    return 2 * p[0] + p[1]


def _set_slot(slots, me, blk):
    return lax.dynamic_update_slice(slots, blk[None], (me,) + (0,) * blk.ndim)


def _by_core(c, mine, other, axis):
    return jnp.where(c == 0, jnp.stack([mine, other], axis), jnp.stack([other, mine], axis))


def _plan_gather_chips(n):
    def plan(x, y, c):
        return [(a, (), a, (2 * x + y,), p, (_chip_of(p),)) for a in range(n) for p in _chip_peers(x, y, c)]

    return plan


def _gather_chips(name, blocks):
    recv, token = _exchange(name, blocks, [((4,) + b.shape, b.dtype) for b in blocks], _plan_gather_chips(len(blocks)))
    x, y, _ = _position()
    return [_set_slot(r, 2 * x + y, b) for r, b in zip(recv, blocks)], token


def _gather_pass_cores(name, blocks, from_chips):
    n = len(blocks)

    def plan_cores(x, y, c):
        me, sib = 2 * x + y, (x, y, 1 - c)
        own = [(a, (), a, (me,), sib, (me,)) for a in range(n)]
        passed = [(n + a, (_chip_of(p),), a, (_chip_of(p),), sib, (_chip_of(p),)) for a in range(n) for p in _chip_peers(x, y, c)]
        return own + passed

    from_core, token = _exchange(name + "_cores", list(blocks) + list(from_chips), [((4,) + b.shape, b.dtype) for b in blocks], plan_cores)
    x, y, c = _position()
    return [_by_core(c, _set_slot(r1, 2 * x + y, b), r2, 1) for b, r1, r2 in zip(blocks, from_chips, from_core)], token


def _gather_two_level(name, blocks):
    from_chips, _ = _exchange(name + "_chips", blocks, [((4,) + b.shape, b.dtype) for b in blocks], _plan_gather_chips(len(blocks)))
    return _gather_pass_cores(name, blocks, from_chips)


def _pair_add(name, g42, r4):
    _, _, rh, cols = g42.shape
    tr = _divisors(rh, 512, SUBLANES * 2)[-1]

    def body(c_ref, a_ref, b_ref, o_ref):
        o_ref[0] = (a_ref[0, 0].astype(F32) + b_ref[0].astype(F32)).astype(o_ref.dtype)

    cidx = lax.axis_index("c").astype(jnp.int32).reshape(1)
    return _pcall(
        body, name, _sds(r4.shape, BF16), grid=(4, rh // tr),
        in_specs=[pl.BlockSpec((1, 1, tr, cols), lambda s, i, c_ref: (s, c_ref[0], i, 0)), pl.BlockSpec((1, tr, cols), lambda s, i, c_ref: (s, i, 0))],
        out_specs=pl.BlockSpec((1, tr, cols), lambda s, i, c_ref: (s, i, 0)), sem=("parallel", "parallel"), prefetch=1,
    )(cidx, g42, r4)


def _slot_sum(name, parts):
    n, r, cols = parts.shape
    cap = max(2 * SUBLANES, (4 * 1024 * 1024) // (n * cols * parts.dtype.itemsize))
    tr = _divisors(r, cap, 2 * SUBLANES)[-1]

    def body(p_ref, o_ref):
        acc = p_ref[0].astype(F32)
        for k in range(1, n):
            acc = acc + p_ref[k].astype(F32)
        o_ref[...] = acc

    return _pcall(
        body, name, _sds((r, cols), F32), grid=(r // tr,), in_specs=[pl.BlockSpec((n, tr, cols), lambda i: (0, i, 0))],
        out_specs=pl.BlockSpec((tr, cols), lambda i: (i, 0)), sem=("parallel",),
    )(parts)


def _slot_sums(name, parts):
    k = len(parts)

    def body(*refs):
        for p_ref, o_ref in zip(refs[:k], refs[k:]):
            acc = p_ref[0].astype(F32)
            for j in range(1, p_ref.shape[0]):
                acc = acc + p_ref[j].astype(F32)
            o_ref[...] = acc

    return list(_pcall(body, name, [_sds(p.shape[1:], F32) for p in parts])(*parts))


def _gather_weights_start(name, shards, token):
    c = lax.axis_index("c")
    halves = [lax.dynamic_slice_in_dim(w, c * (w.shape[0] // 2), w.shape[0] // 2, 0).astype(BF16) for w in shards]
    if token is not None:
        halves[0] = _after(halves[0], token)
    n = len(halves)

    def plan(x, y, c):
        return [(a, (), a, (2 * x + y, c), p, (_chip_of(p), c)) for a in range(n) for p in _chip_peers(x, y, c)]

    return _exchange_start(name + "_ag_chips", halves, [((4, 2) + h.shape, h.dtype) for h in halves], plan)


def _gather_weights_finish(name, started, after):
    halves, slots, token = _exchange_wait(started, after)
    n = len(halves)
    x, y, c = _position()
    slots = [lax.dynamic_update_slice(s, h[None, None], (2 * x + y, c, 0, 0)) for s, h in zip(slots, halves)]

    def plan_cores(x, y, c):
        return [(a, (s, c), a, (s, c), (x, y, 1 - c), (s, 1 - c)) for a in range(n) for s in range(4)]

    full, _ = _exchange(name + "_ag_cores", slots, None, plan_cores, inplace=True)
    return [f.reshape((4, 2 * h.shape[0], h.shape[1])) for f, h in zip(full, halves)], token


def _reduce_weights_start(name, grads, token):
    n = len(grads)
    g42 = [g.reshape(4, 2, g.shape[1] // 2, g.shape[2]) for g in grads]

    def plan_swap(x, y, c):
        return [(a, (s, 1 - c), a, (s,), (x, y, 1 - c), (s,)) for a in range(n) for s in range(4)]

    def plan_chips(x, y, c):
        return [(a, (_chip_of(p),), a, (2 * x + y,), p, (_chip_of(p),)) for a in range(n) for p in _chip_peers(x, y, c)]

    other, _ = _exchange(name + "_rs_cores", g42, [((4,) + g.shape[2:], g.dtype) for g in g42], plan_swap, after=token)
    pair = [_pair_add(f"{name}_rs_pair{a}", g, o) for a, (g, o) in enumerate(zip(g42, other))]
    return _exchange_start(name + "_rs_chips", pair, [(p.shape, p.dtype) for p in pair], plan_chips)


def _reduce_weights_finish(name, starteds, after):
    pair, recv = [], []
    for started in starteds:
        p, r, token = _exchange_wait(started, after)
        pair, recv = pair + p, recv + r
    n = len(pair)
    x, y, c = _position()
    me = 2 * x + y

    def plan_share(x, y, c):
        return [(a, (c,), a, (c,), (x, y, 1 - c), (1 - c,)) for a in range(n)]

    parts = [lax.dynamic_update_slice(r, lax.dynamic_slice_in_dim(p, me, 1, 0), (me, 0, 0)) for r, p in zip(recv, pair)]
    mine = [_slot_sum(f"{name}_rs_sum{a}", p) for a, p in enumerate(parts)]
    both = [lax.dynamic_update_slice(lax.empty((2,) + m.shape, m.dtype), m[None], (c, 0, 0)) for m in mine]
    both, _ = _exchange(name + "_rs_share", both, None, plan_share, inplace=True)
    return [b.reshape(2 * b.shape[1], b.shape[2]) for b in both], token


def _allreduce_small(name, grads):
    both, token = _gather_two_level(name + "_ag", [g.astype(BF16) if g.size >= SMALL_BF16_SIZE else g for g in grads])
    return _slot_sums(name + "_sum", [g.reshape((8,) + g.shape[2:]) for g in both]), token


def _ada_fwd_call(name, c_all, w, b_shard):
    nl, d, ns = w.shape
    nb = c_all.shape[0]

    def body(c_ref, w_ref, b_ref, o_ref):
        cv = c_ref[...]
        o_ref[0] = _dg(cv * _sigmoid(cv), w_ref[0], ((1,), (0,))) + b_ref[0]

    return _pcall(
        body, name, _sds((nl, nb, ns), F32), grid=(nl,),
        in_specs=[pl.BlockSpec((nb, d), lambda l: (0, 0)), pl.BlockSpec((1, d, ns), lambda l: (l, 0, 0)), pl.BlockSpec((1, 1, ns), lambda l: (l, 0, 0))],
        out_specs=pl.BlockSpec((1, nb, ns), lambda l: (l, 0, 0)), sem=("parallel",),
    )(c_all, w, b_shard)


def _ada_bwd_call(name, c_all, dm_shard, dm_all):
    nl, nb, ns = dm_shard.shape
    d = c_all.shape[1]
    nm = dm_all.shape[2]

    def body(c_ref, ds_ref, da_ref, dw_ref, db_ref):
        cv = c_ref[...]
        dw_ref[0] = _dg(cv * _sigmoid(cv), ds_ref[0], ((0,), (0,)))
        db_ref[0] = jnp.sum(da_ref[0], axis=0, keepdims=True)

    return _pcall(
        body, name, [_sds((nl, d, ns), F32), _sds((nl, 1, nm), F32)], grid=(nl,),
        in_specs=[pl.BlockSpec((nb, d), lambda l: (0, 0)), pl.BlockSpec((1, nb, ns), lambda l: (l, 0, 0)), pl.BlockSpec((1, nb, nm), lambda l: (l, 0, 0))],
        out_specs=[pl.BlockSpec((1, d, ns), lambda l: (l, 0, 0)), pl.BlockSpec((1, 1, nm), lambda l: (l, 0, 0))], sem=("parallel",),
    )(c_all, dm_shard, dm_all)


def _ada_fwd(name, bl, c_all, w, b):
    nl, d, ns = w.shape
    chip = 2 * lax.axis_index("x") + lax.axis_index("y")
    b_shard = lax.dynamic_slice(b, (0, chip * ns), (nl, ns)).reshape(nl, 1, ns)
    shard = _ada_fwd_call(name + "_fwd", c_all, w, b_shard)
    (allc,), token = _gather_chips(name + "_ag", [shard])
    mods = jnp.transpose(allc, (1, 2, 0, 3)).reshape(nl, c_all.shape[0], 4 * ns)
    return lax.dynamic_slice(mods, (0, (2 * chip + lax.axis_index("c")) * bl, 0), (nl, bl, 4 * ns)), token


def _ada_bwd(name, bl, c_all, ns, dm):
    nl = dm.shape[0]
    chip = 2 * lax.axis_index("x") + lax.axis_index("y")
    (dm_all,), token = _gather_two_level(name + "_bwd_ag", [dm])
    dm_all = jnp.transpose(dm_all.reshape((8,) + dm.shape), (1, 0, 2, 3)).reshape(nl, 8 * bl, 4 * ns)
    dm_shard = lax.dynamic_slice(dm_all, (0, 0, chip * ns), (nl, 8 * bl, ns))
    dw, db = _ada_bwd_call(name + "_bwd", c_all, dm_shard, dm_all)
    return dw, db.reshape(nl, 4 * ns), token


def _adamw(name, w, g, m, v):
    shape = w.shape
    cols = shape[-1]
    w2, g2, m2, v2 = (t.reshape(-1, cols) for t in (w, g, m, v))
    rows = w2.shape[0]
    cap = max(SUBLANES, (512 * 1024) // max(cols, 1) // SUBLANES * SUBLANES)
    tr = _divisors(rows, cap, SUBLANES)[-1]

    def body(w_ref, g_ref, m_ref, v_ref, d_ref, mo_ref, vo_ref):
        gv = g_ref[...]
        mn = ADAM_B1 * m_ref[...] + (1.0 - ADAM_B1) * gv
        vn = ADAM_B2 * v_ref[...] + (1.0 - ADAM_B2) * (gv * gv)
        m_hat = mn / (1.0 - ADAM_B1 ** ADAM_STEP)
        v_hat = vn / (1.0 - ADAM_B2 ** ADAM_STEP)
        d_ref[...] = -ADAM_LR * (m_hat / (jnp.sqrt(v_hat) + ADAM_EPS) + ADAM_WD * w_ref[...])
        mo_ref[...] = mn
        vo_ref[...] = vn

    spec = pl.BlockSpec((tr, cols), lambda i: (i, 0))
    outs = _pcall(body, name, [_sds((rows, cols), F32)] * 3, grid=(rows // tr,), in_specs=[spec] * 4, out_specs=[spec] * 3, sem=("parallel",))(w2, g2, m2, v2)
    return tuple(o.reshape(shape) for o in outs)


def _adamw_small(name, ws, gs, ms, vs):
    n = len(ws)

    def body(*refs):
        for k in range(n):
            w_ref, g_ref, m_ref, v_ref = (refs[j * n + k] for j in range(4))
            d_ref, mo_ref, vo_ref = (refs[(4 + j) * n + k] for j in range(3))
            gv = g_ref[...]
            mn = ADAM_B1 * m_ref[...] + (1.0 - ADAM_B1) * gv
            vn = ADAM_B2 * v_ref[...] + (1.0 - ADAM_B2) * (gv * gv)
            m_hat = mn / (1.0 - ADAM_B1 ** ADAM_STEP)
            v_hat = vn / (1.0 - ADAM_B2 ** ADAM_STEP)
            d_ref[...] = -ADAM_LR * (m_hat / (jnp.sqrt(v_hat) + ADAM_EPS) + ADAM_WD * w_ref[...])
            mo_ref[...] = mn
            vo_ref[...] = vn

    outs = _pcall(body, name, [_sds(w.shape, F32) for w in ws] * 3)(*ws, *gs, *ms, *vs)
    return outs[:n], outs[n:2 * n], outs[2 * n:]


def _pad_rows(w, rows):
    return jnp.pad(w, ((0, rows - w.shape[0]), (0, 0)))


def _pad_lanes(v):
    return jnp.pad(v, (0, LANES - v.shape[0])).reshape(1, LANES)


BIG = ("w_in", "w_out", "ff_up", "ff_down")
BIG_AXIS = {"w_in": 1, "w_out": 0, "ff_up": 1, "ff_down": 0}
CONVW = ("ssd_conv_w", "ff_conv_w")
SMALL = ("norm1_g", "norm2_g", "ssd_conv_b", "ssd_dt_bias", "ssd_a_log", "ssd_d", "ssd_norm_g", "gm_vnorm_g", "gm_ws", "gm_bs", "gm_out_g", "ff_conv_b")
WEIGHTS = ("ada_w", "ada_b", "norm1_g", "norm2_g", "w_in", "ssd_conv_w", "ssd_conv_b", "ssd_dt_bias", "ssd_a_log", "ssd_d", "ssd_norm_g", "gm_vnorm_g", "gm_ws", "gm_bs", "gm_out_g", "w_out", "ff_up", "ff_conv_w", "ff_conv_b", "ff_down", "final_g")


def kernel(x, c, ada_w, ada_b, norm1_g, norm2_g, w_in, ssd_conv_w, ssd_conv_b, ssd_dt_bias, ssd_a_log, ssd_d, ssd_norm_g, gm_vnorm_g, gm_ws, gm_bs, gm_out_g, w_out, ff_up, ff_conv_w, ff_conv_b, ff_down, final_g, loss_target, m_ada_w, m_ada_b, m_norm1_g, m_norm2_g, m_w_in, m_ssd_conv_w, m_ssd_conv_b, m_ssd_dt_bias, m_ssd_a_log, m_ssd_d, m_ssd_norm_g, m_gm_vnorm_g, m_gm_ws, m_gm_bs, m_gm_out_g, m_w_out, m_ff_up, m_ff_conv_w, m_ff_conv_b, m_ff_down, m_final_g, v_ada_w, v_ada_b, v_norm1_g, v_norm2_g, v_w_in, v_ssd_conv_w, v_ssd_conv_b, v_ssd_dt_bias, v_ssd_a_log, v_ssd_d, v_ssd_norm_g, v_gm_vnorm_g, v_gm_ws, v_gm_bs, v_gm_out_g, v_w_out, v_ff_up, v_ff_conv_w, v_ff_conv_b, v_ff_down, v_final_g):
    given = dict(locals())
    weights = {n: given[n] for n in WEIGHTS}
    bl, s, d = x.shape
    nl = ada_w.shape[0]
    heads = d // HEAD_DIM
    cd = d + 2 * SSD_GROUPS * STATE
    f = ff_down.shape[1] * 4
    n_in = d + cd + heads + 2 * d
    n_proj = _round_up(3 * d + cd + LANES, 2 * LANES)
    tm = _divisors(s, 512)[-1]

    gathering = {}

    def start_gathers(l, names, token):
        for n in names:
            gathering[l, n] = _gather_weights_start(f"wg{l}_{n}", [weights[n][l]], token)
            token = gathering[l, n]["token"]
        return token

    pre, _ = _gather_two_level("pre_ag", [c] + [weights[n] for n in CONVW])
    c_all = pre[0].reshape(8 * bl, d)
    conv_full = {n: jnp.concatenate([p[k, 0] for k in range(4)], axis=-1) for n, p in zip(CONVW, pre[1:])}
    mods_all, ada_token = _ada_fwd("ada", bl, c_all, ada_w, ada_b)
    mods = mods_all.reshape(nl, bl, N_MOD, 1, d)

    def landed(l, n, after):
        (full,), token = _gather_weights_finish(f"wg{l}_{n}", gathering.pop((l, n)), after)
        if n == "w_in":
            return full, token
        return (full.reshape(-1, full.shape[2]) if BIG_AXIS[n] == 0 else jnp.concatenate([full[k] for k in range(4)], axis=1)), token

    issued = start_gathers(0, BIG, ada_token)
    r2 = lambda v: v.reshape(1, -1)
    n_gm = d // GM_HEAD
    fg = r2(final_g)

    def layer_params(l):
        return dict(
            norm1=r2(norm1_g[l]), norm2=r2(norm2_g[l]), conv_w8=_pad_rows(conv_full["ssd_conv_w"][l], SUBLANES), conv_b=r2(ssd_conv_b[l]),
            dtb=_pad_lanes(ssd_dt_bias[l]), alog=_pad_lanes(ssd_a_log[l]), dsk=r2(jnp.repeat(ssd_d[l], HEAD_DIM)), ng=r2(ssd_norm_g[l]),
            vg=r2(gm_vnorm_g[l]), ws=gm_ws[l], bs_t=jnp.pad(gm_bs[l].T, ((0, 0), (0, LANES - n_gm))), og=r2(gm_out_g[l]),
            ff_w8=_pad_rows(conv_full["ff_conv_w"][l], SUBLANES), ff_b=r2(ff_conv_b[l]))

    def mixer_args(p):
        return (p["conv_w8"], p["conv_b"], p["dtb"], p["alog"], p["dsk"], p["ng"], p["vg"], p["ws"], p["bs_t"], p["og"])

    def padded_w_in(full):
        ns = full.shape[2]

        def cols(a, b):
            return [full[k][:, max(a, k * ns) - k * ns:min(b, (k + 1) * ns) - k * ns] for k in range(4) if max(a, k * ns) < min(b, (k + 1) * ns)]

        parts = cols(0, d) + cols(d + cd + heads, n_in) + cols(d, d + cd) + cols(d + cd, d + cd + heads)
        return jnp.concatenate(parts + [jnp.zeros((d, n_proj - (3 * d + cd + heads)), BF16)], axis=1)

    saved, xcur, pending = [], x, None
    for l in range(nl):
        p, w = layer_params(l), {}
        wi, token = landed(l, "w_in", mods_all if l == 0 else pending[0])
        w["w_in"] = padded_w_in(wi)
        sh1, sc1, g1, sh2, sc2, g2 = (mods[l, :, k] for k in range(N_MOD))
        if l + 1 < nl:
            issued = start_gathers(l + 1, BIG, token + issued)
        sc1 = _after(sc1, issued)
        if pending is None:
            (h,) = _row_fwd(f"nm{l}_fwd", _nm_tile, [(xcur, d, 0, None)], [sc1, sh1], [p["norm1"]], [(d, BF16)], 0, tm)
            x1 = xcur
        else:
            rows = [(xcur, d, 0, None), (pending[0], d, 0, None)]
            x1, h = _row_fwd(f"rnm{l}a_fwd", _rnm_tile, rows, [pending[1], sc1, sh1], [p["norm1"]], [(d, F32), (d, BF16)], 0, tm)
        proj = _mm(f"win{l}_fwd", h, w["w_in"], F32)
        ycat, (xbc, hprev) = _mixer_fwd(f"mix{l}", proj, d, cd, *mixer_args(p))
        w["w_out"], _ = landed(l, "w_out", ycat)
        mix = _mm(f"wout{l}_fwd", ycat, w["w_out"], F32)
        x2, h2 = _row_fwd(f"rnm{l}b_fwd", _rnm_tile, [(x1, d, 0, None), (mix, d, 0, None)], [g1, sc2, sh2], [p["norm2"]], [(d, F32), (d, BF16)], 0, tm)
        w_up, _ = landed(l, "ff_up", h2)
        w["ff_up"] = _gate_value_blocks(f"ffup{l}_blocks", w_up, f)
        up = _mm(f"ffup{l}_fwd", h2, w["ff_up"], F32)
        act = _conv_fwd(f"ffact{l}_fwd", up, 0, f, p["ff_w8"], p["ff_b"], FF_CONV, True, BF16)
        w["ff_down"], _ = landed(l, "ff_down", act)
        down = _mm(f"ffdown{l}_fwd", act, w["ff_down"], F32)
        saved.append(dict(p=p, w=w, x_in=xcur, pending=pending, h=h, proj=proj, xbc=xbc, hprev=hprev, ycat=ycat, x1=x1, mix=mix, h2=h2, up=up, act=act))
        xcur, pending = x2, (down, g2)
    rows = [(xcur, d, 0, F32), (pending[0], d, 0, BF16), (loss_target, d, 0, None)]

    (dx2, ddown), (dg2,), (dfg,), (loss_local,) = _row_bwd(
        "final_bwd", _final_tile, rows, [pending[1]], [fg], [], 1, tm, [jnp.ones((1, 1), F32)], emit_sums=True)
    loss = lax.psum(loss_local[0, 0], AXES)
    dmods, small_grads = [None] * nl, [None] * nl
    reducing = {}

    def start_reduce(l, n, g, token=None):
        width = g.shape[1] // 4
        by_chip = g.reshape(4, g.shape[0] // 4, g.shape[1]) if BIG_AXIS[n] == 0 else jnp.stack([g[:, k * width:(k + 1) * width] for k in range(4)])
        reducing[l, n] = _reduce_weights_start(f"wg{l}_{n}", [by_chip], token)
        return reducing[l, n]["token"]

    for l in reversed(range(nl)):
        sv = saved[l]
        p, w = sv["p"], sv["w"]
        sh1, sc1, g1, sh2, sc2, g2 = (mods[l, :, k] for k in range(N_MOD))
        dact, dw_down = _mm_bwd(f"ffdown{l}", sv["act"], w["ff_down"], ddown)
        ff_b = _after(p["ff_b"], start_reduce(l, "ff_down", dw_down))
        dup, dff_w8, dff_b = _conv_bwd(f"ffact{l}_bwd", sv["up"], 0, f, p["ff_w8"], ff_b, FF_CONV, True, dact, BF16)
        dh2, dw_up = _mm_bwd(f"ffup{l}", sv["h2"], w["ff_up"], dup)
        sc2 = _after(sc2, start_reduce(l, "ff_up", _gate_value_blocks(f"ffup{l}_columns", dw_up, f, inverse=True)))
        rows = [(sv["x1"], d, 0, F32), (sv["mix"], d, 0, BF16)]
        (dx1, dmix), (dg1, dsc2, dsh2), (dn2,) = _row_bwd(f"rnm{l}b_bwd", _rnm_tile, rows, [g1, sc2, sh2], [p["norm2"]], [(d, F32), (d, BF16)], 0, tm, [dx2, dh2])
        dycat, dw_out = _mm_bwd(f"wout{l}", sv["ycat"], w["w_out"], dmix)
        p_tied = dict(p, dtb=_after(p["dtb"], start_reduce(l, "w_out", dw_out)))
        dproj, (dw8, dcb, ddtb, dalog, ddsk, dng, dvg, dws, dbs_t, dog) = _mixer_bwd(f"mix{l}", sv["proj"], d, cd, *mixer_args(p_tied), sv["xbc"], sv["hprev"], dycat)
        dh, dw_in_p = _mm_bwd(f"win{l}", sv["h"], w["w_in"], dproj)
        dw_in = jnp.concatenate([dw_in_p[:, :d], dw_in_p[:, 3 * d:3 * d + cd], dw_in_p[:, 3 * d + cd:3 * d + cd + heads], dw_in_p[:, d:3 * d]], axis=1)
        if l > 0:
            sc1 = _after(sc1, start_reduce(l, "w_in", dw_in))
        if sv["pending"] is None:
            (dx2,), (dsc1, dsh1), (dn1,) = _row_bwd(f"nm{l}_bwd", _nm_res_tile, [(sv["x_in"], d, 0, F32)], [sc1, sh1], [p["norm1"]], [(d, F32), (d, BF16)], 0, tm, [dx1, dh])
        else:
            rows = [(sv["x_in"], d, 0, F32), (sv["pending"][0], d, 0, BF16)]
            (dx2, ddown), (dg2_prev, dsc1, dsh1), (dn1,) = _row_bwd(
                f"rnm{l}a_bwd", _rnm_tile, rows, [sv["pending"][1], sc1, sh1], [p["norm1"]], [(d, F32), (d, BF16)], 0, tm, [dx1, dh])
        dmods[l] = jnp.concatenate([dsh1, dsc1, dg1, dsh2, dsc2, dg2], axis=1).reshape(bl, N_MOD * d)
        if sv["pending"] is not None:
            dg2 = dg2_prev
        small_grads[l] = dict(
            norm1_g=dn1.reshape(d), norm2_g=dn2.reshape(d), ssd_conv_b=dcb.reshape(cd), ssd_dt_bias=ddtb[0, :heads], ssd_a_log=dalog[0, :heads],
            ssd_d=ddsk.reshape(heads, HEAD_DIM).sum(-1), ssd_norm_g=dng.reshape(d), gm_vnorm_g=dvg.reshape(d), gm_ws=dws, gm_bs=dbs_t[:, :n_gm].T,
            gm_out_g=dog.reshape(d), ff_conv_b=dff_b.reshape(f), ssd_conv_w=dw8[:SSD_CONV], ff_conv_w=dff_w8[:FF_CONV])
    grad_x = dx2

    g_ada_w, g_ada_b, ada_token = _ada_bwd("ada", bl, c_all, ada_w.shape[2], jnp.stack(dmods))
    small_names_r = SMALL + CONVW + ("final_g",)
    summed, small_token = _allreduce_small("small", [jnp.stack([small_grads[l][n] for l in range(nl)]) for n in SMALL + CONVW] + [dfg])
    travelling = start_reduce(0, "w_in", dw_in, ada_token + small_token)
    grad = {"ada_w": g_ada_w, "ada_b": g_ada_b}
    grad.update(zip(small_names_r, summed))
    chip = 2 * lax.axis_index("x") + lax.axis_index("y")
    for n in CONVW:
        width = weights[n].shape[-1]
        grad[n] = lax.dynamic_slice_in_dim(grad[n], chip * width, width, axis=2)

    delta, new_m, new_v = {}, {}, {}
    early = [k for k in reducing if k != (0, "w_in")]
    finished, _ = _reduce_weights_finish("wg_early", [reducing[k] for k in early], _after(summed[0], travelling))
    big_grads = dict(zip(early, finished))
    order = ("ada_w",) + tuple(reversed(BIG))
    for n in order:
        if n == "w_in":
            (big_grads[0, n],), _ = _reduce_weights_finish("wg_last", [reducing[0, n]], delta[order[-2]])
        if n != "ada_w":
            grad[n] = jnp.stack([big_grads[l, n] for l in range(nl)])
        delta[n], new_m[n], new_v[n] = _adamw("adam_" + n, weights[n], grad[n], given["m_" + n], given["v_" + n])
    small_names = ("ada_b",) + SMALL + CONVW + ("final_g",)
    as2d = lambda t: t.reshape(1, -1) if t.ndim == 1 else t
    res = _adamw_small(
        "adam_small", [as2d(weights[n]) for n in small_names], [as2d(grad[n]) for n in small_names],
        [as2d(given["m_" + n]) for n in small_names], [as2d(given["v_" + n]) for n in small_names])
    for out, vals in zip((delta, new_m, new_v), res):
        for n, val in zip(small_names, vals):
            out[n] = val.reshape(weights[n].shape)
    grad["final_g"] = grad["final_g"].reshape(final_g.shape)

    return (loss, grad_x, *[grad[n] for n in WEIGHTS], *[delta[n] for n in WEIGHTS], *[new_m[n] for n in WEIGHTS], *[new_v[n] for n in WEIGHTS])
```

```python
import jax
import jax.numpy as jnp
from jax import lax
from jax.experimental import pallas as pl
from jax.experimental.pallas import tpu as pltpu

F32 = jnp.float32
BF16 = jnp.bfloat16
EPS = 1e-6
CHUNK = 128
HEAD_DIM = 64
STATE = 128
GM_HEAD = 128
SSD_GROUPS = 2
SSD_CONV = 4
FF_CONV = 3
N_MOD = 6
LANES = 128
SUBLANES = 8
SMALL_BF16_SIZE = 1 << 16
CONV_LANES = 256
CONV_ROWS = 32
V7X_VMEM_LIMIT = 48 * 1024 * 1024
MM_VMEM_BUDGET = 30 * 1024 * 1024
MM_STEP_MACS = 2048 * 1024 * 1024
MM_ACC_SECONDS_PER_VREG = 1.4e-9
V7X_MXU_FLOPS = 996e12
V7X_HBM_BYTES_PER_S = 3.3e12
V7X_STEP_SECONDS = 0.35e-6
ADAM_LR, ADAM_B1, ADAM_B2, ADAM_EPS, ADAM_WD, ADAM_STEP = 0.001, 0.9, 0.999, 1e-08, 0.01, 10
MESH = pl.DeviceIdType.MESH
AXES = ("x", "y", "c")


def _round_up(n, m):
    return (n + m - 1) // m * m


def _divisors(n, cap, mult=LANES):
    out = [t for t in range(mult, min(n, cap) + 1, mult) if n % t == 0]
    return out or [n]


def _pcall(body, name, out_shape, grid=(), in_specs=None, out_specs=None, scratch=(), sem=None, prefetch=0):
    params = pltpu.CompilerParams(dimension_semantics=sem, vmem_limit_bytes=V7X_VMEM_LIMIT)
    if prefetch:
        spec = pltpu.PrefetchScalarGridSpec(num_scalar_prefetch=prefetch, grid=grid, in_specs=in_specs, out_specs=out_specs, scratch_shapes=list(scratch))
        return pl.pallas_call(body, name=name, out_shape=out_shape, grid_spec=spec, compiler_params=params)
    if in_specs is None:
        return pl.pallas_call(body, name=name, out_shape=out_shape, compiler_params=params)
    return pl.pallas_call(body, name=name, out_shape=out_shape, grid=grid, in_specs=in_specs, out_specs=out_specs, scratch_shapes=list(scratch), compiler_params=params)


def _sds(shape, dtype):
    return jax.ShapeDtypeStruct(tuple(shape), dtype)


def _mm_tiles(m, n, k, a_bytes, b_bytes, o_bytes):
    best, best_key = None, None
    for tm in _divisors(m, 2048):
        for tn in _divisors(n, 2560):
            for tk in _divisors(k, 2560):
                vmem = 2 * (tm * tk * a_bytes + tk * tn * b_bytes + tm * tn * o_bytes) + tm * tn * 4
                if vmem > MM_VMEM_BUDGET or tm * tn * tk > MM_STEP_MACS:
                    continue
                ni, nj, nk = m // tm, n // tn, k // tk
                a_reads = 1 if nk == 1 else nj
                b_reads = 1 if (nk == 1 and nj == 1) else ni
                hbm = a_reads * m * k * a_bytes + b_reads * k * n * b_bytes + m * n * o_bytes
                t = max(2.0 * m * n * k / V7X_MXU_FLOPS, hbm / V7X_HBM_BYTES_PER_S) + ni * nj * nk * V7X_STEP_SECONDS
                if nk > 1:
                    t += ni * nj * nk * (tm * tn // (SUBLANES * LANES)) * MM_ACC_SECONDS_PER_VREG
                key = (-t, tm * tn * tk)
                if best_key is None or key > best_key:
                    best, best_key = (tm, tn, tk), key
    return best


def _matmul(name, a, b, mode, out_dtype):
    if mode == "nn":
        (m, k), n = a.shape, b.shape[1]
    elif mode == "nt":
        (m, k), n = a.shape, b.shape[0]
    else:
        (k, m), n = a.shape, b.shape[1]
    tm, tn, tk = _mm_tiles(m, n, k, a.dtype.itemsize, b.dtype.itemsize, jnp.dtype(out_dtype).itemsize)
    nk = k // tk
    if mode == "nn":
        a_spec = pl.BlockSpec((tm, tk), lambda i, j, kk: (i, kk))
        b_spec = pl.BlockSpec((tk, tn), lambda i, j, kk: (kk, j))
        dims = ((1,), (0,))
    elif mode == "nt":
        a_spec = pl.BlockSpec((tm, tk), lambda i, j, kk: (i, kk))
        b_spec = pl.BlockSpec((tn, tk), lambda i, j, kk: (j, kk))
        dims = ((1,), (1,))
    else:
        a_spec = pl.BlockSpec((tk, tm), lambda i, j, kk: (kk, i))
        b_spec = pl.BlockSpec((tk, tn), lambda i, j, kk: (kk, j))
        dims = ((0,), (0,))

    def body(a_ref, b_ref, o_ref, acc_ref):
        kk = pl.program_id(2)
        p = lax.dot_general(a_ref[...].astype(BF16), b_ref[...].astype(BF16), (dims, ((), ())), preferred_element_type=F32)
        if nk == 1:
            o_ref[...] = p.astype(o_ref.dtype)
        else:
            @pl.when(kk == 0)
            def _():
                acc_ref[...] = p

            @pl.when(kk > 0)
            def _():
                acc_ref[...] += p

            @pl.when(kk == nk - 1)
            def _():
                o_ref[...] = acc_ref[...].astype(o_ref.dtype)

    return _pcall(
        body, name, _sds((m, n), out_dtype), grid=(m // tm, n // tn, nk), in_specs=[a_spec, b_spec],
        out_specs=pl.BlockSpec((tm, tn), lambda i, j, kk: (i, j)), scratch=[pltpu.VMEM((tm, tn), F32)],
        sem=("parallel", "parallel", "arbitrary"),
    )(a, b)


def _mm(name, a, w, out_dtype):
    return _matmul(name, a.reshape(-1, a.shape[-1]), w, "nn", out_dtype).reshape(a.shape[:-1] + (w.shape[1],))


def _mm_bwd(name, a, w, dy):
    a2, dy2 = a.reshape(-1, a.shape[-1]), dy.reshape(-1, dy.shape[-1])
    return _matmul(name + "_dx", dy2, w, "nt", BF16).reshape(a.shape), _matmul(name + "_dw", a2, dy2, "tn", BF16)


def _dg(a, b, dims):
    return lax.dot_general(a.astype(BF16), b.astype(BF16), (dims, ((), ())), preferred_element_type=F32)


@jax.custom_vjp
def _dot_nn(a, b):
    return _dg(a, b, ((1,), (0,)))


_dot_nn.defvjp(lambda a, b: (_dot_nn(a, b), (a, b)), lambda r, d: (_dg(d, r[1], ((1,), (1,))), _dg(r[0], d, ((0,), (0,)))))


@jax.custom_vjp
def _dot_nt(a, b):
    return _dg(a, b, ((1,), (1,)))


_dot_nt.defvjp(lambda a, b: (_dot_nt(a, b), (a, b)), lambda r, d: (_dg(d, r[1], ((1,), (0,))), _dg(d, r[0], ((0,), (0,)))))


@jax.custom_vjp
def _dot_tn(a, b):
    return _dg(a, b, ((0,), (0,)))


_dot_tn.defvjp(lambda a, b: (_dot_tn(a, b), (a, b)), lambda r, d: (_dg(r[1], d, ((1,), (1,))), _dg(r[0], d, ((1,), (0,)))))


def _exact_dot(a, c, dims):
    hi = a.astype(BF16)
    r1 = a - hi.astype(F32)
    mid = r1.astype(BF16)
    lo = (r1 - mid.astype(F32)).astype(BF16)
    cb = c.astype(BF16)
    f = lambda t: lax.dot_general(t, cb, (dims, ((), ())), preferred_element_type=F32)
    return f(hi) + f(mid) + f(lo)


@jax.custom_vjp
def _sel_right(a, c):
    return _exact_dot(a, c, ((1,), (0,)))


_sel_right.defvjp(lambda a, c: (_sel_right(a, c), c), lambda c, d: (_exact_dot(d, c, ((1,), (1,))), jnp.zeros_like(c)))


def _exact_dot_left(c, a, dims):
    hi = a.astype(BF16)
    r1 = a - hi.astype(F32)
    mid = r1.astype(BF16)
    lo = (r1 - mid.astype(F32)).astype(BF16)
    cb = c.astype(BF16)
    f = lambda t: lax.dot_general(cb, t, (dims, ((), ())), preferred_element_type=F32)
    return f(hi) + f(mid) + f(lo)


@jax.custom_vjp
def _sel_left(c, a):
    return _exact_dot_left(c, a, ((1,), (0,)))


_sel_left.defvjp(lambda c, a: (_sel_left(c, a), c), lambda c, d: (jnp.zeros_like(c), _exact_dot_left(c, d, ((0,), (0,)))))


def _sigmoid(x):
    return 1.0 / (1.0 + jnp.exp(-x))


def _rms(x, g):
    return x * lax.rsqrt(jnp.mean(x * x, axis=-1, keepdims=True) + EPS) * g


def _gelu(x):
    return 0.5 * x * (1.0 + lax.erf(x * (2.0 ** -0.5)))


def _causal(n):
    return lax.broadcasted_iota(jnp.int32, (n, n), 0) >= lax.broadcasted_iota(jnp.int32, (n, n), 1)


def _row_in_specs(rows, bparams, gparams, tm):
    specs = [pl.BlockSpec((1, tm, w), lambda b, i, cb=cb: (b, i, cb)) for (_, w, cb, _) in rows]
    specs += [pl.BlockSpec((1, 1, p.shape[-1]), lambda b, i: (b, 0, 0)) for p in bparams]
    specs += [pl.BlockSpec(p.shape, lambda b, i, n=p.ndim: (0,) * n) for p in gparams]
    return specs


def _row_vals(refs, n_rows, n_b, n_g):
    vals = [r[0].astype(F32) for r in refs[:n_rows]]
    vals += [r[0].astype(F32) for r in refs[n_rows:n_rows + n_b]]
    vals += [r[...].astype(F32) for r in refs[n_rows + n_b:n_rows + n_b + n_g]]
    return vals


def _row_fwd(name, tile, rows, bparams, gparams, outs, n_sum, tm):
    bl, s = rows[0][0].shape[:2]
    n_in = len(rows) + len(bparams) + len(gparams)

    def body(*refs):
        first = (pl.program_id(0) == 0) & (pl.program_id(1) == 0)
        res = tile(*_row_vals(refs, len(rows), len(bparams), len(gparams)))
        o_refs = refs[n_in:]
        for k in range(len(outs)):
            o_refs[k][0] = res[k].astype(o_refs[k].dtype)
        for k in range(n_sum):
            o_ref, val = o_refs[len(outs) + k], res[len(outs) + k]

            @pl.when(first)
            def _(o_ref=o_ref, val=val):
                o_ref[...] = val

            @pl.when(jnp.logical_not(first))
            def _(o_ref=o_ref, val=val):
                o_ref[...] += val

    out_shape = [_sds((bl, s, w), dt) for (w, dt) in outs] + [_sds((1, 1), F32)] * n_sum
    out_specs = [pl.BlockSpec((1, tm, w), lambda b, i: (b, i, 0)) for (w, _) in outs] + [pl.BlockSpec((1, 1), lambda b, i: (0, 0))] * n_sum
    return _pcall(
        body, name, out_shape, grid=(bl, s // tm), in_specs=_row_in_specs(rows, bparams, gparams, tm), out_specs=out_specs,
        sem=("arbitrary", "arbitrary"),
    )(*[r[0] for r in rows], *bparams, *gparams)


def _row_bwd(name, tile, rows, bparams, gparams, outs, n_sum, tm, cts, emit_sums=False):
    bl, s = rows[0][0].shape[:2]
    n_r, n_b, n_g = len(rows), len(bparams), len(gparams)
    n_extra = n_sum if emit_sums else 0
    n_in = n_r + n_b + n_g
    n_ct = len(outs) + n_sum
    grad_rows = [k for k in range(n_r) if rows[k][3]]

    def body(*refs):
        b, i = pl.program_id(0), pl.program_id(1)
        vals = _row_vals(refs, n_r, n_b, n_g)
        ct_refs = refs[n_in:n_in + n_ct]
        ct = [r[0].astype(F32) for r in ct_refs[:len(outs)]] + [r[...] for r in ct_refs[len(outs):]]
        primal, vjp = jax.vjp(tile, *vals)
        grads = tuple(vjp(tuple(ct))) + (tuple(primal[len(outs):]) if emit_sums else ())
        o_refs = refs[n_in + n_ct:]
        for j, k in enumerate(grad_rows):
            o_refs[j][0] = grads[k].astype(o_refs[j].dtype)
        for k in range(n_b):
            o_ref, val = o_refs[len(grad_rows) + k], grads[n_r + k]

            @pl.when(i == 0)
            def _(o_ref=o_ref, val=val):
                o_ref[0] = val

            @pl.when(i > 0)
            def _(o_ref=o_ref, val=val):
                o_ref[0] += val

        first = (b == 0) & (i == 0)
        for k in range(n_g + n_extra):
            o_ref, val = o_refs[len(grad_rows) + n_b + k], grads[n_r + n_b + k]

            @pl.when(first)
            def _(o_ref=o_ref, val=val):
                o_ref[...] = val

            @pl.when(jnp.logical_not(first))
            def _(o_ref=o_ref, val=val):
                o_ref[...] += val

    one = pl.BlockSpec((1, 1), lambda b, i: (0, 0))
    in_specs = _row_in_specs(rows, bparams, gparams, tm)
    in_specs += [pl.BlockSpec((1, tm, w), lambda b, i: (b, i, 0)) for (w, _) in outs] + [one] * n_sum
    out_shape = [_sds((bl, s, rows[k][1]), rows[k][3]) for k in grad_rows]
    out_shape += [_sds(p.shape, F32) for p in bparams] + [_sds(p.shape, F32) for p in gparams] + [_sds((1, 1), F32)] * n_extra
    out_specs = [pl.BlockSpec((1, tm, rows[k][1]), lambda b, i: (b, i, 0)) for k in grad_rows]
    out_specs += [pl.BlockSpec((1, 1, p.shape[-1]), lambda b, i: (b, 0, 0)) for p in bparams]
    out_specs += [pl.BlockSpec(p.shape, lambda b, i, n=p.ndim: (0,) * n) for p in gparams] + [one] * n_extra
    res = _pcall(
        body, name, out_shape, grid=(bl, s // tm), in_specs=in_specs, out_specs=out_specs, sem=("arbitrary", "arbitrary"),
    )(*[r[0] for r in rows], *bparams, *gparams, *cts)
    n0, n1, n2 = len(grad_rows), len(grad_rows) + n_b, len(grad_rows) + n_b + n_g
    return (res[:n0], res[n0:n1], res[n1:n2]) + ((res[n2:],) if emit_sums else ())


def _nm_tile(x, sc, sh, g):
    return (_rms(x, g) * (1.0 + sc) + sh,)


def _nm_res_tile(x, sc, sh, g):
    return x, _rms(x, g) * (1.0 + sc) + sh


def _rnm_tile(x, o, gate, sc, sh, g):
    xn = x + gate * o
    return xn, _rms(xn, g) * (1.0 + sc) + sh


def _final_tile(x, o, tgt, gate, g):
    e = _rms(x + gate * o, g) - tgt
    return (0.5 * jnp.sum(jnp.mean(e * e, axis=-1, keepdims=True), axis=0, keepdims=True),)


def _gm_tile(u_in, v_in, vg, ws, bs_t, og):
    d = u_in.shape[1]
    u, vn = _gelu(u_in), _rms(_gelu(v_in), vg)
    causal = _causal(CHUNK)
    lane = lax.broadcasted_iota(jnp.int32, (1, LANES), 1)
    parts = []
    for h in range(d // GM_HEAD):
        bias = jnp.sum(bs_t * (lane == h).astype(F32), axis=1, keepdims=True)
        parts.append(_dot_nn(jnp.where(causal, ws[h], 0.0), vn[:, h * GM_HEAD:(h + 1) * GM_HEAD]) + bias)
    return (_rms(u * jnp.concatenate(parts, axis=1), og),)


def _conv_window(ref, r0, rows, c0, tc, seq, before, after):
    parts = []
    if before:
        p0 = pl.multiple_of(jnp.maximum(r0 - SUBLANES, 0), SUBLANES)
        parts.append(jnp.where(r0 > 0, ref[0, pl.ds(p0, SUBLANES), pl.ds(c0, tc)].astype(F32), 0.0))
    parts.append(ref[0, pl.ds(r0, rows), pl.ds(c0, tc)].astype(F32))
    if after:
        n0 = pl.multiple_of(jnp.minimum(r0 + rows, seq - SUBLANES), SUBLANES)
        parts.append(jnp.where(r0 + rows < seq, ref[0, pl.ds(n0, SUBLANES), pl.ds(c0, tc)].astype(F32), 0.0))
    return jnp.concatenate(parts, axis=0) if len(parts) > 1 else parts[0]


def _conv_taps_pre(xe, w, b, taps, rows):
    pre = xe[SUBLANES:SUBLANES + rows] * w[taps - 1] + b
    for j in range(1, taps):
        pre = pre + pltpu.roll(xe, j, 0)[SUBLANES:SUBLANES + rows] * w[taps - 1 - j]
    return pre


def _conv_fwd(name, src, col0, chans, w8, b, taps, gated, out_dtype):
    bl, s, _ = src.shape
    tc = CONV_LANES
    xw = 2 * tc if gated else tc

    def body(x_ref, w_ref, b_ref, o_ref):
        w = [w_ref[k:k + 1, :] for k in range(taps)]
        bias = b_ref[...]

        def step(c, carry):
            r0 = pl.multiple_of(c * CONV_ROWS, CONV_ROWS)
            xe = _conv_window(x_ref, r0, CONV_ROWS, 0, tc, s, True, False)
            pre = _conv_taps_pre(xe, w, bias, taps, CONV_ROWS)
            y = pre * _sigmoid(pre)
            if gated:
                y = y * x_ref[0, pl.ds(r0, CONV_ROWS), pl.ds(tc, tc)]
            o_ref[0, pl.ds(r0, CONV_ROWS), :] = y.astype(o_ref.dtype)
            return carry

        lax.fori_loop(0, s // CONV_ROWS, step, 0, unroll=4)

    first = 0 if gated else col0 // tc
    return _pcall(
        body, name, _sds((bl, s, chans), out_dtype), grid=(chans // tc, bl),
        in_specs=[pl.BlockSpec((1, s, xw), lambda j, bb: (bb, 0, first + j)), pl.BlockSpec((SUBLANES, tc), lambda j, bb: (0, j)), pl.BlockSpec((1, tc), lambda j, bb: (0, j))],
        out_specs=pl.BlockSpec((1, s, tc), lambda j, bb: (bb, 0, j)), sem=("parallel", "arbitrary"),
    )(src, w8, b)


def _conv_bwd(name, src, col0, chans, w8, b, taps, gated, dy, dx_dtype):
    bl, s, _ = src.shape
    tc = CONV_LANES
    xw = 2 * tc if gated else tc
    ext = CONV_ROWS + SUBLANES

    def body(x_ref, w_ref, b_ref, dy_ref, dx_ref, dw_ref, db_ref):
        bb = pl.program_id(1)
        w = [w_ref[k:k + 1, :] for k in range(taps)]
        bias = b_ref[...]

        def fold(v):
            acc = v[0:SUBLANES]
            for i in range(1, CONV_ROWS // SUBLANES):
                acc = acc + v[i * SUBLANES:(i + 1) * SUBLANES]
            return acc

        def step(c, carry):
            r0 = pl.multiple_of(c * CONV_ROWS, CONV_ROWS)
            xe = _conv_window(x_ref, r0, CONV_ROWS, 0, tc, s, True, True)
            pre = _conv_taps_pre(xe, w, bias, taps, ext)
            sig = _sigmoid(pre)
            d = _conv_window(dy_ref, r0, CONV_ROWS, 0, tc, s, False, True)
            dsil = d * _conv_window(x_ref, r0, CONV_ROWS, tc, tc, s, False, True) if gated else d
            dpre = dsil * sig * (1.0 + pre * (1.0 - sig))
            dx = dpre[:CONV_ROWS] * w[taps - 1]
            for j in range(1, taps):
                dx = dx + pltpu.roll(dpre, ext - j, 0)[:CONV_ROWS] * w[taps - 1 - j]
            dx_ref[0, pl.ds(r0, CONV_ROWS), pl.ds(0, tc)] = dx.astype(dx_ref.dtype)
            if gated:
                dx_ref[0, pl.ds(r0, CONV_ROWS), pl.ds(tc, tc)] = (d[:CONV_ROWS] * (pre * sig)[:CONV_ROWS]).astype(dx_ref.dtype)
            here = dpre[:CONV_ROWS]
            sums = [fold(here * (pltpu.roll(xe, taps - 1 - k, 0) if k < taps - 1 else xe)[SUBLANES:ext]) + carry[k] for k in range(taps)]
            return tuple(sums) + (fold(here) + carry[taps],)

        zero = jnp.zeros((SUBLANES, tc), F32)
        sums = lax.fori_loop(0, s // CONV_ROWS, step, (zero,) * (taps + 1), unroll=2)
        rows = [jnp.sum(t, axis=0, keepdims=True) for t in sums]
        dw = jnp.concatenate(rows[:taps] + [jnp.zeros_like(rows[0])] * (SUBLANES - taps), axis=0)

        @pl.when(bb == 0)
        def _():
            dw_ref[...] = dw
            db_ref[...] = rows[taps]

        @pl.when(bb > 0)
        def _():
            dw_ref[...] += dw
            db_ref[...] += rows[taps]

    first = 0 if gated else col0 // tc
    out_shape = [_sds((bl, s, chans * (2 if gated else 1)), dx_dtype), _sds((SUBLANES, chans), F32), _sds((1, chans), F32)]
    return _pcall(
        body, name, out_shape, grid=(chans // tc, bl),
        in_specs=[
            pl.BlockSpec((1, s, xw), lambda j, bb: (bb, 0, first + j)), pl.BlockSpec((SUBLANES, tc), lambda j, bb: (0, j)),
            pl.BlockSpec((1, tc), lambda j, bb: (0, j)), pl.BlockSpec((1, s, tc), lambda j, bb: (bb, 0, j)),
        ],
        out_specs=[pl.BlockSpec((1, s, xw), lambda j, bb: (bb, 0, j)), pl.BlockSpec((SUBLANES, tc), lambda j, bb: (0, j)), pl.BlockSpec((1, tc), lambda j, bb: (0, j))],
        sem=("parallel", "arbitrary"),
    )(src, w8, b, dy)


def _gate_value_blocks(name, w, f, inverse=False):
    nb = f // CONV_LANES
    src = (lambda j: (0, 2 * (j % nb) + j // nb)) if inverse else (lambda j: (0, (j % 2) * nb + j // 2))

    def body(x_ref, o_ref):
        o_ref[...] = x_ref[...]

    return _pcall(
        body, name, _sds(w.shape, w.dtype), grid=(2 * nb,), in_specs=[pl.BlockSpec((w.shape[0], CONV_LANES), src)],
        out_specs=pl.BlockSpec((w.shape[0], CONV_LANES), lambda j: (0, j)), sem=("parallel",),
    )(w)


def _ssd_step(states, xs, bm, cm, dtr, z, dtb, alog, dsk, ng):
    gw = xs.shape[1] // SSD_GROUPS
    hpg, npair = gw // HEAD_DIM, gw // LANES
    dt = jax.nn.softplus(dtr + dtb)
    da = dt * (-jnp.exp(alog))
    causal = _causal(CHUNK)
    acs = _sel_left(causal.astype(F32), da)
    acs_t = acs.T
    lane = lax.broadcasted_iota(jnp.int32, (1, LANES), 1)
    sub = lax.broadcasted_iota(jnp.int32, (LANES, 1), 0)
    last = lax.broadcasted_iota(jnp.int32, (CHUNK, gw), 0) == CHUNK - 1
    outs, new_states = [], []
    for g in range(SSD_GROUPS):
        xg, zg = xs[:, g * gw:(g + 1) * gw], z[:, g * gw:(g + 1) * gw]
        bg, cg = bm[:, g * STATE:(g + 1) * STATE], cm[:, g * STATE:(g + 1) * STATE]
        head_of_col = g * hpg + lax.broadcasted_iota(jnp.int32, (LANES, gw), 1) // HEAD_DIM
        expand = (lax.broadcasted_iota(jnp.int32, (LANES, gw), 0) == head_of_col).astype(F32)
        dt_e, acs_e = _sel_right(dt, expand), _sel_right(acs, expand)
        alast_e = jnp.sum(jnp.where(last, acs_e, 0.0), axis=0, keepdims=True)
        xc = xg * dt_e
        xc_st = xc * jnp.exp(alast_e - acs_e)
        decay_out, chunk_decay = jnp.exp(acs_e), jnp.exp(alast_e)
        cb = _dot_nt(cg, bg)
        ys = []
        for p in range(npair):
            sl = slice(p * LANES, (p + 1) * LANES)
            state = states[g * npair + p]
            y = _dot_nn(cg, state) * decay_out[:, sl]
            for q in range(2):
                head = g * hpg + 2 * p + q
                col = jnp.sum(acs * (lane == head).astype(F32), axis=1, keepdims=True)
                row = jnp.sum(acs_t * (sub == head).astype(F32), axis=0, keepdims=True)
                decay = jnp.where(causal, jnp.exp(jnp.where(causal, col - row, 0.0)), 0.0)
                y = y + _dot_nn(cb * decay, xc[:, sl] * ((lane // HEAD_DIM) == q).astype(F32))
            ys.append(y)
            new_states.append(state * chunk_decay[:, sl] + _dot_tn(bg, xc_st[:, sl]))
        y = jnp.concatenate(ys, axis=1) + dsk[:, g * gw:(g + 1) * gw] * xg
        outs.append(_rms(y * (zg * _sigmoid(zg)), ng[:, g * gw:(g + 1) * gw]))
    return tuple(new_states), jnp.concatenate(outs, axis=1)


def _ssd_scan_specs(d, dt_col, nc, rev):
    ci = (lambda i: nc - 1 - i) if rev else (lambda i: i)
    bc = SSD_GROUPS * STATE
    return [
        pl.BlockSpec((1, CHUNK, d), lambda b, i: (b, ci(i), 0)),
        pl.BlockSpec((1, CHUNK, bc), lambda b, i: (b, ci(i), d // bc)),
        pl.BlockSpec((1, CHUNK, bc), lambda b, i: (b, ci(i), d // bc + 1)),
        pl.BlockSpec((1, CHUNK, LANES), lambda b, i: (b, ci(i), dt_col // LANES)),
        pl.BlockSpec((1, CHUNK, d), lambda b, i: (b, ci(i), 0)),
        pl.BlockSpec((1, LANES), lambda b, i: (0, 0)), pl.BlockSpec((1, LANES), lambda b, i: (0, 0)),
        pl.BlockSpec((1, d), lambda b, i: (0, 0)), pl.BlockSpec((1, d), lambda b, i: (0, 0)),
    ]


def _ssd_scan_fwd(name, xbc, proj, dt_col, dtb, alog, dsk, ng):
    bl, s, _ = xbc.shape
    d = dsk.shape[1]
    nc, nst = s // CHUNK, d // LANES

    def body(xs_ref, bm_ref, cm_ref, dt_ref, z_ref, dtb_ref, alog_ref, dsk_ref, ng_ref, y_ref, hp_ref, st_ref):
        @pl.when(pl.program_id(1) == 0)
        def _():
            st_ref[...] = jnp.zeros_like(st_ref)

        states = tuple(st_ref[p] for p in range(nst))
        hp_ref[0, 0] = st_ref[...]
        new_states, yn = _ssd_step(states, xs_ref[0], bm_ref[0], cm_ref[0], dt_ref[0], z_ref[0], dtb_ref[...], alog_ref[...], dsk_ref[...], ng_ref[...])
        for p in range(nst):
            st_ref[p] = new_states[p]
        y_ref[0] = yn.astype(y_ref.dtype)

    return _pcall(
        body, name, [_sds((bl, s, d), BF16), _sds((bl, nc, nst, STATE, LANES), F32)], grid=(bl, nc), in_specs=_ssd_scan_specs(d, dt_col, nc, False),
        out_specs=[pl.BlockSpec((1, CHUNK, d), lambda b, i: (b, i, 0)), pl.BlockSpec((1, 1, nst, STATE, LANES), lambda b, i: (b, i, 0, 0, 0))],
        scratch=[pltpu.VMEM((nst, STATE, LANES), F32)], sem=("arbitrary", "arbitrary"),
    )(xbc, xbc, xbc, proj, proj, dtb, alog, dsk, ng)


def _ssd_scan_bwd(name, xbc, proj, dt_col, dtb, alog, dsk, ng, hprev, dy):
    bl, s, _ = xbc.shape
    d = dsk.shape[1]
    nc, nst, bc = s // CHUNK, d // LANES, SSD_GROUPS * STATE

    def body(xs_ref, bm_ref, cm_ref, dt_ref, z_ref, dtb_ref, alog_ref, dsk_ref, ng_ref, hp_ref, dy_ref,
             dxs_ref, dbm_ref, dcm_ref, ddt_ref, dz_ref, ddtb_ref, dalog_ref, ddsk_ref, dng_ref, dst_ref):
        b, i = pl.program_id(0), pl.program_id(1)

        @pl.when(i == 0)
        def _():
            dst_ref[...] = jnp.zeros_like(dst_ref)

        states = tuple(hp_ref[0, 0, p] for p in range(nst))
        _, vjp = jax.vjp(_ssd_step, states, xs_ref[0], bm_ref[0], cm_ref[0], dt_ref[0], z_ref[0], dtb_ref[...], alog_ref[...], dsk_ref[...], ng_ref[...])
        d_states, dxs, dbm, dcm, ddt, dz, ddtb, dalog, ddsk, dng = vjp((tuple(dst_ref[p] for p in range(nst)), dy_ref[0].astype(F32)))
        for p in range(nst):
            dst_ref[p] = d_states[p]
        dxs_ref[0], dbm_ref[0], dcm_ref[0], ddt_ref[0] = dxs, dbm, dcm, ddt
        dz_ref[0] = dz.astype(dz_ref.dtype)
        first = (b == 0) & (i == 0)
        for o_ref, val in ((ddtb_ref, ddtb), (dalog_ref, dalog), (ddsk_ref, ddsk), (dng_ref, dng)):
            @pl.when(first)
            def _(o_ref=o_ref, val=val):
                o_ref[...] = val

            @pl.when(jnp.logical_not(first))
            def _(o_ref=o_ref, val=val):
                o_ref[...] += val

    rc = lambda i: nc - 1 - i
    in_specs = _ssd_scan_specs(d, dt_col, nc, True) + [
        pl.BlockSpec((1, 1, nst, STATE, LANES), lambda b, i: (b, rc(i), 0, 0, 0)), pl.BlockSpec((1, CHUNK, d), lambda b, i: (b, rc(i), 0)),
    ]
    out_shape = [
        _sds((bl, s, d), F32), _sds((bl, s, bc), F32), _sds((bl, s, bc), F32), _sds((bl, s, LANES), F32), _sds((bl, s, d), BF16),
        _sds((1, LANES), F32), _sds((1, LANES), F32), _sds((1, d), F32), _sds((1, d), F32),
    ]
    row = lambda w: pl.BlockSpec((1, CHUNK, w), lambda b, i: (b, rc(i), 0))
    whole = lambda w: pl.BlockSpec((1, w), lambda b, i: (0, 0))
    out_specs = [row(d), row(bc), row(bc), row(LANES), row(d), whole(LANES), whole(LANES), whole(d), whole(d)]
    return _pcall(
        body, name, out_shape, grid=(bl, nc), in_specs=in_specs, out_specs=out_specs,
        scratch=[pltpu.VMEM((nst, STATE, LANES), F32)], sem=("arbitrary", "arbitrary"),
    )(xbc, xbc, xbc, proj, proj, dtb, alog, dsk, ng, hprev, dy)


def _mixer_fwd(name, proj, d, cd, conv_w8, conv_b, dtb, alog, dsk, ng, vg, ws, bs_t, og):
    xbc = _conv_fwd(name + "_conv_fwd", proj, 3 * d, cd, conv_w8, conv_b, SSD_CONV, False, F32)
    y, hprev = _ssd_scan_fwd(name + "_ssd_fwd", xbc, proj, 3 * d + cd, dtb, alog, dsk, ng)
    rows = [(proj, d, 1, BF16), (proj, d, 2, BF16)]
    (g_out,) = _row_fwd(name + "_gm_fwd", _gm_tile, rows, [], [vg, ws, bs_t, og], [(d, BF16)], 0, CHUNK)
    return jnp.concatenate([y, g_out], axis=-1), (xbc, hprev)


def _mixer_bwd(name, proj, d, cd, conv_w8, conv_b, dtb, alog, dsk, ng, vg, ws, bs_t, og, xbc, hprev, dycat):
    bl, s, n_proj = proj.shape
    dt_col = 3 * d + cd
    dy, dg_out = dycat[..., :d], dycat[..., d:]
    dxs, dbm, dcm, ddt, dz, ddtb, dalog, ddsk, dng = _ssd_scan_bwd(name + "_ssd_bwd", xbc, proj, dt_col, dtb, alog, dsk, ng, hprev, dy)
    dxbc_act = jnp.concatenate([dxs, dbm, dcm], axis=-1)
    dxbc, dw8, dcb = _conv_bwd(name + "_conv_bwd", proj, 3 * d, cd, conv_w8, conv_b, SSD_CONV, False, dxbc_act, BF16)
    rows = [(proj, d, 1, BF16), (proj, d, 2, BF16)]
    (du, dv), _, (dvg, dws, dbs_t, dog) = _row_bwd(name + "_gm_bwd", _gm_tile, rows, [], [vg, ws, bs_t, og], [(d, BF16)], 0, CHUNK, [dg_out])
    pad = jnp.zeros((bl, s, n_proj - dt_col - LANES), BF16)
    dproj = jnp.concatenate([dz, du, dv, dxbc, ddt.astype(BF16), pad], axis=-1)
    return dproj, (dw8, dcb, ddtb, dalog, ddsk, dng, dvg, dws, dbs_t, dog)


def _position():
    return lax.axis_index("x"), lax.axis_index("y"), lax.axis_index("c")


def _at(ref, idx):
    return ref.at[idx] if len(idx) else ref


def _exchange(name, inputs, out_shapes, plan, inplace=False, after=None):
    if inplace:
        out_shapes = [(a.shape, a.dtype) for a in inputs]
    n_in, n_out = len(inputs), len(out_shapes)
    n_after = 0 if after is None else 1
    n_copy = len(plan(0, 0, 0))

    def body(*refs):
        in_refs, out_refs, token = refs[:n_in], refs[n_in + n_after:n_in + n_after + n_out], refs[n_in + n_after + n_out]
        send_sems, recv_sems = refs[n_in + n_after + n_out + 1:]
        token[...] = jnp.zeros_like(token)
        x, y, c = _position()
        copies = plan(x, y, c)

        def copy(k, src, dst, peer):
            return pltpu.make_async_remote_copy(src_ref=src, dst_ref=dst, send_sem=send_sems.at[k], recv_sem=recv_sems.at[k], device_id=peer, device_id_type=MESH)

        src_refs = out_refs if inplace else in_refs
        sends = [copy(k, _at(src_refs[sa], si), _at(out_refs[da], di), peer) for k, (sa, si, da, di, peer, _) in enumerate(copies)]
        for cp in sends:
            cp.start()
        for k, (sa, si, da, _, peer, li) in enumerate(copies):
            copy(k, _at(src_refs[sa], si), _at(out_refs[da], li), peer).wait_recv()
        for cp in sends:
            cp.wait_send()

    any_spec = pl.BlockSpec(memory_space=pl.ANY)
    outs = pl.pallas_call(
        body, name=name, out_shape=[_sds(s, dt) for s, dt in out_shapes] + [_sds((SUBLANES, LANES), F32)], in_specs=[any_spec] * (n_in + n_after),
        out_specs=[any_spec] * n_out + [pl.BlockSpec(memory_space=pltpu.VMEM)],
        scratch_shapes=[pltpu.SemaphoreType.DMA((n_copy,)), pltpu.SemaphoreType.DMA((n_copy,))],
        input_output_aliases={i: i for i in range(n_in)} if inplace else {},
    )(*inputs, *([] if after is None else [after]))
    return list(outs[:n_out]), outs[n_out]


def _exchange_start(name, inputs, out_shapes, plan):
    n_in, n_out = len(inputs), len(out_shapes)
    n_copy = len(plan(0, 0, 0))

    def body(*refs):
        in_refs, land_refs = refs[:n_in], refs[n_in:n_in + n_out]
        send_sems, recv_sems = refs[n_in + n_out:n_in + n_out + 2]
        token = refs[-1]
        x, y, c = _position()
        for k, (sa, si, da, di, peer, _) in enumerate(plan(x, y, c)):
            pltpu.make_async_remote_copy(
                src_ref=_at(in_refs[sa], si), dst_ref=_at(land_refs[da], di), send_sem=send_sems.at[k], recv_sem=recv_sems.at[k],
                device_id=peer, device_id_type=MESH).start()
        token[...] = jnp.zeros_like(token)

    hbm, sem = pl.BlockSpec(memory_space=pltpu.HBM), pl.BlockSpec(memory_space=pltpu.SEMAPHORE)
    lands = [lax.empty(s, dt) for s, dt in out_shapes]
    args = [pltpu.with_memory_space_constraint(a, pltpu.HBM) for a in list(inputs) + lands]
    outs = pl.pallas_call(
        body, name=name,
        out_shape=(pltpu.SemaphoreType.DMA((n_copy,)), pltpu.SemaphoreType.DMA((n_copy,)), *[pltpu.HBM(a.shape, a.dtype) for a in args], _sds((SUBLANES, LANES), F32)),
        in_specs=[hbm] * (n_in + n_out), out_specs=(sem, sem, *[hbm] * (n_in + n_out), pl.BlockSpec(memory_space=pltpu.VMEM)),
        input_output_aliases={i: 2 + i for i in range(n_in + n_out)},
        compiler_params=pltpu.CompilerParams(has_side_effects=pltpu.SideEffectType.DATAFLOW_SIDE_EFFECTING),
    )(*args)
    return dict(name=name, plan=plan, sems=outs[:2], ins=list(outs[2:2 + n_in]), lands=list(outs[2 + n_in:2 + n_in + n_out]), token=outs[-1])


def _exchange_wait(started, after):
    plan, n_in, n_out = started["plan"], len(started["ins"]), len(started["lands"])

    def body(*refs):
        in_refs, land_refs = refs[:n_in], refs[n_in:n_in + n_out]
        send_sems, recv_sems = refs[n_in + n_out:n_in + n_out + 2]
        token = refs[-1]
        x, y, c = _position()
        for k, (sa, si, da, _, peer, li) in enumerate(plan(x, y, c)):
            cp = pltpu.make_async_remote_copy(
                src_ref=_at(in_refs[sa], si), dst_ref=_at(land_refs[da], li), send_sem=send_sems.at[k], recv_sem=recv_sems.at[k],
                device_id=peer, device_id_type=MESH)
            cp.wait_send()
            cp.wait_recv()
        token[...] = jnp.zeros_like(token)

    hbm, sem = pl.BlockSpec(memory_space=pltpu.HBM), pl.BlockSpec(memory_space=pltpu.SEMAPHORE)
    bufs = started["ins"] + started["lands"]
    outs = pl.pallas_call(
        body, name=started["name"] + "_wait", out_shape=(*[pltpu.HBM(a.shape, a.dtype) for a in bufs], _sds((SUBLANES, LANES), F32)),
        in_specs=[hbm] * len(bufs) + [sem, sem, pl.BlockSpec(memory_space=pl.ANY)], out_specs=(*[hbm] * len(bufs), pl.BlockSpec(memory_space=pltpu.VMEM)),
        input_output_aliases={i: i for i in range(len(bufs))},
        compiler_params=pltpu.CompilerParams(has_side_effects=pltpu.SideEffectType.DATAFLOW_SIDE_EFFECTING),
    )(*bufs, *started["sems"], after)
    return list(outs[:n_in]), list(outs[n_in:n_in + n_out]), outs[-1]


def _after(value, token):
    return value + token[0, 0].astype(value.dtype)


def _chip_peers(x, y, c):
    return [(1 - x, y, c), (x, 1 - y, c), (1 - x, 1 - y, c)]


def _chip_of(p):
    return 2 * p[0] + p[1]


def _set_slot(slots, me, blk):
    return lax.dynamic_update_slice(slots, blk[None], (me,) + (0,) * blk.ndim)


def _by_core(c, mine, other, axis):
    return jnp.where(c == 0, jnp.stack([mine, other], axis), jnp.stack([other, mine], axis))


def _plan_gather_chips(n):
    def plan(x, y, c):
        return [(a, (), a, (2 * x + y,), p, (_chip_of(p),)) for a in range(n) for p in _chip_peers(x, y, c)]

    return plan


def _gather_chips(name, blocks):
    recv, token = _exchange(name, blocks, [((4,) + b.shape, b.dtype) for b in blocks], _plan_gather_chips(len(blocks)))
    x, y, _ = _position()
    return [_set_slot(r, 2 * x + y, b) for r, b in zip(recv, blocks)], token


def _gather_pass_cores(name, blocks, from_chips):
    n = len(blocks)

    def plan_cores(x, y, c):
        me, sib = 2 * x + y, (x, y, 1 - c)
        own = [(a, (), a, (me,), sib, (me,)) for a in range(n)]
        passed = [(n + a, (_chip_of(p),), a, (_chip_of(p),), sib, (_chip_of(p),)) for a in range(n) for p in _chip_peers(x, y, c)]
        return own + passed

    from_core, token = _exchange(name + "_cores", list(blocks) + list(from_chips), [((4,) + b.shape, b.dtype) for b in blocks], plan_cores)
    x, y, c = _position()
    return [_by_core(c, _set_slot(r1, 2 * x + y, b), r2, 1) for b, r1, r2 in zip(blocks, from_chips, from_core)], token


def _gather_two_level(name, blocks):
    from_chips, _ = _exchange(name + "_chips", blocks, [((4,) + b.shape, b.dtype) for b in blocks], _plan_gather_chips(len(blocks)))
    return _gather_pass_cores(name, blocks, from_chips)


def _pair_add(name, g42, r4):
    _, _, rh, cols = g42.shape
    tr = _divisors(rh, 512, SUBLANES * 2)[-1]

    def body(c_ref, a_ref, b_ref, o_ref):
        o_ref[0] = (a_ref[0, 0].astype(F32) + b_ref[0].astype(F32)).astype(o_ref.dtype)

    cidx = lax.axis_index("c").astype(jnp.int32).reshape(1)
    return _pcall(
        body, name, _sds(r4.shape, BF16), grid=(4, rh // tr),
        in_specs=[pl.BlockSpec((1, 1, tr, cols), lambda s, i, c_ref: (s, c_ref[0], i, 0)), pl.BlockSpec((1, tr, cols), lambda s, i, c_ref: (s, i, 0))],
        out_specs=pl.BlockSpec((1, tr, cols), lambda s, i, c_ref: (s, i, 0)), sem=("parallel", "parallel"), prefetch=1,
    )(cidx, g42, r4)


def _slot_sum(name, parts):
    n, r, cols = parts.shape
    cap = max(2 * SUBLANES, (4 * 1024 * 1024) // (n * cols * parts.dtype.itemsize))
    tr = _divisors(r, cap, 2 * SUBLANES)[-1]

    def body(p_ref, o_ref):
        acc = p_ref[0].astype(F32)
        for k in range(1, n):
            acc = acc + p_ref[k].astype(F32)
        o_ref[...] = acc

    return _pcall(
        body, name, _sds((r, cols), F32), grid=(r // tr,), in_specs=[pl.BlockSpec((n, tr, cols), lambda i: (0, i, 0))],
        out_specs=pl.BlockSpec((tr, cols), lambda i: (i, 0)), sem=("parallel",),
    )(parts)


def _slot_sums(name, parts):
    k = len(parts)

    def body(*refs):
        for p_ref, o_ref in zip(refs[:k], refs[k:]):
            acc = p_ref[0].astype(F32)
            for j in range(1, p_ref.shape[0]):
                acc = acc + p_ref[j].astype(F32)
            o_ref[...] = acc

    return list(_pcall(body, name, [_sds(p.shape[1:], F32) for p in parts])(*parts))


def _gather_weights_start(name, shards, token):
    c = lax.axis_index("c")
    halves = [lax.dynamic_slice_in_dim(w, c * (w.shape[0] // 2), w.shape[0] // 2, 0).astype(BF16) for w in shards]
    if token is not None:
        halves[0] = _after(halves[0], token)
    n = len(halves)

    def plan(x, y, c):
        return [(a, (), a, (2 * x + y, c), p, (_chip_of(p), c)) for a in range(n) for p in _chip_peers(x, y, c)]

    return _exchange_start(name + "_ag_chips", halves, [((4, 2) + h.shape, h.dtype) for h in halves], plan)


def _gather_weights_finish(name, started, after):
    halves, slots, token = _exchange_wait(started, after)
    n = len(halves)
    x, y, c = _position()
    slots = [lax.dynamic_update_slice(s, h[None, None], (2 * x + y, c, 0, 0)) for s, h in zip(slots, halves)]

    def plan_cores(x, y, c):
        return [(a, (s, c), a, (s, c), (x, y, 1 - c), (s, 1 - c)) for a in range(n) for s in range(4)]

    full, _ = _exchange(name + "_ag_cores", slots, None, plan_cores, inplace=True)
    return [f.reshape((4, 2 * h.shape[0], h.shape[1])) for f, h in zip(full, halves)], token


def _reduce_weights_start(name, grads, token):
    n = len(grads)
    g42 = [g.reshape(4, 2, g.shape[1] // 2, g.shape[2]) for g in grads]

    def plan_swap(x, y, c):
        return [(a, (s, 1 - c), a, (s,), (x, y, 1 - c), (s,)) for a in range(n) for s in range(4)]

    def plan_chips(x, y, c):
        return [(a, (_chip_of(p),), a, (2 * x + y,), p, (_chip_of(p),)) for a in range(n) for p in _chip_peers(x, y, c)]

    other, _ = _exchange(name + "_rs_cores", g42, [((4,) + g.shape[2:], g.dtype) for g in g42], plan_swap, after=token)
    pair = [_pair_add(f"{name}_rs_pair{a}", g, o) for a, (g, o) in enumerate(zip(g42, other))]
    return _exchange_start(name + "_rs_chips", pair, [(p.shape, p.dtype) for p in pair], plan_chips)


def _reduce_weights_finish(name, starteds, after):
    pair, recv = [], []
    for started in starteds:
        p, r, token = _exchange_wait(started, after)
        pair, recv = pair + p, recv + r
    n = len(pair)
    x, y, c = _position()
    me = 2 * x + y

    def plan_share(x, y, c):
        return [(a, (c,), a, (c,), (x, y, 1 - c), (1 - c,)) for a in range(n)]

    parts = [lax.dynamic_update_slice(r, lax.dynamic_slice_in_dim(p, me, 1, 0), (me, 0, 0)) for r, p in zip(recv, pair)]
    mine = [_slot_sum(f"{name}_rs_sum{a}", p) for a, p in enumerate(parts)]
    both = [lax.dynamic_update_slice(lax.empty((2,) + m.shape, m.dtype), m[None], (c, 0, 0)) for m in mine]
    both, _ = _exchange(name + "_rs_share", both, None, plan_share, inplace=True)
    return [b.reshape(2 * b.shape[1], b.shape[2]) for b in both], token


def _allreduce_small(name, grads):
    both, token = _gather_two_level(name + "_ag", [g.astype(BF16) if g.size >= SMALL_BF16_SIZE else g for g in grads])
    return _slot_sums(name + "_sum", [g.reshape((8,) + g.shape[2:]) for g in both]), token


def _ada_fwd_call(name, c_all, w, b_shard):
    nl, d, ns = w.shape
    nb = c_all.shape[0]

    def body(c_ref, w_ref, b_ref, o_ref):
        cv = c_ref[...]
        o_ref[0] = _dg(cv * _sigmoid(cv), w_ref[0], ((1,), (0,))) + b_ref[0]

    return _pcall(
        body, name, _sds((nl, nb, ns), F32), grid=(nl,),
        in_specs=[pl.BlockSpec((nb, d), lambda l: (0, 0)), pl.BlockSpec((1, d, ns), lambda l: (l, 0, 0)), pl.BlockSpec((1, 1, ns), lambda l: (l, 0, 0))],
        out_specs=pl.BlockSpec((1, nb, ns), lambda l: (l, 0, 0)), sem=("parallel",),
    )(c_all, w, b_shard)


def _ada_bwd_call(name, c_all, dm_shard, dm_all):
    nl, nb, ns = dm_shard.shape
    d = c_all.shape[1]
    nm = dm_all.shape[2]

    def body(c_ref, ds_ref, da_ref, dw_ref, db_ref):
        cv = c_ref[...]
        dw_ref[0] = _dg(cv * _sigmoid(cv), ds_ref[0], ((0,), (0,)))
        db_ref[0] = jnp.sum(da_ref[0], axis=0, keepdims=True)

    return _pcall(
        body, name, [_sds((nl, d, ns), F32), _sds((nl, 1, nm), F32)], grid=(nl,),
        in_specs=[pl.BlockSpec((nb, d), lambda l: (0, 0)), pl.BlockSpec((1, nb, ns), lambda l: (l, 0, 0)), pl.BlockSpec((1, nb, nm), lambda l: (l, 0, 0))],
        out_specs=[pl.BlockSpec((1, d, ns), lambda l: (l, 0, 0)), pl.BlockSpec((1, 1, nm), lambda l: (l, 0, 0))], sem=("parallel",),
    )(c_all, dm_shard, dm_all)


def _ada_fwd(name, bl, c_all, w, b):
    nl, d, ns = w.shape
    chip = 2 * lax.axis_index("x") + lax.axis_index("y")
    b_shard = lax.dynamic_slice(b, (0, chip * ns), (nl, ns)).reshape(nl, 1, ns)
    shard = _ada_fwd_call(name + "_fwd", c_all, w, b_shard)
    (allc,), token = _gather_chips(name + "_ag", [shard])
    mods = jnp.transpose(allc, (1, 2, 0, 3)).reshape(nl, c_all.shape[0], 4 * ns)
    return lax.dynamic_slice(mods, (0, (2 * chip + lax.axis_index("c")) * bl, 0), (nl, bl, 4 * ns)), token


def _ada_bwd(name, bl, c_all, ns, dm):
    nl = dm.shape[0]
    chip = 2 * lax.axis_index("x") + lax.axis_index("y")
    (dm_all,), token = _gather_two_level(name + "_bwd_ag", [dm])
    dm_all = jnp.transpose(dm_all.reshape((8,) + dm.shape), (1, 0, 2, 3)).reshape(nl, 8 * bl, 4 * ns)
    dm_shard = lax.dynamic_slice(dm_all, (0, 0, chip * ns), (nl, 8 * bl, ns))
    dw, db = _ada_bwd_call(name + "_bwd", c_all, dm_shard, dm_all)
    return dw, db.reshape(nl, 4 * ns), token


def _adamw(name, w, g, m, v):
    shape = w.shape
    cols = shape[-1]
    w2, g2, m2, v2 = (t.reshape(-1, cols) for t in (w, g, m, v))
    rows = w2.shape[0]
    cap = max(SUBLANES, (512 * 1024) // max(cols, 1) // SUBLANES * SUBLANES)
    tr = _divisors(rows, cap, SUBLANES)[-1]

    def body(w_ref, g_ref, m_ref, v_ref, d_ref, mo_ref, vo_ref):
        gv = g_ref[...]
        mn = ADAM_B1 * m_ref[...] + (1.0 - ADAM_B1) * gv
        vn = ADAM_B2 * v_ref[...] + (1.0 - ADAM_B2) * (gv * gv)
        m_hat = mn / (1.0 - ADAM_B1 ** ADAM_STEP)
        v_hat = vn / (1.0 - ADAM_B2 ** ADAM_STEP)
        d_ref[...] = -ADAM_LR * (m_hat / (jnp.sqrt(v_hat) + ADAM_EPS) + ADAM_WD * w_ref[...])
        mo_ref[...] = mn
        vo_ref[...] = vn

    spec = pl.BlockSpec((tr, cols), lambda i: (i, 0))
    outs = _pcall(body, name, [_sds((rows, cols), F32)] * 3, grid=(rows // tr,), in_specs=[spec] * 4, out_specs=[spec] * 3, sem=("parallel",))(w2, g2, m2, v2)
    return tuple(o.reshape(shape) for o in outs)


def _adamw_small(name, ws, gs, ms, vs):
    n = len(ws)

    def body(*refs):
        for k in range(n):
            w_ref, g_ref, m_ref, v_ref = (refs[j * n + k] for j in range(4))
            d_ref, mo_ref, vo_ref = (refs[(4 + j) * n + k] for j in range(3))
            gv = g_ref[...]
            mn = ADAM_B1 * m_ref[...] + (1.0 - ADAM_B1) * gv
            vn = ADAM_B2 * v_ref[...] + (1.0 - ADAM_B2) * (gv * gv)
            m_hat = mn / (1.0 - ADAM_B1 ** ADAM_STEP)
            v_hat = vn / (1.0 - ADAM_B2 ** ADAM_STEP)
            d_ref[...] = -ADAM_LR * (m_hat / (jnp.sqrt(v_hat) + ADAM_EPS) + ADAM_WD * w_ref[...])
            mo_ref[...] = mn
            vo_ref[...] = vn

    outs = _pcall(body, name, [_sds(w.shape, F32) for w in ws] * 3)(*ws, *gs, *ms, *vs)
    return outs[:n], outs[n:2 * n], outs[2 * n:]


def _pad_rows(w, rows):
    return jnp.pad(w, ((0, rows - w.shape[0]), (0, 0)))


def _pad_lanes(v):
    return jnp.pad(v, (0, LANES - v.shape[0])).reshape(1, LANES)


BIG = ("w_in", "w_out", "ff_up", "ff_down")
BIG_AXIS = {"w_in": 1, "w_out": 0, "ff_up": 1, "ff_down": 0}
CONVW = ("ssd_conv_w", "ff_conv_w")
SMALL = ("norm1_g", "norm2_g", "ssd_conv_b", "ssd_dt_bias", "ssd_a_log", "ssd_d", "ssd_norm_g", "gm_vnorm_g", "gm_ws", "gm_bs", "gm_out_g", "ff_conv_b")
WEIGHTS = ("ada_w", "ada_b", "norm1_g", "norm2_g", "w_in", "ssd_conv_w", "ssd_conv_b", "ssd_dt_bias", "ssd_a_log", "ssd_d", "ssd_norm_g", "gm_vnorm_g", "gm_ws", "gm_bs", "gm_out_g", "w_out", "ff_up", "ff_conv_w", "ff_conv_b", "ff_down", "final_g")


def kernel(x, c, ada_w, ada_b, norm1_g, norm2_g, w_in, ssd_conv_w, ssd_conv_b, ssd_dt_bias, ssd_a_log, ssd_d, ssd_norm_g, gm_vnorm_g, gm_ws, gm_bs, gm_out_g, w_out, ff_up, ff_conv_w, ff_conv_b, ff_down, final_g, loss_target, m_ada_w, m_ada_b, m_norm1_g, m_norm2_g, m_w_in, m_ssd_conv_w, m_ssd_conv_b, m_ssd_dt_bias, m_ssd_a_log, m_ssd_d, m_ssd_norm_g, m_gm_vnorm_g, m_gm_ws, m_gm_bs, m_gm_out_g, m_w_out, m_ff_up, m_ff_conv_w, m_ff_conv_b, m_ff_down, m_final_g, v_ada_w, v_ada_b, v_norm1_g, v_norm2_g, v_w_in, v_ssd_conv_w, v_ssd_conv_b, v_ssd_dt_bias, v_ssd_a_log, v_ssd_d, v_ssd_norm_g, v_gm_vnorm_g, v_gm_ws, v_gm_bs, v_gm_out_g, v_w_out, v_ff_up, v_ff_conv_w, v_ff_conv_b, v_ff_down, v_final_g):
    given = dict(locals())
    weights = {n: given[n] for n in WEIGHTS}
    bl, s, d = x.shape
    nl = ada_w.shape[0]
    heads = d // HEAD_DIM
    cd = d + 2 * SSD_GROUPS * STATE
    f = ff_down.shape[1] * 4
    n_in = d + cd + heads + 2 * d
    n_proj = _round_up(3 * d + cd + LANES, 2 * LANES)
    tm = _divisors(s, 512)[-1]

    gathering = {}

    def start_gathers(l, names, token):
        for n in names:
            gathering[l, n] = _gather_weights_start(f"wg{l}_{n}", [weights[n][l]], token)
            token = gathering[l, n]["token"]
        return token

    pre, _ = _gather_two_level("pre_ag", [c] + [weights[n] for n in CONVW])
    c_all = pre[0].reshape(8 * bl, d)
    conv_full = {n: jnp.concatenate([p[k, 0] for k in range(4)], axis=-1) for n, p in zip(CONVW, pre[1:])}
    mods_all, ada_token = _ada_fwd("ada", bl, c_all, ada_w, ada_b)
    mods = mods_all.reshape(nl, bl, N_MOD, 1, d)

    def landed(l, n, after):
        (full,), token = _gather_weights_finish(f"wg{l}_{n}", gathering.pop((l, n)), after)
        if n == "w_in":
            return full, token
        return (full.reshape(-1, full.shape[2]) if BIG_AXIS[n] == 0 else jnp.concatenate([full[k] for k in range(4)], axis=1)), token

    issued = start_gathers(0, BIG, ada_token)
    r2 = lambda v: v.reshape(1, -1)
    n_gm = d // GM_HEAD
    fg = r2(final_g)

    def layer_params(l):
        return dict(
            norm1=r2(norm1_g[l]), norm2=r2(norm2_g[l]), conv_w8=_pad_rows(conv_full["ssd_conv_w"][l], SUBLANES), conv_b=r2(ssd_conv_b[l]),
            dtb=_pad_lanes(ssd_dt_bias[l]), alog=_pad_lanes(ssd_a_log[l]), dsk=r2(jnp.repeat(ssd_d[l], HEAD_DIM)), ng=r2(ssd_norm_g[l]),
            vg=r2(gm_vnorm_g[l]), ws=gm_ws[l], bs_t=jnp.pad(gm_bs[l].T, ((0, 0), (0, LANES - n_gm))), og=r2(gm_out_g[l]),
            ff_w8=_pad_rows(conv_full["ff_conv_w"][l], SUBLANES), ff_b=r2(ff_conv_b[l]))

    def mixer_args(p):
        return (p["conv_w8"], p["conv_b"], p["dtb"], p["alog"], p["dsk"], p["ng"], p["vg"], p["ws"], p["bs_t"], p["og"])

    def padded_w_in(full):
        ns = full.shape[2]

        def cols(a, b):
            return [full[k][:, max(a, k * ns) - k * ns:min(b, (k + 1) * ns) - k * ns] for k in range(4) if max(a, k * ns) < min(b, (k + 1) * ns)]

        parts = cols(0, d) + cols(d + cd + heads, n_in) + cols(d, d + cd) + cols(d + cd, d + cd + heads)
        return jnp.concatenate(parts + [jnp.zeros((d, n_proj - (3 * d + cd + heads)), BF16)], axis=1)

    saved, xcur, pending = [], x, None
    for l in range(nl):
        p, w = layer_params(l), {}
        wi, token = landed(l, "w_in", mods_all if l == 0 else pending[0])
        w["w_in"] = padded_w_in(wi)
        sh1, sc1, g1, sh2, sc2, g2 = (mods[l, :, k] for k in range(N_MOD))
        if l + 1 < nl:
            issued = start_gathers(l + 1, BIG, token + issued)
        sc1 = _after(sc1, issued)
        if pending is None:
            (h,) = _row_fwd(f"nm{l}_fwd", _nm_tile, [(xcur, d, 0, None)], [sc1, sh1], [p["norm1"]], [(d, BF16)], 0, tm)
            x1 = xcur
        else:
            rows = [(xcur, d, 0, None), (pending[0], d, 0, None)]
            x1, h = _row_fwd(f"rnm{l}a_fwd", _rnm_tile, rows, [pending[1], sc1, sh1], [p["norm1"]], [(d, F32), (d, BF16)], 0, tm)
        proj = _mm(f"win{l}_fwd", h, w["w_in"], F32)
        ycat, (xbc, hprev) = _mixer_fwd(f"mix{l}", proj, d, cd, *mixer_args(p))
        w["w_out"], _ = landed(l, "w_out", ycat)
        mix = _mm(f"wout{l}_fwd", ycat, w["w_out"], F32)
        x2, h2 = _row_fwd(f"rnm{l}b_fwd", _rnm_tile, [(x1, d, 0, None), (mix, d, 0, None)], [g1, sc2, sh2], [p["norm2"]], [(d, F32), (d, BF16)], 0, tm)
        w_up, _ = landed(l, "ff_up", h2)
        w["ff_up"] = _gate_value_blocks(f"ffup{l}_blocks", w_up, f)
        up = _mm(f"ffup{l}_fwd", h2, w["ff_up"], F32)
        act = _conv_fwd(f"ffact{l}_fwd", up, 0, f, p["ff_w8"], p["ff_b"], FF_CONV, True, BF16)
        w["ff_down"], _ = landed(l, "ff_down", act)
        down = _mm(f"ffdown{l}_fwd", act, w["ff_down"], F32)
        saved.append(dict(p=p, w=w, x_in=xcur, pending=pending, h=h, proj=proj, xbc=xbc, hprev=hprev, ycat=ycat, x1=x1, mix=mix, h2=h2, up=up, act=act))
        xcur, pending = x2, (down, g2)
    rows = [(xcur, d, 0, F32), (pending[0], d, 0, BF16), (loss_target, d, 0, None)]

    (dx2, ddown), (dg2,), (dfg,), (loss_local,) = _row_bwd(
        "final_bwd", _final_tile, rows, [pending[1]], [fg], [], 1, tm, [jnp.ones((1, 1), F32)], emit_sums=True)
    loss = lax.psum(loss_local[0, 0], AXES)
    dmods, small_grads = [None] * nl, [None] * nl
    reducing = {}

    def start_reduce(l, named, token=None):
        by_chip = []
        for n, g in named:
            width = g.shape[1] // 4
            by_chip.append(g.reshape(4, g.shape[0] // 4, g.shape[1]) if BIG_AXIS[n] == 0 else jnp.stack([g[:, k * width:(k + 1) * width] for k in range(4)]))
        key = (l, tuple(n for n, _ in named))
        reducing[key] = _reduce_weights_start(f"wg{l}_" + "_".join(key[1]), by_chip, token)
        return reducing[key]["token"]

    for l in reversed(range(nl)):
        sv = saved[l]
        p, w = sv["p"], sv["w"]
        sh1, sc1, g1, sh2, sc2, g2 = (mods[l, :, k] for k in range(N_MOD))
        dact, dw_down = _mm_bwd(f"ffdown{l}", sv["act"], w["ff_down"], ddown)
        dup, dff_w8, dff_b = _conv_bwd(f"ffact{l}_bwd", sv["up"], 0, f, p["ff_w8"], p["ff_b"], FF_CONV, True, dact, BF16)
        dh2, dw_up = _mm_bwd(f"ffup{l}", sv["h2"], w["ff_up"], dup)
        dw_up = _gate_value_blocks(f"ffup{l}_columns", dw_up, f, inverse=True)
        sc2 = _after(sc2, start_reduce(l, [("ff_down", dw_down), ("ff_up", dw_up)]))
        rows = [(sv["x1"], d, 0, F32), (sv["mix"], d, 0, BF16)]
        (dx1, dmix), (dg1, dsc2, dsh2), (dn2,) = _row_bwd(f"rnm{l}b_bwd", _rnm_tile, rows, [g1, sc2, sh2], [p["norm2"]], [(d, F32), (d, BF16)], 0, tm, [dx2, dh2])
        dycat, dw_out = _mm_bwd(f"wout{l}", sv["ycat"], w["w_out"], dmix)
        p_tied = p
        if l == 0:
            p_tied = dict(p, dtb=_after(p["dtb"], start_reduce(l, [("w_out", dw_out)])))
        dproj, (dw8, dcb, ddtb, dalog, ddsk, dng, dvg, dws, dbs_t, dog) = _mixer_bwd(f"mix{l}", sv["proj"], d, cd, *mixer_args(p_tied), sv["xbc"], sv["hprev"], dycat)
        dh, dw_in_p = _mm_bwd(f"win{l}", sv["h"], w["w_in"], dproj)
        dw_in = jnp.concatenate([dw_in_p[:, :d], dw_in_p[:, 3 * d:3 * d + cd], dw_in_p[:, 3 * d + cd:3 * d + cd + heads], dw_in_p[:, d:3 * d]], axis=1)
        if l > 0:
            sc1 = _after(sc1, start_reduce(l, [("w_out", dw_out), ("w_in", dw_in)]))
        if sv["pending"] is None:
            (dx2,), (dsc1, dsh1), (dn1,) = _row_bwd(f"nm{l}_bwd", _nm_res_tile, [(sv["x_in"], d, 0, F32)], [sc1, sh1], [p["norm1"]], [(d, F32), (d, BF16)], 0, tm, [dx1, dh])
        else:
            rows = [(sv["x_in"], d, 0, F32), (sv["pending"][0], d, 0, BF16)]
            (dx2, ddown), (dg2_prev, dsc1, dsh1), (dn1,) = _row_bwd(
                f"rnm{l}a_bwd", _rnm_tile, rows, [sv["pending"][1], sc1, sh1], [p["norm1"]], [(d, F32), (d, BF16)], 0, tm, [dx1, dh])
        dmods[l] = jnp.concatenate([dsh1, dsc1, dg1, dsh2, dsc2, dg2], axis=1).reshape(bl, N_MOD * d)
        if sv["pending"] is not None:
            dg2 = dg2_prev
        small_grads[l] = dict(
            norm1_g=dn1.reshape(d), norm2_g=dn2.reshape(d), ssd_conv_b=dcb.reshape(cd), ssd_dt_bias=ddtb[0, :heads], ssd_a_log=dalog[0, :heads],
            ssd_d=ddsk.reshape(heads, HEAD_DIM).sum(-1), ssd_norm_g=dng.reshape(d), gm_vnorm_g=dvg.reshape(d), gm_ws=dws, gm_bs=dbs_t[:, :n_gm].T,
            gm_out_g=dog.reshape(d), ff_conv_b=dff_b.reshape(f), ssd_conv_w=dw8[:SSD_CONV], ff_conv_w=dff_w8[:FF_CONV])
    grad_x = dx2

    g_ada_w, g_ada_b, ada_token = _ada_bwd("ada", bl, c_all, ada_w.shape[2], jnp.stack(dmods))
    small_names_r = SMALL + CONVW + ("final_g",)
    summed, small_token = _allreduce_small("small", [jnp.stack([small_grads[l][n] for l in range(nl)]) for n in SMALL + CONVW] + [dfg])
    travelling = start_reduce(0, [("w_in", dw_in)], ada_token + small_token)
    grad = {"ada_w": g_ada_w, "ada_b": g_ada_b}
    grad.update(zip(small_names_r, summed))
    chip = 2 * lax.axis_index("x") + lax.axis_index("y")
    for n in CONVW:
        width = weights[n].shape[-1]
        grad[n] = lax.dynamic_slice_in_dim(grad[n], chip * width, width, axis=2)

    delta, new_m, new_v = {}, {}, {}
    last = (0, ("w_in",))
    early = [k for k in reducing if k != last]
    finished, _ = _reduce_weights_finish("wg_early", [reducing[k] for k in early], _after(summed[0], travelling))
    big_grads = dict(zip([(l, n) for l, names in early for n in names], finished))
    order = ("ada_w",) + tuple(reversed(BIG))
    for n in order:
        if n == "w_in":
            (big_grads[0, n],), _ = _reduce_weights_finish("wg_last", [reducing[last]], delta[order[-2]])
        if n != "ada_w":
            grad[n] = jnp.stack([big_grads[l, n] for l in range(nl)])
        delta[n], new_m[n], new_v[n] = _adamw("adam_" + n, weights[n], grad[n], given["m_" + n], given["v_" + n])
    small_names = ("ada_b",) + SMALL + CONVW + ("final_g",)
    as2d = lambda t: t.reshape(1, -1) if t.ndim == 1 else t
    res = _adamw_small(
        "adam_small", [as2d(weights[n]) for n in small_names], [as2d(grad[n]) for n in small_names],
        [as2d(given["m_" + n]) for n in small_names], [as2d(given["v_" + n]) for n in small_names])
    for out, vals in zip((delta, new_m, new_v), res):
        for n, val in zip(small_names, vals):
            out[n] = val.reshape(weights[n].shape)
    grad["final_g"] = grad["final_g"].reshape(final_g.shape)

    return (loss, grad_x, *[grad[n] for n in WEIGHTS], *[delta[n] for n in WEIGHTS], *[new_m[n] for n in WEIGHTS], *[new_v[n] for n in WEIGHTS])
```

```python
import jax
import jax.numpy as jnp
from jax import lax
from jax.experimental import pallas as pl
from jax.experimental.pallas import tpu as pltpu

F32 = jnp.float32
BF16 = jnp.bfloat16
EPS = 1e-6
CHUNK = 128
HEAD_DIM = 64
STATE = 128
GM_HEAD = 128
SSD_GROUPS = 2
SSD_CONV = 4
FF_CONV = 3
N_MOD = 6
LANES = 128
SUBLANES = 8
SMALL_BF16_SIZE = 1 << 16
CONV_LANES = 256
CONV_ROWS = 32
V7X_VMEM_LIMIT = 48 * 1024 * 1024
MM_VMEM_BUDGET = 30 * 1024 * 1024
MM_STEP_MACS = 2048 * 1024 * 1024
MM_ACC_SECONDS_PER_VREG = 1.4e-9
V7X_MXU_FLOPS = 996e12
V7X_HBM_BYTES_PER_S = 3.3e12
V7X_STEP_SECONDS = 0.35e-6
ADAM_LR, ADAM_B1, ADAM_B2, ADAM_EPS, ADAM_WD, ADAM_STEP = 0.001, 0.9, 0.999, 1e-08, 0.01, 10
MESH = pl.DeviceIdType.MESH
AXES = ("x", "y", "c")


def _round_up(n, m):
    return (n + m - 1) // m * m


def _divisors(n, cap, mult=LANES):
    out = [t for t in range(mult, min(n, cap) + 1, mult) if n % t == 0]
    return out or [n]


def _pcall(body, name, out_shape, grid=(), in_specs=None, out_specs=None, scratch=(), sem=None, prefetch=0):
    params = pltpu.CompilerParams(dimension_semantics=sem, vmem_limit_bytes=V7X_VMEM_LIMIT)
    if prefetch:
        spec = pltpu.PrefetchScalarGridSpec(num_scalar_prefetch=prefetch, grid=grid, in_specs=in_specs, out_specs=out_specs, scratch_shapes=list(scratch))
        return pl.pallas_call(body, name=name, out_shape=out_shape, grid_spec=spec, compiler_params=params)
    if in_specs is None:
        return pl.pallas_call(body, name=name, out_shape=out_shape, compiler_params=params)
    return pl.pallas_call(body, name=name, out_shape=out_shape, grid=grid, in_specs=in_specs, out_specs=out_specs, scratch_shapes=list(scratch), compiler_params=params)


def _sds(shape, dtype):
    return jax.ShapeDtypeStruct(tuple(shape), dtype)


def _mm_tiles(m, n, k, a_bytes, b_bytes, o_bytes):
    best, best_key = None, None
    for tm in _divisors(m, 2048):
        for tn in _divisors(n, 2560):
            for tk in _divisors(k, 2560):
                vmem = 2 * (tm * tk * a_bytes + tk * tn * b_bytes + tm * tn * o_bytes) + tm * tn * 4
                if vmem > MM_VMEM_BUDGET or tm * tn * tk > MM_STEP_MACS:
                    continue
                ni, nj, nk = m // tm, n // tn, k // tk
                a_reads = 1 if nk == 1 else nj
                b_reads = 1 if (nk == 1 and nj == 1) else ni
                hbm = a_reads * m * k * a_bytes + b_reads * k * n * b_bytes + m * n * o_bytes
                t = max(2.0 * m * n * k / V7X_MXU_FLOPS, hbm / V7X_HBM_BYTES_PER_S) + ni * nj * nk * V7X_STEP_SECONDS
                if nk > 1:
                    t += ni * nj * nk * (tm * tn // (SUBLANES * LANES)) * MM_ACC_SECONDS_PER_VREG
                key = (-t, tm * tn * tk)
                if best_key is None or key > best_key:
                    best, best_key = (tm, tn, tk), key
    return best


def _matmul(name, a, b, mode, out_dtype):
    if mode == "nn":
        (m, k), n = a.shape, b.shape[1]
    elif mode == "nt":
        (m, k), n = a.shape, b.shape[0]
    else:
        (k, m), n = a.shape, b.shape[1]
    tm, tn, tk = _mm_tiles(m, n, k, a.dtype.itemsize, b.dtype.itemsize, jnp.dtype(out_dtype).itemsize)
    nk = k // tk
    if mode == "nn":
        a_spec = pl.BlockSpec((tm, tk), lambda i, j, kk: (i, kk))
        b_spec = pl.BlockSpec((tk, tn), lambda i, j, kk: (kk, j))
        dims = ((1,), (0,))
    elif mode == "nt":
        a_spec = pl.BlockSpec((tm, tk), lambda i, j, kk: (i, kk))
        b_spec = pl.BlockSpec((tn, tk), lambda i, j, kk: (j, kk))
        dims = ((1,), (1,))
    else:
        a_spec = pl.BlockSpec((tk, tm), lambda i, j, kk: (kk, i))
        b_spec = pl.BlockSpec((tk, tn), lambda i, j, kk: (kk, j))
        dims = ((0,), (0,))

    def body(a_ref, b_ref, o_ref, acc_ref):
        kk = pl.program_id(2)
        p = lax.dot_general(a_ref[...].astype(BF16), b_ref[...].astype(BF16), (dims, ((), ())), preferred_element_type=F32)
        if nk == 1:
            o_ref[...] = p.astype(o_ref.dtype)
        else:
            @pl.when(kk == 0)
            def _():
                acc_ref[...] = p

            @pl.when(kk > 0)
            def _():
                acc_ref[...] += p

            @pl.when(kk == nk - 1)
            def _():
                o_ref[...] = acc_ref[...].astype(o_ref.dtype)

    return _pcall(
        body, name, _sds((m, n), out_dtype), grid=(m // tm, n // tn, nk), in_specs=[a_spec, b_spec],
        out_specs=pl.BlockSpec((tm, tn), lambda i, j, kk: (i, j)), scratch=[pltpu.VMEM((tm, tn), F32)],
        sem=("parallel", "parallel", "arbitrary"),
    )(a, b)


def _mm(name, a, w, out_dtype):
    return _matmul(name, a.reshape(-1, a.shape[-1]), w, "nn", out_dtype).reshape(a.shape[:-1] + (w.shape[1],))


def _mm_bwd(name, a, w, dy):
    a2, dy2 = a.reshape(-1, a.shape[-1]), dy.reshape(-1, dy.shape[-1])
    return _matmul(name + "_dx", dy2, w, "nt", BF16).reshape(a.shape), _matmul(name + "_dw", a2, dy2, "tn", BF16)


def _dg(a, b, dims):
    return lax.dot_general(a.astype(BF16), b.astype(BF16), (dims, ((), ())), preferred_element_type=F32)


@jax.custom_vjp
def _dot_nn(a, b):
    return _dg(a, b, ((1,), (0,)))


_dot_nn.defvjp(lambda a, b: (_dot_nn(a, b), (a, b)), lambda r, d: (_dg(d, r[1], ((1,), (1,))), _dg(r[0], d, ((0,), (0,)))))


@jax.custom_vjp
def _dot_nt(a, b):
    return _dg(a, b, ((1,), (1,)))


_dot_nt.defvjp(lambda a, b: (_dot_nt(a, b), (a, b)), lambda r, d: (_dg(d, r[1], ((1,), (0,))), _dg(d, r[0], ((0,), (0,)))))


@jax.custom_vjp
def _dot_tn(a, b):
    return _dg(a, b, ((0,), (0,)))


_dot_tn.defvjp(lambda a, b: (_dot_tn(a, b), (a, b)), lambda r, d: (_dg(r[1], d, ((1,), (1,))), _dg(r[0], d, ((1,), (0,)))))


def _exact_dot(a, c, dims):
    hi = a.astype(BF16)
    r1 = a - hi.astype(F32)
    mid = r1.astype(BF16)
    lo = (r1 - mid.astype(F32)).astype(BF16)
    cb = c.astype(BF16)
    f = lambda t: lax.dot_general(t, cb, (dims, ((), ())), preferred_element_type=F32)
    return f(hi) + f(mid) + f(lo)


@jax.custom_vjp
def _sel_right(a, c):
    return _exact_dot(a, c, ((1,), (0,)))


_sel_right.defvjp(lambda a, c: (_sel_right(a, c), c), lambda c, d: (_exact_dot(d, c, ((1,), (1,))), jnp.zeros_like(c)))


def _exact_dot_left(c, a, dims):
    hi = a.astype(BF16)
    r1 = a - hi.astype(F32)
    mid = r1.astype(BF16)
    lo = (r1 - mid.astype(F32)).astype(BF16)
    cb = c.astype(BF16)
    f = lambda t: lax.dot_general(cb, t, (dims, ((), ())), preferred_element_type=F32)
    return f(hi) + f(mid) + f(lo)


@jax.custom_vjp
def _sel_left(c, a):
    return _exact_dot_left(c, a, ((1,), (0,)))


_sel_left.defvjp(lambda c, a: (_sel_left(c, a), c), lambda c, d: (jnp.zeros_like(c), _exact_dot_left(c, d, ((0,), (0,)))))


def _sigmoid(x):
    return 1.0 / (1.0 + jnp.exp(-x))


def _rms(x, g):
    return x * lax.rsqrt(jnp.mean(x * x, axis=-1, keepdims=True) + EPS) * g


def _gelu(x):
    return 0.5 * x * (1.0 + lax.erf(x * (2.0 ** -0.5)))


def _causal(n):
    return lax.broadcasted_iota(jnp.int32, (n, n), 0) >= lax.broadcasted_iota(jnp.int32, (n, n), 1)


def _row_in_specs(rows, bparams, gparams, tm):
    specs = [pl.BlockSpec((1, tm, w), lambda b, i, cb=cb: (b, i, cb)) for (_, w, cb, _) in rows]
    specs += [pl.BlockSpec((1, 1, p.shape[-1]), lambda b, i: (b, 0, 0)) for p in bparams]
    specs += [pl.BlockSpec(p.shape, lambda b, i, n=p.ndim: (0,) * n) for p in gparams]
    return specs


def _row_vals(refs, n_rows, n_b, n_g):
    vals = [r[0].astype(F32) for r in refs[:n_rows]]
    vals += [r[0].astype(F32) for r in refs[n_rows:n_rows + n_b]]
    vals += [r[...].astype(F32) for r in refs[n_rows + n_b:n_rows + n_b + n_g]]
    return vals


def _row_fwd(name, tile, rows, bparams, gparams, outs, n_sum, tm):
    bl, s = rows[0][0].shape[:2]
    n_in = len(rows) + len(bparams) + len(gparams)

    def body(*refs):
        first = (pl.program_id(0) == 0) & (pl.program_id(1) == 0)
        res = tile(*_row_vals(refs, len(rows), len(bparams), len(gparams)))
        o_refs = refs[n_in:]
        for k in range(len(outs)):
            o_refs[k][0] = res[k].astype(o_refs[k].dtype)
        for k in range(n_sum):
            o_ref, val = o_refs[len(outs) + k], res[len(outs) + k]

            @pl.when(first)
            def _(o_ref=o_ref, val=val):
                o_ref[...] = val

            @pl.when(jnp.logical_not(first))
            def _(o_ref=o_ref, val=val):
                o_ref[...] += val

    out_shape = [_sds((bl, s, w), dt) for (w, dt) in outs] + [_sds((1, 1), F32)] * n_sum
    out_specs = [pl.BlockSpec((1, tm, w), lambda b, i: (b, i, 0)) for (w, _) in outs] + [pl.BlockSpec((1, 1), lambda b, i: (0, 0))] * n_sum
    return _pcall(
        body, name, out_shape, grid=(bl, s // tm), in_specs=_row_in_specs(rows, bparams, gparams, tm), out_specs=out_specs,
        sem=("arbitrary", "arbitrary"),
    )(*[r[0] for r in rows], *bparams, *gparams)


def _row_bwd(name, tile, rows, bparams, gparams, outs, n_sum, tm, cts, emit_sums=False):
    bl, s = rows[0][0].shape[:2]
    n_r, n_b, n_g = len(rows), len(bparams), len(gparams)
    n_extra = n_sum if emit_sums else 0
    n_in = n_r + n_b + n_g
    n_ct = len(outs) + n_sum
    grad_rows = [k for k in range(n_r) if rows[k][3]]

    def body(*refs):
        b, i = pl.program_id(0), pl.program_id(1)
        vals = _row_vals(refs, n_r, n_b, n_g)
        ct_refs = refs[n_in:n_in + n_ct]
        ct = [r[0].astype(F32) for r in ct_refs[:len(outs)]] + [r[...] for r in ct_refs[len(outs):]]
        primal, vjp = jax.vjp(tile, *vals)
        grads = tuple(vjp(tuple(ct))) + (tuple(primal[len(outs):]) if emit_sums else ())
        o_refs = refs[n_in + n_ct:]
        for j, k in enumerate(grad_rows):
            o_refs[j][0] = grads[k].astype(o_refs[j].dtype)
        for k in range(n_b):
            o_ref, val = o_refs[len(grad_rows) + k], grads[n_r + k]

            @pl.when(i == 0)
            def _(o_ref=o_ref, val=val):
                o_ref[0] = val

            @pl.when(i > 0)
            def _(o_ref=o_ref, val=val):
                o_ref[0] += val

        first = (b == 0) & (i == 0)
        for k in range(n_g + n_extra):
            o_ref, val = o_refs[len(grad_rows) + n_b + k], grads[n_r + n_b + k]

            @pl.when(first)
            def _(o_ref=o_ref, val=val):
                o_ref[...] = val

            @pl.when(jnp.logical_not(first))
            def _(o_ref=o_ref, val=val):
                o_ref[...] += val

    one = pl.BlockSpec((1, 1), lambda b, i: (0, 0))
    in_specs = _row_in_specs(rows, bparams, gparams, tm)
    in_specs += [pl.BlockSpec((1, tm, w), lambda b, i: (b, i, 0)) for (w, _) in outs] + [one] * n_sum
    out_shape = [_sds((bl, s, rows[k][1]), rows[k][3]) for k in grad_rows]
    out_shape += [_sds(p.shape, F32) for p in bparams] + [_sds(p.shape, F32) for p in gparams] + [_sds((1, 1), F32)] * n_extra
    out_specs = [pl.BlockSpec((1, tm, rows[k][1]), lambda b, i: (b, i, 0)) for k in grad_rows]
    out_specs += [pl.BlockSpec((1, 1, p.shape[-1]), lambda b, i: (b, 0, 0)) for p in bparams]
    out_specs += [pl.BlockSpec(p.shape, lambda b, i, n=p.ndim: (0,) * n) for p in gparams] + [one] * n_extra
    res = _pcall(
        body, name, out_shape, grid=(bl, s // tm), in_specs=in_specs, out_specs=out_specs, sem=("arbitrary", "arbitrary"),
    )(*[r[0] for r in rows], *bparams, *gparams, *cts)
    n0, n1, n2 = len(grad_rows), len(grad_rows) + n_b, len(grad_rows) + n_b + n_g
    return (res[:n0], res[n0:n1], res[n1:n2]) + ((res[n2:],) if emit_sums else ())


def _nm_tile(x, sc, sh, g):
    return (_rms(x, g) * (1.0 + sc) + sh,)


def _nm_res_tile(x, sc, sh, g):
    return x, _rms(x, g) * (1.0 + sc) + sh


def _rnm_tile(x, o, gate, sc, sh, g):
    xn = x + gate * o
    return xn, _rms(xn, g) * (1.0 + sc) + sh


def _final_tile(x, o, tgt, gate, g):
    e = _rms(x + gate * o, g) - tgt
    return (0.5 * jnp.sum(jnp.mean(e * e, axis=-1, keepdims=True), axis=0, keepdims=True),)


def _gm_tile(u_in, v_in, vg, ws, bs_t, og):
    d = u_in.shape[1]
    u, vn = _gelu(u_in), _rms(_gelu(v_in), vg)
    causal = _causal(CHUNK)
    lane = lax.broadcasted_iota(jnp.int32, (1, LANES), 1)
    parts = []
    for h in range(d // GM_HEAD):
        bias = jnp.sum(bs_t * (lane == h).astype(F32), axis=1, keepdims=True)
        parts.append(_dot_nn(jnp.where(causal, ws[h], 0.0), vn[:, h * GM_HEAD:(h + 1) * GM_HEAD]) + bias)
    return (_rms(u * jnp.concatenate(parts, axis=1), og),)


def _conv_window(ref, r0, rows, c0, tc, seq, before, after):
    parts = []
    if before:
        p0 = pl.multiple_of(jnp.maximum(r0 - SUBLANES, 0), SUBLANES)
        parts.append(jnp.where(r0 > 0, ref[0, pl.ds(p0, SUBLANES), pl.ds(c0, tc)].astype(F32), 0.0))
    parts.append(ref[0, pl.ds(r0, rows), pl.ds(c0, tc)].astype(F32))
    if after:
        n0 = pl.multiple_of(jnp.minimum(r0 + rows, seq - SUBLANES), SUBLANES)
        parts.append(jnp.where(r0 + rows < seq, ref[0, pl.ds(n0, SUBLANES), pl.ds(c0, tc)].astype(F32), 0.0))
    return jnp.concatenate(parts, axis=0) if len(parts) > 1 else parts[0]


def _conv_taps_pre(xe, w, b, taps, rows):
    pre = xe[SUBLANES:SUBLANES + rows] * w[taps - 1] + b
    for j in range(1, taps):
        pre = pre + pltpu.roll(xe, j, 0)[SUBLANES:SUBLANES + rows] * w[taps - 1 - j]
    return pre


def _conv_fwd(name, src, col0, chans, w8, b, taps, gated, out_dtype):
    bl, s, _ = src.shape
    tc = CONV_LANES
    xw = 2 * tc if gated else tc

    def body(x_ref, w_ref, b_ref, o_ref):
        w = [w_ref[k:k + 1, :] for k in range(taps)]
        bias = b_ref[...]

        def step(c, carry):
            r0 = pl.multiple_of(c * CONV_ROWS, CONV_ROWS)
            xe = _conv_window(x_ref, r0, CONV_ROWS, 0, tc, s, True, False)
            pre = _conv_taps_pre(xe, w, bias, taps, CONV_ROWS)
            y = pre * _sigmoid(pre)
            if gated:
                y = y * x_ref[0, pl.ds(r0, CONV_ROWS), pl.ds(tc, tc)]
            o_ref[0, pl.ds(r0, CONV_ROWS), :] = y.astype(o_ref.dtype)
            return carry

        lax.fori_loop(0, s // CONV_ROWS, step, 0, unroll=4)

    first = 0 if gated else col0 // tc
    return _pcall(
        body, name, _sds((bl, s, chans), out_dtype), grid=(chans // tc, bl),
        in_specs=[pl.BlockSpec((1, s, xw), lambda j, bb: (bb, 0, first + j)), pl.BlockSpec((SUBLANES, tc), lambda j, bb: (0, j)), pl.BlockSpec((1, tc), lambda j, bb: (0, j))],
        out_specs=pl.BlockSpec((1, s, tc), lambda j, bb: (bb, 0, j)), sem=("parallel", "arbitrary"),
    )(src, w8, b)


def _conv_bwd(name, src, col0, chans, w8, b, taps, gated, dy, dx_dtype):
    bl, s, _ = src.shape
    tc = CONV_LANES
    xw = 2 * tc if gated else tc
    ext = CONV_ROWS + SUBLANES

    def body(x_ref, w_ref, b_ref, dy_ref, dx_ref, dw_ref, db_ref):
        bb = pl.program_id(1)
        w = [w_ref[k:k + 1, :] for k in range(taps)]
        bias = b_ref[...]

        def fold(v):
            acc = v[0:SUBLANES]
            for i in range(1, CONV_ROWS // SUBLANES):
                acc = acc + v[i * SUBLANES:(i + 1) * SUBLANES]
            return acc

        def step(c, carry):
            r0 = pl.multiple_of(c * CONV_ROWS, CONV_ROWS)
            xe = _conv_window(x_ref, r0, CONV_ROWS, 0, tc, s, True, True)
            pre = _conv_taps_pre(xe, w, bias, taps, ext)
            sig = _sigmoid(pre)
            d = _conv_window(dy_ref, r0, CONV_ROWS, 0, tc, s, False, True)
            dsil = d * _conv_window(x_ref, r0, CONV_ROWS, tc, tc, s, False, True) if gated else d
            dpre = dsil * sig * (1.0 + pre * (1.0 - sig))
            dx = dpre[:CONV_ROWS] * w[taps - 1]
            for j in range(1, taps):
                dx = dx + pltpu.roll(dpre, ext - j, 0)[:CONV_ROWS] * w[taps - 1 - j]
            dx_ref[0, pl.ds(r0, CONV_ROWS), pl.ds(0, tc)] = dx.astype(dx_ref.dtype)
            if gated:
                dx_ref[0, pl.ds(r0, CONV_ROWS), pl.ds(tc, tc)] = (d[:CONV_ROWS] * (pre * sig)[:CONV_ROWS]).astype(dx_ref.dtype)
            here = dpre[:CONV_ROWS]
            sums = [fold(here * (pltpu.roll(xe, taps - 1 - k, 0) if k < taps - 1 else xe)[SUBLANES:ext]) + carry[k] for k in range(taps)]
            return tuple(sums) + (fold(here) + carry[taps],)

        zero = jnp.zeros((SUBLANES, tc), F32)
        sums = lax.fori_loop(0, s // CONV_ROWS, step, (zero,) * (taps + 1), unroll=2)
        rows = [jnp.sum(t, axis=0, keepdims=True) for t in sums]
        dw = jnp.concatenate(rows[:taps] + [jnp.zeros_like(rows[0])] * (SUBLANES - taps), axis=0)

        @pl.when(bb == 0)
        def _():
            dw_ref[...] = dw
            db_ref[...] = rows[taps]

        @pl.when(bb > 0)
        def _():
            dw_ref[...] += dw
            db_ref[...] += rows[taps]

    first = 0 if gated else col0 // tc
    out_shape = [_sds((bl, s, chans * (2 if gated else 1)), dx_dtype), _sds((SUBLANES, chans), F32), _sds((1, chans), F32)]
    return _pcall(
        body, name, out_shape, grid=(chans // tc, bl),
        in_specs=[
            pl.BlockSpec((1, s, xw), lambda j, bb: (bb, 0, first + j)), pl.BlockSpec((SUBLANES, tc), lambda j, bb: (0, j)),
            pl.BlockSpec((1, tc), lambda j, bb: (0, j)), pl.BlockSpec((1, s, tc), lambda j, bb: (bb, 0, j)),
        ],
        out_specs=[pl.BlockSpec((1, s, xw), lambda j, bb: (bb, 0, j)), pl.BlockSpec((SUBLANES, tc), lambda j, bb: (0, j)), pl.BlockSpec((1, tc), lambda j, bb: (0, j))],
        sem=("parallel", "arbitrary"),
    )(src, w8, b, dy)


def _gate_value_blocks(name, w, f, inverse=False):
    nb = f // CONV_LANES
    src = (lambda j: (0, 2 * (j % nb) + j // nb)) if inverse else (lambda j: (0, (j % 2) * nb + j // 2))

    def body(x_ref, o_ref):
        o_ref[...] = x_ref[...]

    return _pcall(
        body, name, _sds(w.shape, w.dtype), grid=(2 * nb,), in_specs=[pl.BlockSpec((w.shape[0], CONV_LANES), src)],
        out_specs=pl.BlockSpec((w.shape[0], CONV_LANES), lambda j: (0, j)), sem=("parallel",),
    )(w)


def _ssd_step(states, xs, bm, cm, dtr, z, dtb, alog, dsk, ng):
    gw = xs.shape[1] // SSD_GROUPS
    hpg, npair = gw // HEAD_DIM, gw // LANES
    dt = jax.nn.softplus(dtr + dtb)
    da = dt * (-jnp.exp(alog))
    causal = _causal(CHUNK)
    acs = _sel_left(causal.astype(F32), da)
    acs_t = acs.T
    lane = lax.broadcasted_iota(jnp.int32, (1, LANES), 1)
    sub = lax.broadcasted_iota(jnp.int32, (LANES, 1), 0)
    last = lax.broadcasted_iota(jnp.int32, (CHUNK, gw), 0) == CHUNK - 1
    outs, new_states = [], []
    for g in range(SSD_GROUPS):
        xg, zg = xs[:, g * gw:(g + 1) * gw], z[:, g * gw:(g + 1) * gw]
        bg, cg = bm[:, g * STATE:(g + 1) * STATE], cm[:, g * STATE:(g + 1) * STATE]
        head_of_col = g * hpg + lax.broadcasted_iota(jnp.int32, (LANES, gw), 1) // HEAD_DIM
        expand = (lax.broadcasted_iota(jnp.int32, (LANES, gw), 0) == head_of_col).astype(F32)
        dt_e, acs_e = _sel_right(dt, expand), _sel_right(acs, expand)
        alast_e = jnp.sum(jnp.where(last, acs_e, 0.0), axis=0, keepdims=True)
        xc = xg * dt_e
        xc_st = xc * jnp.exp(alast_e - acs_e)
        decay_out, chunk_decay = jnp.exp(acs_e), jnp.exp(alast_e)
        cb = _dot_nt(cg, bg)
        ys = []
        for p in range(npair):
            sl = slice(p * LANES, (p + 1) * LANES)
            state = states[g * npair + p]
            y = _dot_nn(cg, state) * decay_out[:, sl]
            for q in range(2):
                head = g * hpg + 2 * p + q
                col = jnp.sum(acs * (lane == head).astype(F32), axis=1, keepdims=True)
                row = jnp.sum(acs_t * (sub == head).astype(F32), axis=0, keepdims=True)
                decay = jnp.where(causal, jnp.exp(jnp.where(causal, col - row, 0.0)), 0.0)
                y = y + _dot_nn(cb * decay, xc[:, sl] * ((lane // HEAD_DIM) == q).astype(F32))
            ys.append(y)
            new_states.append(state * chunk_decay[:, sl] + _dot_tn(bg, xc_st[:, sl]))
        y = jnp.concatenate(ys, axis=1) + dsk[:, g * gw:(g + 1) * gw] * xg
        outs.append(_rms(y * (zg * _sigmoid(zg)), ng[:, g * gw:(g + 1) * gw]))
    return tuple(new_states), jnp.concatenate(outs, axis=1)


def _ssd_scan_specs(d, dt_col, nc, rev):
    ci = (lambda i: nc - 1 - i) if rev else (lambda i: i)
    bc = SSD_GROUPS * STATE
    return [
        pl.BlockSpec((1, CHUNK, d), lambda b, i: (b, ci(i), 0)),
        pl.BlockSpec((1, CHUNK, bc), lambda b, i: (b, ci(i), d // bc)),
        pl.BlockSpec((1, CHUNK, bc), lambda b, i: (b, ci(i), d // bc + 1)),
        pl.BlockSpec((1, CHUNK, LANES), lambda b, i: (b, ci(i), dt_col // LANES)),
        pl.BlockSpec((1, CHUNK, d), lambda b, i: (b, ci(i), 0)),
        pl.BlockSpec((1, LANES), lambda b, i: (0, 0)), pl.BlockSpec((1, LANES), lambda b, i: (0, 0)),
        pl.BlockSpec((1, d), lambda b, i: (0, 0)), pl.BlockSpec((1, d), lambda b, i: (0, 0)),
    ]


def _ssd_scan_fwd(name, xbc, proj, dt_col, dtb, alog, dsk, ng):
    bl, s, _ = xbc.shape
    d = dsk.shape[1]
    nc, nst = s // CHUNK, d // LANES

    def body(xs_ref, bm_ref, cm_ref, dt_ref, z_ref, dtb_ref, alog_ref, dsk_ref, ng_ref, y_ref, hp_ref, st_ref):
        @pl.when(pl.program_id(1) == 0)
        def _():
            st_ref[...] = jnp.zeros_like(st_ref)

        states = tuple(st_ref[p] for p in range(nst))
        hp_ref[0, 0] = st_ref[...]
        new_states, yn = _ssd_step(states, xs_ref[0], bm_ref[0], cm_ref[0], dt_ref[0], z_ref[0], dtb_ref[...], alog_ref[...], dsk_ref[...], ng_ref[...])
        for p in range(nst):
            st_ref[p] = new_states[p]
        y_ref[0] = yn.astype(y_ref.dtype)

    return _pcall(
        body, name, [_sds((bl, s, d), BF16), _sds((bl, nc, nst, STATE, LANES), F32)], grid=(bl, nc), in_specs=_ssd_scan_specs(d, dt_col, nc, False),
        out_specs=[pl.BlockSpec((1, CHUNK, d), lambda b, i: (b, i, 0)), pl.BlockSpec((1, 1, nst, STATE, LANES), lambda b, i: (b, i, 0, 0, 0))],
        scratch=[pltpu.VMEM((nst, STATE, LANES), F32)], sem=("arbitrary", "arbitrary"),
    )(xbc, xbc, xbc, proj, proj, dtb, alog, dsk, ng)


def _ssd_scan_bwd(name, xbc, proj, dt_col, dtb, alog, dsk, ng, hprev, dy):
    bl, s, _ = xbc.shape
    d = dsk.shape[1]
    nc, nst, bc = s // CHUNK, d // LANES, SSD_GROUPS * STATE

    def body(xs_ref, bm_ref, cm_ref, dt_ref, z_ref, dtb_ref, alog_ref, dsk_ref, ng_ref, hp_ref, dy_ref,
             dxs_ref, dbm_ref, dcm_ref, ddt_ref, dz_ref, ddtb_ref, dalog_ref, ddsk_ref, dng_ref, dst_ref):
        b, i = pl.program_id(0), pl.program_id(1)

        @pl.when(i == 0)
        def _():
            dst_ref[...] = jnp.zeros_like(dst_ref)

        states = tuple(hp_ref[0, 0, p] for p in range(nst))
        _, vjp = jax.vjp(_ssd_step, states, xs_ref[0], bm_ref[0], cm_ref[0], dt_ref[0], z_ref[0], dtb_ref[...], alog_ref[...], dsk_ref[...], ng_ref[...])
        d_states, dxs, dbm, dcm, ddt, dz, ddtb, dalog, ddsk, dng = vjp((tuple(dst_ref[p] for p in range(nst)), dy_ref[0].astype(F32)))
        for p in range(nst):
            dst_ref[p] = d_states[p]
        dxs_ref[0], dbm_ref[0], dcm_ref[0], ddt_ref[0] = dxs, dbm, dcm, ddt
        dz_ref[0] = dz.astype(dz_ref.dtype)
        first = (b == 0) & (i == 0)
        for o_ref, val in ((ddtb_ref, ddtb), (dalog_ref, dalog), (ddsk_ref, ddsk), (dng_ref, dng)):
            @pl.when(first)
            def _(o_ref=o_ref, val=val):
                o_ref[...] = val

            @pl.when(jnp.logical_not(first))
            def _(o_ref=o_ref, val=val):
                o_ref[...] += val

    rc = lambda i: nc - 1 - i
    in_specs = _ssd_scan_specs(d, dt_col, nc, True) + [
        pl.BlockSpec((1, 1, nst, STATE, LANES), lambda b, i: (b, rc(i), 0, 0, 0)), pl.BlockSpec((1, CHUNK, d), lambda b, i: (b, rc(i), 0)),
    ]
    out_shape = [
        _sds((bl, s, d), F32), _sds((bl, s, bc), F32), _sds((bl, s, bc), F32), _sds((bl, s, LANES), F32), _sds((bl, s, d), BF16),
        _sds((1, LANES), F32), _sds((1, LANES), F32), _sds((1, d), F32), _sds((1, d), F32),
    ]
    row = lambda w: pl.BlockSpec((1, CHUNK, w), lambda b, i: (b, rc(i), 0))
    whole = lambda w: pl.BlockSpec((1, w), lambda b, i: (0, 0))
    out_specs = [row(d), row(bc), row(bc), row(LANES), row(d), whole(LANES), whole(LANES), whole(d), whole(d)]
    return _pcall(
        body, name, out_shape, grid=(bl, nc), in_specs=in_specs, out_specs=out_specs,
        scratch=[pltpu.VMEM((nst, STATE, LANES), F32)], sem=("arbitrary", "arbitrary"),
    )(xbc, xbc, xbc, proj, proj, dtb, alog, dsk, ng, hprev, dy)


def _mixer_fwd(name, proj, d, cd, conv_w8, conv_b, dtb, alog, dsk, ng, vg, ws, bs_t, og):
    xbc = _conv_fwd(name + "_conv_fwd", proj, 3 * d, cd, conv_w8, conv_b, SSD_CONV, False, F32)
    y, hprev = _ssd_scan_fwd(name + "_ssd_fwd", xbc, proj, 3 * d + cd, dtb, alog, dsk, ng)
    rows = [(proj, d, 1, BF16), (proj, d, 2, BF16)]
    (g_out,) = _row_fwd(name + "_gm_fwd", _gm_tile, rows, [], [vg, ws, bs_t, og], [(d, BF16)], 0, CHUNK)
    return jnp.concatenate([y, g_out], axis=-1), (xbc, hprev)


def _mixer_bwd(name, proj, d, cd, conv_w8, conv_b, dtb, alog, dsk, ng, vg, ws, bs_t, og, xbc, hprev, dycat):
    bl, s, n_proj = proj.shape
    dt_col = 3 * d + cd
    dy, dg_out = dycat[..., :d], dycat[..., d:]
    dxs, dbm, dcm, ddt, dz, ddtb, dalog, ddsk, dng = _ssd_scan_bwd(name + "_ssd_bwd", xbc, proj, dt_col, dtb, alog, dsk, ng, hprev, dy)
    dxbc_act = jnp.concatenate([dxs, dbm, dcm], axis=-1)
    dxbc, dw8, dcb = _conv_bwd(name + "_conv_bwd", proj, 3 * d, cd, conv_w8, conv_b, SSD_CONV, False, dxbc_act, BF16)
    rows = [(proj, d, 1, BF16), (proj, d, 2, BF16)]
    (du, dv), _, (dvg, dws, dbs_t, dog) = _row_bwd(name + "_gm_bwd", _gm_tile, rows, [], [vg, ws, bs_t, og], [(d, BF16)], 0, CHUNK, [dg_out])
    pad = jnp.zeros((bl, s, n_proj - dt_col - LANES), BF16)
    dproj = jnp.concatenate([dz, du, dv, dxbc, ddt.astype(BF16), pad], axis=-1)
    return dproj, (dw8, dcb, ddtb, dalog, ddsk, dng, dvg, dws, dbs_t, dog)


def _position():
    return lax.axis_index("x"), lax.axis_index("y"), lax.axis_index("c")


def _at(ref, idx):
    return ref.at[idx] if len(idx) else ref


def _exchange(name, inputs, out_shapes, plan, inplace=False, after=None):
    if inplace:
        out_shapes = [(a.shape, a.dtype) for a in inputs]
    n_in, n_out = len(inputs), len(out_shapes)
    n_after = 0 if after is None else 1
    n_copy = len(plan(0, 0, 0))

    def body(*refs):
        in_refs, out_refs, token = refs[:n_in], refs[n_in + n_after:n_in + n_after + n_out], refs[n_in + n_after + n_out]
        send_sems, recv_sems = refs[n_in + n_after + n_out + 1:]
        token[...] = jnp.zeros_like(token)
        x, y, c = _position()
        copies = plan(x, y, c)

        def copy(k, src, dst, peer):
            return pltpu.make_async_remote_copy(src_ref=src, dst_ref=dst, send_sem=send_sems.at[k], recv_sem=recv_sems.at[k], device_id=peer, device_id_type=MESH)

        src_refs = out_refs if inplace else in_refs
        sends = [copy(k, _at(src_refs[sa], si), _at(out_refs[da], di), peer) for k, (sa, si, da, di, peer, _) in enumerate(copies)]
        for cp in sends:
            cp.start()
        for k, (sa, si, da, _, peer, li) in enumerate(copies):
            copy(k, _at(src_refs[sa], si), _at(out_refs[da], li), peer).wait_recv()
        for cp in sends:
            cp.wait_send()

    any_spec = pl.BlockSpec(memory_space=pl.ANY)
    outs = pl.pallas_call(
        body, name=name, out_shape=[_sds(s, dt) for s, dt in out_shapes] + [_sds((SUBLANES, LANES), F32)], in_specs=[any_spec] * (n_in + n_after),
        out_specs=[any_spec] * n_out + [pl.BlockSpec(memory_space=pltpu.VMEM)],
        scratch_shapes=[pltpu.SemaphoreType.DMA((n_copy,)), pltpu.SemaphoreType.DMA((n_copy,))],
        input_output_aliases={i: i for i in range(n_in)} if inplace else {},
    )(*inputs, *([] if after is None else [after]))
    return list(outs[:n_out]), outs[n_out]


def _exchange_start(name, inputs, out_shapes, plan):
    n_in, n_out = len(inputs), len(out_shapes)
    n_copy = len(plan(0, 0, 0))

    def body(*refs):
        in_refs, land_refs = refs[:n_in], refs[n_in:n_in + n_out]
        send_sems, recv_sems = refs[n_in + n_out:n_in + n_out + 2]
        token = refs[-1]
        x, y, c = _position()
        for k, (sa, si, da, di, peer, _) in enumerate(plan(x, y, c)):
            pltpu.make_async_remote_copy(
                src_ref=_at(in_refs[sa], si), dst_ref=_at(land_refs[da], di), send_sem=send_sems.at[k], recv_sem=recv_sems.at[k],
                device_id=peer, device_id_type=MESH).start()
        token[...] = jnp.zeros_like(token)

    hbm, sem = pl.BlockSpec(memory_space=pltpu.HBM), pl.BlockSpec(memory_space=pltpu.SEMAPHORE)
    lands = [lax.empty(s, dt) for s, dt in out_shapes]
    args = [pltpu.with_memory_space_constraint(a, pltpu.HBM) for a in list(inputs) + lands]
    outs = pl.pallas_call(
        body, name=name,
        out_shape=(pltpu.SemaphoreType.DMA((n_copy,)), pltpu.SemaphoreType.DMA((n_copy,)), *[pltpu.HBM(a.shape, a.dtype) for a in args], _sds((SUBLANES, LANES), F32)),
        in_specs=[hbm] * (n_in + n_out), out_specs=(sem, sem, *[hbm] * (n_in + n_out), pl.BlockSpec(memory_space=pltpu.VMEM)),
        input_output_aliases={i: 2 + i for i in range(n_in + n_out)},
        compiler_params=pltpu.CompilerParams(has_side_effects=pltpu.SideEffectType.DATAFLOW_SIDE_EFFECTING),
    )(*args)
    return dict(name=name, plan=plan, sems=outs[:2], ins=list(outs[2:2 + n_in]), lands=list(outs[2 + n_in:2 + n_in + n_out]), token=outs[-1])


def _exchange_wait(started, after):
    plan, n_in, n_out = started["plan"], len(started["ins"]), len(started["lands"])

    def body(*refs):
        in_refs, land_refs = refs[:n_in], refs[n_in:n_in + n_out]
        send_sems, recv_sems = refs[n_in + n_out:n_in + n_out + 2]
        token = refs[-1]
        x, y, c = _position()
        for k, (sa, si, da, _, peer, li) in enumerate(plan(x, y, c)):
            cp = pltpu.make_async_remote_copy(
                src_ref=_at(in_refs[sa], si), dst_ref=_at(land_refs[da], li), send_sem=send_sems.at[k], recv_sem=recv_sems.at[k],
                device_id=peer, device_id_type=MESH)
            cp.wait_send()
            cp.wait_recv()
        token[...] = jnp.zeros_like(token)

    hbm, sem = pl.BlockSpec(memory_space=pltpu.HBM), pl.BlockSpec(memory_space=pltpu.SEMAPHORE)
    bufs = started["ins"] + started["lands"]
    outs = pl.pallas_call(
        body, name=started["name"] + "_wait", out_shape=(*[pltpu.HBM(a.shape, a.dtype) for a in bufs], _sds((SUBLANES, LANES), F32)),
        in_specs=[hbm] * len(bufs) + [sem, sem, pl.BlockSpec(memory_space=pl.ANY)], out_specs=(*[hbm] * len(bufs), pl.BlockSpec(memory_space=pltpu.VMEM)),
        input_output_aliases={i: i for i in range(len(bufs))},
        compiler_params=pltpu.CompilerParams(has_side_effects=pltpu.SideEffectType.DATAFLOW_SIDE_EFFECTING),
    )(*bufs, *started["sems"], after)
    return list(outs[:n_in]), list(outs[n_in:n_in + n_out]), outs[-1]


def _after(value, token):
    return value + token[0, 0].astype(value.dtype)


def _chip_peers(x, y, c):
    return [(1 - x, y, c), (x, 1 - y, c), (1 - x, 1 - y, c)]


def _chip_of(p):
    return 2 * p[0] + p[1]


def _set_slot(slots, me, blk):
    return lax.dynamic_update_slice(slots, blk[None], (me,) + (0,) * blk.ndim)


def _by_core(c, mine, other, axis):
    return jnp.where(c == 0, jnp.stack([mine, other], axis), jnp.stack([other, mine], axis))


def _plan_gather_chips(n):
    def plan(x, y, c):
        return [(a, (), a, (2 * x + y,), p, (_chip_of(p),)) for a in range(n) for p in _chip_peers(x, y, c)]

    return plan


def _gather_chips(name, blocks):
    recv, token = _exchange(name, blocks, [((4,) + b.shape, b.dtype) for b in blocks], _plan_gather_chips(len(blocks)))
    x, y, _ = _position()
    return [_set_slot(r, 2 * x + y, b) for r, b in zip(recv, blocks)], token


def _gather_pass_cores(name, blocks, from_chips):
    n = len(blocks)

    def plan_cores(x, y, c):
        me, sib = 2 * x + y, (x, y, 1 - c)
        own = [(a, (), a, (me,), sib, (me,)) for a in range(n)]
        passed = [(n + a, (_chip_of(p),), a, (_chip_of(p),), sib, (_chip_of(p),)) for a in range(n) for p in _chip_peers(x, y, c)]
        return own + passed

    from_core, token = _exchange(name + "_cores", list(blocks) + list(from_chips), [((4,) + b.shape, b.dtype) for b in blocks], plan_cores)
    x, y, c = _position()
    return [_by_core(c, _set_slot(r1, 2 * x + y, b), r2, 1) for b, r1, r2 in zip(blocks, from_chips, from_core)], token


def _gather_two_level(name, blocks):
    from_chips, _ = _exchange(name + "_chips", blocks, [((4,) + b.shape, b.dtype) for b in blocks], _plan_gather_chips(len(blocks)))
    return _gather_pass_cores(name, blocks, from_chips)


def _pair_add(name, g42, r4):
    _, _, rh, cols = g42.shape
    tr = _divisors(rh, 512, SUBLANES * 2)[-1]

    def body(c_ref, a_ref, b_ref, o_ref):
        o_ref[0] = (a_ref[0, 0].astype(F32) + b_ref[0].astype(F32)).astype(o_ref.dtype)

    cidx = lax.axis_index("c").astype(jnp.int32).reshape(1)
    return _pcall(
        body, name, _sds(r4.shape, BF16), grid=(4, rh // tr),
        in_specs=[pl.BlockSpec((1, 1, tr, cols), lambda s, i, c_ref: (s, c_ref[0], i, 0)), pl.BlockSpec((1, tr, cols), lambda s, i, c_ref: (s, i, 0))],
        out_specs=pl.BlockSpec((1, tr, cols), lambda s, i, c_ref: (s, i, 0)), sem=("parallel", "parallel"), prefetch=1,
    )(cidx, g42, r4)


def _slot_sum(name, parts):
    n, r, cols = parts.shape
    cap = max(2 * SUBLANES, (4 * 1024 * 1024) // (n * cols * parts.dtype.itemsize))
    tr = _divisors(r, cap, 2 * SUBLANES)[-1]

    def body(p_ref, o_ref):
        acc = p_ref[0].astype(F32)
        for k in range(1, n):
            acc = acc + p_ref[k].astype(F32)
        o_ref[...] = acc

    return _pcall(
        body, name, _sds((r, cols), F32), grid=(r // tr,), in_specs=[pl.BlockSpec((n, tr, cols), lambda i: (0, i, 0))],
        out_specs=pl.BlockSpec((tr, cols), lambda i: (i, 0)), sem=("parallel",),
    )(parts)


def _slot_sums(name, parts):
    k = len(parts)

    def body(*refs):
        for p_ref, o_ref in zip(refs[:k], refs[k:]):
            acc = p_ref[0].astype(F32)
            for j in range(1, p_ref.shape[0]):
                acc = acc + p_ref[j].astype(F32)
            o_ref[...] = acc

    return list(_pcall(body, name, [_sds(p.shape[1:], F32) for p in parts])(*parts))


def _gather_weights_start(name, shards, token):
    c = lax.axis_index("c")
    halves = [lax.dynamic_slice_in_dim(w, c * (w.shape[0] // 2), w.shape[0] // 2, 0).astype(BF16) for w in shards]
    if token is not None:
        halves[0] = _after(halves[0], token)
    n = len(halves)

    def plan(x, y, c):
        return [(a, (), a, (2 * x + y, c), p, (_chip_of(p), c)) for a in range(n) for p in _chip_peers(x, y, c)]

    return _exchange_start(name + "_ag_chips", halves, [((4, 2) + h.shape, h.dtype) for h in halves], plan)


def _gather_weights_finish(name, starteds, after):
    halves, slots = [], []
    for started in starteds:
        h, s, token = _exchange_wait(started, after)
        halves, slots = halves + h, slots + s
    n = len(halves)
    x, y, c = _position()
    slots = [lax.dynamic_update_slice(s, h[None, None], (2 * x + y, c, 0, 0)) for s, h in zip(slots, halves)]

    def plan_cores(x, y, c):
        return [(a, (s, c), a, (s, c), (x, y, 1 - c), (s, 1 - c)) for a in range(n) for s in range(4)]

    full, _ = _exchange(name + "_ag_cores", slots, None, plan_cores, inplace=True)
    return [f.reshape((4, 2 * h.shape[0], h.shape[1])) for f, h in zip(full, halves)], token


def _reduce_weights_start(name, grads, token):
    n = len(grads)
    g42 = [g.reshape(4, 2, g.shape[1] // 2, g.shape[2]) for g in grads]

    def plan_swap(x, y, c):
        return [(a, (s, 1 - c), a, (s,), (x, y, 1 - c), (s,)) for a in range(n) for s in range(4)]

    def plan_chips(x, y, c):
        return [(a, (_chip_of(p),), a, (2 * x + y,), p, (_chip_of(p),)) for a in range(n) for p in _chip_peers(x, y, c)]

    other, _ = _exchange(name + "_rs_cores", g42, [((4,) + g.shape[2:], g.dtype) for g in g42], plan_swap, after=token)
    pair = [_pair_add(f"{name}_rs_pair{a}", g, o) for a, (g, o) in enumerate(zip(g42, other))]
    return _exchange_start(name + "_rs_chips", pair, [(p.shape, p.dtype) for p in pair], plan_chips)


def _reduce_weights_finish(name, starteds, after):
    pair, recv = [], []
    for started in starteds:
        p, r, token = _exchange_wait(started, after)
        pair, recv = pair + p, recv + r
    n = len(pair)
    x, y, c = _position()
    me = 2 * x + y

    def plan_share(x, y, c):
        return [(a, (c,), a, (c,), (x, y, 1 - c), (1 - c,)) for a in range(n)]

    parts = [lax.dynamic_update_slice(r, lax.dynamic_slice_in_dim(p, me, 1, 0), (me, 0, 0)) for r, p in zip(recv, pair)]
    mine = [_slot_sum(f"{name}_rs_sum{a}", p) for a, p in enumerate(parts)]
    both = [lax.dynamic_update_slice(lax.empty((2,) + m.shape, m.dtype), m[None], (c, 0, 0)) for m in mine]
    both, _ = _exchange(name + "_rs_share", both, None, plan_share, inplace=True)
    return [b.reshape(2 * b.shape[1], b.shape[2]) for b in both], token


def _allreduce_small(name, grads):
    both, token = _gather_two_level(name + "_ag", [g.astype(BF16) if g.size >= SMALL_BF16_SIZE else g for g in grads])
    return _slot_sums(name + "_sum", [g.reshape((8,) + g.shape[2:]) for g in both]), token


def _ada_fwd_call(name, c_all, w, b_shard):
    nl, d, ns = w.shape
    nb = c_all.shape[0]

    def body(c_ref, w_ref, b_ref, o_ref):
        cv = c_ref[...]
        o_ref[0] = _dg(cv * _sigmoid(cv), w_ref[0], ((1,), (0,))) + b_ref[0]

    return _pcall(
        body, name, _sds((nl, nb, ns), F32), grid=(nl,),
        in_specs=[pl.BlockSpec((nb, d), lambda l: (0, 0)), pl.BlockSpec((1, d, ns), lambda l: (l, 0, 0)), pl.BlockSpec((1, 1, ns), lambda l: (l, 0, 0))],
        out_specs=pl.BlockSpec((1, nb, ns), lambda l: (l, 0, 0)), sem=("parallel",),
    )(c_all, w, b_shard)


def _ada_bwd_call(name, c_all, dm_shard, dm_all):
    nl, nb, ns = dm_shard.shape
    d = c_all.shape[1]
    nm = dm_all.shape[2]

    def body(c_ref, ds_ref, da_ref, dw_ref, db_ref):
        cv = c_ref[...]
        dw_ref[0] = _dg(cv * _sigmoid(cv), ds_ref[0], ((0,), (0,)))
        db_ref[0] = jnp.sum(da_ref[0], axis=0, keepdims=True)

    return _pcall(
        body, name, [_sds((nl, d, ns), F32), _sds((nl, 1, nm), F32)], grid=(nl,),
        in_specs=[pl.BlockSpec((nb, d), lambda l: (0, 0)), pl.BlockSpec((1, nb, ns), lambda l: (l, 0, 0)), pl.BlockSpec((1, nb, nm), lambda l: (l, 0, 0))],
        out_specs=[pl.BlockSpec((1, d, ns), lambda l: (l, 0, 0)), pl.BlockSpec((1, 1, nm), lambda l: (l, 0, 0))], sem=("parallel",),
    )(c_all, dm_shard, dm_all)


def _ada_fwd(name, bl, c_all, w, b):
    nl, d, ns = w.shape
    chip = 2 * lax.axis_index("x") + lax.axis_index("y")
    b_shard = lax.dynamic_slice(b, (0, chip * ns), (nl, ns)).reshape(nl, 1, ns)
    shard = _ada_fwd_call(name + "_fwd", c_all, w, b_shard)
    (allc,), token = _gather_chips(name + "_ag", [shard])
    mods = jnp.transpose(allc, (1, 2, 0, 3)).reshape(nl, c_all.shape[0], 4 * ns)
    return lax.dynamic_slice(mods, (0, (2 * chip + lax.axis_index("c")) * bl, 0), (nl, bl, 4 * ns)), token


def _ada_bwd(name, bl, c_all, ns, dm):
    nl = dm.shape[0]
    chip = 2 * lax.axis_index("x") + lax.axis_index("y")
    (dm_all,), token = _gather_two_level(name + "_bwd_ag", [dm])
    dm_all = jnp.transpose(dm_all.reshape((8,) + dm.shape), (1, 0, 2, 3)).reshape(nl, 8 * bl, 4 * ns)
    dm_shard = lax.dynamic_slice(dm_all, (0, 0, chip * ns), (nl, 8 * bl, ns))
    dw, db = _ada_bwd_call(name + "_bwd", c_all, dm_shard, dm_all)
    return dw, db.reshape(nl, 4 * ns), token


def _adamw(name, w, g, m, v):
    shape = w.shape
    cols = shape[-1]
    w2, g2, m2, v2 = (t.reshape(-1, cols) for t in (w, g, m, v))
    rows = w2.shape[0]
    cap = max(SUBLANES, (512 * 1024) // max(cols, 1) // SUBLANES * SUBLANES)
    tr = _divisors(rows, cap, SUBLANES)[-1]

    def body(w_ref, g_ref, m_ref, v_ref, d_ref, mo_ref, vo_ref):
        gv = g_ref[...]
        mn = ADAM_B1 * m_ref[...] + (1.0 - ADAM_B1) * gv
        vn = ADAM_B2 * v_ref[...] + (1.0 - ADAM_B2) * (gv * gv)
        m_hat = mn / (1.0 - ADAM_B1 ** ADAM_STEP)
        v_hat = vn / (1.0 - ADAM_B2 ** ADAM_STEP)
        d_ref[...] = -ADAM_LR * (m_hat / (jnp.sqrt(v_hat) + ADAM_EPS) + ADAM_WD * w_ref[...])
        mo_ref[...] = mn
        vo_ref[...] = vn

    spec = pl.BlockSpec((tr, cols), lambda i: (i, 0))
    outs = _pcall(body, name, [_sds((rows, cols), F32)] * 3, grid=(rows // tr,), in_specs=[spec] * 4, out_specs=[spec] * 3, sem=("parallel",))(w2, g2, m2, v2)
    return tuple(o.reshape(shape) for o in outs)


def _adamw_small(name, ws, gs, ms, vs):
    n = len(ws)

    def body(*refs):
        for k in range(n):
            w_ref, g_ref, m_ref, v_ref = (refs[j * n + k] for j in range(4))
            d_ref, mo_ref, vo_ref = (refs[(4 + j) * n + k] for j in range(3))
            gv = g_ref[...]
            mn = ADAM_B1 * m_ref[...] + (1.0 - ADAM_B1) * gv
            vn = ADAM_B2 * v_ref[...] + (1.0 - ADAM_B2) * (gv * gv)
            m_hat = mn / (1.0 - ADAM_B1 ** ADAM_STEP)
            v_hat = vn / (1.0 - ADAM_B2 ** ADAM_STEP)
            d_ref[...] = -ADAM_LR * (m_hat / (jnp.sqrt(v_hat) + ADAM_EPS) + ADAM_WD * w_ref[...])
            mo_ref[...] = mn
            vo_ref[...] = vn

    outs = _pcall(body, name, [_sds(w.shape, F32) for w in ws] * 3)(*ws, *gs, *ms, *vs)
    return outs[:n], outs[n:2 * n], outs[2 * n:]


def _pad_rows(w, rows):
    return jnp.pad(w, ((0, rows - w.shape[0]), (0, 0)))


def _pad_lanes(v):
    return jnp.pad(v, (0, LANES - v.shape[0])).reshape(1, LANES)


BIG = ("w_in", "w_out", "ff_up", "ff_down")
BIG_AXIS = {"w_in": 1, "w_out": 0, "ff_up": 1, "ff_down": 0}
CONVW = ("ssd_conv_w", "ff_conv_w")
SMALL = ("norm1_g", "norm2_g", "ssd_conv_b", "ssd_dt_bias", "ssd_a_log", "ssd_d", "ssd_norm_g", "gm_vnorm_g", "gm_ws", "gm_bs", "gm_out_g", "ff_conv_b")
WEIGHTS = ("ada_w", "ada_b", "norm1_g", "norm2_g", "w_in", "ssd_conv_w", "ssd_conv_b", "ssd_dt_bias", "ssd_a_log", "ssd_d", "ssd_norm_g", "gm_vnorm_g", "gm_ws", "gm_bs", "gm_out_g", "w_out", "ff_up", "ff_conv_w", "ff_conv_b", "ff_down", "final_g")


def kernel(x, c, ada_w, ada_b, norm1_g, norm2_g, w_in, ssd_conv_w, ssd_conv_b, ssd_dt_bias, ssd_a_log, ssd_d, ssd_norm_g, gm_vnorm_g, gm_ws, gm_bs, gm_out_g, w_out, ff_up, ff_conv_w, ff_conv_b, ff_down, final_g, loss_target, m_ada_w, m_ada_b, m_norm1_g, m_norm2_g, m_w_in, m_ssd_conv_w, m_ssd_conv_b, m_ssd_dt_bias, m_ssd_a_log, m_ssd_d, m_ssd_norm_g, m_gm_vnorm_g, m_gm_ws, m_gm_bs, m_gm_out_g, m_w_out, m_ff_up, m_ff_conv_w, m_ff_conv_b, m_ff_down, m_final_g, v_ada_w, v_ada_b, v_norm1_g, v_norm2_g, v_w_in, v_ssd_conv_w, v_ssd_conv_b, v_ssd_dt_bias, v_ssd_a_log, v_ssd_d, v_ssd_norm_g, v_gm_vnorm_g, v_gm_ws, v_gm_bs, v_gm_out_g, v_w_out, v_ff_up, v_ff_conv_w, v_ff_conv_b, v_ff_down, v_final_g):
    given = dict(locals())
    weights = {n: given[n] for n in WEIGHTS}
    bl, s, d = x.shape
    nl = ada_w.shape[0]
    heads = d // HEAD_DIM
    cd = d + 2 * SSD_GROUPS * STATE
    f = ff_down.shape[1] * 4
    n_in = d + cd + heads + 2 * d
    n_proj = _round_up(3 * d + cd + LANES, 2 * LANES)
    tm = _divisors(s, 512)[-1]

    gathering = {}

    def start_gathers(l, names, token):
        for n in names:
            gathering[l, n] = _gather_weights_start(f"wg{l}_{n}", [weights[n][l]], token)
            token = gathering[l, n]["token"]
        return token

    pre, _ = _gather_two_level("pre_ag", [c] + [weights[n] for n in CONVW])
    c_all = pre[0].reshape(8 * bl, d)
    conv_full = {n: jnp.concatenate([p[k, 0] for k in range(4)], axis=-1) for n, p in zip(CONVW, pre[1:])}
    mods_all, ada_token = _ada_fwd("ada", bl, c_all, ada_w, ada_b)
    mods = mods_all.reshape(nl, bl, N_MOD, 1, d)

    ready = {}

    def assembled(n, full):
        if n == "w_in":
            return full
        return full.reshape(-1, full.shape[2]) if BIG_AXIS[n] == 0 else jnp.concatenate([full[k] for k in range(4)], axis=1)

    def landed(l, n, after):
        if l > 0 and (l, n) not in ready:
            fulls, token = _gather_weights_finish(f"wg{l}", [gathering.pop((l, m)) for m in BIG], after)
            ready.update({(l, m): (assembled(m, t), token) for m, t in zip(BIG, fulls)})
        if (l, n) in ready:
            return ready.pop((l, n))
        (full,), token = _gather_weights_finish(f"wg{l}_{n}", [gathering.pop((l, n))], after)
        return assembled(n, full), token

    issued = start_gathers(0, BIG, ada_token)
    r2 = lambda v: v.reshape(1, -1)
    n_gm = d // GM_HEAD
    fg = r2(final_g)

    def layer_params(l):
        return dict(
            norm1=r2(norm1_g[l]), norm2=r2(norm2_g[l]), conv_w8=_pad_rows(conv_full["ssd_conv_w"][l], SUBLANES), conv_b=r2(ssd_conv_b[l]),
            dtb=_pad_lanes(ssd_dt_bias[l]), alog=_pad_lanes(ssd_a_log[l]), dsk=r2(jnp.repeat(ssd_d[l], HEAD_DIM)), ng=r2(ssd_norm_g[l]),
            vg=r2(gm_vnorm_g[l]), ws=gm_ws[l], bs_t=jnp.pad(gm_bs[l].T, ((0, 0), (0, LANES - n_gm))), og=r2(gm_out_g[l]),
            ff_w8=_pad_rows(conv_full["ff_conv_w"][l], SUBLANES), ff_b=r2(ff_conv_b[l]))

    def mixer_args(p):
        return (p["conv_w8"], p["conv_b"], p["dtb"], p["alog"], p["dsk"], p["ng"], p["vg"], p["ws"], p["bs_t"], p["og"])

    def padded_w_in(full):
        ns = full.shape[2]

        def cols(a, b):
            return [full[k][:, max(a, k * ns) - k * ns:min(b, (k + 1) * ns) - k * ns] for k in range(4) if max(a, k * ns) < min(b, (k + 1) * ns)]

        parts = cols(0, d) + cols(d + cd + heads, n_in) + cols(d, d + cd) + cols(d + cd, d + cd + heads)
        return jnp.concatenate(parts + [jnp.zeros((d, n_proj - (3 * d + cd + heads)), BF16)], axis=1)

    saved, xcur, pending = [], x, None
    for l in range(nl):
        p, w = layer_params(l), {}
        wi, token = landed(l, "w_in", mods_all if l == 0 else pending[0])
        w["w_in"] = padded_w_in(wi)
        sh1, sc1, g1, sh2, sc2, g2 = (mods[l, :, k] for k in range(N_MOD))
        if l + 1 < nl:
            issued = start_gathers(l + 1, BIG, token + issued)
        sc1 = _after(sc1, issued)
        if pending is None:
            (h,) = _row_fwd(f"nm{l}_fwd", _nm_tile, [(xcur, d, 0, None)], [sc1, sh1], [p["norm1"]], [(d, BF16)], 0, tm)
            x1 = xcur
        else:
            rows = [(xcur, d, 0, None), (pending[0], d, 0, None)]
            x1, h = _row_fwd(f"rnm{l}a_fwd", _rnm_tile, rows, [pending[1], sc1, sh1], [p["norm1"]], [(d, F32), (d, BF16)], 0, tm)
        proj = _mm(f"win{l}_fwd", h, w["w_in"], F32)
        ycat, (xbc, hprev) = _mixer_fwd(f"mix{l}", proj, d, cd, *mixer_args(p))
        w["w_out"], _ = landed(l, "w_out", ycat)
        mix = _mm(f"wout{l}_fwd", ycat, w["w_out"], F32)
        x2, h2 = _row_fwd(f"rnm{l}b_fwd", _rnm_tile, [(x1, d, 0, None), (mix, d, 0, None)], [g1, sc2, sh2], [p["norm2"]], [(d, F32), (d, BF16)], 0, tm)
        w_up, _ = landed(l, "ff_up", h2)
        w["ff_up"] = _gate_value_blocks(f"ffup{l}_blocks", w_up, f)
        up = _mm(f"ffup{l}_fwd", h2, w["ff_up"], F32)
        act = _conv_fwd(f"ffact{l}_fwd", up, 0, f, p["ff_w8"], p["ff_b"], FF_CONV, True, BF16)
        w["ff_down"], _ = landed(l, "ff_down", act)
        down = _mm(f"ffdown{l}_fwd", act, w["ff_down"], F32)
        saved.append(dict(p=p, w=w, x_in=xcur, pending=pending, h=h, proj=proj, xbc=xbc, hprev=hprev, ycat=ycat, x1=x1, mix=mix, h2=h2, up=up, act=act))
        xcur, pending = x2, (down, g2)
    rows = [(xcur, d, 0, F32), (pending[0], d, 0, BF16), (loss_target, d, 0, None)]

    (dx2, ddown), (dg2,), (dfg,), (loss_local,) = _row_bwd(
        "final_bwd", _final_tile, rows, [pending[1]], [fg], [], 1, tm, [jnp.ones((1, 1), F32)], emit_sums=True)
    loss = lax.psum(loss_local[0, 0], AXES)
    dmods, small_grads = [None] * nl, [None] * nl
    reducing = {}

    def start_reduce(l, named, token=None):
        by_chip = []
        for n, g in named:
            width = g.shape[1] // 4
            by_chip.append(g.reshape(4, g.shape[0] // 4, g.shape[1]) if BIG_AXIS[n] == 0 else jnp.stack([g[:, k * width:(k + 1) * width] for k in range(4)]))
        key = (l, tuple(n for n, _ in named))
        reducing[key] = _reduce_weights_start(f"wg{l}_" + "_".join(key[1]), by_chip, token)
        return reducing[key]["token"]

    for l in reversed(range(nl)):
        sv = saved[l]
        p, w = sv["p"], sv["w"]
        sh1, sc1, g1, sh2, sc2, g2 = (mods[l, :, k] for k in range(N_MOD))
        dact, dw_down = _mm_bwd(f"ffdown{l}", sv["act"], w["ff_down"], ddown)
        dup, dff_w8, dff_b = _conv_bwd(f"ffact{l}_bwd", sv["up"], 0, f, p["ff_w8"], p["ff_b"], FF_CONV, True, dact, BF16)
        dh2, dw_up = _mm_bwd(f"ffup{l}", sv["h2"], w["ff_up"], dup)
        dw_up = _gate_value_blocks(f"ffup{l}_columns", dw_up, f, inverse=True)
        sc2 = _after(sc2, start_reduce(l, [("ff_down", dw_down), ("ff_up", dw_up)]))
        rows = [(sv["x1"], d, 0, F32), (sv["mix"], d, 0, BF16)]
        (dx1, dmix), (dg1, dsc2, dsh2), (dn2,) = _row_bwd(f"rnm{l}b_bwd", _rnm_tile, rows, [g1, sc2, sh2], [p["norm2"]], [(d, F32), (d, BF16)], 0, tm, [dx2, dh2])
        dycat, dw_out = _mm_bwd(f"wout{l}", sv["ycat"], w["w_out"], dmix)
        p_tied = p
        if l == 0:
            p_tied = dict(p, dtb=_after(p["dtb"], start_reduce(l, [("w_out", dw_out)])))
        dproj, (dw8, dcb, ddtb, dalog, ddsk, dng, dvg, dws, dbs_t, dog) = _mixer_bwd(f"mix{l}", sv["proj"], d, cd, *mixer_args(p_tied), sv["xbc"], sv["hprev"], dycat)
        dh, dw_in_p = _mm_bwd(f"win{l}", sv["h"], w["w_in"], dproj)
        dw_in = jnp.concatenate([dw_in_p[:, :d], dw_in_p[:, 3 * d:3 * d + cd], dw_in_p[:, 3 * d + cd:3 * d + cd + heads], dw_in_p[:, d:3 * d]], axis=1)
        if l > 0:
            sc1 = _after(sc1, start_reduce(l, [("w_out", dw_out), ("w_in", dw_in)]))
        if sv["pending"] is None:
            (dx2,), (dsc1, dsh1), (dn1,) = _row_bwd(f"nm{l}_bwd", _nm_res_tile, [(sv["x_in"], d, 0, F32)], [sc1, sh1], [p["norm1"]], [(d, F32), (d, BF16)], 0, tm, [dx1, dh])
        else:
            rows = [(sv["x_in"], d, 0, F32), (sv["pending"][0], d, 0, BF16)]
            (dx2, ddown), (dg2_prev, dsc1, dsh1), (dn1,) = _row_bwd(
                f"rnm{l}a_bwd", _rnm_tile, rows, [sv["pending"][1], sc1, sh1], [p["norm1"]], [(d, F32), (d, BF16)], 0, tm, [dx1, dh])
        dmods[l] = jnp.concatenate([dsh1, dsc1, dg1, dsh2, dsc2, dg2], axis=1).reshape(bl, N_MOD * d)
        if sv["pending"] is not None:
            dg2 = dg2_prev
        small_grads[l] = dict(
            norm1_g=dn1.reshape(d), norm2_g=dn2.reshape(d), ssd_conv_b=dcb.reshape(cd), ssd_dt_bias=ddtb[0, :heads], ssd_a_log=dalog[0, :heads],
            ssd_d=ddsk.reshape(heads, HEAD_DIM).sum(-1), ssd_norm_g=dng.reshape(d), gm_vnorm_g=dvg.reshape(d), gm_ws=dws, gm_bs=dbs_t[:, :n_gm].T,
            gm_out_g=dog.reshape(d), ff_conv_b=dff_b.reshape(f), ssd_conv_w=dw8[:SSD_CONV], ff_conv_w=dff_w8[:FF_CONV])
    grad_x = dx2

    g_ada_w, g_ada_b, ada_token = _ada_bwd("ada", bl, c_all, ada_w.shape[2], jnp.stack(dmods))
    small_names_r = SMALL + CONVW + ("final_g",)
    summed, small_token = _allreduce_small("small", [jnp.stack([small_grads[l][n] for l in range(nl)]) for n in SMALL + CONVW] + [dfg])
    travelling = start_reduce(0, [("w_in", dw_in)], ada_token + small_token)
    grad = {"ada_w": g_ada_w, "ada_b": g_ada_b}
    grad.update(zip(small_names_r, summed))
    chip = 2 * lax.axis_index("x") + lax.axis_index("y")
    for n in CONVW:
        width = weights[n].shape[-1]
        grad[n] = lax.dynamic_slice_in_dim(grad[n], chip * width, width, axis=2)

    delta, new_m, new_v = {}, {}, {}
    last = (0, ("w_in",))
    early = [k for k in reducing if k != last]
    finished, _ = _reduce_weights_finish("wg_early", [reducing[k] for k in early], _after(summed[0], travelling))
    big_grads = dict(zip([(l, n) for l, names in early for n in names], finished))
    order = ("ada_w",) + tuple(reversed(BIG))
    for n in order:
        if n == "w_in":
            (big_grads[0, n],), _ = _reduce_weights_finish("wg_last", [reducing[last]], delta[order[-2]])
        if n != "ada_w":
            grad[n] = jnp.stack([big_grads[l, n] for l in range(nl)])
        delta[n], new_m[n], new_v[n] = _adamw("adam_" + n, weights[n], grad[n], given["m_" + n], given["v_" + n])
    small_names = ("ada_b",) + SMALL + CONVW + ("final_g",)
    as2d = lambda t: t.reshape(1, -1) if t.ndim == 1 else t
    res = _adamw_small(
        "adam_small", [as2d(weights[n]) for n in small_names], [as2d(grad[n]) for n in small_names],
        [as2d(given["m_" + n]) for n in small_names], [as2d(given["v_" + n]) for n in small_names])
    for out, vals in zip((delta, new_m, new_v), res):
        for n, val in zip(small_names, vals):
            out[n] = val.reshape(weights[n].shape)
    grad["final_g"] = grad["final_g"].reshape(final_g.shape)

    return (loss, grad_x, *[grad[n] for n in WEIGHTS], *[delta[n] for n in WEIGHTS], *[new_m[n] for n in WEIGHTS], *[new_v[n] for n in WEIGHTS])
```

```python
import jax
import jax.numpy as jnp
from jax import lax
from jax.experimental import pallas as pl
from jax.experimental.pallas import tpu as pltpu

F32 = jnp.float32
BF16 = jnp.bfloat16
EPS = 1e-6
CHUNK = 128
HEAD_DIM = 64
STATE = 128
GM_HEAD = 128
SSD_GROUPS = 2
SSD_CONV = 4
FF_CONV = 3
N_MOD = 6
LANES = 128
SUBLANES = 8
SMALL_BF16_SIZE = 1 << 16
CONV_LANES = 256
CONV_ROWS = 32
V7X_VMEM_LIMIT = 48 * 1024 * 1024
MM_VMEM_BUDGET = 30 * 1024 * 1024
MM_STEP_MACS = 2048 * 1024 * 1024
MM_ACC_SECONDS_PER_VREG = 1.4e-9
V7X_MXU_FLOPS = 996e12
V7X_HBM_BYTES_PER_S = 3.3e12
V7X_STEP_SECONDS = 0.35e-6
ADAM_LR, ADAM_B1, ADAM_B2, ADAM_EPS, ADAM_WD, ADAM_STEP = 0.001, 0.9, 0.999, 1e-08, 0.01, 10
MESH = pl.DeviceIdType.MESH
AXES = ("x", "y", "c")


def _round_up(n, m):
    return (n + m - 1) // m * m


def _divisors(n, cap, mult=LANES):
    out = [t for t in range(mult, min(n, cap) + 1, mult) if n % t == 0]
    return out or [n]


def _pcall(body, name, out_shape, grid=(), in_specs=None, out_specs=None, scratch=(), sem=None, prefetch=0):
    params = pltpu.CompilerParams(dimension_semantics=sem, vmem_limit_bytes=V7X_VMEM_LIMIT)
    if prefetch:
        spec = pltpu.PrefetchScalarGridSpec(num_scalar_prefetch=prefetch, grid=grid, in_specs=in_specs, out_specs=out_specs, scratch_shapes=list(scratch))
        return pl.pallas_call(body, name=name, out_shape=out_shape, grid_spec=spec, compiler_params=params)
    if in_specs is None:
        return pl.pallas_call(body, name=name, out_shape=out_shape, compiler_params=params)
    return pl.pallas_call(body, name=name, out_shape=out_shape, grid=grid, in_specs=in_specs, out_specs=out_specs, scratch_shapes=list(scratch), compiler_params=params)


def _sds(shape, dtype):
    return jax.ShapeDtypeStruct(tuple(shape), dtype)


def _mm_tiles(m, n, k, a_bytes, b_bytes, o_bytes):
    best, best_key = None, None
    for tm in _divisors(m, 2048):
        for tn in _divisors(n, 2560):
            for tk in _divisors(k, 2560):
                vmem = 2 * (tm * tk * a_bytes + tk * tn * b_bytes + tm * tn * o_bytes) + tm * tn * 4
                if vmem > MM_VMEM_BUDGET or tm * tn * tk > MM_STEP_MACS:
                    continue
                ni, nj, nk = m // tm, n // tn, k // tk
                a_reads = 1 if nk == 1 else nj
                b_reads = 1 if (nk == 1 and nj == 1) else ni
                hbm = a_reads * m * k * a_bytes + b_reads * k * n * b_bytes + m * n * o_bytes
                t = max(2.0 * m * n * k / V7X_MXU_FLOPS, hbm / V7X_HBM_BYTES_PER_S) + ni * nj * nk * V7X_STEP_SECONDS
                if nk > 1:
                    t += ni * nj * nk * (tm * tn // (SUBLANES * LANES)) * MM_ACC_SECONDS_PER_VREG
                key = (-t, tm * tn * tk)
                if best_key is None or key > best_key:
                    best, best_key = (tm, tn, tk), key
    return best


def _matmul(name, a, b, mode, out_dtype):
    if mode == "nn":
        (m, k), n = a.shape, b.shape[1]
    elif mode == "nt":
        (m, k), n = a.shape, b.shape[0]
    else:
        (k, m), n = a.shape, b.shape[1]
    tm, tn, tk = _mm_tiles(m, n, k, a.dtype.itemsize, b.dtype.itemsize, jnp.dtype(out_dtype).itemsize)
    nk = k // tk
    if mode == "nn":
        a_spec = pl.BlockSpec((tm, tk), lambda i, j, kk: (i, kk))
        b_spec = pl.BlockSpec((tk, tn), lambda i, j, kk: (kk, j))
        dims = ((1,), (0,))
    elif mode == "nt":
        a_spec = pl.BlockSpec((tm, tk), lambda i, j, kk: (i, kk))
        b_spec = pl.BlockSpec((tn, tk), lambda i, j, kk: (j, kk))
        dims = ((1,), (1,))
    else:
        a_spec = pl.BlockSpec((tk, tm), lambda i, j, kk: (kk, i))
        b_spec = pl.BlockSpec((tk, tn), lambda i, j, kk: (kk, j))
        dims = ((0,), (0,))

    def body(a_ref, b_ref, o_ref, acc_ref):
        kk = pl.program_id(2)
        p = lax.dot_general(a_ref[...].astype(BF16), b_ref[...].astype(BF16), (dims, ((), ())), preferred_element_type=F32)
        if nk == 1:
            o_ref[...] = p.astype(o_ref.dtype)
        else:
            @pl.when(kk == 0)
            def _():
                acc_ref[...] = p

            @pl.when(kk > 0)
            def _():
                acc_ref[...] += p

            @pl.when(kk == nk - 1)
            def _():
                o_ref[...] = acc_ref[...].astype(o_ref.dtype)

    return _pcall(
        body, name, _sds((m, n), out_dtype), grid=(m // tm, n // tn, nk), in_specs=[a_spec, b_spec],
        out_specs=pl.BlockSpec((tm, tn), lambda i, j, kk: (i, j)), scratch=[pltpu.VMEM((tm, tn), F32)],
        sem=("parallel", "parallel", "arbitrary"),
    )(a, b)


def _mm(name, a, w, out_dtype):
    return _matmul(name, a.reshape(-1, a.shape[-1]), w, "nn", out_dtype).reshape(a.shape[:-1] + (w.shape[1],))


def _mm_bwd(name, a, w, dy):
    a2, dy2 = a.reshape(-1, a.shape[-1]), dy.reshape(-1, dy.shape[-1])
    return _matmul(name + "_dx", dy2, w, "nt", BF16).reshape(a.shape), _matmul(name + "_dw", a2, dy2, "tn", BF16)


def _dg(a, b, dims):
    return lax.dot_general(a.astype(BF16), b.astype(BF16), (dims, ((), ())), preferred_element_type=F32)


@jax.custom_vjp
def _dot_nn(a, b):
    return _dg(a, b, ((1,), (0,)))


_dot_nn.defvjp(lambda a, b: (_dot_nn(a, b), (a, b)), lambda r, d: (_dg(d, r[1], ((1,), (1,))), _dg(r[0], d, ((0,), (0,)))))


@jax.custom_vjp
def _dot_nt(a, b):
    return _dg(a, b, ((1,), (1,)))


_dot_nt.defvjp(lambda a, b: (_dot_nt(a, b), (a, b)), lambda r, d: (_dg(d, r[1], ((1,), (0,))), _dg(d, r[0], ((0,), (0,)))))


@jax.custom_vjp
def _dot_tn(a, b):
    return _dg(a, b, ((0,), (0,)))


_dot_tn.defvjp(lambda a, b: (_dot_tn(a, b), (a, b)), lambda r, d: (_dg(r[1], d, ((1,), (1,))), _dg(r[0], d, ((1,), (0,)))))


def _exact_dot(a, c, dims):
    hi = a.astype(BF16)
    r1 = a - hi.astype(F32)
    mid = r1.astype(BF16)
    lo = (r1 - mid.astype(F32)).astype(BF16)
    cb = c.astype(BF16)
    f = lambda t: lax.dot_general(t, cb, (dims, ((), ())), preferred_element_type=F32)
    return f(hi) + f(mid) + f(lo)


@jax.custom_vjp
def _sel_right(a, c):
    return _exact_dot(a, c, ((1,), (0,)))


_sel_right.defvjp(lambda a, c: (_sel_right(a, c), c), lambda c, d: (_exact_dot(d, c, ((1,), (1,))), jnp.zeros_like(c)))


def _exact_dot_left(c, a, dims):
    hi = a.astype(BF16)
    r1 = a - hi.astype(F32)
    mid = r1.astype(BF16)
    lo = (r1 - mid.astype(F32)).astype(BF16)
    cb = c.astype(BF16)
    f = lambda t: lax.dot_general(cb, t, (dims, ((), ())), preferred_element_type=F32)
    return f(hi) + f(mid) + f(lo)


@jax.custom_vjp
def _sel_left(c, a):
    return _exact_dot_left(c, a, ((1,), (0,)))


_sel_left.defvjp(lambda c, a: (_sel_left(c, a), c), lambda c, d: (jnp.zeros_like(c), _exact_dot_left(c, d, ((0,), (0,)))))


def _sigmoid(x):
    return 0.5 * jnp.tanh(0.5 * x) + 0.5


def _rms(x, g):
    return x * lax.rsqrt(jnp.mean(x * x, axis=-1, keepdims=True) + EPS) * g


def _gelu(x):
    return 0.5 * x * (1.0 + lax.erf(x * (2.0 ** -0.5)))


def _causal(n):
    return lax.broadcasted_iota(jnp.int32, (n, n), 0) >= lax.broadcasted_iota(jnp.int32, (n, n), 1)


def _row_in_specs(rows, bparams, gparams, tm):
    specs = [pl.BlockSpec((1, tm, w), lambda b, i, cb=cb: (b, i, cb)) for (_, w, cb, _) in rows]
    specs += [pl.BlockSpec((1, 1, p.shape[-1]), lambda b, i: (b, 0, 0)) for p in bparams]
    specs += [pl.BlockSpec(p.shape, lambda b, i, n=p.ndim: (0,) * n) for p in gparams]
    return specs


def _row_vals(refs, n_rows, n_b, n_g):
    vals = [r[0].astype(F32) for r in refs[:n_rows]]
    vals += [r[0].astype(F32) for r in refs[n_rows:n_rows + n_b]]
    vals += [r[...].astype(F32) for r in refs[n_rows + n_b:n_rows + n_b + n_g]]
    return vals


def _row_fwd(name, tile, rows, bparams, gparams, outs, n_sum, tm):
    bl, s = rows[0][0].shape[:2]
    n_in = len(rows) + len(bparams) + len(gparams)

    def body(*refs):
        first = (pl.program_id(0) == 0) & (pl.program_id(1) == 0)
        res = tile(*_row_vals(refs, len(rows), len(bparams), len(gparams)))
        o_refs = refs[n_in:]
        for k in range(len(outs)):
            o_refs[k][0] = res[k].astype(o_refs[k].dtype)
        for k in range(n_sum):
            o_ref, val = o_refs[len(outs) + k], res[len(outs) + k]

            @pl.when(first)
            def _(o_ref=o_ref, val=val):
                o_ref[...] = val

            @pl.when(jnp.logical_not(first))
            def _(o_ref=o_ref, val=val):
                o_ref[...] += val

    out_shape = [_sds((bl, s, w), dt) for (w, dt) in outs] + [_sds((1, 1), F32)] * n_sum
    out_specs = [pl.BlockSpec((1, tm, w), lambda b, i: (b, i, 0)) for (w, _) in outs] + [pl.BlockSpec((1, 1), lambda b, i: (0, 0))] * n_sum
    return _pcall(
        body, name, out_shape, grid=(bl, s // tm), in_specs=_row_in_specs(rows, bparams, gparams, tm), out_specs=out_specs,
        sem=("arbitrary", "arbitrary"),
    )(*[r[0] for r in rows], *bparams, *gparams)


def _row_bwd(name, tile, rows, bparams, gparams, outs, n_sum, tm, cts, emit_sums=False):
    bl, s = rows[0][0].shape[:2]
    n_r, n_b, n_g = len(rows), len(bparams), len(gparams)
    n_extra = n_sum if emit_sums else 0
    n_in = n_r + n_b + n_g
    n_ct = len(outs) + n_sum
    grad_rows = [k for k in range(n_r) if rows[k][3]]

    def body(*refs):
        b, i = pl.program_id(0), pl.program_id(1)
        vals = _row_vals(refs, n_r, n_b, n_g)
        ct_refs = refs[n_in:n_in + n_ct]
        ct = [r[0].astype(F32) for r in ct_refs[:len(outs)]] + [r[...] for r in ct_refs[len(outs):]]
        primal, vjp = jax.vjp(tile, *vals)
        grads = tuple(vjp(tuple(ct))) + (tuple(primal[len(outs):]) if emit_sums else ())
        o_refs = refs[n_in + n_ct:]
        for j, k in enumerate(grad_rows):
            o_refs[j][0] = grads[k].astype(o_refs[j].dtype)
        for k in range(n_b):
            o_ref, val = o_refs[len(grad_rows) + k], grads[n_r + k]

            @pl.when(i == 0)
            def _(o_ref=o_ref, val=val):
                o_ref[0] = val

            @pl.when(i > 0)
            def _(o_ref=o_ref, val=val):
                o_ref[0] += val

        first = (b == 0) & (i == 0)
        for k in range(n_g + n_extra):
            o_ref, val = o_refs[len(grad_rows) + n_b + k], grads[n_r + n_b + k]

            @pl.when(first)
            def _(o_ref=o_ref, val=val):
                o_ref[...] = val

            @pl.when(jnp.logical_not(first))
            def _(o_ref=o_ref, val=val):
                o_ref[...] += val

    one = pl.BlockSpec((1, 1), lambda b, i: (0, 0))
    in_specs = _row_in_specs(rows, bparams, gparams, tm)
    in_specs += [pl.BlockSpec((1, tm, w), lambda b, i: (b, i, 0)) for (w, _) in outs] + [one] * n_sum
    out_shape = [_sds((bl, s, rows[k][1]), rows[k][3]) for k in grad_rows]
    out_shape += [_sds(p.shape, F32) for p in bparams] + [_sds(p.shape, F32) for p in gparams] + [_sds((1, 1), F32)] * n_extra
    out_specs = [pl.BlockSpec((1, tm, rows[k][1]), lambda b, i: (b, i, 0)) for k in grad_rows]
    out_specs += [pl.BlockSpec((1, 1, p.shape[-1]), lambda b, i: (b, 0, 0)) for p in bparams]
    out_specs += [pl.BlockSpec(p.shape, lambda b, i, n=p.ndim: (0,) * n) for p in gparams] + [one] * n_extra
    res = _pcall(
        body, name, out_shape, grid=(bl, s // tm), in_specs=in_specs, out_specs=out_specs, sem=("arbitrary", "arbitrary"),
    )(*[r[0] for r in rows], *bparams, *gparams, *cts)
    n0, n1, n2 = len(grad_rows), len(grad_rows) + n_b, len(grad_rows) + n_b + n_g
    return (res[:n0], res[n0:n1], res[n1:n2]) + ((res[n2:],) if emit_sums else ())


def _nm_tile(x, sc, sh, g):
    return (_rms(x, g) * (1.0 + sc) + sh,)


def _nm_res_tile(x, sc, sh, g):
    return x, _rms(x, g) * (1.0 + sc) + sh


def _rnm_tile(x, o, gate, sc, sh, g):
    xn = x + gate * o
    return xn, _rms(xn, g) * (1.0 + sc) + sh


def _final_tile(x, o, tgt, gate, g):
    e = _rms(x + gate * o, g) - tgt
    return (0.5 * jnp.sum(jnp.mean(e * e, axis=-1, keepdims=True), axis=0, keepdims=True),)


def _gm_tile(u_in, v_in, vg, ws, bs_t, og):
    d = u_in.shape[1]
    u, vn = _gelu(u_in), _rms(_gelu(v_in), vg)
    causal = _causal(CHUNK)
    lane = lax.broadcasted_iota(jnp.int32, (1, LANES), 1)
    parts = []
    for h in range(d // GM_HEAD):
        bias = jnp.sum(bs_t * (lane == h).astype(F32), axis=1, keepdims=True)
        parts.append(_dot_nn(jnp.where(causal, ws[h], 0.0), vn[:, h * GM_HEAD:(h + 1) * GM_HEAD]) + bias)
    return (_rms(u * jnp.concatenate(parts, axis=1), og),)


def _conv_window(ref, r0, rows, c0, tc, seq, before, after):
    parts = []
    if before:
        p0 = pl.multiple_of(jnp.maximum(r0 - SUBLANES, 0), SUBLANES)
        parts.append(jnp.where(r0 > 0, ref[0, pl.ds(p0, SUBLANES), pl.ds(c0, tc)].astype(F32), 0.0))
    parts.append(ref[0, pl.ds(r0, rows), pl.ds(c0, tc)].astype(F32))
    if after:
        n0 = pl.multiple_of(jnp.minimum(r0 + rows, seq - SUBLANES), SUBLANES)
        parts.append(jnp.where(r0 + rows < seq, ref[0, pl.ds(n0, SUBLANES), pl.ds(c0, tc)].astype(F32), 0.0))
    return jnp.concatenate(parts, axis=0) if len(parts) > 1 else parts[0]


def _conv_taps_pre(xe, w, b, taps, rows):
    pre = xe[SUBLANES:SUBLANES + rows] * w[taps - 1] + b
    for j in range(1, taps):
        pre = pre + pltpu.roll(xe, j, 0)[SUBLANES:SUBLANES + rows] * w[taps - 1 - j]
    return pre


def _conv_fwd(name, src, col0, chans, w8, b, taps, gated, out_dtype):
    bl, s, _ = src.shape
    tc = CONV_LANES
    xw = 2 * tc if gated else tc

    def body(x_ref, w_ref, b_ref, o_ref):
        w = [w_ref[k:k + 1, :] for k in range(taps)]
        bias = b_ref[...]

        def step(c, carry):
            r0 = pl.multiple_of(c * CONV_ROWS, CONV_ROWS)
            xe = _conv_window(x_ref, r0, CONV_ROWS, 0, tc, s, True, False)
            pre = _conv_taps_pre(xe, w, bias, taps, CONV_ROWS)
            y = pre * _sigmoid(pre)
            if gated:
                y = y * x_ref[0, pl.ds(r0, CONV_ROWS), pl.ds(tc, tc)]
            o_ref[0, pl.ds(r0, CONV_ROWS), :] = y.astype(o_ref.dtype)
            return carry

        lax.fori_loop(0, s // CONV_ROWS, step, 0, unroll=4)

    first = 0 if gated else col0 // tc
    return _pcall(
        body, name, _sds((bl, s, chans), out_dtype), grid=(chans // tc, bl),
        in_specs=[pl.BlockSpec((1, s, xw), lambda j, bb: (bb, 0, first + j)), pl.BlockSpec((SUBLANES, tc), lambda j, bb: (0, j)), pl.BlockSpec((1, tc), lambda j, bb: (0, j))],
        out_specs=pl.BlockSpec((1, s, tc), lambda j, bb: (bb, 0, j)), sem=("parallel", "arbitrary"),
    )(src, w8, b)


def _conv_bwd(name, src, col0, chans, w8, b, taps, gated, dy, dx_dtype):
    bl, s, _ = src.shape
    tc = CONV_LANES
    xw = 2 * tc if gated else tc
    ext = CONV_ROWS + SUBLANES

    def body(x_ref, w_ref, b_ref, dy_ref, dx_ref, dw_ref, db_ref):
        bb = pl.program_id(1)
        w = [w_ref[k:k + 1, :] for k in range(taps)]
        bias = b_ref[...]

        def fold(v):
            acc = v[0:SUBLANES]
            for i in range(1, CONV_ROWS // SUBLANES):
                acc = acc + v[i * SUBLANES:(i + 1) * SUBLANES]
            return acc

        def step(c, carry):
            r0 = pl.multiple_of(c * CONV_ROWS, CONV_ROWS)
            xe = _conv_window(x_ref, r0, CONV_ROWS, 0, tc, s, True, True)
            pre = _conv_taps_pre(xe, w, bias, taps, ext)
            sig = _sigmoid(pre)
            d = _conv_window(dy_ref, r0, CONV_ROWS, 0, tc, s, False, True)
            dsil = d * _conv_window(x_ref, r0, CONV_ROWS, tc, tc, s, False, True) if gated else d
            dpre = dsil * sig * (1.0 + pre * (1.0 - sig))
            dx = dpre[:CONV_ROWS] * w[taps - 1]
            for j in range(1, taps):
                dx = dx + pltpu.roll(dpre, ext - j, 0)[:CONV_ROWS] * w[taps - 1 - j]
            dx_ref[0, pl.ds(r0, CONV_ROWS), pl.ds(0, tc)] = dx.astype(dx_ref.dtype)
            if gated:
                dx_ref[0, pl.ds(r0, CONV_ROWS), pl.ds(tc, tc)] = (d[:CONV_ROWS] * (pre * sig)[:CONV_ROWS]).astype(dx_ref.dtype)
            here = dpre[:CONV_ROWS]
            sums = [fold(here * (pltpu.roll(xe, taps - 1 - k, 0) if k < taps - 1 else xe)[SUBLANES:ext]) + carry[k] for k in range(taps)]
            return tuple(sums) + (fold(here) + carry[taps],)

        zero = jnp.zeros((SUBLANES, tc), F32)
        sums = lax.fori_loop(0, s // CONV_ROWS, step, (zero,) * (taps + 1), unroll=2)
        rows = [jnp.sum(t, axis=0, keepdims=True) for t in sums]
        dw = jnp.concatenate(rows[:taps] + [jnp.zeros_like(rows[0])] * (SUBLANES - taps), axis=0)

        @pl.when(bb == 0)
        def _():
            dw_ref[...] = dw
            db_ref[...] = rows[taps]

        @pl.when(bb > 0)
        def _():
            dw_ref[...] += dw
            db_ref[...] += rows[taps]

    first = 0 if gated else col0 // tc
    out_shape = [_sds((bl, s, chans * (2 if gated else 1)), dx_dtype), _sds((SUBLANES, chans), F32), _sds((1, chans), F32)]
    return _pcall(
        body, name, out_shape, grid=(chans // tc, bl),
        in_specs=[
            pl.BlockSpec((1, s, xw), lambda j, bb: (bb, 0, first + j)), pl.BlockSpec((SUBLANES, tc), lambda j, bb: (0, j)),
            pl.BlockSpec((1, tc), lambda j, bb: (0, j)), pl.BlockSpec((1, s, tc), lambda j, bb: (bb, 0, j)),
        ],
        out_specs=[pl.BlockSpec((1, s, xw), lambda j, bb: (bb, 0, j)), pl.BlockSpec((SUBLANES, tc), lambda j, bb: (0, j)), pl.BlockSpec((1, tc), lambda j, bb: (0, j))],
        sem=("parallel", "arbitrary"),
    )(src, w8, b, dy)


def _gate_value_blocks(name, w, f, inverse=False):
    nb = f // CONV_LANES
    src = (lambda j: (0, 2 * (j % nb) + j // nb)) if inverse else (lambda j: (0, (j % 2) * nb + j // 2))

    def body(x_ref, o_ref):
        o_ref[...] = x_ref[...]

    return _pcall(
        body, name, _sds(w.shape, w.dtype), grid=(2 * nb,), in_specs=[pl.BlockSpec((w.shape[0], CONV_LANES), src)],
        out_specs=pl.BlockSpec((w.shape[0], CONV_LANES), lambda j: (0, j)), sem=("parallel",),
    )(w)


def _ssd_step(states, xs, bm, cm, dtr, z, dtb, alog, dsk, ng):
    gw = xs.shape[1] // SSD_GROUPS
    hpg, npair = gw // HEAD_DIM, gw // LANES
    dt = jax.nn.softplus(dtr + dtb)
    da = dt * (-jnp.exp(alog))
    causal = _causal(CHUNK)
    acs = _sel_left(causal.astype(F32), da)
    acs_t = acs.T
    lane = lax.broadcasted_iota(jnp.int32, (1, LANES), 1)
    sub = lax.broadcasted_iota(jnp.int32, (LANES, 1), 0)
    last = lax.broadcasted_iota(jnp.int32, (CHUNK, gw), 0) == CHUNK - 1
    outs, new_states = [], []
    for g in range(SSD_GROUPS):
        xg, zg = xs[:, g * gw:(g + 1) * gw], z[:, g * gw:(g + 1) * gw]
        bg, cg = bm[:, g * STATE:(g + 1) * STATE], cm[:, g * STATE:(g + 1) * STATE]
        head_of_col = g * hpg + lax.broadcasted_iota(jnp.int32, (LANES, gw), 1) // HEAD_DIM
        expand = (lax.broadcasted_iota(jnp.int32, (LANES, gw), 0) == head_of_col).astype(F32)
        dt_e, acs_e = _sel_right(dt, expand), _sel_right(acs, expand)
        alast_e = jnp.sum(jnp.where(last, acs_e, 0.0), axis=0, keepdims=True)
        xc = xg * dt_e
        xc_st = xc * jnp.exp(alast_e - acs_e)
        decay_out, chunk_decay = jnp.exp(acs_e), jnp.exp(alast_e)
        cb = _dot_nt(cg, bg)
        ys = []
        for p in range(npair):
            sl = slice(p * LANES, (p + 1) * LANES)
            state = states[g * npair + p]
            y = _dot_nn(cg, state) * decay_out[:, sl]
            for q in range(2):
                head = g * hpg + 2 * p + q
                col = jnp.sum(acs * (lane == head).astype(F32), axis=1, keepdims=True)
                row = jnp.sum(acs_t * (sub == head).astype(F32), axis=0, keepdims=True)
                decay = jnp.where(causal, jnp.exp(jnp.where(causal, col - row, 0.0)), 0.0)
                y = y + _dot_nn(cb * decay, xc[:, sl] * ((lane // HEAD_DIM) == q).astype(F32))
            ys.append(y)
            new_states.append(state * chunk_decay[:, sl] + _dot_tn(bg, xc_st[:, sl]))
        y = jnp.concatenate(ys, axis=1) + dsk[:, g * gw:(g + 1) * gw] * xg
        outs.append(_rms(y * (zg * _sigmoid(zg)), ng[:, g * gw:(g + 1) * gw]))
    return tuple(new_states), jnp.concatenate(outs, axis=1)


def _ssd_scan_specs(d, dt_col, nc, rev):
    ci = (lambda i: nc - 1 - i) if rev else (lambda i: i)
    bc = SSD_GROUPS * STATE
    return [
        pl.BlockSpec((1, CHUNK, d), lambda b, i: (b, ci(i), 0)),
        pl.BlockSpec((1, CHUNK, bc), lambda b, i: (b, ci(i), d // bc)),
        pl.BlockSpec((1, CHUNK, bc), lambda b, i: (b, ci(i), d // bc + 1)),
        pl.BlockSpec((1, CHUNK, LANES), lambda b, i: (b, ci(i), dt_col // LANES)),
        pl.BlockSpec((1, CHUNK, d), lambda b, i: (b, ci(i), 0)),
        pl.BlockSpec((1, LANES), lambda b, i: (0, 0)), pl.BlockSpec((1, LANES), lambda b, i: (0, 0)),
        pl.BlockSpec((1, d), lambda b, i: (0, 0)), pl.BlockSpec((1, d), lambda b, i: (0, 0)),
    ]


def _ssd_scan_fwd(name, xbc, proj, dt_col, dtb, alog, dsk, ng):
    bl, s, _ = xbc.shape
    d = dsk.shape[1]
    nc, nst = s // CHUNK, d // LANES

    def body(xs_ref, bm_ref, cm_ref, dt_ref, z_ref, dtb_ref, alog_ref, dsk_ref, ng_ref, y_ref, hp_ref, st_ref):
        @pl.when(pl.program_id(1) == 0)
        def _():
            st_ref[...] = jnp.zeros_like(st_ref)

        states = tuple(st_ref[p] for p in range(nst))
        hp_ref[0, 0] = st_ref[...]
        new_states, yn = _ssd_step(states, xs_ref[0], bm_ref[0], cm_ref[0], dt_ref[0], z_ref[0], dtb_ref[...], alog_ref[...], dsk_ref[...], ng_ref[...])
        for p in range(nst):
            st_ref[p] = new_states[p]
        y_ref[0] = yn.astype(y_ref.dtype)

    return _pcall(
        body, name, [_sds((bl, s, d), BF16), _sds((bl, nc, nst, STATE, LANES), F32)], grid=(bl, nc), in_specs=_ssd_scan_specs(d, dt_col, nc, False),
        out_specs=[pl.BlockSpec((1, CHUNK, d), lambda b, i: (b, i, 0)), pl.BlockSpec((1, 1, nst, STATE, LANES), lambda b, i: (b, i, 0, 0, 0))],
        scratch=[pltpu.VMEM((nst, STATE, LANES), F32)], sem=("arbitrary", "arbitrary"),
    )(xbc, xbc, xbc, proj, proj, dtb, alog, dsk, ng)


def _ssd_scan_bwd(name, xbc, proj, dt_col, dtb, alog, dsk, ng, hprev, dy):
    bl, s, _ = xbc.shape
    d = dsk.shape[1]
    nc, nst, bc = s // CHUNK, d // LANES, SSD_GROUPS * STATE

    def body(xs_ref, bm_ref, cm_ref, dt_ref, z_ref, dtb_ref, alog_ref, dsk_ref, ng_ref, hp_ref, dy_ref,
             dxs_ref, dbm_ref, dcm_ref, ddt_ref, dz_ref, ddtb_ref, dalog_ref, ddsk_ref, dng_ref, dst_ref):
        b, i = pl.program_id(0), pl.program_id(1)

        @pl.when(i == 0)
        def _():
            dst_ref[...] = jnp.zeros_like(dst_ref)

        states = tuple(hp_ref[0, 0, p] for p in range(nst))
        _, vjp = jax.vjp(_ssd_step, states, xs_ref[0], bm_ref[0], cm_ref[0], dt_ref[0], z_ref[0], dtb_ref[...], alog_ref[...], dsk_ref[...], ng_ref[...])
        d_states, dxs, dbm, dcm, ddt, dz, ddtb, dalog, ddsk, dng = vjp((tuple(dst_ref[p] for p in range(nst)), dy_ref[0].astype(F32)))
        for p in range(nst):
            dst_ref[p] = d_states[p]
        dxs_ref[0], dbm_ref[0], dcm_ref[0], ddt_ref[0] = dxs, dbm, dcm, ddt
        dz_ref[0] = dz.astype(dz_ref.dtype)
        first = (b == 0) & (i == 0)
        for o_ref, val in ((ddtb_ref, ddtb), (dalog_ref, dalog), (ddsk_ref, ddsk), (dng_ref, dng)):
            @pl.when(first)
            def _(o_ref=o_ref, val=val):
                o_ref[...] = val

            @pl.when(jnp.logical_not(first))
            def _(o_ref=o_ref, val=val):
                o_ref[...] += val

    rc = lambda i: nc - 1 - i
    in_specs = _ssd_scan_specs(d, dt_col, nc, True) + [
        pl.BlockSpec((1, 1, nst, STATE, LANES), lambda b, i: (b, rc(i), 0, 0, 0)), pl.BlockSpec((1, CHUNK, d), lambda b, i: (b, rc(i), 0)),
    ]
    out_shape = [
        _sds((bl, s, d), F32), _sds((bl, s, bc), F32), _sds((bl, s, bc), F32), _sds((bl, s, LANES), F32), _sds((bl, s, d), BF16),
        _sds((1, LANES), F32), _sds((1, LANES), F32), _sds((1, d), F32), _sds((1, d), F32),
    ]
    row = lambda w: pl.BlockSpec((1, CHUNK, w), lambda b, i: (b, rc(i), 0))
    whole = lambda w: pl.BlockSpec((1, w), lambda b, i: (0, 0))
    out_specs = [row(d), row(bc), row(bc), row(LANES), row(d), whole(LANES), whole(LANES), whole(d), whole(d)]
    return _pcall(
        body, name, out_shape, grid=(bl, nc), in_specs=in_specs, out_specs=out_specs,
        scratch=[pltpu.VMEM((nst, STATE, LANES), F32)], sem=("arbitrary", "arbitrary"),
    )(xbc, xbc, xbc, proj, proj, dtb, alog, dsk, ng, hprev, dy)


def _mixer_fwd(name, proj, d, cd, conv_w8, conv_b, dtb, alog, dsk, ng, vg, ws, bs_t, og):
    xbc = _conv_fwd(name + "_conv_fwd", proj, 3 * d, cd, conv_w8, conv_b, SSD_CONV, False, F32)
    y, hprev = _ssd_scan_fwd(name + "_ssd_fwd", xbc, proj, 3 * d + cd, dtb, alog, dsk, ng)
    rows = [(proj, d, 1, BF16), (proj, d, 2, BF16)]
    (g_out,) = _row_fwd(name + "_gm_fwd", _gm_tile, rows, [], [vg, ws, bs_t, og], [(d, BF16)], 0, CHUNK)
    return jnp.concatenate([y, g_out], axis=-1), (xbc, hprev)


def _mixer_bwd(name, proj, d, cd, conv_w8, conv_b, dtb, alog, dsk, ng, vg, ws, bs_t, og, xbc, hprev, dycat):
    bl, s, n_proj = proj.shape
    dt_col = 3 * d + cd
    dy, dg_out = dycat[..., :d], dycat[..., d:]
    dxs, dbm, dcm, ddt, dz, ddtb, dalog, ddsk, dng = _ssd_scan_bwd(name + "_ssd_bwd", xbc, proj, dt_col, dtb, alog, dsk, ng, hprev, dy)
    dxbc_act = jnp.concatenate([dxs, dbm, dcm], axis=-1)
    dxbc, dw8, dcb = _conv_bwd(name + "_conv_bwd", proj, 3 * d, cd, conv_w8, conv_b, SSD_CONV, False, dxbc_act, BF16)
    rows = [(proj, d, 1, BF16), (proj, d, 2, BF16)]
    (du, dv), _, (dvg, dws, dbs_t, dog) = _row_bwd(name + "_gm_bwd", _gm_tile, rows, [], [vg, ws, bs_t, og], [(d, BF16)], 0, CHUNK, [dg_out])
    pad = jnp.zeros((bl, s, n_proj - dt_col - LANES), BF16)
    dproj = jnp.concatenate([dz, du, dv, dxbc, ddt.astype(BF16), pad], axis=-1)
    return dproj, (dw8, dcb, ddtb, dalog, ddsk, dng, dvg, dws, dbs_t, dog)


def _position():
    return lax.axis_index("x"), lax.axis_index("y"), lax.axis_index("c")


def _at(ref, idx):
    return ref.at[idx] if len(idx) else ref


def _exchange(name, inputs, out_shapes, plan, inplace=False, after=None):
    if inplace:
        out_shapes = [(a.shape, a.dtype) for a in inputs]
    n_in, n_out = len(inputs), len(out_shapes)
    n_after = 0 if after is None else 1
    n_copy = len(plan(0, 0, 0))

    def body(*refs):
        in_refs, out_refs, token = refs[:n_in], refs[n_in + n_after:n_in + n_after + n_out], refs[n_in + n_after + n_out]
        send_sems, recv_sems = refs[n_in + n_after + n_out + 1:]
        token[...] = jnp.zeros_like(token)
        x, y, c = _position()
        copies = plan(x, y, c)

        def copy(k, src, dst, peer):
            return pltpu.make_async_remote_copy(src_ref=src, dst_ref=dst, send_sem=send_sems.at[k], recv_sem=recv_sems.at[k], device_id=peer, device_id_type=MESH)

        src_refs = out_refs if inplace else in_refs
        sends = [copy(k, _at(src_refs[sa], si), _at(out_refs[da], di), peer) for k, (sa, si, da, di, peer, _) in enumerate(copies)]
        for cp in sends:
            cp.start()
        for k, (sa, si, da, _, peer, li) in enumerate(copies):
            copy(k, _at(src_refs[sa], si), _at(out_refs[da], li), peer).wait_recv()
        for cp in sends:
            cp.wait_send()

    any_spec = pl.BlockSpec(memory_space=pl.ANY)
    outs = pl.pallas_call(
        body, name=name, out_shape=[_sds(s, dt) for s, dt in out_shapes] + [_sds((SUBLANES, LANES), F32)], in_specs=[any_spec] * (n_in + n_after),
        out_specs=[any_spec] * n_out + [pl.BlockSpec(memory_space=pltpu.VMEM)],
        scratch_shapes=[pltpu.SemaphoreType.DMA((n_copy,)), pltpu.SemaphoreType.DMA((n_copy,))],
        input_output_aliases={i: i for i in range(n_in)} if inplace else {},
    )(*inputs, *([] if after is None else [after]))
    return list(outs[:n_out]), outs[n_out]


def _exchange_start(name, inputs, out_shapes, plan):
    n_in, n_out = len(inputs), len(out_shapes)
    n_copy = len(plan(0, 0, 0))

    def body(*refs):
        in_refs, land_refs = refs[:n_in], refs[n_in:n_in + n_out]
        send_sems, recv_sems = refs[n_in + n_out:n_in + n_out + 2]
        token = refs[-1]
        x, y, c = _position()
        for k, (sa, si, da, di, peer, _) in enumerate(plan(x, y, c)):
            pltpu.make_async_remote_copy(
                src_ref=_at(in_refs[sa], si), dst_ref=_at(land_refs[da], di), send_sem=send_sems.at[k], recv_sem=recv_sems.at[k],
                device_id=peer, device_id_type=MESH).start()
        token[...] = jnp.zeros_like(token)

    hbm, sem = pl.BlockSpec(memory_space=pltpu.HBM), pl.BlockSpec(memory_space=pltpu.SEMAPHORE)
    lands = [lax.empty(s, dt) for s, dt in out_shapes]
    args = [pltpu.with_memory_space_constraint(a, pltpu.HBM) for a in list(inputs) + lands]
    outs = pl.pallas_call(
        body, name=name,
        out_shape=(pltpu.SemaphoreType.DMA((n_copy,)), pltpu.SemaphoreType.DMA((n_copy,)), *[pltpu.HBM(a.shape, a.dtype) for a in args], _sds((SUBLANES, LANES), F32)),
        in_specs=[hbm] * (n_in + n_out), out_specs=(sem, sem, *[hbm] * (n_in + n_out), pl.BlockSpec(memory_space=pltpu.VMEM)),
        input_output_aliases={i: 2 + i for i in range(n_in + n_out)},
        compiler_params=pltpu.CompilerParams(has_side_effects=pltpu.SideEffectType.DATAFLOW_SIDE_EFFECTING),
    )(*args)
    return dict(name=name, plan=plan, sems=outs[:2], ins=list(outs[2:2 + n_in]), lands=list(outs[2 + n_in:2 + n_in + n_out]), token=outs[-1])


def _exchange_wait(started, after):
    plan, n_in, n_out = started["plan"], len(started["ins"]), len(started["lands"])

    def body(*refs):
        in_refs, land_refs = refs[:n_in], refs[n_in:n_in + n_out]
        send_sems, recv_sems = refs[n_in + n_out:n_in + n_out + 2]
        token = refs[-1]
        x, y, c = _position()
        for k, (sa, si, da, _, peer, li) in enumerate(plan(x, y, c)):
            cp = pltpu.make_async_remote_copy(
                src_ref=_at(in_refs[sa], si), dst_ref=_at(land_refs[da], li), send_sem=send_sems.at[k], recv_sem=recv_sems.at[k],
                device_id=peer, device_id_type=MESH)
            cp.wait_send()
            cp.wait_recv()
        token[...] = jnp.zeros_like(token)

    hbm, sem = pl.BlockSpec(memory_space=pltpu.HBM), pl.BlockSpec(memory_space=pltpu.SEMAPHORE)
    bufs = started["ins"] + started["lands"]
    outs = pl.pallas_call(
        body, name=started["name"] + "_wait", out_shape=(*[pltpu.HBM(a.shape, a.dtype) for a in bufs], _sds((SUBLANES, LANES), F32)),
        in_specs=[hbm] * len(bufs) + [sem, sem, pl.BlockSpec(memory_space=pl.ANY)], out_specs=(*[hbm] * len(bufs), pl.BlockSpec(memory_space=pltpu.VMEM)),
        input_output_aliases={i: i for i in range(len(bufs))},
        compiler_params=pltpu.CompilerParams(has_side_effects=pltpu.SideEffectType.DATAFLOW_SIDE_EFFECTING),
    )(*bufs, *started["sems"], after)
    return list(outs[:n_in]), list(outs[n_in:n_in + n_out]), outs[-1]


def _after(value, token):
    return value + token[0, 0].astype(value.dtype)


def _chip_peers(x, y, c):
    return [(1 - x, y, c), (x, 1 - y, c), (1 - x, 1 - y, c)]


def _chip_of(p):
    return 2 * p[0] + p[1]


def _set_slot(slots, me, blk):
    return lax.dynamic_update_slice(slots, blk[None], (me,) + (0,) * blk.ndim)


def _by_core(c, mine, other, axis):
    return jnp.where(c == 0, jnp.stack([mine, other], axis), jnp.stack([other, mine], axis))


def _plan_gather_chips(n):
    def plan(x, y, c):
        return [(a, (), a, (2 * x + y,), p, (_chip_of(p),)) for a in range(n) for p in _chip_peers(x, y, c)]

    return plan


def _gather_chips(name, blocks):
    recv, token = _exchange(name, blocks, [((4,) + b.shape, b.dtype) for b in blocks], _plan_gather_chips(len(blocks)))
    x, y, _ = _position()
    return [_set_slot(r, 2 * x + y, b) for r, b in zip(recv, blocks)], token


def _gather_pass_cores(name, blocks, from_chips):
    n = len(blocks)

    def plan_cores(x, y, c):
        me, sib = 2 * x + y, (x, y, 1 - c)
        own = [(a, (), a, (me,), sib, (me,)) for a in range(n)]
        passed = [(n + a, (_chip_of(p),), a, (_chip_of(p),), sib, (_chip_of(p),)) for a in range(n) for p in _chip_peers(x, y, c)]
        return own + passed

    from_core, token = _exchange(name + "_cores", list(blocks) + list(from_chips), [((4,) + b.shape, b.dtype) for b in blocks], plan_cores)
    x, y, c = _position()
    return [_by_core(c, _set_slot(r1, 2 * x + y, b), r2, 1) for b, r1, r2 in zip(blocks, from_chips, from_core)], token


def _gather_two_level(name, blocks):
    from_chips, _ = _exchange(name + "_chips", blocks, [((4,) + b.shape, b.dtype) for b in blocks], _plan_gather_chips(len(blocks)))
    return _gather_pass_cores(name, blocks, from_chips)


def _pair_add(name, g42, r4):
    _, _, rh, cols = g42.shape
    tr = _divisors(rh, 512, SUBLANES * 2)[-1]

    def body(c_ref, a_ref, b_ref, o_ref):
        o_ref[0] = (a_ref[0, 0].astype(F32) + b_ref[0].astype(F32)).astype(o_ref.dtype)

    cidx = lax.axis_index("c").astype(jnp.int32).reshape(1)
    return _pcall(
        body, name, _sds(r4.shape, BF16), grid=(4, rh // tr),
        in_specs=[pl.BlockSpec((1, 1, tr, cols), lambda s, i, c_ref: (s, c_ref[0], i, 0)), pl.BlockSpec((1, tr, cols), lambda s, i, c_ref: (s, i, 0))],
        out_specs=pl.BlockSpec((1, tr, cols), lambda s, i, c_ref: (s, i, 0)), sem=("parallel", "parallel"), prefetch=1,
    )(cidx, g42, r4)


def _slot_sum(name, parts):
    n, r, cols = parts.shape
    cap = max(2 * SUBLANES, (4 * 1024 * 1024) // (n * cols * parts.dtype.itemsize))
    tr = _divisors(r, cap, 2 * SUBLANES)[-1]

    def body(p_ref, o_ref):
        acc = p_ref[0].astype(F32)
        for k in range(1, n):
            acc = acc + p_ref[k].astype(F32)
        o_ref[...] = acc

    return _pcall(
        body, name, _sds((r, cols), F32), grid=(r // tr,), in_specs=[pl.BlockSpec((n, tr, cols), lambda i: (0, i, 0))],
        out_specs=pl.BlockSpec((tr, cols), lambda i: (i, 0)), sem=("parallel",),
    )(parts)


def _slot_sums(name, parts):
    k = len(parts)

    def body(*refs):
        for p_ref, o_ref in zip(refs[:k], refs[k:]):
            acc = p_ref[0].astype(F32)
            for j in range(1, p_ref.shape[0]):
                acc = acc + p_ref[j].astype(F32)
            o_ref[...] = acc

    return list(_pcall(body, name, [_sds(p.shape[1:], F32) for p in parts])(*parts))


def _gather_weights_start(name, shards, token):
    c = lax.axis_index("c")
    halves = [lax.dynamic_slice_in_dim(w, c * (w.shape[0] // 2), w.shape[0] // 2, 0).astype(BF16) for w in shards]
    if token is not None:
        halves[0] = _after(halves[0], token)
    n = len(halves)

    def plan(x, y, c):
        return [(a, (), a, (2 * x + y, c), p, (_chip_of(p), c)) for a in range(n) for p in _chip_peers(x, y, c)]

    return _exchange_start(name + "_ag_chips", halves, [((4, 2) + h.shape, h.dtype) for h in halves], plan)


def _gather_weights_finish(name, starteds, after):
    halves, slots = [], []
    for started in starteds:
        h, s, token = _exchange_wait(started, after)
        halves, slots = halves + h, slots + s
    n = len(halves)
    x, y, c = _position()
    slots = [lax.dynamic_update_slice(s, h[None, None], (2 * x + y, c, 0, 0)) for s, h in zip(slots, halves)]

    def plan_cores(x, y, c):
        return [(a, (s, c), a, (s, c), (x, y, 1 - c), (s, 1 - c)) for a in range(n) for s in range(4)]

    full, _ = _exchange(name + "_ag_cores", slots, None, plan_cores, inplace=True)
    return [f.reshape((4, 2 * h.shape[0], h.shape[1])) for f, h in zip(full, halves)], token


def _reduce_weights_start(name, grads, token):
    n = len(grads)
    g42 = [g.reshape(4, 2, g.shape[1] // 2, g.shape[2]) for g in grads]

    def plan_swap(x, y, c):
        return [(a, (s, 1 - c), a, (s,), (x, y, 1 - c), (s,)) for a in range(n) for s in range(4)]

    def plan_chips(x, y, c):
        return [(a, (_chip_of(p),), a, (2 * x + y,), p, (_chip_of(p),)) for a in range(n) for p in _chip_peers(x, y, c)]

    other, _ = _exchange(name + "_rs_cores", g42, [((4,) + g.shape[2:], g.dtype) for g in g42], plan_swap, after=token)
    pair = [_pair_add(f"{name}_rs_pair{a}", g, o) for a, (g, o) in enumerate(zip(g42, other))]
    return _exchange_start(name + "_rs_chips", pair, [(p.shape, p.dtype) for p in pair], plan_chips)


def _reduce_weights_finish(name, starteds, after):
    pair, recv = [], []
    for started in starteds:
        p, r, token = _exchange_wait(started, after)
        pair, recv = pair + p, recv + r
    n = len(pair)
    x, y, c = _position()
    me = 2 * x + y

    def plan_share(x, y, c):
        return [(a, (c,), a, (c,), (x, y, 1 - c), (1 - c,)) for a in range(n)]

    parts = [lax.dynamic_update_slice(r, lax.dynamic_slice_in_dim(p, me, 1, 0), (me, 0, 0)) for r, p in zip(recv, pair)]
    mine = [_slot_sum(f"{name}_rs_sum{a}", p) for a, p in enumerate(parts)]
    both = [lax.dynamic_update_slice(lax.empty((2,) + m.shape, m.dtype), m[None], (c, 0, 0)) for m in mine]
    both, _ = _exchange(name + "_rs_share", both, None, plan_share, inplace=True)
    return [b.reshape(2 * b.shape[1], b.shape[2]) for b in both], token


def _allreduce_small(name, grads):
    both, token = _gather_two_level(name + "_ag", [g.astype(BF16) if g.size >= SMALL_BF16_SIZE else g for g in grads])
    return _slot_sums(name + "_sum", [g.reshape((8,) + g.shape[2:]) for g in both]), token


def _ada_fwd_call(name, c_all, w, b_shard):
    nl, d, ns = w.shape
    nb = c_all.shape[0]

    def body(c_ref, w_ref, b_ref, o_ref):
        cv = c_ref[...]
        o_ref[0] = _dg(cv * _sigmoid(cv), w_ref[0], ((1,), (0,))) + b_ref[0]

    return _pcall(
        body, name, _sds((nl, nb, ns), F32), grid=(nl,),
        in_specs=[pl.BlockSpec((nb, d), lambda l: (0, 0)), pl.BlockSpec((1, d, ns), lambda l: (l, 0, 0)), pl.BlockSpec((1, 1, ns), lambda l: (l, 0, 0))],
        out_specs=pl.BlockSpec((1, nb, ns), lambda l: (l, 0, 0)), sem=("parallel",),
    )(c_all, w, b_shard)


def _ada_bwd_call(name, c_all, dm_shard, dm_all):
    nl, nb, ns = dm_shard.shape
    d = c_all.shape[1]
    nm = dm_all.shape[2]

    def body(c_ref, ds_ref, da_ref, dw_ref, db_ref):
        cv = c_ref[...]
        dw_ref[0] = _dg(cv * _sigmoid(cv), ds_ref[0], ((0,), (0,)))
        db_ref[0] = jnp.sum(da_ref[0], axis=0, keepdims=True)

    return _pcall(
        body, name, [_sds((nl, d, ns), F32), _sds((nl, 1, nm), F32)], grid=(nl,),
        in_specs=[pl.BlockSpec((nb, d), lambda l: (0, 0)), pl.BlockSpec((1, nb, ns), lambda l: (l, 0, 0)), pl.BlockSpec((1, nb, nm), lambda l: (l, 0, 0))],
        out_specs=[pl.BlockSpec((1, d, ns), lambda l: (l, 0, 0)), pl.BlockSpec((1, 1, nm), lambda l: (l, 0, 0))], sem=("parallel",),
    )(c_all, dm_shard, dm_all)


def _ada_fwd(name, bl, c_all, w, b):
    nl, d, ns = w.shape
    chip = 2 * lax.axis_index("x") + lax.axis_index("y")
    b_shard = lax.dynamic_slice(b, (0, chip * ns), (nl, ns)).reshape(nl, 1, ns)
    shard = _ada_fwd_call(name + "_fwd", c_all, w, b_shard)
    (allc,), token = _gather_chips(name + "_ag", [shard])
    mods = jnp.transpose(allc, (1, 2, 0, 3)).reshape(nl, c_all.shape[0], 4 * ns)
    return lax.dynamic_slice(mods, (0, (2 * chip + lax.axis_index("c")) * bl, 0), (nl, bl, 4 * ns)), token


def _ada_bwd(name, bl, c_all, ns, dm):
    nl = dm.shape[0]
    chip = 2 * lax.axis_index("x") + lax.axis_index("y")
    (dm_all,), token = _gather_two_level(name + "_bwd_ag", [dm])
    dm_all = jnp.transpose(dm_all.reshape((8,) + dm.shape), (1, 0, 2, 3)).reshape(nl, 8 * bl, 4 * ns)
    dm_shard = lax.dynamic_slice(dm_all, (0, 0, chip * ns), (nl, 8 * bl, ns))
    dw, db = _ada_bwd_call(name + "_bwd", c_all, dm_shard, dm_all)
    return dw, db.reshape(nl, 4 * ns), token


def _adamw(name, w, g, m, v):
    shape = w.shape
    cols = shape[-1]
    w2, g2, m2, v2 = (t.reshape(-1, cols) for t in (w, g, m, v))
    rows = w2.shape[0]
    cap = max(SUBLANES, (512 * 1024) // max(cols, 1) // SUBLANES * SUBLANES)
    tr = _divisors(rows, cap, SUBLANES)[-1]

    def body(w_ref, g_ref, m_ref, v_ref, d_ref, mo_ref, vo_ref):
        gv = g_ref[...]
        mn = ADAM_B1 * m_ref[...] + (1.0 - ADAM_B1) * gv
        vn = ADAM_B2 * v_ref[...] + (1.0 - ADAM_B2) * (gv * gv)
        m_hat = mn / (1.0 - ADAM_B1 ** ADAM_STEP)
        v_hat = vn / (1.0 - ADAM_B2 ** ADAM_STEP)
        d_ref[...] = -ADAM_LR * (m_hat / (jnp.sqrt(v_hat) + ADAM_EPS) + ADAM_WD * w_ref[...])
        mo_ref[...] = mn
        vo_ref[...] = vn

    spec = pl.BlockSpec((tr, cols), lambda i: (i, 0))
    outs = _pcall(body, name, [_sds((rows, cols), F32)] * 3, grid=(rows // tr,), in_specs=[spec] * 4, out_specs=[spec] * 3, sem=("parallel",))(w2, g2, m2, v2)
    return tuple(o.reshape(shape) for o in outs)


def _adamw_small(name, ws, gs, ms, vs):
    n = len(ws)

    def body(*refs):
        for k in range(n):
            w_ref, g_ref, m_ref, v_ref = (refs[j * n + k] for j in range(4))
            d_ref, mo_ref, vo_ref = (refs[(4 + j) * n + k] for j in range(3))
            gv = g_ref[...]
            mn = ADAM_B1 * m_ref[...] + (1.0 - ADAM_B1) * gv
            vn = ADAM_B2 * v_ref[...] + (1.0 - ADAM_B2) * (gv * gv)
            m_hat = mn / (1.0 - ADAM_B1 ** ADAM_STEP)
            v_hat = vn / (1.0 - ADAM_B2 ** ADAM_STEP)
            d_ref[...] = -ADAM_LR * (m_hat / (jnp.sqrt(v_hat) + ADAM_EPS) + ADAM_WD * w_ref[...])
            mo_ref[...] = mn
            vo_ref[...] = vn

    outs = _pcall(body, name, [_sds(w.shape, F32) for w in ws] * 3)(*ws, *gs, *ms, *vs)
    return outs[:n], outs[n:2 * n], outs[2 * n:]


def _pad_rows(w, rows):
    return jnp.pad(w, ((0, rows - w.shape[0]), (0, 0)))


def _pad_lanes(v):
    return jnp.pad(v, (0, LANES - v.shape[0])).reshape(1, LANES)


BIG = ("w_in", "w_out", "ff_up", "ff_down")
BIG_AXIS = {"w_in": 1, "w_out": 0, "ff_up": 1, "ff_down": 0}
CONVW = ("ssd_conv_w", "ff_conv_w")
SMALL = ("norm1_g", "norm2_g", "ssd_conv_b", "ssd_dt_bias", "ssd_a_log", "ssd_d", "ssd_norm_g", "gm_vnorm_g", "gm_ws", "gm_bs", "gm_out_g", "ff_conv_b")
WEIGHTS = ("ada_w", "ada_b", "norm1_g", "norm2_g", "w_in", "ssd_conv_w", "ssd_conv_b", "ssd_dt_bias", "ssd_a_log", "ssd_d", "ssd_norm_g", "gm_vnorm_g", "gm_ws", "gm_bs", "gm_out_g", "w_out", "ff_up", "ff_conv_w", "ff_conv_b", "ff_down", "final_g")


def kernel(x, c, ada_w, ada_b, norm1_g, norm2_g, w_in, ssd_conv_w, ssd_conv_b, ssd_dt_bias, ssd_a_log, ssd_d, ssd_norm_g, gm_vnorm_g, gm_ws, gm_bs, gm_out_g, w_out, ff_up, ff_conv_w, ff_conv_b, ff_down, final_g, loss_target, m_ada_w, m_ada_b, m_norm1_g, m_norm2_g, m_w_in, m_ssd_conv_w, m_ssd_conv_b, m_ssd_dt_bias, m_ssd_a_log, m_ssd_d, m_ssd_norm_g, m_gm_vnorm_g, m_gm_ws, m_gm_bs, m_gm_out_g, m_w_out, m_ff_up, m_ff_conv_w, m_ff_conv_b, m_ff_down, m_final_g, v_ada_w, v_ada_b, v_norm1_g, v_norm2_g, v_w_in, v_ssd_conv_w, v_ssd_conv_b, v_ssd_dt_bias, v_ssd_a_log, v_ssd_d, v_ssd_norm_g, v_gm_vnorm_g, v_gm_ws, v_gm_bs, v_gm_out_g, v_w_out, v_ff_up, v_ff_conv_w, v_ff_conv_b, v_ff_down, v_final_g):
    given = dict(locals())
    weights = {n: given[n] for n in WEIGHTS}
    bl, s, d = x.shape
    nl = ada_w.shape[0]
    heads = d // HEAD_DIM
    cd = d + 2 * SSD_GROUPS * STATE
    f = ff_down.shape[1] * 4
    n_in = d + cd + heads + 2 * d
    n_proj = _round_up(3 * d + cd + LANES, 2 * LANES)
    tm = _divisors(s, 512)[-1]

    gathering = {}

    def start_gathers(l, names, token):
        for n in names:
            gathering[l, n] = _gather_weights_start(f"wg{l}_{n}", [weights[n][l]], token)
            token = gathering[l, n]["token"]
        return token

    pre, _ = _gather_two_level("pre_ag", [c] + [weights[n] for n in CONVW])
    c_all = pre[0].reshape(8 * bl, d)
    conv_full = {n: jnp.concatenate([p[k, 0] for k in range(4)], axis=-1) for n, p in zip(CONVW, pre[1:])}
    mods_all, ada_token = _ada_fwd("ada", bl, c_all, ada_w, ada_b)
    mods = mods_all.reshape(nl, bl, N_MOD, 1, d)

    ready = {}

    def assembled(n, full):
        if n == "w_in":
            return full
        return full.reshape(-1, full.shape[2]) if BIG_AXIS[n] == 0 else jnp.concatenate([full[k] for k in range(4)], axis=1)

    def landed(l, n, after):
        if l > 0 and (l, n) not in ready:
            fulls, token = _gather_weights_finish(f"wg{l}", [gathering.pop((l, m)) for m in BIG], after)
            ready.update({(l, m): (assembled(m, t), token) for m, t in zip(BIG, fulls)})
        if (l, n) in ready:
            return ready.pop((l, n))
        (full,), token = _gather_weights_finish(f"wg{l}_{n}", [gathering.pop((l, n))], after)
        return assembled(n, full), token

    issued = start_gathers(0, BIG, ada_token)
    r2 = lambda v: v.reshape(1, -1)
    n_gm = d // GM_HEAD
    fg = r2(final_g)

    def layer_params(l):
        return dict(
            norm1=r2(norm1_g[l]), norm2=r2(norm2_g[l]), conv_w8=_pad_rows(conv_full["ssd_conv_w"][l], SUBLANES), conv_b=r2(ssd_conv_b[l]),
            dtb=_pad_lanes(ssd_dt_bias[l]), alog=_pad_lanes(ssd_a_log[l]), dsk=r2(jnp.repeat(ssd_d[l], HEAD_DIM)), ng=r2(ssd_norm_g[l]),
            vg=r2(gm_vnorm_g[l]), ws=gm_ws[l], bs_t=jnp.pad(gm_bs[l].T, ((0, 0), (0, LANES - n_gm))), og=r2(gm_out_g[l]),
            ff_w8=_pad_rows(conv_full["ff_conv_w"][l], SUBLANES), ff_b=r2(ff_conv_b[l]))

    def mixer_args(p):
        return (p["conv_w8"], p["conv_b"], p["dtb"], p["alog"], p["dsk"], p["ng"], p["vg"], p["ws"], p["bs_t"], p["og"])

    def padded_w_in(full):
        ns = full.shape[2]

        def cols(a, b):
            return [full[k][:, max(a, k * ns) - k * ns:min(b, (k + 1) * ns) - k * ns] for k in range(4) if max(a, k * ns) < min(b, (k + 1) * ns)]

        parts = cols(0, d) + cols(d + cd + heads, n_in) + cols(d, d + cd) + cols(d + cd, d + cd + heads)
        return jnp.concatenate(parts + [jnp.zeros((d, n_proj - (3 * d + cd + heads)), BF16)], axis=1)

    saved, xcur, pending = [], x, None
    for l in range(nl):
        p, w = layer_params(l), {}
        wi, token = landed(l, "w_in", mods_all if l == 0 else pending[0])
        w["w_in"] = padded_w_in(wi)
        sh1, sc1, g1, sh2, sc2, g2 = (mods[l, :, k] for k in range(N_MOD))
        if l + 1 < nl:
            issued = start_gathers(l + 1, BIG, token + issued)
        sc1 = _after(sc1, issued)
        if pending is None:
            (h,) = _row_fwd(f"nm{l}_fwd", _nm_tile, [(xcur, d, 0, None)], [sc1, sh1], [p["norm1"]], [(d, BF16)], 0, tm)
            x1 = xcur
        else:
            rows = [(xcur, d, 0, None), (pending[0], d, 0, None)]
            x1, h = _row_fwd(f"rnm{l}a_fwd", _rnm_tile, rows, [pending[1], sc1, sh1], [p["norm1"]], [(d, F32), (d, BF16)], 0, tm)
        proj = _mm(f"win{l}_fwd", h, w["w_in"], F32)
        ycat, (xbc, hprev) = _mixer_fwd(f"mix{l}", proj, d, cd, *mixer_args(p))
        w["w_out"], _ = landed(l, "w_out", ycat)
        mix = _mm(f"wout{l}_fwd", ycat, w["w_out"], F32)
        x2, h2 = _row_fwd(f"rnm{l}b_fwd", _rnm_tile, [(x1, d, 0, None), (mix, d, 0, None)], [g1, sc2, sh2], [p["norm2"]], [(d, F32), (d, BF16)], 0, tm)
        w_up, _ = landed(l, "ff_up", h2)
        w["ff_up"] = _gate_value_blocks(f"ffup{l}_blocks", w_up, f)
        up = _mm(f"ffup{l}_fwd", h2, w["ff_up"], F32)
        act = _conv_fwd(f"ffact{l}_fwd", up, 0, f, p["ff_w8"], p["ff_b"], FF_CONV, True, BF16)
        w["ff_down"], _ = landed(l, "ff_down", act)
        down = _mm(f"ffdown{l}_fwd", act, w["ff_down"], F32)
        saved.append(dict(p=p, w=w, x_in=xcur, pending=pending, h=h, proj=proj, xbc=xbc, hprev=hprev, ycat=ycat, x1=x1, mix=mix, h2=h2, up=up, act=act))
        xcur, pending = x2, (down, g2)
    rows = [(xcur, d, 0, F32), (pending[0], d, 0, BF16), (loss_target, d, 0, None)]

    (dx2, ddown), (dg2,), (dfg,), (loss_local,) = _row_bwd(
        "final_bwd", _final_tile, rows, [pending[1]], [fg], [], 1, tm, [jnp.ones((1, 1), F32)], emit_sums=True)
    loss = lax.psum(loss_local[0, 0], AXES)
    dmods, small_grads = [None] * nl, [None] * nl
    reducing = {}

    def start_reduce(l, named, token=None):
        by_chip = []
        for n, g in named:
            width = g.shape[1] // 4
            by_chip.append(g.reshape(4, g.shape[0] // 4, g.shape[1]) if BIG_AXIS[n] == 0 else jnp.stack([g[:, k * width:(k + 1) * width] for k in range(4)]))
        key = (l, tuple(n for n, _ in named))
        reducing[key] = _reduce_weights_start(f"wg{l}_" + "_".join(key[1]), by_chip, token)
        return reducing[key]["token"]

    for l in reversed(range(nl)):
        sv = saved[l]
        p, w = sv["p"], sv["w"]
        sh1, sc1, g1, sh2, sc2, g2 = (mods[l, :, k] for k in range(N_MOD))
        dact, dw_down = _mm_bwd(f"ffdown{l}", sv["act"], w["ff_down"], ddown)
        dup, dff_w8, dff_b = _conv_bwd(f"ffact{l}_bwd", sv["up"], 0, f, p["ff_w8"], p["ff_b"], FF_CONV, True, dact, BF16)
        dh2, dw_up = _mm_bwd(f"ffup{l}", sv["h2"], w["ff_up"], dup)
        dw_up = _gate_value_blocks(f"ffup{l}_columns", dw_up, f, inverse=True)
        sc2 = _after(sc2, start_reduce(l, [("ff_down", dw_down), ("ff_up", dw_up)]))
        rows = [(sv["x1"], d, 0, F32), (sv["mix"], d, 0, BF16)]
        (dx1, dmix), (dg1, dsc2, dsh2), (dn2,) = _row_bwd(f"rnm{l}b_bwd", _rnm_tile, rows, [g1, sc2, sh2], [p["norm2"]], [(d, F32), (d, BF16)], 0, tm, [dx2, dh2])
        dycat, dw_out = _mm_bwd(f"wout{l}", sv["ycat"], w["w_out"], dmix)
        p_tied = p
        if l == 0:
            p_tied = dict(p, dtb=_after(p["dtb"], start_reduce(l, [("w_out", dw_out)])))
        dproj, (dw8, dcb, ddtb, dalog, ddsk, dng, dvg, dws, dbs_t, dog) = _mixer_bwd(f"mix{l}", sv["proj"], d, cd, *mixer_args(p_tied), sv["xbc"], sv["hprev"], dycat)
        dh, dw_in_p = _mm_bwd(f"win{l}", sv["h"], w["w_in"], dproj)
        dw_in = jnp.concatenate([dw_in_p[:, :d], dw_in_p[:, 3 * d:3 * d + cd], dw_in_p[:, 3 * d + cd:3 * d + cd + heads], dw_in_p[:, d:3 * d]], axis=1)
        if l > 0:
            sc1 = _after(sc1, start_reduce(l, [("w_out", dw_out), ("w_in", dw_in)]))
        if sv["pending"] is None:
            (dx2,), (dsc1, dsh1), (dn1,) = _row_bwd(f"nm{l}_bwd", _nm_res_tile, [(sv["x_in"], d, 0, F32)], [sc1, sh1], [p["norm1"]], [(d, F32), (d, BF16)], 0, tm, [dx1, dh])
        else:
            rows = [(sv["x_in"], d, 0, F32), (sv["pending"][0], d, 0, BF16)]
            (dx2, ddown), (dg2_prev, dsc1, dsh1), (dn1,) = _row_bwd(
                f"rnm{l}a_bwd", _rnm_tile, rows, [sv["pending"][1], sc1, sh1], [p["norm1"]], [(d, F32), (d, BF16)], 0, tm, [dx1, dh])
        dmods[l] = jnp.concatenate([dsh1, dsc1, dg1, dsh2, dsc2, dg2], axis=1).reshape(bl, N_MOD * d)
        if sv["pending"] is not None:
            dg2 = dg2_prev
        small_grads[l] = dict(
            norm1_g=dn1.reshape(d), norm2_g=dn2.reshape(d), ssd_conv_b=dcb.reshape(cd), ssd_dt_bias=ddtb[0, :heads], ssd_a_log=dalog[0, :heads],
            ssd_d=ddsk.reshape(heads, HEAD_DIM).sum(-1), ssd_norm_g=dng.reshape(d), gm_vnorm_g=dvg.reshape(d), gm_ws=dws, gm_bs=dbs_t[:, :n_gm].T,
            gm_out_g=dog.reshape(d), ff_conv_b=dff_b.reshape(f), ssd_conv_w=dw8[:SSD_CONV], ff_conv_w=dff_w8[:FF_CONV])
    grad_x = dx2

    g_ada_w, g_ada_b, ada_token = _ada_bwd("ada", bl, c_all, ada_w.shape[2], jnp.stack(dmods))
    small_names_r = SMALL + CONVW + ("final_g",)
    summed, small_token = _allreduce_small("small", [jnp.stack([small_grads[l][n] for l in range(nl)]) for n in SMALL + CONVW] + [dfg])
    travelling = start_reduce(0, [("w_in", dw_in)], ada_token + small_token)
    grad = {"ada_w": g_ada_w, "ada_b": g_ada_b}
    grad.update(zip(small_names_r, summed))
    chip = 2 * lax.axis_index("x") + lax.axis_index("y")
    for n in CONVW:
        width = weights[n].shape[-1]
        grad[n] = lax.dynamic_slice_in_dim(grad[n], chip * width, width, axis=2)

    delta, new_m, new_v = {}, {}, {}
    last = (0, ("w_in",))
    early = [k for k in reducing if k != last]
    finished, _ = _reduce_weights_finish("wg_early", [reducing[k] for k in early], _after(summed[0], travelling))
    big_grads = dict(zip([(l, n) for l, names in early for n in names], finished))
    order = ("ada_w",) + tuple(reversed(BIG))
    for n in order:
        if n == "w_in":
            (big_grads[0, n],), _ = _reduce_weights_finish("wg_last", [reducing[last]], delta[order[-2]])
        if n != "ada_w":
            grad[n] = jnp.stack([big_grads[l, n] for l in range(nl)])
        delta[n], new_m[n], new_v[n] = _adamw("adam_" + n, weights[n], grad[n], given["m_" + n], given["v_" + n])
    small_names = ("ada_b",) + SMALL + CONVW + ("final_g",)
    as2d = lambda t: t.reshape(1, -1) if t.ndim == 1 else t
    res = _adamw_small(
        "adam_small", [as2d(weights[n]) for n in small_names], [as2d(grad[n]) for n in small_names],
        [as2d(given["m_" + n]) for n in small_names], [as2d(given["v_" + n]) for n in small_names])
    for out, vals in zip((delta, new_m, new_v), res):
        for n, val in zip(small_names, vals):
            out[n] = val.reshape(weights[n].shape)
    grad["final_g"] = grad["final_g"].reshape(final_g.shape)

    return (loss, grad_x, *[grad[n] for n in WEIGHTS], *[delta[n] for n in WEIGHTS], *[new_m[n] for n in WEIGHTS], *[new_v[n] for n in WEIGHTS])
```

```python
import jax
import jax.numpy as jnp
from jax import lax
from jax.experimental import pallas as pl
from jax.experimental.pallas import tpu as pltpu

F32 = jnp.float32
BF16 = jnp.bfloat16
EPS = 1e-6
CHUNK = 128
HEAD_DIM = 64
STATE = 128
GM_HEAD = 128
SSD_GROUPS = 2
SSD_CONV = 4
FF_CONV = 3
N_MOD = 6
LANES = 128
SUBLANES = 8
SMALL_BF16_SIZE = 1 << 16
CONV_LANES = 256
CONV_ROWS = 32
V7X_VMEM_LIMIT = 48 * 1024 * 1024
MM_VMEM_BUDGET = 30 * 1024 * 1024
MM_STEP_MACS = 2048 * 1024 * 1024
MM_ACC_SECONDS_PER_VREG = 1.4e-9
V7X_MXU_FLOPS = 996e12
V7X_HBM_BYTES_PER_S = 3.3e12
V7X_STEP_SECONDS = 0.35e-6
ADAM_LR, ADAM_B1, ADAM_B2, ADAM_EPS, ADAM_WD, ADAM_STEP = 0.001, 0.9, 0.999, 1e-08, 0.01, 10
MESH = pl.DeviceIdType.MESH
AXES = ("x", "y", "c")


def _round_up(n, m):
    return (n + m - 1) // m * m


def _divisors(n, cap, mult=LANES):
    out = [t for t in range(mult, min(n, cap) + 1, mult) if n % t == 0]
    return out or [n]


def _pcall(body, name, out_shape, grid=(), in_specs=None, out_specs=None, scratch=(), sem=None, prefetch=0):
    params = pltpu.CompilerParams(dimension_semantics=sem, vmem_limit_bytes=V7X_VMEM_LIMIT)
    if prefetch:
        spec = pltpu.PrefetchScalarGridSpec(num_scalar_prefetch=prefetch, grid=grid, in_specs=in_specs, out_specs=out_specs, scratch_shapes=list(scratch))
        return pl.pallas_call(body, name=name, out_shape=out_shape, grid_spec=spec, compiler_params=params)
    if in_specs is None:
        return pl.pallas_call(body, name=name, out_shape=out_shape, compiler_params=params)
    return pl.pallas_call(body, name=name, out_shape=out_shape, grid=grid, in_specs=in_specs, out_specs=out_specs, scratch_shapes=list(scratch), compiler_params=params)


def _sds(shape, dtype):
    return jax.ShapeDtypeStruct(tuple(shape), dtype)


def _mm_tiles(m, n, k, a_bytes, b_bytes, o_bytes):
    best, best_key = None, None
    for tm in _divisors(m, 2048):
        for tn in _divisors(n, 2560):
            for tk in _divisors(k, 2560):
                vmem = 2 * (tm * tk * a_bytes + tk * tn * b_bytes + tm * tn * o_bytes) + tm * tn * 4
                if vmem > MM_VMEM_BUDGET or tm * tn * tk > MM_STEP_MACS:
                    continue
                ni, nj, nk = m // tm, n // tn, k // tk
                a_reads = 1 if nk == 1 else nj
                b_reads = 1 if (nk == 1 and nj == 1) else ni
                hbm = a_reads * m * k * a_bytes + b_reads * k * n * b_bytes + m * n * o_bytes
                t = max(2.0 * m * n * k / V7X_MXU_FLOPS, hbm / V7X_HBM_BYTES_PER_S) + ni * nj * nk * V7X_STEP_SECONDS
                if nk > 1:
                    t += ni * nj * nk * (tm * tn // (SUBLANES * LANES)) * MM_ACC_SECONDS_PER_VREG
                key = (-t, tm * tn * tk)
                if best_key is None or key > best_key:
                    best, best_key = (tm, tn, tk), key
    return best


def _matmul(name, a, b, mode, out_dtype):
    if mode == "nn":
        (m, k), n = a.shape, b.shape[1]
    elif mode == "nt":
        (m, k), n = a.shape, b.shape[0]
    else:
        (k, m), n = a.shape, b.shape[1]
    tm, tn, tk = _mm_tiles(m, n, k, a.dtype.itemsize, b.dtype.itemsize, jnp.dtype(out_dtype).itemsize)
    nk = k // tk
    if mode == "nn":
        a_spec = pl.BlockSpec((tm, tk), lambda i, j, kk: (i, kk))
        b_spec = pl.BlockSpec((tk, tn), lambda i, j, kk: (kk, j))
        dims = ((1,), (0,))
    elif mode == "nt":
        a_spec = pl.BlockSpec((tm, tk), lambda i, j, kk: (i, kk))
        b_spec = pl.BlockSpec((tn, tk), lambda i, j, kk: (j, kk))
        dims = ((1,), (1,))
    else:
        a_spec = pl.BlockSpec((tk, tm), lambda i, j, kk: (kk, i))
        b_spec = pl.BlockSpec((tk, tn), lambda i, j, kk: (kk, j))
        dims = ((0,), (0,))

    def body(a_ref, b_ref, o_ref, acc_ref):
        kk = pl.program_id(2)
        p = lax.dot_general(a_ref[...].astype(BF16), b_ref[...].astype(BF16), (dims, ((), ())), preferred_element_type=F32)
        if nk == 1:
            o_ref[...] = p.astype(o_ref.dtype)
        else:
            @pl.when(kk == 0)
            def _():
                acc_ref[...] = p

            @pl.when(kk > 0)
            def _():
                acc_ref[...] += p

            @pl.when(kk == nk - 1)
            def _():
                o_ref[...] = acc_ref[...].astype(o_ref.dtype)

    return _pcall(
        body, name, _sds((m, n), out_dtype), grid=(m // tm, n // tn, nk), in_specs=[a_spec, b_spec],
        out_specs=pl.BlockSpec((tm, tn), lambda i, j, kk: (i, j)), scratch=[pltpu.VMEM((tm, tn), F32)],
        sem=("parallel", "parallel", "arbitrary"),
    )(a, b)


def _mm(name, a, w, out_dtype):
    return _matmul(name, a.reshape(-1, a.shape[-1]), w, "nn", out_dtype).reshape(a.shape[:-1] + (w.shape[1],))


def _mm_bwd(name, a, w, dy):
    a2, dy2 = a.reshape(-1, a.shape[-1]), dy.reshape(-1, dy.shape[-1])
    return _matmul(name + "_dx", dy2, w, "nt", BF16).reshape(a.shape), _matmul(name + "_dw", a2, dy2, "tn", BF16)


def _dg(a, b, dims):
    return lax.dot_general(a.astype(BF16), b.astype(BF16), (dims, ((), ())), preferred_element_type=F32)


@jax.custom_vjp
def _dot_nn(a, b):
    return _dg(a, b, ((1,), (0,)))


_dot_nn.defvjp(lambda a, b: (_dot_nn(a, b), (a, b)), lambda r, d: (_dg(d, r[1], ((1,), (1,))), _dg(r[0], d, ((0,), (0,)))))


@jax.custom_vjp
def _dot_nt(a, b):
    return _dg(a, b, ((1,), (1,)))


_dot_nt.defvjp(lambda a, b: (_dot_nt(a, b), (a, b)), lambda r, d: (_dg(d, r[1], ((1,), (0,))), _dg(d, r[0], ((0,), (0,)))))


@jax.custom_vjp
def _dot_tn(a, b):
    return _dg(a, b, ((0,), (0,)))


_dot_tn.defvjp(lambda a, b: (_dot_tn(a, b), (a, b)), lambda r, d: (_dg(r[1], d, ((1,), (1,))), _dg(r[0], d, ((1,), (0,)))))


def _exact_dot(a, c, dims):
    hi = a.astype(BF16)
    r1 = a - hi.astype(F32)
    mid = r1.astype(BF16)
    lo = (r1 - mid.astype(F32)).astype(BF16)
    cb = c.astype(BF16)
    f = lambda t: lax.dot_general(t, cb, (dims, ((), ())), preferred_element_type=F32)
    return f(hi) + f(mid) + f(lo)


@jax.custom_vjp
def _sel_right(a, c):
    return _exact_dot(a, c, ((1,), (0,)))


_sel_right.defvjp(lambda a, c: (_sel_right(a, c), c), lambda c, d: (_exact_dot(d, c, ((1,), (1,))), jnp.zeros_like(c)))


def _exact_dot_left(c, a, dims):
    hi = a.astype(BF16)
    r1 = a - hi.astype(F32)
    mid = r1.astype(BF16)
    lo = (r1 - mid.astype(F32)).astype(BF16)
    cb = c.astype(BF16)
    f = lambda t: lax.dot_general(cb, t, (dims, ((), ())), preferred_element_type=F32)
    return f(hi) + f(mid) + f(lo)


@jax.custom_vjp
def _sel_left(c, a):
    return _exact_dot_left(c, a, ((1,), (0,)))


_sel_left.defvjp(lambda c, a: (_sel_left(c, a), c), lambda c, d: (jnp.zeros_like(c), _exact_dot_left(c, d, ((0,), (0,)))))


def _sigmoid(x):
    return 0.5 * jnp.tanh(0.5 * x) + 0.5


def _rms(x, g):
    return x * lax.rsqrt(jnp.mean(x * x, axis=-1, keepdims=True) + EPS) * g


def _gelu(x):
    return 0.5 * x * (1.0 + lax.erf(x * (2.0 ** -0.5)))


def _causal(n):
    return lax.broadcasted_iota(jnp.int32, (n, n), 0) >= lax.broadcasted_iota(jnp.int32, (n, n), 1)


def _row_in_specs(rows, bparams, gparams, tm):
    specs = [pl.BlockSpec((1, tm, w), lambda b, i, cb=cb: (b, i, cb)) for (_, w, cb, _) in rows]
    specs += [pl.BlockSpec((1, 1, p.shape[-1]), lambda b, i: (b, 0, 0)) for p in bparams]
    specs += [pl.BlockSpec(p.shape, lambda b, i, n=p.ndim: (0,) * n) for p in gparams]
    return specs


def _row_vals(refs, n_rows, n_b, n_g):
    vals = [r[0].astype(F32) for r in refs[:n_rows]]
    vals += [r[0].astype(F32) for r in refs[n_rows:n_rows + n_b]]
    vals += [r[...].astype(F32) for r in refs[n_rows + n_b:n_rows + n_b + n_g]]
    return vals


def _row_fwd(name, tile, rows, bparams, gparams, outs, n_sum, tm):
    bl, s = rows[0][0].shape[:2]
    n_in = len(rows) + len(bparams) + len(gparams)

    def body(*refs):
        first = (pl.program_id(0) == 0) & (pl.program_id(1) == 0)
        res = tile(*_row_vals(refs, len(rows), len(bparams), len(gparams)))
        o_refs = refs[n_in:]
        for k in range(len(outs)):
            o_refs[k][0] = res[k].astype(o_refs[k].dtype)
        for k in range(n_sum):
            o_ref, val = o_refs[len(outs) + k], res[len(outs) + k]

            @pl.when(first)
            def _(o_ref=o_ref, val=val):
                o_ref[...] = val

            @pl.when(jnp.logical_not(first))
            def _(o_ref=o_ref, val=val):
                o_ref[...] += val

    out_shape = [_sds((bl, s, w), dt) for (w, dt) in outs] + [_sds((1, 1), F32)] * n_sum
    out_specs = [pl.BlockSpec((1, tm, w), lambda b, i: (b, i, 0)) for (w, _) in outs] + [pl.BlockSpec((1, 1), lambda b, i: (0, 0))] * n_sum
    return _pcall(
        body, name, out_shape, grid=(bl, s // tm), in_specs=_row_in_specs(rows, bparams, gparams, tm), out_specs=out_specs,
        sem=("arbitrary", "arbitrary"),
    )(*[r[0] for r in rows], *bparams, *gparams)


def _row_bwd(name, tile, rows, bparams, gparams, outs, n_sum, tm, cts, emit_sums=False):
    bl, s = rows[0][0].shape[:2]
    n_r, n_b, n_g = len(rows), len(bparams), len(gparams)
    n_extra = n_sum if emit_sums else 0
    n_in = n_r + n_b + n_g
    n_ct = len(outs) + n_sum
    grad_rows = [k for k in range(n_r) if rows[k][3]]

    def body(*refs):
        b, i = pl.program_id(0), pl.program_id(1)
        vals = _row_vals(refs, n_r, n_b, n_g)
        ct_refs = refs[n_in:n_in + n_ct]
        ct = [r[0].astype(F32) for r in ct_refs[:len(outs)]] + [r[...] for r in ct_refs[len(outs):]]
        primal, vjp = jax.vjp(tile, *vals)
        grads = tuple(vjp(tuple(ct))) + (tuple(primal[len(outs):]) if emit_sums else ())
        o_refs = refs[n_in + n_ct:]
        for j, k in enumerate(grad_rows):
            o_refs[j][0] = grads[k].astype(o_refs[j].dtype)
        for k in range(n_b):
            o_ref, val = o_refs[len(grad_rows) + k], grads[n_r + k]

            @pl.when(i == 0)
            def _(o_ref=o_ref, val=val):
                o_ref[0] = val

            @pl.when(i > 0)
            def _(o_ref=o_ref, val=val):
                o_ref[0] += val

        first = (b == 0) & (i == 0)
        for k in range(n_g + n_extra):
            o_ref, val = o_refs[len(grad_rows) + n_b + k], grads[n_r + n_b + k]

            @pl.when(first)
            def _(o_ref=o_ref, val=val):
                o_ref[...] = val

            @pl.when(jnp.logical_not(first))
            def _(o_ref=o_ref, val=val):
                o_ref[...] += val

    one = pl.BlockSpec((1, 1), lambda b, i: (0, 0))
    in_specs = _row_in_specs(rows, bparams, gparams, tm)
    in_specs += [pl.BlockSpec((1, tm, w), lambda b, i: (b, i, 0)) for (w, _) in outs] + [one] * n_sum
    out_shape = [_sds((bl, s, rows[k][1]), rows[k][3]) for k in grad_rows]
    out_shape += [_sds(p.shape, F32) for p in bparams] + [_sds(p.shape, F32) for p in gparams] + [_sds((1, 1), F32)] * n_extra
    out_specs = [pl.BlockSpec((1, tm, rows[k][1]), lambda b, i: (b, i, 0)) for k in grad_rows]
    out_specs += [pl.BlockSpec((1, 1, p.shape[-1]), lambda b, i: (b, 0, 0)) for p in bparams]
    out_specs += [pl.BlockSpec(p.shape, lambda b, i, n=p.ndim: (0,) * n) for p in gparams] + [one] * n_extra
    res = _pcall(
        body, name, out_shape, grid=(bl, s // tm), in_specs=in_specs, out_specs=out_specs, sem=("arbitrary", "arbitrary"),
    )(*[r[0] for r in rows], *bparams, *gparams, *cts)
    n0, n1, n2 = len(grad_rows), len(grad_rows) + n_b, len(grad_rows) + n_b + n_g
    return (res[:n0], res[n0:n1], res[n1:n2]) + ((res[n2:],) if emit_sums else ())


def _nm_tile(x, sc, sh, g):
    return (_rms(x, g) * (1.0 + sc) + sh,)


def _nm_res_tile(x, sc, sh, g):
    return x, _rms(x, g) * (1.0 + sc) + sh


def _rnm_tile(x, o, gate, sc, sh, g):
    xn = x + gate * o
    return xn, _rms(xn, g) * (1.0 + sc) + sh


def _final_tile(x, o, tgt, gate, g):
    e = _rms(x + gate * o, g) - tgt
    return (0.5 * jnp.sum(jnp.mean(e * e, axis=-1, keepdims=True), axis=0, keepdims=True),)


def _gm_tile(u_in, v_in, vg, ws, bs_t, og):
    d = u_in.shape[1]
    u, vn = _gelu(u_in), _rms(_gelu(v_in), vg)
    causal = _causal(CHUNK)
    lane = lax.broadcasted_iota(jnp.int32, (1, LANES), 1)
    parts = []
    for h in range(d // GM_HEAD):
        bias = jnp.sum(bs_t * (lane == h).astype(F32), axis=1, keepdims=True)
        parts.append(_dot_nn(jnp.where(causal, ws[h], 0.0), vn[:, h * GM_HEAD:(h + 1) * GM_HEAD]) + bias)
    return (_rms(u * jnp.concatenate(parts, axis=1), og),)


def _conv_window(ref, r0, rows, c0, tc, seq, before, after):
    parts = []
    if before:
        p0 = pl.multiple_of(jnp.maximum(r0 - SUBLANES, 0), SUBLANES)
        parts.append(jnp.where(r0 > 0, ref[0, pl.ds(p0, SUBLANES), pl.ds(c0, tc)].astype(F32), 0.0))
    parts.append(ref[0, pl.ds(r0, rows), pl.ds(c0, tc)].astype(F32))
    if after:
        n0 = pl.multiple_of(jnp.minimum(r0 + rows, seq - SUBLANES), SUBLANES)
        parts.append(jnp.where(r0 + rows < seq, ref[0, pl.ds(n0, SUBLANES), pl.ds(c0, tc)].astype(F32), 0.0))
    return jnp.concatenate(parts, axis=0) if len(parts) > 1 else parts[0]


def _conv_taps_pre(xe, w, b, taps, rows):
    pre = xe[SUBLANES:SUBLANES + rows] * w[taps - 1] + b
    for j in range(1, taps):
        pre = pre + pltpu.roll(xe, j, 0)[SUBLANES:SUBLANES + rows] * w[taps - 1 - j]
    return pre


def _conv_fwd(name, src, col0, chans, w8, b, taps, gated, out_dtype):
    bl, s, _ = src.shape
    tc = CONV_LANES
    xw = 2 * tc if gated else tc

    def body(x_ref, w_ref, b_ref, o_ref):
        w = [w_ref[k:k + 1, :] for k in range(taps)]
        bias = b_ref[...]

        def step(c, carry):
            r0 = pl.multiple_of(c * CONV_ROWS, CONV_ROWS)
            xe = _conv_window(x_ref, r0, CONV_ROWS, 0, tc, s, True, False)
            pre = _conv_taps_pre(xe, w, bias, taps, CONV_ROWS)
            y = pre * _sigmoid(pre)
            if gated:
                y = y * x_ref[0, pl.ds(r0, CONV_ROWS), pl.ds(tc, tc)]
            o_ref[0, pl.ds(r0, CONV_ROWS), :] = y.astype(o_ref.dtype)
            return carry

        lax.fori_loop(0, s // CONV_ROWS, step, 0, unroll=4)

    first = 0 if gated else col0 // tc
    return _pcall(
        body, name, _sds((bl, s, chans), out_dtype), grid=(chans // tc, bl),
        in_specs=[pl.BlockSpec((1, s, xw), lambda j, bb: (bb, 0, first + j)), pl.BlockSpec((SUBLANES, tc), lambda j, bb: (0, j)), pl.BlockSpec((1, tc), lambda j, bb: (0, j))],
        out_specs=pl.BlockSpec((1, s, tc), lambda j, bb: (bb, 0, j)), sem=("parallel", "arbitrary"),
    )(src, w8, b)


def _conv_bwd(name, src, col0, chans, w8, b, taps, gated, dy, dx_dtype):
    bl, s, _ = src.shape
    tc = CONV_LANES
    xw = 2 * tc if gated else tc
    ext = CONV_ROWS + SUBLANES
    pieces = list(dy) if isinstance(dy, (list, tuple)) else [dy]
    starts = [sum(p.shape[2] for p in pieces[:k]) // tc for k in range(len(pieces))]

    def body(x_ref, w_ref, b_ref, *rest):
        dy_refs, (dx_ref, dw_ref, db_ref) = rest[:len(pieces)], rest[len(pieces):]
        j, bb = pl.program_id(0), pl.program_id(1)
        w = [w_ref[k:k + 1, :] for k in range(taps)]
        bias = b_ref[...]

        def dy_window(r0):
            d = _conv_window(dy_refs[0], r0, CONV_ROWS, 0, tc, s, False, True)
            for k in range(1, len(pieces)):
                d = jnp.where(j >= starts[k], _conv_window(dy_refs[k], r0, CONV_ROWS, 0, tc, s, False, True), d)
            return d

        def fold(v):
            acc = v[0:SUBLANES]
            for i in range(1, CONV_ROWS // SUBLANES):
                acc = acc + v[i * SUBLANES:(i + 1) * SUBLANES]
            return acc

        def step(c, carry):
            r0 = pl.multiple_of(c * CONV_ROWS, CONV_ROWS)
            xe = _conv_window(x_ref, r0, CONV_ROWS, 0, tc, s, True, True)
            pre = _conv_taps_pre(xe, w, bias, taps, ext)
            sig = _sigmoid(pre)
            d = dy_window(r0)
            dsil = d * _conv_window(x_ref, r0, CONV_ROWS, tc, tc, s, False, True) if gated else d
            dpre = dsil * sig * (1.0 + pre * (1.0 - sig))
            dx = dpre[:CONV_ROWS] * w[taps - 1]
            for j in range(1, taps):
                dx = dx + pltpu.roll(dpre, ext - j, 0)[:CONV_ROWS] * w[taps - 1 - j]
            dx_ref[0, pl.ds(r0, CONV_ROWS), pl.ds(0, tc)] = dx.astype(dx_ref.dtype)
            if gated:
                dx_ref[0, pl.ds(r0, CONV_ROWS), pl.ds(tc, tc)] = (d[:CONV_ROWS] * (pre * sig)[:CONV_ROWS]).astype(dx_ref.dtype)
            here = dpre[:CONV_ROWS]
            sums = [fold(here * (pltpu.roll(xe, taps - 1 - k, 0) if k < taps - 1 else xe)[SUBLANES:ext]) + carry[k] for k in range(taps)]
            return tuple(sums) + (fold(here) + carry[taps],)

        zero = jnp.zeros((SUBLANES, tc), F32)
        sums = lax.fori_loop(0, s // CONV_ROWS, step, (zero,) * (taps + 1), unroll=2)
        rows = [jnp.sum(t, axis=0, keepdims=True) for t in sums]
        dw = jnp.concatenate(rows[:taps] + [jnp.zeros_like(rows[0])] * (SUBLANES - taps), axis=0)

        @pl.when(bb == 0)
        def _():
            dw_ref[...] = dw
            db_ref[...] = rows[taps]

        @pl.when(bb > 0)
        def _():
            dw_ref[...] += dw
            db_ref[...] += rows[taps]

    first = 0 if gated else col0 // tc
    out_shape = [_sds((bl, s, chans * (2 if gated else 1)), dx_dtype), _sds((SUBLANES, chans), F32), _sds((1, chans), F32)]
    dy_specs = [
        pl.BlockSpec((1, s, tc), lambda j, bb, k=k: (bb, 0, jnp.clip(j - starts[k], 0, pieces[k].shape[2] // tc - 1)))
        for k in range(len(pieces))
    ]
    return _pcall(
        body, name, out_shape, grid=(chans // tc, bl),
        in_specs=[
            pl.BlockSpec((1, s, xw), lambda j, bb: (bb, 0, first + j)), pl.BlockSpec((SUBLANES, tc), lambda j, bb: (0, j)),
            pl.BlockSpec((1, tc), lambda j, bb: (0, j)), *dy_specs,
        ],
        out_specs=[pl.BlockSpec((1, s, xw), lambda j, bb: (bb, 0, j)), pl.BlockSpec((SUBLANES, tc), lambda j, bb: (0, j)), pl.BlockSpec((1, tc), lambda j, bb: (0, j))],
        sem=("parallel", "arbitrary"),
    )(src, w8, b, *pieces)


def _gate_value_blocks(name, w, f, inverse=False):
    nb = f // CONV_LANES
    src = (lambda j: (0, 2 * (j % nb) + j // nb)) if inverse else (lambda j: (0, (j % 2) * nb + j // 2))

    def body(x_ref, o_ref):
        o_ref[...] = x_ref[...]

    return _pcall(
        body, name, _sds(w.shape, w.dtype), grid=(2 * nb,), in_specs=[pl.BlockSpec((w.shape[0], CONV_LANES), src)],
        out_specs=pl.BlockSpec((w.shape[0], CONV_LANES), lambda j: (0, j)), sem=("parallel",),
    )(w)


def _ssd_step(states, xs, bm, cm, dtr, z, dtb, alog, dsk, ng):
    gw = xs.shape[1] // SSD_GROUPS
    hpg, npair = gw // HEAD_DIM, gw // LANES
    dt = jax.nn.softplus(dtr + dtb)
    da = dt * (-jnp.exp(alog))
    causal = _causal(CHUNK)
    acs = _sel_left(causal.astype(F32), da)
    acs_t = acs.T
    lane = lax.broadcasted_iota(jnp.int32, (1, LANES), 1)
    sub = lax.broadcasted_iota(jnp.int32, (LANES, 1), 0)
    last = lax.broadcasted_iota(jnp.int32, (CHUNK, gw), 0) == CHUNK - 1
    outs, new_states = [], []
    for g in range(SSD_GROUPS):
        xg, zg = xs[:, g * gw:(g + 1) * gw], z[:, g * gw:(g + 1) * gw]
        bg, cg = bm[:, g * STATE:(g + 1) * STATE], cm[:, g * STATE:(g + 1) * STATE]
        head_of_col = g * hpg + lax.broadcasted_iota(jnp.int32, (LANES, gw), 1) // HEAD_DIM
        expand = (lax.broadcasted_iota(jnp.int32, (LANES, gw), 0) == head_of_col).astype(F32)
        dt_e, acs_e = _sel_right(dt, expand), _sel_right(acs, expand)
        alast_e = jnp.sum(jnp.where(last, acs_e, 0.0), axis=0, keepdims=True)
        xc = xg * dt_e
        xc_st = xc * jnp.exp(alast_e - acs_e)
        decay_out, chunk_decay = jnp.exp(acs_e), jnp.exp(alast_e)
        cb = _dot_nt(cg, bg)
        ys = []
        for p in range(npair):
            sl = slice(p * LANES, (p + 1) * LANES)
            state = states[g * npair + p]
            y = _dot_nn(cg, state) * decay_out[:, sl]
            for q in range(2):
                head = g * hpg + 2 * p + q
                col = jnp.sum(acs * (lane == head).astype(F32), axis=1, keepdims=True)
                row = jnp.sum(acs_t * (sub == head).astype(F32), axis=0, keepdims=True)
                decay = jnp.where(causal, jnp.exp(jnp.where(causal, col - row, 0.0)), 0.0)
                y = y + _dot_nn(cb * decay, xc[:, sl] * ((lane // HEAD_DIM) == q).astype(F32))
            ys.append(y)
            new_states.append(state * chunk_decay[:, sl] + _dot_tn(bg, xc_st[:, sl]))
        y = jnp.concatenate(ys, axis=1) + dsk[:, g * gw:(g + 1) * gw] * xg
        outs.append(_rms(y * (zg * _sigmoid(zg)), ng[:, g * gw:(g + 1) * gw]))
    return tuple(new_states), jnp.concatenate(outs, axis=1)


def _ssd_scan_specs(d, dt_col, nc, rev):
    ci = (lambda i: nc - 1 - i) if rev else (lambda i: i)
    bc = SSD_GROUPS * STATE
    return [
        pl.BlockSpec((1, CHUNK, d), lambda b, i: (b, ci(i), 0)),
        pl.BlockSpec((1, CHUNK, bc), lambda b, i: (b, ci(i), d // bc)),
        pl.BlockSpec((1, CHUNK, bc), lambda b, i: (b, ci(i), d // bc + 1)),
        pl.BlockSpec((1, CHUNK, LANES), lambda b, i: (b, ci(i), dt_col // LANES)),
        pl.BlockSpec((1, CHUNK, d), lambda b, i: (b, ci(i), 0)),
        pl.BlockSpec((1, LANES), lambda b, i: (0, 0)), pl.BlockSpec((1, LANES), lambda b, i: (0, 0)),
        pl.BlockSpec((1, d), lambda b, i: (0, 0)), pl.BlockSpec((1, d), lambda b, i: (0, 0)),
    ]


def _ssd_scan_fwd(name, xbc, proj, dt_col, dtb, alog, dsk, ng):
    bl, s, _ = xbc.shape
    d = dsk.shape[1]
    nc, nst = s // CHUNK, d // LANES

    def body(xs_ref, bm_ref, cm_ref, dt_ref, z_ref, dtb_ref, alog_ref, dsk_ref, ng_ref, y_ref, hp_ref, st_ref):
        @pl.when(pl.program_id(1) == 0)
        def _():
            st_ref[...] = jnp.zeros_like(st_ref)

        states = tuple(st_ref[p] for p in range(nst))
        hp_ref[0, 0] = st_ref[...]
        new_states, yn = _ssd_step(states, xs_ref[0], bm_ref[0], cm_ref[0], dt_ref[0], z_ref[0], dtb_ref[...], alog_ref[...], dsk_ref[...], ng_ref[...])
        for p in range(nst):
            st_ref[p] = new_states[p]
        y_ref[0] = yn.astype(y_ref.dtype)

    return _pcall(
        body, name, [_sds((bl, s, d), BF16), _sds((bl, nc, nst, STATE, LANES), F32)], grid=(bl, nc), in_specs=_ssd_scan_specs(d, dt_col, nc, False),
        out_specs=[pl.BlockSpec((1, CHUNK, d), lambda b, i: (b, i, 0)), pl.BlockSpec((1, 1, nst, STATE, LANES), lambda b, i: (b, i, 0, 0, 0))],
        scratch=[pltpu.VMEM((nst, STATE, LANES), F32)], sem=("arbitrary", "arbitrary"),
    )(xbc, xbc, xbc, proj, proj, dtb, alog, dsk, ng)


def _ssd_scan_bwd(name, xbc, proj, dt_col, dtb, alog, dsk, ng, hprev, dy):
    bl, s, _ = xbc.shape
    d = dsk.shape[1]
    nc, nst, bc = s // CHUNK, d // LANES, SSD_GROUPS * STATE

    def body(xs_ref, bm_ref, cm_ref, dt_ref, z_ref, dtb_ref, alog_ref, dsk_ref, ng_ref, hp_ref, dy_ref,
             dxs_ref, dbm_ref, dcm_ref, ddt_ref, dz_ref, ddtb_ref, dalog_ref, ddsk_ref, dng_ref, dst_ref):
        b, i = pl.program_id(0), pl.program_id(1)

        @pl.when(i == 0)
        def _():
            dst_ref[...] = jnp.zeros_like(dst_ref)

        states = tuple(hp_ref[0, 0, p] for p in range(nst))
        _, vjp = jax.vjp(_ssd_step, states, xs_ref[0], bm_ref[0], cm_ref[0], dt_ref[0], z_ref[0], dtb_ref[...], alog_ref[...], dsk_ref[...], ng_ref[...])
        d_states, dxs, dbm, dcm, ddt, dz, ddtb, dalog, ddsk, dng = vjp((tuple(dst_ref[p] for p in range(nst)), dy_ref[0].astype(F32)))
        for p in range(nst):
            dst_ref[p] = d_states[p]
        dxs_ref[0], dbm_ref[0], dcm_ref[0], ddt_ref[0] = dxs, dbm, dcm, ddt
        dz_ref[0] = dz.astype(dz_ref.dtype)
        first = (b == 0) & (i == 0)
        for o_ref, val in ((ddtb_ref, ddtb), (dalog_ref, dalog), (ddsk_ref, ddsk), (dng_ref, dng)):
            @pl.when(first)
            def _(o_ref=o_ref, val=val):
                o_ref[...] = val

            @pl.when(jnp.logical_not(first))
            def _(o_ref=o_ref, val=val):
                o_ref[...] += val

    rc = lambda i: nc - 1 - i
    in_specs = _ssd_scan_specs(d, dt_col, nc, True) + [
        pl.BlockSpec((1, 1, nst, STATE, LANES), lambda b, i: (b, rc(i), 0, 0, 0)), pl.BlockSpec((1, CHUNK, d), lambda b, i: (b, rc(i), 0)),
    ]
    out_shape = [
        _sds((bl, s, d), F32), _sds((bl, s, bc), F32), _sds((bl, s, bc), F32), _sds((bl, s, LANES), F32), _sds((bl, s, d), BF16),
        _sds((1, LANES), F32), _sds((1, LANES), F32), _sds((1, d), F32), _sds((1, d), F32),
    ]
    row = lambda w: pl.BlockSpec((1, CHUNK, w), lambda b, i: (b, rc(i), 0))
    whole = lambda w: pl.BlockSpec((1, w), lambda b, i: (0, 0))
    out_specs = [row(d), row(bc), row(bc), row(LANES), row(d), whole(LANES), whole(LANES), whole(d), whole(d)]
    return _pcall(
        body, name, out_shape, grid=(bl, nc), in_specs=in_specs, out_specs=out_specs,
        scratch=[pltpu.VMEM((nst, STATE, LANES), F32)], sem=("arbitrary", "arbitrary"),
    )(xbc, xbc, xbc, proj, proj, dtb, alog, dsk, ng, hprev, dy)


def _mixer_fwd(name, proj, d, cd, conv_w8, conv_b, dtb, alog, dsk, ng, vg, ws, bs_t, og):
    xbc = _conv_fwd(name + "_conv_fwd", proj, 3 * d, cd, conv_w8, conv_b, SSD_CONV, False, F32)
    y, hprev = _ssd_scan_fwd(name + "_ssd_fwd", xbc, proj, 3 * d + cd, dtb, alog, dsk, ng)
    rows = [(proj, d, 1, BF16), (proj, d, 2, BF16)]
    (g_out,) = _row_fwd(name + "_gm_fwd", _gm_tile, rows, [], [vg, ws, bs_t, og], [(d, BF16)], 0, CHUNK)
    return jnp.concatenate([y, g_out], axis=-1), (xbc, hprev)


def _mixer_bwd(name, proj, d, cd, conv_w8, conv_b, dtb, alog, dsk, ng, vg, ws, bs_t, og, xbc, hprev, dycat):
    bl, s, n_proj = proj.shape
    dt_col = 3 * d + cd
    dy, dg_out = dycat[..., :d], dycat[..., d:]
    dxs, dbm, dcm, ddt, dz, ddtb, dalog, ddsk, dng = _ssd_scan_bwd(name + "_ssd_bwd", xbc, proj, dt_col, dtb, alog, dsk, ng, hprev, dy)
    dxbc, dw8, dcb = _conv_bwd(name + "_conv_bwd", proj, 3 * d, cd, conv_w8, conv_b, SSD_CONV, False, [dxs, dbm, dcm], BF16)
    rows = [(proj, d, 1, BF16), (proj, d, 2, BF16)]
    (du, dv), _, (dvg, dws, dbs_t, dog) = _row_bwd(name + "_gm_bwd", _gm_tile, rows, [], [vg, ws, bs_t, og], [(d, BF16)], 0, CHUNK, [dg_out])
    pad = jnp.zeros((bl, s, n_proj - dt_col - LANES), BF16)
    dproj = jnp.concatenate([dz, du, dv, dxbc, ddt.astype(BF16), pad], axis=-1)
    return dproj, (dw8, dcb, ddtb, dalog, ddsk, dng, dvg, dws, dbs_t, dog)


def _position():
    return lax.axis_index("x"), lax.axis_index("y"), lax.axis_index("c")


def _at(ref, idx):
    return ref.at[idx] if len(idx) else ref


def _exchange(name, inputs, out_shapes, plan, inplace=False, after=None):
    if inplace:
        out_shapes = [(a.shape, a.dtype) for a in inputs]
    n_in, n_out = len(inputs), len(out_shapes)
    n_after = 0 if after is None else 1
    n_copy = len(plan(0, 0, 0))

    def body(*refs):
        in_refs, out_refs, token = refs[:n_in], refs[n_in + n_after:n_in + n_after + n_out], refs[n_in + n_after + n_out]
        send_sems, recv_sems = refs[n_in + n_after + n_out + 1:]
        token[...] = jnp.zeros_like(token)
        x, y, c = _position()
        copies = plan(x, y, c)

        def copy(k, src, dst, peer):
            return pltpu.make_async_remote_copy(src_ref=src, dst_ref=dst, send_sem=send_sems.at[k], recv_sem=recv_sems.at[k], device_id=peer, device_id_type=MESH)

        src_refs = out_refs if inplace else in_refs
        sends = [copy(k, _at(src_refs[sa], si), _at(out_refs[da], di), peer) for k, (sa, si, da, di, peer, _) in enumerate(copies)]
        for cp in sends:
            cp.start()
        for k, (sa, si, da, _, peer, li) in enumerate(copies):
            copy(k, _at(src_refs[sa], si), _at(out_refs[da], li), peer).wait_recv()
        for cp in sends:
            cp.wait_send()

    any_spec = pl.BlockSpec(memory_space=pl.ANY)
    outs = pl.pallas_call(
        body, name=name, out_shape=[_sds(s, dt) for s, dt in out_shapes] + [_sds((SUBLANES, LANES), F32)], in_specs=[any_spec] * (n_in + n_after),
        out_specs=[any_spec] * n_out + [pl.BlockSpec(memory_space=pltpu.VMEM)],
        scratch_shapes=[pltpu.SemaphoreType.DMA((n_copy,)), pltpu.SemaphoreType.DMA((n_copy,))],
        input_output_aliases={i: i for i in range(n_in)} if inplace else {},
    )(*inputs, *([] if after is None else [after]))
    return list(outs[:n_out]), outs[n_out]


def _exchange_start(name, inputs, out_shapes, plan):
    n_in, n_out = len(inputs), len(out_shapes)
    n_copy = len(plan(0, 0, 0))

    def body(*refs):
        in_refs, land_refs = refs[:n_in], refs[n_in:n_in + n_out]
        send_sems, recv_sems = refs[n_in + n_out:n_in + n_out + 2]
        token = refs[-1]
        x, y, c = _position()
        for k, (sa, si, da, di, peer, _) in enumerate(plan(x, y, c)):
            pltpu.make_async_remote_copy(
                src_ref=_at(in_refs[sa], si), dst_ref=_at(land_refs[da], di), send_sem=send_sems.at[k], recv_sem=recv_sems.at[k],
                device_id=peer, device_id_type=MESH).start()
        token[...] = jnp.zeros_like(token)

    hbm, sem = pl.BlockSpec(memory_space=pltpu.HBM), pl.BlockSpec(memory_space=pltpu.SEMAPHORE)
    lands = [lax.empty(s, dt) for s, dt in out_shapes]
    args = [pltpu.with_memory_space_constraint(a, pltpu.HBM) for a in list(inputs) + lands]
    outs = pl.pallas_call(
        body, name=name,
        out_shape=(pltpu.SemaphoreType.DMA((n_copy,)), pltpu.SemaphoreType.DMA((n_copy,)), *[pltpu.HBM(a.shape, a.dtype) for a in args], _sds((SUBLANES, LANES), F32)),
        in_specs=[hbm] * (n_in + n_out), out_specs=(sem, sem, *[hbm] * (n_in + n_out), pl.BlockSpec(memory_space=pltpu.VMEM)),
        input_output_aliases={i: 2 + i for i in range(n_in + n_out)},
        compiler_params=pltpu.CompilerParams(has_side_effects=pltpu.SideEffectType.DATAFLOW_SIDE_EFFECTING),
    )(*args)
    return dict(name=name, plan=plan, sems=outs[:2], ins=list(outs[2:2 + n_in]), lands=list(outs[2 + n_in:2 + n_in + n_out]), token=outs[-1])


def _exchange_wait(started, after):
    plan, n_in, n_out = started["plan"], len(started["ins"]), len(started["lands"])

    def body(*refs):
        in_refs, land_refs = refs[:n_in], refs[n_in:n_in + n_out]
        send_sems, recv_sems = refs[n_in + n_out:n_in + n_out + 2]
        token = refs[-1]
        x, y, c = _position()
        for k, (sa, si, da, _, peer, li) in enumerate(plan(x, y, c)):
            cp = pltpu.make_async_remote_copy(
                src_ref=_at(in_refs[sa], si), dst_ref=_at(land_refs[da], li), send_sem=send_sems.at[k], recv_sem=recv_sems.at[k],
                device_id=peer, device_id_type=MESH)
            cp.wait_send()
            cp.wait_recv()
        token[...] = jnp.zeros_like(token)

    hbm, sem = pl.BlockSpec(memory_space=pltpu.HBM), pl.BlockSpec(memory_space=pltpu.SEMAPHORE)
    bufs = started["ins"] + started["lands"]
    outs = pl.pallas_call(
        body, name=started["name"] + "_wait", out_shape=(*[pltpu.HBM(a.shape, a.dtype) for a in bufs], _sds((SUBLANES, LANES), F32)),
        in_specs=[hbm] * len(bufs) + [sem, sem, pl.BlockSpec(memory_space=pl.ANY)], out_specs=(*[hbm] * len(bufs), pl.BlockSpec(memory_space=pltpu.VMEM)),
        input_output_aliases={i: i for i in range(len(bufs))},
        compiler_params=pltpu.CompilerParams(has_side_effects=pltpu.SideEffectType.DATAFLOW_SIDE_EFFECTING),
    )(*bufs, *started["sems"], after)
    return list(outs[:n_in]), list(outs[n_in:n_in + n_out]), outs[-1]


def _after(value, token):
    return value + token[0, 0].astype(value.dtype)


def _chip_peers(x, y, c):
    return [(1 - x, y, c), (x, 1 - y, c), (1 - x, 1 - y, c)]


def _chip_of(p):
    return 2 * p[0] + p[1]


def _set_slot(slots, me, blk):
    return lax.dynamic_update_slice(slots, blk[None], (me,) + (0,) * blk.ndim)


def _by_core(c, mine, other, axis):
    return jnp.where(c == 0, jnp.stack([mine, other], axis), jnp.stack([other, mine], axis))


def _plan_gather_chips(n):
    def plan(x, y, c):
        return [(a, (), a, (2 * x + y,), p, (_chip_of(p),)) for a in range(n) for p in _chip_peers(x, y, c)]

    return plan


def _gather_chips(name, blocks):
    recv, token = _exchange(name, blocks, [((4,) + b.shape, b.dtype) for b in blocks], _plan_gather_chips(len(blocks)))
    x, y, _ = _position()
    return [_set_slot(r, 2 * x + y, b) for r, b in zip(recv, blocks)], token


def _gather_pass_cores(name, blocks, from_chips):
    n = len(blocks)

    def plan_cores(x, y, c):
        me, sib = 2 * x + y, (x, y, 1 - c)
        own = [(a, (), a, (me,), sib, (me,)) for a in range(n)]
        passed = [(n + a, (_chip_of(p),), a, (_chip_of(p),), sib, (_chip_of(p),)) for a in range(n) for p in _chip_peers(x, y, c)]
        return own + passed

    from_core, token = _exchange(name + "_cores", list(blocks) + list(from_chips), [((4,) + b.shape, b.dtype) for b in blocks], plan_cores)
    x, y, c = _position()
    return [_by_core(c, _set_slot(r1, 2 * x + y, b), r2, 1) for b, r1, r2 in zip(blocks, from_chips, from_core)], token


def _gather_two_level(name, blocks):
    from_chips, _ = _exchange(name + "_chips", blocks, [((4,) + b.shape, b.dtype) for b in blocks], _plan_gather_chips(len(blocks)))
    return _gather_pass_cores(name, blocks, from_chips)


def _pair_add(name, g42, r4):
    _, _, rh, cols = g42.shape
    tr = _divisors(rh, 512, SUBLANES * 2)[-1]

    def body(c_ref, a_ref, b_ref, o_ref):
        o_ref[0] = (a_ref[0, 0].astype(F32) + b_ref[0].astype(F32)).astype(o_ref.dtype)

    cidx = lax.axis_index("c").astype(jnp.int32).reshape(1)
    return _pcall(
        body, name, _sds(r4.shape, BF16), grid=(4, rh // tr),
        in_specs=[pl.BlockSpec((1, 1, tr, cols), lambda s, i, c_ref: (s, c_ref[0], i, 0)), pl.BlockSpec((1, tr, cols), lambda s, i, c_ref: (s, i, 0))],
        out_specs=pl.BlockSpec((1, tr, cols), lambda s, i, c_ref: (s, i, 0)), sem=("parallel", "parallel"), prefetch=1,
    )(cidx, g42, r4)


def _slot_sum(name, parts):
    n, r, cols = parts.shape
    cap = max(2 * SUBLANES, (4 * 1024 * 1024) // (n * cols * parts.dtype.itemsize))
    tr = _divisors(r, cap, 2 * SUBLANES)[-1]

    def body(p_ref, o_ref):
        acc = p_ref[0].astype(F32)
        for k in range(1, n):
            acc = acc + p_ref[k].astype(F32)
        o_ref[...] = acc

    return _pcall(
        body, name, _sds((r, cols), F32), grid=(r // tr,), in_specs=[pl.BlockSpec((n, tr, cols), lambda i: (0, i, 0))],
        out_specs=pl.BlockSpec((tr, cols), lambda i: (i, 0)), sem=("parallel",),
    )(parts)


def _slot_sums(name, parts):
    k = len(parts)

    def body(*refs):
        for p_ref, o_ref in zip(refs[:k], refs[k:]):
            acc = p_ref[0].astype(F32)
            for j in range(1, p_ref.shape[0]):
                acc = acc + p_ref[j].astype(F32)
            o_ref[...] = acc

    return list(_pcall(body, name, [_sds(p.shape[1:], F32) for p in parts])(*parts))


def _gather_weights_start(name, shards, token):
    c = lax.axis_index("c")
    halves = [lax.dynamic_slice_in_dim(w, c * (w.shape[0] // 2), w.shape[0] // 2, 0).astype(BF16) for w in shards]
    if token is not None:
        halves[0] = _after(halves[0], token)
    n = len(halves)

    def plan(x, y, c):
        return [(a, (), a, (2 * x + y, c), p, (_chip_of(p), c)) for a in range(n) for p in _chip_peers(x, y, c)]

    return _exchange_start(name + "_ag_chips", halves, [((4, 2) + h.shape, h.dtype) for h in halves], plan)


def _gather_weights_finish(name, starteds, after):
    halves, slots = [], []
    for started in starteds:
        h, s, token = _exchange_wait(started, after)
        halves, slots = halves + h, slots + s
    n = len(halves)
    x, y, c = _position()
    slots = [lax.dynamic_update_slice(s, h[None, None], (2 * x + y, c, 0, 0)) for s, h in zip(slots, halves)]

    def plan_cores(x, y, c):
        return [(a, (s, c), a, (s, c), (x, y, 1 - c), (s, 1 - c)) for a in range(n) for s in range(4)]

    full, _ = _exchange(name + "_ag_cores", slots, None, plan_cores, inplace=True)
    return [f.reshape((4, 2 * h.shape[0], h.shape[1])) for f, h in zip(full, halves)], token


def _reduce_weights_start(name, grads, token):
    n = len(grads)
    g42 = [g.reshape(4, 2, g.shape[1] // 2, g.shape[2]) for g in grads]

    def plan_swap(x, y, c):
        return [(a, (s, 1 - c), a, (s,), (x, y, 1 - c), (s,)) for a in range(n) for s in range(4)]

    def plan_chips(x, y, c):
        return [(a, (_chip_of(p),), a, (2 * x + y,), p, (_chip_of(p),)) for a in range(n) for p in _chip_peers(x, y, c)]

    other, _ = _exchange(name + "_rs_cores", g42, [((4,) + g.shape[2:], g.dtype) for g in g42], plan_swap, after=token)
    pair = [_pair_add(f"{name}_rs_pair{a}", g, o) for a, (g, o) in enumerate(zip(g42, other))]
    return _exchange_start(name + "_rs_chips", pair, [(p.shape, p.dtype) for p in pair], plan_chips)


def _reduce_weights_finish(name, starteds, after):
    pair, recv = [], []
    for started in starteds:
        p, r, token = _exchange_wait(started, after)
        pair, recv = pair + p, recv + r
    n = len(pair)
    x, y, c = _position()
    me = 2 * x + y

    def plan_share(x, y, c):
        return [(a, (c,), a, (c,), (x, y, 1 - c), (1 - c,)) for a in range(n)]

    parts = [lax.dynamic_update_slice(r, lax.dynamic_slice_in_dim(p, me, 1, 0), (me, 0, 0)) for r, p in zip(recv, pair)]
    mine = [_slot_sum(f"{name}_rs_sum{a}", p) for a, p in enumerate(parts)]
    both = [lax.dynamic_update_slice(lax.empty((2,) + m.shape, m.dtype), m[None], (c, 0, 0)) for m in mine]
    both, _ = _exchange(name + "_rs_share", both, None, plan_share, inplace=True)
    return [b.reshape(2 * b.shape[1], b.shape[2]) for b in both], token


def _allreduce_small(name, grads):
    both, token = _gather_two_level(name + "_ag", [g.astype(BF16) if g.size >= SMALL_BF16_SIZE else g for g in grads])
    return _slot_sums(name + "_sum", [g.reshape((8,) + g.shape[2:]) for g in both]), token


def _ada_fwd_call(name, c_all, w, b_shard):
    nl, d, ns = w.shape
    nb = c_all.shape[0]

    def body(c_ref, w_ref, b_ref, o_ref):
        cv = c_ref[...]
        o_ref[0] = _dg(cv * _sigmoid(cv), w_ref[0], ((1,), (0,))) + b_ref[0]

    return _pcall(
        body, name, _sds((nl, nb, ns), F32), grid=(nl,),
        in_specs=[pl.BlockSpec((nb, d), lambda l: (0, 0)), pl.BlockSpec((1, d, ns), lambda l: (l, 0, 0)), pl.BlockSpec((1, 1, ns), lambda l: (l, 0, 0))],
        out_specs=pl.BlockSpec((1, nb, ns), lambda l: (l, 0, 0)), sem=("parallel",),
    )(c_all, w, b_shard)


def _ada_bwd_call(name, c_all, dm_shard, dm_all):
    nl, nb, ns = dm_shard.shape
    d = c_all.shape[1]
    nm = dm_all.shape[2]

    def body(c_ref, ds_ref, da_ref, dw_ref, db_ref):
        cv = c_ref[...]
        dw_ref[0] = _dg(cv * _sigmoid(cv), ds_ref[0], ((0,), (0,)))
        db_ref[0] = jnp.sum(da_ref[0], axis=0, keepdims=True)

    return _pcall(
        body, name, [_sds((nl, d, ns), F32), _sds((nl, 1, nm), F32)], grid=(nl,),
        in_specs=[pl.BlockSpec((nb, d), lambda l: (0, 0)), pl.BlockSpec((1, nb, ns), lambda l: (l, 0, 0)), pl.BlockSpec((1, nb, nm), lambda l: (l, 0, 0))],
        out_specs=[pl.BlockSpec((1, d, ns), lambda l: (l, 0, 0)), pl.BlockSpec((1, 1, nm), lambda l: (l, 0, 0))], sem=("parallel",),
    )(c_all, dm_shard, dm_all)


def _ada_fwd(name, bl, c_all, w, b):
    nl, d, ns = w.shape
    chip = 2 * lax.axis_index("x") + lax.axis_index("y")
    b_shard = lax.dynamic_slice(b, (0, chip * ns), (nl, ns)).reshape(nl, 1, ns)
    shard = _ada_fwd_call(name + "_fwd", c_all, w, b_shard)
    (allc,), token = _gather_chips(name + "_ag", [shard])
    mods = jnp.transpose(allc, (1, 2, 0, 3)).reshape(nl, c_all.shape[0], 4 * ns)
    return lax.dynamic_slice(mods, (0, (2 * chip + lax.axis_index("c")) * bl, 0), (nl, bl, 4 * ns)), token


def _ada_bwd(name, bl, c_all, ns, dm):
    nl = dm.shape[0]
    chip = 2 * lax.axis_index("x") + lax.axis_index("y")
    (dm_all,), token = _gather_two_level(name + "_bwd_ag", [dm])
    dm_all = jnp.transpose(dm_all.reshape((8,) + dm.shape), (1, 0, 2, 3)).reshape(nl, 8 * bl, 4 * ns)
    dm_shard = lax.dynamic_slice(dm_all, (0, 0, chip * ns), (nl, 8 * bl, ns))
    dw, db = _ada_bwd_call(name + "_bwd", c_all, dm_shard, dm_all)
    return dw, db.reshape(nl, 4 * ns), token


def _adamw(name, w, g, m, v):
    shape = w.shape
    cols = shape[-1]
    w2, g2, m2, v2 = (t.reshape(-1, cols) for t in (w, g, m, v))
    rows = w2.shape[0]
    cap = max(SUBLANES, (512 * 1024) // max(cols, 1) // SUBLANES * SUBLANES)
    tr = _divisors(rows, cap, SUBLANES)[-1]

    def body(w_ref, g_ref, m_ref, v_ref, d_ref, mo_ref, vo_ref):
        gv = g_ref[...]
        mn = ADAM_B1 * m_ref[...] + (1.0 - ADAM_B1) * gv
        vn = ADAM_B2 * v_ref[...] + (1.0 - ADAM_B2) * (gv * gv)
        m_hat = mn / (1.0 - ADAM_B1 ** ADAM_STEP)
        v_hat = vn / (1.0 - ADAM_B2 ** ADAM_STEP)
        d_ref[...] = -ADAM_LR * (m_hat / (jnp.sqrt(v_hat) + ADAM_EPS) + ADAM_WD * w_ref[...])
        mo_ref[...] = mn
        vo_ref[...] = vn

    spec = pl.BlockSpec((tr, cols), lambda i: (i, 0))
    outs = _pcall(body, name, [_sds((rows, cols), F32)] * 3, grid=(rows // tr,), in_specs=[spec] * 4, out_specs=[spec] * 3, sem=("parallel",))(w2, g2, m2, v2)
    return tuple(o.reshape(shape) for o in outs)


def _adamw_small(name, ws, gs, ms, vs):
    n = len(ws)

    def body(*refs):
        for k in range(n):
            w_ref, g_ref, m_ref, v_ref = (refs[j * n + k] for j in range(4))
            d_ref, mo_ref, vo_ref = (refs[(4 + j) * n + k] for j in range(3))
            gv = g_ref[...]
            mn = ADAM_B1 * m_ref[...] + (1.0 - ADAM_B1) * gv
            vn = ADAM_B2 * v_ref[...] + (1.0 - ADAM_B2) * (gv * gv)
            m_hat = mn / (1.0 - ADAM_B1 ** ADAM_STEP)
            v_hat = vn / (1.0 - ADAM_B2 ** ADAM_STEP)
            d_ref[...] = -ADAM_LR * (m_hat / (jnp.sqrt(v_hat) + ADAM_EPS) + ADAM_WD * w_ref[...])
            mo_ref[...] = mn
            vo_ref[...] = vn

    outs = _pcall(body, name, [_sds(w.shape, F32) for w in ws] * 3)(*ws, *gs, *ms, *vs)
    return outs[:n], outs[n:2 * n], outs[2 * n:]


def _pad_rows(w, rows):
    return jnp.pad(w, ((0, rows - w.shape[0]), (0, 0)))


def _pad_lanes(v):
    return jnp.pad(v, (0, LANES - v.shape[0])).reshape(1, LANES)


BIG = ("w_in", "w_out", "ff_up", "ff_down")
BIG_AXIS = {"w_in": 1, "w_out": 0, "ff_up": 1, "ff_down": 0}
CONVW = ("ssd_conv_w", "ff_conv_w")
SMALL = ("norm1_g", "norm2_g", "ssd_conv_b", "ssd_dt_bias", "ssd_a_log", "ssd_d", "ssd_norm_g", "gm_vnorm_g", "gm_ws", "gm_bs", "gm_out_g", "ff_conv_b")
WEIGHTS = ("ada_w", "ada_b", "norm1_g", "norm2_g", "w_in", "ssd_conv_w", "ssd_conv_b", "ssd_dt_bias", "ssd_a_log", "ssd_d", "ssd_norm_g", "gm_vnorm_g", "gm_ws", "gm_bs", "gm_out_g", "w_out", "ff_up", "ff_conv_w", "ff_conv_b", "ff_down", "final_g")


def kernel(x, c, ada_w, ada_b, norm1_g, norm2_g, w_in, ssd_conv_w, ssd_conv_b, ssd_dt_bias, ssd_a_log, ssd_d, ssd_norm_g, gm_vnorm_g, gm_ws, gm_bs, gm_out_g, w_out, ff_up, ff_conv_w, ff_conv_b, ff_down, final_g, loss_target, m_ada_w, m_ada_b, m_norm1_g, m_norm2_g, m_w_in, m_ssd_conv_w, m_ssd_conv_b, m_ssd_dt_bias, m_ssd_a_log, m_ssd_d, m_ssd_norm_g, m_gm_vnorm_g, m_gm_ws, m_gm_bs, m_gm_out_g, m_w_out, m_ff_up, m_ff_conv_w, m_ff_conv_b, m_ff_down, m_final_g, v_ada_w, v_ada_b, v_norm1_g, v_norm2_g, v_w_in, v_ssd_conv_w, v_ssd_conv_b, v_ssd_dt_bias, v_ssd_a_log, v_ssd_d, v_ssd_norm_g, v_gm_vnorm_g, v_gm_ws, v_gm_bs, v_gm_out_g, v_w_out, v_ff_up, v_ff_conv_w, v_ff_conv_b, v_ff_down, v_final_g):
    given = dict(locals())
    weights = {n: given[n] for n in WEIGHTS}
    bl, s, d = x.shape
    nl = ada_w.shape[0]
    heads = d // HEAD_DIM
    cd = d + 2 * SSD_GROUPS * STATE
    f = ff_down.shape[1] * 4
    n_in = d + cd + heads + 2 * d
    n_proj = _round_up(3 * d + cd + LANES, 2 * LANES)
    tm = _divisors(s, 512)[-1]

    gathering = {}

    def start_gathers(l, names, token):
        for n in names:
            gathering[l, n] = _gather_weights_start(f"wg{l}_{n}", [weights[n][l]], token)
            token = gathering[l, n]["token"]
        return token

    pre, _ = _gather_two_level("pre_ag", [c] + [weights[n] for n in CONVW])
    c_all = pre[0].reshape(8 * bl, d)
    conv_full = {n: jnp.concatenate([p[k, 0] for k in range(4)], axis=-1) for n, p in zip(CONVW, pre[1:])}
    mods_all, ada_token = _ada_fwd("ada", bl, c_all, ada_w, ada_b)
    mods = mods_all.reshape(nl, bl, N_MOD, 1, d)

    ready = {}

    def assembled(n, full):
        if n == "w_in":
            return full
        return full.reshape(-1, full.shape[2]) if BIG_AXIS[n] == 0 else jnp.concatenate([full[k] for k in range(4)], axis=1)

    def landed(l, n, after):
        if l > 0 and (l, n) not in ready:
            fulls, token = _gather_weights_finish(f"wg{l}", [gathering.pop((l, m)) for m in BIG], after)
            ready.update({(l, m): (assembled(m, t), token) for m, t in zip(BIG, fulls)})
        if (l, n) in ready:
            return ready.pop((l, n))
        (full,), token = _gather_weights_finish(f"wg{l}_{n}", [gathering.pop((l, n))], after)
        return assembled(n, full), token

    issued = start_gathers(0, BIG, ada_token)
    r2 = lambda v: v.reshape(1, -1)
    n_gm = d // GM_HEAD
    fg = r2(final_g)

    def layer_params(l):
        return dict(
            norm1=r2(norm1_g[l]), norm2=r2(norm2_g[l]), conv_w8=_pad_rows(conv_full["ssd_conv_w"][l], SUBLANES), conv_b=r2(ssd_conv_b[l]),
            dtb=_pad_lanes(ssd_dt_bias[l]), alog=_pad_lanes(ssd_a_log[l]), dsk=r2(jnp.repeat(ssd_d[l], HEAD_DIM)), ng=r2(ssd_norm_g[l]),
            vg=r2(gm_vnorm_g[l]), ws=gm_ws[l], bs_t=jnp.pad(gm_bs[l].T, ((0, 0), (0, LANES - n_gm))), og=r2(gm_out_g[l]),
            ff_w8=_pad_rows(conv_full["ff_conv_w"][l], SUBLANES), ff_b=r2(ff_conv_b[l]))

    def mixer_args(p):
        return (p["conv_w8"], p["conv_b"], p["dtb"], p["alog"], p["dsk"], p["ng"], p["vg"], p["ws"], p["bs_t"], p["og"])

    def padded_w_in(full):
        ns = full.shape[2]

        def cols(a, b):
            return [full[k][:, max(a, k * ns) - k * ns:min(b, (k + 1) * ns) - k * ns] for k in range(4) if max(a, k * ns) < min(b, (k + 1) * ns)]

        parts = cols(0, d) + cols(d + cd + heads, n_in) + cols(d, d + cd) + cols(d + cd, d + cd + heads)
        return jnp.concatenate(parts + [jnp.zeros((d, n_proj - (3 * d + cd + heads)), BF16)], axis=1)

    saved, xcur, pending = [], x, None
    for l in range(nl):
        p, w = layer_params(l), {}
        wi, token = landed(l, "w_in", mods_all if l == 0 else pending[0])
        w["w_in"] = padded_w_in(wi)
        sh1, sc1, g1, sh2, sc2, g2 = (mods[l, :, k] for k in range(N_MOD))
        if l + 1 < nl:
            issued = start_gathers(l + 1, BIG, token + issued)
        sc1 = _after(sc1, issued)
        if pending is None:
            (h,) = _row_fwd(f"nm{l}_fwd", _nm_tile, [(xcur, d, 0, None)], [sc1, sh1], [p["norm1"]], [(d, BF16)], 0, tm)
            x1 = xcur
        else:
            rows = [(xcur, d, 0, None), (pending[0], d, 0, None)]
            x1, h = _row_fwd(f"rnm{l}a_fwd", _rnm_tile, rows, [pending[1], sc1, sh1], [p["norm1"]], [(d, F32), (d, BF16)], 0, tm)
        proj = _mm(f"win{l}_fwd", h, w["w_in"], F32)
        ycat, (xbc, hprev) = _mixer_fwd(f"mix{l}", proj, d, cd, *mixer_args(p))
        w["w_out"], _ = landed(l, "w_out", ycat)
        mix = _mm(f"wout{l}_fwd", ycat, w["w_out"], F32)
        x2, h2 = _row_fwd(f"rnm{l}b_fwd", _rnm_tile, [(x1, d, 0, None), (mix, d, 0, None)], [g1, sc2, sh2], [p["norm2"]], [(d, F32), (d, BF16)], 0, tm)
        w_up, _ = landed(l, "ff_up", h2)
        w["ff_up"] = _gate_value_blocks(f"ffup{l}_blocks", w_up, f)
        up = _mm(f"ffup{l}_fwd", h2, w["ff_up"], F32)
        act = _conv_fwd(f"ffact{l}_fwd", up, 0, f, p["ff_w8"], p["ff_b"], FF_CONV, True, BF16)
        w["ff_down"], _ = landed(l, "ff_down", act)
        down = _mm(f"ffdown{l}_fwd", act, w["ff_down"], F32)
        saved.append(dict(p=p, w=w, x_in=xcur, pending=pending, h=h, proj=proj, xbc=xbc, hprev=hprev, ycat=ycat, x1=x1, mix=mix, h2=h2, up=up, act=act))
        xcur, pending = x2, (down, g2)
    rows = [(xcur, d, 0, F32), (pending[0], d, 0, BF16), (loss_target, d, 0, None)]

    (dx2, ddown), (dg2,), (dfg,), (loss_local,) = _row_bwd(
        "final_bwd", _final_tile, rows, [pending[1]], [fg], [], 1, tm, [jnp.ones((1, 1), F32)], emit_sums=True)
    loss = lax.psum(loss_local[0, 0], AXES)
    dmods, small_grads = [None] * nl, [None] * nl
    reducing = {}

    def start_reduce(l, named, token=None):
        by_chip = []
        for n, g in named:
            width = g.shape[1] // 4
            by_chip.append(g.reshape(4, g.shape[0] // 4, g.shape[1]) if BIG_AXIS[n] == 0 else jnp.stack([g[:, k * width:(k + 1) * width] for k in range(4)]))
        key = (l, tuple(n for n, _ in named))
        reducing[key] = _reduce_weights_start(f"wg{l}_" + "_".join(key[1]), by_chip, token)
        return reducing[key]["token"]

    for l in reversed(range(nl)):
        sv = saved[l]
        p, w = sv["p"], sv["w"]
        sh1, sc1, g1, sh2, sc2, g2 = (mods[l, :, k] for k in range(N_MOD))
        dact, dw_down = _mm_bwd(f"ffdown{l}", sv["act"], w["ff_down"], ddown)
        dup, dff_w8, dff_b = _conv_bwd(f"ffact{l}_bwd", sv["up"], 0, f, p["ff_w8"], p["ff_b"], FF_CONV, True, dact, BF16)
        dh2, dw_up = _mm_bwd(f"ffup{l}", sv["h2"], w["ff_up"], dup)
        dw_up = _gate_value_blocks(f"ffup{l}_columns", dw_up, f, inverse=True)
        sc2 = _after(sc2, start_reduce(l, [("ff_down", dw_down), ("ff_up", dw_up)]))
        rows = [(sv["x1"], d, 0, F32), (sv["mix"], d, 0, BF16)]
        (dx1, dmix), (dg1, dsc2, dsh2), (dn2,) = _row_bwd(f"rnm{l}b_bwd", _rnm_tile, rows, [g1, sc2, sh2], [p["norm2"]], [(d, F32), (d, BF16)], 0, tm, [dx2, dh2])
        dycat, dw_out = _mm_bwd(f"wout{l}", sv["ycat"], w["w_out"], dmix)
        p_tied = p
        if l == 0:
            p_tied = dict(p, dtb=_after(p["dtb"], start_reduce(l, [("w_out", dw_out)])))
        dproj, (dw8, dcb, ddtb, dalog, ddsk, dng, dvg, dws, dbs_t, dog) = _mixer_bwd(f"mix{l}", sv["proj"], d, cd, *mixer_args(p_tied), sv["xbc"], sv["hprev"], dycat)
        dh, dw_in_p = _mm_bwd(f"win{l}", sv["h"], w["w_in"], dproj)
        dw_in = jnp.concatenate([dw_in_p[:, :d], dw_in_p[:, 3 * d:3 * d + cd], dw_in_p[:, 3 * d + cd:3 * d + cd + heads], dw_in_p[:, d:3 * d]], axis=1)
        if l > 0:
            sc1 = _after(sc1, start_reduce(l, [("w_out", dw_out), ("w_in", dw_in)]))
        if sv["pending"] is None:
            (dx2,), (dsc1, dsh1), (dn1,) = _row_bwd(f"nm{l}_bwd", _nm_res_tile, [(sv["x_in"], d, 0, F32)], [sc1, sh1], [p["norm1"]], [(d, F32), (d, BF16)], 0, tm, [dx1, dh])
        else:
            rows = [(sv["x_in"], d, 0, F32), (sv["pending"][0], d, 0, BF16)]
            (dx2, ddown), (dg2_prev, dsc1, dsh1), (dn1,) = _row_bwd(
                f"rnm{l}a_bwd", _rnm_tile, rows, [sv["pending"][1], sc1, sh1], [p["norm1"]], [(d, F32), (d, BF16)], 0, tm, [dx1, dh])
        dmods[l] = jnp.concatenate([dsh1, dsc1, dg1, dsh2, dsc2, dg2], axis=1).reshape(bl, N_MOD * d)
        if sv["pending"] is not None:
            dg2 = dg2_prev
        small_grads[l] = dict(
            norm1_g=dn1.reshape(d), norm2_g=dn2.reshape(d), ssd_conv_b=dcb.reshape(cd), ssd_dt_bias=ddtb[0, :heads], ssd_a_log=dalog[0, :heads],
            ssd_d=ddsk.reshape(heads, HEAD_DIM).sum(-1), ssd_norm_g=dng.reshape(d), gm_vnorm_g=dvg.reshape(d), gm_ws=dws, gm_bs=dbs_t[:, :n_gm].T,
            gm_out_g=dog.reshape(d), ff_conv_b=dff_b.reshape(f), ssd_conv_w=dw8[:SSD_CONV], ff_conv_w=dff_w8[:FF_CONV])
    grad_x = dx2

    g_ada_w, g_ada_b, ada_token = _ada_bwd("ada", bl, c_all, ada_w.shape[2], jnp.stack(dmods))
    small_names_r = SMALL + CONVW + ("final_g",)
    summed, small_token = _allreduce_small("small", [jnp.stack([small_grads[l][n] for l in range(nl)]) for n in SMALL + CONVW] + [dfg])
    travelling = start_reduce(0, [("w_in", dw_in)], ada_token + small_token)
    grad = {"ada_w": g_ada_w, "ada_b": g_ada_b}
    grad.update(zip(small_names_r, summed))
    chip = 2 * lax.axis_index("x") + lax.axis_index("y")
    for n in CONVW:
        width = weights[n].shape[-1]
        grad[n] = lax.dynamic_slice_in_dim(grad[n], chip * width, width, axis=2)

    delta, new_m, new_v = {}, {}, {}
    last = (0, ("w_in",))
    early = [k for k in reducing if k != last]
    finished, _ = _reduce_weights_finish("wg_early", [reducing[k] for k in early], _after(summed[0], travelling))
    big_grads = dict(zip([(l, n) for l, names in early for n in names], finished))
    order = ("ada_w",) + tuple(reversed(BIG))
    for n in order:
        if n == "w_in":
            (big_grads[0, n],), _ = _reduce_weights_finish("wg_last", [reducing[last]], delta[order[-2]])
        if n != "ada_w":
            grad[n] = jnp.stack([big_grads[l, n] for l in range(nl)])
        delta[n], new_m[n], new_v[n] = _adamw("adam_" + n, weights[n], grad[n], given["m_" + n], given["v_" + n])
    small_names = ("ada_b",) + SMALL + CONVW + ("final_g",)
    as2d = lambda t: t.reshape(1, -1) if t.ndim == 1 else t
    res = _adamw_small(
        "adam_small", [as2d(weights[n]) for n in small_names], [as2d(grad[n]) for n in small_names],
        [as2d(given["m_" + n]) for n in small_names], [as2d(given["v_" + n]) for n in small_names])
    for out, vals in zip((delta, new_m, new_v), res):
        for n, val in zip(small_names, vals):
            out[n] = val.reshape(weights[n].shape)
    grad["final_g"] = grad["final_g"].reshape(final_g.shape)

    return (loss, grad_x, *[grad[n] for n in WEIGHTS], *[delta[n] for n in WEIGHTS], *[new_m[n] for n in WEIGHTS], *[new_v[n] for n in WEIGHTS])
```

```python
import jax
import jax.numpy as jnp
from jax import lax
from jax.experimental import pallas as pl
from jax.experimental.pallas import tpu as pltpu

F32 = jnp.float32
BF16 = jnp.bfloat16
EPS = 1e-6
CHUNK = 128
HEAD_DIM = 64
STATE = 128
GM_HEAD = 128
SSD_GROUPS = 2
SSD_CONV = 4
FF_CONV = 3
N_MOD = 6
LANES = 128
SUBLANES = 8
SMALL_BF16_SIZE = 1 << 16
CONV_LANES = 256
CONV_ROWS = 32
V7X_VMEM_LIMIT = 48 * 1024 * 1024
MM_VMEM_BUDGET = 30 * 1024 * 1024
MM_STEP_MACS = 2048 * 1024 * 1024
MM_ACC_SECONDS_PER_VREG = 1.4e-9
V7X_MXU_FLOPS = 996e12
V7X_HBM_BYTES_PER_S = 3.3e12
V7X_STEP_SECONDS = 0.35e-6
ADAM_LR, ADAM_B1, ADAM_B2, ADAM_EPS, ADAM_WD, ADAM_STEP = 0.001, 0.9, 0.999, 1e-08, 0.01, 10
MESH = pl.DeviceIdType.MESH
AXES = ("x", "y", "c")


def _round_up(n, m):
    return (n + m - 1) // m * m


def _divisors(n, cap, mult=LANES):
    out = [t for t in range(mult, min(n, cap) + 1, mult) if n % t == 0]
    return out or [n]


def _pcall(body, name, out_shape, grid=(), in_specs=None, out_specs=None, scratch=(), sem=None, prefetch=0, aliases=None):
    params = pltpu.CompilerParams(dimension_semantics=sem, vmem_limit_bytes=V7X_VMEM_LIMIT)
    if aliases:
        return pl.pallas_call(
            body, name=name, out_shape=out_shape, grid=grid, in_specs=in_specs, out_specs=out_specs, scratch_shapes=list(scratch),
            compiler_params=params, input_output_aliases=aliases)
    if prefetch:
        spec = pltpu.PrefetchScalarGridSpec(num_scalar_prefetch=prefetch, grid=grid, in_specs=in_specs, out_specs=out_specs, scratch_shapes=list(scratch))
        return pl.pallas_call(body, name=name, out_shape=out_shape, grid_spec=spec, compiler_params=params)
    if in_specs is None:
        return pl.pallas_call(body, name=name, out_shape=out_shape, compiler_params=params)
    return pl.pallas_call(body, name=name, out_shape=out_shape, grid=grid, in_specs=in_specs, out_specs=out_specs, scratch_shapes=list(scratch), compiler_params=params)


def _sds(shape, dtype):
    return jax.ShapeDtypeStruct(tuple(shape), dtype)


def _mm_tiles(m, n, k, a_bytes, b_bytes, o_bytes):
    best, best_key = None, None
    for tm in _divisors(m, 2048):
        for tn in _divisors(n, 2560):
            for tk in _divisors(k, 2560):
                vmem = 2 * (tm * tk * a_bytes + tk * tn * b_bytes + tm * tn * o_bytes) + tm * tn * 4
                if vmem > MM_VMEM_BUDGET or tm * tn * tk > MM_STEP_MACS:
                    continue
                ni, nj, nk = m // tm, n // tn, k // tk
                a_reads = 1 if nk == 1 else nj
                b_reads = 1 if (nk == 1 and nj == 1) else ni
                hbm = a_reads * m * k * a_bytes + b_reads * k * n * b_bytes + m * n * o_bytes
                t = max(2.0 * m * n * k / V7X_MXU_FLOPS, hbm / V7X_HBM_BYTES_PER_S) + ni * nj * nk * V7X_STEP_SECONDS
                if nk > 1:
                    t += ni * nj * nk * (tm * tn // (SUBLANES * LANES)) * MM_ACC_SECONDS_PER_VREG
                key = (-t, tm * tn * tk)
                if best_key is None or key > best_key:
                    best, best_key = (tm, tn, tk), key
    return best


def _matmul(name, a, b, mode, out_dtype):
    if mode == "nn":
        (m, k), n = a.shape, b.shape[1]
    elif mode == "nt":
        (m, k), n = a.shape, b.shape[0]
    else:
        (k, m), n = a.shape, b.shape[1]
    tm, tn, tk = _mm_tiles(m, n, k, a.dtype.itemsize, b.dtype.itemsize, jnp.dtype(out_dtype).itemsize)
    nk = k // tk
    if mode == "nn":
        a_spec = pl.BlockSpec((tm, tk), lambda i, j, kk: (i, kk))
        b_spec = pl.BlockSpec((tk, tn), lambda i, j, kk: (kk, j))
        dims = ((1,), (0,))
    elif mode == "nt":
        a_spec = pl.BlockSpec((tm, tk), lambda i, j, kk: (i, kk))
        b_spec = pl.BlockSpec((tn, tk), lambda i, j, kk: (j, kk))
        dims = ((1,), (1,))
    else:
        a_spec = pl.BlockSpec((tk, tm), lambda i, j, kk: (kk, i))
        b_spec = pl.BlockSpec((tk, tn), lambda i, j, kk: (kk, j))
        dims = ((0,), (0,))

    def body(a_ref, b_ref, o_ref, acc_ref):
        kk = pl.program_id(2)
        p = lax.dot_general(a_ref[...].astype(BF16), b_ref[...].astype(BF16), (dims, ((), ())), preferred_element_type=F32)
        if nk == 1:
            o_ref[...] = p.astype(o_ref.dtype)
        else:
            @pl.when(kk == 0)
            def _():
                acc_ref[...] = p

            @pl.when(kk > 0)
            def _():
                acc_ref[...] += p

            @pl.when(kk == nk - 1)
            def _():
                o_ref[...] = acc_ref[...].astype(o_ref.dtype)

    return _pcall(
        body, name, _sds((m, n), out_dtype), grid=(m // tm, n // tn, nk), in_specs=[a_spec, b_spec],
        out_specs=pl.BlockSpec((tm, tn), lambda i, j, kk: (i, j)), scratch=[pltpu.VMEM((tm, tn), F32)],
        sem=("parallel", "parallel", "arbitrary"),
    )(a, b)


def _mm(name, a, w, out_dtype):
    return _matmul(name, a.reshape(-1, a.shape[-1]), w, "nn", out_dtype).reshape(a.shape[:-1] + (w.shape[1],))


def _mm_bwd(name, a, w, dy):
    a2, dy2 = a.reshape(-1, a.shape[-1]), dy.reshape(-1, dy.shape[-1])
    return _matmul(name + "_dx", dy2, w, "nt", BF16).reshape(a.shape), _matmul(name + "_dw", a2, dy2, "tn", BF16)


def _dg(a, b, dims):
    return lax.dot_general(a.astype(BF16), b.astype(BF16), (dims, ((), ())), preferred_element_type=F32)


@jax.custom_vjp
def _dot_nn(a, b):
    return _dg(a, b, ((1,), (0,)))


_dot_nn.defvjp(lambda a, b: (_dot_nn(a, b), (a, b)), lambda r, d: (_dg(d, r[1], ((1,), (1,))), _dg(r[0], d, ((0,), (0,)))))


@jax.custom_vjp
def _dot_nt(a, b):
    return _dg(a, b, ((1,), (1,)))


_dot_nt.defvjp(lambda a, b: (_dot_nt(a, b), (a, b)), lambda r, d: (_dg(d, r[1], ((1,), (0,))), _dg(d, r[0], ((0,), (0,)))))


@jax.custom_vjp
def _dot_tn(a, b):
    return _dg(a, b, ((0,), (0,)))


_dot_tn.defvjp(lambda a, b: (_dot_tn(a, b), (a, b)), lambda r, d: (_dg(r[1], d, ((1,), (1,))), _dg(r[0], d, ((1,), (0,)))))


def _exact_dot(a, c, dims):
    hi = a.astype(BF16)
    r1 = a - hi.astype(F32)
    mid = r1.astype(BF16)
    lo = (r1 - mid.astype(F32)).astype(BF16)
    cb = c.astype(BF16)
    f = lambda t: lax.dot_general(t, cb, (dims, ((), ())), preferred_element_type=F32)
    return f(hi) + f(mid) + f(lo)


@jax.custom_vjp
def _sel_right(a, c):
    return _exact_dot(a, c, ((1,), (0,)))


_sel_right.defvjp(lambda a, c: (_sel_right(a, c), c), lambda c, d: (_exact_dot(d, c, ((1,), (1,))), jnp.zeros_like(c)))


def _exact_dot_left(c, a, dims):
    hi = a.astype(BF16)
    r1 = a - hi.astype(F32)
    mid = r1.astype(BF16)
    lo = (r1 - mid.astype(F32)).astype(BF16)
    cb = c.astype(BF16)
    f = lambda t: lax.dot_general(cb, t, (dims, ((), ())), preferred_element_type=F32)
    return f(hi) + f(mid) + f(lo)


@jax.custom_vjp
def _sel_left(c, a):
    return _exact_dot_left(c, a, ((1,), (0,)))


_sel_left.defvjp(lambda c, a: (_sel_left(c, a), c), lambda c, d: (jnp.zeros_like(c), _exact_dot_left(c, d, ((0,), (0,)))))


def _sigmoid(x):
    return 0.5 * jnp.tanh(0.5 * x) + 0.5


def _rms(x, g):
    return x * lax.rsqrt(jnp.mean(x * x, axis=-1, keepdims=True) + EPS) * g


def _gelu(x):
    return 0.5 * x * (1.0 + lax.erf(x * (2.0 ** -0.5)))


def _causal(n):
    return lax.broadcasted_iota(jnp.int32, (n, n), 0) >= lax.broadcasted_iota(jnp.int32, (n, n), 1)


def _row_in_specs(rows, bparams, gparams, tm):
    specs = [pl.BlockSpec((1, tm, w), lambda b, i, cb=cb: (b, i, cb)) for (_, w, cb, _) in rows]
    specs += [pl.BlockSpec((1, 1, p.shape[-1]), lambda b, i: (b, 0, 0)) for p in bparams]
    specs += [pl.BlockSpec(p.shape, lambda b, i, n=p.ndim: (0,) * n) for p in gparams]
    return specs


def _row_vals(refs, n_rows, n_b, n_g):
    vals = [r[0].astype(F32) for r in refs[:n_rows]]
    vals += [r[0].astype(F32) for r in refs[n_rows:n_rows + n_b]]
    vals += [r[...].astype(F32) for r in refs[n_rows + n_b:n_rows + n_b + n_g]]
    return vals


def _row_fwd(name, tile, rows, bparams, gparams, outs, n_sum, tm, into=None):
    bl, s = rows[0][0].shape[:2]
    n_in = len(rows) + len(bparams) + len(gparams) + (0 if into is None else 1)

    def body(*refs):
        first = (pl.program_id(0) == 0) & (pl.program_id(1) == 0)
        res = tile(*_row_vals(refs, len(rows), len(bparams), len(gparams)))
        o_refs = refs[n_in:]
        for k in range(len(outs)):
            o_refs[k][0] = res[k].astype(o_refs[k].dtype)
        for k in range(n_sum):
            o_ref, val = o_refs[len(outs) + k], res[len(outs) + k]

            @pl.when(first)
            def _(o_ref=o_ref, val=val):
                o_ref[...] = val

            @pl.when(jnp.logical_not(first))
            def _(o_ref=o_ref, val=val):
                o_ref[...] += val

    out_shape = [_sds((bl, s, w), dt) for (w, dt) in outs] + [_sds((1, 1), F32)] * n_sum
    out_specs = [pl.BlockSpec((1, tm, w), lambda b, i: (b, i, 0)) for (w, _) in outs] + [pl.BlockSpec((1, 1), lambda b, i: (0, 0))] * n_sum
    in_specs = _row_in_specs(rows, bparams, gparams, tm)
    args = [r[0] for r in rows] + list(bparams) + list(gparams)
    if into is None:
        return _pcall(body, name, out_shape, grid=(bl, s // tm), in_specs=in_specs, out_specs=out_specs, sem=("arbitrary", "arbitrary"))(*args)
    dest, col = into
    out_shape[0] = _sds(dest.shape, dest.dtype)
    out_specs[0] = pl.BlockSpec((1, tm, outs[0][0]), lambda b, i: (b, i, col))
    return _pcall(
        body, name, out_shape, grid=(bl, s // tm), in_specs=in_specs + [pl.BlockSpec(memory_space=pl.ANY)], out_specs=out_specs,
        sem=("arbitrary", "arbitrary"), aliases={len(args): 0},
    )(*args, dest)


def _row_bwd(name, tile, rows, bparams, gparams, outs, n_sum, tm, cts, emit_sums=False):
    bl, s = rows[0][0].shape[:2]
    n_r, n_b, n_g = len(rows), len(bparams), len(gparams)
    n_extra = n_sum if emit_sums else 0
    n_in = n_r + n_b + n_g
    n_ct = len(outs) + n_sum
    grad_rows = [k for k in range(n_r) if rows[k][3]]

    def body(*refs):
        b, i = pl.program_id(0), pl.program_id(1)
        vals = _row_vals(refs, n_r, n_b, n_g)
        ct_refs = refs[n_in:n_in + n_ct]
        ct = [r[0].astype(F32) for r in ct_refs[:len(outs)]] + [r[...] for r in ct_refs[len(outs):]]
        primal, vjp = jax.vjp(tile, *vals)
        grads = tuple(vjp(tuple(ct))) + (tuple(primal[len(outs):]) if emit_sums else ())
        o_refs = refs[n_in + n_ct:]
        for j, k in enumerate(grad_rows):
            o_refs[j][0] = grads[k].astype(o_refs[j].dtype)
        for k in range(n_b):
            o_ref, val = o_refs[len(grad_rows) + k], grads[n_r + k]

            @pl.when(i == 0)
            def _(o_ref=o_ref, val=val):
                o_ref[0] = val

            @pl.when(i > 0)
            def _(o_ref=o_ref, val=val):
                o_ref[0] += val

        first = (b == 0) & (i == 0)
        for k in range(n_g + n_extra):
            o_ref, val = o_refs[len(grad_rows) + n_b + k], grads[n_r + n_b + k]

            @pl.when(first)
            def _(o_ref=o_ref, val=val):
                o_ref[...] = val

            @pl.when(jnp.logical_not(first))
            def _(o_ref=o_ref, val=val):
                o_ref[...] += val

    one = pl.BlockSpec((1, 1), lambda b, i: (0, 0))
    in_specs = _row_in_specs(rows, bparams, gparams, tm)
    in_specs += [pl.BlockSpec((1, tm, w), lambda b, i: (b, i, 0)) for (w, _) in outs] + [one] * n_sum
    out_shape = [_sds((bl, s, rows[k][1]), rows[k][3]) for k in grad_rows]
    out_shape += [_sds(p.shape, F32) for p in bparams] + [_sds(p.shape, F32) for p in gparams] + [_sds((1, 1), F32)] * n_extra
    out_specs = [pl.BlockSpec((1, tm, rows[k][1]), lambda b, i: (b, i, 0)) for k in grad_rows]
    out_specs += [pl.BlockSpec((1, 1, p.shape[-1]), lambda b, i: (b, 0, 0)) for p in bparams]
    out_specs += [pl.BlockSpec(p.shape, lambda b, i, n=p.ndim: (0,) * n) for p in gparams] + [one] * n_extra
    res = _pcall(
        body, name, out_shape, grid=(bl, s // tm), in_specs=in_specs, out_specs=out_specs, sem=("arbitrary", "arbitrary"),
    )(*[r[0] for r in rows], *bparams, *gparams, *cts)
    n0, n1, n2 = len(grad_rows), len(grad_rows) + n_b, len(grad_rows) + n_b + n_g
    return (res[:n0], res[n0:n1], res[n1:n2]) + ((res[n2:],) if emit_sums else ())


def _nm_tile(x, sc, sh, g):
    return (_rms(x, g) * (1.0 + sc) + sh,)


def _nm_res_tile(x, sc, sh, g):
    return x, _rms(x, g) * (1.0 + sc) + sh


def _rnm_tile(x, o, gate, sc, sh, g):
    xn = x + gate * o
    return xn, _rms(xn, g) * (1.0 + sc) + sh


def _final_tile(x, o, tgt, gate, g):
    e = _rms(x + gate * o, g) - tgt
    return (0.5 * jnp.sum(jnp.mean(e * e, axis=-1, keepdims=True), axis=0, keepdims=True),)


def _gm_tile(u_in, v_in, vg, ws, bs_t, og):
    d = u_in.shape[1]
    u, vn = _gelu(u_in), _rms(_gelu(v_in), vg)
    causal = _causal(CHUNK)
    lane = lax.broadcasted_iota(jnp.int32, (1, LANES), 1)
    parts = []
    for h in range(d // GM_HEAD):
        bias = jnp.sum(bs_t * (lane == h).astype(F32), axis=1, keepdims=True)
        parts.append(_dot_nn(jnp.where(causal, ws[h], 0.0), vn[:, h * GM_HEAD:(h + 1) * GM_HEAD]) + bias)
    return (_rms(u * jnp.concatenate(parts, axis=1), og),)


def _conv_window(ref, r0, rows, c0, tc, seq, before, after):
    parts = []
    if before:
        p0 = pl.multiple_of(jnp.maximum(r0 - SUBLANES, 0), SUBLANES)
        parts.append(jnp.where(r0 > 0, ref[0, pl.ds(p0, SUBLANES), pl.ds(c0, tc)].astype(F32), 0.0))
    parts.append(ref[0, pl.ds(r0, rows), pl.ds(c0, tc)].astype(F32))
    if after:
        n0 = pl.multiple_of(jnp.minimum(r0 + rows, seq - SUBLANES), SUBLANES)
        parts.append(jnp.where(r0 + rows < seq, ref[0, pl.ds(n0, SUBLANES), pl.ds(c0, tc)].astype(F32), 0.0))
    return jnp.concatenate(parts, axis=0) if len(parts) > 1 else parts[0]


def _conv_taps_pre(xe, w, b, taps, rows):
    pre = xe[SUBLANES:SUBLANES + rows] * w[taps - 1] + b
    for j in range(1, taps):
        pre = pre + pltpu.roll(xe, j, 0)[SUBLANES:SUBLANES + rows] * w[taps - 1 - j]
    return pre


def _conv_fwd(name, src, col0, chans, w8, b, taps, gated, out_dtype):
    bl, s, _ = src.shape
    tc = CONV_LANES
    xw = 2 * tc if gated else tc

    def body(x_ref, w_ref, b_ref, o_ref):
        w = [w_ref[k:k + 1, :] for k in range(taps)]
        bias = b_ref[...]

        def step(c, carry):
            r0 = pl.multiple_of(c * CONV_ROWS, CONV_ROWS)
            xe = _conv_window(x_ref, r0, CONV_ROWS, 0, tc, s, True, False)
            pre = _conv_taps_pre(xe, w, bias, taps, CONV_ROWS)
            y = pre * _sigmoid(pre)
            if gated:
                y = y * x_ref[0, pl.ds(r0, CONV_ROWS), pl.ds(tc, tc)]
            o_ref[0, pl.ds(r0, CONV_ROWS), :] = y.astype(o_ref.dtype)
            return carry

        lax.fori_loop(0, s // CONV_ROWS, step, 0, unroll=4)

    first = 0 if gated else col0 // tc
    return _pcall(
        body, name, _sds((bl, s, chans), out_dtype), grid=(chans // tc, bl),
        in_specs=[pl.BlockSpec((1, s, xw), lambda j, bb: (bb, 0, first + j)), pl.BlockSpec((SUBLANES, tc), lambda j, bb: (0, j)), pl.BlockSpec((1, tc), lambda j, bb: (0, j))],
        out_specs=pl.BlockSpec((1, s, tc), lambda j, bb: (bb, 0, j)), sem=("parallel", "arbitrary"),
    )(src, w8, b)


def _conv_bwd(name, src, col0, chans, w8, b, taps, gated, dy, dx_dtype):
    bl, s, _ = src.shape
    tc = CONV_LANES
    xw = 2 * tc if gated else tc
    ext = CONV_ROWS + SUBLANES
    pieces = list(dy) if isinstance(dy, (list, tuple)) else [dy]
    starts = [sum(p.shape[2] for p in pieces[:k]) // tc for k in range(len(pieces))]

    def body(x_ref, w_ref, b_ref, *rest):
        dy_refs, (dx_ref, dw_ref, db_ref) = rest[:len(pieces)], rest[len(pieces):]
        j, bb = pl.program_id(0), pl.program_id(1)
        w = [w_ref[k:k + 1, :] for k in range(taps)]
        bias = b_ref[...]

        def dy_window(r0):
            d = _conv_window(dy_refs[0], r0, CONV_ROWS, 0, tc, s, False, True)
            for k in range(1, len(pieces)):
                d = jnp.where(j >= starts[k], _conv_window(dy_refs[k], r0, CONV_ROWS, 0, tc, s, False, True), d)
            return d

        def fold(v):
            acc = v[0:SUBLANES]
            for i in range(1, CONV_ROWS // SUBLANES):
                acc = acc + v[i * SUBLANES:(i + 1) * SUBLANES]
            return acc

        def step(c, carry):
            r0 = pl.multiple_of(c * CONV_ROWS, CONV_ROWS)
            xe = _conv_window(x_ref, r0, CONV_ROWS, 0, tc, s, True, True)
            pre = _conv_taps_pre(xe, w, bias, taps, ext)
            sig = _sigmoid(pre)
            d = dy_window(r0)
            dsil = d * _conv_window(x_ref, r0, CONV_ROWS, tc, tc, s, False, True) if gated else d
            dpre = dsil * sig * (1.0 + pre * (1.0 - sig))
            dx = dpre[:CONV_ROWS] * w[taps - 1]
            for j in range(1, taps):
                dx = dx + pltpu.roll(dpre, ext - j, 0)[:CONV_ROWS] * w[taps - 1 - j]
            dx_ref[0, pl.ds(r0, CONV_ROWS), pl.ds(0, tc)] = dx.astype(dx_ref.dtype)
            if gated:
                dx_ref[0, pl.ds(r0, CONV_ROWS), pl.ds(tc, tc)] = (d[:CONV_ROWS] * (pre * sig)[:CONV_ROWS]).astype(dx_ref.dtype)
            here = dpre[:CONV_ROWS]
            sums = [fold(here * (pltpu.roll(xe, taps - 1 - k, 0) if k < taps - 1 else xe)[SUBLANES:ext]) + carry[k] for k in range(taps)]
            return tuple(sums) + (fold(here) + carry[taps],)

        zero = jnp.zeros((SUBLANES, tc), F32)
        sums = lax.fori_loop(0, s // CONV_ROWS, step, (zero,) * (taps + 1), unroll=2)
        rows = [jnp.sum(t, axis=0, keepdims=True) for t in sums]
        dw = jnp.concatenate(rows[:taps] + [jnp.zeros_like(rows[0])] * (SUBLANES - taps), axis=0)

        @pl.when(bb == 0)
        def _():
            dw_ref[...] = dw
            db_ref[...] = rows[taps]

        @pl.when(bb > 0)
        def _():
            dw_ref[...] += dw
            db_ref[...] += rows[taps]

    first = 0 if gated else col0 // tc
    out_shape = [_sds((bl, s, chans * (2 if gated else 1)), dx_dtype), _sds((SUBLANES, chans), F32), _sds((1, chans), F32)]
    dy_specs = [
        pl.BlockSpec((1, s, tc), lambda j, bb, k=k: (bb, 0, jnp.clip(j - starts[k], 0, pieces[k].shape[2] // tc - 1)))
        for k in range(len(pieces))
    ]
    return _pcall(
        body, name, out_shape, grid=(chans // tc, bl),
        in_specs=[
            pl.BlockSpec((1, s, xw), lambda j, bb: (bb, 0, first + j)), pl.BlockSpec((SUBLANES, tc), lambda j, bb: (0, j)),
            pl.BlockSpec((1, tc), lambda j, bb: (0, j)), *dy_specs,
        ],
        out_specs=[pl.BlockSpec((1, s, xw), lambda j, bb: (bb, 0, j)), pl.BlockSpec((SUBLANES, tc), lambda j, bb: (0, j)), pl.BlockSpec((1, tc), lambda j, bb: (0, j))],
        sem=("parallel", "arbitrary"),
    )(src, w8, b, *pieces)


def _gate_value_blocks(name, w, f, inverse=False):
    nb = f // CONV_LANES
    src = (lambda j: (0, 2 * (j % nb) + j // nb)) if inverse else (lambda j: (0, (j % 2) * nb + j // 2))

    def body(x_ref, o_ref):
        o_ref[...] = x_ref[...]

    return _pcall(
        body, name, _sds(w.shape, w.dtype), grid=(2 * nb,), in_specs=[pl.BlockSpec((w.shape[0], CONV_LANES), src)],
        out_specs=pl.BlockSpec((w.shape[0], CONV_LANES), lambda j: (0, j)), sem=("parallel",),
    )(w)


def _ssd_step(states, xs, bm, cm, dtr, z, dtb, alog, dsk, ng):
    gw = xs.shape[1] // SSD_GROUPS
    hpg, npair = gw // HEAD_DIM, gw // LANES
    dt = jax.nn.softplus(dtr + dtb)
    da = dt * (-jnp.exp(alog))
    causal = _causal(CHUNK)
    acs = _sel_left(causal.astype(F32), da)
    acs_t = acs.T
    lane = lax.broadcasted_iota(jnp.int32, (1, LANES), 1)
    sub = lax.broadcasted_iota(jnp.int32, (LANES, 1), 0)
    last = lax.broadcasted_iota(jnp.int32, (CHUNK, gw), 0) == CHUNK - 1
    outs, new_states = [], []
    for g in range(SSD_GROUPS):
        xg, zg = xs[:, g * gw:(g + 1) * gw], z[:, g * gw:(g + 1) * gw]
        bg, cg = bm[:, g * STATE:(g + 1) * STATE], cm[:, g * STATE:(g + 1) * STATE]
        head_of_col = g * hpg + lax.broadcasted_iota(jnp.int32, (LANES, gw), 1) // HEAD_DIM
        expand = (lax.broadcasted_iota(jnp.int32, (LANES, gw), 0) == head_of_col).astype(F32)
        dt_e, acs_e = _sel_right(dt, expand), _sel_right(acs, expand)
        alast_e = jnp.sum(jnp.where(last, acs_e, 0.0), axis=0, keepdims=True)
        xc = xg * dt_e
        xc_st = xc * jnp.exp(alast_e - acs_e)
        decay_out, chunk_decay = jnp.exp(acs_e), jnp.exp(alast_e)
        cb = _dot_nt(cg, bg)
        ys = []
        for p in range(npair):
            sl = slice(p * LANES, (p + 1) * LANES)
            state = states[g * npair + p]
            y = _dot_nn(cg, state) * decay_out[:, sl]
            for q in range(2):
                head = g * hpg + 2 * p + q
                col = jnp.sum(acs * (lane == head).astype(F32), axis=1, keepdims=True)
                row = jnp.sum(acs_t * (sub == head).astype(F32), axis=0, keepdims=True)
                decay = jnp.where(causal, jnp.exp(jnp.where(causal, col - row, 0.0)), 0.0)
                y = y + _dot_nn(cb * decay, xc[:, sl] * ((lane // HEAD_DIM) == q).astype(F32))
            ys.append(y)
            new_states.append(state * chunk_decay[:, sl] + _dot_tn(bg, xc_st[:, sl]))
        y = jnp.concatenate(ys, axis=1) + dsk[:, g * gw:(g + 1) * gw] * xg
        outs.append(_rms(y * (zg * _sigmoid(zg)), ng[:, g * gw:(g + 1) * gw]))
    return tuple(new_states), jnp.concatenate(outs, axis=1)


def _ssd_scan_specs(d, dt_col, nc, rev):
    ci = (lambda i: nc - 1 - i) if rev else (lambda i: i)
    bc = SSD_GROUPS * STATE
    return [
        pl.BlockSpec((1, CHUNK, d), lambda b, i: (b, ci(i), 0)),
        pl.BlockSpec((1, CHUNK, bc), lambda b, i: (b, ci(i), d // bc)),
        pl.BlockSpec((1, CHUNK, bc), lambda b, i: (b, ci(i), d // bc + 1)),
        pl.BlockSpec((1, CHUNK, LANES), lambda b, i: (b, ci(i), dt_col // LANES)),
        pl.BlockSpec((1, CHUNK, d), lambda b, i: (b, ci(i), 0)),
        pl.BlockSpec((1, LANES), lambda b, i: (0, 0)), pl.BlockSpec((1, LANES), lambda b, i: (0, 0)),
        pl.BlockSpec((1, d), lambda b, i: (0, 0)), pl.BlockSpec((1, d), lambda b, i: (0, 0)),
    ]


def _ssd_scan_fwd(name, xbc, proj, dt_col, dtb, alog, dsk, ng):
    bl, s, _ = xbc.shape
    d = dsk.shape[1]
    nc, nst = s // CHUNK, d // LANES

    def body(xs_ref, bm_ref, cm_ref, dt_ref, z_ref, dtb_ref, alog_ref, dsk_ref, ng_ref, y_ref, hp_ref, st_ref):
        @pl.when(pl.program_id(1) == 0)
        def _():
            st_ref[...] = jnp.zeros_like(st_ref)

        states = tuple(st_ref[p] for p in range(nst))
        hp_ref[0, 0] = st_ref[...]
        new_states, yn = _ssd_step(states, xs_ref[0], bm_ref[0], cm_ref[0], dt_ref[0], z_ref[0], dtb_ref[...], alog_ref[...], dsk_ref[...], ng_ref[...])
        for p in range(nst):
            st_ref[p] = new_states[p]
        y_ref[0] = yn.astype(y_ref.dtype)

    return _pcall(
        body, name, [_sds((bl, s, 2 * d), BF16), _sds((bl, nc, nst, STATE, LANES), F32)], grid=(bl, nc), in_specs=_ssd_scan_specs(d, dt_col, nc, False),
        out_specs=[pl.BlockSpec((1, CHUNK, d), lambda b, i: (b, i, 0)), pl.BlockSpec((1, 1, nst, STATE, LANES), lambda b, i: (b, i, 0, 0, 0))],
        scratch=[pltpu.VMEM((nst, STATE, LANES), F32)], sem=("arbitrary", "arbitrary"),
    )(xbc, xbc, xbc, proj, proj, dtb, alog, dsk, ng)


def _ssd_scan_bwd(name, xbc, proj, dt_col, dtb, alog, dsk, ng, hprev, dy):
    bl, s, _ = xbc.shape
    d = dsk.shape[1]
    nc, nst, bc = s // CHUNK, d // LANES, SSD_GROUPS * STATE

    def body(xs_ref, bm_ref, cm_ref, dt_ref, z_ref, dtb_ref, alog_ref, dsk_ref, ng_ref, hp_ref, dy_ref,
             dxs_ref, dbm_ref, dcm_ref, ddt_ref, dz_ref, ddtb_ref, dalog_ref, ddsk_ref, dng_ref, dst_ref):
        b, i = pl.program_id(0), pl.program_id(1)

        @pl.when(i == 0)
        def _():
            dst_ref[...] = jnp.zeros_like(dst_ref)

        states = tuple(hp_ref[0, 0, p] for p in range(nst))
        _, vjp = jax.vjp(_ssd_step, states, xs_ref[0], bm_ref[0], cm_ref[0], dt_ref[0], z_ref[0], dtb_ref[...], alog_ref[...], dsk_ref[...], ng_ref[...])
        d_states, dxs, dbm, dcm, ddt, dz, ddtb, dalog, ddsk, dng = vjp((tuple(dst_ref[p] for p in range(nst)), dy_ref[0].astype(F32)))
        for p in range(nst):
            dst_ref[p] = d_states[p]
        dxs_ref[0], dbm_ref[0], dcm_ref[0], ddt_ref[0] = dxs, dbm, dcm, ddt
        dz_ref[0] = dz.astype(dz_ref.dtype)
        first = (b == 0) & (i == 0)
        for o_ref, val in ((ddtb_ref, ddtb), (dalog_ref, dalog), (ddsk_ref, ddsk), (dng_ref, dng)):
            @pl.when(first)
            def _(o_ref=o_ref, val=val):
                o_ref[...] = val

            @pl.when(jnp.logical_not(first))
            def _(o_ref=o_ref, val=val):
                o_ref[...] += val

    rc = lambda i: nc - 1 - i
    in_specs = _ssd_scan_specs(d, dt_col, nc, True) + [
        pl.BlockSpec((1, 1, nst, STATE, LANES), lambda b, i: (b, rc(i), 0, 0, 0)), pl.BlockSpec((1, CHUNK, d), lambda b, i: (b, rc(i), 0)),
    ]
    out_shape = [
        _sds((bl, s, d), F32), _sds((bl, s, bc), F32), _sds((bl, s, bc), F32), _sds((bl, s, LANES), F32), _sds((bl, s, d), BF16),
        _sds((1, LANES), F32), _sds((1, LANES), F32), _sds((1, d), F32), _sds((1, d), F32),
    ]
    row = lambda w: pl.BlockSpec((1, CHUNK, w), lambda b, i: (b, rc(i), 0))
    whole = lambda w: pl.BlockSpec((1, w), lambda b, i: (0, 0))
    out_specs = [row(d), row(bc), row(bc), row(LANES), row(d), whole(LANES), whole(LANES), whole(d), whole(d)]
    return _pcall(
        body, name, out_shape, grid=(bl, nc), in_specs=in_specs, out_specs=out_specs,
        scratch=[pltpu.VMEM((nst, STATE, LANES), F32)], sem=("arbitrary", "arbitrary"),
    )(xbc, xbc, xbc, proj, proj, dtb, alog, dsk, ng, hprev, dy)


def _mixer_fwd(name, proj, d, cd, conv_w8, conv_b, dtb, alog, dsk, ng, vg, ws, bs_t, og):
    xbc = _conv_fwd(name + "_conv_fwd", proj, 3 * d, cd, conv_w8, conv_b, SSD_CONV, False, F32)
    y, hprev = _ssd_scan_fwd(name + "_ssd_fwd", xbc, proj, 3 * d + cd, dtb, alog, dsk, ng)
    rows = [(proj, d, 1, BF16), (proj, d, 2, BF16)]
    (ycat,) = _row_fwd(name + "_gm_fwd", _gm_tile, rows, [], [vg, ws, bs_t, og], [(d, BF16)], 0, CHUNK, into=(y, 1))
    return ycat, (xbc, hprev)


def _mixer_bwd(name, proj, d, cd, conv_w8, conv_b, dtb, alog, dsk, ng, vg, ws, bs_t, og, xbc, hprev, dycat):
    bl, s, n_proj = proj.shape
    dt_col = 3 * d + cd
    dy, dg_out = dycat[..., :d], dycat[..., d:]
    dxs, dbm, dcm, ddt, dz, ddtb, dalog, ddsk, dng = _ssd_scan_bwd(name + "_ssd_bwd", xbc, proj, dt_col, dtb, alog, dsk, ng, hprev, dy)
    dxbc, dw8, dcb = _conv_bwd(name + "_conv_bwd", proj, 3 * d, cd, conv_w8, conv_b, SSD_CONV, False, [dxs, dbm, dcm], BF16)
    rows = [(proj, d, 1, BF16), (proj, d, 2, BF16)]
    (du, dv), _, (dvg, dws, dbs_t, dog) = _row_bwd(name + "_gm_bwd", _gm_tile, rows, [], [vg, ws, bs_t, og], [(d, BF16)], 0, CHUNK, [dg_out])
    pad = jnp.zeros((bl, s, n_proj - dt_col - LANES), BF16)
    dproj = jnp.concatenate([dz, du, dv, dxbc, ddt.astype(BF16), pad], axis=-1)
    return dproj, (dw8, dcb, ddtb, dalog, ddsk, dng, dvg, dws, dbs_t, dog)


def _position():
    return lax.axis_index("x"), lax.axis_index("y"), lax.axis_index("c")


def _at(ref, idx):
    return ref.at[idx] if len(idx) else ref


def _exchange(name, inputs, out_shapes, plan, inplace=False, after=None):
    if inplace:
        out_shapes = [(a.shape, a.dtype) for a in inputs]
    n_in, n_out = len(inputs), len(out_shapes)
    n_after = 0 if after is None else 1
    n_copy = len(plan(0, 0, 0))

    def body(*refs):
        in_refs, out_refs, token = refs[:n_in], refs[n_in + n_after:n_in + n_after + n_out], refs[n_in + n_after + n_out]
        send_sems, recv_sems = refs[n_in + n_after + n_out + 1:]
        token[...] = jnp.zeros_like(token)
        x, y, c = _position()
        copies = plan(x, y, c)

        def copy(k, src, dst, peer):
            return pltpu.make_async_remote_copy(src_ref=src, dst_ref=dst, send_sem=send_sems.at[k], recv_sem=recv_sems.at[k], device_id=peer, device_id_type=MESH)

        src_refs = out_refs if inplace else in_refs
        sends = [copy(k, _at(src_refs[sa], si), _at(out_refs[da], di), peer) for k, (sa, si, da, di, peer, _) in enumerate(copies)]
        for cp in sends:
            cp.start()
        for k, (sa, si, da, _, peer, li) in enumerate(copies):
            copy(k, _at(src_refs[sa], si), _at(out_refs[da], li), peer).wait_recv()
        for cp in sends:
            cp.wait_send()

    any_spec = pl.BlockSpec(memory_space=pl.ANY)
    outs = pl.pallas_call(
        body, name=name, out_shape=[_sds(s, dt) for s, dt in out_shapes] + [_sds((SUBLANES, LANES), F32)], in_specs=[any_spec] * (n_in + n_after),
        out_specs=[any_spec] * n_out + [pl.BlockSpec(memory_space=pltpu.VMEM)],
        scratch_shapes=[pltpu.SemaphoreType.DMA((n_copy,)), pltpu.SemaphoreType.DMA((n_copy,))],
        input_output_aliases={i: i for i in range(n_in)} if inplace else {},
    )(*inputs, *([] if after is None else [after]))
    return list(outs[:n_out]), outs[n_out]


def _exchange_start(name, inputs, out_shapes, plan):
    n_in, n_out = len(inputs), len(out_shapes)
    n_copy = len(plan(0, 0, 0))

    def body(*refs):
        in_refs, land_refs = refs[:n_in], refs[n_in:n_in + n_out]
        send_sems, recv_sems = refs[n_in + n_out:n_in + n_out + 2]
        token = refs[-1]
        x, y, c = _position()
        for k, (sa, si, da, di, peer, _) in enumerate(plan(x, y, c)):
            pltpu.make_async_remote_copy(
                src_ref=_at(in_refs[sa], si), dst_ref=_at(land_refs[da], di), send_sem=send_sems.at[k], recv_sem=recv_sems.at[k],
                device_id=peer, device_id_type=MESH).start()
        token[...] = jnp.zeros_like(token)

    hbm, sem = pl.BlockSpec(memory_space=pltpu.HBM), pl.BlockSpec(memory_space=pltpu.SEMAPHORE)
    lands = [lax.empty(s, dt) for s, dt in out_shapes]
    args = [pltpu.with_memory_space_constraint(a, pltpu.HBM) for a in list(inputs) + lands]
    outs = pl.pallas_call(
        body, name=name,
        out_shape=(pltpu.SemaphoreType.DMA((n_copy,)), pltpu.SemaphoreType.DMA((n_copy,)), *[pltpu.HBM(a.shape, a.dtype) for a in args], _sds((SUBLANES, LANES), F32)),
        in_specs=[hbm] * (n_in + n_out), out_specs=(sem, sem, *[hbm] * (n_in + n_out), pl.BlockSpec(memory_space=pltpu.VMEM)),
        input_output_aliases={i: 2 + i for i in range(n_in + n_out)},
        compiler_params=pltpu.CompilerParams(has_side_effects=pltpu.SideEffectType.DATAFLOW_SIDE_EFFECTING),
    )(*args)
    return dict(name=name, plan=plan, sems=outs[:2], ins=list(outs[2:2 + n_in]), lands=list(outs[2 + n_in:2 + n_in + n_out]), token=outs[-1])


def _exchange_wait(started, after):
    plan, n_in, n_out = started["plan"], len(started["ins"]), len(started["lands"])

    def body(*refs):
        in_refs, land_refs = refs[:n_in], refs[n_in:n_in + n_out]
        send_sems, recv_sems = refs[n_in + n_out:n_in + n_out + 2]
        token = refs[-1]
        x, y, c = _position()
        for k, (sa, si, da, _, peer, li) in enumerate(plan(x, y, c)):
            cp = pltpu.make_async_remote_copy(
                src_ref=_at(in_refs[sa], si), dst_ref=_at(land_refs[da], li), send_sem=send_sems.at[k], recv_sem=recv_sems.at[k],
                device_id=peer, device_id_type=MESH)
            cp.wait_send()
            cp.wait_recv()
        token[...] = jnp.zeros_like(token)

    hbm, sem = pl.BlockSpec(memory_space=pltpu.HBM), pl.BlockSpec(memory_space=pltpu.SEMAPHORE)
    bufs = started["ins"] + started["lands"]
    outs = pl.pallas_call(
        body, name=started["name"] + "_wait", out_shape=(*[pltpu.HBM(a.shape, a.dtype) for a in bufs], _sds((SUBLANES, LANES), F32)),
        in_specs=[hbm] * len(bufs) + [sem, sem, pl.BlockSpec(memory_space=pl.ANY)], out_specs=(*[hbm] * len(bufs), pl.BlockSpec(memory_space=pltpu.VMEM)),
        input_output_aliases={i: i for i in range(len(bufs))},
        compiler_params=pltpu.CompilerParams(has_side_effects=pltpu.SideEffectType.DATAFLOW_SIDE_EFFECTING),
    )(*bufs, *started["sems"], after)
    return list(outs[:n_in]), list(outs[n_in:n_in + n_out]), outs[-1]


def _after(value, token):
    return value + token[0, 0].astype(value.dtype)


def _chip_peers(x, y, c):
    return [(1 - x, y, c), (x, 1 - y, c), (1 - x, 1 - y, c)]


def _chip_of(p):
    return 2 * p[0] + p[1]


def _set_slot(slots, me, blk):
    return lax.dynamic_update_slice(slots, blk[None], (me,) + (0,) * blk.ndim)


def _by_core(c, mine, other, axis):
    return jnp.where(c == 0, jnp.stack([mine, other], axis), jnp.stack([other, mine], axis))


def _plan_gather_chips(n):
    def plan(x, y, c):
        return [(a, (), a, (2 * x + y,), p, (_chip_of(p),)) for a in range(n) for p in _chip_peers(x, y, c)]

    return plan


def _gather_chips(name, blocks):
    recv, token = _exchange(name, blocks, [((4,) + b.shape, b.dtype) for b in blocks], _plan_gather_chips(len(blocks)))
    x, y, _ = _position()
    return [_set_slot(r, 2 * x + y, b) for r, b in zip(recv, blocks)], token


def _gather_pass_cores(name, blocks, from_chips):
    n = len(blocks)

    def plan_cores(x, y, c):
        me, sib = 2 * x + y, (x, y, 1 - c)
        own = [(a, (), a, (me,), sib, (me,)) for a in range(n)]
        passed = [(n + a, (_chip_of(p),), a, (_chip_of(p),), sib, (_chip_of(p),)) for a in range(n) for p in _chip_peers(x, y, c)]
        return own + passed

    from_core, token = _exchange(name + "_cores", list(blocks) + list(from_chips), [((4,) + b.shape, b.dtype) for b in blocks], plan_cores)
    x, y, c = _position()
    return [_by_core(c, _set_slot(r1, 2 * x + y, b), r2, 1) for b, r1, r2 in zip(blocks, from_chips, from_core)], token


def _gather_two_level(name, blocks):
    from_chips, _ = _exchange(name + "_chips", blocks, [((4,) + b.shape, b.dtype) for b in blocks], _plan_gather_chips(len(blocks)))
    return _gather_pass_cores(name, blocks, from_chips)


def _pair_add(name, g42, r4):
    _, _, rh, cols = g42.shape
    tr = _divisors(rh, 512, SUBLANES * 2)[-1]

    def body(c_ref, a_ref, b_ref, o_ref):
        o_ref[0] = (a_ref[0, 0].astype(F32) + b_ref[0].astype(F32)).astype(o_ref.dtype)

    cidx = lax.axis_index("c").astype(jnp.int32).reshape(1)
    return _pcall(
        body, name, _sds(r4.shape, BF16), grid=(4, rh // tr),
        in_specs=[pl.BlockSpec((1, 1, tr, cols), lambda s, i, c_ref: (s, c_ref[0], i, 0)), pl.BlockSpec((1, tr, cols), lambda s, i, c_ref: (s, i, 0))],
        out_specs=pl.BlockSpec((1, tr, cols), lambda s, i, c_ref: (s, i, 0)), sem=("parallel", "parallel"), prefetch=1,
    )(cidx, g42, r4)


def _slot_sum(name, parts):
    n, r, cols = parts.shape
    cap = max(2 * SUBLANES, (4 * 1024 * 1024) // (n * cols * parts.dtype.itemsize))
    tr = _divisors(r, cap, 2 * SUBLANES)[-1]

    def body(p_ref, o_ref):
        acc = p_ref[0].astype(F32)
        for k in range(1, n):
            acc = acc + p_ref[k].astype(F32)
        o_ref[...] = acc

    return _pcall(
        body, name, _sds((r, cols), F32), grid=(r // tr,), in_specs=[pl.BlockSpec((n, tr, cols), lambda i: (0, i, 0))],
        out_specs=pl.BlockSpec((tr, cols), lambda i: (i, 0)), sem=("parallel",),
    )(parts)


def _slot_sums(name, parts):
    k = len(parts)

    def body(*refs):
        for p_ref, o_ref in zip(refs[:k], refs[k:]):
            acc = p_ref[0].astype(F32)
            for j in range(1, p_ref.shape[0]):
                acc = acc + p_ref[j].astype(F32)
            o_ref[...] = acc

    return list(_pcall(body, name, [_sds(p.shape[1:], F32) for p in parts])(*parts))


def _gather_weights_start(name, shards, token):
    c = lax.axis_index("c")
    halves = [lax.dynamic_slice_in_dim(w, c * (w.shape[0] // 2), w.shape[0] // 2, 0).astype(BF16) for w in shards]
    if token is not None:
        halves[0] = _after(halves[0], token)
    n = len(halves)

    def plan(x, y, c):
        return [(a, (), a, (2 * x + y, c), p, (_chip_of(p), c)) for a in range(n) for p in _chip_peers(x, y, c)]

    return _exchange_start(name + "_ag_chips", halves, [((4, 2) + h.shape, h.dtype) for h in halves], plan)


def _gather_weights_finish(name, starteds, after):
    halves, slots = [], []
    for started in starteds:
        h, s, token = _exchange_wait(started, after)
        halves, slots = halves + h, slots + s
    n = len(halves)
    x, y, c = _position()
    slots = [lax.dynamic_update_slice(s, h[None, None], (2 * x + y, c, 0, 0)) for s, h in zip(slots, halves)]

    def plan_cores(x, y, c):
        return [(a, (s, c), a, (s, c), (x, y, 1 - c), (s, 1 - c)) for a in range(n) for s in range(4)]

    full, _ = _exchange(name + "_ag_cores", slots, None, plan_cores, inplace=True)
    return [f.reshape((4, 2 * h.shape[0], h.shape[1])) for f, h in zip(full, halves)], token


def _reduce_weights_start(name, grads, token):
    n = len(grads)
    g42 = [g.reshape(4, 2, g.shape[1] // 2, g.shape[2]) for g in grads]

    def plan_swap(x, y, c):
        return [(a, (s, 1 - c), a, (s,), (x, y, 1 - c), (s,)) for a in range(n) for s in range(4)]

    def plan_chips(x, y, c):
        return [(a, (_chip_of(p),), a, (2 * x + y,), p, (_chip_of(p),)) for a in range(n) for p in _chip_peers(x, y, c)]

    other, _ = _exchange(name + "_rs_cores", g42, [((4,) + g.shape[2:], g.dtype) for g in g42], plan_swap, after=token)
    pair = [_pair_add(f"{name}_rs_pair{a}", g, o) for a, (g, o) in enumerate(zip(g42, other))]
    return _exchange_start(name + "_rs_chips", pair, [(p.shape, p.dtype) for p in pair], plan_chips)


def _reduce_weights_finish(name, starteds, after):
    pair, recv = [], []
    for started in starteds:
        p, r, token = _exchange_wait(started, after)
        pair, recv = pair + p, recv + r
    n = len(pair)
    x, y, c = _position()
    me = 2 * x + y

    def plan_share(x, y, c):
        return [(a, (c,), a, (c,), (x, y, 1 - c), (1 - c,)) for a in range(n)]

    parts = [lax.dynamic_update_slice(r, lax.dynamic_slice_in_dim(p, me, 1, 0), (me, 0, 0)) for r, p in zip(recv, pair)]
    mine = [_slot_sum(f"{name}_rs_sum{a}", p) for a, p in enumerate(parts)]
    both = [lax.dynamic_update_slice(lax.empty((2,) + m.shape, m.dtype), m[None], (c, 0, 0)) for m in mine]
    both, _ = _exchange(name + "_rs_share", both, None, plan_share, inplace=True)
    return [b.reshape(2 * b.shape[1], b.shape[2]) for b in both], token


def _allreduce_small(name, grads):
    both, token = _gather_two_level(name + "_ag", [g.astype(BF16) if g.size >= SMALL_BF16_SIZE else g for g in grads])
    return _slot_sums(name + "_sum", [g.reshape((8,) + g.shape[2:]) for g in both]), token


def _ada_fwd_call(name, c_all, w, b_shard):
    nl, d, ns = w.shape
    nb = c_all.shape[0]

    def body(c_ref, w_ref, b_ref, o_ref):
        cv = c_ref[...]
        o_ref[0] = _dg(cv * _sigmoid(cv), w_ref[0], ((1,), (0,))) + b_ref[0]

    return _pcall(
        body, name, _sds((nl, nb, ns), F32), grid=(nl,),
        in_specs=[pl.BlockSpec((nb, d), lambda l: (0, 0)), pl.BlockSpec((1, d, ns), lambda l: (l, 0, 0)), pl.BlockSpec((1, 1, ns), lambda l: (l, 0, 0))],
        out_specs=pl.BlockSpec((1, nb, ns), lambda l: (l, 0, 0)), sem=("parallel",),
    )(c_all, w, b_shard)


def _ada_bwd_call(name, c_all, dm_shard, dm_all):
    nl, nb, ns = dm_shard.shape
    d = c_all.shape[1]
    nm = dm_all.shape[2]

    def body(c_ref, ds_ref, da_ref, dw_ref, db_ref):
        cv = c_ref[...]
        dw_ref[0] = _dg(cv * _sigmoid(cv), ds_ref[0], ((0,), (0,)))
        db_ref[0] = jnp.sum(da_ref[0], axis=0, keepdims=True)

    return _pcall(
        body, name, [_sds((nl, d, ns), F32), _sds((nl, 1, nm), F32)], grid=(nl,),
        in_specs=[pl.BlockSpec((nb, d), lambda l: (0, 0)), pl.BlockSpec((1, nb, ns), lambda l: (l, 0, 0)), pl.BlockSpec((1, nb, nm), lambda l: (l, 0, 0))],
        out_specs=[pl.BlockSpec((1, d, ns), lambda l: (l, 0, 0)), pl.BlockSpec((1, 1, nm), lambda l: (l, 0, 0))], sem=("parallel",),
    )(c_all, dm_shard, dm_all)


def _ada_fwd(name, bl, c_all, w, b):
    nl, d, ns = w.shape
    chip = 2 * lax.axis_index("x") + lax.axis_index("y")
    b_shard = lax.dynamic_slice(b, (0, chip * ns), (nl, ns)).reshape(nl, 1, ns)
    shard = _ada_fwd_call(name + "_fwd", c_all, w, b_shard)
    (allc,), token = _gather_chips(name + "_ag", [shard])
    mods = jnp.transpose(allc, (1, 2, 0, 3)).reshape(nl, c_all.shape[0], 4 * ns)
    return lax.dynamic_slice(mods, (0, (2 * chip + lax.axis_index("c")) * bl, 0), (nl, bl, 4 * ns)), token


def _ada_bwd(name, bl, c_all, ns, dm):
    nl = dm.shape[0]
    chip = 2 * lax.axis_index("x") + lax.axis_index("y")
    (dm_all,), token = _gather_two_level(name + "_bwd_ag", [dm])
    dm_all = jnp.transpose(dm_all.reshape((8,) + dm.shape), (1, 0, 2, 3)).reshape(nl, 8 * bl, 4 * ns)
    dm_shard = lax.dynamic_slice(dm_all, (0, 0, chip * ns), (nl, 8 * bl, ns))
    dw, db = _ada_bwd_call(name + "_bwd", c_all, dm_shard, dm_all)
    return dw, db.reshape(nl, 4 * ns), token


def _adamw(name, w, g, m, v):
    shape = w.shape
    cols = shape[-1]
    w2, g2, m2, v2 = (t.reshape(-1, cols) for t in (w, g, m, v))
    rows = w2.shape[0]
    cap = max(SUBLANES, (512 * 1024) // max(cols, 1) // SUBLANES * SUBLANES)
    tr = _divisors(rows, cap, SUBLANES)[-1]

    def body(w_ref, g_ref, m_ref, v_ref, d_ref, mo_ref, vo_ref):
        gv = g_ref[...]
        mn = ADAM_B1 * m_ref[...] + (1.0 - ADAM_B1) * gv
        vn = ADAM_B2 * v_ref[...] + (1.0 - ADAM_B2) * (gv * gv)
        m_hat = mn / (1.0 - ADAM_B1 ** ADAM_STEP)
        v_hat = vn / (1.0 - ADAM_B2 ** ADAM_STEP)
        d_ref[...] = -ADAM_LR * (m_hat / (jnp.sqrt(v_hat) + ADAM_EPS) + ADAM_WD * w_ref[...])
        mo_ref[...] = mn
        vo_ref[...] = vn

    spec = pl.BlockSpec((tr, cols), lambda i: (i, 0))
    outs = _pcall(body, name, [_sds((rows, cols), F32)] * 3, grid=(rows // tr,), in_specs=[spec] * 4, out_specs=[spec] * 3, sem=("parallel",))(w2, g2, m2, v2)
    return tuple(o.reshape(shape) for o in outs)


def _adamw_small(name, ws, gs, ms, vs):
    n = len(ws)

    def body(*refs):
        for k in range(n):
            w_ref, g_ref, m_ref, v_ref = (refs[j * n + k] for j in range(4))
            d_ref, mo_ref, vo_ref = (refs[(4 + j) * n + k] for j in range(3))
            gv = g_ref[...]
            mn = ADAM_B1 * m_ref[...] + (1.0 - ADAM_B1) * gv
            vn = ADAM_B2 * v_ref[...] + (1.0 - ADAM_B2) * (gv * gv)
            m_hat = mn / (1.0 - ADAM_B1 ** ADAM_STEP)
            v_hat = vn / (1.0 - ADAM_B2 ** ADAM_STEP)
            d_ref[...] = -ADAM_LR * (m_hat / (jnp.sqrt(v_hat) + ADAM_EPS) + ADAM_WD * w_ref[...])
            mo_ref[...] = mn
            vo_ref[...] = vn

    outs = _pcall(body, name, [_sds(w.shape, F32) for w in ws] * 3)(*ws, *gs, *ms, *vs)
    return outs[:n], outs[n:2 * n], outs[2 * n:]


def _pad_rows(w, rows):
    return jnp.pad(w, ((0, rows - w.shape[0]), (0, 0)))


def _pad_lanes(v):
    return jnp.pad(v, (0, LANES - v.shape[0])).reshape(1, LANES)


BIG = ("w_in", "w_out", "ff_up", "ff_down")
BIG_AXIS = {"w_in": 1, "w_out": 0, "ff_up": 1, "ff_down": 0}
CONVW = ("ssd_conv_w", "ff_conv_w")
SMALL = ("norm1_g", "norm2_g", "ssd_conv_b", "ssd_dt_bias", "ssd_a_log", "ssd_d", "ssd_norm_g", "gm_vnorm_g", "gm_ws", "gm_bs", "gm_out_g", "ff_conv_b")
WEIGHTS = ("ada_w", "ada_b", "norm1_g", "norm2_g", "w_in", "ssd_conv_w", "ssd_conv_b", "ssd_dt_bias", "ssd_a_log", "ssd_d", "ssd_norm_g", "gm_vnorm_g", "gm_ws", "gm_bs", "gm_out_g", "w_out", "ff_up", "ff_conv_w", "ff_conv_b", "ff_down", "final_g")


def kernel(x, c, ada_w, ada_b, norm1_g, norm2_g, w_in, ssd_conv_w, ssd_conv_b, ssd_dt_bias, ssd_a_log, ssd_d, ssd_norm_g, gm_vnorm_g, gm_ws, gm_bs, gm_out_g, w_out, ff_up, ff_conv_w, ff_conv_b, ff_down, final_g, loss_target, m_ada_w, m_ada_b, m_norm1_g, m_norm2_g, m_w_in, m_ssd_conv_w, m_ssd_conv_b, m_ssd_dt_bias, m_ssd_a_log, m_ssd_d, m_ssd_norm_g, m_gm_vnorm_g, m_gm_ws, m_gm_bs, m_gm_out_g, m_w_out, m_ff_up, m_ff_conv_w, m_ff_conv_b, m_ff_down, m_final_g, v_ada_w, v_ada_b, v_norm1_g, v_norm2_g, v_w_in, v_ssd_conv_w, v_ssd_conv_b, v_ssd_dt_bias, v_ssd_a_log, v_ssd_d, v_ssd_norm_g, v_gm_vnorm_g, v_gm_ws, v_gm_bs, v_gm_out_g, v_w_out, v_ff_up, v_ff_conv_w, v_ff_conv_b, v_ff_down, v_final_g):
    given = dict(locals())
    weights = {n: given[n] for n in WEIGHTS}
    bl, s, d = x.shape
    nl = ada_w.shape[0]
    heads = d // HEAD_DIM
    cd = d + 2 * SSD_GROUPS * STATE
    f = ff_down.shape[1] * 4
    n_in = d + cd + heads + 2 * d
    n_proj = _round_up(3 * d + cd + LANES, 2 * LANES)
    tm = _divisors(s, 512)[-1]

    gathering = {}

    def start_gathers(l, names, token):
        for n in names:
            gathering[l, n] = _gather_weights_start(f"wg{l}_{n}", [weights[n][l]], token)
            token = gathering[l, n]["token"]
        return token

    pre, _ = _gather_two_level("pre_ag", [c] + [weights[n] for n in CONVW])
    c_all = pre[0].reshape(8 * bl, d)
    conv_full = {n: jnp.concatenate([p[k, 0] for k in range(4)], axis=-1) for n, p in zip(CONVW, pre[1:])}
    mods_all, ada_token = _ada_fwd("ada", bl, c_all, ada_w, ada_b)
    mods = mods_all.reshape(nl, bl, N_MOD, 1, d)

    ready = {}

    def assembled(n, full):
        if n == "w_in":
            return full
        return full.reshape(-1, full.shape[2]) if BIG_AXIS[n] == 0 else jnp.concatenate([full[k] for k in range(4)], axis=1)

    def landed(l, n, after):
        if l > 0 and (l, n) not in ready:
            fulls, token = _gather_weights_finish(f"wg{l}", [gathering.pop((l, m)) for m in BIG], after)
            ready.update({(l, m): (assembled(m, t), token) for m, t in zip(BIG, fulls)})
        if (l, n) in ready:
            return ready.pop((l, n))
        (full,), token = _gather_weights_finish(f"wg{l}_{n}", [gathering.pop((l, n))], after)
        return assembled(n, full), token

    issued = start_gathers(0, BIG, ada_token)
    r2 = lambda v: v.reshape(1, -1)
    n_gm = d // GM_HEAD
    fg = r2(final_g)

    def layer_params(l):
        return dict(
            norm1=r2(norm1_g[l]), norm2=r2(norm2_g[l]), conv_w8=_pad_rows(conv_full["ssd_conv_w"][l], SUBLANES), conv_b=r2(ssd_conv_b[l]),
            dtb=_pad_lanes(ssd_dt_bias[l]), alog=_pad_lanes(ssd_a_log[l]), dsk=r2(jnp.repeat(ssd_d[l], HEAD_DIM)), ng=r2(ssd_norm_g[l]),
            vg=r2(gm_vnorm_g[l]), ws=gm_ws[l], bs_t=jnp.pad(gm_bs[l].T, ((0, 0), (0, LANES - n_gm))), og=r2(gm_out_g[l]),
            ff_w8=_pad_rows(conv_full["ff_conv_w"][l], SUBLANES), ff_b=r2(ff_conv_b[l]))

    def mixer_args(p):
        return (p["conv_w8"], p["conv_b"], p["dtb"], p["alog"], p["dsk"], p["ng"], p["vg"], p["ws"], p["bs_t"], p["og"])

    def padded_w_in(full):
        ns = full.shape[2]

        def cols(a, b):
            return [full[k][:, max(a, k * ns) - k * ns:min(b, (k + 1) * ns) - k * ns] for k in range(4) if max(a, k * ns) < min(b, (k + 1) * ns)]

        parts = cols(0, d) + cols(d + cd + heads, n_in) + cols(d, d + cd) + cols(d + cd, d + cd + heads)
        return jnp.concatenate(parts + [jnp.zeros((d, n_proj - (3 * d + cd + heads)), BF16)], axis=1)

    saved, xcur, pending = [], x, None
    for l in range(nl):
        p, w = layer_params(l), {}
        wi, token = landed(l, "w_in", mods_all if l == 0 else pending[0])
        w["w_in"] = padded_w_in(wi)
        sh1, sc1, g1, sh2, sc2, g2 = (mods[l, :, k] for k in range(N_MOD))
        if l + 1 < nl:
            issued = start_gathers(l + 1, BIG, token + issued)
        sc1 = _after(sc1, issued)
        if pending is None:
            (h,) = _row_fwd(f"nm{l}_fwd", _nm_tile, [(xcur, d, 0, None)], [sc1, sh1], [p["norm1"]], [(d, BF16)], 0, tm)
            x1 = xcur
        else:
            rows = [(xcur, d, 0, None), (pending[0], d, 0, None)]
            x1, h = _row_fwd(f"rnm{l}a_fwd", _rnm_tile, rows, [pending[1], sc1, sh1], [p["norm1"]], [(d, F32), (d, BF16)], 0, tm)
        proj = _mm(f"win{l}_fwd", h, w["w_in"], F32)
        ycat, (xbc, hprev) = _mixer_fwd(f"mix{l}", proj, d, cd, *mixer_args(p))
        w["w_out"], _ = landed(l, "w_out", ycat)
        mix = _mm(f"wout{l}_fwd", ycat, w["w_out"], F32)
        x2, h2 = _row_fwd(f"rnm{l}b_fwd", _rnm_tile, [(x1, d, 0, None), (mix, d, 0, None)], [g1, sc2, sh2], [p["norm2"]], [(d, F32), (d, BF16)], 0, tm)
        w_up, _ = landed(l, "ff_up", h2)
        w["ff_up"] = _gate_value_blocks(f"ffup{l}_blocks", w_up, f)
        up = _mm(f"ffup{l}_fwd", h2, w["ff_up"], F32)
        act = _conv_fwd(f"ffact{l}_fwd", up, 0, f, p["ff_w8"], p["ff_b"], FF_CONV, True, BF16)
        w["ff_down"], _ = landed(l, "ff_down", act)
        down = _mm(f"ffdown{l}_fwd", act, w["ff_down"], F32)
        saved.append(dict(p=p, w=w, x_in=xcur, pending=pending, h=h, proj=proj, xbc=xbc, hprev=hprev, ycat=ycat, x1=x1, mix=mix, h2=h2, up=up, act=act))
        xcur, pending = x2, (down, g2)
    rows = [(xcur, d, 0, F32), (pending[0], d, 0, BF16), (loss_target, d, 0, None)]

    (dx2, ddown), (dg2,), (dfg,), (loss_local,) = _row_bwd(
        "final_bwd", _final_tile, rows, [pending[1]], [fg], [], 1, tm, [jnp.ones((1, 1), F32)], emit_sums=True)
    loss = lax.psum(loss_local[0, 0], AXES)
    dmods, small_grads = [None] * nl, [None] * nl
    reducing = {}

    def start_reduce(l, named, token=None):
        by_chip = []
        for n, g in named:
            width = g.shape[1] // 4
            by_chip.append(g.reshape(4, g.shape[0] // 4, g.shape[1]) if BIG_AXIS[n] == 0 else jnp.stack([g[:, k * width:(k + 1) * width] for k in range(4)]))
        key = (l, tuple(n for n, _ in named))
        reducing[key] = _reduce_weights_start(f"wg{l}_" + "_".join(key[1]), by_chip, token)
        return reducing[key]["token"]

    for l in reversed(range(nl)):
        sv = saved[l]
        p, w = sv["p"], sv["w"]
        sh1, sc1, g1, sh2, sc2, g2 = (mods[l, :, k] for k in range(N_MOD))
        dact, dw_down = _mm_bwd(f"ffdown{l}", sv["act"], w["ff_down"], ddown)
        dup, dff_w8, dff_b = _conv_bwd(f"ffact{l}_bwd", sv["up"], 0, f, p["ff_w8"], p["ff_b"], FF_CONV, True, dact, BF16)
        dh2, dw_up = _mm_bwd(f"ffup{l}", sv["h2"], w["ff_up"], dup)
        dw_up = _gate_value_blocks(f"ffup{l}_columns", dw_up, f, inverse=True)
        sc2 = _after(sc2, start_reduce(l, [("ff_down", dw_down), ("ff_up", dw_up)]))
        rows = [(sv["x1"], d, 0, F32), (sv["mix"], d, 0, BF16)]
        (dx1, dmix), (dg1, dsc2, dsh2), (dn2,) = _row_bwd(f"rnm{l}b_bwd", _rnm_tile, rows, [g1, sc2, sh2], [p["norm2"]], [(d, F32), (d, BF16)], 0, tm, [dx2, dh2])
        dycat, dw_out = _mm_bwd(f"wout{l}", sv["ycat"], w["w_out"], dmix)
        p_tied = p
        if l == 0:
            p_tied = dict(p, dtb=_after(p["dtb"], start_reduce(l, [("w_out", dw_out)])))
        dproj, (dw8, dcb, ddtb, dalog, ddsk, dng, dvg, dws, dbs_t, dog) = _mixer_bwd(f"mix{l}", sv["proj"], d, cd, *mixer_args(p_tied), sv["xbc"], sv["hprev"], dycat)
        dh, dw_in_p = _mm_bwd(f"win{l}", sv["h"], w["w_in"], dproj)
        dw_in = jnp.concatenate([dw_in_p[:, :d], dw_in_p[:, 3 * d:3 * d + cd], dw_in_p[:, 3 * d + cd:3 * d + cd + heads], dw_in_p[:, d:3 * d]], axis=1)
        if l > 0:
            sc1 = _after(sc1, start_reduce(l, [("w_out", dw_out), ("w_in", dw_in)]))
        if sv["pending"] is None:
            (dx2,), (dsc1, dsh1), (dn1,) = _row_bwd(f"nm{l}_bwd", _nm_res_tile, [(sv["x_in"], d, 0, F32)], [sc1, sh1], [p["norm1"]], [(d, F32), (d, BF16)], 0, tm, [dx1, dh])
        else:
            rows = [(sv["x_in"], d, 0, F32), (sv["pending"][0], d, 0, BF16)]
            (dx2, ddown), (dg2_prev, dsc1, dsh1), (dn1,) = _row_bwd(
                f"rnm{l}a_bwd", _rnm_tile, rows, [sv["pending"][1], sc1, sh1], [p["norm1"]], [(d, F32), (d, BF16)], 0, tm, [dx1, dh])
        dmods[l] = jnp.concatenate([dsh1, dsc1, dg1, dsh2, dsc2, dg2], axis=1).reshape(bl, N_MOD * d)
        if sv["pending"] is not None:
            dg2 = dg2_prev
        small_grads[l] = dict(
            norm1_g=dn1.reshape(d), norm2_g=dn2.reshape(d), ssd_conv_b=dcb.reshape(cd), ssd_dt_bias=ddtb[0, :heads], ssd_a_log=dalog[0, :heads],
            ssd_d=ddsk.reshape(heads, HEAD_DIM).sum(-1), ssd_norm_g=dng.reshape(d), gm_vnorm_g=dvg.reshape(d), gm_ws=dws, gm_bs=dbs_t[:, :n_gm].T,
            gm_out_g=dog.reshape(d), ff_conv_b=dff_b.reshape(f), ssd_conv_w=dw8[:SSD_CONV], ff_conv_w=dff_w8[:FF_CONV])
    grad_x = dx2

    g_ada_w, g_ada_b, ada_token = _ada_bwd("ada", bl, c_all, ada_w.shape[2], jnp.stack(dmods))
    small_names_r = SMALL + CONVW + ("final_g",)
    summed, small_token = _allreduce_small("small", [jnp.stack([small_grads[l][n] for l in range(nl)]) for n in SMALL + CONVW] + [dfg])
    travelling = start_reduce(0, [("w_in", dw_in)], ada_token + small_token)
    grad = {"ada_w": g_ada_w, "ada_b": g_ada_b}
    grad.update(zip(small_names_r, summed))
    chip = 2 * lax.axis_index("x") + lax.axis_index("y")
    for n in CONVW:
        width = weights[n].shape[-1]
        grad[n] = lax.dynamic_slice_in_dim(grad[n], chip * width, width, axis=2)

    delta, new_m, new_v = {}, {}, {}
    last = (0, ("w_in",))
    early = [k for k in reducing if k != last]
    finished, _ = _reduce_weights_finish("wg_early", [reducing[k] for k in early], _after(summed[0], travelling))
    big_grads = dict(zip([(l, n) for l, names in early for n in names], finished))
    order = ("ada_w",) + tuple(reversed(BIG))
    for n in order:
        if n == "w_in":
            (big_grads[0, n],), _ = _reduce_weights_finish("wg_last", [reducing[last]], delta[order[-2]])
        if n != "ada_w":
            grad[n] = jnp.stack([big_grads[l, n] for l in range(nl)])
        delta[n], new_m[n], new_v[n] = _adamw("adam_" + n, weights[n], grad[n], given["m_" + n], given["v_" + n])
    small_names = ("ada_b",) + SMALL + CONVW + ("final_g",)
    as2d = lambda t: t.reshape(1, -1) if t.ndim == 1 else t
    res = _adamw_small(
        "adam_small", [as2d(weights[n]) for n in small_names], [as2d(grad[n]) for n in small_names],
        [as2d(given["m_" + n]) for n in small_names], [as2d(given["v_" + n]) for n in small_names])
    for out, vals in zip((delta, new_m, new_v), res):
        for n, val in zip(small_names, vals):
            out[n] = val.reshape(weights[n].shape)
    grad["final_g"] = grad["final_g"].reshape(final_g.shape)

    return (loss, grad_x, *[grad[n] for n in WEIGHTS], *[delta[n] for n in WEIGHTS], *[new_m[n] for n in WEIGHTS], *[new_v[n] for n in WEIGHTS])
```

```python
import jax
import jax.numpy as jnp
from jax import lax
from jax.experimental import pallas as pl
from jax.experimental.pallas import tpu as pltpu

F32 = jnp.float32
BF16 = jnp.bfloat16
EPS = 1e-6
CHUNK = 128
HEAD_DIM = 64
STATE = 128
GM_HEAD = 128
SSD_GROUPS = 2
SSD_CONV = 4
FF_CONV = 3
N_MOD = 6
LANES = 128
SUBLANES = 8
SMALL_BF16_SIZE = 1 << 16
ROW_SUB = 16
CONV_LANES = 256
CONV_ROWS = 32
V7X_VMEM_LIMIT = 48 * 1024 * 1024
MM_VMEM_BUDGET = 30 * 1024 * 1024
MM_STEP_MACS = 2048 * 1024 * 1024
MM_ACC_SECONDS_PER_VREG = 1.4e-9
V7X_MXU_FLOPS = 996e12
V7X_HBM_BYTES_PER_S = 3.3e12
V7X_STEP_SECONDS = 0.35e-6
ADAM_LR, ADAM_B1, ADAM_B2, ADAM_EPS, ADAM_WD, ADAM_STEP = 0.001, 0.9, 0.999, 1e-08, 0.01, 10
MESH = pl.DeviceIdType.MESH
AXES = ("x", "y", "c")


def _round_up(n, m):
    return (n + m - 1) // m * m


def _divisors(n, cap, mult=LANES):
    out = [t for t in range(mult, min(n, cap) + 1, mult) if n % t == 0]
    return out or [n]


def _pcall(body, name, out_shape, grid=(), in_specs=None, out_specs=None, scratch=(), sem=None, prefetch=0, aliases=None):
    params = pltpu.CompilerParams(dimension_semantics=sem, vmem_limit_bytes=V7X_VMEM_LIMIT)
    if aliases:
        return pl.pallas_call(
            body, name=name, out_shape=out_shape, grid=grid, in_specs=in_specs, out_specs=out_specs, scratch_shapes=list(scratch),
            compiler_params=params, input_output_aliases=aliases)
    if prefetch:
        spec = pltpu.PrefetchScalarGridSpec(num_scalar_prefetch=prefetch, grid=grid, in_specs=in_specs, out_specs=out_specs, scratch_shapes=list(scratch))
        return pl.pallas_call(body, name=name, out_shape=out_shape, grid_spec=spec, compiler_params=params)
    if in_specs is None:
        return pl.pallas_call(body, name=name, out_shape=out_shape, compiler_params=params)
    return pl.pallas_call(body, name=name, out_shape=out_shape, grid=grid, in_specs=in_specs, out_specs=out_specs, scratch_shapes=list(scratch), compiler_params=params)


def _sds(shape, dtype):
    return jax.ShapeDtypeStruct(tuple(shape), dtype)


def _mm_tiles(m, n, k, a_bytes, b_bytes, o_bytes):
    best, best_key = None, None
    for tm in _divisors(m, 2048):
        for tn in _divisors(n, 2560):
            for tk in _divisors(k, 2560):
                vmem = 2 * (tm * tk * a_bytes + tk * tn * b_bytes + tm * tn * o_bytes) + tm * tn * 4
                if vmem > MM_VMEM_BUDGET or tm * tn * tk > MM_STEP_MACS:
                    continue
                ni, nj, nk = m // tm, n // tn, k // tk
                a_reads = 1 if nk == 1 else nj
                b_reads = 1 if (nk == 1 and nj == 1) else ni
                hbm = a_reads * m * k * a_bytes + b_reads * k * n * b_bytes + m * n * o_bytes
                t = max(2.0 * m * n * k / V7X_MXU_FLOPS, hbm / V7X_HBM_BYTES_PER_S) + ni * nj * nk * V7X_STEP_SECONDS
                if nk > 1:
                    t += ni * nj * nk * (tm * tn // (SUBLANES * LANES)) * MM_ACC_SECONDS_PER_VREG
                key = (-t, tm * tn * tk)
                if best_key is None or key > best_key:
                    best, best_key = (tm, tn, tk), key
    return best


def _matmul(name, a, b, mode, out_dtype):
    if mode == "nn":
        (m, k), n = a.shape, b.shape[1]
    elif mode == "nt":
        (m, k), n = a.shape, b.shape[0]
    else:
        (k, m), n = a.shape, b.shape[1]
    tm, tn, tk = _mm_tiles(m, n, k, a.dtype.itemsize, b.dtype.itemsize, jnp.dtype(out_dtype).itemsize)
    nk = k // tk
    if mode == "nn":
        a_spec = pl.BlockSpec((tm, tk), lambda i, j, kk: (i, kk))
        b_spec = pl.BlockSpec((tk, tn), lambda i, j, kk: (kk, j))
        dims = ((1,), (0,))
    elif mode == "nt":
        a_spec = pl.BlockSpec((tm, tk), lambda i, j, kk: (i, kk))
        b_spec = pl.BlockSpec((tn, tk), lambda i, j, kk: (j, kk))
        dims = ((1,), (1,))
    else:
        a_spec = pl.BlockSpec((tk, tm), lambda i, j, kk: (kk, i))
        b_spec = pl.BlockSpec((tk, tn), lambda i, j, kk: (kk, j))
        dims = ((0,), (0,))

    def body(a_ref, b_ref, o_ref, acc_ref):
        kk = pl.program_id(2)
        p = lax.dot_general(a_ref[...].astype(BF16), b_ref[...].astype(BF16), (dims, ((), ())), preferred_element_type=F32)
        if nk == 1:
            o_ref[...] = p.astype(o_ref.dtype)
        else:
            @pl.when(kk == 0)
            def _():
                acc_ref[...] = p

            @pl.when(kk > 0)
            def _():
                acc_ref[...] += p

            @pl.when(kk == nk - 1)
            def _():
                o_ref[...] = acc_ref[...].astype(o_ref.dtype)

    return _pcall(
        body, name, _sds((m, n), out_dtype), grid=(m // tm, n // tn, nk), in_specs=[a_spec, b_spec],
        out_specs=pl.BlockSpec((tm, tn), lambda i, j, kk: (i, j)), scratch=[pltpu.VMEM((tm, tn), F32)],
        sem=("parallel", "parallel", "arbitrary"),
    )(a, b)


def _mm(name, a, w, out_dtype):
    return _matmul(name, a.reshape(-1, a.shape[-1]), w, "nn", out_dtype).reshape(a.shape[:-1] + (w.shape[1],))


def _mm_bwd(name, a, w, dy):
    a2, dy2 = a.reshape(-1, a.shape[-1]), dy.reshape(-1, dy.shape[-1])
    return _matmul(name + "_dx", dy2, w, "nt", BF16).reshape(a.shape), _matmul(name + "_dw", a2, dy2, "tn", BF16)


def _dg(a, b, dims):
    return lax.dot_general(a.astype(BF16), b.astype(BF16), (dims, ((), ())), preferred_element_type=F32)


@jax.custom_vjp
def _dot_nn(a, b):
    return _dg(a, b, ((1,), (0,)))


_dot_nn.defvjp(lambda a, b: (_dot_nn(a, b), (a, b)), lambda r, d: (_dg(d, r[1], ((1,), (1,))), _dg(r[0], d, ((0,), (0,)))))


@jax.custom_vjp
def _dot_nt(a, b):
    return _dg(a, b, ((1,), (1,)))


_dot_nt.defvjp(lambda a, b: (_dot_nt(a, b), (a, b)), lambda r, d: (_dg(d, r[1], ((1,), (0,))), _dg(d, r[0], ((0,), (0,)))))


@jax.custom_vjp
def _dot_tn(a, b):
    return _dg(a, b, ((0,), (0,)))


_dot_tn.defvjp(lambda a, b: (_dot_tn(a, b), (a, b)), lambda r, d: (_dg(r[1], d, ((1,), (1,))), _dg(r[0], d, ((1,), (0,)))))


def _exact_dot(a, c, dims):
    hi = a.astype(BF16)
    r1 = a - hi.astype(F32)
    mid = r1.astype(BF16)
    lo = (r1 - mid.astype(F32)).astype(BF16)
    cb = c.astype(BF16)
    f = lambda t: lax.dot_general(t, cb, (dims, ((), ())), preferred_element_type=F32)
    return f(hi) + f(mid) + f(lo)


@jax.custom_vjp
def _sel_right(a, c):
    return _exact_dot(a, c, ((1,), (0,)))


_sel_right.defvjp(lambda a, c: (_sel_right(a, c), c), lambda c, d: (_exact_dot(d, c, ((1,), (1,))), jnp.zeros_like(c)))


def _exact_dot_left(c, a, dims):
    hi = a.astype(BF16)
    r1 = a - hi.astype(F32)
    mid = r1.astype(BF16)
    lo = (r1 - mid.astype(F32)).astype(BF16)
    cb = c.astype(BF16)
    f = lambda t: lax.dot_general(cb, t, (dims, ((), ())), preferred_element_type=F32)
    return f(hi) + f(mid) + f(lo)


@jax.custom_vjp
def _sel_left(c, a):
    return _exact_dot_left(c, a, ((1,), (0,)))


_sel_left.defvjp(lambda c, a: (_sel_left(c, a), c), lambda c, d: (jnp.zeros_like(c), _exact_dot_left(c, d, ((0,), (0,)))))


def _sigmoid(x):
    return 0.5 * jnp.tanh(0.5 * x) + 0.5


def _rms(x, g):
    return x * lax.rsqrt(jnp.mean(x * x, axis=-1, keepdims=True) + EPS) * g


def _gelu(x):
    return 0.5 * x * (1.0 + lax.erf(x * (2.0 ** -0.5)))


def _causal(n):
    return lax.broadcasted_iota(jnp.int32, (n, n), 0) >= lax.broadcasted_iota(jnp.int32, (n, n), 1)


def _row_in_specs(rows, bparams, gparams, tm):
    specs = [pl.BlockSpec((1, tm, w), lambda b, i, cb=cb: (b, i, cb)) for (_, w, cb, _) in rows]
    specs += [pl.BlockSpec((1, 1, p.shape[-1]), lambda b, i: (b, 0, 0)) for p in bparams]
    specs += [pl.BlockSpec(p.shape, lambda b, i, n=p.ndim: (0,) * n) for p in gparams]
    return specs


def _row_vals(refs, n_rows, n_b, n_g):
    vals = [r[0].astype(F32) for r in refs[:n_rows]]
    vals += [r[0].astype(F32) for r in refs[n_rows:n_rows + n_b]]
    vals += [r[...].astype(F32) for r in refs[n_rows + n_b:n_rows + n_b + n_g]]
    return vals


def _row_fwd(name, tile, rows, bparams, gparams, outs, n_sum, tm, into=None):
    bl, s = rows[0][0].shape[:2]
    n_in = len(rows) + len(bparams) + len(gparams) + (0 if into is None else 1)

    def body(*refs):
        first = (pl.program_id(0) == 0) & (pl.program_id(1) == 0)
        res = tile(*_row_vals(refs, len(rows), len(bparams), len(gparams)))
        o_refs = refs[n_in:]
        for k in range(len(outs)):
            o_refs[k][0] = res[k].astype(o_refs[k].dtype)
        for k in range(n_sum):
            o_ref, val = o_refs[len(outs) + k], res[len(outs) + k]

            @pl.when(first)
            def _(o_ref=o_ref, val=val):
                o_ref[...] = val

            @pl.when(jnp.logical_not(first))
            def _(o_ref=o_ref, val=val):
                o_ref[...] += val

    out_shape = [_sds((bl, s, w), dt) for (w, dt) in outs] + [_sds((1, 1), F32)] * n_sum
    out_specs = [pl.BlockSpec((1, tm, w), lambda b, i: (b, i, 0)) for (w, _) in outs] + [pl.BlockSpec((1, 1), lambda b, i: (0, 0))] * n_sum
    in_specs = _row_in_specs(rows, bparams, gparams, tm)
    args = [r[0] for r in rows] + list(bparams) + list(gparams)
    if into is None:
        return _pcall(body, name, out_shape, grid=(bl, s // tm), in_specs=in_specs, out_specs=out_specs, sem=("arbitrary", "arbitrary"))(*args)
    dest, col = into
    out_shape[0] = _sds(dest.shape, dest.dtype)
    out_specs[0] = pl.BlockSpec((1, tm, outs[0][0]), lambda b, i: (b, i, col))
    return _pcall(
        body, name, out_shape, grid=(bl, s // tm), in_specs=in_specs + [pl.BlockSpec(memory_space=pl.ANY)], out_specs=out_specs,
        sem=("arbitrary", "arbitrary"), aliases={len(args): 0},
    )(*args, dest)


def _row_bwd(name, tile, rows, bparams, gparams, outs, n_sum, tm, cts, emit_sums=False, sub_rows=None):
    bl, s = rows[0][0].shape[:2]
    n_r, n_b, n_g = len(rows), len(bparams), len(gparams)
    n_extra = n_sum if emit_sums else 0
    n_in = n_r + n_b + n_g
    n_ct = len(outs) + n_sum
    grad_rows = [k for k in range(n_r) if rows[k][3]]

    sub = sub_rows or tm

    def body(*refs):
        b, i = pl.program_id(0), pl.program_id(1)
        ct_refs = refs[n_in:n_in + n_ct]
        o_refs = refs[n_in + n_ct:]
        params = [r[0].astype(F32) for r in refs[n_r:n_r + n_b]] + [r[...].astype(F32) for r in refs[n_r + n_b:n_in]]
        sum_cts = [r[...] for r in ct_refs[len(outs):]]

        def step(c, acc):
            r0 = pl.multiple_of(c * sub, sub)
            vals = [r[0, pl.ds(r0, sub), :].astype(F32) for r in refs[:n_r]]
            ct = [r[0, pl.ds(r0, sub), :].astype(F32) for r in ct_refs[:len(outs)]] + sum_cts
            primal, vjp = jax.vjp(tile, *vals, *params)
            g = tuple(vjp(tuple(ct))) + (tuple(primal[len(outs):]) if emit_sums else ())
            for j, k in enumerate(grad_rows):
                o_refs[j][0, pl.ds(r0, sub), :] = g[k].astype(o_refs[j].dtype)
            return tuple(a + t for a, t in zip(acc, g[n_r:]))

        zeros = tuple(jnp.zeros_like(p) for p in params) + tuple(jnp.zeros((1, 1), F32) for _ in range(n_extra))
        grads = (None,) * n_r + tuple(lax.fori_loop(0, tm // sub, step, zeros))
        for k in range(n_b):
            o_ref, val = o_refs[len(grad_rows) + k], grads[n_r + k]

            @pl.when(i == 0)
            def _(o_ref=o_ref, val=val):
                o_ref[0] = val

            @pl.when(i > 0)
            def _(o_ref=o_ref, val=val):
                o_ref[0] += val

        first = (b == 0) & (i == 0)
        for k in range(n_g + n_extra):
            o_ref, val = o_refs[len(grad_rows) + n_b + k], grads[n_r + n_b + k]

            @pl.when(first)
            def _(o_ref=o_ref, val=val):
                o_ref[...] = val

            @pl.when(jnp.logical_not(first))
            def _(o_ref=o_ref, val=val):
                o_ref[...] += val

    one = pl.BlockSpec((1, 1), lambda b, i: (0, 0))
    in_specs = _row_in_specs(rows, bparams, gparams, tm)
    in_specs += [pl.BlockSpec((1, tm, w), lambda b, i: (b, i, 0)) for (w, _) in outs] + [one] * n_sum
    out_shape = [_sds((bl, s, rows[k][1]), rows[k][3]) for k in grad_rows]
    out_shape += [_sds(p.shape, F32) for p in bparams] + [_sds(p.shape, F32) for p in gparams] + [_sds((1, 1), F32)] * n_extra
    out_specs = [pl.BlockSpec((1, tm, rows[k][1]), lambda b, i: (b, i, 0)) for k in grad_rows]
    out_specs += [pl.BlockSpec((1, 1, p.shape[-1]), lambda b, i: (b, 0, 0)) for p in bparams]
    out_specs += [pl.BlockSpec(p.shape, lambda b, i, n=p.ndim: (0,) * n) for p in gparams] + [one] * n_extra
    res = _pcall(
        body, name, out_shape, grid=(bl, s // tm), in_specs=in_specs, out_specs=out_specs, sem=("arbitrary", "arbitrary"),
    )(*[r[0] for r in rows], *bparams, *gparams, *cts)
    n0, n1, n2 = len(grad_rows), len(grad_rows) + n_b, len(grad_rows) + n_b + n_g
    return (res[:n0], res[n0:n1], res[n1:n2]) + ((res[n2:],) if emit_sums else ())


def _nm_tile(x, sc, sh, g):
    return (_rms(x, g) * (1.0 + sc) + sh,)


def _nm_res_tile(x, sc, sh, g):
    return x, _rms(x, g) * (1.0 + sc) + sh


def _rnm_tile(x, o, gate, sc, sh, g):
    xn = x + gate * o
    return xn, _rms(xn, g) * (1.0 + sc) + sh


def _final_tile(x, o, tgt, gate, g):
    e = _rms(x + gate * o, g) - tgt
    return (0.5 * jnp.sum(jnp.mean(e * e, axis=-1, keepdims=True), axis=0, keepdims=True),)


def _gm_tile(u_in, v_in, vg, ws, bs_t, og):
    d = u_in.shape[1]
    u, vn = _gelu(u_in), _rms(_gelu(v_in), vg)
    causal = _causal(CHUNK)
    lane = lax.broadcasted_iota(jnp.int32, (1, LANES), 1)
    parts = []
    for h in range(d // GM_HEAD):
        bias = jnp.sum(bs_t * (lane == h).astype(F32), axis=1, keepdims=True)
        parts.append(_dot_nn(jnp.where(causal, ws[h], 0.0), vn[:, h * GM_HEAD:(h + 1) * GM_HEAD]) + bias)
    return (_rms(u * jnp.concatenate(parts, axis=1), og),)


def _conv_window(ref, r0, rows, c0, tc, seq, before, after):
    parts = []
    if before:
        p0 = pl.multiple_of(jnp.maximum(r0 - SUBLANES, 0), SUBLANES)
        parts.append(jnp.where(r0 > 0, ref[0, pl.ds(p0, SUBLANES), pl.ds(c0, tc)].astype(F32), 0.0))
    parts.append(ref[0, pl.ds(r0, rows), pl.ds(c0, tc)].astype(F32))
    if after:
        n0 = pl.multiple_of(jnp.minimum(r0 + rows, seq - SUBLANES), SUBLANES)
        parts.append(jnp.where(r0 + rows < seq, ref[0, pl.ds(n0, SUBLANES), pl.ds(c0, tc)].astype(F32), 0.0))
    return jnp.concatenate(parts, axis=0) if len(parts) > 1 else parts[0]


def _conv_taps_pre(xe, w, b, taps, rows):
    pre = xe[SUBLANES:SUBLANES + rows] * w[taps - 1] + b
    for j in range(1, taps):
        pre = pre + pltpu.roll(xe, j, 0)[SUBLANES:SUBLANES + rows] * w[taps - 1 - j]
    return pre


def _conv_fwd(name, src, col0, chans, w8, b, taps, gated, out_dtype):
    bl, s, _ = src.shape
    tc = CONV_LANES
    xw = 2 * tc if gated else tc

    def body(x_ref, w_ref, b_ref, o_ref):
        w = [w_ref[k:k + 1, :] for k in range(taps)]
        bias = b_ref[...]

        def step(c, carry):
            r0 = pl.multiple_of(c * CONV_ROWS, CONV_ROWS)
            xe = _conv_window(x_ref, r0, CONV_ROWS, 0, tc, s, True, False)
            pre = _conv_taps_pre(xe, w, bias, taps, CONV_ROWS)
            y = pre * _sigmoid(pre)
            if gated:
                y = y * x_ref[0, pl.ds(r0, CONV_ROWS), pl.ds(tc, tc)]
            o_ref[0, pl.ds(r0, CONV_ROWS), :] = y.astype(o_ref.dtype)
            return carry

        lax.fori_loop(0, s // CONV_ROWS, step, 0, unroll=4)

    first = 0 if gated else col0 // tc
    return _pcall(
        body, name, _sds((bl, s, chans), out_dtype), grid=(chans // tc, bl),
        in_specs=[pl.BlockSpec((1, s, xw), lambda j, bb: (bb, 0, first + j)), pl.BlockSpec((SUBLANES, tc), lambda j, bb: (0, j)), pl.BlockSpec((1, tc), lambda j, bb: (0, j))],
        out_specs=pl.BlockSpec((1, s, tc), lambda j, bb: (bb, 0, j)), sem=("parallel", "arbitrary"),
    )(src, w8, b)


def _conv_bwd(name, src, col0, chans, w8, b, taps, gated, dy, dx_dtype):
    bl, s, _ = src.shape
    tc = CONV_LANES
    xw = 2 * tc if gated else tc
    ext = CONV_ROWS + SUBLANES
    pieces = list(dy) if isinstance(dy, (list, tuple)) else [dy]
    starts = [sum(p.shape[2] for p in pieces[:k]) // tc for k in range(len(pieces))]

    def body(x_ref, w_ref, b_ref, *rest):
        dy_refs, (dx_ref, dw_ref, db_ref) = rest[:len(pieces)], rest[len(pieces):]
        j, bb = pl.program_id(0), pl.program_id(1)
        w = [w_ref[k:k + 1, :] for k in range(taps)]
        bias = b_ref[...]

        def dy_window(r0):
            d = _conv_window(dy_refs[0], r0, CONV_ROWS, 0, tc, s, False, True)
            for k in range(1, len(pieces)):
                d = jnp.where(j >= starts[k], _conv_window(dy_refs[k], r0, CONV_ROWS, 0, tc, s, False, True), d)
            return d

        def fold(v):
            acc = v[0:SUBLANES]
            for i in range(1, CONV_ROWS // SUBLANES):
                acc = acc + v[i * SUBLANES:(i + 1) * SUBLANES]
            return acc

        def step(c, carry):
            r0 = pl.multiple_of(c * CONV_ROWS, CONV_ROWS)
            xe = _conv_window(x_ref, r0, CONV_ROWS, 0, tc, s, True, True)
            pre = _conv_taps_pre(xe, w, bias, taps, ext)
            sig = _sigmoid(pre)
            d = dy_window(r0)
            dsil = d * _conv_window(x_ref, r0, CONV_ROWS, tc, tc, s, False, True) if gated else d
            dpre = dsil * sig * (1.0 + pre * (1.0 - sig))
            dx = dpre[:CONV_ROWS] * w[taps - 1]
            for j in range(1, taps):
                dx = dx + pltpu.roll(dpre, ext - j, 0)[:CONV_ROWS] * w[taps - 1 - j]
            dx_ref[0, pl.ds(r0, CONV_ROWS), pl.ds(0, tc)] = dx.astype(dx_ref.dtype)
            if gated:
                dx_ref[0, pl.ds(r0, CONV_ROWS), pl.ds(tc, tc)] = (d[:CONV_ROWS] * (pre * sig)[:CONV_ROWS]).astype(dx_ref.dtype)
            here = dpre[:CONV_ROWS]
            sums = [fold(here * (pltpu.roll(xe, taps - 1 - k, 0) if k < taps - 1 else xe)[SUBLANES:ext]) + carry[k] for k in range(taps)]
            return tuple(sums) + (fold(here) + carry[taps],)

        zero = jnp.zeros((SUBLANES, tc), F32)
        sums = lax.fori_loop(0, s // CONV_ROWS, step, (zero,) * (taps + 1), unroll=2)
        rows = [jnp.sum(t, axis=0, keepdims=True) for t in sums]
        dw = jnp.concatenate(rows[:taps] + [jnp.zeros_like(rows[0])] * (SUBLANES - taps), axis=0)

        @pl.when(bb == 0)
        def _():
            dw_ref[...] = dw
            db_ref[...] = rows[taps]

        @pl.when(bb > 0)
        def _():
            dw_ref[...] += dw
            db_ref[...] += rows[taps]

    first = 0 if gated else col0 // tc
    out_shape = [_sds((bl, s, chans * (2 if gated else 1)), dx_dtype), _sds((SUBLANES, chans), F32), _sds((1, chans), F32)]
    dy_specs = [
        pl.BlockSpec((1, s, tc), lambda j, bb, k=k: (bb, 0, jnp.clip(j - starts[k], 0, pieces[k].shape[2] // tc - 1)))
        for k in range(len(pieces))
    ]
    return _pcall(
        body, name, out_shape, grid=(chans // tc, bl),
        in_specs=[
            pl.BlockSpec((1, s, xw), lambda j, bb: (bb, 0, first + j)), pl.BlockSpec((SUBLANES, tc), lambda j, bb: (0, j)),
            pl.BlockSpec((1, tc), lambda j, bb: (0, j)), *dy_specs,
        ],
        out_specs=[pl.BlockSpec((1, s, xw), lambda j, bb: (bb, 0, j)), pl.BlockSpec((SUBLANES, tc), lambda j, bb: (0, j)), pl.BlockSpec((1, tc), lambda j, bb: (0, j))],
        sem=("parallel", "arbitrary"),
    )(src, w8, b, *pieces)


def _gate_value_blocks(name, w, f, inverse=False):
    nb = f // CONV_LANES
    src = (lambda j: (0, 2 * (j % nb) + j // nb)) if inverse else (lambda j: (0, (j % 2) * nb + j // 2))

    def body(x_ref, o_ref):
        o_ref[...] = x_ref[...]

    return _pcall(
        body, name, _sds(w.shape, w.dtype), grid=(2 * nb,), in_specs=[pl.BlockSpec((w.shape[0], CONV_LANES), src)],
        out_specs=pl.BlockSpec((w.shape[0], CONV_LANES), lambda j: (0, j)), sem=("parallel",),
    )(w)


def _ssd_step(states, xs, bm, cm, dtr, z, dtb, alog, dsk, ng):
    gw = xs.shape[1] // SSD_GROUPS
    hpg, npair = gw // HEAD_DIM, gw // LANES
    dt = jax.nn.softplus(dtr + dtb)
    da = dt * (-jnp.exp(alog))
    causal = _causal(CHUNK)
    acs = _sel_left(causal.astype(F32), da)
    acs_t = acs.T
    lane = lax.broadcasted_iota(jnp.int32, (1, LANES), 1)
    sub = lax.broadcasted_iota(jnp.int32, (LANES, 1), 0)
    last = lax.broadcasted_iota(jnp.int32, (CHUNK, gw), 0) == CHUNK - 1
    outs, new_states = [], []
    for g in range(SSD_GROUPS):
        xg, zg = xs[:, g * gw:(g + 1) * gw], z[:, g * gw:(g + 1) * gw]
        bg, cg = bm[:, g * STATE:(g + 1) * STATE], cm[:, g * STATE:(g + 1) * STATE]
        head_of_col = g * hpg + lax.broadcasted_iota(jnp.int32, (LANES, gw), 1) // HEAD_DIM
        expand = (lax.broadcasted_iota(jnp.int32, (LANES, gw), 0) == head_of_col).astype(F32)
        dt_e, acs_e = _sel_right(dt, expand), _sel_right(acs, expand)
        alast_e = jnp.sum(jnp.where(last, acs_e, 0.0), axis=0, keepdims=True)
        xc = xg * dt_e
        xc_st = xc * jnp.exp(alast_e - acs_e)
        decay_out, chunk_decay = jnp.exp(acs_e), jnp.exp(alast_e)
        cb = _dot_nt(cg, bg)
        ys = []
        for p in range(npair):
            sl = slice(p * LANES, (p + 1) * LANES)
            state = states[g * npair + p]
            y = _dot_nn(cg, state) * decay_out[:, sl]
            for q in range(2):
                head = g * hpg + 2 * p + q
                col = jnp.sum(acs * (lane == head).astype(F32), axis=1, keepdims=True)
                row = jnp.sum(acs_t * (sub == head).astype(F32), axis=0, keepdims=True)
                decay = jnp.where(causal, jnp.exp(jnp.where(causal, col - row, 0.0)), 0.0)
                y = y + _dot_nn(cb * decay, xc[:, sl] * ((lane // HEAD_DIM) == q).astype(F32))
            ys.append(y)
            new_states.append(state * chunk_decay[:, sl] + _dot_tn(bg, xc_st[:, sl]))
        y = jnp.concatenate(ys, axis=1) + dsk[:, g * gw:(g + 1) * gw] * xg
        outs.append(_rms(y * (zg * _sigmoid(zg)), ng[:, g * gw:(g + 1) * gw]))
    return tuple(new_states), jnp.concatenate(outs, axis=1)


def _ssd_scan_specs(d, dt_col, nc, rev):
    ci = (lambda i: nc - 1 - i) if rev else (lambda i: i)
    bc = SSD_GROUPS * STATE
    return [
        pl.BlockSpec((1, CHUNK, d), lambda b, i: (b, ci(i), 0)),
        pl.BlockSpec((1, CHUNK, bc), lambda b, i: (b, ci(i), d // bc)),
        pl.BlockSpec((1, CHUNK, bc), lambda b, i: (b, ci(i), d // bc + 1)),
        pl.BlockSpec((1, CHUNK, LANES), lambda b, i: (b, ci(i), dt_col // LANES)),
        pl.BlockSpec((1, CHUNK, d), lambda b, i: (b, ci(i), 0)),
        pl.BlockSpec((1, LANES), lambda b, i: (0, 0)), pl.BlockSpec((1, LANES), lambda b, i: (0, 0)),
        pl.BlockSpec((1, d), lambda b, i: (0, 0)), pl.BlockSpec((1, d), lambda b, i: (0, 0)),
    ]


def _ssd_scan_fwd(name, xbc, proj, dt_col, dtb, alog, dsk, ng):
    bl, s, _ = xbc.shape
    d = dsk.shape[1]
    nc, nst = s // CHUNK, d // LANES

    def body(xs_ref, bm_ref, cm_ref, dt_ref, z_ref, dtb_ref, alog_ref, dsk_ref, ng_ref, y_ref, hp_ref, st_ref):
        @pl.when(pl.program_id(1) == 0)
        def _():
            st_ref[...] = jnp.zeros_like(st_ref)

        states = tuple(st_ref[p] for p in range(nst))
        hp_ref[0, 0] = st_ref[...]
        new_states, yn = _ssd_step(states, xs_ref[0], bm_ref[0], cm_ref[0], dt_ref[0], z_ref[0], dtb_ref[...], alog_ref[...], dsk_ref[...], ng_ref[...])
        for p in range(nst):
            st_ref[p] = new_states[p]
        y_ref[0] = yn.astype(y_ref.dtype)

    return _pcall(
        body, name, [_sds((bl, s, 2 * d), BF16), _sds((bl, nc, nst, STATE, LANES), F32)], grid=(bl, nc), in_specs=_ssd_scan_specs(d, dt_col, nc, False),
        out_specs=[pl.BlockSpec((1, CHUNK, d), lambda b, i: (b, i, 0)), pl.BlockSpec((1, 1, nst, STATE, LANES), lambda b, i: (b, i, 0, 0, 0))],
        scratch=[pltpu.VMEM((nst, STATE, LANES), F32)], sem=("arbitrary", "arbitrary"),
    )(xbc, xbc, xbc, proj, proj, dtb, alog, dsk, ng)


def _ssd_scan_bwd(name, xbc, proj, dt_col, dtb, alog, dsk, ng, hprev, dy):
    bl, s, _ = xbc.shape
    d = dsk.shape[1]
    nc, nst, bc = s // CHUNK, d // LANES, SSD_GROUPS * STATE

    def body(xs_ref, bm_ref, cm_ref, dt_ref, z_ref, dtb_ref, alog_ref, dsk_ref, ng_ref, hp_ref, dy_ref,
             dxs_ref, dbm_ref, dcm_ref, ddt_ref, dz_ref, ddtb_ref, dalog_ref, ddsk_ref, dng_ref, dst_ref):
        b, i = pl.program_id(0), pl.program_id(1)

        @pl.when(i == 0)
        def _():
            dst_ref[...] = jnp.zeros_like(dst_ref)

        states = tuple(hp_ref[0, 0, p] for p in range(nst))
        _, vjp = jax.vjp(_ssd_step, states, xs_ref[0], bm_ref[0], cm_ref[0], dt_ref[0], z_ref[0], dtb_ref[...], alog_ref[...], dsk_ref[...], ng_ref[...])
        d_states, dxs, dbm, dcm, ddt, dz, ddtb, dalog, ddsk, dng = vjp((tuple(dst_ref[p] for p in range(nst)), dy_ref[0].astype(F32)))
        for p in range(nst):
            dst_ref[p] = d_states[p]
        dxs_ref[0], dbm_ref[0], dcm_ref[0], ddt_ref[0] = dxs, dbm, dcm, ddt
        dz_ref[0] = dz.astype(dz_ref.dtype)
        first = (b == 0) & (i == 0)
        for o_ref, val in ((ddtb_ref, ddtb), (dalog_ref, dalog), (ddsk_ref, ddsk), (dng_ref, dng)):
            @pl.when(first)
            def _(o_ref=o_ref, val=val):
                o_ref[...] = val

            @pl.when(jnp.logical_not(first))
            def _(o_ref=o_ref, val=val):
                o_ref[...] += val

    rc = lambda i: nc - 1 - i
    in_specs = _ssd_scan_specs(d, dt_col, nc, True) + [
        pl.BlockSpec((1, 1, nst, STATE, LANES), lambda b, i: (b, rc(i), 0, 0, 0)), pl.BlockSpec((1, CHUNK, d), lambda b, i: (b, rc(i), 0)),
    ]
    out_shape = [
        _sds((bl, s, d), F32), _sds((bl, s, bc), F32), _sds((bl, s, bc), F32), _sds((bl, s, LANES), F32), _sds((bl, s, d), BF16),
        _sds((1, LANES), F32), _sds((1, LANES), F32), _sds((1, d), F32), _sds((1, d), F32),
    ]
    row = lambda w: pl.BlockSpec((1, CHUNK, w), lambda b, i: (b, rc(i), 0))
    whole = lambda w: pl.BlockSpec((1, w), lambda b, i: (0, 0))
    out_specs = [row(d), row(bc), row(bc), row(LANES), row(d), whole(LANES), whole(LANES), whole(d), whole(d)]
    return _pcall(
        body, name, out_shape, grid=(bl, nc), in_specs=in_specs, out_specs=out_specs,
        scratch=[pltpu.VMEM((nst, STATE, LANES), F32)], sem=("arbitrary", "arbitrary"),
    )(xbc, xbc, xbc, proj, proj, dtb, alog, dsk, ng, hprev, dy)


def _mixer_fwd(name, proj, d, cd, conv_w8, conv_b, dtb, alog, dsk, ng, vg, ws, bs_t, og):
    xbc = _conv_fwd(name + "_conv_fwd", proj, 3 * d, cd, conv_w8, conv_b, SSD_CONV, False, F32)
    y, hprev = _ssd_scan_fwd(name + "_ssd_fwd", xbc, proj, 3 * d + cd, dtb, alog, dsk, ng)
    rows = [(proj, d, 1, BF16), (proj, d, 2, BF16)]
    (ycat,) = _row_fwd(name + "_gm_fwd", _gm_tile, rows, [], [vg, ws, bs_t, og], [(d, BF16)], 0, CHUNK, into=(y, 1))
    return ycat, (xbc, hprev)


def _mixer_bwd(name, proj, d, cd, conv_w8, conv_b, dtb, alog, dsk, ng, vg, ws, bs_t, og, xbc, hprev, dycat):
    bl, s, n_proj = proj.shape
    dt_col = 3 * d + cd
    dy, dg_out = dycat[..., :d], dycat[..., d:]
    dxs, dbm, dcm, ddt, dz, ddtb, dalog, ddsk, dng = _ssd_scan_bwd(name + "_ssd_bwd", xbc, proj, dt_col, dtb, alog, dsk, ng, hprev, dy)
    dxbc, dw8, dcb = _conv_bwd(name + "_conv_bwd", proj, 3 * d, cd, conv_w8, conv_b, SSD_CONV, False, [dxs, dbm, dcm], BF16)
    rows = [(proj, d, 1, BF16), (proj, d, 2, BF16)]
    (du, dv), _, (dvg, dws, dbs_t, dog) = _row_bwd(name + "_gm_bwd", _gm_tile, rows, [], [vg, ws, bs_t, og], [(d, BF16)], 0, CHUNK, [dg_out])
    pad = jnp.zeros((bl, s, n_proj - dt_col - LANES), BF16)
    dproj = jnp.concatenate([dz, du, dv, dxbc, ddt.astype(BF16), pad], axis=-1)
    return dproj, (dw8, dcb, ddtb, dalog, ddsk, dng, dvg, dws, dbs_t, dog)


def _position():
    return lax.axis_index("x"), lax.axis_index("y"), lax.axis_index("c")


def _at(ref, idx):
    return ref.at[idx] if len(idx) else ref


def _exchange(name, inputs, out_shapes, plan, inplace=False, after=None):
    if inplace:
        out_shapes = [(a.shape, a.dtype) for a in inputs]
    n_in, n_out = len(inputs), len(out_shapes)
    n_after = 0 if after is None else 1
    n_copy = len(plan(0, 0, 0))

    def body(*refs):
        in_refs, out_refs, token = refs[:n_in], refs[n_in + n_after:n_in + n_after + n_out], refs[n_in + n_after + n_out]
        send_sems, recv_sems = refs[n_in + n_after + n_out + 1:]
        token[...] = jnp.zeros_like(token)
        x, y, c = _position()
        copies = plan(x, y, c)

        def copy(k, src, dst, peer):
            return pltpu.make_async_remote_copy(src_ref=src, dst_ref=dst, send_sem=send_sems.at[k], recv_sem=recv_sems.at[k], device_id=peer, device_id_type=MESH)

        src_refs = out_refs if inplace else in_refs
        sends = [copy(k, _at(src_refs[sa], si), _at(out_refs[da], di), peer) for k, (sa, si, da, di, peer, _) in enumerate(copies)]
        for cp in sends:
            cp.start()
        for k, (sa, si, da, _, peer, li) in enumerate(copies):
            copy(k, _at(src_refs[sa], si), _at(out_refs[da], li), peer).wait_recv()
        for cp in sends:
            cp.wait_send()

    any_spec = pl.BlockSpec(memory_space=pl.ANY)
    outs = pl.pallas_call(
        body, name=name, out_shape=[_sds(s, dt) for s, dt in out_shapes] + [_sds((SUBLANES, LANES), F32)], in_specs=[any_spec] * (n_in + n_after),
        out_specs=[any_spec] * n_out + [pl.BlockSpec(memory_space=pltpu.VMEM)],
        scratch_shapes=[pltpu.SemaphoreType.DMA((n_copy,)), pltpu.SemaphoreType.DMA((n_copy,))],
        input_output_aliases={i: i for i in range(n_in)} if inplace else {},
    )(*inputs, *([] if after is None else [after]))
    return list(outs[:n_out]), outs[n_out]


def _exchange_start(name, inputs, out_shapes, plan):
    n_in, n_out = len(inputs), len(out_shapes)
    n_copy = len(plan(0, 0, 0))

    def body(*refs):
        in_refs, land_refs = refs[:n_in], refs[n_in:n_in + n_out]
        send_sems, recv_sems = refs[n_in + n_out:n_in + n_out + 2]
        token = refs[-1]
        x, y, c = _position()
        for k, (sa, si, da, di, peer, _) in enumerate(plan(x, y, c)):
            pltpu.make_async_remote_copy(
                src_ref=_at(in_refs[sa], si), dst_ref=_at(land_refs[da], di), send_sem=send_sems.at[k], recv_sem=recv_sems.at[k],
                device_id=peer, device_id_type=MESH).start()
        token[...] = jnp.zeros_like(token)

    hbm, sem = pl.BlockSpec(memory_space=pltpu.HBM), pl.BlockSpec(memory_space=pltpu.SEMAPHORE)
    lands = [lax.empty(s, dt) for s, dt in out_shapes]
    args = [pltpu.with_memory_space_constraint(a, pltpu.HBM) for a in list(inputs) + lands]
    outs = pl.pallas_call(
        body, name=name,
        out_shape=(pltpu.SemaphoreType.DMA((n_copy,)), pltpu.SemaphoreType.DMA((n_copy,)), *[pltpu.HBM(a.shape, a.dtype) for a in args], _sds((SUBLANES, LANES), F32)),
        in_specs=[hbm] * (n_in + n_out), out_specs=(sem, sem, *[hbm] * (n_in + n_out), pl.BlockSpec(memory_space=pltpu.VMEM)),
        input_output_aliases={i: 2 + i for i in range(n_in + n_out)},
        compiler_params=pltpu.CompilerParams(has_side_effects=pltpu.SideEffectType.DATAFLOW_SIDE_EFFECTING),
    )(*args)
    return dict(name=name, plan=plan, sems=outs[:2], ins=list(outs[2:2 + n_in]), lands=list(outs[2 + n_in:2 + n_in + n_out]), token=outs[-1])


def _exchange_wait(started, after):
    plan, n_in, n_out = started["plan"], len(started["ins"]), len(started["lands"])

    def body(*refs):
        in_refs, land_refs = refs[:n_in], refs[n_in:n_in + n_out]
        send_sems, recv_sems = refs[n_in + n_out:n_in + n_out + 2]
        token = refs[-1]
        x, y, c = _position()
        for k, (sa, si, da, _, peer, li) in enumerate(plan(x, y, c)):
            cp = pltpu.make_async_remote_copy(
                src_ref=_at(in_refs[sa], si), dst_ref=_at(land_refs[da], li), send_sem=send_sems.at[k], recv_sem=recv_sems.at[k],
                device_id=peer, device_id_type=MESH)
            cp.wait_send()
            cp.wait_recv()
        token[...] = jnp.zeros_like(token)

    hbm, sem = pl.BlockSpec(memory_space=pltpu.HBM), pl.BlockSpec(memory_space=pltpu.SEMAPHORE)
    bufs = started["ins"] + started["lands"]
    outs = pl.pallas_call(
        body, name=started["name"] + "_wait", out_shape=(*[pltpu.HBM(a.shape, a.dtype) for a in bufs], _sds((SUBLANES, LANES), F32)),
        in_specs=[hbm] * len(bufs) + [sem, sem, pl.BlockSpec(memory_space=pl.ANY)], out_specs=(*[hbm] * len(bufs), pl.BlockSpec(memory_space=pltpu.VMEM)),
        input_output_aliases={i: i for i in range(len(bufs))},
        compiler_params=pltpu.CompilerParams(has_side_effects=pltpu.SideEffectType.DATAFLOW_SIDE_EFFECTING),
    )(*bufs, *started["sems"], after)
    return list(outs[:n_in]), list(outs[n_in:n_in + n_out]), outs[-1]


def _after(value, token):
    return value + token[0, 0].astype(value.dtype)


def _chip_peers(x, y, c):
    return [(1 - x, y, c), (x, 1 - y, c), (1 - x, 1 - y, c)]


def _chip_of(p):
    return 2 * p[0] + p[1]


def _set_slot(slots, me, blk):
    return lax.dynamic_update_slice(slots, blk[None], (me,) + (0,) * blk.ndim)


def _by_core(c, mine, other, axis):
    return jnp.where(c == 0, jnp.stack([mine, other], axis), jnp.stack([other, mine], axis))


def _plan_gather_chips(n):
    def plan(x, y, c):
        return [(a, (), a, (2 * x + y,), p, (_chip_of(p),)) for a in range(n) for p in _chip_peers(x, y, c)]

    return plan


def _gather_chips(name, blocks):
    recv, token = _exchange(name, blocks, [((4,) + b.shape, b.dtype) for b in blocks], _plan_gather_chips(len(blocks)))
    x, y, _ = _position()
    return [_set_slot(r, 2 * x + y, b) for r, b in zip(recv, blocks)], token


def _gather_pass_cores(name, blocks, from_chips):
    n = len(blocks)

    def plan_cores(x, y, c):
        me, sib = 2 * x + y, (x, y, 1 - c)
        own = [(a, (), a, (me,), sib, (me,)) for a in range(n)]
        passed = [(n + a, (_chip_of(p),), a, (_chip_of(p),), sib, (_chip_of(p),)) for a in range(n) for p in _chip_peers(x, y, c)]
        return own + passed

    from_core, token = _exchange(name + "_cores", list(blocks) + list(from_chips), [((4,) + b.shape, b.dtype) for b in blocks], plan_cores)
    x, y, c = _position()
    return [_by_core(c, _set_slot(r1, 2 * x + y, b), r2, 1) for b, r1, r2 in zip(blocks, from_chips, from_core)], token


def _gather_two_level(name, blocks):
    from_chips, _ = _exchange(name + "_chips", blocks, [((4,) + b.shape, b.dtype) for b in blocks], _plan_gather_chips(len(blocks)))
    return _gather_pass_cores(name, blocks, from_chips)


def _pair_add(name, g42, r4):
    _, _, rh, cols = g42.shape
    tr = _divisors(rh, 512, SUBLANES * 2)[-1]

    def body(c_ref, a_ref, b_ref, o_ref):
        o_ref[0] = (a_ref[0, 0].astype(F32) + b_ref[0].astype(F32)).astype(o_ref.dtype)

    cidx = lax.axis_index("c").astype(jnp.int32).reshape(1)
    return _pcall(
        body, name, _sds(r4.shape, BF16), grid=(4, rh // tr),
        in_specs=[pl.BlockSpec((1, 1, tr, cols), lambda s, i, c_ref: (s, c_ref[0], i, 0)), pl.BlockSpec((1, tr, cols), lambda s, i, c_ref: (s, i, 0))],
        out_specs=pl.BlockSpec((1, tr, cols), lambda s, i, c_ref: (s, i, 0)), sem=("parallel", "parallel"), prefetch=1,
    )(cidx, g42, r4)


def _slot_sum(name, parts):
    n, r, cols = parts.shape
    cap = max(2 * SUBLANES, (4 * 1024 * 1024) // (n * cols * parts.dtype.itemsize))
    tr = _divisors(r, cap, 2 * SUBLANES)[-1]

    def body(p_ref, o_ref):
        acc = p_ref[0].astype(F32)
        for k in range(1, n):
            acc = acc + p_ref[k].astype(F32)
        o_ref[...] = acc

    return _pcall(
        body, name, _sds((r, cols), F32), grid=(r // tr,), in_specs=[pl.BlockSpec((n, tr, cols), lambda i: (0, i, 0))],
        out_specs=pl.BlockSpec((tr, cols), lambda i: (i, 0)), sem=("parallel",),
    )(parts)


def _slot_sums(name, parts):
    k = len(parts)

    def body(*refs):
        for p_ref, o_ref in zip(refs[:k], refs[k:]):
            acc = p_ref[0].astype(F32)
            for j in range(1, p_ref.shape[0]):
                acc = acc + p_ref[j].astype(F32)
            o_ref[...] = acc

    return list(_pcall(body, name, [_sds(p.shape[1:], F32) for p in parts])(*parts))


def _gather_weights_start(name, shards, token):
    c = lax.axis_index("c")
    halves = [lax.dynamic_slice_in_dim(w, c * (w.shape[0] // 2), w.shape[0] // 2, 0).astype(BF16) for w in shards]
    if token is not None:
        halves[0] = _after(halves[0], token)
    n = len(halves)

    def plan(x, y, c):
        return [(a, (), a, (2 * x + y, c), p, (_chip_of(p), c)) for a in range(n) for p in _chip_peers(x, y, c)]

    return _exchange_start(name + "_ag_chips", halves, [((4, 2) + h.shape, h.dtype) for h in halves], plan)


def _gather_weights_finish(name, starteds, after):
    halves, slots = [], []
    for started in starteds:
        h, s, token = _exchange_wait(started, after)
        halves, slots = halves + h, slots + s
    n = len(halves)
    x, y, c = _position()
    slots = [lax.dynamic_update_slice(s, h[None, None], (2 * x + y, c, 0, 0)) for s, h in zip(slots, halves)]

    def plan_cores(x, y, c):
        return [(a, (s, c), a, (s, c), (x, y, 1 - c), (s, 1 - c)) for a in range(n) for s in range(4)]

    full, _ = _exchange(name + "_ag_cores", slots, None, plan_cores, inplace=True)
    return [f.reshape((4, 2 * h.shape[0], h.shape[1])) for f, h in zip(full, halves)], token


def _reduce_weights_start(name, grads, token):
    n = len(grads)
    g42 = [g.reshape(4, 2, g.shape[1] // 2, g.shape[2]) for g in grads]

    def plan_swap(x, y, c):
        return [(a, (s, 1 - c), a, (s,), (x, y, 1 - c), (s,)) for a in range(n) for s in range(4)]

    def plan_chips(x, y, c):
        return [(a, (_chip_of(p),), a, (2 * x + y,), p, (_chip_of(p),)) for a in range(n) for p in _chip_peers(x, y, c)]

    other, _ = _exchange(name + "_rs_cores", g42, [((4,) + g.shape[2:], g.dtype) for g in g42], plan_swap, after=token)
    pair = [_pair_add(f"{name}_rs_pair{a}", g, o) for a, (g, o) in enumerate(zip(g42, other))]
    return _exchange_start(name + "_rs_chips", pair, [(p.shape, p.dtype) for p in pair], plan_chips)


def _reduce_weights_finish(name, starteds, after):
    pair, recv = [], []
    for started in starteds:
        p, r, token = _exchange_wait(started, after)
        pair, recv = pair + p, recv + r
    n = len(pair)
    x, y, c = _position()
    me = 2 * x + y

    def plan_share(x, y, c):
        return [(a, (c,), a, (c,), (x, y, 1 - c), (1 - c,)) for a in range(n)]

    parts = [lax.dynamic_update_slice(r, lax.dynamic_slice_in_dim(p, me, 1, 0), (me, 0, 0)) for r, p in zip(recv, pair)]
    mine = [_slot_sum(f"{name}_rs_sum{a}", p) for a, p in enumerate(parts)]
    both = [lax.dynamic_update_slice(lax.empty((2,) + m.shape, m.dtype), m[None], (c, 0, 0)) for m in mine]
    both, _ = _exchange(name + "_rs_share", both, None, plan_share, inplace=True)
    return [b.reshape(2 * b.shape[1], b.shape[2]) for b in both], token


def _allreduce_small(name, grads):
    both, token = _gather_two_level(name + "_ag", [g.astype(BF16) if g.size >= SMALL_BF16_SIZE else g for g in grads])
    return _slot_sums(name + "_sum", [g.reshape((8,) + g.shape[2:]) for g in both]), token


def _ada_fwd_call(name, c_all, w, b_shard):
    nl, d, ns = w.shape
    nb = c_all.shape[0]

    def body(c_ref, w_ref, b_ref, o_ref):
        cv = c_ref[...]
        o_ref[0] = _dg(cv * _sigmoid(cv), w_ref[0], ((1,), (0,))) + b_ref[0]

    return _pcall(
        body, name, _sds((nl, nb, ns), F32), grid=(nl,),
        in_specs=[pl.BlockSpec((nb, d), lambda l: (0, 0)), pl.BlockSpec((1, d, ns), lambda l: (l, 0, 0)), pl.BlockSpec((1, 1, ns), lambda l: (l, 0, 0))],
        out_specs=pl.BlockSpec((1, nb, ns), lambda l: (l, 0, 0)), sem=("parallel",),
    )(c_all, w, b_shard)


def _ada_bwd_call(name, c_all, dm_shard, dm_all):
    nl, nb, ns = dm_shard.shape
    d = c_all.shape[1]
    nm = dm_all.shape[2]

    def body(c_ref, ds_ref, da_ref, dw_ref, db_ref):
        cv = c_ref[...]
        dw_ref[0] = _dg(cv * _sigmoid(cv), ds_ref[0], ((0,), (0,)))
        db_ref[0] = jnp.sum(da_ref[0], axis=0, keepdims=True)

    return _pcall(
        body, name, [_sds((nl, d, ns), F32), _sds((nl, 1, nm), F32)], grid=(nl,),
        in_specs=[pl.BlockSpec((nb, d), lambda l: (0, 0)), pl.BlockSpec((1, nb, ns), lambda l: (l, 0, 0)), pl.BlockSpec((1, nb, nm), lambda l: (l, 0, 0))],
        out_specs=[pl.BlockSpec((1, d, ns), lambda l: (l, 0, 0)), pl.BlockSpec((1, 1, nm), lambda l: (l, 0, 0))], sem=("parallel",),
    )(c_all, dm_shard, dm_all)


def _ada_fwd(name, bl, c_all, w, b):
    nl, d, ns = w.shape
    chip = 2 * lax.axis_index("x") + lax.axis_index("y")
    b_shard = lax.dynamic_slice(b, (0, chip * ns), (nl, ns)).reshape(nl, 1, ns)
    shard = _ada_fwd_call(name + "_fwd", c_all, w, b_shard)
    (allc,), token = _gather_chips(name + "_ag", [shard])
    mods = jnp.transpose(allc, (1, 2, 0, 3)).reshape(nl, c_all.shape[0], 4 * ns)
    return lax.dynamic_slice(mods, (0, (2 * chip + lax.axis_index("c")) * bl, 0), (nl, bl, 4 * ns)), token


def _ada_bwd(name, bl, c_all, ns, dm):
    nl = dm.shape[0]
    chip = 2 * lax.axis_index("x") + lax.axis_index("y")
    (dm_all,), token = _gather_two_level(name + "_bwd_ag", [dm])
    dm_all = jnp.transpose(dm_all.reshape((8,) + dm.shape), (1, 0, 2, 3)).reshape(nl, 8 * bl, 4 * ns)
    dm_shard = lax.dynamic_slice(dm_all, (0, 0, chip * ns), (nl, 8 * bl, ns))
    dw, db = _ada_bwd_call(name + "_bwd", c_all, dm_shard, dm_all)
    return dw, db.reshape(nl, 4 * ns), token


def _adamw(name, w, g, m, v):
    shape = w.shape
    cols = shape[-1]
    w2, g2, m2, v2 = (t.reshape(-1, cols) for t in (w, g, m, v))
    rows = w2.shape[0]
    cap = max(SUBLANES, (512 * 1024) // max(cols, 1) // SUBLANES * SUBLANES)
    tr = _divisors(rows, cap, SUBLANES)[-1]

    def body(w_ref, g_ref, m_ref, v_ref, d_ref, mo_ref, vo_ref):
        gv = g_ref[...]
        mn = ADAM_B1 * m_ref[...] + (1.0 - ADAM_B1) * gv
        vn = ADAM_B2 * v_ref[...] + (1.0 - ADAM_B2) * (gv * gv)
        m_hat = mn / (1.0 - ADAM_B1 ** ADAM_STEP)
        v_hat = vn / (1.0 - ADAM_B2 ** ADAM_STEP)
        d_ref[...] = -ADAM_LR * (m_hat / (jnp.sqrt(v_hat) + ADAM_EPS) + ADAM_WD * w_ref[...])
        mo_ref[...] = mn
        vo_ref[...] = vn

    spec = pl.BlockSpec((tr, cols), lambda i: (i, 0))
    outs = _pcall(body, name, [_sds((rows, cols), F32)] * 3, grid=(rows // tr,), in_specs=[spec] * 4, out_specs=[spec] * 3, sem=("parallel",))(w2, g2, m2, v2)
    return tuple(o.reshape(shape) for o in outs)


def _adamw_small(name, ws, gs, ms, vs):
    n = len(ws)

    def body(*refs):
        for k in range(n):
            w_ref, g_ref, m_ref, v_ref = (refs[j * n + k] for j in range(4))
            d_ref, mo_ref, vo_ref = (refs[(4 + j) * n + k] for j in range(3))
            gv = g_ref[...]
            mn = ADAM_B1 * m_ref[...] + (1.0 - ADAM_B1) * gv
            vn = ADAM_B2 * v_ref[...] + (1.0 - ADAM_B2) * (gv * gv)
            m_hat = mn / (1.0 - ADAM_B1 ** ADAM_STEP)
            v_hat = vn / (1.0 - ADAM_B2 ** ADAM_STEP)
            d_ref[...] = -ADAM_LR * (m_hat / (jnp.sqrt(v_hat) + ADAM_EPS) + ADAM_WD * w_ref[...])
            mo_ref[...] = mn
            vo_ref[...] = vn

    outs = _pcall(body, name, [_sds(w.shape, F32) for w in ws] * 3)(*ws, *gs, *ms, *vs)
    return outs[:n], outs[n:2 * n], outs[2 * n:]


def _pad_rows(w, rows):
    return jnp.pad(w, ((0, rows - w.shape[0]), (0, 0)))


def _pad_lanes(v):
    return jnp.pad(v, (0, LANES - v.shape[0])).reshape(1, LANES)


BIG = ("w_in", "w_out", "ff_up", "ff_down")
BIG_AXIS = {"w_in": 1, "w_out": 0, "ff_up": 1, "ff_down": 0}
CONVW = ("ssd_conv_w", "ff_conv_w")
SMALL = ("norm1_g", "norm2_g", "ssd_conv_b", "ssd_dt_bias", "ssd_a_log", "ssd_d", "ssd_norm_g", "gm_vnorm_g", "gm_ws", "gm_bs", "gm_out_g", "ff_conv_b")
WEIGHTS = ("ada_w", "ada_b", "norm1_g", "norm2_g", "w_in", "ssd_conv_w", "ssd_conv_b", "ssd_dt_bias", "ssd_a_log", "ssd_d", "ssd_norm_g", "gm_vnorm_g", "gm_ws", "gm_bs", "gm_out_g", "w_out", "ff_up", "ff_conv_w", "ff_conv_b", "ff_down", "final_g")


def kernel(x, c, ada_w, ada_b, norm1_g, norm2_g, w_in, ssd_conv_w, ssd_conv_b, ssd_dt_bias, ssd_a_log, ssd_d, ssd_norm_g, gm_vnorm_g, gm_ws, gm_bs, gm_out_g, w_out, ff_up, ff_conv_w, ff_conv_b, ff_down, final_g, loss_target, m_ada_w, m_ada_b, m_norm1_g, m_norm2_g, m_w_in, m_ssd_conv_w, m_ssd_conv_b, m_ssd_dt_bias, m_ssd_a_log, m_ssd_d, m_ssd_norm_g, m_gm_vnorm_g, m_gm_ws, m_gm_bs, m_gm_out_g, m_w_out, m_ff_up, m_ff_conv_w, m_ff_conv_b, m_ff_down, m_final_g, v_ada_w, v_ada_b, v_norm1_g, v_norm2_g, v_w_in, v_ssd_conv_w, v_ssd_conv_b, v_ssd_dt_bias, v_ssd_a_log, v_ssd_d, v_ssd_norm_g, v_gm_vnorm_g, v_gm_ws, v_gm_bs, v_gm_out_g, v_w_out, v_ff_up, v_ff_conv_w, v_ff_conv_b, v_ff_down, v_final_g):
    given = dict(locals())
    weights = {n: given[n] for n in WEIGHTS}
    bl, s, d = x.shape
    nl = ada_w.shape[0]
    heads = d // HEAD_DIM
    cd = d + 2 * SSD_GROUPS * STATE
    f = ff_down.shape[1] * 4
    n_in = d + cd + heads + 2 * d
    n_proj = _round_up(3 * d + cd + LANES, 2 * LANES)
    tm = _divisors(s, 512)[-1]

    gathering = {}

    def start_gathers(l, names, token):
        for n in names:
            gathering[l, n] = _gather_weights_start(f"wg{l}_{n}", [weights[n][l]], token)
            token = gathering[l, n]["token"]
        return token

    pre, _ = _gather_two_level("pre_ag", [c] + [weights[n] for n in CONVW])
    c_all = pre[0].reshape(8 * bl, d)
    conv_full = {n: jnp.concatenate([p[k, 0] for k in range(4)], axis=-1) for n, p in zip(CONVW, pre[1:])}
    mods_all, ada_token = _ada_fwd("ada", bl, c_all, ada_w, ada_b)
    mods = mods_all.reshape(nl, bl, N_MOD, 1, d)

    ready = {}

    def assembled(n, full):
        if n == "w_in":
            return full
        return full.reshape(-1, full.shape[2]) if BIG_AXIS[n] == 0 else jnp.concatenate([full[k] for k in range(4)], axis=1)

    def landed(l, n, after):
        if l > 0 and (l, n) not in ready:
            fulls, token = _gather_weights_finish(f"wg{l}", [gathering.pop((l, m)) for m in BIG], after)
            ready.update({(l, m): (assembled(m, t), token) for m, t in zip(BIG, fulls)})
        if (l, n) in ready:
            return ready.pop((l, n))
        (full,), token = _gather_weights_finish(f"wg{l}_{n}", [gathering.pop((l, n))], after)
        return assembled(n, full), token

    issued = start_gathers(0, BIG, ada_token)
    r2 = lambda v: v.reshape(1, -1)
    n_gm = d // GM_HEAD
    fg = r2(final_g)

    def layer_params(l):
        return dict(
            norm1=r2(norm1_g[l]), norm2=r2(norm2_g[l]), conv_w8=_pad_rows(conv_full["ssd_conv_w"][l], SUBLANES), conv_b=r2(ssd_conv_b[l]),
            dtb=_pad_lanes(ssd_dt_bias[l]), alog=_pad_lanes(ssd_a_log[l]), dsk=r2(jnp.repeat(ssd_d[l], HEAD_DIM)), ng=r2(ssd_norm_g[l]),
            vg=r2(gm_vnorm_g[l]), ws=gm_ws[l], bs_t=jnp.pad(gm_bs[l].T, ((0, 0), (0, LANES - n_gm))), og=r2(gm_out_g[l]),
            ff_w8=_pad_rows(conv_full["ff_conv_w"][l], SUBLANES), ff_b=r2(ff_conv_b[l]))

    def mixer_args(p):
        return (p["conv_w8"], p["conv_b"], p["dtb"], p["alog"], p["dsk"], p["ng"], p["vg"], p["ws"], p["bs_t"], p["og"])

    def padded_w_in(full):
        ns = full.shape[2]

        def cols(a, b):
            return [full[k][:, max(a, k * ns) - k * ns:min(b, (k + 1) * ns) - k * ns] for k in range(4) if max(a, k * ns) < min(b, (k + 1) * ns)]

        parts = cols(0, d) + cols(d + cd + heads, n_in) + cols(d, d + cd) + cols(d + cd, d + cd + heads)
        return jnp.concatenate(parts + [jnp.zeros((d, n_proj - (3 * d + cd + heads)), BF16)], axis=1)

    saved, xcur, pending = [], x, None
    for l in range(nl):
        p, w = layer_params(l), {}
        wi, token = landed(l, "w_in", mods_all if l == 0 else pending[0])
        w["w_in"] = padded_w_in(wi)
        sh1, sc1, g1, sh2, sc2, g2 = (mods[l, :, k] for k in range(N_MOD))
        if l + 1 < nl:
            issued = start_gathers(l + 1, BIG, token + issued)
        sc1 = _after(sc1, issued)
        if pending is None:
            (h,) = _row_fwd(f"nm{l}_fwd", _nm_tile, [(xcur, d, 0, None)], [sc1, sh1], [p["norm1"]], [(d, BF16)], 0, tm)
            x1 = xcur
        else:
            rows = [(xcur, d, 0, None), (pending[0], d, 0, None)]
            x1, h = _row_fwd(f"rnm{l}a_fwd", _rnm_tile, rows, [pending[1], sc1, sh1], [p["norm1"]], [(d, F32), (d, BF16)], 0, tm)
        proj = _mm(f"win{l}_fwd", h, w["w_in"], F32)
        ycat, (xbc, hprev) = _mixer_fwd(f"mix{l}", proj, d, cd, *mixer_args(p))
        w["w_out"], _ = landed(l, "w_out", ycat)
        mix = _mm(f"wout{l}_fwd", ycat, w["w_out"], F32)
        x2, h2 = _row_fwd(f"rnm{l}b_fwd", _rnm_tile, [(x1, d, 0, None), (mix, d, 0, None)], [g1, sc2, sh2], [p["norm2"]], [(d, F32), (d, BF16)], 0, tm)
        w_up, _ = landed(l, "ff_up", h2)
        w["ff_up"] = _gate_value_blocks(f"ffup{l}_blocks", w_up, f)
        up = _mm(f"ffup{l}_fwd", h2, w["ff_up"], F32)
        act = _conv_fwd(f"ffact{l}_fwd", up, 0, f, p["ff_w8"], p["ff_b"], FF_CONV, True, BF16)
        w["ff_down"], _ = landed(l, "ff_down", act)
        down = _mm(f"ffdown{l}_fwd", act, w["ff_down"], F32)
        saved.append(dict(p=p, w=w, x_in=xcur, pending=pending, h=h, proj=proj, xbc=xbc, hprev=hprev, ycat=ycat, x1=x1, mix=mix, h2=h2, up=up, act=act))
        xcur, pending = x2, (down, g2)
    rows = [(xcur, d, 0, F32), (pending[0], d, 0, BF16), (loss_target, d, 0, None)]

    (dx2, ddown), (dg2,), (dfg,), (loss_local,) = _row_bwd(
        "final_bwd", _final_tile, rows, [pending[1]], [fg], [], 1, tm, [jnp.ones((1, 1), F32)], emit_sums=True, sub_rows=ROW_SUB)
    loss = lax.psum(loss_local[0, 0], AXES)
    dmods, small_grads = [None] * nl, [None] * nl
    reducing = {}

    def start_reduce(l, named, token=None):
        by_chip = []
        for n, g in named:
            width = g.shape[1] // 4
            by_chip.append(g.reshape(4, g.shape[0] // 4, g.shape[1]) if BIG_AXIS[n] == 0 else jnp.stack([g[:, k * width:(k + 1) * width] for k in range(4)]))
        key = (l, tuple(n for n, _ in named))
        reducing[key] = _reduce_weights_start(f"wg{l}_" + "_".join(key[1]), by_chip, token)
        return reducing[key]["token"]

    for l in reversed(range(nl)):
        sv = saved[l]
        p, w = sv["p"], sv["w"]
        sh1, sc1, g1, sh2, sc2, g2 = (mods[l, :, k] for k in range(N_MOD))
        dact, dw_down = _mm_bwd(f"ffdown{l}", sv["act"], w["ff_down"], ddown)
        dup, dff_w8, dff_b = _conv_bwd(f"ffact{l}_bwd", sv["up"], 0, f, p["ff_w8"], p["ff_b"], FF_CONV, True, dact, BF16)
        dh2, dw_up = _mm_bwd(f"ffup{l}", sv["h2"], w["ff_up"], dup)
        dw_up = _gate_value_blocks(f"ffup{l}_columns", dw_up, f, inverse=True)
        sc2 = _after(sc2, start_reduce(l, [("ff_down", dw_down), ("ff_up", dw_up)]))
        rows = [(sv["x1"], d, 0, F32), (sv["mix"], d, 0, BF16)]
        (dx1, dmix), (dg1, dsc2, dsh2), (dn2,) = _row_bwd(f"rnm{l}b_bwd", _rnm_tile, rows, [g1, sc2, sh2], [p["norm2"]], [(d, F32), (d, BF16)], 0, tm, [dx2, dh2], sub_rows=ROW_SUB)
        dycat, dw_out = _mm_bwd(f"wout{l}", sv["ycat"], w["w_out"], dmix)
        p_tied = p
        if l == 0:
            p_tied = dict(p, dtb=_after(p["dtb"], start_reduce(l, [("w_out", dw_out)])))
        dproj, (dw8, dcb, ddtb, dalog, ddsk, dng, dvg, dws, dbs_t, dog) = _mixer_bwd(f"mix{l}", sv["proj"], d, cd, *mixer_args(p_tied), sv["xbc"], sv["hprev"], dycat)
        dh, dw_in_p = _mm_bwd(f"win{l}", sv["h"], w["w_in"], dproj)
        dw_in = jnp.concatenate([dw_in_p[:, :d], dw_in_p[:, 3 * d:3 * d + cd], dw_in_p[:, 3 * d + cd:3 * d + cd + heads], dw_in_p[:, d:3 * d]], axis=1)
        if l > 0:
            sc1 = _after(sc1, start_reduce(l, [("w_out", dw_out), ("w_in", dw_in)]))
        if sv["pending"] is None:
            (dx2,), (dsc1, dsh1), (dn1,) = _row_bwd(f"nm{l}_bwd", _nm_res_tile, [(sv["x_in"], d, 0, F32)], [sc1, sh1], [p["norm1"]], [(d, F32), (d, BF16)], 0, tm, [dx1, dh], sub_rows=ROW_SUB)
        else:
            rows = [(sv["x_in"], d, 0, F32), (sv["pending"][0], d, 0, BF16)]
            (dx2, ddown), (dg2_prev, dsc1, dsh1), (dn1,) = _row_bwd(
                f"rnm{l}a_bwd", _rnm_tile, rows, [sv["pending"][1], sc1, sh1], [p["norm1"]], [(d, F32), (d, BF16)], 0, tm, [dx1, dh], sub_rows=ROW_SUB)
        dmods[l] = jnp.concatenate([dsh1, dsc1, dg1, dsh2, dsc2, dg2], axis=1).reshape(bl, N_MOD * d)
        if sv["pending"] is not None:
            dg2 = dg2_prev
        small_grads[l] = dict(
            norm1_g=dn1.reshape(d), norm2_g=dn2.reshape(d), ssd_conv_b=dcb.reshape(cd), ssd_dt_bias=ddtb[0, :heads], ssd_a_log=dalog[0, :heads],
            ssd_d=ddsk.reshape(heads, HEAD_DIM).sum(-1), ssd_norm_g=dng.reshape(d), gm_vnorm_g=dvg.reshape(d), gm_ws=dws, gm_bs=dbs_t[:, :n_gm].T,
            gm_out_g=dog.reshape(d), ff_conv_b=dff_b.reshape(f), ssd_conv_w=dw8[:SSD_CONV], ff_conv_w=dff_w8[:FF_CONV])
    grad_x = dx2

    g_ada_w, g_ada_b, ada_token = _ada_bwd("ada", bl, c_all, ada_w.shape[2], jnp.stack(dmods))
    small_names_r = SMALL + CONVW + ("final_g",)
    summed, small_token = _allreduce_small("small", [jnp.stack([small_grads[l][n] for l in range(nl)]) for n in SMALL + CONVW] + [dfg])
    travelling = start_reduce(0, [("w_in", dw_in)], ada_token + small_token)
    grad = {"ada_w": g_ada_w, "ada_b": g_ada_b}
    grad.update(zip(small_names_r, summed))
    chip = 2 * lax.axis_index("x") + lax.axis_index("y")
    for n in CONVW:
        width = weights[n].shape[-1]
        grad[n] = lax.dynamic_slice_in_dim(grad[n], chip * width, width, axis=2)

    delta, new_m, new_v = {}, {}, {}
    last = (0, ("w_in",))
    early = [k for k in reducing if k != last]
    finished, _ = _reduce_weights_finish("wg_early", [reducing[k] for k in early], _after(summed[0], travelling))
    big_grads = dict(zip([(l, n) for l, names in early for n in names], finished))
    order = ("ada_w",) + tuple(reversed(BIG))
    for n in order:
        if n == "w_in":
            (big_grads[0, n],), _ = _reduce_weights_finish("wg_last", [reducing[last]], delta[order[-2]])
        if n != "ada_w":
            grad[n] = jnp.stack([big_grads[l, n] for l in range(nl)])
        delta[n], new_m[n], new_v[n] = _adamw("adam_" + n, weights[n], grad[n], given["m_" + n], given["v_" + n])
    small_names = ("ada_b",) + SMALL + CONVW + ("final_g",)
    as2d = lambda t: t.reshape(1, -1) if t.ndim == 1 else t
    res = _adamw_small(
        "adam_small", [as2d(weights[n]) for n in small_names], [as2d(grad[n]) for n in small_names],
        [as2d(given["m_" + n]) for n in small_names], [as2d(given["v_" + n]) for n in small_names])
    for out, vals in zip((delta, new_m, new_v), res):
        for n, val in zip(small_names, vals):
            out[n] = val.reshape(weights[n].shape)
    grad["final_g"] = grad["final_g"].reshape(final_g.shape)

    return (loss, grad_x, *[grad[n] for n in WEIGHTS], *[delta[n] for n in WEIGHTS], *[new_m[n] for n in WEIGHTS], *[new_v[n] for n in WEIGHTS])
```
